```python
import math
import jax, jax.numpy as jnp
from jax import lax
import numpy as np

D_MODEL = 1024
BATCH = 8
SEQ = 2048
DEPTH = 1
DEC_BATCH = 32
DEC_SEQ = 4
PAST_LEN = 8192
PAGE_SIZE = 128

HEAD_DIM = 64
POOL_WINDOWS = (2, 4, 8, 16)
POOL_GROUP = D_MODEL // 16
POOL_WIDTH = POOL_GROUP * len(POOL_WINDOWS)
POOL_BUF = max(POOL_WINDOWS) - 1
NSA_HEADS = D_MODEL // 128
NSA_KV_HEADS = 2
NSA_HPG = NSA_HEADS // NSA_KV_HEADS
NSA_WIDTH = NSA_HEADS * HEAD_DIM
KV_WIDTH = 2 * NSA_KV_HEADS * HEAD_DIM
CMP_BLOCK = 32
CMP_STRIDE = 16
SLC_BLOCK = 64
SLC_TOPK = 16
WINDOW = 512
SEL_Q_BLOCK = 32
WIN_Q_BLOCK = 128
MEM_TOKENS = 256
MEM_HEADS = 4
MEM_WIDTH = MEM_HEADS * HEAD_DIM
NUM_BUCKETS = 32
MAX_DISTANCE = 128
N_EXPERTS = 32
TOP_K = 4
D_FF = D_MODEL
SWIGLU_ALPHA = 1.702
SWIGLU_LIMIT = 7.0
MOE_BLOCK = 64
N_BRANCH = 3
EPS = 1e-6
IN_SPLITS = (POOL_WIDTH, NSA_WIDTH, MEM_WIDTH, KV_WIDTH, KV_WIDTH, KV_WIDTH, 3 * NSA_HEADS, N_BRANCH * D_MODEL)
D_IN = sum(IN_SPLITS)
SCALE = HEAD_DIM ** -0.5

kernel_name = "hybrid_pool_nsa_memxattn_moe_step"


def rmsnorm(x, g):
    xf = x.astype(jnp.float32)
    r = lax.rsqrt(jnp.mean(xf * xf, axis=-1, keepdims=True) + EPS)
    return (xf * r).astype(x.dtype) * g


def masked_softmax(s, mask, axes=-1):
    s = jnp.where(mask, s.astype(jnp.float32), -jnp.inf)
    m = jnp.max(s, axis=axes, keepdims=True)
    m = jnp.where(jnp.isfinite(m), m, 0.0)
    e = jnp.where(mask, jnp.exp(s - m), 0.0)
    return e / jnp.maximum(jnp.sum(e, axis=axes, keepdims=True), 1e-30)


def rel_bucket(dist):
    n = jnp.maximum(dist, 0)
    max_exact = NUM_BUCKETS // 2
    nf = jnp.maximum(n, 1).astype(jnp.float32)
    large = max_exact + (jnp.log(nf / max_exact) / math.log(MAX_DISTANCE / max_exact)
                         * (NUM_BUCKETS - max_exact)).astype(jnp.int32)
    large = jnp.minimum(large, NUM_BUCKETS - 1)
    return jnp.where(n < max_exact, n, large)


def split_cols(z):
    outs, o = [], 0
    for n in IN_SPLITS:
        outs.append(z[..., o:o + n])
        o += n
    return outs


def norm_keys(kv, g):
    return jnp.stack([rmsnorm(kv[:, :, 0], g), kv[:, :, 1]], axis=2)


def project_in(x, p):
    b, t = x.shape[:2]
    u, q, qm, kvc, kvs, kvw, gn, gb = split_cols(rmsnorm(x, p["norm1_g"]) @ p["w_in"])
    q = rmsnorm(q.reshape(b, t, NSA_KV_HEADS, NSA_HPG, HEAD_DIM), p["nsa_qk_norm"][0])
    qm = rmsnorm(qm.reshape(b, t, MEM_HEADS, HEAD_DIM), p["mem_qk_norm"][0])
    kvc = kvc.reshape(b, t, 2, NSA_KV_HEADS, HEAD_DIM)
    kvs = norm_keys(kvs.reshape(b, t, 2, NSA_KV_HEADS, HEAD_DIM), p["nsa_qk_norm"][2])
    kvw = norm_keys(kvw.reshape(b, t, 2, NSA_KV_HEADS, HEAD_DIM), p["nsa_qk_norm"][3])
    gn = jax.nn.sigmoid(gn.astype(jnp.float32)).astype(x.dtype).reshape(b, t, 3, NSA_KV_HEADS, NSA_HPG)
    gb = jax.nn.sigmoid(gb.astype(jnp.float32)).astype(x.dtype).reshape(b, t, N_BRANCH, D_MODEL)
    return u, q, qm, kvc, kvs, kvw, gn, gb


def pool_mix(u, buf, pos0, p):
    b, t = u.shape[:2]
    z = jnp.concatenate([buf, u], axis=1).astype(jnp.float32)
    cs = jnp.concatenate([jnp.zeros((b, 1, POOL_WIDTH), jnp.float32), lax.cumsum(z, axis=1)], axis=1)
    pos = pos0 + jnp.arange(t)
    outs = []
    for gi, w in enumerate(POOL_WINDOWS):
        ch = slice(gi * POOL_GROUP, (gi + 1) * POOL_GROUP)
        hi = cs[:, POOL_BUF + 1:POOL_BUF + 1 + t, ch]
        lo = cs[:, POOL_BUF + 1 - w:POOL_BUF + 1 - w + t, ch]
        cnt = jnp.minimum(pos + 1, w).astype(jnp.float32)[None, :, None]
        d = ((hi - lo) / cnt).astype(u.dtype) - u[:, :, ch]
        outs.append(d @ p["pool_w"][gi])
    return jnp.concatenate(outs, axis=-1) * p["pool_scale"]


def compress_blocks(kv, cmp_w, cmp_pe):
    b, T = kv.shape[:2]
    n_cmp = (T - CMP_BLOCK) // CMP_STRIDE + 1
    r_sub = CMP_BLOCK // CMP_STRIDE
    n_sub = n_cmp + r_sub - 1
    sub = kv[:, :n_sub * CMP_STRIDE].reshape(b, n_sub, CMP_STRIDE, 2, NSA_KV_HEADS, HEAD_DIM)
    out = None
    for r in range(r_sub):
        sl = slice(r * CMP_STRIDE, (r + 1) * CMP_STRIDE)
        pe_r = jnp.transpose(cmp_pe[:, sl], (1, 0, 2))[:, :, None, :]
        part = jnp.einsum("bnjcgd,cjde->bncge", sub + pe_r, cmp_w[:, sl])[:, r:r + n_cmp]
        out = part if out is None else out + part
    end_pos = jnp.arange(n_cmp) * CMP_STRIDE + (CMP_BLOCK - 1)
    return out, end_pos


def cmp_attention(q, qpos, kc, vc, end_pos, rel_bias):
    t, n = q.shape[1], kc.shape[1]
    s = jnp.einsum("btgpd,bngd->btgpn", q, kc).astype(jnp.float32) * SCALE
    bias = rel_bias[rel_bucket(qpos[:, None] - end_pos[None, :])]
    bias = bias.reshape(t, n, NSA_KV_HEADS, NSA_HPG).transpose(0, 2, 3, 1)
    mask = (end_pos[None, :] <= qpos[:, None])[:, None, None, :]
    pr = masked_softmax(s + bias, mask)
    o = jnp.einsum("btgpn,bngd->btgpd", pr.astype(vc.dtype), vc)
    return o, pr


def select_blocks(p_cmp, qpos, end_pos, n_slc):
    start = end_pos - CMP_BLOCK + 1
    js = jnp.arange(n_slc) * SLC_BLOCK
    covers = (start[:, None] < js[None, :] + SLC_BLOCK) & (end_pos[:, None] >= js[None, :])
    imp = jnp.einsum("btgpn,nj->btgj", p_cmp, covers.astype(jnp.float32))
    qblk = qpos // SLC_BLOCK
    jj = jnp.arange(n_slc)[None, :]
    causal = jj <= qblk[:, None]
    forced = (jj == 0) | (jj == qblk[:, None]) | (jj == qblk[:, None] - 1)
    score = jnp.where(forced[None, :, None, :], jnp.inf, imp)
    score = jnp.where(causal[None, :, None, :], score, -jnp.inf)
    top, idx = lax.top_k(score, min(SLC_TOPK, n_slc))
    return idx, top > -jnp.inf


def slc_attention(q, qpos, kv_blocks, idx, valid, rel_bias):
    b = q.shape[0]
    bi = jnp.arange(b)[:, None, None, None]
    gi = jnp.arange(NSA_KV_HEADS)[None, None, :, None]
    kv = kv_blocks[bi, gi, idx]
    s = jnp.einsum("bcgpd,bcgkld->bcgpkl", q, kv[..., 0, :]).astype(jnp.float32) * SCALE
    kpos = idx[..., None] * SLC_BLOCK + jnp.arange(SLC_BLOCK)
    dist = qpos[None, :, None, None, None] - kpos
    rel_g = rel_bias.reshape(NUM_BUCKETS, NSA_KV_HEADS, NSA_HPG)
    bias = rel_g[rel_bucket(dist), gi[..., None]]
    bias = jnp.moveaxis(bias, -1, 3)
    mask = (valid[..., None] & (dist >= 0))[:, :, :, None]
    pr = masked_softmax(s + bias, mask, axes=(-2, -1))
    return jnp.einsum("bcgpkl,bcgkld->bcgpd", pr.astype(kv.dtype), kv[..., 1, :])


def nsa_global(q, qpos, kv_cmp, kv_slc, p, rel_bias):
    b, t = q.shape[:2]
    blk, end_pos = compress_blocks(kv_cmp, p["cmp_w"], p["cmp_pe"])
    kc = rmsnorm(blk[:, :, 0], p["nsa_qk_norm"][1])
    o_cmp, p_cmp = cmp_attention(q, qpos, kc, blk[:, :, 1], end_pos, rel_bias)
    T = kv_slc.shape[1]
    n_slc = -(-T // SLC_BLOCK)
    idx, valid = select_blocks(p_cmp, qpos, end_pos, n_slc)
    kvs = jnp.pad(kv_slc, ((0, 0), (0, n_slc * SLC_BLOCK - T), (0, 0), (0, 0), (0, 0)))
    kvs = kvs.reshape(b, n_slc, SLC_BLOCK, 2, NSA_KV_HEADS, HEAD_DIM).transpose(0, 4, 1, 2, 3, 5)
    c = math.gcd(SEL_Q_BLOCK, t)
    nc = t // c

    def chunks(a):
        return jnp.moveaxis(a.reshape((b, nc, c) + a.shape[2:]), 1, 0)

    xs = (chunks(q), qpos.reshape(nc, c), chunks(idx), chunks(valid))
    o = lax.map(lambda a: slc_attention(a[0], a[1], kvs, a[2], a[3], rel_bias), xs)
    o_slc = jnp.moveaxis(o, 0, 1).reshape(q.shape)
    return o_cmp, o_slc


def win_attention(q, qpos, kv, kpos, rel_bias):
    nb, c = qpos.shape
    k = kpos.shape[1]
    s = jnp.einsum("bncgpd,bnkgd->bngpck", q, kv[:, :, :, 0]).astype(jnp.float32) * SCALE
    dist = qpos[:, :, None] - kpos[:, None, :]
    mask = (dist >= 0) & (dist < WINDOW) & (kpos[:, None, :] >= 0)
    bias = rel_bias[rel_bucket(dist)].reshape(nb, c, k, NSA_KV_HEADS, NSA_HPG).transpose(0, 3, 4, 1, 2)
    pr = masked_softmax(s + bias[None], mask[None, :, None, None])
    return jnp.einsum("bngpck,bnkgd->bncgpd", pr.astype(kv.dtype), kv[:, :, :, 1])


def nsa_combine(gn, o_cmp, o_slc, o_win):
    return (gn[:, :, 0, :, :, None] * o_cmp + gn[:, :, 1, :, :, None] * o_slc
            + gn[:, :, 2, :, :, None] * o_win)


def mem_kv_proj(mem, p):
    b, m = mem.shape[:2]
    kv = (rmsnorm(mem, p["mem_norm_g"]) @ p["w_mem_kv"]).reshape(b, m, 2, MEM_HEADS, HEAD_DIM)
    return norm_keys(kv, p["mem_qk_norm"][1])


def mem_attention(qm, mem_kv):
    s = jnp.einsum("bthd,bmhd->bhtm", qm, mem_kv[:, :, 0]).astype(jnp.float32) * SCALE
    pr = jax.nn.softmax(s, axis=-1)
    return jnp.einsum("bhtm,bmhd->bthd", pr.astype(mem_kv.dtype), mem_kv[:, :, 1])


def moe(h, p):
    b, t, d = h.shape
    hf = h.reshape(-1, d)
    n = hf.shape[0]
    logits = (hf @ p["router_w"] + p["router_b"]).astype(jnp.float32)
    top, eidx = lax.top_k(logits, TOP_K)
    gates = jax.nn.softmax(top, axis=-1)
    a = n * TOP_K
    flat_e = eidx.reshape(-1)
    order = jnp.argsort(flat_e)
    sorted_e = flat_e[order]
    counts = jnp.bincount(flat_e, length=N_EXPERTS)
    starts = jnp.cumsum(counts) - counts
    padded = (counts + MOE_BLOCK - 1) // MOE_BLOCK * MOE_BLOCK
    pad_ends = jnp.cumsum(padded)
    pad_starts = pad_ends - padded
    dest = pad_starts[sorted_e] + jnp.arange(a) - starts[sorted_e]
    n_blocks = -(-a // MOE_BLOCK) + N_EXPERTS
    rows = jnp.zeros((n_blocks * MOE_BLOCK, d), h.dtype).at[dest].set(hf[order // TOP_K])
    blk_e = jnp.minimum(jnp.searchsorted(pad_ends, jnp.arange(n_blocks) * MOE_BLOCK, side="right"),
                        N_EXPERTS - 1)

    def expert_block(args):
        xb, e = args
        gu = xb @ p["w_gu"][e] + p["b_gu"][e]
        gate = jnp.minimum(gu[:, :D_FF], SWIGLU_LIMIT)
        up = jnp.clip(gu[:, D_FF:], -SWIGLU_LIMIT, SWIGLU_LIMIT)
        act = gate * jax.nn.sigmoid(SWIGLU_ALPHA * gate) * (up + 1)
        return act @ p["w_down"][e] + p["b_down"][e]

    out_rows = lax.map(expert_block, (rows.reshape(n_blocks, MOE_BLOCK, d), blk_e)).reshape(-1, d)
    y = jnp.zeros((a, d), h.dtype).at[order].set(out_rows[dest]).reshape(n, TOP_K, d)
    return jnp.einsum("nk,nkd->nd", gates.astype(h.dtype), y).reshape(b, t, d)


def layer_tail(x, gb, o_pool, o_nsa, o_mem, p):
    b, t = x.shape[:2]
    mixed = (gb[:, :, 0] * (o_pool @ p["w_up_pool"])
             + gb[:, :, 1] * (o_nsa.reshape(b, t, NSA_WIDTH) @ p["w_up_nsa"])
             + gb[:, :, 2] * (o_mem.reshape(b, t, MEM_WIDTH) @ p["w_up_mem"]))
    x = x + mixed @ p["w_out"]
    return x + moe(rmsnorm(x, p["norm2_g"]), p)


def last_rows(a, n):
    t = a.shape[1]
    if t < n:
        a = jnp.pad(a, [(0, 0), (n - t, 0)] + [(0, 0)] * (a.ndim - 2))
    return a[:, a.shape[1] - n:]


def prompt_layer(x, mem, p, rel_bias):
    b, s = x.shape[:2]
    u, q, qm, kvc, kvs, kvw, gn, gb = project_in(x, p)
    pos = jnp.arange(s)
    o_pool = pool_mix(u, jnp.zeros((b, POOL_BUF, POOL_WIDTH), u.dtype), 0, p)
    o_cmp, o_slc = nsa_global(q, pos, kvc, kvs, p, rel_bias)
    nb = s // WIN_Q_BLOCK
    n_prev = -(-(WINDOW - 1) // WIN_Q_BLOCK)
    padded = jnp.pad(kvw, ((0, 0), (n_prev * WIN_Q_BLOCK, 0), (0, 0), (0, 0), (0, 0)))
    padded = padded.reshape((b, nb + n_prev, WIN_Q_BLOCK) + kvw.shape[2:])
    span = jnp.concatenate([padded[:, i:i + nb] for i in range(n_prev + 1)], axis=2)
    kpos = (jnp.arange(nb)[:, None] - n_prev) * WIN_Q_BLOCK + jnp.arange((n_prev + 1) * WIN_Q_BLOCK)[None, :]
    qb = q.reshape((b, nb, WIN_Q_BLOCK) + q.shape[2:])
    o_win = win_attention(qb, pos.reshape(nb, WIN_Q_BLOCK), span, kpos, rel_bias).reshape(q.shape)
    o_nsa = nsa_combine(gn, o_cmp, o_slc, o_win)
    mem_kv = mem_kv_proj(mem, p)
    o_mem = mem_attention(qm, mem_kv)
    y = layer_tail(x, gb, o_pool, o_nsa, o_mem, p)
    win_buf = min(WINDOW, PAST_LEN)
    return y, (kvc, kvs, last_rows(kvw, win_buf), mem_kv, last_rows(u, POOL_BUF))


def sample_layer(x, cache_cmp, cache_slc, cache_win, cache_mem, pool_buf, page_table, p, rel_bias):
    b, t = x.shape[:2]
    past = page_table.shape[1] * cache_cmp.shape[1]
    u, q, qm, kvc, kvs, kvw, gn, gb = project_in(x, p)
    pos = past + jnp.arange(t)

    def gather(pool):
        return pool[page_table].reshape((b, past) + pool.shape[2:])

    full_c = jnp.concatenate([gather(cache_cmp), kvc], axis=1)
    full_s = jnp.concatenate([gather(cache_slc), kvs], axis=1)
    o_pool = pool_mix(u, pool_buf, past, p)
    o_cmp, o_slc = nsa_global(q, pos, full_c, full_s, p, rel_bias)
    wb = cache_win.shape[1]
    win_kv = jnp.concatenate([cache_win, kvw], axis=1)
    kpos = (past - wb + jnp.arange(wb + t))[None, :]
    o_win = win_attention(q[:, None], pos[None, :], win_kv[:, None], kpos, rel_bias)[:, 0]
    o_nsa = nsa_combine(gn, o_cmp, o_slc, o_win)
    o_mem = mem_attention(qm, cache_mem)
    y = layer_tail(x, gb, o_pool, o_nsa, o_mem, p)
    new_pool = jnp.concatenate([pool_buf, u], axis=1)[:, t:]
    return y, (kvc, kvs, win_kv[:, t:], new_pool)


def setup_inputs(seed: int = 0) -> dict:
    key = jax.random.key(seed)
    ks = iter(jax.random.split(key, 40))

    def nrm(shape, scale):
        return jax.random.normal(next(ks), shape, jnp.float32) * scale

    n_pages = PAST_LEN // PAGE_SIZE
    n_used = DEC_BATCH * n_pages
    n_phys = n_used + max(1, n_used // 4)
    page_table = jax.random.permutation(next(ks), n_phys)[:n_used].astype(jnp.int32).reshape(DEC_BATCH, n_pages)
    win_buf = min(WINDOW, PAST_LEN)
    kv_tail = (2, NSA_KV_HEADS, HEAD_DIM)
    L = DEPTH
    return {
        "x_prompt": nrm((BATCH, SEQ, D_MODEL), 1.0),
        "x_sample": nrm((DEC_BATCH, DEC_SEQ, D_MODEL), 1.0),
        "cache_cmp_kv": nrm((L, n_phys, PAGE_SIZE) + kv_tail, 1.0),
        "cache_slc_kv": nrm((L, n_phys, PAGE_SIZE) + kv_tail, 1.0),
        "cache_win_kv": nrm((L, DEC_BATCH, win_buf) + kv_tail, 1.0),
        "cache_mem_kv": nrm((L, DEC_BATCH, MEM_TOKENS, 2, MEM_HEADS, HEAD_DIM), 1.0),
        "state_pool": nrm((L, DEC_BATCH, POOL_BUF, POOL_WIDTH), 1.0),
        "page_table": page_table,
        "mem_prompt": nrm((BATCH, MEM_TOKENS, D_MODEL), 1.0),
        "rel_bias": nrm((NUM_BUCKETS, NSA_HEADS), 0.5),
        "norm1_g": 1.0 + nrm((L, D_MODEL), 0.02),
        "w_in": nrm((L, D_MODEL, D_IN), D_MODEL ** -0.5),
        "nsa_qk_norm": 1.0 + nrm((L, 4, HEAD_DIM), 0.02),
        "mem_qk_norm": 1.0 + nrm((L, 2, HEAD_DIM), 0.02),
        "cmp_w": nrm((L, 2, CMP_BLOCK, HEAD_DIM, HEAD_DIM), (CMP_BLOCK * HEAD_DIM) ** -0.5),
        "cmp_pe": nrm((L, 2, CMP_BLOCK, HEAD_DIM), 0.1),
        "pool_w": nrm((L, len(POOL_WINDOWS), POOL_GROUP, POOL_GROUP), POOL_GROUP ** -0.5),
        "pool_scale": 1.0 + nrm((L, POOL_WIDTH), 0.1),
        "mem_norm_g": 1.0 + nrm((L, D_MODEL), 0.02),
        "w_mem_kv": nrm((L, D_MODEL, 2 * MEM_WIDTH), D_MODEL ** -0.5),
        "w_up_pool": nrm((L, POOL_WIDTH, D_MODEL), POOL_WIDTH ** -0.5),
        "w_up_nsa": nrm((L, NSA_WIDTH, D_MODEL), NSA_WIDTH ** -0.5),
        "w_up_mem": nrm((L, MEM_WIDTH, D_MODEL), MEM_WIDTH ** -0.5),
        "w_out": nrm((L, D_MODEL, D_MODEL), D_MODEL ** -0.5),
        "norm2_g": 1.0 + nrm((L, D_MODEL), 0.02),
        "router_w": nrm((L, D_MODEL, N_EXPERTS), D_MODEL ** -0.5),
        "router_b": nrm((L, N_EXPERTS), 0.01),
        "w_gu": nrm((L, N_EXPERTS, D_MODEL, 2 * D_FF), D_MODEL ** -0.5),
        "b_gu": nrm((L, N_EXPERTS, 2 * D_FF), 0.01),
        "w_down": nrm((L, N_EXPERTS, D_FF, D_MODEL), D_FF ** -0.5),
        "b_down": nrm((L, N_EXPERTS, D_MODEL), 0.01),
    }


def reference(x_prompt, x_sample, cache_cmp_kv, cache_slc_kv, cache_win_kv, cache_mem_kv, state_pool,
              page_table, mem_prompt, rel_bias, norm1_g, w_in, nsa_qk_norm, mem_qk_norm, cmp_w, cmp_pe,
              pool_w, pool_scale, mem_norm_g, w_mem_kv, w_up_pool, w_up_nsa, w_up_mem, w_out, norm2_g,
              router_w, router_b, w_gu, b_gu, w_down, b_down):
    yp, ys = x_prompt, x_sample
    outs_p = [[] for _ in range(5)]
    outs_s = [[] for _ in range(4)]
    for l in range(DEPTH):
        p = {"norm1_g": norm1_g[l], "w_in": w_in[l], "nsa_qk_norm": nsa_qk_norm[l],
             "mem_qk_norm": mem_qk_norm[l], "cmp_w": cmp_w[l], "cmp_pe": cmp_pe[l],
             "pool_w": pool_w[l], "pool_scale": pool_scale[l], "mem_norm_g": mem_norm_g[l],
             "w_mem_kv": w_mem_kv[l], "w_up_pool": w_up_pool[l], "w_up_nsa": w_up_nsa[l],
             "w_up_mem": w_up_mem[l], "w_out": w_out[l], "norm2_g": norm2_g[l],
             "router_w": router_w[l], "router_b": router_b[l], "w_gu": w_gu[l], "b_gu": b_gu[l],
             "w_down": w_down[l], "b_down": b_down[l]}
        yp, st_p = prompt_layer(yp, mem_prompt, p, rel_bias)
        ys, st_s = sample_layer(ys, cache_cmp_kv[l], cache_slc_kv[l], cache_win_kv[l], cache_mem_kv[l],
                                state_pool[l], page_table, p, rel_bias)
        for lst, a in zip(outs_p, st_p):
            lst.append(a)
        for lst, a in zip(outs_s, st_s):
            lst.append(a)
    new_cmp_p, new_slc_p, new_win_p, new_mem_p, new_pool_p = [jnp.stack(a) for a in outs_p]
    new_cmp_s, new_slc_s, new_win_s, new_pool_s = [jnp.stack(a) for a in outs_s]
    return (yp, ys, new_cmp_p, new_slc_p, new_win_p, new_mem_p, new_pool_p,
            new_cmp_s, new_slc_s, new_win_s, new_pool_s)
```

```python
import functools
import math

import jax
import jax.numpy as jnp
from jax import lax
from jax.experimental import pallas as pl
from jax.experimental.pallas import tpu as pltpu

F32 = jnp.float32
BF16 = jnp.bfloat16

D_MODEL = 1024
HEAD_DIM = 64
POOL_WINDOWS = (2, 4, 8, 16)
POOL_GROUP = 64
POOL_WIDTH = 256
POOL_BUF = 15
NSA_HEADS = 8
NSA_KV_HEADS = 2
NSA_HPG = 4
NSA_WIDTH = 512
KV_WIDTH = 256
CMP_BLOCK = 32
CMP_STRIDE = 16
SLC_BLOCK = 64
SLC_TOPK = 16
WINDOW = 512
MEM_HEADS = 4
MEM_WIDTH = 256
NUM_BUCKETS = 32
MAX_DISTANCE = 128
N_EXPERTS = 32
TOP_K = 4
D_FF = 1024
SWIGLU_ALPHA = 1.702
SWIGLU_LIMIT = 7.0
EPS = 1e-6
SCALE = HEAD_DIM ** -0.5

LANES = 128
MXU_DIM = 256
NEG = -1e30
ATT_TILE = 128
MOE_TILE = 256
VMEM_LIMIT = 48 * 1024 * 1024


def _cparams(*sem):
    return pltpu.CompilerParams(dimension_semantics=sem, vmem_limit_bytes=VMEM_LIMIT)


def _dot(a, b):
    return jnp.dot(a, b, preferred_element_type=F32)


def _dot_nt(a, b):
    return lax.dot_general(a, b, (((1,), (1,)), ((), ())), preferred_element_type=F32)


def _split_dot(a, b):
    hi = a.astype(BF16)
    lo = (a - hi.astype(F32)).astype(BF16)
    return _dot(hi, b) + _dot(lo, b)


def _rms(x, g):
    r = lax.rsqrt(jnp.mean(x * x, axis=-1, keepdims=True) + EPS)
    return (x * r) * g


def _sigmoid(x):
    return 1.0 / (1.0 + jnp.exp(-x))


def _lane_iota(shape):
    return lax.broadcasted_iota(jnp.int32, shape, len(shape) - 1)


def _row_iota(shape):
    return lax.broadcasted_iota(jnp.int32, shape, len(shape) - 2)


def _proj_kernel(x_ref, g_ref, w_ref, gain_ref, nmask_ref, seg_ref, *out_refs, segs, n_norm):
    h = _rms(x_ref[...], g_ref[...]).astype(BF16)
    seg = seg_ref[...]
    for (start, width, kind), o_ref in zip(segs, out_refs):
        z = _dot(h, w_ref[:, start:start + width])
        if start < n_norm:
            pieces = []
            for c in range(0, width, MXU_DIM):
                zc = z[:, c:c + MXU_DIM]
                ms = _split_dot(zc * zc, seg)
                zn = (zc * lax.rsqrt(ms + EPS)) * gain_ref[:, start + c:start + c + MXU_DIM]
                pieces.append(jnp.where(nmask_ref[:, start + c:start + c + MXU_DIM] > 0, zn, zc))
            z = pieces[0] if len(pieces) == 1 else jnp.concatenate(pieces, axis=1)
        if kind == "sigmoid":
            z = _sigmoid(z)
        elif kind == "qscale":
            z = z * SCALE
        o_ref[...] = z.astype(o_ref.dtype)


def _project(x, g, w, gain, nmask, segs, out_dtypes, n_norm, tm):
    n = x.shape[0]
    ncol = w.shape[1]
    seg = _seg_matrix(MXU_DIM)
    full = lambda i: (0, 0)
    return pl.pallas_call(
        functools.partial(_proj_kernel, segs=segs, n_norm=n_norm),
        grid=(n // tm,),
        in_specs=[
            pl.BlockSpec((tm, D_MODEL), lambda i: (i, 0)),
            pl.BlockSpec((1, D_MODEL), full),
            pl.BlockSpec((D_MODEL, ncol), full),
            pl.BlockSpec((1, gain.shape[1]), full),
            pl.BlockSpec((1, nmask.shape[1]), full),
            pl.BlockSpec((MXU_DIM, MXU_DIM), full),
        ],
        out_specs=[pl.BlockSpec((tm, wd), lambda i: (i, 0)) for (_, wd, _) in segs],
        out_shape=[jax.ShapeDtypeStruct((n, wd), dt) for (_, wd, _), dt in zip(segs, out_dtypes)],
        compiler_params=_cparams("parallel"),
        name="proj",
    )(x, g, w, gain, nmask, seg)


def _seg_matrix(n):
    i = jnp.arange(n) // HEAD_DIM
    return jnp.where(i[:, None] == i[None, :], 1.0 / HEAD_DIM, 0.0).astype(BF16)


def _pool_kernel(u_ref, buf_ref, w_ref, scale_ref, o_ref, zs_ref, *, t, pos0):
    zs_ref[0:16, :] = buf_ref[...]
    zs_ref[16:16 + t, :] = u_ref[...]
    u = u_ref[...]
    lane = _lane_iota((1, POOL_WIDTH))
    pos = (pos0 + _row_iota((t, 1))).astype(F32)
    acc = u
    mean = None
    for i in range(1, max(POOL_WINDOWS)):
        acc = acc + zs_ref[16 - i:16 - i + t, :]
        if i + 1 in POOL_WINDOWS:
            gi = POOL_WINDOWS.index(i + 1)
            m = acc / jnp.minimum(pos + 1.0, float(i + 1))
            mean = m if mean is None else jnp.where(lane >= gi * POOL_GROUP, m, mean)
    d = (mean - u).astype(BF16)
    o_ref[...] = _dot(d, w_ref[...]) * scale_ref[...]


def _pool_mix(u, buf16, w_bd, scale, pos0):
    b, t, _ = u.shape
    return pl.pallas_call(
        functools.partial(_pool_kernel, t=t, pos0=pos0),
        grid=(b,),
        in_specs=[
            pl.BlockSpec((None, t, POOL_WIDTH), lambda i: (i, 0, 0)),
            pl.BlockSpec((None, 16, POOL_WIDTH), lambda i: (i, 0, 0)),
            pl.BlockSpec((POOL_WIDTH, POOL_WIDTH), lambda i: (0, 0)),
            pl.BlockSpec((1, POOL_WIDTH), lambda i: (0, 0)),
        ],
        out_specs=pl.BlockSpec((None, t, POOL_WIDTH), lambda i: (i, 0, 0)),
        out_shape=jax.ShapeDtypeStruct((b, t, POOL_WIDTH), F32),
        scratch_shapes=[pltpu.VMEM((t + 16, POOL_WIDTH), F32)],
        compiler_params=_cparams("parallel"),
        name="pool",
    )(u, buf16, w_bd, scale)


def _cpart_kernel(*refs, nop, rows_per):
    x_refs = refs[-(2 * nop + 3):-3]
    pe_ref, w_ref, o_ref = refs[-3:]
    n = rows_per // CMP_STRIDE
    m = nop * n
    for c in range(2):
        acc = jnp.zeros((m + 8, KV_WIDTH), F32)
        for j in range(CMP_STRIDE):
            parts = [r[pl.ds(j, n, stride=CMP_STRIDE), :] for r in x_refs[c * nop:(c + 1) * nop]]
            lhs = jnp.concatenate(parts + [pe_ref[c, j]], axis=0).astype(BF16)
            acc = acc + _dot(lhs, w_ref[c, j])
        lane = _lane_iota((1, KV_WIDTH))
        pe_term = jnp.where(lane < LANES, acc[m:m + 1], acc[m + 1:m + 2])
        o_ref[:, c * KV_WIDTH:(c + 1) * KV_WIDTH] = acc[0:m] + pe_term


def _cmp_partials_dense(kv, pe, w_c):
    b, t, _ = kv.shape
    rows = (t // CMP_STRIDE) * CMP_STRIDE
    n = rows // CMP_STRIDE
    return pl.pallas_call(
        functools.partial(_cpart_kernel, nop=1, rows_per=rows),
        grid=(b,),
        in_specs=[
            pl.BlockSpec((None, rows, LANES), lambda i: (i, 0, 0)),
            pl.BlockSpec((None, rows, LANES), lambda i: (i, 0, 1)),
            pl.BlockSpec(pe.shape, lambda i: (0, 0, 0, 0)),
            pl.BlockSpec(w_c.shape, lambda i: (0, 0, 0, 0)),
        ],
        out_specs=pl.BlockSpec((None, n, 2 * KV_WIDTH), lambda i: (i, 0, 0)),
        out_shape=jax.ShapeDtypeStruct((b, n, 2 * KV_WIDTH), F32),
        compiler_params=_cparams("parallel"),
        name="cmp_partials",
    )(kv, kv, pe, w_c)


def _cmp_partials_paged(pool, page_table, pe, w_c, pages_per_step):
    b, n_pages = page_table.shape
    page = pool.shape[1]
    nop = pages_per_step
    n = nop * page // CMP_STRIDE

    def page_spec(k, half):
        return pl.BlockSpec((None, page, LANES), lambda i, c, pt: (pt[i, c * nop + k], 0, half))

    grid_spec = pltpu.PrefetchScalarGridSpec(
        num_scalar_prefetch=1,
        grid=(b, n_pages // nop),
        in_specs=[page_spec(k, half) for half in range(2) for k in range(nop)] + [
            pl.BlockSpec(pe.shape, lambda i, c, pt: (0, 0, 0, 0)),
            pl.BlockSpec(w_c.shape, lambda i, c, pt: (0, 0, 0, 0)),
        ],
        out_specs=pl.BlockSpec((None, n, 2 * KV_WIDTH), lambda i, c, pt: (i, c, 0)),
    )
    return pl.pallas_call(
        functools.partial(_cpart_kernel, nop=nop, rows_per=page),
        grid_spec=grid_spec,
        out_shape=jax.ShapeDtypeStruct((b, n_pages * page // CMP_STRIDE, 2 * KV_WIDTH), F32),
        compiler_params=_cparams("parallel", "arbitrary"),
        name="cmp_partials_paged",
    )(page_table, *([pool] * (2 * nop)), pe, w_c)


def _pair_operands(x128, g):
    lane = _lane_iota(x128.shape)
    if g == 0:
        lo = jnp.where(lane < HEAD_DIM, x128, 0.0)
        hi = pltpu.roll(lo, HEAD_DIM, 1)
    else:
        hi = jnp.where(lane >= HEAD_DIM, x128, 0.0)
        lo = pltpu.roll(hi, HEAD_DIM, 1)
    return lo.astype(BF16), hi.astype(BF16)


def _cattn_kernel(q_ref, p_ref, bias_ref, gn_ref, gk_ref, seg_ref, o_ref, ns0_ref, ns1_ref, *,
                  tq, n_sub, n_cmp, n_slc, nslp, pos0):
    qi = pl.program_id(0)
    pall = p_ref[...]
    kraw = pall[:, 0:LANES] + pltpu.roll(pall[:, LANES:2 * LANES], n_sub - 1, 0)
    vc = pall[:, 2 * LANES:3 * LANES] + pltpu.roll(pall[:, 3 * LANES:4 * LANES], n_sub - 1, 0)
    ms = _split_dot(kraw * kraw, seg_ref[...])
    kc = (kraw * lax.rsqrt(ms + EPS)) * gk_ref[...]

    qpos = pos0 + qi * tq + _row_iota((tq, 1))
    qblk = jnp.right_shift(qpos, SLC_BLOCK.bit_length() - 1)
    nn = _row_iota((n_sub, nslp))
    jj = _lane_iota((n_sub, nslp))
    covers = ((nn * CMP_STRIDE < (jj + 1) * SLC_BLOCK) & (nn * CMP_STRIDE + CMP_BLOCK - 1 >= jj * SLC_BLOCK)
              & (nn < n_cmp) & (jj < n_slc))
    covers = jnp.where(covers, 1.0, 0.0).astype(BF16)
    jl = _lane_iota((tq, nslp))
    jlf = jl.astype(F32)
    forced = (jl == 0) | (jl == qblk) | (jl == qblk - 1)
    causal = jl <= qblk

    for g in range(NSA_KV_HEADS):
        k_lo, k_hi = _pair_operands(kc, g)
        v_lo, v_hi = _pair_operands(vc, g)
        prsum = jnp.zeros((tq, n_sub), F32)
        for pr in range(NSA_HPG // 2):
            qpair = q_ref[:, (2 * g + pr) * LANES:(2 * g + pr + 1) * LANES]
            outs = []
            for half, (kk, vv) in enumerate(((k_lo, v_lo), (k_hi, v_hi))):
                h = NSA_HPG * g + 2 * pr + half
                bias = bias_ref[h]
                mask = bias > 0.5 * NEG
                s = _dot_nt(qpair, kk) + bias
                m = jnp.max(s, axis=1, keepdims=True)
                m = jnp.where(m > 0.5 * NEG, m, 0.0)
                e = jnp.where(mask, jnp.exp(s - m), 0.0)
                p = e / jnp.maximum(jnp.sum(e, axis=1, keepdims=True), 1e-30)
                prsum = prsum + p
                outs.append(_dot(p.astype(BF16), vv) * gn_ref[:, h:h + 1])
            o_ref[:, (2 * g + pr) * LANES:(2 * g + pr + 1) * LANES] = outs[0] + outs[1]
        imp = _split_dot(prsum, covers)
        score = jnp.where(forced, jnp.inf, imp)
        score = jnp.where(causal, score, -jnp.inf)
        sel = jnp.zeros((tq, nslp), F32)
        for _ in range(min(SLC_TOPK, n_slc)):
            m = jnp.max(score, axis=1, keepdims=True)
            idx = jnp.min(jnp.where(score == m, jlf, 1e9), axis=1, keepdims=True)
            pick = jlf == idx
            sel = jnp.where(pick & (m > -jnp.inf), 1.0, sel)
            score = jnp.where(pick, -jnp.inf, score)
        (ns0_ref if g == 0 else ns1_ref)[...] = 1.0 - sel


def _cmp_attention(q, parts, bias, gn, gk, *, tq, n_cmp, n_slc, pos0):
    b, s, _ = q.shape
    n_sub = parts.shape[1]
    nslp = -(-n_slc // LANES) * LANES
    kern = functools.partial(_cattn_kernel, tq=tq, n_sub=n_sub, n_cmp=n_cmp, n_slc=n_slc, nslp=nslp, pos0=pos0)
    return pl.pallas_call(
        kern,
        grid=(s // tq, b),
        in_specs=[
            pl.BlockSpec((None, tq, NSA_WIDTH), lambda i, j: (j, i, 0)),
            pl.BlockSpec((None, n_sub, 2 * KV_WIDTH), lambda i, j: (j, 0, 0)),
            pl.BlockSpec((NSA_HEADS, tq, n_sub), lambda i, j: (0, i, 0)),
            pl.BlockSpec((None, tq, LANES), lambda i, j: (j, i, 0)),
            pl.BlockSpec((1, LANES), lambda i, j: (0, 0)),
            pl.BlockSpec((LANES, LANES), lambda i, j: (0, 0)),
        ],
        out_specs=[
            pl.BlockSpec((None, tq, NSA_WIDTH), lambda i, j: (j, i, 0)),
            pl.BlockSpec((None, tq, nslp), lambda i, j: (j, i, 0)),
            pl.BlockSpec((None, tq, nslp), lambda i, j: (j, i, 0)),
        ],
        out_shape=[
            jax.ShapeDtypeStruct((b, s, NSA_WIDTH), F32),
            jax.ShapeDtypeStruct((b, s, nslp), F32),
            jax.ShapeDtypeStruct((b, s, nslp), F32),
        ],
        compiler_params=_cparams("parallel", "parallel"),
        name="cmp_attention",
    )(q, parts, bias, gn, gk, _seg_matrix(LANES))


def _flash_kernel(q_ref, ns0_ref, ns1_ref, kv_ref, tab_ref, gn_ref, o_ref, acc_ref, ma_ref, la_ref, mb_ref,
                  lb_ref, *, t, use_sel, band, gate_base):
    qi = pl.program_id(1)
    lane = _lane_iota((2 * t, LANES))
    low = lane < HEAD_DIM
    row_k = _row_iota((t, LANES))
    lane_k = _lane_iota((t, LANES))
    lo_tile = jnp.maximum(qi - band, 0) if band is not None else 0

    for g in range(NSA_KV_HEADS):
        q2 = jnp.concatenate([q_ref[:, (2 * g) * LANES:(2 * g + 1) * LANES],
                              q_ref[:, (2 * g + 1) * LANES:(2 * g + 2) * LANES]], axis=0)
        if use_sel:
            ns = (ns0_ref if g == 0 else ns1_ref)[...].astype(BF16)
            q2 = jnp.concatenate([q2, jnp.concatenate([ns, ns], axis=0)], axis=1)
        acc_ref[...] = jnp.zeros_like(acc_ref)
        for r in (ma_ref, mb_ref):
            r[...] = jnp.full_like(r, NEG)
        for r in (la_ref, lb_ref):
            r[...] = jnp.zeros_like(r)

        def body(kj, carry, g=g, q2=q2):
            k0 = pl.multiple_of(kj * t, t)
            k128 = kv_ref[pl.ds(k0, t), 0:LANES]
            v128 = kv_ref[pl.ds(k0, t), LANES:2 * LANES]
            k_lo, k_hi = _pair_operands(k128, g)
            v_lo, v_hi = _pair_operands(v128, g)
            if use_sel:
                blk = 2 * kj + jnp.where(row_k >= SLC_BLOCK, 1, 0)
                onehot = jnp.where(lane_k == blk, -(2.0 ** 30), 0.0).astype(BF16)
                k_lo = jnp.concatenate([k_lo, onehot], axis=1)
                k_hi = jnp.concatenate([k_hi, onehot], axis=1)
            delta = qi - kj
            if band is None:
                kind = jnp.minimum(delta, 2)
            else:
                kind = jnp.where(delta < 2, delta, jnp.where(delta < band, 2, 3))
            ps = []
            alphas = []
            for half, (kk, m_ref, l_ref) in enumerate(((k_lo, ma_ref, la_ref), (k_hi, mb_ref, lb_ref))):
                s = _dot_nt(q2, kk) + tab_ref[g, kind, half]
                m_old = m_ref[...]
                m_new = jnp.maximum(m_old, jnp.max(s, axis=1, keepdims=True))
                alpha = jnp.exp(m_old - m_new)
                p = jnp.exp(s - m_new)
                l_ref[...] = alpha * l_ref[...] + jnp.sum(p, axis=1, keepdims=True)
                m_ref[...] = m_new
                ps.append(p.astype(BF16))
                alphas.append(alpha)
            pv = _dot(ps[0], v_lo) + _dot(ps[1], v_hi)
            acc_ref[...] = acc_ref[...] * jnp.where(low, alphas[0], alphas[1]) + pv
            return carry

        lax.fori_loop(lo_tile, qi + 1, body, 0)
        out = acc_ref[...] / jnp.where(low, la_ref[...], lb_ref[...])
        for pr in range(2):
            h0 = gate_base + NSA_HPG * g + 2 * pr
            gate = jnp.where(lane_k < HEAD_DIM, gn_ref[:, h0:h0 + 1], gn_ref[:, h0 + 1:h0 + 2])
            o_ref[:, (2 * g + pr) * LANES:(2 * g + pr + 1) * LANES] = out[pr * t:(pr + 1) * t] * gate


def _flash_attention(q, ns0, ns1, kv, tab, gn, *, use_sel, band, gate_base):
    b, s, _ = q.shape
    t = ATT_TILE
    kern = functools.partial(_flash_kernel, t=t, use_sel=use_sel, band=band, gate_base=gate_base)
    tile = lambda w: pl.BlockSpec((None, t, w), lambda i, j: (i, j, 0))
    return pl.pallas_call(
        kern,
        grid=(b, s // t),
        in_specs=[
            tile(NSA_WIDTH), tile(ns0.shape[2]), tile(ns1.shape[2]),
            pl.BlockSpec((None, s, KV_WIDTH), lambda i, j: (i, 0, 0)),
            pl.BlockSpec(tab.shape, lambda i, j: (0, 0, 0, 0, 0)),
            tile(LANES),
        ],
        out_specs=tile(NSA_WIDTH),
        out_shape=jax.ShapeDtypeStruct((b, s, NSA_WIDTH), F32),
        scratch_shapes=[pltpu.VMEM((2 * t, LANES), F32)] + [pltpu.VMEM((2 * t, 1), F32)] * 4,
        compiler_params=_cparams("parallel", "parallel"),
        name="flash_sel" if use_sel else "flash_win",
    )(q, ns0, ns1, kv, tab, gn)


def _memattn_kernel(q_ref, kv_ref, o_ref):
    lane = _lane_iota((kv_ref.shape[0], LANES))
    for pr in range(MEM_HEADS // 2):
        qpair = q_ref[:, pr * LANES:(pr + 1) * LANES]
        kblk = kv_ref[:, pr * LANES:(pr + 1) * LANES]
        vblk = kv_ref[:, MEM_WIDTH + pr * LANES:MEM_WIDTH + (pr + 1) * LANES]
        out = None
        for half in range(2):
            keep = (lane < HEAD_DIM) if half == 0 else (lane >= HEAD_DIM)
            kk = jnp.where(keep, kblk, 0.0).astype(BF16)
            vv = jnp.where(keep, vblk, 0.0).astype(BF16)
            s = _dot_nt(qpair, kk)
            m = jnp.max(s, axis=1, keepdims=True)
            e = jnp.exp(s - m)
            p = e / jnp.sum(e, axis=1, keepdims=True)
            o = _dot(p.astype(BF16), vv)
            out = o if out is None else out + o
        o_ref[:, pr * LANES:(pr + 1) * LANES] = out


def _mem_attention(qm, mem_kv, tq):
    b, s, _ = qm.shape
    m = mem_kv.shape[1]
    return pl.pallas_call(
        _memattn_kernel,
        grid=(b, s // tq),
        in_specs=[
            pl.BlockSpec((None, tq, MEM_WIDTH), lambda i, j: (i, j, 0)),
            pl.BlockSpec((None, m, 2 * MEM_WIDTH), lambda i, j: (i, 0, 0)),
        ],
        out_specs=pl.BlockSpec((None, tq, MEM_WIDTH), lambda i, j: (i, j, 0)),
        out_shape=jax.ShapeDtypeStruct((b, s, MEM_WIDTH), F32),
        compiler_params=_cparams("parallel", "parallel"),
        name="mem_attention",
    )(qm, mem_kv)


def _dec_kernel(tab_ref, bidx_ref, pages_ref, new_ref, wq_ref, bias_ref, ns_ref, gate_ref, o_ref,
                acc_ref, m_ref, l_ref, *, n_pg, has_new):
    c = pl.program_id(1)
    n_chunks = pl.num_programs(1)

    @pl.when(c == 0)
    def _():
        acc_ref[...] = jnp.zeros_like(acc_ref)
        m_ref[...] = jnp.full_like(m_ref, NEG)
        l_ref[...] = jnp.zeros_like(l_ref)

    def step(rows):
        s = _dot(rows.astype(BF16), wq_ref[...]) + bias_ref[...]
        rk = _row_iota((LANES, LANES))
        ns = jnp.where(rk < SLC_BLOCK, ns_ref[pl.ds(2 * c, 1), :], ns_ref[pl.ds(2 * c + 1, 1), :])
        s = jnp.where(ns > 0.5, NEG, s)
        m_old = m_ref[...]
        m_new = jnp.maximum(m_old, jnp.max(s, axis=0, keepdims=True))
        alpha = jnp.exp(m_old - m_new)
        p = jnp.exp(s - m_new)
        l_ref[...] = alpha * l_ref[...] + jnp.sum(p, axis=0, keepdims=True)
        m_ref[...] = m_new
        acc_ref[...] = acc_ref[...] * alpha + _dot(rows.T.astype(BF16), p.astype(BF16))

    if has_new:
        @pl.when(c < n_pg)
        def _():
            step(pages_ref[...])

        @pl.when(c == n_pg)
        def _():
            step(new_ref[...])
    else:
        step(pages_ref[...])

    @pl.when(c == n_chunks - 1)
    def _():
        o_ref[...] = acc_ref[...] / l_ref[...] * gate_ref[...]


def _decode_attention(pages, table, new_rows, wq, bias_tab, bias_idx, notsel, gate):
    bsz, n_pg = table.shape
    w = pages.shape[2]
    has_new = new_rows is not None
    n_chunks = n_pg + (1 if has_new else 0)
    if not has_new:
        new_rows = jnp.zeros((1, LANES, w), F32)
    new_map = (lambda i, c, tab, bi: (i, 0, 0)) if has_new else (lambda i, c, tab, bi: (0, 0, 0))
    grid_spec = pltpu.PrefetchScalarGridSpec(
        num_scalar_prefetch=2,
        grid=(bsz, n_chunks),
        in_specs=[
            pl.BlockSpec((None, LANES, w), lambda i, c, tab, bi: (tab[i, jnp.minimum(c, n_pg - 1)], 0, 0)),
            pl.BlockSpec((None, LANES, w), new_map),
            pl.BlockSpec((None, w, LANES), lambda i, c, tab, bi: (i, 0, 0)),
            pl.BlockSpec((None, LANES, LANES), lambda i, c, tab, bi: (bi[c], 0, 0)),
            pl.BlockSpec((None, notsel.shape[1], LANES), lambda i, c, tab, bi: (i, 0, 0)),
            pl.BlockSpec((None, 1, LANES), lambda i, c, tab, bi: (i, 0, 0)),
        ],
        out_specs=pl.BlockSpec((None, w, LANES), lambda i, c, tab, bi: (i, 0, 0)),
        scratch_shapes=[pltpu.VMEM((w, LANES), F32), pltpu.VMEM((1, LANES), F32), pltpu.VMEM((1, LANES), F32)],
    )
    return pl.pallas_call(
        functools.partial(_dec_kernel, n_pg=n_pg, has_new=has_new),
        grid_spec=grid_spec,
        out_shape=jax.ShapeDtypeStruct((bsz, w, LANES), F32),
        compiler_params=_cparams("parallel", "arbitrary"),
        name="decode_attention",
    )(table, bias_idx, pages, new_rows, wq, bias_tab, notsel, gate)


def _tail_kernel(x_ref, g1_ref, op_ref, oc_ref, os_ref, ow_ref, om_ref, wgb_ref, wup_p_ref, wup_n_ref, wup_m_ref,
                 wout_ref, g2_ref, rwh_ref, rwl_ref, rb_ref, x2_ref, h2_ref, ei_ref, gt_ref):
    x = x_ref[...]
    h = _rms(x, g1_ref[...]).astype(BF16)
    onsa = (oc_ref[...] + os_ref[...] + ow_ref[...]).astype(BF16)
    ups = (_dot(op_ref[...].astype(BF16), wup_p_ref[...]), _dot(onsa, wup_n_ref[...]),
           _dot(om_ref[...].astype(BF16), wup_m_ref[...]))
    mixed = None
    for br in range(3):
        gb = _sigmoid(_dot(h, wgb_ref[:, br * D_MODEL:(br + 1) * D_MODEL]))
        mixed = gb * ups[br] if mixed is None else mixed + gb * ups[br]
    x2 = x + _dot(mixed.astype(BF16), wout_ref[...])
    x2_ref[...] = x2
    h2 = _rms(x2, g2_ref[...])
    h2_ref[...] = h2.astype(BF16)
    hi = h2.astype(BF16)
    lo = (h2 - hi.astype(F32)).astype(BF16)
    logits = _dot(hi, rwh_ref[...]) + _dot(lo, rwh_ref[...]) + _dot(hi, rwl_ref[...]) + rb_ref[...]
    lane = _lane_iota(logits.shape)
    lanef = lane.astype(F32)
    tops, idxs = [], []
    for _ in range(TOP_K):
        m = jnp.max(logits, axis=1, keepdims=True)
        idx = jnp.min(jnp.where(logits == m, lanef, 1e9), axis=1, keepdims=True)
        logits = jnp.where(lanef == idx, -jnp.inf, logits)
        tops.append(m)
        idxs.append(idx)
    es = [jnp.exp(tk - tops[0]) for tk in tops]
    den = es[0] + es[1] + es[2] + es[3]
    ei = jnp.zeros(logits.shape, F32)
    gt = jnp.zeros(logits.shape, F32)
    for k in range(TOP_K):
        ei = jnp.where(lane == k, idxs[k], ei)
        gt = jnp.where(lane == k, es[k] / den, gt)
    ei_ref[...] = ei.astype(jnp.int32)
    gt_ref[...] = gt


def _layer_tail(x, o_pool, o_cmp, o_slc, o_win, o_mem, w, tm):
    n = x.shape[0]
    row = lambda wd: pl.BlockSpec((tm, wd), lambda i: (i, 0))
    full = lambda a: pl.BlockSpec(a.shape, lambda i: (0,) * a.ndim)
    weights = (w["wgb"], w["wup_pool"], w["wup_nsa"], w["wup_mem"], w["wout"], w["g2"], w["rw_hi"], w["rw_lo"],
               w["rb"])
    return pl.pallas_call(
        _tail_kernel,
        grid=(n // tm,),
        in_specs=[row(D_MODEL), full(w["g1"]), row(POOL_WIDTH), row(NSA_WIDTH), row(NSA_WIDTH), row(NSA_WIDTH),
                  row(MEM_WIDTH)] + [full(a) for a in weights],
        out_specs=[row(D_MODEL), row(D_MODEL), row(LANES), row(LANES)],
        out_shape=[jax.ShapeDtypeStruct((n, D_MODEL), F32), jax.ShapeDtypeStruct((n, D_MODEL), BF16),
                   jax.ShapeDtypeStruct((n, LANES), jnp.int32), jax.ShapeDtypeStruct((n, LANES), F32)],
        compiler_params=_cparams("parallel"),
        name="layer_tail",
    )(x, w["g1"], o_pool, o_cmp, o_slc, o_win, o_mem, *weights)


def _ffn_kernel(be_ref, nu_ref, x_ref, wgu_ref, bgu_ref, wd_ref, bd_ref, o_ref):
    @pl.when(pl.program_id(0) < nu_ref[0])
    def _():
        gu = _dot(x_ref[...], wgu_ref[...]) + bgu_ref[...]
        gate = jnp.minimum(gu[:, :D_FF], SWIGLU_LIMIT)
        up = jnp.clip(gu[:, D_FF:], -SWIGLU_LIMIT, SWIGLU_LIMIT)
        act = gate * _sigmoid(SWIGLU_ALPHA * gate) * (up + 1.0)
        o_ref[...] = _dot(act.astype(BF16), wd_ref[...]) + bd_ref[...]


def _expert_ffn(rows, blk_e, n_used, wgu, bgu, wd, bd):
    n_rows = rows.shape[0]
    n_blocks = n_rows // MOE_TILE
    blk = lambda i, be, nu: (jnp.minimum(i, nu[0] - 1), 0)
    grid_spec = pltpu.PrefetchScalarGridSpec(
        num_scalar_prefetch=2,
        grid=(n_blocks,),
        in_specs=[
            pl.BlockSpec((MOE_TILE, D_MODEL), blk),
            pl.BlockSpec((None, D_MODEL, 2 * D_FF), lambda i, be, nu: (be[i], 0, 0)),
            pl.BlockSpec((None, 1, 2 * D_FF), lambda i, be, nu: (be[i], 0, 0)),
            pl.BlockSpec((None, D_FF, D_MODEL), lambda i, be, nu: (be[i], 0, 0)),
            pl.BlockSpec((None, 1, D_MODEL), lambda i, be, nu: (be[i], 0, 0)),
        ],
        out_specs=pl.BlockSpec((MOE_TILE, D_MODEL), blk),
    )
    return pl.pallas_call(
        _ffn_kernel,
        grid_spec=grid_spec,
        out_shape=jax.ShapeDtypeStruct((n_rows, D_MODEL), F32),
        compiler_params=_cparams("arbitrary"),
        name="expert_ffn",
    )(blk_e, n_used, rows, wgu, bgu, wd, bd)


def _combine_kernel(x2_ref, y_ref, g_ref, o_ref):
    out = x2_ref[...]
    for k in range(TOP_K):
        out = out + g_ref[:, k:k + 1] * y_ref[:, k, :]
    o_ref[...] = out


def _moe_combine(x2, y4, gates, tm):
    n = x2.shape[0]
    return pl.pallas_call(
        _combine_kernel,
        grid=(n // tm,),
        in_specs=[pl.BlockSpec((tm, D_MODEL), lambda i: (i, 0)),
                  pl.BlockSpec((tm, TOP_K, D_MODEL), lambda i: (i, 0, 0)),
                  pl.BlockSpec((tm, LANES), lambda i: (i, 0))],
        out_specs=pl.BlockSpec((tm, D_MODEL), lambda i: (i, 0)),
        out_shape=jax.ShapeDtypeStruct((n, D_MODEL), F32),
        compiler_params=_cparams("parallel"),
        name="moe_combine",
    )(x2, y4, gates)


def _moe(x2, h2, eidx, gates, w):
    n = x2.shape[0]
    a = n * TOP_K
    flat_e = eidx[:, :TOP_K].reshape(-1)
    order = jnp.argsort(flat_e, stable=True)
    sorted_e = flat_e[order]
    counts = jnp.bincount(flat_e, length=N_EXPERTS)
    starts = jnp.cumsum(counts) - counts
    padded = (counts + MOE_TILE - 1) // MOE_TILE * MOE_TILE
    pad_ends = jnp.cumsum(padded)
    pad_starts = pad_ends - padded
    dest = (pad_starts[sorted_e] + jnp.arange(a) - starts[sorted_e]).astype(jnp.int32)
    n_blocks = -(-a // MOE_TILE) + N_EXPERTS
    n_rows = n_blocks * MOE_TILE
    row_tok = jnp.zeros((n_rows,), jnp.int32).at[dest].set((order // TOP_K).astype(jnp.int32))
    rows = jnp.take(h2, row_tok, axis=0)
    blk_e = jnp.minimum(jnp.searchsorted(pad_ends, jnp.arange(n_blocks) * MOE_TILE, side="right"),
                        N_EXPERTS - 1).astype(jnp.int32)
    n_used = (pad_ends[-1] // MOE_TILE).astype(jnp.int32).reshape(1)
    out_rows = _expert_ffn(rows, blk_e, n_used, w["wgu"], w["bgu"], w["wd"], w["bd"])
    pos = jnp.zeros((a,), jnp.int32).at[order].set(dest)
    y4 = jnp.take(out_rows, pos, axis=0).reshape(n, TOP_K, D_MODEL)
    tm = 256 if n % 256 == 0 else 128
    return _moe_combine(x2, y4, gates, tm)


def _rel_bucket(dist):
    n = jnp.maximum(dist, 0)
    max_exact = NUM_BUCKETS // 2
    nf = jnp.maximum(n, 1).astype(F32)
    large = max_exact + (jnp.log(nf / max_exact) / math.log(MAX_DISTANCE / max_exact)
                         * (NUM_BUCKETS - max_exact)).astype(jnp.int32)
    large = jnp.minimum(large, NUM_BUCKETS - 1)
    return jnp.where(n < max_exact, n, large)


def _bias_of(rel_bias, dist, valid):
    return jnp.where(valid[..., None], rel_bias[_rel_bucket(dist)], NEG)


_PROJ_SEGS = ((0, 512, "qscale"), (512, 256, "qscale"), (768, 256, "id"), (1024, 256, "id"), (1280, 256, "id"),
              (1536, 256, "id"), (1792, 128, "sigmoid"))
_PROJ_DTYPES = (BF16, BF16, F32, F32, F32, F32, F32)
_PROJ_NNORM = 1280


def _prep_layer(l, rel_bias, norm1_g, w_in, nsa_qk_norm, mem_qk_norm, cmp_w, cmp_pe, pool_w, pool_scale,
                mem_norm_g, w_mem_kv, w_up_pool, w_up_nsa, w_up_mem, w_out, norm2_g, router_w, router_b,
                w_gu, b_gu, w_down, b_down):
    wi = w_in[l]
    o_u, o_q, o_qm, o_kvc, o_kvs, o_kvw, o_gn, o_gb = 0, 256, 768, 1024, 1280, 1536, 1792, 1816
    w_proj = jnp.concatenate([
        wi[:, o_q:o_q + 512], wi[:, o_qm:o_qm + 256], wi[:, o_kvs:o_kvs + 256], wi[:, o_kvw:o_kvw + 256],
        wi[:, o_kvc:o_kvc + 256], wi[:, o_u:o_u + 256], wi[:, o_gn:o_gn + 24],
        jnp.zeros((D_MODEL, LANES - 24), F32)], axis=1).astype(BF16)
    nq, mq = nsa_qk_norm[l], mem_qk_norm[l]
    ones = jnp.ones((LANES,), F32)
    gain = jnp.concatenate([jnp.tile(nq[0], 8), jnp.tile(mq[0], 4), jnp.tile(nq[2], 2), ones,
                            jnp.tile(nq[3], 2), ones])[None, :]
    nmask = jnp.concatenate([jnp.ones((768,), F32), ones, 0 * ones, ones, 0 * ones])[None, :]
    eye4 = jnp.eye(4, dtype=F32)
    cw = cmp_w[l].reshape(2, 2, CMP_STRIDE, HEAD_DIM, HEAD_DIM)
    w_c = jnp.einsum("crjde,xy->cjxdrye", cw, jnp.eye(2, dtype=F32))
    w_c = w_c.reshape(2, CMP_STRIDE, LANES, KV_WIDTH).astype(BF16)
    pe = cmp_pe[l].reshape(2, 2, CMP_STRIDE, HEAD_DIM)
    pe_c = jnp.tile(pe.transpose(0, 2, 1, 3), (1, 1, 1, NSA_KV_HEADS))
    pe_c = jnp.pad(pe_c, ((0, 0), (0, 0), (0, 6), (0, 0)))
    w_pool = jnp.einsum("gde,gh->gdhe", pool_w[l], eye4).reshape(POOL_WIDTH, POOL_WIDTH).astype(BF16)
    rw = jnp.pad(router_w[l], ((0, 0), (0, LANES - N_EXPERTS)))
    rw_hi = rw.astype(BF16)
    rw_lo = (rw - rw_hi.astype(F32)).astype(BF16)
    rb = jnp.concatenate([router_b[l], jnp.full((LANES - N_EXPERTS,), NEG, F32)])[None, :]
    return {
        "g1": norm1_g[l][None, :], "w_proj": w_proj, "gain": gain, "nmask": nmask,
        "gk_cmp": jnp.tile(nq[1], 2)[None, :], "w_c": w_c, "pe_c": pe_c,
        "w_pool": w_pool, "pool_scale": pool_scale[l][None, :],
        "mem_g": mem_norm_g[l][None, :], "w_mem": w_mem_kv[l].astype(BF16),
        "mem_gain": jnp.concatenate([jnp.tile(mq[1], 4), jnp.ones((256,), F32)])[None, :],
        "mem_nmask": jnp.concatenate([jnp.ones((256,), F32), jnp.zeros((256,), F32)])[None, :],
        "wgb": wi[:, o_gb:o_gb + 3 * D_MODEL].astype(BF16),
        "wup_pool": w_up_pool[l].astype(BF16), "wup_nsa": w_up_nsa[l].astype(BF16),
        "wup_mem": w_up_mem[l].astype(BF16), "wout": w_out[l].astype(BF16), "g2": norm2_g[l][None, :],
        "rw_hi": rw_hi, "rw_lo": rw_lo, "rb": rb,
        "wgu": w_gu[l].astype(BF16), "bgu": b_gu[l][:, None, :], "wd": w_down[l].astype(BF16),
        "bd": b_down[l][:, None, :],
    }


def _project_in(x2d, w, tm):
    return _project(x2d, w["g1"], w["w_proj"], w["gain"], w["nmask"], _PROJ_SEGS, _PROJ_DTYPES, _PROJ_NNORM, tm)


def _flash_tables(rel_bias):
    t = ATT_TILE
    i = jnp.arange(t)[:, None]
    j = jnp.arange(t)[None, :]
    d0 = i - j
    kinds = [
        _bias_of(rel_bias, d0, d0 >= 0),
        _bias_of(rel_bias, d0 + t, jnp.ones((t, t), bool)),
        _bias_of(rel_bias, jnp.full((t, t), 2 * t), jnp.ones((t, t), bool)),
        _bias_of(rel_bias, d0 + WINDOW, d0 + WINDOW < WINDOW),
    ]
    tab = jnp.stack(kinds)
    tab = tab.reshape(4, t, t, NSA_KV_HEADS, 2, 2)
    return tab.transpose(3, 0, 5, 4, 1, 2).reshape(NSA_KV_HEADS, 4, 2, 2 * t, t)


def _prompt_pre(x, mem, w, rel_bias):
    b, s, _ = x.shape
    n = b * s
    tm = 512 if n % 512 == 0 else ATT_TILE
    x2d = x.reshape(n, D_MODEL)
    q, qm, kvs, kvw, kvc, u, gn = _project_in(x2d, w, tm)
    r3 = lambda a: a.reshape(b, s, a.shape[-1])
    q, qm, kvs, kvw, kvc, u, gn = map(r3, (q, qm, kvs, kvw, kvc, u, gn))

    o_pool = _pool_mix(u, jnp.zeros((b, 16, POOL_WIDTH), F32), w["w_pool"], w["pool_scale"], 0)

    n_cmp = (s - CMP_BLOCK) // CMP_STRIDE + 1
    n_slc = -(-s // SLC_BLOCK)
    parts = _cmp_partials_dense(kvc, w["pe_c"], w["w_c"])
    n_sub = parts.shape[1]
    tpos = jnp.arange(s)[:, None]
    end = jnp.arange(n_sub)[None, :] * CMP_STRIDE + CMP_BLOCK - 1
    bias_c = _bias_of(rel_bias, tpos - end, (end <= tpos) & (jnp.arange(n_sub)[None, :] < n_cmp))
    bias_c = bias_c.transpose(2, 0, 1)
    tq = 256 if s % 256 == 0 else ATT_TILE
    o_cmp, ns0, ns1 = _cmp_attention(q, parts, bias_c, gn, w["gk_cmp"], tq=tq, n_cmp=n_cmp, n_slc=n_slc, pos0=0)

    tab = _flash_tables(rel_bias)
    o_slc = _flash_attention(q, ns0, ns1, kvs, tab, gn, use_sel=True, band=None, gate_base=NSA_HEADS)
    o_win = _flash_attention(q, ns0, ns1, kvw, tab, gn, use_sel=False, band=WINDOW // ATT_TILE,
                             gate_base=2 * NSA_HEADS)

    m = mem.shape[1]
    (mem_kv,) = _project(mem.reshape(b * m, D_MODEL), w["mem_g"], w["w_mem"], w["mem_gain"], w["mem_nmask"],
                         ((0, 2 * MEM_WIDTH, "id"),), (F32,), MXU_DIM, tm=min(512, b * m))
    mem_kv = mem_kv.reshape(b, m, 2 * MEM_WIDTH)
    o_mem = _mem_attention(qm, mem_kv, tq=min(512, s))

    f2 = lambda a: a.reshape(n, a.shape[-1])
    x2, h2, eidx, gates = _layer_tail(x2d, f2(o_pool), f2(o_cmp), f2(o_slc), f2(o_win), f2(o_mem), w, tm)
    kvshape = (b, s, 2, NSA_KV_HEADS, HEAD_DIM)
    win_buf = WINDOW
    states = (kvc.reshape(kvshape), kvs.reshape(kvshape),
              _last_rows(kvw, win_buf).reshape(b, win_buf, 2, NSA_KV_HEADS, HEAD_DIM),
              mem_kv.reshape(b, m, 2, MEM_HEADS, HEAD_DIM), _last_rows(u, POOL_BUF))
    return (x2, h2, eidx, gates), states


def _last_rows(a, n):
    t = a.shape[1]
    if t < n:
        a = jnp.pad(a, [(0, 0), (n - t, 0)] + [(0, 0)] * (a.ndim - 2))
    return a[:, a.shape[1] - n:]


def _dec_columns_nsa(q):
    b, t, _ = q.shape
    qh = q.reshape(b, t, NSA_KV_HEADS, NSA_HPG, HEAD_DIM)
    w = jnp.einsum("btgpd,gx->bxdgtp", qh.astype(F32), jnp.eye(NSA_KV_HEADS, dtype=F32))
    w = w.reshape(b, NSA_KV_HEADS * HEAD_DIM, NSA_KV_HEADS * t * NSA_HPG)
    return jnp.pad(w, ((0, 0), (0, KV_WIDTH - w.shape[1]), (0, LANES - w.shape[2]))).astype(BF16)


def _dec_extract_nsa(o, t):
    b = o.shape[0]
    v = o[:, LANES:, :NSA_KV_HEADS * t * NSA_HPG]
    v = v.reshape(b, NSA_KV_HEADS, HEAD_DIM, NSA_KV_HEADS, t, NSA_HPG)
    v = jnp.einsum("bxdgtp,gx->btgpd", v, jnp.eye(NSA_KV_HEADS, dtype=F32))
    return v.reshape(b, t, NSA_WIDTH)


def _dec_bias_cols(bias_tph):
    k, t, _ = bias_tph.shape
    bt = bias_tph.reshape(k, t, NSA_KV_HEADS, NSA_HPG).transpose(0, 2, 1, 3).reshape(k, NSA_KV_HEADS * t * NSA_HPG)
    return jnp.pad(bt, ((0, 0), (0, LANES - bt.shape[1])))


def _dec_gate_cols(gn, base, t):
    b = gn.shape[0]
    gt = gn[:, :, base:base + NSA_HEADS].reshape(b, t, NSA_KV_HEADS, NSA_HPG).transpose(0, 2, 1, 3)
    gt = gt.reshape(b, 1, NSA_KV_HEADS * t * NSA_HPG)
    return jnp.pad(gt, ((0, 0), (0, 0), (0, LANES - gt.shape[2])), constant_values=1.0)


def _sample_pre(x, cache_cmp, cache_slc, cache_win, cache_mem, pool_buf, page_table, w, rel_bias):
    b, t, _ = x.shape
    n = b * t
    page = cache_cmp.shape[1]
    n_pages = page_table.shape[1]
    past = n_pages * page
    x2d = x.reshape(n, D_MODEL)
    q, qm, kvs, kvw, kvc, u, gn = _project_in(x2d, w, n if n <= 512 else ATT_TILE)
    r3 = lambda a: a.reshape(b, t, a.shape[-1])
    q, qm, kvs, kvw, kvc, u, gn = map(r3, (q, qm, kvs, kvw, kvc, u, gn))
    qpos = past + jnp.arange(t)

    buf16 = jnp.pad(pool_buf, ((0, 0), (16 - POOL_BUF, 0), (0, 0)))
    o_pool = _pool_mix(u, buf16, w["w_pool"], w["pool_scale"], past)

    total = past + t
    n_cmp = (total - CMP_BLOCK) // CMP_STRIDE + 1
    n_sub_used = n_cmp + CMP_BLOCK // CMP_STRIDE - 1
    n_slc = -(-total // SLC_BLOCK)
    pool_c = cache_cmp.reshape(cache_cmp.shape[0], page, KV_WIDTH)
    pps = math.gcd(n_pages, 16)
    parts = _cmp_partials_paged(pool_c, page_table, w["pe_c"], w["w_c"], pps)
    extra = n_sub_used * CMP_STRIDE - past
    if extra > 0:
        tail_rows = -(-extra // CMP_STRIDE) * CMP_STRIDE
        new_c = jnp.pad(kvc, ((0, 0), (0, max(0, tail_rows - t)), (0, 0)))[:, :tail_rows]
        parts = jnp.concatenate([parts, _cmp_partials_dense(new_c, w["pe_c"], w["w_c"])], axis=1)
    n_sub = parts.shape[1]
    end = jnp.arange(n_sub)[None, :] * CMP_STRIDE + CMP_BLOCK - 1
    bias_c = _bias_of(rel_bias, qpos[:, None] - end, (end <= qpos[:, None]) & (jnp.arange(n_sub)[None, :] < n_cmp))
    bias_c = bias_c.transpose(2, 0, 1)
    o_cmp, ns0, ns1 = _cmp_attention(q, parts, bias_c, gn, w["gk_cmp"], tq=t, n_cmp=n_cmp, n_slc=n_slc, pos0=past)

    wq = _dec_columns_nsa(q)
    ncol = NSA_KV_HEADS * t * NSA_HPG

    n_chunks = n_pages + 1
    nblk = -(-2 * n_chunks // 8) * 8
    ns = jnp.stack([ns0, ns1], axis=1)
    ns = jnp.pad(ns, ((0, 0), (0, 0), (0, 0), (0, max(0, nblk - ns.shape[3]))))[..., :nblk]
    ns = jnp.broadcast_to(ns[:, :, :, None, :], (b, NSA_KV_HEADS, t, NSA_HPG, nblk)).reshape(b, ncol, nblk)
    notsel = jnp.pad(ns.transpose(0, 2, 1), ((0, 0), (0, 0), (0, LANES - ncol)))
    rows = jnp.arange(LANES)
    far = _bias_of(rel_bias, jnp.full((LANES, t), 2 * MAX_DISTANCE), jnp.ones((LANES, t), bool))
    kpos_last = past - LANES + rows
    d_last = qpos[None, :] - kpos_last[:, None]
    near = _bias_of(rel_bias, d_last, d_last >= 0)
    kpos_new = past + rows
    d_new = qpos[None, :] - kpos_new[:, None]
    newb = _bias_of(rel_bias, d_new, (d_new >= 0) & (rows[:, None] < t))
    bias_tab = jnp.stack([_dec_bias_cols(far), _dec_bias_cols(near), _dec_bias_cols(newb)])
    bias_idx = jnp.concatenate([jnp.zeros((n_pages - 1,), jnp.int32), jnp.array([1, 2], jnp.int32)])
    pool_s = cache_slc.reshape(cache_slc.shape[0], page, KV_WIDTH)
    new_s = jnp.pad(kvs, ((0, 0), (0, LANES - t), (0, 0)))
    o_slc = _decode_attention(pool_s, page_table, new_s, wq, bias_tab, bias_idx, notsel,
                              _dec_gate_cols(gn, NSA_HEADS, t))
    o_slc = _dec_extract_nsa(o_slc, t)

    wb = cache_win.shape[1]
    n_wpg = wb // LANES
    win_pages = cache_win.reshape(b * n_wpg, LANES, KV_WIDTH)
    win_table = (jnp.arange(b)[:, None] * n_wpg + jnp.arange(n_wpg)[None, :]).astype(jnp.int32)
    kpos_w = past - wb + jnp.arange(wb + LANES)
    d_w = qpos[None, :] - kpos_w[:, None]
    valid_w = (d_w >= 0) & (d_w < WINDOW) & (kpos_w[:, None] >= 0) & (jnp.arange(wb + LANES)[:, None] < wb + t)
    bias_w = _dec_bias_cols(_bias_of(rel_bias, d_w, valid_w)).reshape(n_wpg + 1, LANES, LANES)
    zeros_ns = jnp.zeros((b, -(-2 * (n_wpg + 1) // 8) * 8, LANES), F32)
    new_w = jnp.pad(kvw, ((0, 0), (0, LANES - t), (0, 0)))
    o_win = _decode_attention(win_pages, win_table, new_w, wq, bias_w, jnp.arange(n_wpg + 1, dtype=jnp.int32),
                              zeros_ns, _dec_gate_cols(gn, 2 * NSA_HEADS, t))
    o_win = _dec_extract_nsa(o_win, t)

    m = cache_mem.shape[1]
    n_mpg = m // LANES
    mem_pages = cache_mem.reshape(b * n_mpg, LANES, 2 * MEM_WIDTH)
    mem_table = (jnp.arange(b)[:, None] * n_mpg + jnp.arange(n_mpg)[None, :]).astype(jnp.int32)
    qmh = qm.reshape(b, t, MEM_HEADS, HEAD_DIM).astype(F32)
    wqm = jnp.einsum("bthd,hx->bxdht", qmh, jnp.eye(MEM_HEADS, dtype=F32))
    wqm = wqm.reshape(b, MEM_WIDTH, MEM_HEADS * t)
    wqm = jnp.pad(wqm, ((0, 0), (0, MEM_WIDTH), (0, LANES - MEM_HEADS * t))).astype(BF16)
    o_mem = _decode_attention(mem_pages, mem_table, None, wqm, jnp.zeros((1, LANES, LANES), F32),
                              jnp.zeros((n_mpg,), jnp.int32), jnp.zeros((b, 8, LANES), F32),
                              jnp.ones((b, 1, LANES), F32))
    om = o_mem[:, MEM_WIDTH:, :MEM_HEADS * t].reshape(b, MEM_HEADS, HEAD_DIM, MEM_HEADS, t)
    o_mem = jnp.einsum("bxdht,hx->bthd", om, jnp.eye(MEM_HEADS, dtype=F32)).reshape(b, t, MEM_WIDTH)

    f2 = lambda a: a.reshape(n, a.shape[-1])
    x2, h2, eidx, gates = _layer_tail(x2d, f2(o_pool), f2(o_cmp), f2(o_slc), f2(o_win), f2(o_mem), w,
                                      n if n <= 512 else ATT_TILE)
    kvshape = (b, t, 2, NSA_KV_HEADS, HEAD_DIM)
    new_win = jnp.concatenate([cache_win, kvw], axis=1)[:, t:]
    new_pool = jnp.concatenate([pool_buf, u], axis=1)[:, t:]
    states = (kvc.reshape(kvshape), kvs.reshape(kvshape), new_win, new_pool)
    return (x2, h2, eidx, gates), states


def kernel(x_prompt, x_sample, cache_cmp_kv, cache_slc_kv, cache_win_kv, cache_mem_kv, state_pool, page_table,
           mem_prompt, rel_bias, norm1_g, w_in, nsa_qk_norm, mem_qk_norm, cmp_w, cmp_pe, pool_w, pool_scale,
           mem_norm_g, w_mem_kv, w_up_pool, w_up_nsa, w_up_mem, w_out, norm2_g, router_w, router_b, w_gu, b_gu,
           w_down, b_down):
    depth = w_in.shape[0]
    yp, ys = x_prompt, x_sample
    bp, sp, _ = x_prompt.shape
    bs, ts, _ = x_sample.shape
    outs_p = [[] for _ in range(5)]
    outs_s = [[] for _ in range(4)]
    for l in range(depth):
        w = _prep_layer(l, rel_bias, norm1_g, w_in, nsa_qk_norm, mem_qk_norm, cmp_w, cmp_pe, pool_w, pool_scale,
                        mem_norm_g, w_mem_kv, w_up_pool, w_up_nsa, w_up_mem, w_out, norm2_g, router_w, router_b,
                        w_gu, b_gu, w_down, b_down)
        pre_p, st_p = _prompt_pre(yp, mem_prompt, w, rel_bias)
        cw = cache_win_kv[l]
        pre_s, st_s = _sample_pre(ys, cache_cmp_kv[l], cache_slc_kv[l],
                                  cw.reshape(cw.shape[0], cw.shape[1], KV_WIDTH),
                                  cache_mem_kv[l].reshape(bs, cache_mem_kv.shape[2], 2 * MEM_WIDTH),
                                  state_pool[l], page_table, w, rel_bias)
        x2, h2, eidx, gates = (jnp.concatenate([a, c], axis=0) for a, c in zip(pre_p, pre_s))
        y = _moe(x2, h2, eidx, gates, w)
        yp = y[:bp * sp].reshape(bp, sp, D_MODEL)
        ys = y[bp * sp:].reshape(bs, ts, D_MODEL)
        for lst, a in zip(outs_p, st_p):
            lst.append(a)
        for lst, a in zip(outs_s, st_s):
            lst.append(a)
    new_cmp_p, new_slc_p, new_win_p, new_mem_p, new_pool_p = [jnp.stack(a) for a in outs_p]
    new_cmp_s, new_slc_s, new_win_s, new_pool_s = [jnp.stack(a) for a in outs_s]
    new_win_s = new_win_s.reshape(new_win_s.shape[:3] + (2, NSA_KV_HEADS, HEAD_DIM))
    return (yp, ys, new_cmp_p, new_slc_p, new_win_p, new_mem_p, new_pool_p,
            new_cmp_s, new_slc_s, new_win_s, new_pool_s)
```

```python
import functools
import math

import jax
import jax.numpy as jnp
from jax import lax
from jax.experimental import pallas as pl
from jax.experimental.pallas import tpu as pltpu

F32 = jnp.float32
BF16 = jnp.bfloat16

D_MODEL = 1024
HEAD_DIM = 64
POOL_WINDOWS = (2, 4, 8, 16)
POOL_GROUP = 64
POOL_WIDTH = 256
POOL_BUF = 15
NSA_HEADS = 8
NSA_KV_HEADS = 2
NSA_HPG = 4
NSA_WIDTH = 512
KV_WIDTH = 256
CMP_BLOCK = 32
CMP_STRIDE = 16
SLC_BLOCK = 64
SLC_TOPK = 16
WINDOW = 512
MEM_HEADS = 4
MEM_WIDTH = 256
NUM_BUCKETS = 32
MAX_DISTANCE = 128
N_EXPERTS = 32
TOP_K = 4
D_FF = 1024
SWIGLU_ALPHA = 1.702
SWIGLU_LIMIT = 7.0
EPS = 1e-6
SCALE = HEAD_DIM ** -0.5

LANES = 128
MXU_DIM = 256
NEG = -1e30
ATT_TILE = 256
DEC_PAGES_PER_STEP = 8
MOE_TILE = 256
MOE_DMA_TOKENS = 512
VMEM_LIMIT = 48 * 1024 * 1024


def _cparams(*sem):
    return pltpu.CompilerParams(dimension_semantics=sem, vmem_limit_bytes=VMEM_LIMIT)


def _dot(a, b):
    return jnp.dot(a, b, preferred_element_type=F32)


def _dot_nt(a, b):
    return lax.dot_general(a, b, (((1,), (1,)), ((), ())), preferred_element_type=F32)


def _split_dot(a, b):
    hi = a.astype(BF16)
    lo = (a - hi.astype(F32)).astype(BF16)
    return _dot(hi, b) + _dot(lo, b)


def _rms(x, g):
    r = lax.rsqrt(jnp.mean(x * x, axis=-1, keepdims=True) + EPS)
    return (x * r) * g


def _sigmoid(x):
    return 1.0 / (1.0 + jnp.exp(-x))


def _lane_iota(shape):
    return lax.broadcasted_iota(jnp.int32, shape, len(shape) - 1)


def _row_iota(shape):
    return lax.broadcasted_iota(jnp.int32, shape, len(shape) - 2)


def _proj_kernel(x_ref, g_ref, w_ref, gain_ref, nmask_ref, seg_ref, *out_refs, segs, n_norm):
    h = _rms(x_ref[...], g_ref[...]).astype(BF16)
    seg = seg_ref[...]
    outs = iter(out_refs)
    for (start, width, kind) in segs:
        o_ref = next(outs)
        z = _dot(h, w_ref[:, start:start + width])
        if start < n_norm:
            pieces = []
            for c in range(0, width, MXU_DIM):
                zc = z[:, c:c + MXU_DIM]
                ms = _split_dot(zc * zc, seg)
                zn = (zc * lax.rsqrt(ms + EPS)) * gain_ref[:, start + c:start + c + MXU_DIM]
                pieces.append(jnp.where(nmask_ref[:, start + c:start + c + MXU_DIM] > 0, zn, zc))
            z = pieces[0] if len(pieces) == 1 else jnp.concatenate(pieces, axis=1)
        if kind == "sigmoid":
            z = _sigmoid(z)
        elif kind == "qscale":
            z = z * SCALE
        o_ref[...] = z.astype(o_ref.dtype)
        if kind == "kv":
            next(outs)[...] = z[:, LANES:2 * LANES].T


def _project(x, g, w, gain, nmask, segs, out_dtypes, n_norm, tm):
    n = x.shape[0]
    ncol = w.shape[1]
    seg = _seg_matrix(MXU_DIM)
    full = lambda i: (0, 0)
    out_specs, out_shape = [], []
    for (_, wd, kind), dt in zip(segs, out_dtypes):
        out_specs.append(pl.BlockSpec((tm, wd), lambda i: (i, 0)))
        out_shape.append(jax.ShapeDtypeStruct((n, wd), dt))
        if kind == "kv":
            out_specs.append(pl.BlockSpec((LANES, tm), lambda i: (0, i)))
            out_shape.append(jax.ShapeDtypeStruct((LANES, n), F32))
    return pl.pallas_call(
        functools.partial(_proj_kernel, segs=segs, n_norm=n_norm),
        grid=(n // tm,),
        in_specs=[
            pl.BlockSpec((tm, D_MODEL), lambda i: (i, 0)),
            pl.BlockSpec((1, D_MODEL), full),
            pl.BlockSpec((D_MODEL, ncol), full),
            pl.BlockSpec((1, gain.shape[1]), full),
            pl.BlockSpec((1, nmask.shape[1]), full),
            pl.BlockSpec((MXU_DIM, MXU_DIM), full),
        ],
        out_specs=out_specs,
        out_shape=out_shape,
        compiler_params=_cparams("parallel"),
        name="proj",
    )(x, g, w, gain, nmask, seg)


def _seg_matrix(n):
    i = jnp.arange(n) // HEAD_DIM
    return jnp.where(i[:, None] == i[None, :], 1.0 / HEAD_DIM, 0.0).astype(BF16)


def _pool_kernel(u_ref, buf_ref, w_ref, scale_ref, o_ref, zs_ref, *, t, pos0):
    zs_ref[0:16, :] = buf_ref[...]
    zs_ref[16:16 + t, :] = u_ref[...]
    u = u_ref[...]
    lane = _lane_iota((1, POOL_WIDTH))
    pos = (pos0 + _row_iota((t, 1))).astype(F32)
    acc = u
    mean = None
    for i in range(1, max(POOL_WINDOWS)):
        acc = acc + zs_ref[16 - i:16 - i + t, :]
        if i + 1 in POOL_WINDOWS:
            gi = POOL_WINDOWS.index(i + 1)
            m = acc / jnp.minimum(pos + 1.0, float(i + 1))
            mean = m if mean is None else jnp.where(lane >= gi * POOL_GROUP, m, mean)
    d = (mean - u).astype(BF16)
    o_ref[...] = _dot(d, w_ref[...]) * scale_ref[...]


def _pool_mix(u, buf16, w_bd, scale, pos0):
    b, t, _ = u.shape
    return pl.pallas_call(
        functools.partial(_pool_kernel, t=t, pos0=pos0),
        grid=(b,),
        in_specs=[
            pl.BlockSpec((None, t, POOL_WIDTH), lambda i: (i, 0, 0)),
            pl.BlockSpec((None, 16, POOL_WIDTH), lambda i: (i, 0, 0)),
            pl.BlockSpec((POOL_WIDTH, POOL_WIDTH), lambda i: (0, 0)),
            pl.BlockSpec((1, POOL_WIDTH), lambda i: (0, 0)),
        ],
        out_specs=pl.BlockSpec((None, t, POOL_WIDTH), lambda i: (i, 0, 0)),
        out_shape=jax.ShapeDtypeStruct((b, t, POOL_WIDTH), F32),
        scratch_shapes=[pltpu.VMEM((t + 16, POOL_WIDTH), F32)],
        compiler_params=_cparams("parallel"),
        name="pool",
    )(u, buf16, w_bd, scale)


def _cpart_kernel(*refs, nop, rows_per):
    x_refs = refs[-(2 * nop + 3):-3]
    pe_ref, w_ref, o_ref = refs[-3:]
    n = rows_per // CMP_STRIDE
    m = nop * n
    for c in range(2):
        acc = jnp.zeros((m + 8, KV_WIDTH), F32)
        for j in range(CMP_STRIDE):
            parts = [r[pl.ds(j, n, stride=CMP_STRIDE), :] for r in x_refs[c * nop:(c + 1) * nop]]
            lhs = jnp.concatenate(parts + [pe_ref[c, j]], axis=0).astype(BF16)
            acc = acc + _dot(lhs, w_ref[c, j])
        lane = _lane_iota((1, KV_WIDTH))
        pe_term = jnp.where(lane < LANES, acc[m:m + 1], acc[m + 1:m + 2])
        o_ref[:, c * KV_WIDTH:(c + 1) * KV_WIDTH] = acc[0:m] + pe_term


def _cmp_partials_dense(kv, pe, w_c):
    b, t, _ = kv.shape
    rows = (t // CMP_STRIDE) * CMP_STRIDE
    n = rows // CMP_STRIDE
    return pl.pallas_call(
        functools.partial(_cpart_kernel, nop=1, rows_per=rows),
        grid=(b,),
        in_specs=[
            pl.BlockSpec((None, rows, LANES), lambda i: (i, 0, 0)),
            pl.BlockSpec((None, rows, LANES), lambda i: (i, 0, 1)),
            pl.BlockSpec(pe.shape, lambda i: (0, 0, 0, 0)),
            pl.BlockSpec(w_c.shape, lambda i: (0, 0, 0, 0)),
        ],
        out_specs=pl.BlockSpec((None, n, 2 * KV_WIDTH), lambda i: (i, 0, 0)),
        out_shape=jax.ShapeDtypeStruct((b, n, 2 * KV_WIDTH), F32),
        compiler_params=_cparams("parallel"),
        name="cmp_partials",
    )(kv, kv, pe, w_c)


def _cmp_partials_paged(pool, page_table, pe, w_c, pages_per_step):
    b, n_pages = page_table.shape
    page = pool.shape[1]
    nop = pages_per_step
    n = nop * page // CMP_STRIDE

    def page_spec(k, half):
        return pl.BlockSpec((None, page, LANES), lambda i, c, pt: (pt[i, c * nop + k], 0, half))

    grid_spec = pltpu.PrefetchScalarGridSpec(
        num_scalar_prefetch=1,
        grid=(b, n_pages // nop),
        in_specs=[page_spec(k, half) for half in range(2) for k in range(nop)] + [
            pl.BlockSpec(pe.shape, lambda i, c, pt: (0, 0, 0, 0)),
            pl.BlockSpec(w_c.shape, lambda i, c, pt: (0, 0, 0, 0)),
        ],
        out_specs=pl.BlockSpec((None, n, 2 * KV_WIDTH), lambda i, c, pt: (i, c, 0)),
    )
    return pl.pallas_call(
        functools.partial(_cpart_kernel, nop=nop, rows_per=page),
        grid_spec=grid_spec,
        out_shape=jax.ShapeDtypeStruct((b, n_pages * page // CMP_STRIDE, 2 * KV_WIDTH), F32),
        compiler_params=_cparams("parallel", "arbitrary"),
        name="cmp_partials_paged",
    )(page_table, *([pool] * (2 * nop)), pe, w_c)


def _pair_operands(x128, g):
    lane = _lane_iota(x128.shape)
    if g == 0:
        lo = jnp.where(lane < HEAD_DIM, x128, 0.0)
        hi = pltpu.roll(lo, HEAD_DIM, 1)
    else:
        hi = jnp.where(lane >= HEAD_DIM, x128, 0.0)
        lo = pltpu.roll(hi, HEAD_DIM, 1)
    return lo.astype(BF16), hi.astype(BF16)


def _cattn_kernel(q_ref, p_ref, bias_ref, gn_ref, gk_ref, seg_ref, o_ref, ns0_ref, ns1_ref, *,
                  tq, n_sub, n_cmp, n_slc, nslp, pos0):
    qi = pl.program_id(0)
    pall = p_ref[...]
    kraw = pall[:, 0:LANES] + pltpu.roll(pall[:, LANES:2 * LANES], n_sub - 1, 0)
    vc = pall[:, 2 * LANES:3 * LANES] + pltpu.roll(pall[:, 3 * LANES:4 * LANES], n_sub - 1, 0)
    ms = _split_dot(kraw * kraw, seg_ref[...])
    kc = (kraw * lax.rsqrt(ms + EPS)) * gk_ref[...]

    qpos = pos0 + qi * tq + _row_iota((tq, 1))
    qblk = jnp.right_shift(qpos, SLC_BLOCK.bit_length() - 1)
    nn = _row_iota((n_sub, nslp))
    jj = _lane_iota((n_sub, nslp))
    covers = ((nn * CMP_STRIDE < (jj + 1) * SLC_BLOCK) & (nn * CMP_STRIDE + CMP_BLOCK - 1 >= jj * SLC_BLOCK)
              & (nn < n_cmp) & (jj < n_slc))
    covers = jnp.where(covers, 1.0, 0.0).astype(BF16)
    jl = _lane_iota((tq, nslp))
    jlf = jl.astype(F32)
    forced = (jl == 0) | (jl == qblk) | (jl == qblk - 1)
    causal = jl <= qblk

    for g in range(NSA_KV_HEADS):
        k_lo, k_hi = _pair_operands(kc, g)
        v_lo, v_hi = _pair_operands(vc, g)
        prsum = jnp.zeros((tq, n_sub), F32)
        for pr in range(NSA_HPG // 2):
            qpair = q_ref[:, (2 * g + pr) * LANES:(2 * g + pr + 1) * LANES]
            outs = []
            for half, (kk, vv) in enumerate(((k_lo, v_lo), (k_hi, v_hi))):
                h = NSA_HPG * g + 2 * pr + half
                bias = bias_ref[h]
                mask = bias > 0.5 * NEG
                s = _dot_nt(qpair, kk) + bias
                m = jnp.max(s, axis=1, keepdims=True)
                m = jnp.where(m > 0.5 * NEG, m, 0.0)
                e = jnp.where(mask, jnp.exp(s - m), 0.0)
                p = e / jnp.maximum(jnp.sum(e, axis=1, keepdims=True), 1e-30)
                prsum = prsum + p
                outs.append(_dot(p.astype(BF16), vv) * gn_ref[:, h:h + 1])
            o_ref[:, (2 * g + pr) * LANES:(2 * g + pr + 1) * LANES] = outs[0] + outs[1]
        imp = _split_dot(prsum, covers)
        score = jnp.where(forced, jnp.inf, imp)
        score = jnp.where(causal, score, -jnp.inf)
        sel = jnp.zeros((tq, nslp), F32)
        for _ in range(min(SLC_TOPK, n_slc)):
            m = jnp.max(score, axis=1, keepdims=True)
            idx = jnp.min(jnp.where(score == m, jlf, 1e9), axis=1, keepdims=True)
            pick = jlf == idx
            sel = jnp.where(pick & (m > -jnp.inf), 1.0, sel)
            score = jnp.where(pick, -jnp.inf, score)
        (ns0_ref if g == 0 else ns1_ref)[...] = 1.0 - sel


def _cmp_attention(q, parts, bias, gn, gk, *, tq, n_cmp, n_slc, pos0):
    b, s, _ = q.shape
    n_sub = parts.shape[1]
    nslp = -(-n_slc // LANES) * LANES
    kern = functools.partial(_cattn_kernel, tq=tq, n_sub=n_sub, n_cmp=n_cmp, n_slc=n_slc, nslp=nslp, pos0=pos0)
    return pl.pallas_call(
        kern,
        grid=(s // tq, b),
        in_specs=[
            pl.BlockSpec((None, tq, NSA_WIDTH), lambda i, j: (j, i, 0)),
            pl.BlockSpec((None, n_sub, 2 * KV_WIDTH), lambda i, j: (j, 0, 0)),
            pl.BlockSpec((NSA_HEADS, tq, n_sub), lambda i, j: (0, i, 0)),
            pl.BlockSpec((None, tq, LANES), lambda i, j: (j, i, 0)),
            pl.BlockSpec((1, LANES), lambda i, j: (0, 0)),
            pl.BlockSpec((LANES, LANES), lambda i, j: (0, 0)),
        ],
        out_specs=[
            pl.BlockSpec((None, tq, NSA_WIDTH), lambda i, j: (j, i, 0)),
            pl.BlockSpec((None, tq, nslp), lambda i, j: (j, i, 0)),
            pl.BlockSpec((None, tq, nslp), lambda i, j: (j, i, 0)),
        ],
        out_shape=[
            jax.ShapeDtypeStruct((b, s, NSA_WIDTH), F32),
            jax.ShapeDtypeStruct((b, s, nslp), F32),
            jax.ShapeDtypeStruct((b, s, nslp), F32),
        ],
        compiler_params=_cparams("parallel", "parallel"),
        name="cmp_attention",
    )(q, parts, bias, gn, gk, _seg_matrix(LANES))


def _flash_kernel(q_ref, ns0_ref, ns1_ref, kv_ref, vt_ref, tab_ref, gn_ref, o_ref, *, t, use_sel, band, gate_base):
    qi = pl.program_id(1)
    row_k = _row_iota((t, LANES))
    lane_k = _lane_iota((t, LANES))
    lo_tile = jnp.maximum(qi - band, 0) if band is not None else 0
    gnt = gn_ref[...].T
    zeros64 = jnp.zeros((HEAD_DIM, t), BF16)

    qts = []
    for g in range(NSA_KV_HEADS):
        cols = []
        for pr in range(2):
            qt = q_ref[:, (2 * g + pr) * LANES:(2 * g + pr + 1) * LANES].astype(F32).T.astype(BF16)
            for half in range(2):
                qh = qt[half * HEAD_DIM:(half + 1) * HEAD_DIM]
                cols.append(jnp.concatenate([qh, zeros64] if g == 0 else [zeros64, qh], axis=0))
        qt_g = jnp.concatenate(cols, axis=1)
        if use_sel:
            nst = (ns0_ref if g == 0 else ns1_ref)[...].T.astype(BF16)
            qt_g = jnp.concatenate([qt_g, jnp.concatenate([nst] * NSA_HPG, axis=1)], axis=0)
        qts.append(qt_g)

    def body(kj, carry):
        k0 = pl.multiple_of(kj * t, t)
        kk = kv_ref[pl.ds(k0, t), 0:LANES].astype(BF16)
        if use_sel:
            blk = kj * (t // SLC_BLOCK) + jnp.right_shift(row_k, SLC_BLOCK.bit_length() - 1)
            onehot = jnp.where(lane_k == blk, -(2.0 ** 30), 0.0).astype(BF16)
            kk = jnp.concatenate([kk, onehot], axis=1)
        delta = qi - kj
        if band is None:
            kind = jnp.minimum(delta, 2)
        else:
            kind = jnp.where(delta < 2, delta, jnp.where(delta < band, 2, 3))
        new = []
        for g in range(NSA_KV_HEADS):
            m_old, l_old, acc = carry[g]
            s = _dot(kk, qts[g]) + tab_ref[g, kind]
            m_new = jnp.maximum(m_old, jnp.max(s, axis=0, keepdims=True))
            alpha = jnp.exp(m_old - m_new)
            p = jnp.exp(s - m_new)
            l_new = alpha * l_old + jnp.sum(p, axis=0, keepdims=True)
            vt = vt_ref[g * HEAD_DIM:(g + 1) * HEAD_DIM, pl.ds(k0, t)].astype(BF16)
            new.append((m_new, l_new, acc * alpha + _dot(vt, p.astype(BF16))))
        return tuple(new)

    init = (jnp.full((1, NSA_HPG * t), NEG, F32), jnp.zeros((1, NSA_HPG * t), F32),
            jnp.zeros((HEAD_DIM, NSA_HPG * t), F32))
    fin = lax.fori_loop(lo_tile, qi + 1, body, (init, init))
    for g in range(NSA_KV_HEADS):
        _, l_fin, acc = fin[g]
        h0 = gate_base + NSA_HPG * g
        gate = jnp.concatenate([gnt[h0 + c:h0 + c + 1] for c in range(NSA_HPG)], axis=1)
        out = acc * (gate / l_fin)
        for pr in range(2):
            pair = jnp.concatenate([out[:, (2 * pr) * t:(2 * pr + 1) * t], out[:, (2 * pr + 1) * t:(2 * pr + 2) * t]],
                                   axis=0)
            o_ref[:, (2 * g + pr) * LANES:(2 * g + pr + 1) * LANES] = pair.T


def _flash_attention(q, ns0, ns1, kv, vt, tab, gn, *, use_sel, band, gate_base):
    b, s, _ = q.shape
    t = ATT_TILE
    kern = functools.partial(_flash_kernel, t=t, use_sel=use_sel, band=band, gate_base=gate_base)
    tile = lambda w: pl.BlockSpec((None, t, w), lambda i, j: (i, j, 0))
    return pl.pallas_call(
        kern,
        grid=(b, s // t),
        in_specs=[
            tile(NSA_WIDTH), tile(ns0.shape[2]), tile(ns1.shape[2]),
            pl.BlockSpec((None, s, KV_WIDTH), lambda i, j: (i, 0, 0)),
            pl.BlockSpec((LANES, s), lambda i, j: (0, i)),
            pl.BlockSpec(tab.shape, lambda i, j: (0, 0, 0, 0)),
            tile(LANES),
        ],
        out_specs=tile(NSA_WIDTH),
        out_shape=jax.ShapeDtypeStruct((b, s, NSA_WIDTH), F32),
        compiler_params=_cparams("parallel", "parallel"),
        name="flash_sel" if use_sel else "flash_win",
    )(q, ns0, ns1, kv, vt, tab, gn)


def _memattn_kernel(q_ref, kv_ref, o_ref):
    lane = _lane_iota((kv_ref.shape[0], LANES))
    for pr in range(MEM_HEADS // 2):
        qpair = q_ref[:, pr * LANES:(pr + 1) * LANES]
        kblk = kv_ref[:, pr * LANES:(pr + 1) * LANES]
        vblk = kv_ref[:, MEM_WIDTH + pr * LANES:MEM_WIDTH + (pr + 1) * LANES]
        out = None
        for half in range(2):
            keep = (lane < HEAD_DIM) if half == 0 else (lane >= HEAD_DIM)
            kk = jnp.where(keep, kblk, 0.0).astype(BF16)
            vv = jnp.where(keep, vblk, 0.0).astype(BF16)
            s = _dot_nt(qpair, kk)
            m = jnp.max(s, axis=1, keepdims=True)
            e = jnp.exp(s - m)
            p = e / jnp.sum(e, axis=1, keepdims=True)
            o = _dot(p.astype(BF16), vv)
            out = o if out is None else out + o
        o_ref[:, pr * LANES:(pr + 1) * LANES] = out


def _mem_attention(qm, mem_kv, tq):
    b, s, _ = qm.shape
    m = mem_kv.shape[1]
    return pl.pallas_call(
        _memattn_kernel,
        grid=(b, s // tq),
        in_specs=[
            pl.BlockSpec((None, tq, MEM_WIDTH), lambda i, j: (i, j, 0)),
            pl.BlockSpec((None, m, 2 * MEM_WIDTH), lambda i, j: (i, 0, 0)),
        ],
        out_specs=pl.BlockSpec((None, tq, MEM_WIDTH), lambda i, j: (i, j, 0)),
        out_shape=jax.ShapeDtypeStruct((b, s, MEM_WIDTH), F32),
        compiler_params=_cparams("parallel", "parallel"),
        name="mem_attention",
    )(qm, mem_kv)


def _dec_kernel(*refs, n_pg, pps, has_new):
    tab_ref, bidx_ref = refs[0], refs[1]
    page_refs = refs[2:2 + pps]
    new_ref, wq_ref, bias_ref, ns_ref, gate_ref, o_ref, acc_ref, m_ref, l_ref = refs[2 + pps:]
    del tab_ref
    c = pl.program_id(1)
    n_chunks = pl.num_programs(1)

    @pl.when(c == 0)
    def _():
        acc_ref[...] = jnp.zeros_like(acc_ref)
        m_ref[...] = jnp.full_like(m_ref, NEG)
        l_ref[...] = jnp.zeros_like(l_ref)

    rk = _row_iota((LANES, LANES))

    def step(row_tiles, first_page):
        rows = row_tiles[0] if len(row_tiles) == 1 else jnp.concatenate(row_tiles, axis=0)
        s = _dot(rows.astype(BF16), wq_ref[...])
        extra = []
        for k in range(len(row_tiles)):
            pg = first_page + k
            ns = jnp.where(rk < SLC_BLOCK, ns_ref[pl.ds(2 * pg, 1), :], ns_ref[pl.ds(2 * pg + 1, 1), :])
            extra.append(jnp.where(ns > 0.5, NEG, bias_ref[bidx_ref[pg]]))
        s = s + (extra[0] if len(extra) == 1 else jnp.concatenate(extra, axis=0))
        m_old = m_ref[...]
        m_new = jnp.maximum(m_old, jnp.max(s, axis=0, keepdims=True))
        alpha = jnp.exp(m_old - m_new)
        p = jnp.exp(s - m_new)
        l_ref[...] = alpha * l_ref[...] + jnp.sum(p, axis=0, keepdims=True)
        m_ref[...] = m_new
        acc_ref[...] = acc_ref[...] * alpha + _dot(rows.T.astype(BF16), p.astype(BF16))

    if has_new:
        @pl.when(c < n_chunks - 1)
        def _():
            step([r[...] for r in page_refs], c * pps)

        @pl.when(c == n_chunks - 1)
        def _():
            step([new_ref[...]], n_pg)
    else:
        step([r[...] for r in page_refs], c * pps)

    @pl.when(c == n_chunks - 1)
    def _():
        o_ref[...] = acc_ref[...] / l_ref[...] * gate_ref[...]


def _decode_attention(pages, table, new_rows, wq, bias_tab, bias_idx, notsel, gate):
    bsz, n_pg = table.shape
    w = pages.shape[2]
    has_new = new_rows is not None
    pps = math.gcd(n_pg, DEC_PAGES_PER_STEP)
    n_steps = n_pg // pps
    n_chunks = n_steps + (1 if has_new else 0)
    if not has_new:
        new_rows = jnp.zeros((1, LANES, w), F32)
    new_map = (lambda i, c, tab, bi: (i, 0, 0)) if has_new else (lambda i, c, tab, bi: (0, 0, 0))

    def page_spec(k):
        return pl.BlockSpec((None, LANES, w),
                            lambda i, c, tab, bi: (tab[i, jnp.minimum(c, n_steps - 1) * pps + k], 0, 0))

    grid_spec = pltpu.PrefetchScalarGridSpec(
        num_scalar_prefetch=2,
        grid=(bsz, n_chunks),
        in_specs=[page_spec(k) for k in range(pps)] + [
            pl.BlockSpec((None, LANES, w), new_map),
            pl.BlockSpec((None, w, LANES), lambda i, c, tab, bi: (i, 0, 0)),
            pl.BlockSpec(bias_tab.shape, lambda i, c, tab, bi: (0, 0, 0)),
            pl.BlockSpec((None, notsel.shape[1], LANES), lambda i, c, tab, bi: (i, 0, 0)),
            pl.BlockSpec((None, 1, LANES), lambda i, c, tab, bi: (i, 0, 0)),
        ],
        out_specs=pl.BlockSpec((None, w, LANES), lambda i, c, tab, bi: (i, 0, 0)),
        scratch_shapes=[pltpu.VMEM((w, LANES), F32), pltpu.VMEM((1, LANES), F32), pltpu.VMEM((1, LANES), F32)],
    )
    return pl.pallas_call(
        functools.partial(_dec_kernel, n_pg=n_pg, pps=pps, has_new=has_new),
        grid_spec=grid_spec,
        out_shape=jax.ShapeDtypeStruct((bsz, w, LANES), F32),
        compiler_params=_cparams("parallel", "arbitrary"),
        name="decode_attention",
    )(table, bias_idx, *([pages] * pps), new_rows, wq, bias_tab, notsel, gate)


def _tail_kernel(x_ref, g1_ref, op_ref, oc_ref, os_ref, ow_ref, om_ref, cnt0_ref, wgb_ref, wup_p_ref, wup_n_ref,
                 wup_m_ref, wout_ref, g2_ref, rwh_ref, rwl_ref, rb_ref, x2_ref, h2_ref, ei_ref, gt_ref, cnt_ref):
    @pl.when(pl.program_id(0) == 0)
    def _():
        cnt_ref[...] = cnt0_ref[...]

    x = x_ref[...]
    h = _rms(x, g1_ref[...]).astype(BF16)
    onsa = (oc_ref[...] + os_ref[...] + ow_ref[...]).astype(BF16)
    ups = (_dot(op_ref[...].astype(BF16), wup_p_ref[...]), _dot(onsa, wup_n_ref[...]),
           _dot(om_ref[...].astype(BF16), wup_m_ref[...]))
    mixed = None
    for br in range(3):
        gb = _sigmoid(_dot(h, wgb_ref[:, br * D_MODEL:(br + 1) * D_MODEL]))
        mixed = gb * ups[br] if mixed is None else mixed + gb * ups[br]
    x2 = x + _dot(mixed.astype(BF16), wout_ref[...])
    x2_ref[...] = x2
    h2 = _rms(x2, g2_ref[...])
    h2_ref[...] = h2
    hi = h2.astype(BF16)
    lo = (h2 - hi.astype(F32)).astype(BF16)
    logits = _dot(hi, rwh_ref[...]) + _dot(lo, rwh_ref[...]) + _dot(hi, rwl_ref[...]) + rb_ref[...]
    lane = _lane_iota(logits.shape)
    lanef = lane.astype(F32)
    tops, idxs = [], []
    for _ in range(TOP_K):
        m = jnp.max(logits, axis=1, keepdims=True)
        idx = jnp.min(jnp.where(logits == m, lanef, 1e9), axis=1, keepdims=True)
        logits = jnp.where(lanef == idx, -jnp.inf, logits)
        tops.append(m)
        idxs.append(idx)
    es = [jnp.exp(tk - tops[0]) for tk in tops]
    den = es[0] + es[1] + es[2] + es[3]
    tm = logits.shape[0]
    onehot = jnp.zeros(logits.shape, F32)
    for k in range(TOP_K):
        onehot = jnp.where(lanef == idxs[k], 1.0, onehot)
    tri = jnp.where(_row_iota((tm, tm)) > _lane_iota((tm, tm)), 1.0, 0.0).astype(BF16)
    before = _dot(tri, onehot.astype(BF16)) + cnt_ref[...]
    cnt_ref[...] = cnt_ref[...] + jnp.sum(onehot, axis=0, keepdims=True)
    ei = jnp.zeros(logits.shape, F32)
    gt = jnp.zeros(logits.shape, F32)
    for k in range(TOP_K):
        rank = jnp.sum(jnp.where(lanef == idxs[k], before, 0.0), axis=1, keepdims=True)
        ei = jnp.where(lane == k, idxs[k], ei)
        ei = jnp.where(lane == TOP_K + k, rank, ei)
        gt = jnp.where(lane == k, es[k] / den, gt)
    ei_ref[...] = ei.astype(jnp.int32)
    gt_ref[...] = gt


def _layer_tail(x, o_pool, o_cmp, o_slc, o_win, o_mem, cnt0, w, tm):
    n = x.shape[0]
    row = lambda wd: pl.BlockSpec((tm, wd), lambda i: (i, 0))
    full = lambda a: pl.BlockSpec(a.shape, lambda i: (0,) * a.ndim)
    weights = (w["wgb"], w["wup_pool"], w["wup_nsa"], w["wup_mem"], w["wout"], w["g2"], w["rw_hi"], w["rw_lo"],
               w["rb"])
    return pl.pallas_call(
        _tail_kernel,
        grid=(n // tm,),
        in_specs=[row(D_MODEL), full(w["g1"]), row(POOL_WIDTH), row(NSA_WIDTH), row(NSA_WIDTH), row(NSA_WIDTH),
                  row(MEM_WIDTH), full(cnt0)] + [full(a) for a in weights],
        out_specs=[row(D_MODEL), row(D_MODEL), row(LANES), row(LANES), full(cnt0)],
        out_shape=[jax.ShapeDtypeStruct((n, D_MODEL), F32), jax.ShapeDtypeStruct((n, D_MODEL), F32),
                   jax.ShapeDtypeStruct((n, LANES), jnp.int32), jax.ShapeDtypeStruct((n, LANES), F32),
                   jax.ShapeDtypeStruct((1, LANES), F32)],
        compiler_params=_cparams("arbitrary"),
        name="layer_tail",
    )(x, w["g1"], o_pool, o_cmp, o_slc, o_win, o_mem, cnt0, *weights)


def _ffn_kernel(be_ref, nu_ref, x_ref, wgu_ref, bgu_ref, wd_ref, bd_ref, o_ref):
    @pl.when(pl.program_id(0) < nu_ref[0])
    def _():
        gu = _dot(x_ref[...].astype(BF16), wgu_ref[...]) + bgu_ref[...]
        gate = jnp.minimum(gu[:, :D_FF], SWIGLU_LIMIT)
        up = jnp.clip(gu[:, D_FF:], -SWIGLU_LIMIT, SWIGLU_LIMIT)
        act = gate * _sigmoid(SWIGLU_ALPHA * gate) * (up + 1.0)
        o_ref[...] = _dot(act.astype(BF16), wd_ref[...]) + bd_ref[...]

    @pl.when(pl.program_id(0) >= nu_ref[0])
    def _():
        o_ref[...] = jnp.zeros_like(o_ref)


def _expert_ffn(rows, blk_e, n_used, wgu, bgu, wd, bd):
    n_rows = rows.shape[0]
    n_blocks = n_rows // MOE_TILE
    blk = lambda i, be, nu: (jnp.minimum(i, nu[0] - 1), 0)
    grid_spec = pltpu.PrefetchScalarGridSpec(
        num_scalar_prefetch=2,
        grid=(n_blocks,),
        in_specs=[
            pl.BlockSpec((MOE_TILE, D_MODEL), blk),
            pl.BlockSpec((None, D_MODEL, 2 * D_FF), lambda i, be, nu: (be[i], 0, 0)),
            pl.BlockSpec((None, 1, 2 * D_FF), lambda i, be, nu: (be[i], 0, 0)),
            pl.BlockSpec((None, D_FF, D_MODEL), lambda i, be, nu: (be[i], 0, 0)),
            pl.BlockSpec((None, 1, D_MODEL), lambda i, be, nu: (be[i], 0, 0)),
        ],
        out_specs=pl.BlockSpec((MOE_TILE, D_MODEL), lambda i, be, nu: (i, 0)),
    )
    return pl.pallas_call(
        _ffn_kernel,
        grid_spec=grid_spec,
        out_shape=jax.ShapeDtypeStruct((n_rows, D_MODEL), F32),
        compiler_params=_cparams("arbitrary"),
        name="expert_ffn",
    )(blk_e, n_used, rows, wgu, bgu, wd, bd)


def _token_tile(n):
    return next(t for t in (MOE_DMA_TOKENS, 384, 256, 128, n) if n % t == 0)


def _dispatch_kernel(dest_ref, h_ref, buf_ref, rows_ref, sem, *, td):
    del buf_ref

    def issue(t, carry):
        for k in range(TOP_K):
            pltpu.make_async_copy(h_ref.at[pl.ds(t, 1)], rows_ref.at[pl.ds(dest_ref[t * TOP_K + k], 1)], sem).start()
        return carry

    lax.fori_loop(0, td, issue, 0)
    for k in range(TOP_K):
        pltpu.make_async_copy(h_ref, rows_ref.at[pl.ds(0, td)], sem).wait()


def _moe_dispatch(rows_buf, h2, dest):
    n = h2.shape[0]
    td = _token_tile(n)
    return pl.pallas_call(
        functools.partial(_dispatch_kernel, td=td),
        grid=(n // td,),
        in_specs=[pl.BlockSpec((td * TOP_K,), lambda i: (i,), memory_space=pltpu.SMEM),
                  pl.BlockSpec((td, D_MODEL), lambda i: (i, 0)),
                  pl.BlockSpec(memory_space=pl.ANY)],
        out_specs=pl.BlockSpec(memory_space=pl.ANY),
        out_shape=jax.ShapeDtypeStruct(rows_buf.shape, rows_buf.dtype),
        scratch_shapes=[pltpu.SemaphoreType.DMA(())],
        input_output_aliases={2: 0},
        compiler_params=_cparams("arbitrary"),
        name="moe_dispatch",
    )(dest, h2, rows_buf)


def _combine_kernel(dest_ref, x2_ref, g_ref, rows_ref, o_ref, ybuf, sem, *, td):
    def issue(t, carry):
        for k in range(TOP_K):
            pltpu.make_async_copy(rows_ref.at[pl.ds(dest_ref[t * TOP_K + k], 1)], ybuf.at[k, pl.ds(t, 1)], sem).start()
        return carry

    lax.fori_loop(0, td, issue, 0)
    for k in range(TOP_K):
        pltpu.make_async_copy(rows_ref.at[pl.ds(0, td)], ybuf.at[k], sem).wait()
    out = x2_ref[...]
    for k in range(TOP_K):
        out = out + g_ref[:, k:k + 1] * ybuf[k]
    o_ref[...] = out


def _moe_combine(x2, out_rows, dest, gates):
    n = x2.shape[0]
    td = _token_tile(n)
    return pl.pallas_call(
        functools.partial(_combine_kernel, td=td),
        grid=(n // td,),
        in_specs=[pl.BlockSpec((td * TOP_K,), lambda i: (i,), memory_space=pltpu.SMEM),
                  pl.BlockSpec((td, D_MODEL), lambda i: (i, 0)),
                  pl.BlockSpec((td, LANES), lambda i: (i, 0)),
                  pl.BlockSpec(memory_space=pl.ANY)],
        out_specs=pl.BlockSpec((td, D_MODEL), lambda i: (i, 0)),
        out_shape=jax.ShapeDtypeStruct((n, D_MODEL), F32),
        scratch_shapes=[pltpu.VMEM((TOP_K, td, D_MODEL), F32), pltpu.SemaphoreType.DMA(())],
        compiler_params=_cparams("arbitrary"),
        name="moe_combine",
    )(dest, x2, gates, out_rows)


def _moe(groups, counts, w):
    n_total = sum(g[0].shape[0] for g in groups)
    cnt = counts[0, :N_EXPERTS].astype(jnp.int32)
    padded = (cnt + MOE_TILE - 1) // MOE_TILE * MOE_TILE
    pad_ends = jnp.cumsum(padded)
    pad_starts = pad_ends - padded
    n_blocks = -(-n_total * TOP_K // MOE_TILE) + N_EXPERTS
    blk_start = jnp.arange(n_blocks, dtype=jnp.int32) * MOE_TILE
    blk_e = jnp.minimum(jnp.sum(blk_start[:, None] >= pad_ends[None, :], axis=1), N_EXPERTS - 1).astype(jnp.int32)
    n_used = (pad_ends[-1] // MOE_TILE).astype(jnp.int32).reshape(1)
    dests = [(pad_starts[er[:, :TOP_K]] + er[:, TOP_K:2 * TOP_K]).astype(jnp.int32).reshape(-1)
             for (_, _, er, _) in groups]
    rows = jnp.zeros((n_blocks * MOE_TILE, D_MODEL), F32)
    for (_, h2, _, _), dest in zip(groups, dests):
        rows = _moe_dispatch(rows, h2, dest)
    out_rows = _expert_ffn(rows, blk_e, n_used, w["wgu"], w["bgu"], w["wd"], w["bd"])
    return [_moe_combine(x2, out_rows, dest, gates) for (x2, _, _, gates), dest in zip(groups, dests)]


def _rel_bucket(dist):
    n = jnp.maximum(dist, 0)
    max_exact = NUM_BUCKETS // 2
    nf = jnp.maximum(n, 1).astype(F32)
    large = max_exact + (jnp.log(nf / max_exact) / math.log(MAX_DISTANCE / max_exact)
                         * (NUM_BUCKETS - max_exact)).astype(jnp.int32)
    large = jnp.minimum(large, NUM_BUCKETS - 1)
    return jnp.where(n < max_exact, n, large)


def _bias_of(rel_bias, dist, valid):
    return jnp.where(valid[..., None], rel_bias[_rel_bucket(dist)], NEG)


_PROJ_SEGS = ((0, 512, "qscale"), (512, 256, "qscale"), (768, 256, "kv"), (1024, 256, "kv"), (1280, 256, "id"),
              (1536, 256, "id"), (1792, 128, "sigmoid"))
_PROJ_DTYPES = (BF16, BF16, F32, F32, F32, F32, F32)
_PROJ_NNORM = 1280


def _prep_layer(l, rel_bias, norm1_g, w_in, nsa_qk_norm, mem_qk_norm, cmp_w, cmp_pe, pool_w, pool_scale,
                mem_norm_g, w_mem_kv, w_up_pool, w_up_nsa, w_up_mem, w_out, norm2_g, router_w, router_b,
                w_gu, b_gu, w_down, b_down):
    wi = w_in[l]
    o_u, o_q, o_qm, o_kvc, o_kvs, o_kvw, o_gn, o_gb = 0, 256, 768, 1024, 1280, 1536, 1792, 1816
    w_proj = jnp.concatenate([
        wi[:, o_q:o_q + 512], wi[:, o_qm:o_qm + 256], wi[:, o_kvs:o_kvs + 256], wi[:, o_kvw:o_kvw + 256],
        wi[:, o_kvc:o_kvc + 256], wi[:, o_u:o_u + 256], wi[:, o_gn:o_gn + 24],
        jnp.zeros((D_MODEL, LANES - 24), F32)], axis=1).astype(BF16)
    nq, mq = nsa_qk_norm[l], mem_qk_norm[l]
    ones = jnp.ones((LANES,), F32)
    gain = jnp.concatenate([jnp.tile(nq[0], 8), jnp.tile(mq[0], 4), jnp.tile(nq[2], 2), ones,
                            jnp.tile(nq[3], 2), ones])[None, :]
    nmask = jnp.concatenate([jnp.ones((768,), F32), ones, 0 * ones, ones, 0 * ones])[None, :]
    eye4 = jnp.eye(4, dtype=F32)
    cw = cmp_w[l].reshape(2, 2, CMP_STRIDE, HEAD_DIM, HEAD_DIM)
    w_c = jnp.einsum("crjde,xy->cjxdrye", cw, jnp.eye(2, dtype=F32))
    w_c = w_c.reshape(2, CMP_STRIDE, LANES, KV_WIDTH).astype(BF16)
    pe = cmp_pe[l].reshape(2, 2, CMP_STRIDE, HEAD_DIM)
    pe_c = jnp.tile(pe.transpose(0, 2, 1, 3), (1, 1, 1, NSA_KV_HEADS))
    pe_c = jnp.pad(pe_c, ((0, 0), (0, 0), (0, 6), (0, 0)))
    w_pool = jnp.einsum("gde,gh->gdhe", pool_w[l], eye4).reshape(POOL_WIDTH, POOL_WIDTH).astype(BF16)
    rw = jnp.pad(router_w[l], ((0, 0), (0, LANES - N_EXPERTS)))
    rw_hi = rw.astype(BF16)
    rw_lo = (rw - rw_hi.astype(F32)).astype(BF16)
    rb = jnp.concatenate([router_b[l], jnp.full((LANES - N_EXPERTS,), NEG, F32)])[None, :]
    return {
        "g1": norm1_g[l][None, :], "w_proj": w_proj, "gain": gain, "nmask": nmask,
        "gk_cmp": jnp.tile(nq[1], 2)[None, :], "w_c": w_c, "pe_c": pe_c,
        "w_pool": w_pool, "pool_scale": pool_scale[l][None, :],
        "mem_g": mem_norm_g[l][None, :], "w_mem": w_mem_kv[l].astype(BF16),
        "mem_gain": jnp.concatenate([jnp.tile(mq[1], 4), jnp.ones((256,), F32)])[None, :],
        "mem_nmask": jnp.concatenate([jnp.ones((256,), F32), jnp.zeros((256,), F32)])[None, :],
        "wgb": wi[:, o_gb:o_gb + 3 * D_MODEL].astype(BF16),
        "wup_pool": w_up_pool[l].astype(BF16), "wup_nsa": w_up_nsa[l].astype(BF16),
        "wup_mem": w_up_mem[l].astype(BF16), "wout": w_out[l].astype(BF16), "g2": norm2_g[l][None, :],
        "rw_hi": rw_hi, "rw_lo": rw_lo, "rb": rb,
        "wgu": w_gu[l].astype(BF16), "bgu": b_gu[l][:, None, :], "wd": w_down[l].astype(BF16),
        "bd": b_down[l][:, None, :],
    }


def _project_in(x2d, w, tm):
    return _project(x2d, w["g1"], w["w_proj"], w["gain"], w["nmask"], _PROJ_SEGS, _PROJ_DTYPES, _PROJ_NNORM, tm)


def _flash_tables(rel_bias):
    t = ATT_TILE
    i = jnp.arange(t)[:, None]
    j = jnp.arange(t)[None, :]
    d0 = i - j
    kinds = [
        _bias_of(rel_bias, d0, d0 >= 0),
        _bias_of(rel_bias, d0 + t, jnp.ones((t, t), bool)),
        _bias_of(rel_bias, jnp.full((t, t), 2 * t), jnp.ones((t, t), bool)),
        _bias_of(rel_bias, d0 + WINDOW, d0 + WINDOW < WINDOW),
    ]
    tab = jnp.stack(kinds)
    tab = tab.reshape(4, t, t, NSA_KV_HEADS, NSA_HPG)
    return tab.transpose(3, 0, 2, 4, 1).reshape(NSA_KV_HEADS, 4, t, NSA_HPG * t)


def _prompt_pre(x, mem, w, rel_bias, cnt0):
    b, s, _ = x.shape
    n = b * s
    tm = 512 if n % 512 == 0 else ATT_TILE
    x2d = x.reshape(n, D_MODEL)
    q, qm, kvs, vts, kvw, vtw, kvc, u, gn = _project_in(x2d, w, tm)
    r3 = lambda a: a.reshape(b, s, a.shape[-1])
    q, qm, kvs, kvw, kvc, u, gn = map(r3, (q, qm, kvs, kvw, kvc, u, gn))

    o_pool = _pool_mix(u, jnp.zeros((b, 16, POOL_WIDTH), F32), w["w_pool"], w["pool_scale"], 0)

    n_cmp = (s - CMP_BLOCK) // CMP_STRIDE + 1
    n_slc = -(-s // SLC_BLOCK)
    parts = _cmp_partials_dense(kvc, w["pe_c"], w["w_c"])
    n_sub = parts.shape[1]
    tpos = jnp.arange(s)[:, None]
    end = jnp.arange(n_sub)[None, :] * CMP_STRIDE + CMP_BLOCK - 1
    bias_c = _bias_of(rel_bias, tpos - end, (end <= tpos) & (jnp.arange(n_sub)[None, :] < n_cmp))
    bias_c = bias_c.transpose(2, 0, 1)
    tq = 256 if s % 256 == 0 else ATT_TILE
    o_cmp, ns0, ns1 = _cmp_attention(q, parts, bias_c, gn, w["gk_cmp"], tq=tq, n_cmp=n_cmp, n_slc=n_slc, pos0=0)

    tab = _flash_tables(rel_bias)
    o_slc = _flash_attention(q, ns0, ns1, kvs, vts, tab, gn, use_sel=True, band=None, gate_base=NSA_HEADS)
    o_win = _flash_attention(q, ns0, ns1, kvw, vtw, tab, gn, use_sel=False, band=WINDOW // ATT_TILE,
                             gate_base=2 * NSA_HEADS)

    m = mem.shape[1]
    (mem_kv,) = _project(mem.reshape(b * m, D_MODEL), w["mem_g"], w["w_mem"], w["mem_gain"], w["mem_nmask"],
                         ((0, 2 * MEM_WIDTH, "id"),), (F32,), MXU_DIM, tm=min(512, b * m))
    mem_kv = mem_kv.reshape(b, m, 2 * MEM_WIDTH)
    o_mem = _mem_attention(qm, mem_kv, tq=min(512, s))

    f2 = lambda a: a.reshape(n, a.shape[-1])
    x2, h2, eidx, gates, cnt = _layer_tail(x2d, f2(o_pool), f2(o_cmp), f2(o_slc), f2(o_win), f2(o_mem), cnt0, w, tm)
    kvshape = (b, s, 2, NSA_KV_HEADS, HEAD_DIM)
    win_buf = WINDOW
    states = (kvc.reshape(kvshape), kvs.reshape(kvshape),
              _last_rows(kvw, win_buf).reshape(b, win_buf, 2, NSA_KV_HEADS, HEAD_DIM),
              mem_kv.reshape(b, m, 2, MEM_HEADS, HEAD_DIM), _last_rows(u, POOL_BUF))
    return (x2, h2, eidx, gates), states, cnt


def _last_rows(a, n):
    t = a.shape[1]
    if t < n:
        a = jnp.pad(a, [(0, 0), (n - t, 0)] + [(0, 0)] * (a.ndim - 2))
    return a[:, a.shape[1] - n:]


def _dec_columns_nsa(q):
    b, t, _ = q.shape
    qh = q.reshape(b, t, NSA_KV_HEADS, NSA_HPG, HEAD_DIM)
    w = jnp.einsum("btgpd,gx->bxdgtp", qh.astype(F32), jnp.eye(NSA_KV_HEADS, dtype=F32))
    w = w.reshape(b, NSA_KV_HEADS * HEAD_DIM, NSA_KV_HEADS * t * NSA_HPG)
    return jnp.pad(w, ((0, 0), (0, KV_WIDTH - w.shape[1]), (0, LANES - w.shape[2]))).astype(BF16)


def _dec_extract_nsa(o, t):
    b = o.shape[0]
    v = o[:, LANES:, :NSA_KV_HEADS * t * NSA_HPG]
    v = v.reshape(b, NSA_KV_HEADS, HEAD_DIM, NSA_KV_HEADS, t, NSA_HPG)
    v = jnp.einsum("bxdgtp,gx->btgpd", v, jnp.eye(NSA_KV_HEADS, dtype=F32))
    return v.reshape(b, t, NSA_WIDTH)


def _dec_bias_cols(bias_tph):
    k, t, _ = bias_tph.shape
    bt = bias_tph.reshape(k, t, NSA_KV_HEADS, NSA_HPG).transpose(0, 2, 1, 3).reshape(k, NSA_KV_HEADS * t * NSA_HPG)
    return jnp.pad(bt, ((0, 0), (0, LANES - bt.shape[1])))


def _dec_gate_cols(gn, base, t):
    b = gn.shape[0]
    gt = gn[:, :, base:base + NSA_HEADS].reshape(b, t, NSA_KV_HEADS, NSA_HPG).transpose(0, 2, 1, 3)
    gt = gt.reshape(b, 1, NSA_KV_HEADS * t * NSA_HPG)
    return jnp.pad(gt, ((0, 0), (0, 0), (0, LANES - gt.shape[2])), constant_values=1.0)


def _sample_pre(x, cache_cmp, cache_slc, cache_win, cache_mem, pool_buf, page_table, w, rel_bias, cnt0):
    b, t, _ = x.shape
    n = b * t
    page = cache_cmp.shape[1]
    n_pages = page_table.shape[1]
    past = n_pages * page
    x2d = x.reshape(n, D_MODEL)
    q, qm, kvs, _, kvw, _, kvc, u, gn = _project_in(x2d, w, n if n <= 512 else ATT_TILE)
    r3 = lambda a: a.reshape(b, t, a.shape[-1])
    q, qm, kvs, kvw, kvc, u, gn = map(r3, (q, qm, kvs, kvw, kvc, u, gn))
    qpos = past + jnp.arange(t)

    buf16 = jnp.pad(pool_buf, ((0, 0), (16 - POOL_BUF, 0), (0, 0)))
    o_pool = _pool_mix(u, buf16, w["w_pool"], w["pool_scale"], past)

    total = past + t
    n_cmp = (total - CMP_BLOCK) // CMP_STRIDE + 1
    n_sub_used = n_cmp + CMP_BLOCK // CMP_STRIDE - 1
    n_slc = -(-total // SLC_BLOCK)
    pool_c = cache_cmp.reshape(cache_cmp.shape[0], page, KV_WIDTH)
    pps = math.gcd(n_pages, 16)
    parts = _cmp_partials_paged(pool_c, page_table, w["pe_c"], w["w_c"], pps)
    extra = n_sub_used * CMP_STRIDE - past
    if extra > 0:
        tail_rows = -(-extra // CMP_STRIDE) * CMP_STRIDE
        new_c = jnp.pad(kvc, ((0, 0), (0, max(0, tail_rows - t)), (0, 0)))[:, :tail_rows]
        parts = jnp.concatenate([parts, _cmp_partials_dense(new_c, w["pe_c"], w["w_c"])], axis=1)
    n_sub = parts.shape[1]
    end = jnp.arange(n_sub)[None, :] * CMP_STRIDE + CMP_BLOCK - 1
    bias_c = _bias_of(rel_bias, qpos[:, None] - end, (end <= qpos[:, None]) & (jnp.arange(n_sub)[None, :] < n_cmp))
    bias_c = bias_c.transpose(2, 0, 1)
    o_cmp, ns0, ns1 = _cmp_attention(q, parts, bias_c, gn, w["gk_cmp"], tq=t, n_cmp=n_cmp, n_slc=n_slc, pos0=past)

    wq = _dec_columns_nsa(q)
    ncol = NSA_KV_HEADS * t * NSA_HPG

    n_chunks = n_pages + 1
    nblk = -(-2 * n_chunks // 8) * 8
    ns = jnp.stack([ns0, ns1], axis=1)
    ns = jnp.pad(ns, ((0, 0), (0, 0), (0, 0), (0, max(0, nblk - ns.shape[3]))))[..., :nblk]
    ns = jnp.broadcast_to(ns[:, :, :, None, :], (b, NSA_KV_HEADS, t, NSA_HPG, nblk)).reshape(b, ncol, nblk)
    notsel = jnp.pad(ns.transpose(0, 2, 1), ((0, 0), (0, 0), (0, LANES - ncol)))
    rows = jnp.arange(LANES)
    far = _bias_of(rel_bias, jnp.full((LANES, t), 2 * MAX_DISTANCE), jnp.ones((LANES, t), bool))
    kpos_last = past - LANES + rows
    d_last = qpos[None, :] - kpos_last[:, None]
    near = _bias_of(rel_bias, d_last, d_last >= 0)
    kpos_new = past + rows
    d_new = qpos[None, :] - kpos_new[:, None]
    newb = _bias_of(rel_bias, d_new, (d_new >= 0) & (rows[:, None] < t))
    bias_tab = jnp.stack([_dec_bias_cols(far), _dec_bias_cols(near), _dec_bias_cols(newb)])
    bias_idx = jnp.concatenate([jnp.zeros((n_pages - 1,), jnp.int32), jnp.array([1, 2], jnp.int32)])
    pool_s = cache_slc.reshape(cache_slc.shape[0], page, KV_WIDTH)
    new_s = jnp.pad(kvs, ((0, 0), (0, LANES - t), (0, 0)))
    o_slc = _decode_attention(pool_s, page_table, new_s, wq, bias_tab, bias_idx, notsel,
                              _dec_gate_cols(gn, NSA_HEADS, t))
    o_slc = _dec_extract_nsa(o_slc, t)

    wb = cache_win.shape[1]
    n_wpg = wb // LANES
    win_pages = cache_win.reshape(b * n_wpg, LANES, KV_WIDTH)
    win_table = (jnp.arange(b)[:, None] * n_wpg + jnp.arange(n_wpg)[None, :]).astype(jnp.int32)
    kpos_w = past - wb + jnp.arange(wb + LANES)
    d_w = qpos[None, :] - kpos_w[:, None]
    valid_w = (d_w >= 0) & (d_w < WINDOW) & (kpos_w[:, None] >= 0) & (jnp.arange(wb + LANES)[:, None] < wb + t)
    bias_w = _dec_bias_cols(_bias_of(rel_bias, d_w, valid_w)).reshape(n_wpg + 1, LANES, LANES)
    zeros_ns = jnp.zeros((b, -(-2 * (n_wpg + 1) // 8) * 8, LANES), F32)
    new_w = jnp.pad(kvw, ((0, 0), (0, LANES - t), (0, 0)))
    o_win = _decode_attention(win_pages, win_table, new_w, wq, bias_w, jnp.arange(n_wpg + 1, dtype=jnp.int32),
                              zeros_ns, _dec_gate_cols(gn, 2 * NSA_HEADS, t))
    o_win = _dec_extract_nsa(o_win, t)

    m = cache_mem.shape[1]
    n_mpg = m // LANES
    mem_pages = cache_mem.reshape(b * n_mpg, LANES, 2 * MEM_WIDTH)
    mem_table = (jnp.arange(b)[:, None] * n_mpg + jnp.arange(n_mpg)[None, :]).astype(jnp.int32)
    qmh = qm.reshape(b, t, MEM_HEADS, HEAD_DIM).astype(F32)
    wqm = jnp.einsum("bthd,hx->bxdht", qmh, jnp.eye(MEM_HEADS, dtype=F32))
    wqm = wqm.reshape(b, MEM_WIDTH, MEM_HEADS * t)
    wqm = jnp.pad(wqm, ((0, 0), (0, MEM_WIDTH), (0, LANES - MEM_HEADS * t))).astype(BF16)
    o_mem = _decode_attention(mem_pages, mem_table, None, wqm, jnp.zeros((1, LANES, LANES), F32),
                              jnp.zeros((n_mpg,), jnp.int32), jnp.zeros((b, 8, LANES), F32),
                              jnp.ones((b, 1, LANES), F32))
    om = o_mem[:, MEM_WIDTH:, :MEM_HEADS * t].reshape(b, MEM_HEADS, HEAD_DIM, MEM_HEADS, t)
    o_mem = jnp.einsum("bxdht,hx->bthd", om, jnp.eye(MEM_HEADS, dtype=F32)).reshape(b, t, MEM_WIDTH)

    f2 = lambda a: a.reshape(n, a.shape[-1])
    x2, h2, eidx, gates, cnt = _layer_tail(x2d, f2(o_pool), f2(o_cmp), f2(o_slc), f2(o_win), f2(o_mem), cnt0, w,
                                           n if n <= 512 else LANES)
    kvshape = (b, t, 2, NSA_KV_HEADS, HEAD_DIM)
    new_win = jnp.concatenate([cache_win, kvw], axis=1)[:, t:]
    new_pool = jnp.concatenate([pool_buf, u], axis=1)[:, t:]
    states = (kvc.reshape(kvshape), kvs.reshape(kvshape), new_win, new_pool)
    return (x2, h2, eidx, gates), states, cnt


def kernel(x_prompt, x_sample, cache_cmp_kv, cache_slc_kv, cache_win_kv, cache_mem_kv, state_pool, page_table,
           mem_prompt, rel_bias, norm1_g, w_in, nsa_qk_norm, mem_qk_norm, cmp_w, cmp_pe, pool_w, pool_scale,
           mem_norm_g, w_mem_kv, w_up_pool, w_up_nsa, w_up_mem, w_out, norm2_g, router_w, router_b, w_gu, b_gu,
           w_down, b_down):
    depth = w_in.shape[0]
    yp, ys = x_prompt, x_sample
    bp, sp, _ = x_prompt.shape
    bs, ts, _ = x_sample.shape
    outs_p = [[] for _ in range(5)]
    outs_s = [[] for _ in range(4)]
    for l in range(depth):
        w = _prep_layer(l, rel_bias, norm1_g, w_in, nsa_qk_norm, mem_qk_norm, cmp_w, cmp_pe, pool_w, pool_scale,
                        mem_norm_g, w_mem_kv, w_up_pool, w_up_nsa, w_up_mem, w_out, norm2_g, router_w, router_b,
                        w_gu, b_gu, w_down, b_down)
        pre_p, st_p, cnt = _prompt_pre(yp, mem_prompt, w, rel_bias, jnp.zeros((1, LANES), F32))
        cw = cache_win_kv[l]
        pre_s, st_s, cnt = _sample_pre(ys, cache_cmp_kv[l], cache_slc_kv[l],
                                       cw.reshape(cw.shape[0], cw.shape[1], KV_WIDTH),
                                       cache_mem_kv[l].reshape(bs, cache_mem_kv.shape[2], 2 * MEM_WIDTH),
                                       state_pool[l], page_table, w, rel_bias, cnt)
        yp, ys = _moe([pre_p, pre_s], cnt, w)
        yp = yp.reshape(bp, sp, D_MODEL)
        ys = ys.reshape(bs, ts, D_MODEL)
        for lst, a in zip(outs_p, st_p):
            lst.append(a)
        for lst, a in zip(outs_s, st_s):
            lst.append(a)
    new_cmp_p, new_slc_p, new_win_p, new_mem_p, new_pool_p = [jnp.stack(a) for a in outs_p]
    new_cmp_s, new_slc_s, new_win_s, new_pool_s = [jnp.stack(a) for a in outs_s]
    new_win_s = new_win_s.reshape(new_win_s.shape[:3] + (2, NSA_KV_HEADS, HEAD_DIM))
    return (yp, ys, new_cmp_p, new_slc_p, new_win_p, new_mem_p, new_pool_p,
            new_cmp_s, new_slc_s, new_win_s, new_pool_s)
```

```python
import functools
import math

import jax
import jax.numpy as jnp
from jax import lax
from jax.experimental import pallas as pl
from jax.experimental.pallas import tpu as pltpu

F32 = jnp.float32
BF16 = jnp.bfloat16

D_MODEL = 1024
HEAD_DIM = 64
POOL_WINDOWS = (2, 4, 8, 16)
POOL_GROUP = 64
POOL_WIDTH = 256
POOL_BUF = 15
NSA_HEADS = 8
NSA_KV_HEADS = 2
NSA_HPG = 4
NSA_WIDTH = 512
KV_WIDTH = 256
CMP_BLOCK = 32
CMP_STRIDE = 16
SLC_BLOCK = 64
SLC_TOPK = 16
WINDOW = 512
MEM_HEADS = 4
MEM_WIDTH = 256
NUM_BUCKETS = 32
MAX_DISTANCE = 128
N_EXPERTS = 32
TOP_K = 4
D_FF = 1024
SWIGLU_ALPHA = 1.702
SWIGLU_LIMIT = 7.0
EPS = 1e-6
SCALE = HEAD_DIM ** -0.5

LANES = 128
MXU_DIM = 256
NEG = -1e30
ATT_TILE = 256
DEC_PAGES_PER_STEP = 8
MOE_TILE = 256
MOE_DMA_TOKENS = 512
VMEM_LIMIT = 48 * 1024 * 1024


def _cparams(*sem):
    return pltpu.CompilerParams(dimension_semantics=sem, vmem_limit_bytes=VMEM_LIMIT)


def _dot(a, b):
    return jnp.dot(a, b, preferred_element_type=F32)


def _dot_nt(a, b):
    return lax.dot_general(a, b, (((1,), (1,)), ((), ())), preferred_element_type=F32)


def _split_dot(a, b):
    hi = a.astype(BF16)
    lo = (a - hi.astype(F32)).astype(BF16)
    return _dot(hi, b) + _dot(lo, b)


def _rms(x, g):
    r = lax.rsqrt(jnp.mean(x * x, axis=-1, keepdims=True) + EPS)
    return (x * r) * g


def _sigmoid(x):
    return 1.0 / (1.0 + jnp.exp(-x))


def _lane_iota(shape):
    return lax.broadcasted_iota(jnp.int32, shape, len(shape) - 1)


def _row_iota(shape):
    return lax.broadcasted_iota(jnp.int32, shape, len(shape) - 2)


def _proj_kernel(x_ref, g_ref, w_ref, gain_ref, nmask_ref, seg_ref, *out_refs, segs, n_norm):
    h = _rms(x_ref[...], g_ref[...]).astype(BF16)
    seg = seg_ref[...]
    outs = iter(out_refs)
    for (start, width, kind) in segs:
        o_ref = next(outs)
        z = _dot(h, w_ref[:, start:start + width])
        if start < n_norm:
            pieces = []
            for c in range(0, width, MXU_DIM):
                zc = z[:, c:c + MXU_DIM]
                ms = _split_dot(zc * zc, seg)
                zn = (zc * lax.rsqrt(ms + EPS)) * gain_ref[:, start + c:start + c + MXU_DIM]
                pieces.append(jnp.where(nmask_ref[:, start + c:start + c + MXU_DIM] > 0, zn, zc))
            z = pieces[0] if len(pieces) == 1 else jnp.concatenate(pieces, axis=1)
        if kind == "sigmoid":
            z = _sigmoid(z)
        elif kind == "qscale":
            z = z * SCALE
        o_ref[...] = z.astype(o_ref.dtype)
        if kind == "kv":
            next(outs)[...] = z[:, LANES:2 * LANES].T


def _project(x, g, w, gain, nmask, segs, out_dtypes, n_norm, tm):
    n = x.shape[0]
    ncol = w.shape[1]
    seg = _seg_matrix(MXU_DIM)
    full = lambda i: (0, 0)
    out_specs, out_shape = [], []
    for (_, wd, kind), dt in zip(segs, out_dtypes):
        out_specs.append(pl.BlockSpec((tm, wd), lambda i: (i, 0)))
        out_shape.append(jax.ShapeDtypeStruct((n, wd), dt))
        if kind == "kv":
            out_specs.append(pl.BlockSpec((LANES, tm), lambda i: (0, i)))
            out_shape.append(jax.ShapeDtypeStruct((LANES, n), F32))
    return pl.pallas_call(
        functools.partial(_proj_kernel, segs=segs, n_norm=n_norm),
        grid=(n // tm,),
        in_specs=[
            pl.BlockSpec((tm, D_MODEL), lambda i: (i, 0)),
            pl.BlockSpec((1, D_MODEL), full),
            pl.BlockSpec((D_MODEL, ncol), full),
            pl.BlockSpec((1, gain.shape[1]), full),
            pl.BlockSpec((1, nmask.shape[1]), full),
            pl.BlockSpec((MXU_DIM, MXU_DIM), full),
        ],
        out_specs=out_specs,
        out_shape=out_shape,
        compiler_params=_cparams("parallel"),
        name="proj",
    )(x, g, w, gain, nmask, seg)


def _seg_matrix(n):
    i = jnp.arange(n) // HEAD_DIM
    return jnp.where(i[:, None] == i[None, :], 1.0 / HEAD_DIM, 0.0).astype(BF16)


def _pool_kernel(u_ref, buf_ref, w_ref, scale_ref, o_ref, zs_ref, *, t, pos0):
    zs_ref[0:16, :] = buf_ref[...]
    zs_ref[16:16 + t, :] = u_ref[...]
    u = u_ref[...]
    lane = _lane_iota((1, POOL_WIDTH))
    pos = (pos0 + _row_iota((t, 1))).astype(F32)
    acc = u
    mean = None
    for i in range(1, max(POOL_WINDOWS)):
        acc = acc + zs_ref[16 - i:16 - i + t, :]
        if i + 1 in POOL_WINDOWS:
            gi = POOL_WINDOWS.index(i + 1)
            m = acc / jnp.minimum(pos + 1.0, float(i + 1))
            mean = m if mean is None else jnp.where(lane >= gi * POOL_GROUP, m, mean)
    d = (mean - u).astype(BF16)
    o_ref[...] = _dot(d, w_ref[...]) * scale_ref[...]


def _pool_mix(u, buf16, w_bd, scale, pos0):
    b, t, _ = u.shape
    return pl.pallas_call(
        functools.partial(_pool_kernel, t=t, pos0=pos0),
        grid=(b,),
        in_specs=[
            pl.BlockSpec((None, t, POOL_WIDTH), lambda i: (i, 0, 0)),
            pl.BlockSpec((None, 16, POOL_WIDTH), lambda i: (i, 0, 0)),
            pl.BlockSpec((POOL_WIDTH, POOL_WIDTH), lambda i: (0, 0)),
            pl.BlockSpec((1, POOL_WIDTH), lambda i: (0, 0)),
        ],
        out_specs=pl.BlockSpec((None, t, POOL_WIDTH), lambda i: (i, 0, 0)),
        out_shape=jax.ShapeDtypeStruct((b, t, POOL_WIDTH), F32),
        scratch_shapes=[pltpu.VMEM((t + 16, POOL_WIDTH), F32)],
        compiler_params=_cparams("parallel"),
        name="pool",
    )(u, buf16, w_bd, scale)


def _cpart_kernel(*refs, nop, rows_per):
    x_refs = refs[-(2 * nop + 3):-3]
    pe_ref, w_ref, o_ref = refs[-3:]
    n = rows_per // CMP_STRIDE
    m = nop * n
    for c in range(2):
        acc = jnp.zeros((m + 8, KV_WIDTH), F32)
        for j in range(CMP_STRIDE):
            parts = [r[pl.ds(j, n, stride=CMP_STRIDE), :] for r in x_refs[c * nop:(c + 1) * nop]]
            lhs = jnp.concatenate(parts + [pe_ref[c, j]], axis=0).astype(BF16)
            acc = acc + _dot(lhs, w_ref[c, j])
        lane = _lane_iota((1, KV_WIDTH))
        pe_term = jnp.where(lane < LANES, acc[m:m + 1], acc[m + 1:m + 2])
        o_ref[:, c * KV_WIDTH:(c + 1) * KV_WIDTH] = acc[0:m] + pe_term


def _cmp_partials_dense(kv, pe, w_c):
    b, t, _ = kv.shape
    rows = (t // CMP_STRIDE) * CMP_STRIDE
    n = rows // CMP_STRIDE
    return pl.pallas_call(
        functools.partial(_cpart_kernel, nop=1, rows_per=rows),
        grid=(b,),
        in_specs=[
            pl.BlockSpec((None, rows, LANES), lambda i: (i, 0, 0)),
            pl.BlockSpec((None, rows, LANES), lambda i: (i, 0, 1)),
            pl.BlockSpec(pe.shape, lambda i: (0, 0, 0, 0)),
            pl.BlockSpec(w_c.shape, lambda i: (0, 0, 0, 0)),
        ],
        out_specs=pl.BlockSpec((None, n, 2 * KV_WIDTH), lambda i: (i, 0, 0)),
        out_shape=jax.ShapeDtypeStruct((b, n, 2 * KV_WIDTH), F32),
        compiler_params=_cparams("parallel"),
        name="cmp_partials",
    )(kv, kv, pe, w_c)


def _cmp_partials_paged(pool, page_table, pe, w_c, pages_per_step):
    b, n_pages = page_table.shape
    page = pool.shape[1]
    nop = pages_per_step
    n = nop * page // CMP_STRIDE

    def page_spec(k, half):
        return pl.BlockSpec((None, page, LANES), lambda i, c, pt: (pt[i, c * nop + k], 0, half))

    grid_spec = pltpu.PrefetchScalarGridSpec(
        num_scalar_prefetch=1,
        grid=(b, n_pages // nop),
        in_specs=[page_spec(k, half) for half in range(2) for k in range(nop)] + [
            pl.BlockSpec(pe.shape, lambda i, c, pt: (0, 0, 0, 0)),
            pl.BlockSpec(w_c.shape, lambda i, c, pt: (0, 0, 0, 0)),
        ],
        out_specs=pl.BlockSpec((None, n, 2 * KV_WIDTH), lambda i, c, pt: (i, c, 0)),
    )
    return pl.pallas_call(
        functools.partial(_cpart_kernel, nop=nop, rows_per=page),
        grid_spec=grid_spec,
        out_shape=jax.ShapeDtypeStruct((b, n_pages * page // CMP_STRIDE, 2 * KV_WIDTH), F32),
        compiler_params=_cparams("parallel", "arbitrary"),
        name="cmp_partials_paged",
    )(page_table, *([pool] * (2 * nop)), pe, w_c)


def _pair_operands(x128, g):
    lane = _lane_iota(x128.shape)
    if g == 0:
        lo = jnp.where(lane < HEAD_DIM, x128, 0.0)
        hi = pltpu.roll(lo, HEAD_DIM, 1)
    else:
        hi = jnp.where(lane >= HEAD_DIM, x128, 0.0)
        lo = pltpu.roll(hi, HEAD_DIM, 1)
    return lo.astype(BF16), hi.astype(BF16)


def _cattn_kernel(q_ref, p_ref, bias_ref, gn_ref, gk_ref, seg_ref, o_ref, ns0_ref, ns1_ref, *,
                  tq, n_sub, n_cmp, n_slc, nslp, pos0):
    qi = pl.program_id(0)
    pall = p_ref[...]
    kraw = pall[:, 0:LANES] + pltpu.roll(pall[:, LANES:2 * LANES], n_sub - 1, 0)
    vc = pall[:, 2 * LANES:3 * LANES] + pltpu.roll(pall[:, 3 * LANES:4 * LANES], n_sub - 1, 0)
    ms = _split_dot(kraw * kraw, seg_ref[...])
    kc = (kraw * lax.rsqrt(ms + EPS)) * gk_ref[...]

    qpos = pos0 + qi * tq + _row_iota((tq, 1))
    qblk = jnp.right_shift(qpos, SLC_BLOCK.bit_length() - 1)
    nn = _row_iota((n_sub, nslp))
    jj = _lane_iota((n_sub, nslp))
    covers = ((nn * CMP_STRIDE < (jj + 1) * SLC_BLOCK) & (nn * CMP_STRIDE + CMP_BLOCK - 1 >= jj * SLC_BLOCK)
              & (nn < n_cmp) & (jj < n_slc))
    covers = jnp.where(covers, 1.0, 0.0).astype(BF16)
    jl = _lane_iota((tq, nslp))
    jlf = jl.astype(F32)
    forced = (jl == 0) | (jl == qblk) | (jl == qblk - 1)
    causal = jl <= qblk

    for g in range(NSA_KV_HEADS):
        k_lo, k_hi = _pair_operands(kc, g)
        v_lo, v_hi = _pair_operands(vc, g)
        prsum = jnp.zeros((tq, n_sub), F32)
        for pr in range(NSA_HPG // 2):
            qpair = q_ref[:, (2 * g + pr) * LANES:(2 * g + pr + 1) * LANES]
            outs = []
            for half, (kk, vv) in enumerate(((k_lo, v_lo), (k_hi, v_hi))):
                h = NSA_HPG * g + 2 * pr + half
                bias = bias_ref[h]
                mask = bias > 0.5 * NEG
                s = _dot_nt(qpair, kk) + bias
                m = jnp.max(s, axis=1, keepdims=True)
                m = jnp.where(m > 0.5 * NEG, m, 0.0)
                e = jnp.where(mask, jnp.exp(s - m), 0.0)
                p = e / jnp.maximum(jnp.sum(e, axis=1, keepdims=True), 1e-30)
                prsum = prsum + p
                outs.append(_dot(p.astype(BF16), vv) * gn_ref[:, h:h + 1])
            o_ref[:, (2 * g + pr) * LANES:(2 * g + pr + 1) * LANES] = outs[0] + outs[1]
        imp = _split_dot(prsum, covers)
        score = jnp.where(forced, jnp.inf, imp)
        score = jnp.where(causal, score, -jnp.inf)
        sel = jnp.zeros((tq, nslp), F32)
        for _ in range(min(SLC_TOPK, n_slc)):
            m = jnp.max(score, axis=1, keepdims=True)
            idx = jnp.min(jnp.where(score == m, jlf, 1e9), axis=1, keepdims=True)
            pick = jlf == idx
            sel = jnp.where(pick & (m > -jnp.inf), 1.0, sel)
            score = jnp.where(pick, -jnp.inf, score)
        (ns0_ref if g == 0 else ns1_ref)[...] = 1.0 - sel


def _cmp_attention(q, parts, bias, gn, gk, *, tq, n_cmp, n_slc, pos0):
    b, s, _ = q.shape
    n_sub = parts.shape[1]
    nslp = -(-n_slc // LANES) * LANES
    kern = functools.partial(_cattn_kernel, tq=tq, n_sub=n_sub, n_cmp=n_cmp, n_slc=n_slc, nslp=nslp, pos0=pos0)
    return pl.pallas_call(
        kern,
        grid=(s // tq, b),
        in_specs=[
            pl.BlockSpec((None, tq, NSA_WIDTH), lambda i, j: (j, i, 0)),
            pl.BlockSpec((None, n_sub, 2 * KV_WIDTH), lambda i, j: (j, 0, 0)),
            pl.BlockSpec((NSA_HEADS, tq, n_sub), lambda i, j: (0, i, 0)),
            pl.BlockSpec((None, tq, LANES), lambda i, j: (j, i, 0)),
            pl.BlockSpec((1, LANES), lambda i, j: (0, 0)),
            pl.BlockSpec((LANES, LANES), lambda i, j: (0, 0)),
        ],
        out_specs=[
            pl.BlockSpec((None, tq, NSA_WIDTH), lambda i, j: (j, i, 0)),
            pl.BlockSpec((None, tq, nslp), lambda i, j: (j, i, 0)),
            pl.BlockSpec((None, tq, nslp), lambda i, j: (j, i, 0)),
        ],
        out_shape=[
            jax.ShapeDtypeStruct((b, s, NSA_WIDTH), F32),
            jax.ShapeDtypeStruct((b, s, nslp), F32),
            jax.ShapeDtypeStruct((b, s, nslp), F32),
        ],
        compiler_params=_cparams("parallel", "parallel"),
        name="cmp_attention",
    )(q, parts, bias, gn, gk, _seg_matrix(LANES))


def _flash_kernel(q_ref, ns0_ref, ns1_ref, kv_ref, vt_ref, tab_ref, gn_ref, o_ref, *, t, use_sel, band, gate_base):
    qi = pl.program_id(1)
    row_k = _row_iota((t, LANES))
    lane_k = _lane_iota((t, LANES))
    lo_tile = jnp.maximum(qi - band, 0) if band is not None else 0
    gnt = gn_ref[...].T
    zeros64 = jnp.zeros((HEAD_DIM, t), BF16)

    qts = []
    for g in range(NSA_KV_HEADS):
        cols = []
        for pr in range(2):
            qt = q_ref[:, (2 * g + pr) * LANES:(2 * g + pr + 1) * LANES].astype(F32).T.astype(BF16)
            for half in range(2):
                qh = qt[half * HEAD_DIM:(half + 1) * HEAD_DIM]
                cols.append(jnp.concatenate([qh, zeros64] if g == 0 else [zeros64, qh], axis=0))
        qt_g = jnp.concatenate(cols, axis=1)
        if use_sel:
            nst = (ns0_ref if g == 0 else ns1_ref)[...].T.astype(BF16)
            qt_g = jnp.concatenate([qt_g, jnp.concatenate([nst] * NSA_HPG, axis=1)], axis=0)
        qts.append(qt_g)

    def body(kj, carry):
        k0 = pl.multiple_of(kj * t, t)
        kk = kv_ref[pl.ds(k0, t), 0:LANES].astype(BF16)
        if use_sel:
            blk = kj * (t // SLC_BLOCK) + jnp.right_shift(row_k, SLC_BLOCK.bit_length() - 1)
            onehot = jnp.where(lane_k == blk, -(2.0 ** 30), 0.0).astype(BF16)
            kk = jnp.concatenate([kk, onehot], axis=1)
        delta = qi - kj
        if band is None:
            kind = jnp.minimum(delta, 2)
        else:
            kind = jnp.where(delta < 2, delta, jnp.where(delta < band, 2, 3))
        new = []
        for g in range(NSA_KV_HEADS):
            m_old, l_old, acc = carry[g]
            s = _dot(kk, qts[g]) + tab_ref[g, kind]
            m_new = jnp.maximum(m_old, jnp.max(s, axis=0, keepdims=True))
            alpha = jnp.exp(m_old - m_new)
            p = jnp.exp(s - m_new)
            l_new = alpha * l_old + jnp.sum(p, axis=0, keepdims=True)
            vt = vt_ref[g * HEAD_DIM:(g + 1) * HEAD_DIM, pl.ds(k0, t)].astype(BF16)
            new.append((m_new, l_new, acc * alpha + _dot(vt, p.astype(BF16))))
        return tuple(new)

    init = (jnp.full((1, NSA_HPG * t), NEG, F32), jnp.zeros((1, NSA_HPG * t), F32),
            jnp.zeros((HEAD_DIM, NSA_HPG * t), F32))
    fin = lax.fori_loop(lo_tile, qi + 1, body, (init, init))
    for g in range(NSA_KV_HEADS):
        _, l_fin, acc = fin[g]
        h0 = gate_base + NSA_HPG * g
        gate = jnp.concatenate([gnt[h0 + c:h0 + c + 1] for c in range(NSA_HPG)], axis=1)
        out = acc * (gate / l_fin)
        for pr in range(2):
            pair = jnp.concatenate([out[:, (2 * pr) * t:(2 * pr + 1) * t], out[:, (2 * pr + 1) * t:(2 * pr + 2) * t]],
                                   axis=0)
            o_ref[:, (2 * g + pr) * LANES:(2 * g + pr + 1) * LANES] = pair.T


def _flash_attention(q, ns0, ns1, kv, vt, tab, gn, *, use_sel, band, gate_base):
    b, s, _ = q.shape
    t = ATT_TILE
    kern = functools.partial(_flash_kernel, t=t, use_sel=use_sel, band=band, gate_base=gate_base)
    tile = lambda w: pl.BlockSpec((None, t, w), lambda i, j: (i, j, 0))
    return pl.pallas_call(
        kern,
        grid=(b, s // t),
        in_specs=[
            tile(NSA_WIDTH), tile(ns0.shape[2]), tile(ns1.shape[2]),
            pl.BlockSpec((None, s, KV_WIDTH), lambda i, j: (i, 0, 0)),
            pl.BlockSpec((LANES, s), lambda i, j: (0, i)),
            pl.BlockSpec(tab.shape, lambda i, j: (0, 0, 0, 0)),
            tile(LANES),
        ],
        out_specs=tile(NSA_WIDTH),
        out_shape=jax.ShapeDtypeStruct((b, s, NSA_WIDTH), F32),
        compiler_params=_cparams("parallel", "parallel"),
        name="flash_sel" if use_sel else "flash_win",
    )(q, ns0, ns1, kv, vt, tab, gn)


def _memattn_kernel(q_ref, kv_ref, o_ref):
    lane = _lane_iota((kv_ref.shape[0], LANES))
    for pr in range(MEM_HEADS // 2):
        qpair = q_ref[:, pr * LANES:(pr + 1) * LANES]
        kblk = kv_ref[:, pr * LANES:(pr + 1) * LANES]
        vblk = kv_ref[:, MEM_WIDTH + pr * LANES:MEM_WIDTH + (pr + 1) * LANES]
        out = None
        for half in range(2):
            keep = (lane < HEAD_DIM) if half == 0 else (lane >= HEAD_DIM)
            kk = jnp.where(keep, kblk, 0.0).astype(BF16)
            vv = jnp.where(keep, vblk, 0.0).astype(BF16)
            s = _dot_nt(qpair, kk)
            m = jnp.max(s, axis=1, keepdims=True)
            e = jnp.exp(s - m)
            p = e / jnp.sum(e, axis=1, keepdims=True)
            o = _dot(p.astype(BF16), vv)
            out = o if out is None else out + o
        o_ref[:, pr * LANES:(pr + 1) * LANES] = out


def _mem_attention(qm, mem_kv, tq):
    b, s, _ = qm.shape
    m = mem_kv.shape[1]
    return pl.pallas_call(
        _memattn_kernel,
        grid=(b, s // tq),
        in_specs=[
            pl.BlockSpec((None, tq, MEM_WIDTH), lambda i, j: (i, j, 0)),
            pl.BlockSpec((None, m, 2 * MEM_WIDTH), lambda i, j: (i, 0, 0)),
        ],
        out_specs=pl.BlockSpec((None, tq, MEM_WIDTH), lambda i, j: (i, j, 0)),
        out_shape=jax.ShapeDtypeStruct((b, s, MEM_WIDTH), F32),
        compiler_params=_cparams("parallel", "parallel"),
        name="mem_attention",
    )(qm, mem_kv)


def _dec_kernel(*refs, n_pg, pps, has_new):
    tab_ref, bidx_ref = refs[0], refs[1]
    page_refs = refs[2:2 + pps]
    new_ref, wq_ref, bias_ref, ns_ref, gate_ref, o_ref, acc_ref, m_ref, l_ref = refs[2 + pps:]
    del tab_ref
    c = pl.program_id(1)
    n_chunks = pl.num_programs(1)

    @pl.when(c == 0)
    def _():
        acc_ref[...] = jnp.zeros_like(acc_ref)
        m_ref[...] = jnp.full_like(m_ref, NEG)
        l_ref[...] = jnp.zeros_like(l_ref)

    rk = _row_iota((LANES, LANES))

    def step(row_tiles, first_page):
        rows = row_tiles[0] if len(row_tiles) == 1 else jnp.concatenate(row_tiles, axis=0)
        s = _dot(rows.astype(BF16), wq_ref[...])
        extra = []
        for k in range(len(row_tiles)):
            pg = first_page + k
            ns = jnp.where(rk < SLC_BLOCK, ns_ref[pl.ds(2 * pg, 1), :], ns_ref[pl.ds(2 * pg + 1, 1), :])
            extra.append(jnp.where(ns > 0.5, NEG, bias_ref[bidx_ref[pg]]))
        s = s + (extra[0] if len(extra) == 1 else jnp.concatenate(extra, axis=0))
        m_old = m_ref[...]
        m_new = jnp.maximum(m_old, jnp.max(s, axis=0, keepdims=True))
        alpha = jnp.exp(m_old - m_new)
        p = jnp.exp(s - m_new)
        l_ref[...] = alpha * l_ref[...] + jnp.sum(p, axis=0, keepdims=True)
        m_ref[...] = m_new
        acc_ref[...] = acc_ref[...] * alpha + _dot(rows.T.astype(BF16), p.astype(BF16))

    if has_new:
        @pl.when(c < n_chunks - 1)
        def _():
            step([r[...] for r in page_refs], c * pps)

        @pl.when(c == n_chunks - 1)
        def _():
            step([new_ref[...]], n_pg)
    else:
        step([r[...] for r in page_refs], c * pps)

    @pl.when(c == n_chunks - 1)
    def _():
        o_ref[...] = acc_ref[...] / l_ref[...] * gate_ref[...]


def _decode_attention(pages, table, new_rows, wq, bias_tab, bias_idx, notsel, gate):
    bsz, n_pg = table.shape
    w = pages.shape[2]
    has_new = new_rows is not None
    pps = math.gcd(n_pg, DEC_PAGES_PER_STEP)
    n_steps = n_pg // pps
    n_chunks = n_steps + (1 if has_new else 0)
    if not has_new:
        new_rows = jnp.zeros((1, LANES, w), F32)
    new_map = (lambda i, c, tab, bi: (i, 0, 0)) if has_new else (lambda i, c, tab, bi: (0, 0, 0))

    def page_spec(k):
        return pl.BlockSpec((None, LANES, w),
                            lambda i, c, tab, bi: (tab[i, jnp.minimum(c, n_steps - 1) * pps + k], 0, 0))

    grid_spec = pltpu.PrefetchScalarGridSpec(
        num_scalar_prefetch=2,
        grid=(bsz, n_chunks),
        in_specs=[page_spec(k) for k in range(pps)] + [
            pl.BlockSpec((None, LANES, w), new_map),
            pl.BlockSpec((None, w, LANES), lambda i, c, tab, bi: (i, 0, 0)),
            pl.BlockSpec(bias_tab.shape, lambda i, c, tab, bi: (0, 0, 0)),
            pl.BlockSpec((None, notsel.shape[1], LANES), lambda i, c, tab, bi: (i, 0, 0)),
            pl.BlockSpec((None, 1, LANES), lambda i, c, tab, bi: (i, 0, 0)),
        ],
        out_specs=pl.BlockSpec((None, w, LANES), lambda i, c, tab, bi: (i, 0, 0)),
        scratch_shapes=[pltpu.VMEM((w, LANES), F32), pltpu.VMEM((1, LANES), F32), pltpu.VMEM((1, LANES), F32)],
    )
    return pl.pallas_call(
        functools.partial(_dec_kernel, n_pg=n_pg, pps=pps, has_new=has_new),
        grid_spec=grid_spec,
        out_shape=jax.ShapeDtypeStruct((bsz, w, LANES), F32),
        compiler_params=_cparams("parallel", "arbitrary"),
        name="decode_attention",
    )(table, bias_idx, *([pages] * pps), new_rows, wq, bias_tab, notsel, gate)


def _tail_kernel(x_ref, g1_ref, op_ref, oc_ref, os_ref, ow_ref, om_ref, cnt0_ref, wgb_ref, wup_p_ref, wup_n_ref,
                 wup_m_ref, wout_ref, g2_ref, rwh_ref, rwl_ref, rb_ref, x2_ref, h2_ref, ei_ref, gt_ref, cnt_ref):
    @pl.when(pl.program_id(0) == 0)
    def _():
        cnt_ref[...] = cnt0_ref[...]

    x = x_ref[...]
    h = _rms(x, g1_ref[...]).astype(BF16)
    onsa = (oc_ref[...] + os_ref[...] + ow_ref[...]).astype(BF16)
    ups = (_dot(op_ref[...].astype(BF16), wup_p_ref[...]), _dot(onsa, wup_n_ref[...]),
           _dot(om_ref[...].astype(BF16), wup_m_ref[...]))
    mixed = None
    for br in range(3):
        gb = _sigmoid(_dot(h, wgb_ref[:, br * D_MODEL:(br + 1) * D_MODEL]))
        mixed = gb * ups[br] if mixed is None else mixed + gb * ups[br]
    x2 = x + _dot(mixed.astype(BF16), wout_ref[...])
    x2_ref[...] = x2
    h2 = _rms(x2, g2_ref[...])
    h2_ref[...] = h2
    hi = h2.astype(BF16)
    lo = (h2 - hi.astype(F32)).astype(BF16)
    logits = _dot(hi, rwh_ref[...]) + _dot(lo, rwh_ref[...]) + _dot(hi, rwl_ref[...]) + rb_ref[...]
    lane = _lane_iota(logits.shape)
    lanef = lane.astype(F32)
    tops, idxs = [], []
    for _ in range(TOP_K):
        m = jnp.max(logits, axis=1, keepdims=True)
        idx = jnp.min(jnp.where(logits == m, lanef, 1e9), axis=1, keepdims=True)
        logits = jnp.where(lanef == idx, -jnp.inf, logits)
        tops.append(m)
        idxs.append(idx)
    es = [jnp.exp(tk - tops[0]) for tk in tops]
    den = es[0] + es[1] + es[2] + es[3]
    tm = logits.shape[0]
    onehot = jnp.zeros(logits.shape, F32)
    for k in range(TOP_K):
        onehot = jnp.where(lanef == idxs[k], 1.0, onehot)
    tri = jnp.where(_row_iota((tm, tm)) > _lane_iota((tm, tm)), 1.0, 0.0).astype(BF16)
    before = _dot(tri, onehot.astype(BF16)) + cnt_ref[...]
    cnt_ref[...] = cnt_ref[...] + jnp.sum(onehot, axis=0, keepdims=True)
    ei = jnp.zeros(logits.shape, F32)
    gt = jnp.zeros(logits.shape, F32)
    for k in range(TOP_K):
        rank = jnp.sum(jnp.where(lanef == idxs[k], before, 0.0), axis=1, keepdims=True)
        ei = jnp.where(lane == k, idxs[k], ei)
        ei = jnp.where(lane == TOP_K + k, rank, ei)
        gt = jnp.where(lane == k, es[k] / den, gt)
    ei_ref[...] = ei.astype(jnp.int32)
    gt_ref[...] = gt


def _layer_tail(x, o_pool, o_cmp, o_slc, o_win, o_mem, cnt0, w, tm):
    n = x.shape[0]
    row = lambda wd: pl.BlockSpec((tm, wd), lambda i: (i, 0))
    full = lambda a: pl.BlockSpec(a.shape, lambda i: (0,) * a.ndim)
    weights = (w["wgb"], w["wup_pool"], w["wup_nsa"], w["wup_mem"], w["wout"], w["g2"], w["rw_hi"], w["rw_lo"],
               w["rb"])
    return pl.pallas_call(
        _tail_kernel,
        grid=(n // tm,),
        in_specs=[row(D_MODEL), full(w["g1"]), row(POOL_WIDTH), row(NSA_WIDTH), row(NSA_WIDTH), row(NSA_WIDTH),
                  row(MEM_WIDTH), full(cnt0)] + [full(a) for a in weights],
        out_specs=[row(D_MODEL), row(D_MODEL), row(LANES), row(LANES), full(cnt0)],
        out_shape=[jax.ShapeDtypeStruct((n, D_MODEL), F32), jax.ShapeDtypeStruct((n, D_MODEL), F32),
                   jax.ShapeDtypeStruct((n, LANES), jnp.int32), jax.ShapeDtypeStruct((n, LANES), F32),
                   jax.ShapeDtypeStruct((1, LANES), F32)],
        compiler_params=_cparams("arbitrary"),
        name="layer_tail",
    )(x, w["g1"], o_pool, o_cmp, o_slc, o_win, o_mem, cnt0, *weights)


def _ffn_kernel(be_ref, nu_ref, x_ref, wgu_ref, bgu_ref, wd_ref, bd_ref, o_ref, wgu_bf, wd_bf):
    i = pl.program_id(0)

    @pl.when((i == 0) | (be_ref[i] != be_ref[jnp.maximum(i - 1, 0)]))
    def _():
        wgu_bf[...] = wgu_ref[...].astype(BF16)
        wd_bf[...] = wd_ref[...].astype(BF16)

    @pl.when(i < nu_ref[0])
    def _():
        gu = _dot(x_ref[...].astype(BF16), wgu_bf[...]) + bgu_ref[...]
        gate = jnp.minimum(gu[:, :D_FF], SWIGLU_LIMIT)
        up = jnp.clip(gu[:, D_FF:], -SWIGLU_LIMIT, SWIGLU_LIMIT)
        act = gate * _sigmoid(SWIGLU_ALPHA * gate) * (up + 1.0)
        o_ref[...] = _dot(act.astype(BF16), wd_bf[...]) + bd_ref[...]

    @pl.when(i >= nu_ref[0])
    def _():
        o_ref[...] = jnp.zeros_like(o_ref)


def _expert_ffn(rows, blk_e, n_used, wgu, bgu, wd, bd):
    n_rows = rows.shape[0]
    n_blocks = n_rows // MOE_TILE
    blk = lambda i, be, nu: (jnp.minimum(i, nu[0] - 1), 0)
    grid_spec = pltpu.PrefetchScalarGridSpec(
        num_scalar_prefetch=2,
        grid=(n_blocks,),
        in_specs=[
            pl.BlockSpec((MOE_TILE, D_MODEL), blk),
            pl.BlockSpec((None, D_MODEL, 2 * D_FF), lambda i, be, nu: (be[i], 0, 0)),
            pl.BlockSpec((None, 1, 2 * D_FF), lambda i, be, nu: (be[i], 0, 0)),
            pl.BlockSpec((None, D_FF, D_MODEL), lambda i, be, nu: (be[i], 0, 0)),
            pl.BlockSpec((None, 1, D_MODEL), lambda i, be, nu: (be[i], 0, 0)),
        ],
        out_specs=pl.BlockSpec((MOE_TILE, D_MODEL), lambda i, be, nu: (i, 0)),
        scratch_shapes=[pltpu.VMEM((D_MODEL, 2 * D_FF), BF16), pltpu.VMEM((D_FF, D_MODEL), BF16)],
    )
    return pl.pallas_call(
        _ffn_kernel,
        grid_spec=grid_spec,
        out_shape=jax.ShapeDtypeStruct((n_rows, D_MODEL), F32),
        compiler_params=_cparams("arbitrary"),
        name="expert_ffn",
    )(blk_e, n_used, rows, wgu, bgu, wd, bd)


def _token_tile(n):
    return next(t for t in (MOE_DMA_TOKENS, 384, 256, 128, n) if n % t == 0)


def _dispatch_kernel(dest_ref, h_ref, buf_ref, rows_ref, sem, *, td):
    del buf_ref

    def issue(i, carry):
        t0 = pl.multiple_of(i * 8, 8)
        for r in range(8):
            for k in range(TOP_K):
                d = dest_ref[i * (8 * TOP_K) + r * TOP_K + k]
                pltpu.make_async_copy(h_ref.at[pl.ds(t0 + r, 1)], rows_ref.at[pl.ds(d, 1)], sem).start()
        return carry

    lax.fori_loop(0, td // 8, issue, 0)
    for k in range(TOP_K):
        pltpu.make_async_copy(h_ref, rows_ref.at[pl.ds(0, td)], sem).wait()


def _moe_dispatch(rows_buf, h2, dest):
    n = h2.shape[0]
    td = _token_tile(n)
    return pl.pallas_call(
        functools.partial(_dispatch_kernel, td=td),
        grid=(n // td,),
        in_specs=[pl.BlockSpec((td * TOP_K,), lambda i: (i,), memory_space=pltpu.SMEM),
                  pl.BlockSpec((td, D_MODEL), lambda i: (i, 0)),
                  pl.BlockSpec(memory_space=pl.ANY)],
        out_specs=pl.BlockSpec(memory_space=pl.ANY),
        out_shape=jax.ShapeDtypeStruct(rows_buf.shape, rows_buf.dtype),
        scratch_shapes=[pltpu.SemaphoreType.DMA(())],
        input_output_aliases={2: 0},
        compiler_params=_cparams("arbitrary"),
        name="moe_dispatch",
    )(dest, h2, rows_buf)


def _combine_kernel(dest_ref, x2_ref, g_ref, rows_ref, o_ref, ybuf, sem, *, td):
    def issue(i, carry):
        t0 = pl.multiple_of(i * 8, 8)
        for r in range(8):
            for k in range(TOP_K):
                d = dest_ref[i * (8 * TOP_K) + r * TOP_K + k]
                pltpu.make_async_copy(rows_ref.at[pl.ds(d, 1)], ybuf.at[k, pl.ds(t0 + r, 1)], sem).start()
        return carry

    lax.fori_loop(0, td // 8, issue, 0)
    for k in range(TOP_K):
        pltpu.make_async_copy(rows_ref.at[pl.ds(0, td)], ybuf.at[k], sem).wait()
    out = x2_ref[...]
    for k in range(TOP_K):
        out = out + g_ref[:, k:k + 1] * ybuf[k]
    o_ref[...] = out


def _moe_combine(x2, out_rows, dest, gates):
    n = x2.shape[0]
    td = _token_tile(n)
    return pl.pallas_call(
        functools.partial(_combine_kernel, td=td),
        grid=(n // td,),
        in_specs=[pl.BlockSpec((td * TOP_K,), lambda i: (i,), memory_space=pltpu.SMEM),
                  pl.BlockSpec((td, D_MODEL), lambda i: (i, 0)),
                  pl.BlockSpec((td, LANES), lambda i: (i, 0)),
                  pl.BlockSpec(memory_space=pl.ANY)],
        out_specs=pl.BlockSpec((td, D_MODEL), lambda i: (i, 0)),
        out_shape=jax.ShapeDtypeStruct((n, D_MODEL), F32),
        scratch_shapes=[pltpu.VMEM((TOP_K, td, D_MODEL), F32), pltpu.SemaphoreType.DMA(())],
        compiler_params=_cparams("arbitrary"),
        name="moe_combine",
    )(dest, x2, gates, out_rows)


def _moe(groups, counts, w):
    n_total = sum(g[0].shape[0] for g in groups)
    cnt = counts[0, :N_EXPERTS].astype(jnp.int32)
    padded = (cnt + MOE_TILE - 1) // MOE_TILE * MOE_TILE
    pad_ends = jnp.cumsum(padded)
    pad_starts = pad_ends - padded
    n_blocks = -(-n_total * TOP_K // MOE_TILE) + N_EXPERTS
    blk_start = jnp.arange(n_blocks, dtype=jnp.int32) * MOE_TILE
    blk_e = jnp.minimum(jnp.sum(blk_start[:, None] >= pad_ends[None, :], axis=1), N_EXPERTS - 1).astype(jnp.int32)
    n_used = (pad_ends[-1] // MOE_TILE).astype(jnp.int32).reshape(1)
    dests = [(pad_starts[er[:, :TOP_K]] + er[:, TOP_K:2 * TOP_K]).astype(jnp.int32).reshape(-1)
             for (_, _, er, _) in groups]
    rows = jnp.zeros((n_blocks * MOE_TILE, D_MODEL), F32)
    for (_, h2, _, _), dest in zip(groups, dests):
        rows = _moe_dispatch(rows, h2, dest)
    out_rows = _expert_ffn(rows, blk_e, n_used, w["wgu"], w["bgu"], w["wd"], w["bd"])
    return [_moe_combine(x2, out_rows, dest, gates) for (x2, _, _, gates), dest in zip(groups, dests)]


def _rel_bucket(dist):
    n = jnp.maximum(dist, 0)
    max_exact = NUM_BUCKETS // 2
    nf = jnp.maximum(n, 1).astype(F32)
    large = max_exact + (jnp.log(nf / max_exact) / math.log(MAX_DISTANCE / max_exact)
                         * (NUM_BUCKETS - max_exact)).astype(jnp.int32)
    large = jnp.minimum(large, NUM_BUCKETS - 1)
    return jnp.where(n < max_exact, n, large)


def _bias_of(rel_bias, dist, valid):
    return jnp.where(valid[..., None], rel_bias[_rel_bucket(dist)], NEG)


_PROJ_SEGS = ((0, 512, "qscale"), (512, 256, "qscale"), (768, 256, "kv"), (1024, 256, "kv"), (1280, 256, "id"),
              (1536, 256, "id"), (1792, 128, "sigmoid"))
_PROJ_DTYPES = (BF16, BF16, F32, F32, F32, F32, F32)
_PROJ_NNORM = 1280


def _prep_layer(l, rel_bias, norm1_g, w_in, nsa_qk_norm, mem_qk_norm, cmp_w, cmp_pe, pool_w, pool_scale,
                mem_norm_g, w_mem_kv, w_up_pool, w_up_nsa, w_up_mem, w_out, norm2_g, router_w, router_b,
                w_gu, b_gu, w_down, b_down):
    wi = w_in[l]
    o_u, o_q, o_qm, o_kvc, o_kvs, o_kvw, o_gn, o_gb = 0, 256, 768, 1024, 1280, 1536, 1792, 1816
    w_proj = jnp.concatenate([
        wi[:, o_q:o_q + 512], wi[:, o_qm:o_qm + 256], wi[:, o_kvs:o_kvs + 256], wi[:, o_kvw:o_kvw + 256],
        wi[:, o_kvc:o_kvc + 256], wi[:, o_u:o_u + 256], wi[:, o_gn:o_gn + 24],
        jnp.zeros((D_MODEL, LANES - 24), F32)], axis=1).astype(BF16)
    nq, mq = nsa_qk_norm[l], mem_qk_norm[l]
    ones = jnp.ones((LANES,), F32)
    gain = jnp.concatenate([jnp.tile(nq[0], 8), jnp.tile(mq[0], 4), jnp.tile(nq[2], 2), ones,
                            jnp.tile(nq[3], 2), ones])[None, :]
    nmask = jnp.concatenate([jnp.ones((768,), F32), ones, 0 * ones, ones, 0 * ones])[None, :]
    eye4 = jnp.eye(4, dtype=F32)
    cw = cmp_w[l].reshape(2, 2, CMP_STRIDE, HEAD_DIM, HEAD_DIM)
    w_c = jnp.einsum("crjde,xy->cjxdrye", cw, jnp.eye(2, dtype=F32))
    w_c = w_c.reshape(2, CMP_STRIDE, LANES, KV_WIDTH).astype(BF16)
    pe = cmp_pe[l].reshape(2, 2, CMP_STRIDE, HEAD_DIM)
    pe_c = jnp.tile(pe.transpose(0, 2, 1, 3), (1, 1, 1, NSA_KV_HEADS))
    pe_c = jnp.pad(pe_c, ((0, 0), (0, 0), (0, 6), (0, 0)))
    w_pool = jnp.einsum("gde,gh->gdhe", pool_w[l], eye4).reshape(POOL_WIDTH, POOL_WIDTH).astype(BF16)
    rw = jnp.pad(router_w[l], ((0, 0), (0, LANES - N_EXPERTS)))
    rw_hi = rw.astype(BF16)
    rw_lo = (rw - rw_hi.astype(F32)).astype(BF16)
    rb = jnp.concatenate([router_b[l], jnp.full((LANES - N_EXPERTS,), NEG, F32)])[None, :]
    return {
        "g1": norm1_g[l][None, :], "w_proj": w_proj, "gain": gain, "nmask": nmask,
        "gk_cmp": jnp.tile(nq[1], 2)[None, :], "w_c": w_c, "pe_c": pe_c,
        "w_pool": w_pool, "pool_scale": pool_scale[l][None, :],
        "mem_g": mem_norm_g[l][None, :], "w_mem": w_mem_kv[l].astype(BF16),
        "mem_gain": jnp.concatenate([jnp.tile(mq[1], 4), jnp.ones((256,), F32)])[None, :],
        "mem_nmask": jnp.concatenate([jnp.ones((256,), F32), jnp.zeros((256,), F32)])[None, :],
        "wgb": wi[:, o_gb:o_gb + 3 * D_MODEL].astype(BF16),
        "wup_pool": w_up_pool[l].astype(BF16), "wup_nsa": w_up_nsa[l].astype(BF16),
        "wup_mem": w_up_mem[l].astype(BF16), "wout": w_out[l].astype(BF16), "g2": norm2_g[l][None, :],
        "rw_hi": rw_hi, "rw_lo": rw_lo, "rb": rb,
        "wgu": w_gu[l], "bgu": b_gu[l][:, None, :], "wd": w_down[l],
        "bd": b_down[l][:, None, :],
    }


def _project_in(x2d, w, tm):
    return _project(x2d, w["g1"], w["w_proj"], w["gain"], w["nmask"], _PROJ_SEGS, _PROJ_DTYPES, _PROJ_NNORM, tm)


def _toeplitz(v, t):
    lead = v.shape[:-1]
    flat = jnp.tile(v, (1,) * len(lead) + (t,))[..., t:t + t * (2 * t - 1)]
    return flat.reshape(lead + (t, 2 * t - 1))[..., :t]


def _flash_tables(rel_bias):
    t = ATT_TILE
    d0 = jnp.arange(-t, t)
    kinds = jnp.stack([
        _bias_of(rel_bias, d0, d0 >= 0),
        _bias_of(rel_bias, d0 + t, d0 + t >= 0),
        _bias_of(rel_bias, jnp.full((2 * t,), 2 * t), jnp.ones((2 * t,), bool)),
        _bias_of(rel_bias, d0 + WINDOW, d0 + WINDOW < WINDOW),
    ])
    tab = _toeplitz(kinds.transpose(2, 0, 1), t)
    tab = tab.reshape(NSA_KV_HEADS, NSA_HPG, 4, t, t)
    return tab.transpose(0, 2, 3, 1, 4).reshape(NSA_KV_HEADS, 4, t, NSA_HPG * t)


def _cmp_bias_table(rel_bias, s, n_sub, n_cmp, pos0):
    na = s // CMP_STRIDE
    m = max(na, n_sub)
    k = jnp.arange(-m, m)[None, :]
    r = jnp.arange(CMP_STRIDE)[:, None]
    d = CMP_STRIDE * k + r - (CMP_BLOCK - 1) + pos0
    v = _bias_of(rel_bias, d, d >= 0).transpose(2, 0, 1)
    tz = _toeplitz(v, m)[:, :, :n_sub, :na]
    tab = tz.transpose(0, 3, 1, 2).reshape(NSA_HEADS, s, n_sub)
    return jnp.where(jnp.arange(n_sub)[None, None, :] < n_cmp, tab, NEG)


def _prompt_pre(x, mem, w, rel_bias, cnt0):
    b, s, _ = x.shape
    n = b * s
    tm = 512 if n % 512 == 0 else ATT_TILE
    x2d = x.reshape(n, D_MODEL)
    q, qm, kvs, vts, kvw, vtw, kvc, u, gn = _project_in(x2d, w, tm)
    r3 = lambda a: a.reshape(b, s, a.shape[-1])
    q, qm, kvs, kvw, kvc, u, gn = map(r3, (q, qm, kvs, kvw, kvc, u, gn))

    o_pool = _pool_mix(u, jnp.zeros((b, 16, POOL_WIDTH), F32), w["w_pool"], w["pool_scale"], 0)

    n_cmp = (s - CMP_BLOCK) // CMP_STRIDE + 1
    n_slc = -(-s // SLC_BLOCK)
    parts = _cmp_partials_dense(kvc, w["pe_c"], w["w_c"])
    n_sub = parts.shape[1]
    bias_c = _cmp_bias_table(rel_bias, s, n_sub, n_cmp, 0)
    tq = 256 if s % 256 == 0 else ATT_TILE
    o_cmp, ns0, ns1 = _cmp_attention(q, parts, bias_c, gn, w["gk_cmp"], tq=tq, n_cmp=n_cmp, n_slc=n_slc, pos0=0)

    tab = _flash_tables(rel_bias)
    o_slc = _flash_attention(q, ns0, ns1, kvs, vts, tab, gn, use_sel=True, band=None, gate_base=NSA_HEADS)
    o_win = _flash_attention(q, ns0, ns1, kvw, vtw, tab, gn, use_sel=False, band=WINDOW // ATT_TILE,
                             gate_base=2 * NSA_HEADS)

    m = mem.shape[1]
    (mem_kv,) = _project(mem.reshape(b * m, D_MODEL), w["mem_g"], w["w_mem"], w["mem_gain"], w["mem_nmask"],
                         ((0, 2 * MEM_WIDTH, "id"),), (F32,), MXU_DIM, tm=min(512, b * m))
    mem_kv = mem_kv.reshape(b, m, 2 * MEM_WIDTH)
    o_mem = _mem_attention(qm, mem_kv, tq=min(512, s))

    f2 = lambda a: a.reshape(n, a.shape[-1])
    x2, h2, eidx, gates, cnt = _layer_tail(x2d, f2(o_pool), f2(o_cmp), f2(o_slc), f2(o_win), f2(o_mem), cnt0, w, tm)
    kvshape = (b, s, 2, NSA_KV_HEADS, HEAD_DIM)
    win_buf = WINDOW
    states = (kvc.reshape(kvshape), kvs.reshape(kvshape),
              _last_rows(kvw, win_buf).reshape(b, win_buf, 2, NSA_KV_HEADS, HEAD_DIM),
              mem_kv.reshape(b, m, 2, MEM_HEADS, HEAD_DIM), _last_rows(u, POOL_BUF))
    return (x2, h2, eidx, gates), states, cnt


def _last_rows(a, n):
    t = a.shape[1]
    if t < n:
        a = jnp.pad(a, [(0, 0), (n - t, 0)] + [(0, 0)] * (a.ndim - 2))
    return a[:, a.shape[1] - n:]


def _dec_columns_nsa(q):
    b, t, _ = q.shape
    qh = q.reshape(b, t, NSA_KV_HEADS, NSA_HPG, HEAD_DIM)
    w = jnp.einsum("btgpd,gx->bxdgtp", qh.astype(F32), jnp.eye(NSA_KV_HEADS, dtype=F32))
    w = w.reshape(b, NSA_KV_HEADS * HEAD_DIM, NSA_KV_HEADS * t * NSA_HPG)
    return jnp.pad(w, ((0, 0), (0, KV_WIDTH - w.shape[1]), (0, LANES - w.shape[2]))).astype(BF16)


def _dec_extract_nsa(o, t):
    b = o.shape[0]
    v = o[:, LANES:, :NSA_KV_HEADS * t * NSA_HPG]
    v = v.reshape(b, NSA_KV_HEADS, HEAD_DIM, NSA_KV_HEADS, t, NSA_HPG)
    v = jnp.einsum("bxdgtp,gx->btgpd", v, jnp.eye(NSA_KV_HEADS, dtype=F32))
    return v.reshape(b, t, NSA_WIDTH)


def _dec_bias_cols(bias_tph):
    k, t, _ = bias_tph.shape
    bt = bias_tph.reshape(k, t, NSA_KV_HEADS, NSA_HPG).transpose(0, 2, 1, 3).reshape(k, NSA_KV_HEADS * t * NSA_HPG)
    return jnp.pad(bt, ((0, 0), (0, LANES - bt.shape[1])))


def _dec_gate_cols(gn, base, t):
    b = gn.shape[0]
    gt = gn[:, :, base:base + NSA_HEADS].reshape(b, t, NSA_KV_HEADS, NSA_HPG).transpose(0, 2, 1, 3)
    gt = gt.reshape(b, 1, NSA_KV_HEADS * t * NSA_HPG)
    return jnp.pad(gt, ((0, 0), (0, 0), (0, LANES - gt.shape[2])), constant_values=1.0)


def _sample_pre(x, cache_cmp, cache_slc, cache_win, cache_mem, pool_buf, page_table, w, rel_bias, cnt0):
    b, t, _ = x.shape
    n = b * t
    page = cache_cmp.shape[1]
    n_pages = page_table.shape[1]
    past = n_pages * page
    x2d = x.reshape(n, D_MODEL)
    q, qm, kvs, _, kvw, _, kvc, u, gn = _project_in(x2d, w, n if n <= 512 else ATT_TILE)
    r3 = lambda a: a.reshape(b, t, a.shape[-1])
    q, qm, kvs, kvw, kvc, u, gn = map(r3, (q, qm, kvs, kvw, kvc, u, gn))
    qpos = past + jnp.arange(t)

    buf16 = jnp.pad(pool_buf, ((0, 0), (16 - POOL_BUF, 0), (0, 0)))
    o_pool = _pool_mix(u, buf16, w["w_pool"], w["pool_scale"], past)

    total = past + t
    n_cmp = (total - CMP_BLOCK) // CMP_STRIDE + 1
    n_sub_used = n_cmp + CMP_BLOCK // CMP_STRIDE - 1
    n_slc = -(-total // SLC_BLOCK)
    pool_c = cache_cmp.reshape(cache_cmp.shape[0], page, KV_WIDTH)
    pps = math.gcd(n_pages, 16)
    parts = _cmp_partials_paged(pool_c, page_table, w["pe_c"], w["w_c"], pps)
    extra = n_sub_used * CMP_STRIDE - past
    if extra > 0:
        tail_rows = -(-extra // CMP_STRIDE) * CMP_STRIDE
        new_c = jnp.pad(kvc, ((0, 0), (0, max(0, tail_rows - t)), (0, 0)))[:, :tail_rows]
        parts = jnp.concatenate([parts, _cmp_partials_dense(new_c, w["pe_c"], w["w_c"])], axis=1)
    n_sub = parts.shape[1]
    end = jnp.arange(n_sub)[None, :] * CMP_STRIDE + CMP_BLOCK - 1
    bias_c = _bias_of(rel_bias, qpos[:, None] - end, (end <= qpos[:, None]) & (jnp.arange(n_sub)[None, :] < n_cmp))
    bias_c = bias_c.transpose(2, 0, 1)
    o_cmp, ns0, ns1 = _cmp_attention(q, parts, bias_c, gn, w["gk_cmp"], tq=t, n_cmp=n_cmp, n_slc=n_slc, pos0=past)

    wq = _dec_columns_nsa(q)
    ncol = NSA_KV_HEADS * t * NSA_HPG

    n_chunks = n_pages + 1
    nblk = -(-2 * n_chunks // 8) * 8
    ns = jnp.stack([ns0, ns1], axis=1)
    ns = jnp.pad(ns, ((0, 0), (0, 0), (0, 0), (0, max(0, nblk - ns.shape[3]))))[..., :nblk]
    ns = jnp.broadcast_to(ns[:, :, :, None, :], (b, NSA_KV_HEADS, t, NSA_HPG, nblk)).reshape(b, ncol, nblk)
    notsel = jnp.pad(ns.transpose(0, 2, 1), ((0, 0), (0, 0), (0, LANES - ncol)))
    rows = jnp.arange(LANES)
    far = _bias_of(rel_bias, jnp.full((LANES, t), 2 * MAX_DISTANCE), jnp.ones((LANES, t), bool))
    kpos_last = past - LANES + rows
    d_last = qpos[None, :] - kpos_last[:, None]
    near = _bias_of(rel_bias, d_last, d_last >= 0)
    kpos_new = past + rows
    d_new = qpos[None, :] - kpos_new[:, None]
    newb = _bias_of(rel_bias, d_new, (d_new >= 0) & (rows[:, None] < t))
    bias_tab = jnp.stack([_dec_bias_cols(far), _dec_bias_cols(near), _dec_bias_cols(newb)])
    bias_idx = jnp.concatenate([jnp.zeros((n_pages - 1,), jnp.int32), jnp.array([1, 2], jnp.int32)])
    pool_s = cache_slc.reshape(cache_slc.shape[0], page, KV_WIDTH)
    new_s = jnp.pad(kvs, ((0, 0), (0, LANES - t), (0, 0)))
    o_slc = _decode_attention(pool_s, page_table, new_s, wq, bias_tab, bias_idx, notsel,
                              _dec_gate_cols(gn, NSA_HEADS, t))
    o_slc = _dec_extract_nsa(o_slc, t)

    wb = cache_win.shape[1]
    n_wpg = wb // LANES
    win_pages = cache_win.reshape(b * n_wpg, LANES, KV_WIDTH)
    win_table = (jnp.arange(b)[:, None] * n_wpg + jnp.arange(n_wpg)[None, :]).astype(jnp.int32)
    kpos_w = past - wb + jnp.arange(wb + LANES)
    d_w = qpos[None, :] - kpos_w[:, None]
    valid_w = (d_w >= 0) & (d_w < WINDOW) & (kpos_w[:, None] >= 0) & (jnp.arange(wb + LANES)[:, None] < wb + t)
    bias_w = _dec_bias_cols(_bias_of(rel_bias, d_w, valid_w)).reshape(n_wpg + 1, LANES, LANES)
    zeros_ns = jnp.zeros((b, -(-2 * (n_wpg + 1) // 8) * 8, LANES), F32)
    new_w = jnp.pad(kvw, ((0, 0), (0, LANES - t), (0, 0)))
    o_win = _decode_attention(win_pages, win_table, new_w, wq, bias_w, jnp.arange(n_wpg + 1, dtype=jnp.int32),
                              zeros_ns, _dec_gate_cols(gn, 2 * NSA_HEADS, t))
    o_win = _dec_extract_nsa(o_win, t)

    m = cache_mem.shape[1]
    n_mpg = m // LANES
    mem_pages = cache_mem.reshape(b * n_mpg, LANES, 2 * MEM_WIDTH)
    mem_table = (jnp.arange(b)[:, None] * n_mpg + jnp.arange(n_mpg)[None, :]).astype(jnp.int32)
    qmh = qm.reshape(b, t, MEM_HEADS, HEAD_DIM).astype(F32)
    wqm = jnp.einsum("bthd,hx->bxdht", qmh, jnp.eye(MEM_HEADS, dtype=F32))
    wqm = wqm.reshape(b, MEM_WIDTH, MEM_HEADS * t)
    wqm = jnp.pad(wqm, ((0, 0), (0, MEM_WIDTH), (0, LANES - MEM_HEADS * t))).astype(BF16)
    o_mem = _decode_attention(mem_pages, mem_table, None, wqm, jnp.zeros((1, LANES, LANES), F32),
                              jnp.zeros((n_mpg,), jnp.int32), jnp.zeros((b, 8, LANES), F32),
                              jnp.ones((b, 1, LANES), F32))
    om = o_mem[:, MEM_WIDTH:, :MEM_HEADS * t].reshape(b, MEM_HEADS, HEAD_DIM, MEM_HEADS, t)
    o_mem = jnp.einsum("bxdht,hx->bthd", om, jnp.eye(MEM_HEADS, dtype=F32)).reshape(b, t, MEM_WIDTH)

    f2 = lambda a: a.reshape(n, a.shape[-1])
    x2, h2, eidx, gates, cnt = _layer_tail(x2d, f2(o_pool), f2(o_cmp), f2(o_slc), f2(o_win), f2(o_mem), cnt0, w,
                                           n if n <= 512 else LANES)
    kvshape = (b, t, 2, NSA_KV_HEADS, HEAD_DIM)
    new_win = jnp.concatenate([cache_win, kvw], axis=1)[:, t:]
    new_pool = jnp.concatenate([pool_buf, u], axis=1)[:, t:]
    states = (kvc.reshape(kvshape), kvs.reshape(kvshape), new_win, new_pool)
    return (x2, h2, eidx, gates), states, cnt


def kernel(x_prompt, x_sample, cache_cmp_kv, cache_slc_kv, cache_win_kv, cache_mem_kv, state_pool, page_table,
           mem_prompt, rel_bias, norm1_g, w_in, nsa_qk_norm, mem_qk_norm, cmp_w, cmp_pe, pool_w, pool_scale,
           mem_norm_g, w_mem_kv, w_up_pool, w_up_nsa, w_up_mem, w_out, norm2_g, router_w, router_b, w_gu, b_gu,
           w_down, b_down):
    depth = w_in.shape[0]
    yp, ys = x_prompt, x_sample
    bp, sp, _ = x_prompt.shape
    bs, ts, _ = x_sample.shape
    outs_p = [[] for _ in range(5)]
    outs_s = [[] for _ in range(4)]
    for l in range(depth):
        w = _prep_layer(l, rel_bias, norm1_g, w_in, nsa_qk_norm, mem_qk_norm, cmp_w, cmp_pe, pool_w, pool_scale,
                        mem_norm_g, w_mem_kv, w_up_pool, w_up_nsa, w_up_mem, w_out, norm2_g, router_w, router_b,
                        w_gu, b_gu, w_down, b_down)
        pre_p, st_p, cnt = _prompt_pre(yp, mem_prompt, w, rel_bias, jnp.zeros((1, LANES), F32))
        cw = cache_win_kv[l]
        pre_s, st_s, cnt = _sample_pre(ys, cache_cmp_kv[l], cache_slc_kv[l],
                                       cw.reshape(cw.shape[0], cw.shape[1], KV_WIDTH),
                                       cache_mem_kv[l].reshape(bs, cache_mem_kv.shape[2], 2 * MEM_WIDTH),
                                       state_pool[l], page_table, w, rel_bias, cnt)
        yp, ys = _moe([pre_p, pre_s], cnt, w)
        yp = yp.reshape(bp, sp, D_MODEL)
        ys = ys.reshape(bs, ts, D_MODEL)
        for lst, a in zip(outs_p, st_p):
            lst.append(a)
        for lst, a in zip(outs_s, st_s):
            lst.append(a)
    new_cmp_p, new_slc_p, new_win_p, new_mem_p, new_pool_p = [jnp.stack(a) for a in outs_p]
    new_cmp_s, new_slc_s, new_win_s, new_pool_s = [jnp.stack(a) for a in outs_s]
    new_win_s = new_win_s.reshape(new_win_s.shape[:3] + (2, NSA_KV_HEADS, HEAD_DIM))
    return (yp, ys, new_cmp_p, new_slc_p, new_win_p, new_mem_p, new_pool_p,
            new_cmp_s, new_slc_s, new_win_s, new_pool_s)
```

```python
import functools
import math

import jax
import jax.numpy as jnp
from jax import lax
from jax.experimental import pallas as pl
from jax.experimental.pallas import tpu as pltpu

F32 = jnp.float32
BF16 = jnp.bfloat16

D_MODEL = 1024
HEAD_DIM = 64
POOL_WINDOWS = (2, 4, 8, 16)
POOL_GROUP = 64
POOL_WIDTH = 256
POOL_BUF = 15
NSA_HEADS = 8
NSA_KV_HEADS = 2
NSA_HPG = 4
NSA_WIDTH = 512
KV_WIDTH = 256
CMP_BLOCK = 32
CMP_STRIDE = 16
SLC_BLOCK = 64
SLC_TOPK = 16
WINDOW = 512
MEM_HEADS = 4
MEM_WIDTH = 256
NUM_BUCKETS = 32
MAX_DISTANCE = 128
N_EXPERTS = 32
TOP_K = 4
D_FF = 1024
SWIGLU_ALPHA = 1.702
SWIGLU_LIMIT = 7.0
EPS = 1e-6
SCALE = HEAD_DIM ** -0.5

LANES = 128
MXU_DIM = 256
NEG = -1e30
ATT_TILE = 256
DEC_PAGES_PER_STEP = 8
MOE_TILE = 512
MOE_DMA_TOKENS = 512
VMEM_LIMIT = 48 * 1024 * 1024
FFN_VMEM_LIMIT = 56 * 1024 * 1024


def _cparams(*sem):
    return pltpu.CompilerParams(dimension_semantics=sem, vmem_limit_bytes=VMEM_LIMIT)


def _dot(a, b):
    return jnp.dot(a, b, preferred_element_type=F32)


def _dot_nt(a, b):
    return lax.dot_general(a, b, (((1,), (1,)), ((), ())), preferred_element_type=F32)


def _split_dot(a, b):
    hi = a.astype(BF16)
    lo = (a - hi.astype(F32)).astype(BF16)
    return _dot(hi, b) + _dot(lo, b)


def _rms(x, g):
    r = lax.rsqrt(jnp.mean(x * x, axis=-1, keepdims=True) + EPS)
    return (x * r) * g


def _sigmoid(x):
    return 1.0 / (1.0 + jnp.exp(-x))


def _lane_iota(shape):
    return lax.broadcasted_iota(jnp.int32, shape, len(shape) - 1)


def _row_iota(shape):
    return lax.broadcasted_iota(jnp.int32, shape, len(shape) - 2)


def _proj_kernel(x_ref, g_ref, w_ref, gain_ref, nmask_ref, seg_ref, *out_refs, segs, n_norm):
    h = _rms(x_ref[...], g_ref[...]).astype(BF16)
    seg = seg_ref[...]
    outs = iter(out_refs)
    for (start, width, kind) in segs:
        o_ref = next(outs)
        z = _dot(h, w_ref[:, start:start + width])
        if start < n_norm:
            pieces = []
            for c in range(0, width, MXU_DIM):
                zc = z[:, c:c + MXU_DIM]
                ms = _split_dot(zc * zc, seg)
                zn = (zc * lax.rsqrt(ms + EPS)) * gain_ref[:, start + c:start + c + MXU_DIM]
                pieces.append(jnp.where(nmask_ref[:, start + c:start + c + MXU_DIM] > 0, zn, zc))
            z = pieces[0] if len(pieces) == 1 else jnp.concatenate(pieces, axis=1)
        if kind == "sigmoid":
            z = _sigmoid(z)
        elif kind == "qscale":
            z = z * SCALE
        o_ref[...] = z.astype(o_ref.dtype)
        if kind == "kv":
            next(outs)[...] = z[:, LANES:2 * LANES].T


def _project(x, g, w, gain, nmask, segs, out_dtypes, n_norm, tm):
    n = x.shape[0]
    ncol = w.shape[1]
    seg = _seg_matrix(MXU_DIM)
    full = lambda i: (0, 0)
    out_specs, out_shape = [], []
    for (_, wd, kind), dt in zip(segs, out_dtypes):
        out_specs.append(pl.BlockSpec((tm, wd), lambda i: (i, 0)))
        out_shape.append(jax.ShapeDtypeStruct((n, wd), dt))
        if kind == "kv":
            out_specs.append(pl.BlockSpec((LANES, tm), lambda i: (0, i)))
            out_shape.append(jax.ShapeDtypeStruct((LANES, n), F32))
    return pl.pallas_call(
        functools.partial(_proj_kernel, segs=segs, n_norm=n_norm),
        grid=(n // tm,),
        in_specs=[
            pl.BlockSpec((tm, D_MODEL), lambda i: (i, 0)),
            pl.BlockSpec((1, D_MODEL), full),
            pl.BlockSpec((D_MODEL, ncol), full),
            pl.BlockSpec((1, gain.shape[1]), full),
            pl.BlockSpec((1, nmask.shape[1]), full),
            pl.BlockSpec((MXU_DIM, MXU_DIM), full),
        ],
        out_specs=out_specs,
        out_shape=out_shape,
        compiler_params=_cparams("parallel"),
        name="proj",
    )(x, g, w, gain, nmask, seg)


def _seg_matrix(n):
    i = jnp.arange(n) // HEAD_DIM
    return jnp.where(i[:, None] == i[None, :], 1.0 / HEAD_DIM, 0.0).astype(BF16)


def _pool_kernel(u_ref, buf_ref, w_ref, scale_ref, o_ref, zs_ref, *, t, pos0):
    zs_ref[0:16, :] = buf_ref[...]
    zs_ref[16:16 + t, :] = u_ref[...]
    u = u_ref[...]
    lane = _lane_iota((1, POOL_WIDTH))
    pos = (pos0 + _row_iota((t, 1))).astype(F32)
    acc = u
    mean = None
    for i in range(1, max(POOL_WINDOWS)):
        acc = acc + zs_ref[16 - i:16 - i + t, :]
        if i + 1 in POOL_WINDOWS:
            gi = POOL_WINDOWS.index(i + 1)
            m = acc / jnp.minimum(pos + 1.0, float(i + 1))
            mean = m if mean is None else jnp.where(lane >= gi * POOL_GROUP, m, mean)
    d = (mean - u).astype(BF16)
    o_ref[...] = _dot(d, w_ref[...]) * scale_ref[...]


def _pool_mix(u, buf16, w_bd, scale, pos0):
    b, t, _ = u.shape
    return pl.pallas_call(
        functools.partial(_pool_kernel, t=t, pos0=pos0),
        grid=(b,),
        in_specs=[
            pl.BlockSpec((None, t, POOL_WIDTH), lambda i: (i, 0, 0)),
            pl.BlockSpec((None, 16, POOL_WIDTH), lambda i: (i, 0, 0)),
            pl.BlockSpec((POOL_WIDTH, POOL_WIDTH), lambda i: (0, 0)),
            pl.BlockSpec((1, POOL_WIDTH), lambda i: (0, 0)),
        ],
        out_specs=pl.BlockSpec((None, t, POOL_WIDTH), lambda i: (i, 0, 0)),
        out_shape=jax.ShapeDtypeStruct((b, t, POOL_WIDTH), F32),
        scratch_shapes=[pltpu.VMEM((t + 16, POOL_WIDTH), F32)],
        compiler_params=_cparams("parallel"),
        name="pool",
    )(u, buf16, w_bd, scale)


def _cpart_compute(x_refs, pe_ref, w_ref, o_ref, m):
    for c in range(2):
        acc = jnp.zeros((m + 8, KV_WIDTH), F32)
        for j in range(CMP_STRIDE):
            lhs = jnp.concatenate([x_refs[c][pl.ds(j, m, stride=CMP_STRIDE), :], pe_ref[c, j]], axis=0)
            acc = acc + _dot(lhs.astype(BF16), w_ref[c, j])
        lane = _lane_iota((1, KV_WIDTH))
        pe_term = jnp.where(lane < LANES, acc[m:m + 1], acc[m + 1:m + 2])
        o_ref[:, c * KV_WIDTH:(c + 1) * KV_WIDTH] = acc[0:m] + pe_term


def _cpart_kernel(xk_ref, xv_ref, pe_ref, w_ref, o_ref, *, rows):
    _cpart_compute((xk_ref, xv_ref), pe_ref, w_ref, o_ref, rows // CMP_STRIDE)


def _cpart_paged_kernel(*refs, nop, page):
    page_refs = refs[1:1 + nop]
    pe_ref, w_ref, o_ref, xk_ref, xv_ref = refs[1 + nop:]
    for k, r in enumerate(page_refs):
        xk_ref[k * page:(k + 1) * page, :] = r[0:LANES, :].T
        xv_ref[k * page:(k + 1) * page, :] = r[LANES:2 * LANES, :].T
    _cpart_compute((xk_ref, xv_ref), pe_ref, w_ref, o_ref, nop * page // CMP_STRIDE)


def _cmp_partials_dense(kv, pe, w_c):
    b, t, _ = kv.shape
    rows = (t // CMP_STRIDE) * CMP_STRIDE
    n = rows // CMP_STRIDE
    return pl.pallas_call(
        functools.partial(_cpart_kernel, rows=rows),
        grid=(b,),
        in_specs=[
            pl.BlockSpec((None, rows, LANES), lambda i: (i, 0, 0)),
            pl.BlockSpec((None, rows, LANES), lambda i: (i, 0, 1)),
            pl.BlockSpec(pe.shape, lambda i: (0, 0, 0, 0)),
            pl.BlockSpec(w_c.shape, lambda i: (0, 0, 0, 0)),
        ],
        out_specs=pl.BlockSpec((None, n, 2 * KV_WIDTH), lambda i: (i, 0, 0)),
        out_shape=jax.ShapeDtypeStruct((b, n, 2 * KV_WIDTH), F32),
        compiler_params=_cparams("parallel"),
        name="cmp_partials",
    )(kv, kv, pe, w_c)


def _cmp_partials_paged(pool_t, page_table, pe, w_c, pages_per_step):
    b, n_pages = page_table.shape
    page = pool_t.shape[2]
    nop = pages_per_step
    n = nop * page // CMP_STRIDE

    def page_spec(k):
        return pl.BlockSpec((None, KV_WIDTH, page), lambda i, c, pt: (pt[i, c * nop + k], 0, 0))

    grid_spec = pltpu.PrefetchScalarGridSpec(
        num_scalar_prefetch=1,
        grid=(b, n_pages // nop),
        in_specs=[page_spec(k) for k in range(nop)] + [
            pl.BlockSpec(pe.shape, lambda i, c, pt: (0, 0, 0, 0)),
            pl.BlockSpec(w_c.shape, lambda i, c, pt: (0, 0, 0, 0)),
        ],
        out_specs=pl.BlockSpec((None, n, 2 * KV_WIDTH), lambda i, c, pt: (i, c, 0)),
        scratch_shapes=[pltpu.VMEM((nop * page, LANES), F32), pltpu.VMEM((nop * page, LANES), F32)],
    )
    return pl.pallas_call(
        functools.partial(_cpart_paged_kernel, nop=nop, page=page),
        grid_spec=grid_spec,
        out_shape=jax.ShapeDtypeStruct((b, n_pages * page // CMP_STRIDE, 2 * KV_WIDTH), F32),
        compiler_params=_cparams("parallel", "arbitrary"),
        name="cmp_partials_paged",
    )(page_table, *([pool_t] * nop), pe, w_c)


def _group_query_columns(q_ref, g, t):
    zeros64 = jnp.zeros((HEAD_DIM, t), BF16)
    cols = []
    for pr in range(2):
        qt = q_ref[:, (2 * g + pr) * LANES:(2 * g + pr + 1) * LANES].astype(F32).T.astype(BF16)
        for half in range(2):
            qh = qt[half * HEAD_DIM:(half + 1) * HEAD_DIM]
            cols.append(jnp.concatenate([qh, zeros64] if g == 0 else [zeros64, qh], axis=0))
    return jnp.concatenate(cols, axis=1)


def _store_group_output(o_ref, out_t, g, t):
    for pr in range(2):
        pair = jnp.concatenate([out_t[:, (2 * pr) * t:(2 * pr + 1) * t], out_t[:, (2 * pr + 1) * t:(2 * pr + 2) * t]],
                               axis=0)
        o_ref[:, (2 * g + pr) * LANES:(2 * g + pr + 1) * LANES] = pair.T


def _cattn_kernel(q_ref, p_ref, bias_ref, gn_ref, gk_ref, seg_ref, o_ref, ns0_ref, ns1_ref, *,
                  tq, n_sub, n_cmp, n_slc, nslp, pos0):
    qi = pl.program_id(0)
    pall = p_ref[...]
    kraw = pall[:, 0:LANES] + pltpu.roll(pall[:, LANES:2 * LANES], n_sub - 1, 0)
    vc = pall[:, 2 * LANES:3 * LANES] + pltpu.roll(pall[:, 3 * LANES:4 * LANES], n_sub - 1, 0)
    ms = _split_dot(kraw * kraw, seg_ref[...])
    kc = ((kraw * lax.rsqrt(ms + EPS)) * gk_ref[...]).astype(BF16)
    vct = vc.T.astype(BF16)

    nsel = -(-n_slc // 8) * 8
    jj = _row_iota((nsel, n_sub))
    nn = _lane_iota((nsel, n_sub))
    covers_t = ((nn * CMP_STRIDE < (jj + 1) * SLC_BLOCK) & (nn * CMP_STRIDE + CMP_BLOCK - 1 >= jj * SLC_BLOCK)
                & (nn < n_cmp) & (jj < n_slc))
    covers_t = jnp.where(covers_t, 1.0, 0.0).astype(BF16)
    qpos = pos0 + qi * tq + _lane_iota((1, tq))
    qblk = jnp.right_shift(qpos, SLC_BLOCK.bit_length() - 1)
    jr = _row_iota((nsel, tq))
    jrf = jr.astype(F32)
    forced = (jr == 0) | (jr == qblk) | (jr == qblk - 1)
    causal = jr <= qblk
    gnt = gn_ref[...].T

    for g in range(NSA_KV_HEADS):
        bias = bias_ref[g]
        s = _dot(kc, _group_query_columns(q_ref, g, tq)) + bias
        m = jnp.max(s, axis=0, keepdims=True)
        m = jnp.where(m > 0.5 * NEG, m, 0.0)
        e = jnp.where(bias > 0.5 * NEG, jnp.exp(s - m), 0.0)
        p = e / jnp.maximum(jnp.sum(e, axis=0, keepdims=True), 1e-30)
        h0 = NSA_HPG * g
        gate = jnp.concatenate([gnt[h0 + c:h0 + c + 1] for c in range(NSA_HPG)], axis=1)
        out_t = _dot(vct[g * HEAD_DIM:(g + 1) * HEAD_DIM], p.astype(BF16)) * gate
        _store_group_output(o_ref, out_t, g, tq)

        prsum = p[:, 0:tq] + p[:, tq:2 * tq] + p[:, 2 * tq:3 * tq] + p[:, 3 * tq:4 * tq]
        hi = prsum.astype(BF16)
        lo = (prsum - hi.astype(F32)).astype(BF16)
        imp = _dot(covers_t, hi) + _dot(covers_t, lo)
        score = jnp.where(forced, jnp.inf, imp)
        score = jnp.where(causal, score, -jnp.inf)
        sel = jnp.zeros((nsel, tq), F32)
        for _ in range(min(SLC_TOPK, n_slc)):
            m = jnp.max(score, axis=0, keepdims=True)
            idx = jnp.min(jnp.where(score == m, jrf, 1e9), axis=0, keepdims=True)
            pick = jrf == idx
            sel = jnp.where(pick & (m > -jnp.inf), 1.0, sel)
            score = jnp.where(pick, -jnp.inf, score)
        ns = 1.0 - sel
        if nslp > nsel:
            ns = jnp.concatenate([ns, jnp.ones((nslp - nsel, tq), F32)], axis=0)
        (ns0_ref if g == 0 else ns1_ref)[...] = ns


def _cmp_attention(q, parts, bias, gn, gk, *, tq, n_cmp, n_slc, pos0):
    b, s, _ = q.shape
    n_sub = parts.shape[1]
    nslp = -(-n_slc // LANES) * LANES
    bias_t = bias.reshape(NSA_KV_HEADS, NSA_HPG, s // tq, tq, n_sub).transpose(0, 4, 2, 1, 3)
    bias_t = bias_t.reshape(NSA_KV_HEADS, n_sub, NSA_HPG * s)
    kern = functools.partial(_cattn_kernel, tq=tq, n_sub=n_sub, n_cmp=n_cmp, n_slc=n_slc, nslp=nslp, pos0=pos0)
    return pl.pallas_call(
        kern,
        grid=(s // tq, b),
        in_specs=[
            pl.BlockSpec((None, tq, NSA_WIDTH), lambda i, j: (j, i, 0)),
            pl.BlockSpec((None, n_sub, 2 * KV_WIDTH), lambda i, j: (j, 0, 0)),
            pl.BlockSpec((NSA_KV_HEADS, n_sub, NSA_HPG * tq), lambda i, j: (0, 0, i)),
            pl.BlockSpec((None, tq, LANES), lambda i, j: (j, i, 0)),
            pl.BlockSpec((1, LANES), lambda i, j: (0, 0)),
            pl.BlockSpec((LANES, LANES), lambda i, j: (0, 0)),
        ],
        out_specs=[
            pl.BlockSpec((None, tq, NSA_WIDTH), lambda i, j: (j, i, 0)),
            pl.BlockSpec((None, nslp, tq), lambda i, j: (j, 0, i)),
            pl.BlockSpec((None, nslp, tq), lambda i, j: (j, 0, i)),
        ],
        out_shape=[
            jax.ShapeDtypeStruct((b, s, NSA_WIDTH), F32),
            jax.ShapeDtypeStruct((b, nslp, s), F32),
            jax.ShapeDtypeStruct((b, nslp, s), F32),
        ],
        compiler_params=_cparams("parallel", "parallel"),
        name="cmp_attention",
    )(q, parts, bias_t, gn, gk, _seg_matrix(LANES))


def _flash_kernel(q_ref, ns0_ref, ns1_ref, kv_ref, vt_ref, tab_ref, gn_ref, o_ref, *, t, use_sel, band, gate_base):
    qi = pl.program_id(1)
    row_k = _row_iota((t, LANES))
    lane_k = _lane_iota((t, LANES))
    lo_tile = jnp.maximum(qi - band, 0) if band is not None else 0
    gnt = gn_ref[...].T

    qts = []
    for g in range(NSA_KV_HEADS):
        qt_g = _group_query_columns(q_ref, g, t)
        if use_sel:
            nst = (ns0_ref if g == 0 else ns1_ref)[...].astype(BF16)
            qt_g = jnp.concatenate([qt_g, jnp.concatenate([nst] * NSA_HPG, axis=1)], axis=0)
        qts.append(qt_g)

    def body(kj, carry):
        k0 = pl.multiple_of(kj * t, t)
        kk = kv_ref[pl.ds(k0, t), 0:LANES].astype(BF16)
        if use_sel:
            blk = kj * (t // SLC_BLOCK) + jnp.right_shift(row_k, SLC_BLOCK.bit_length() - 1)
            onehot = jnp.where(lane_k == blk, -(2.0 ** 30), 0.0).astype(BF16)
            kk = jnp.concatenate([kk, onehot], axis=1)
        delta = qi - kj
        if band is None:
            kind = jnp.minimum(delta, 2)
        else:
            kind = jnp.where(delta < 2, delta, jnp.where(delta < band, 2, 3))
        new = []
        for g in range(NSA_KV_HEADS):
            m_old, l_old, acc = carry[g]
            s = _dot(kk, qts[g]) + tab_ref[g, kind]
            m_new = jnp.maximum(m_old, jnp.max(s, axis=0, keepdims=True))
            alpha = jnp.exp(m_old - m_new)
            p = jnp.exp(s - m_new)
            l_new = alpha * l_old + jnp.sum(p, axis=0, keepdims=True)
            vt = vt_ref[g * HEAD_DIM:(g + 1) * HEAD_DIM, pl.ds(k0, t)].astype(BF16)
            new.append((m_new, l_new, acc * alpha + _dot(vt, p.astype(BF16))))
        return tuple(new)

    init = (jnp.full((1, NSA_HPG * t), NEG, F32), jnp.zeros((1, NSA_HPG * t), F32),
            jnp.zeros((HEAD_DIM, NSA_HPG * t), F32))
    fin = lax.fori_loop(lo_tile, qi + 1, body, (init, init))
    for g in range(NSA_KV_HEADS):
        _, l_fin, acc = fin[g]
        h0 = gate_base + NSA_HPG * g
        gate = jnp.concatenate([gnt[h0 + c:h0 + c + 1] for c in range(NSA_HPG)], axis=1)
        _store_group_output(o_ref, acc * (gate / l_fin), g, t)


def _flash_attention(q, ns0, ns1, kv, vt, tab, gn, *, use_sel, band, gate_base):
    b, s, _ = q.shape
    t = ATT_TILE
    assert ns0.shape[1] == LANES
    kern = functools.partial(_flash_kernel, t=t, use_sel=use_sel, band=band, gate_base=gate_base)
    tile = lambda w: pl.BlockSpec((None, t, w), lambda i, j: (i, j, 0))
    ns_tile = pl.BlockSpec((None, LANES, t), lambda i, j: (i, 0, j))
    return pl.pallas_call(
        kern,
        grid=(b, s // t),
        in_specs=[
            tile(NSA_WIDTH), ns_tile, ns_tile,
            pl.BlockSpec((None, s, KV_WIDTH), lambda i, j: (i, 0, 0)),
            pl.BlockSpec((LANES, s), lambda i, j: (0, i)),
            pl.BlockSpec(tab.shape, lambda i, j: (0, 0, 0, 0)),
            tile(LANES),
        ],
        out_specs=tile(NSA_WIDTH),
        out_shape=jax.ShapeDtypeStruct((b, s, NSA_WIDTH), F32),
        compiler_params=_cparams("parallel", "parallel"),
        name="flash_sel" if use_sel else "flash_win",
    )(q, ns0, ns1, kv, vt, tab, gn)


def _memattn_kernel(q_ref, kv_ref, o_ref):
    lane = _lane_iota((kv_ref.shape[0], LANES))
    for pr in range(MEM_HEADS // 2):
        qpair = q_ref[:, pr * LANES:(pr + 1) * LANES]
        kblk = kv_ref[:, pr * LANES:(pr + 1) * LANES]
        vblk = kv_ref[:, MEM_WIDTH + pr * LANES:MEM_WIDTH + (pr + 1) * LANES]
        out = None
        for half in range(2):
            keep = (lane < HEAD_DIM) if half == 0 else (lane >= HEAD_DIM)
            kk = jnp.where(keep, kblk, 0.0).astype(BF16)
            vv = jnp.where(keep, vblk, 0.0).astype(BF16)
            s = _dot_nt(qpair, kk)
            m = jnp.max(s, axis=1, keepdims=True)
            e = jnp.exp(s - m)
            p = e / jnp.sum(e, axis=1, keepdims=True)
            o = _dot(p.astype(BF16), vv)
            out = o if out is None else out + o
        o_ref[:, pr * LANES:(pr + 1) * LANES] = out


def _mem_attention(qm, mem_kv, tq):
    b, s, _ = qm.shape
    m = mem_kv.shape[1]
    return pl.pallas_call(
        _memattn_kernel,
        grid=(b, s // tq),
        in_specs=[
            pl.BlockSpec((None, tq, MEM_WIDTH), lambda i, j: (i, j, 0)),
            pl.BlockSpec((None, m, 2 * MEM_WIDTH), lambda i, j: (i, 0, 0)),
        ],
        out_specs=pl.BlockSpec((None, tq, MEM_WIDTH), lambda i, j: (i, j, 0)),
        out_shape=jax.ShapeDtypeStruct((b, s, MEM_WIDTH), F32),
        compiler_params=_cparams("parallel", "parallel"),
        name="mem_attention",
    )(qm, mem_kv)


def _dec_kernel(*refs, n_pg, pps, has_new):
    bidx_ref = refs[2]
    page_refs = refs[3:3 + pps]
    new_ref, wq_ref, bias_ref, ns_ref, gate_ref, o_ref, acc_ref, m_ref, l_ref = refs[3 + pps:]
    c = pl.program_id(1)
    n_chunks = pl.num_programs(1)

    @pl.when(c == 0)
    def _():
        acc_ref[...] = jnp.zeros_like(acc_ref)
        m_ref[...] = jnp.full_like(m_ref, NEG)
        l_ref[...] = jnp.zeros_like(l_ref)

    rk = _row_iota((LANES, LANES))

    def step(tiles, first_page):
        feats = (tiles[0] if len(tiles) == 1 else jnp.concatenate(tiles, axis=1)).astype(BF16)
        s = lax.dot_general(feats, wq_ref[...], (((0,), (1,)), ((), ())),
                            preferred_element_type=F32)
        extra = []
        for k in range(len(tiles)):
            pg = first_page + k
            ns = jnp.where(rk < SLC_BLOCK, ns_ref[pl.ds(2 * pg, 1), :], ns_ref[pl.ds(2 * pg + 1, 1), :])
            extra.append(jnp.where(ns > 0.5, NEG, bias_ref[bidx_ref[pg]]))
        s = s + (extra[0] if len(extra) == 1 else jnp.concatenate(extra, axis=0))
        m_old = m_ref[...]
        m_new = jnp.maximum(m_old, jnp.max(s, axis=0, keepdims=True))
        alpha = jnp.exp(m_old - m_new)
        p = jnp.exp(s - m_new)
        l_ref[...] = alpha * l_ref[...] + jnp.sum(p, axis=0, keepdims=True)
        m_ref[...] = m_new
        acc_ref[...] = acc_ref[...] * alpha + _dot(feats, p.astype(BF16))

    if has_new:
        @pl.when(c < n_chunks - 1)
        def _():
            step([r[...] for r in page_refs], c * pps)

        @pl.when(c == n_chunks - 1)
        def _():
            step([new_ref[...]], n_pg)
    else:
        step([r[...] for r in page_refs], c * pps)

    @pl.when(c == n_chunks - 1)
    def _():
        o_ref[...] = acc_ref[...] / l_ref[...] * gate_ref[...]


def _decode_attention(pages, phys, lblk, new_rows, wq, bias_tab, bias_idx, notsel, gate):
    bsz, n_pg = phys.shape
    w = pages.shape[1]
    has_new = new_rows is not None
    pps = math.gcd(n_pg, DEC_PAGES_PER_STEP)
    n_steps = n_pg // pps
    n_chunks = n_steps + (1 if has_new else 0)
    if not has_new:
        new_rows = jnp.zeros((1, w, LANES), F32)
    new_map = (lambda i, c, ph, lb, bi: (i, 0, 0)) if has_new else (lambda i, c, ph, lb, bi: (0, 0, 0))

    def page_spec(k):
        def index(i, c, ph, lb, bi):
            pg = jnp.minimum(c, n_steps - 1) * pps + k
            return (ph[i, pg], 0, lb[i, pg])
        return pl.BlockSpec((None, w, LANES), index)

    per_b = lambda i, c, ph, lb, bi: (i, 0, 0)
    grid_spec = pltpu.PrefetchScalarGridSpec(
        num_scalar_prefetch=3,
        grid=(bsz, n_chunks),
        in_specs=[page_spec(k) for k in range(pps)] + [
            pl.BlockSpec((None, w, LANES), new_map),
            pl.BlockSpec((None, LANES, w), per_b),
            pl.BlockSpec(bias_tab.shape, lambda i, c, ph, lb, bi: (0, 0, 0)),
            pl.BlockSpec((None, notsel.shape[1], LANES), per_b),
            pl.BlockSpec((None, 1, LANES), per_b),
        ],
        out_specs=pl.BlockSpec((None, w, LANES), per_b),
        scratch_shapes=[pltpu.VMEM((w, LANES), F32), pltpu.VMEM((1, LANES), F32), pltpu.VMEM((1, LANES), F32)],
    )
    return pl.pallas_call(
        functools.partial(_dec_kernel, n_pg=n_pg, pps=pps, has_new=has_new),
        grid_spec=grid_spec,
        out_shape=jax.ShapeDtypeStruct((bsz, w, LANES), F32),
        compiler_params=_cparams("parallel", "arbitrary"),
        name="decode_attention",
    )(phys, lblk, bias_idx, *([pages] * pps), new_rows, wq, bias_tab, notsel, gate)


def _feature_major(cache):
    n, rows = cache.shape[:2]
    return cache.transpose(0, 2, 3, 4, 1).reshape(n, -1, rows)


def _tail_kernel(x_ref, g1_ref, op_ref, oc_ref, os_ref, ow_ref, om_ref, cnt0_ref, wgb_ref, wup_p_ref, wup_n_ref,
                 wup_m_ref, wout_ref, g2_ref, rwh_ref, rwl_ref, rb_ref, x2_ref, h2_ref, ei_ref, gt_ref, cnt_ref):
    @pl.when(pl.program_id(0) == 0)
    def _():
        cnt_ref[...] = cnt0_ref[...]

    x = x_ref[...]
    h = _rms(x, g1_ref[...]).astype(BF16)
    onsa = (oc_ref[...] + os_ref[...] + ow_ref[...]).astype(BF16)
    ups = (_dot(op_ref[...].astype(BF16), wup_p_ref[...]), _dot(onsa, wup_n_ref[...]),
           _dot(om_ref[...].astype(BF16), wup_m_ref[...]))
    mixed = None
    for br in range(3):
        gb = _sigmoid(_dot(h, wgb_ref[:, br * D_MODEL:(br + 1) * D_MODEL]))
        mixed = gb * ups[br] if mixed is None else mixed + gb * ups[br]
    x2 = x + _dot(mixed.astype(BF16), wout_ref[...])
    x2_ref[...] = x2
    h2 = _rms(x2, g2_ref[...])
    h2_ref[...] = h2
    hi = h2.astype(BF16)
    lo = (h2 - hi.astype(F32)).astype(BF16)
    logits = _dot(hi, rwh_ref[...]) + _dot(lo, rwh_ref[...]) + _dot(hi, rwl_ref[...]) + rb_ref[...]
    lane = _lane_iota(logits.shape)
    lanef = lane.astype(F32)
    tops, idxs = [], []
    for _ in range(TOP_K):
        m = jnp.max(logits, axis=1, keepdims=True)
        idx = jnp.min(jnp.where(logits == m, lanef, 1e9), axis=1, keepdims=True)
        logits = jnp.where(lanef == idx, -jnp.inf, logits)
        tops.append(m)
        idxs.append(idx)
    es = [jnp.exp(tk - tops[0]) for tk in tops]
    den = es[0] + es[1] + es[2] + es[3]
    tm = logits.shape[0]
    onehot = jnp.zeros(logits.shape, F32)
    for k in range(TOP_K):
        onehot = jnp.where(lanef == idxs[k], 1.0, onehot)
    tri = jnp.where(_row_iota((tm, tm)) > _lane_iota((tm, tm)), 1.0, 0.0).astype(BF16)
    before = _dot(tri, onehot.astype(BF16)) + cnt_ref[...]
    cnt_ref[...] = cnt_ref[...] + jnp.sum(onehot, axis=0, keepdims=True)
    ei = jnp.zeros(logits.shape, F32)
    gt = jnp.zeros(logits.shape, F32)
    for k in range(TOP_K):
        rank = jnp.sum(jnp.where(lanef == idxs[k], before, 0.0), axis=1, keepdims=True)
        ei = jnp.where(lane == k, idxs[k], ei)
        ei = jnp.where(lane == TOP_K + k, rank, ei)
        gt = jnp.where(lane == k, es[k] / den, gt)
    ei_ref[...] = ei.astype(jnp.int32)
    gt_ref[...] = gt


def _layer_tail(x, o_pool, o_cmp, o_slc, o_win, o_mem, cnt0, w, tm):
    n = x.shape[0]
    row = lambda wd: pl.BlockSpec((tm, wd), lambda i: (i, 0))
    full = lambda a: pl.BlockSpec(a.shape, lambda i: (0,) * a.ndim)
    weights = (w["wgb"], w["wup_pool"], w["wup_nsa"], w["wup_mem"], w["wout"], w["g2"], w["rw_hi"], w["rw_lo"],
               w["rb"])
    return pl.pallas_call(
        _tail_kernel,
        grid=(n // tm,),
        in_specs=[row(D_MODEL), full(w["g1"]), row(POOL_WIDTH), row(NSA_WIDTH), row(NSA_WIDTH), row(NSA_WIDTH),
                  row(MEM_WIDTH), full(cnt0)] + [full(a) for a in weights],
        out_specs=[row(D_MODEL), row(D_MODEL), row(LANES), row(LANES), full(cnt0)],
        out_shape=[jax.ShapeDtypeStruct((n, D_MODEL), F32), jax.ShapeDtypeStruct((n, D_MODEL), F32),
                   jax.ShapeDtypeStruct((n, LANES), jnp.int32), jax.ShapeDtypeStruct((n, LANES), F32),
                   jax.ShapeDtypeStruct((1, LANES), F32)],
        compiler_params=_cparams("arbitrary"),
        name="layer_tail",
    )(x, w["g1"], o_pool, o_cmp, o_slc, o_win, o_mem, cnt0, *weights)


def _ffn_kernel(be_ref, nu_ref, x_ref, wgu_ref, bgu_ref, wd_ref, bd_ref, o_ref, wgu_bf, wd_bf):
    i = pl.program_id(0)

    @pl.when((i == 0) | (be_ref[i] != be_ref[jnp.maximum(i - 1, 0)]))
    def _():
        wgu_bf[...] = wgu_ref[...].astype(BF16)
        wd_bf[...] = wd_ref[...].astype(BF16)

    @pl.when(i < nu_ref[0])
    def _():
        gu = _dot(x_ref[...].astype(BF16), wgu_bf[...]) + bgu_ref[...]
        gate = jnp.minimum(gu[:, :D_FF], SWIGLU_LIMIT)
        up = jnp.clip(gu[:, D_FF:], -SWIGLU_LIMIT, SWIGLU_LIMIT)
        act = gate * _sigmoid(SWIGLU_ALPHA * gate) * (up + 1.0)
        o_ref[...] = _dot(act.astype(BF16), wd_bf[...]) + bd_ref[...]

    @pl.when(i >= nu_ref[0])
    def _():
        o_ref[...] = jnp.zeros_like(o_ref)


def _expert_ffn(rows, blk_e, n_used, wgu, bgu, wd, bd):
    n_rows = rows.shape[0]
    n_blocks = n_rows // MOE_TILE
    blk = lambda i, be, nu: (jnp.minimum(i, nu[0] - 1), 0)
    grid_spec = pltpu.PrefetchScalarGridSpec(
        num_scalar_prefetch=2,
        grid=(n_blocks,),
        in_specs=[
            pl.BlockSpec((MOE_TILE, D_MODEL), blk),
            pl.BlockSpec((None, D_MODEL, 2 * D_FF), lambda i, be, nu: (be[i], 0, 0)),
            pl.BlockSpec((None, 1, 2 * D_FF), lambda i, be, nu: (be[i], 0, 0)),
            pl.BlockSpec((None, D_FF, D_MODEL), lambda i, be, nu: (be[i], 0, 0)),
            pl.BlockSpec((None, 1, D_MODEL), lambda i, be, nu: (be[i], 0, 0)),
        ],
        out_specs=pl.BlockSpec((MOE_TILE, D_MODEL), lambda i, be, nu: (i, 0)),
        scratch_shapes=[pltpu.VMEM((D_MODEL, 2 * D_FF), BF16), pltpu.VMEM((D_FF, D_MODEL), BF16)],
    )
    return pl.pallas_call(
        _ffn_kernel,
        grid_spec=grid_spec,
        out_shape=jax.ShapeDtypeStruct((n_rows, D_MODEL), F32),
        compiler_params=pltpu.CompilerParams(dimension_semantics=("arbitrary",), vmem_limit_bytes=FFN_VMEM_LIMIT),
        name="expert_ffn",
    )(blk_e, n_used, rows, wgu, bgu, wd, bd)


def _token_tile(n):
    return next(t for t in (MOE_DMA_TOKENS, 384, 256, 128, n) if n % t == 0)


def _dispatch_kernel(dest_ref, h_ref, buf_ref, rows_ref, sem, *, td):
    del buf_ref

    def issue(i, carry):
        t0 = pl.multiple_of(i * 8, 8)
        for r in range(8):
            for k in range(TOP_K):
                d = dest_ref[i * (8 * TOP_K) + r * TOP_K + k]
                pltpu.make_async_copy(h_ref.at[pl.ds(t0 + r, 1)], rows_ref.at[pl.ds(d, 1)], sem).start()
        return carry

    lax.fori_loop(0, td // 8, issue, 0)
    for k in range(TOP_K):
        pltpu.make_async_copy(h_ref, rows_ref.at[pl.ds(0, td)], sem).wait()


def _moe_dispatch(rows_buf, h2, dest):
    n = h2.shape[0]
    td = _token_tile(n)
    return pl.pallas_call(
        functools.partial(_dispatch_kernel, td=td),
        grid=(n // td,),
        in_specs=[pl.BlockSpec((td * TOP_K,), lambda i: (i,), memory_space=pltpu.SMEM),
                  pl.BlockSpec((td, D_MODEL), lambda i: (i, 0)),
                  pl.BlockSpec(memory_space=pl.ANY)],
        out_specs=pl.BlockSpec(memory_space=pl.ANY),
        out_shape=jax.ShapeDtypeStruct(rows_buf.shape, rows_buf.dtype),
        scratch_shapes=[pltpu.SemaphoreType.DMA(())],
        input_output_aliases={2: 0},
        compiler_params=_cparams("arbitrary"),
        name="moe_dispatch",
    )(dest, h2, rows_buf)


def _combine_kernel(dest_ref, x2_ref, g_ref, rows_ref, o_ref, ybuf, sem, *, td):
    def issue(i, carry):
        t0 = pl.multiple_of(i * 8, 8)
        for r in range(8):
            for k in range(TOP_K):
                d = dest_ref[i * (8 * TOP_K) + r * TOP_K + k]
                pltpu.make_async_copy(rows_ref.at[pl.ds(d, 1)], ybuf.at[k, pl.ds(t0 + r, 1)], sem).start()
        return carry

    lax.fori_loop(0, td // 8, issue, 0)
    for k in range(TOP_K):
        pltpu.make_async_copy(rows_ref.at[pl.ds(0, td)], ybuf.at[k], sem).wait()
    out = x2_ref[...]
    for k in range(TOP_K):
        out = out + g_ref[:, k:k + 1] * ybuf[k]
    o_ref[...] = out


def _moe_combine(x2, out_rows, dest, gates):
    n = x2.shape[0]
    td = _token_tile(n)
    return pl.pallas_call(
        functools.partial(_combine_kernel, td=td),
        grid=(n // td,),
        in_specs=[pl.BlockSpec((td * TOP_K,), lambda i: (i,), memory_space=pltpu.SMEM),
                  pl.BlockSpec((td, D_MODEL), lambda i: (i, 0)),
                  pl.BlockSpec((td, LANES), lambda i: (i, 0)),
                  pl.BlockSpec(memory_space=pl.ANY)],
        out_specs=pl.BlockSpec((td, D_MODEL), lambda i: (i, 0)),
        out_shape=jax.ShapeDtypeStruct((n, D_MODEL), F32),
        scratch_shapes=[pltpu.VMEM((TOP_K, td, D_MODEL), F32), pltpu.SemaphoreType.DMA(())],
        compiler_params=_cparams("arbitrary"),
        name="moe_combine",
    )(dest, x2, gates, out_rows)


def _moe(groups, counts, w):
    n_total = sum(g[0].shape[0] for g in groups)
    cnt = counts[0, :N_EXPERTS].astype(jnp.int32)
    padded = (cnt + MOE_TILE - 1) // MOE_TILE * MOE_TILE
    pad_ends = jnp.cumsum(padded)
    pad_starts = pad_ends - padded
    n_blocks = -(-n_total * TOP_K // MOE_TILE) + N_EXPERTS
    blk_start = jnp.arange(n_blocks, dtype=jnp.int32) * MOE_TILE
    blk_e = jnp.minimum(jnp.sum(blk_start[:, None] >= pad_ends[None, :], axis=1), N_EXPERTS - 1).astype(jnp.int32)
    n_used = (pad_ends[-1] // MOE_TILE).astype(jnp.int32).reshape(1)
    dests = [(pad_starts[er[:, :TOP_K]] + er[:, TOP_K:2 * TOP_K]).astype(jnp.int32).reshape(-1)
             for (_, _, er, _) in groups]
    rows = jnp.zeros((n_blocks * MOE_TILE, D_MODEL), F32)
    for (_, h2, _, _), dest in zip(groups, dests):
        rows = _moe_dispatch(rows, h2, dest)
    out_rows = _expert_ffn(rows, blk_e, n_used, w["wgu"], w["bgu"], w["wd"], w["bd"])
    return [_moe_combine(x2, out_rows, dest, gates) for (x2, _, _, gates), dest in zip(groups, dests)]


def _rel_bucket(dist):
    n = jnp.maximum(dist, 0)
    max_exact = NUM_BUCKETS // 2
    nf = jnp.maximum(n, 1).astype(F32)
    large = max_exact + (jnp.log(nf / max_exact) / math.log(MAX_DISTANCE / max_exact)
                         * (NUM_BUCKETS - max_exact)).astype(jnp.int32)
    large = jnp.minimum(large, NUM_BUCKETS - 1)
    return jnp.where(n < max_exact, n, large)


def _bias_of(rel_bias, dist, valid):
    return jnp.where(valid[..., None], rel_bias[_rel_bucket(dist)], NEG)


_PROJ_SEGS = ((0, 512, "qscale"), (512, 256, "qscale"), (768, 256, "kv"), (1024, 256, "kv"), (1280, 256, "id"),
              (1536, 256, "id"), (1792, 128, "sigmoid"))
_PROJ_DTYPES = (BF16, BF16, F32, F32, F32, F32, F32)
_PROJ_NNORM = 1280


def _prep_layer(l, rel_bias, norm1_g, w_in, nsa_qk_norm, mem_qk_norm, cmp_w, cmp_pe, pool_w, pool_scale,
                mem_norm_g, w_mem_kv, w_up_pool, w_up_nsa, w_up_mem, w_out, norm2_g, router_w, router_b,
                w_gu, b_gu, w_down, b_down):
    wi = w_in[l]
    o_u, o_q, o_qm, o_kvc, o_kvs, o_kvw, o_gn, o_gb = 0, 256, 768, 1024, 1280, 1536, 1792, 1816
    w_proj = jnp.concatenate([
        wi[:, o_q:o_q + 512], wi[:, o_qm:o_qm + 256], wi[:, o_kvs:o_kvs + 256], wi[:, o_kvw:o_kvw + 256],
        wi[:, o_kvc:o_kvc + 256], wi[:, o_u:o_u + 256], wi[:, o_gn:o_gn + 24],
        jnp.zeros((D_MODEL, LANES - 24), F32)], axis=1).astype(BF16)
    nq, mq = nsa_qk_norm[l], mem_qk_norm[l]
    ones = jnp.ones((LANES,), F32)
    gain = jnp.concatenate([jnp.tile(nq[0], 8), jnp.tile(mq[0], 4), jnp.tile(nq[2], 2), ones,
                            jnp.tile(nq[3], 2), ones])[None, :]
    nmask = jnp.concatenate([jnp.ones((768,), F32), ones, 0 * ones, ones, 0 * ones])[None, :]
    eye4 = jnp.eye(4, dtype=F32)
    cw = cmp_w[l].reshape(2, 2, CMP_STRIDE, HEAD_DIM, HEAD_DIM)
    w_c = jnp.einsum("crjde,xy->cjxdrye", cw, jnp.eye(2, dtype=F32))
    w_c = w_c.reshape(2, CMP_STRIDE, LANES, KV_WIDTH).astype(BF16)
    pe = cmp_pe[l].reshape(2, 2, CMP_STRIDE, HEAD_DIM)
    pe_c = jnp.tile(pe.transpose(0, 2, 1, 3), (1, 1, 1, NSA_KV_HEADS))
    pe_c = jnp.pad(pe_c, ((0, 0), (0, 0), (0, 6), (0, 0)))
    w_pool = jnp.einsum("gde,gh->gdhe", pool_w[l], eye4).reshape(POOL_WIDTH, POOL_WIDTH).astype(BF16)
    rw = jnp.pad(router_w[l], ((0, 0), (0, LANES - N_EXPERTS)))
    rw_hi = rw.astype(BF16)
    rw_lo = (rw - rw_hi.astype(F32)).astype(BF16)
    rb = jnp.concatenate([router_b[l], jnp.full((LANES - N_EXPERTS,), NEG, F32)])[None, :]
    return {
        "g1": norm1_g[l][None, :], "w_proj": w_proj, "gain": gain, "nmask": nmask,
        "gk_cmp": jnp.tile(nq[1], 2)[None, :], "w_c": w_c, "pe_c": pe_c,
        "w_pool": w_pool, "pool_scale": pool_scale[l][None, :],
        "mem_g": mem_norm_g[l][None, :], "w_mem": w_mem_kv[l].astype(BF16),
        "mem_gain": jnp.concatenate([jnp.tile(mq[1], 4), jnp.ones((256,), F32)])[None, :],
        "mem_nmask": jnp.concatenate([jnp.ones((256,), F32), jnp.zeros((256,), F32)])[None, :],
        "wgb": wi[:, o_gb:o_gb + 3 * D_MODEL].astype(BF16),
        "wup_pool": w_up_pool[l].astype(BF16), "wup_nsa": w_up_nsa[l].astype(BF16),
        "wup_mem": w_up_mem[l].astype(BF16), "wout": w_out[l].astype(BF16), "g2": norm2_g[l][None, :],
        "rw_hi": rw_hi, "rw_lo": rw_lo, "rb": rb,
        "wgu": w_gu[l], "bgu": b_gu[l][:, None, :], "wd": w_down[l],
        "bd": b_down[l][:, None, :],
    }


def _project_in(x2d, w, tm):
    return _project(x2d, w["g1"], w["w_proj"], w["gain"], w["nmask"], _PROJ_SEGS, _PROJ_DTYPES, _PROJ_NNORM, tm)


def _toeplitz(v, t):
    lead = v.shape[:-1]
    flat = jnp.tile(v, (1,) * len(lead) + (t,))[..., t:t + t * (2 * t - 1)]
    return flat.reshape(lead + (t, 2 * t - 1))[..., :t]


def _flash_tables(rel_bias):
    t = ATT_TILE
    d0 = jnp.arange(-t, t)
    kinds = jnp.stack([
        _bias_of(rel_bias, d0, d0 >= 0),
        _bias_of(rel_bias, d0 + t, d0 + t >= 0),
        _bias_of(rel_bias, jnp.full((2 * t,), 2 * t), jnp.ones((2 * t,), bool)),
        _bias_of(rel_bias, d0 + WINDOW, d0 + WINDOW < WINDOW),
    ])
    tab = _toeplitz(kinds.transpose(2, 0, 1), t)
    tab = tab.reshape(NSA_KV_HEADS, NSA_HPG, 4, t, t)
    return tab.transpose(0, 2, 3, 1, 4).reshape(NSA_KV_HEADS, 4, t, NSA_HPG * t)


def _cmp_bias_table(rel_bias, s, n_sub, n_cmp, pos0):
    na = s // CMP_STRIDE
    m = max(na, n_sub)
    k = jnp.arange(-m, m)[None, :]
    r = jnp.arange(CMP_STRIDE)[:, None]
    d = CMP_STRIDE * k + r - (CMP_BLOCK - 1) + pos0
    v = _bias_of(rel_bias, d, d >= 0).transpose(2, 0, 1)
    tz = _toeplitz(v, m)[:, :, :n_sub, :na]
    tab = tz.transpose(0, 3, 1, 2).reshape(NSA_HEADS, s, n_sub)
    return jnp.where(jnp.arange(n_sub)[None, None, :] < n_cmp, tab, NEG)


def _prompt_pre(x, mem, w, rel_bias, cnt0):
    b, s, _ = x.shape
    n = b * s
    tm = 512 if n % 512 == 0 else ATT_TILE
    x2d = x.reshape(n, D_MODEL)
    q, qm, kvs, vts, kvw, vtw, kvc, u, gn = _project_in(x2d, w, tm)
    r3 = lambda a: a.reshape(b, s, a.shape[-1])
    q, qm, kvs, kvw, kvc, u, gn = map(r3, (q, qm, kvs, kvw, kvc, u, gn))

    o_pool = _pool_mix(u, jnp.zeros((b, 16, POOL_WIDTH), F32), w["w_pool"], w["pool_scale"], 0)

    n_cmp = (s - CMP_BLOCK) // CMP_STRIDE + 1
    n_slc = -(-s // SLC_BLOCK)
    parts = _cmp_partials_dense(kvc, w["pe_c"], w["w_c"])
    n_sub = parts.shape[1]
    bias_c = _cmp_bias_table(rel_bias, s, n_sub, n_cmp, 0)
    tq = 256 if s % 256 == 0 else ATT_TILE
    o_cmp, ns0, ns1 = _cmp_attention(q, parts, bias_c, gn, w["gk_cmp"], tq=tq, n_cmp=n_cmp, n_slc=n_slc, pos0=0)

    tab = _flash_tables(rel_bias)
    o_slc = _flash_attention(q, ns0, ns1, kvs, vts, tab, gn, use_sel=True, band=None, gate_base=NSA_HEADS)
    o_win = _flash_attention(q, ns0, ns1, kvw, vtw, tab, gn, use_sel=False, band=WINDOW // ATT_TILE,
                             gate_base=2 * NSA_HEADS)

    m = mem.shape[1]
    (mem_kv,) = _project(mem.reshape(b * m, D_MODEL), w["mem_g"], w["w_mem"], w["mem_gain"], w["mem_nmask"],
                         ((0, 2 * MEM_WIDTH, "id"),), (F32,), MXU_DIM, tm=min(512, b * m))
    mem_kv = mem_kv.reshape(b, m, 2 * MEM_WIDTH)
    o_mem = _mem_attention(qm, mem_kv, tq=min(512, s))

    f2 = lambda a: a.reshape(n, a.shape[-1])
    x2, h2, eidx, gates, cnt = _layer_tail(x2d, f2(o_pool), f2(o_cmp), f2(o_slc), f2(o_win), f2(o_mem), cnt0, w, tm)
    kvshape = (b, s, 2, NSA_KV_HEADS, HEAD_DIM)
    win_buf = WINDOW
    states = (kvc.reshape(kvshape), kvs.reshape(kvshape),
              _last_rows(kvw, win_buf).reshape(b, win_buf, 2, NSA_KV_HEADS, HEAD_DIM),
              mem_kv.reshape(b, m, 2, MEM_HEADS, HEAD_DIM), _last_rows(u, POOL_BUF))
    return (x2, h2, eidx, gates), states, cnt


def _last_rows(a, n):
    t = a.shape[1]
    if t < n:
        a = jnp.pad(a, [(0, 0), (n - t, 0)] + [(0, 0)] * (a.ndim - 2))
    return a[:, a.shape[1] - n:]


def _dec_columns_nsa(q):
    b, t, _ = q.shape
    qh = q.reshape(b, t, NSA_KV_HEADS, NSA_HPG, HEAD_DIM)
    w = jnp.einsum("btgpd,gx->bxdgtp", qh.astype(F32), jnp.eye(NSA_KV_HEADS, dtype=F32))
    w = w.reshape(b, NSA_KV_HEADS * HEAD_DIM, NSA_KV_HEADS * t * NSA_HPG)
    return jnp.pad(w, ((0, 0), (0, KV_WIDTH - w.shape[1]), (0, LANES - w.shape[2]))).astype(BF16)


def _dec_extract_nsa(o, t):
    b = o.shape[0]
    v = o[:, LANES:, :NSA_KV_HEADS * t * NSA_HPG]
    v = v.reshape(b, NSA_KV_HEADS, HEAD_DIM, NSA_KV_HEADS, t, NSA_HPG)
    v = jnp.einsum("bxdgtp,gx->btgpd", v, jnp.eye(NSA_KV_HEADS, dtype=F32))
    return v.reshape(b, t, NSA_WIDTH)


def _dec_bias_cols(bias_tph):
    k, t, _ = bias_tph.shape
    bt = bias_tph.reshape(k, t, NSA_KV_HEADS, NSA_HPG).transpose(0, 2, 1, 3).reshape(k, NSA_KV_HEADS * t * NSA_HPG)
    return jnp.pad(bt, ((0, 0), (0, LANES - bt.shape[1])))


def _dec_gate_cols(gn, base, t):
    b = gn.shape[0]
    gt = gn[:, :, base:base + NSA_HEADS].reshape(b, t, NSA_KV_HEADS, NSA_HPG).transpose(0, 2, 1, 3)
    gt = gt.reshape(b, 1, NSA_KV_HEADS * t * NSA_HPG)
    return jnp.pad(gt, ((0, 0), (0, 0), (0, LANES - gt.shape[2])), constant_values=1.0)


def _sample_pre(x, cache_cmp, cache_slc, cache_win, cache_mem, pool_buf, page_table, w, rel_bias, cnt0):
    b, t, _ = x.shape
    n = b * t
    page = cache_cmp.shape[1]
    n_pages = page_table.shape[1]
    past = n_pages * page
    x2d = x.reshape(n, D_MODEL)
    q, qm, kvs, _, kvw, _, kvc, u, gn = _project_in(x2d, w, n if n <= 512 else ATT_TILE)
    r3 = lambda a: a.reshape(b, t, a.shape[-1])
    q, qm, kvs, kvw, kvc, u, gn = map(r3, (q, qm, kvs, kvw, kvc, u, gn))
    qpos = past + jnp.arange(t)

    buf16 = jnp.pad(pool_buf, ((0, 0), (16 - POOL_BUF, 0), (0, 0)))
    o_pool = _pool_mix(u, buf16, w["w_pool"], w["pool_scale"], past)

    total = past + t
    n_cmp = (total - CMP_BLOCK) // CMP_STRIDE + 1
    n_sub_used = n_cmp + CMP_BLOCK // CMP_STRIDE - 1
    n_slc = -(-total // SLC_BLOCK)
    pps = math.gcd(n_pages, 16)
    parts = _cmp_partials_paged(_feature_major(cache_cmp), page_table, w["pe_c"], w["w_c"], pps)
    extra = n_sub_used * CMP_STRIDE - past
    if extra > 0:
        tail_rows = -(-extra // CMP_STRIDE) * CMP_STRIDE
        new_c = jnp.pad(kvc, ((0, 0), (0, max(0, tail_rows - t)), (0, 0)))[:, :tail_rows]
        parts = jnp.concatenate([parts, _cmp_partials_dense(new_c, w["pe_c"], w["w_c"])], axis=1)
    n_sub = parts.shape[1]
    end = jnp.arange(n_sub)[None, :] * CMP_STRIDE + CMP_BLOCK - 1
    bias_c = _bias_of(rel_bias, qpos[:, None] - end, (end <= qpos[:, None]) & (jnp.arange(n_sub)[None, :] < n_cmp))
    qpad = ((0, 0), (0, LANES - t), (0, 0))
    bias_c = jnp.pad(bias_c.transpose(2, 0, 1), qpad)
    o_cmp, ns0, ns1 = _cmp_attention(jnp.pad(q, qpad), parts, bias_c, jnp.pad(gn, qpad), w["gk_cmp"], tq=LANES,
                                     n_cmp=n_cmp, n_slc=n_slc, pos0=past)
    o_cmp = o_cmp[:, :t]

    wq = _dec_columns_nsa(q).transpose(0, 2, 1)
    ncol = NSA_KV_HEADS * t * NSA_HPG
    new_tile = lambda kv: jnp.pad(kv, ((0, 0), (0, LANES - t), (0, 0))).transpose(0, 2, 1)
    own = lambda npg: jnp.broadcast_to(jnp.arange(b, dtype=jnp.int32)[:, None], (b, npg))
    blocks = lambda npg: jnp.broadcast_to(jnp.arange(npg, dtype=jnp.int32)[None, :], (b, npg))

    n_chunks = n_pages + 1
    nblk = -(-2 * n_chunks // 8) * 8
    ns = jnp.stack([ns0, ns1], axis=1)[:, :, :, :t].transpose(0, 1, 3, 2)
    ns = jnp.pad(ns, ((0, 0), (0, 0), (0, 0), (0, max(0, nblk - ns.shape[3]))))[..., :nblk]
    ns = jnp.broadcast_to(ns[:, :, :, None, :], (b, NSA_KV_HEADS, t, NSA_HPG, nblk)).reshape(b, ncol, nblk)
    notsel = jnp.pad(ns.transpose(0, 2, 1), ((0, 0), (0, 0), (0, LANES - ncol)))
    rows = jnp.arange(LANES)
    far = _bias_of(rel_bias, jnp.full((LANES, t), 2 * MAX_DISTANCE), jnp.ones((LANES, t), bool))
    kpos_last = past - LANES + rows
    d_last = qpos[None, :] - kpos_last[:, None]
    near = _bias_of(rel_bias, d_last, d_last >= 0)
    kpos_new = past + rows
    d_new = qpos[None, :] - kpos_new[:, None]
    newb = _bias_of(rel_bias, d_new, (d_new >= 0) & (rows[:, None] < t))
    bias_tab = jnp.stack([_dec_bias_cols(far), _dec_bias_cols(near), _dec_bias_cols(newb)])
    bias_idx = jnp.concatenate([jnp.zeros((n_pages - 1,), jnp.int32), jnp.array([1, 2], jnp.int32)])
    o_slc = _decode_attention(_feature_major(cache_slc), page_table, jnp.zeros_like(page_table), new_tile(kvs), wq,
                              bias_tab, bias_idx, notsel, _dec_gate_cols(gn, NSA_HEADS, t))
    o_slc = _dec_extract_nsa(o_slc, t)

    wb = cache_win.shape[1]
    n_wpg = wb // LANES
    kpos_w = past - wb + jnp.arange(wb + LANES)
    d_w = qpos[None, :] - kpos_w[:, None]
    valid_w = (d_w >= 0) & (d_w < WINDOW) & (kpos_w[:, None] >= 0) & (jnp.arange(wb + LANES)[:, None] < wb + t)
    bias_w = _dec_bias_cols(_bias_of(rel_bias, d_w, valid_w)).reshape(n_wpg + 1, LANES, LANES)
    zeros_ns = jnp.zeros((b, -(-2 * (n_wpg + 1) // 8) * 8, LANES), F32)
    o_win = _decode_attention(_feature_major(cache_win), own(n_wpg), blocks(n_wpg), new_tile(kvw), wq, bias_w,
                              jnp.arange(n_wpg + 1, dtype=jnp.int32), zeros_ns, _dec_gate_cols(gn, 2 * NSA_HEADS, t))
    o_win = _dec_extract_nsa(o_win, t)

    m = cache_mem.shape[1]
    n_mpg = m // LANES
    qmh = qm.reshape(b, t, MEM_HEADS, HEAD_DIM).astype(F32)
    wqm = jnp.einsum("bthd,hx->bxdht", qmh, jnp.eye(MEM_HEADS, dtype=F32))
    wqm = wqm.reshape(b, MEM_WIDTH, MEM_HEADS * t)
    wqm = jnp.pad(wqm, ((0, 0), (0, MEM_WIDTH), (0, LANES - MEM_HEADS * t))).astype(BF16).transpose(0, 2, 1)
    o_mem = _decode_attention(_feature_major(cache_mem), own(n_mpg), blocks(n_mpg), None, wqm,
                              jnp.zeros((1, LANES, LANES), F32), jnp.zeros((n_mpg,), jnp.int32),
                              jnp.zeros((b, 8, LANES), F32), jnp.ones((b, 1, LANES), F32))
    om = o_mem[:, MEM_WIDTH:, :MEM_HEADS * t].reshape(b, MEM_HEADS, HEAD_DIM, MEM_HEADS, t)
    o_mem = jnp.einsum("bxdht,hx->bthd", om, jnp.eye(MEM_HEADS, dtype=F32)).reshape(b, t, MEM_WIDTH)

    f2 = lambda a: a.reshape(n, a.shape[-1])
    x2, h2, eidx, gates, cnt = _layer_tail(x2d, f2(o_pool), f2(o_cmp), f2(o_slc), f2(o_win), f2(o_mem), cnt0, w,
                                           n if n <= 512 else LANES)
    kvshape = (b, t, 2, NSA_KV_HEADS, HEAD_DIM)
    new_win = jnp.concatenate([cache_win.reshape(b, wb, KV_WIDTH), kvw], axis=1)[:, t:]
    new_pool = jnp.concatenate([pool_buf, u], axis=1)[:, t:]
    states = (kvc.reshape(kvshape), kvs.reshape(kvshape), new_win, new_pool)
    return (x2, h2, eidx, gates), states, cnt


def kernel(x_prompt, x_sample, cache_cmp_kv, cache_slc_kv, cache_win_kv, cache_mem_kv, state_pool, page_table,
           mem_prompt, rel_bias, norm1_g, w_in, nsa_qk_norm, mem_qk_norm, cmp_w, cmp_pe, pool_w, pool_scale,
           mem_norm_g, w_mem_kv, w_up_pool, w_up_nsa, w_up_mem, w_out, norm2_g, router_w, router_b, w_gu, b_gu,
           w_down, b_down):
    depth = w_in.shape[0]
    yp, ys = x_prompt, x_sample
    bp, sp, _ = x_prompt.shape
    bs, ts, _ = x_sample.shape
    outs_p = [[] for _ in range(5)]
    outs_s = [[] for _ in range(4)]
    for l in range(depth):
        w = _prep_layer(l, rel_bias, norm1_g, w_in, nsa_qk_norm, mem_qk_norm, cmp_w, cmp_pe, pool_w, pool_scale,
                        mem_norm_g, w_mem_kv, w_up_pool, w_up_nsa, w_up_mem, w_out, norm2_g, router_w, router_b,
                        w_gu, b_gu, w_down, b_down)
        pre_p, st_p, cnt = _prompt_pre(yp, mem_prompt, w, rel_bias, jnp.zeros((1, LANES), F32))
        pre_s, st_s, cnt = _sample_pre(ys, cache_cmp_kv[l], cache_slc_kv[l], cache_win_kv[l], cache_mem_kv[l],
                                       state_pool[l], page_table, w, rel_bias, cnt)
        yp, ys = _moe([pre_p, pre_s], cnt, w)
        yp = yp.reshape(bp, sp, D_MODEL)
        ys = ys.reshape(bs, ts, D_MODEL)
        for lst, a in zip(outs_p, st_p):
            lst.append(a)
        for lst, a in zip(outs_s, st_s):
            lst.append(a)
    new_cmp_p, new_slc_p, new_win_p, new_mem_p, new_pool_p = [jnp.stack(a) for a in outs_p]
    new_cmp_s, new_slc_s, new_win_s, new_pool_s = [jnp.stack(a) for a in outs_s]
    new_win_s = new_win_s.reshape(new_win_s.shape[:3] + (2, NSA_KV_HEADS, HEAD_DIM))
    return (yp, ys, new_cmp_p, new_slc_p, new_win_p, new_mem_p, new_pool_p,
            new_cmp_s, new_slc_s, new_win_s, new_pool_s)
```

```python
import functools
import math

import jax
import jax.numpy as jnp
from jax import lax
from jax.experimental import pallas as pl
from jax.experimental.pallas import tpu as pltpu

F32 = jnp.float32
BF16 = jnp.bfloat16

D_MODEL = 1024
HEAD_DIM = 64
POOL_WINDOWS = (2, 4, 8, 16)
POOL_GROUP = 64
POOL_WIDTH = 256
POOL_BUF = 15
NSA_HEADS = 8
NSA_KV_HEADS = 2
NSA_HPG = 4
NSA_WIDTH = 512
KV_WIDTH = 256
CMP_BLOCK = 32
CMP_STRIDE = 16
SLC_BLOCK = 64
SLC_TOPK = 16
WINDOW = 512
MEM_HEADS = 4
MEM_WIDTH = 256
NUM_BUCKETS = 32
MAX_DISTANCE = 128
N_EXPERTS = 32
TOP_K = 4
D_FF = 1024
SWIGLU_ALPHA = 1.702
SWIGLU_LIMIT = 7.0
EPS = 1e-6
SCALE = HEAD_DIM ** -0.5

LANES = 128
MXU_DIM = 256
NEG = -1e30
ATT_TILE = 256
DEC_PAGES_PER_STEP = 16
MOE_TILE = 512
MOE_DMA_TOKENS = 512
VMEM_LIMIT = 48 * 1024 * 1024
FFN_VMEM_LIMIT = 56 * 1024 * 1024


def _cparams(*sem):
    return pltpu.CompilerParams(dimension_semantics=sem, vmem_limit_bytes=VMEM_LIMIT)


def _dot(a, b):
    return jnp.dot(a, b, preferred_element_type=F32)


def _dot_nt(a, b):
    return lax.dot_general(a, b, (((1,), (1,)), ((), ())), preferred_element_type=F32)


def _split_dot(a, b):
    hi = a.astype(BF16)
    lo = (a - hi.astype(F32)).astype(BF16)
    return _dot(hi, b) + _dot(lo, b)


def _rms(x, g):
    r = lax.rsqrt(jnp.mean(x * x, axis=-1, keepdims=True) + EPS)
    return (x * r) * g


def _sigmoid(x):
    return 1.0 / (1.0 + jnp.exp(-x))


def _lane_iota(shape):
    return lax.broadcasted_iota(jnp.int32, shape, len(shape) - 1)


def _row_iota(shape):
    return lax.broadcasted_iota(jnp.int32, shape, len(shape) - 2)


def _proj_kernel(x_ref, g_ref, w_ref, gain_ref, nmask_ref, seg_ref, *out_refs, segs, n_norm):
    h = _rms(x_ref[...], g_ref[...]).astype(BF16)
    seg = seg_ref[...]
    outs = iter(out_refs)
    for (start, width, kind) in segs:
        o_ref = next(outs)
        z = _dot(h, w_ref[:, start:start + width])
        if start < n_norm:
            pieces = []
            for c in range(0, width, MXU_DIM):
                zc = z[:, c:c + MXU_DIM]
                ms = _split_dot(zc * zc, seg)
                zn = (zc * lax.rsqrt(ms + EPS)) * gain_ref[:, start + c:start + c + MXU_DIM]
                pieces.append(jnp.where(nmask_ref[:, start + c:start + c + MXU_DIM] > 0, zn, zc))
            z = pieces[0] if len(pieces) == 1 else jnp.concatenate(pieces, axis=1)
        if kind == "sigmoid":
            z = _sigmoid(z)
        elif kind == "qscale":
            z = z * SCALE
        o_ref[...] = z.astype(o_ref.dtype)
        if kind == "kv":
            next(outs)[...] = z[:, LANES:2 * LANES].T


def _project(x, g, w, gain, nmask, segs, out_dtypes, n_norm, tm):
    n = x.shape[0]
    ncol = w.shape[1]
    seg = _seg_matrix(MXU_DIM)
    full = lambda i: (0, 0)
    out_specs, out_shape = [], []
    for (_, wd, kind), dt in zip(segs, out_dtypes):
        out_specs.append(pl.BlockSpec((tm, wd), lambda i: (i, 0)))
        out_shape.append(jax.ShapeDtypeStruct((n, wd), dt))
        if kind == "kv":
            out_specs.append(pl.BlockSpec((LANES, tm), lambda i: (0, i)))
            out_shape.append(jax.ShapeDtypeStruct((LANES, n), F32))
    return pl.pallas_call(
        functools.partial(_proj_kernel, segs=segs, n_norm=n_norm),
        grid=(n // tm,),
        in_specs=[
            pl.BlockSpec((tm, D_MODEL), lambda i: (i, 0)),
            pl.BlockSpec((1, D_MODEL), full),
            pl.BlockSpec((D_MODEL, ncol), full),
            pl.BlockSpec((1, gain.shape[1]), full),
            pl.BlockSpec((1, nmask.shape[1]), full),
            pl.BlockSpec((MXU_DIM, MXU_DIM), full),
        ],
        out_specs=out_specs,
        out_shape=out_shape,
        compiler_params=_cparams("parallel"),
        name="proj",
    )(x, g, w, gain, nmask, seg)


def _seg_matrix(n):
    i = jnp.arange(n) // HEAD_DIM
    return jnp.where(i[:, None] == i[None, :], 1.0 / HEAD_DIM, 0.0).astype(BF16)


def _pool_kernel(u_ref, buf_ref, w_ref, scale_ref, o_ref, zs_ref, *, t, pos0):
    zs_ref[0:16, :] = buf_ref[...]
    zs_ref[16:16 + t, :] = u_ref[...]
    u = u_ref[...]
    lane = _lane_iota((1, POOL_WIDTH))
    pos = (pos0 + _row_iota((t, 1))).astype(F32)
    acc = u
    mean = None
    for i in range(1, max(POOL_WINDOWS)):
        acc = acc + zs_ref[16 - i:16 - i + t, :]
        if i + 1 in POOL_WINDOWS:
            gi = POOL_WINDOWS.index(i + 1)
            m = acc / jnp.minimum(pos + 1.0, float(i + 1))
            mean = m if mean is None else jnp.where(lane >= gi * POOL_GROUP, m, mean)
    d = (mean - u).astype(BF16)
    o_ref[...] = _dot(d, w_ref[...]) * scale_ref[...]


def _pool_mix(u, buf16, w_bd, scale, pos0):
    b, t, _ = u.shape
    return pl.pallas_call(
        functools.partial(_pool_kernel, t=t, pos0=pos0),
        grid=(b,),
        in_specs=[
            pl.BlockSpec((None, t, POOL_WIDTH), lambda i: (i, 0, 0)),
            pl.BlockSpec((None, 16, POOL_WIDTH), lambda i: (i, 0, 0)),
            pl.BlockSpec((POOL_WIDTH, POOL_WIDTH), lambda i: (0, 0)),
            pl.BlockSpec((1, POOL_WIDTH), lambda i: (0, 0)),
        ],
        out_specs=pl.BlockSpec((None, t, POOL_WIDTH), lambda i: (i, 0, 0)),
        out_shape=jax.ShapeDtypeStruct((b, t, POOL_WIDTH), F32),
        scratch_shapes=[pltpu.VMEM((t + 16, POOL_WIDTH), F32)],
        compiler_params=_cparams("parallel"),
        name="pool",
    )(u, buf16, w_bd, scale)


def _cpart_compute(x_refs, pe_ref, w_ref, o_ref, m):
    for c in range(2):
        acc = jnp.zeros((m + 8, KV_WIDTH), F32)
        for j in range(CMP_STRIDE):
            lhs = jnp.concatenate([x_refs[c][pl.ds(j, m, stride=CMP_STRIDE), :], pe_ref[c, j]], axis=0)
            acc = acc + _dot(lhs.astype(BF16), w_ref[c, j])
        lane = _lane_iota((1, KV_WIDTH))
        pe_term = jnp.where(lane < LANES, acc[m:m + 1], acc[m + 1:m + 2])
        o_ref[:, c * KV_WIDTH:(c + 1) * KV_WIDTH] = acc[0:m] + pe_term


def _cpart_kernel(xk_ref, xv_ref, pe_ref, w_ref, o_ref, *, rows):
    _cpart_compute((xk_ref, xv_ref), pe_ref, w_ref, o_ref, rows // CMP_STRIDE)


def _cpart_paged_kernel(*refs, nop, page):
    page_refs = refs[1:1 + nop]
    pe_ref, w_ref, o_ref, xk_ref, xv_ref = refs[1 + nop:]
    for k, r in enumerate(page_refs):
        xk_ref[k * page:(k + 1) * page, :] = r[0:LANES, :].T
        xv_ref[k * page:(k + 1) * page, :] = r[LANES:2 * LANES, :].T
    _cpart_compute((xk_ref, xv_ref), pe_ref, w_ref, o_ref, nop * page // CMP_STRIDE)


def _cmp_partials_dense(kv, pe, w_c):
    b, t, _ = kv.shape
    rows = (t // CMP_STRIDE) * CMP_STRIDE
    n = rows // CMP_STRIDE
    return pl.pallas_call(
        functools.partial(_cpart_kernel, rows=rows),
        grid=(b,),
        in_specs=[
            pl.BlockSpec((None, rows, LANES), lambda i: (i, 0, 0)),
            pl.BlockSpec((None, rows, LANES), lambda i: (i, 0, 1)),
            pl.BlockSpec(pe.shape, lambda i: (0, 0, 0, 0)),
            pl.BlockSpec(w_c.shape, lambda i: (0, 0, 0, 0)),
        ],
        out_specs=pl.BlockSpec((None, n, 2 * KV_WIDTH), lambda i: (i, 0, 0)),
        out_shape=jax.ShapeDtypeStruct((b, n, 2 * KV_WIDTH), F32),
        compiler_params=_cparams("parallel"),
        name="cmp_partials",
    )(kv, kv, pe, w_c)


def _cmp_partials_paged(pool_t, page_table, pe, w_c, pages_per_step):
    b, n_pages = page_table.shape
    page = pool_t.shape[2]
    nop = pages_per_step
    n = nop * page // CMP_STRIDE

    def page_spec(k):
        return pl.BlockSpec((None, KV_WIDTH, page), lambda i, c, pt: (pt[i, c * nop + k], 0, 0))

    grid_spec = pltpu.PrefetchScalarGridSpec(
        num_scalar_prefetch=1,
        grid=(b, n_pages // nop),
        in_specs=[page_spec(k) for k in range(nop)] + [
            pl.BlockSpec(pe.shape, lambda i, c, pt: (0, 0, 0, 0)),
            pl.BlockSpec(w_c.shape, lambda i, c, pt: (0, 0, 0, 0)),
        ],
        out_specs=pl.BlockSpec((None, n, 2 * KV_WIDTH), lambda i, c, pt: (i, c, 0)),
        scratch_shapes=[pltpu.VMEM((nop * page, LANES), F32), pltpu.VMEM((nop * page, LANES), F32)],
    )
    return pl.pallas_call(
        functools.partial(_cpart_paged_kernel, nop=nop, page=page),
        grid_spec=grid_spec,
        out_shape=jax.ShapeDtypeStruct((b, n_pages * page // CMP_STRIDE, 2 * KV_WIDTH), F32),
        compiler_params=_cparams("parallel", "arbitrary"),
        name="cmp_partials_paged",
    )(page_table, *([pool_t] * nop), pe, w_c)


def _group_query_columns(q_ref, g, t):
    zeros64 = jnp.zeros((HEAD_DIM, t), BF16)
    cols = []
    for pr in range(2):
        qt = q_ref[:, (2 * g + pr) * LANES:(2 * g + pr + 1) * LANES].astype(F32).T.astype(BF16)
        for half in range(2):
            qh = qt[half * HEAD_DIM:(half + 1) * HEAD_DIM]
            cols.append(jnp.concatenate([qh, zeros64] if g == 0 else [zeros64, qh], axis=0))
    return jnp.concatenate(cols, axis=1)


def _store_group_output(o_ref, out_t, g, t):
    for pr in range(2):
        pair = jnp.concatenate([out_t[:, (2 * pr) * t:(2 * pr + 1) * t], out_t[:, (2 * pr + 1) * t:(2 * pr + 2) * t]],
                               axis=0)
        o_ref[:, (2 * g + pr) * LANES:(2 * g + pr + 1) * LANES] = pair.T


def _cattn_kernel(q_ref, p_ref, bias_ref, gn_ref, gk_ref, seg_ref, o_ref, ns0_ref, ns1_ref, *,
                  tq, n_sub, n_cmp, n_slc, nslp, pos0):
    qi = pl.program_id(0)
    pall = p_ref[...]
    kraw = pall[:, 0:LANES] + pltpu.roll(pall[:, LANES:2 * LANES], n_sub - 1, 0)
    vc = pall[:, 2 * LANES:3 * LANES] + pltpu.roll(pall[:, 3 * LANES:4 * LANES], n_sub - 1, 0)
    ms = _split_dot(kraw * kraw, seg_ref[...])
    kc = ((kraw * lax.rsqrt(ms + EPS)) * gk_ref[...]).astype(BF16)
    vct = vc.T.astype(BF16)

    nsel = -(-n_slc // 8) * 8
    jj = _row_iota((nsel, n_sub))
    nn = _lane_iota((nsel, n_sub))
    covers_t = ((nn * CMP_STRIDE < (jj + 1) * SLC_BLOCK) & (nn * CMP_STRIDE + CMP_BLOCK - 1 >= jj * SLC_BLOCK)
                & (nn < n_cmp) & (jj < n_slc))
    covers_t = jnp.where(covers_t, 1.0, 0.0).astype(BF16)
    qpos = pos0 + qi * tq + _lane_iota((1, tq))
    qblk = jnp.right_shift(qpos, SLC_BLOCK.bit_length() - 1)
    jr = _row_iota((nsel, tq))
    jrf = jr.astype(F32)
    forced = (jr == 0) | (jr == qblk) | (jr == qblk - 1)
    causal = jr <= qblk
    gnt = gn_ref[...].T

    for g in range(NSA_KV_HEADS):
        bias = bias_ref[g]
        s = _dot(kc, _group_query_columns(q_ref, g, tq)) + bias
        m = jnp.max(s, axis=0, keepdims=True)
        m = jnp.where(m > 0.5 * NEG, m, 0.0)
        e = jnp.where(bias > 0.5 * NEG, jnp.exp(s - m), 0.0)
        p = e / jnp.maximum(jnp.sum(e, axis=0, keepdims=True), 1e-30)
        h0 = NSA_HPG * g
        gate = jnp.concatenate([gnt[h0 + c:h0 + c + 1] for c in range(NSA_HPG)], axis=1)
        out_t = _dot(vct[g * HEAD_DIM:(g + 1) * HEAD_DIM], p.astype(BF16)) * gate
        _store_group_output(o_ref, out_t, g, tq)

        prsum = p[:, 0:tq] + p[:, tq:2 * tq] + p[:, 2 * tq:3 * tq] + p[:, 3 * tq:4 * tq]
        hi = prsum.astype(BF16)
        lo = (prsum - hi.astype(F32)).astype(BF16)
        imp = _dot(covers_t, hi) + _dot(covers_t, lo)
        score = jnp.where(forced, jnp.inf, imp)
        score = jnp.where(causal, score, -jnp.inf)
        sel = jnp.zeros((nsel, tq), F32)
        for _ in range(min(SLC_TOPK, n_slc)):
            m = jnp.max(score, axis=0, keepdims=True)
            idx = jnp.min(jnp.where(score == m, jrf, 1e9), axis=0, keepdims=True)
            pick = jrf == idx
            sel = jnp.where(pick & (m > -jnp.inf), 1.0, sel)
            score = jnp.where(pick, -jnp.inf, score)
        ns = 1.0 - sel
        if nslp > nsel:
            ns = jnp.concatenate([ns, jnp.ones((nslp - nsel, tq), F32)], axis=0)
        (ns0_ref if g == 0 else ns1_ref)[...] = ns


def _cmp_attention(q, parts, bias, gn, gk, *, tq, n_cmp, n_slc, pos0):
    b, s, _ = q.shape
    n_sub = parts.shape[1]
    nslp = -(-n_slc // LANES) * LANES
    bias_t = bias.reshape(NSA_KV_HEADS, NSA_HPG, s // tq, tq, n_sub).transpose(0, 4, 2, 1, 3)
    bias_t = bias_t.reshape(NSA_KV_HEADS, n_sub, NSA_HPG * s)
    kern = functools.partial(_cattn_kernel, tq=tq, n_sub=n_sub, n_cmp=n_cmp, n_slc=n_slc, nslp=nslp, pos0=pos0)
    return pl.pallas_call(
        kern,
        grid=(s // tq, b),
        in_specs=[
            pl.BlockSpec((None, tq, NSA_WIDTH), lambda i, j: (j, i, 0)),
            pl.BlockSpec((None, n_sub, 2 * KV_WIDTH), lambda i, j: (j, 0, 0)),
            pl.BlockSpec((NSA_KV_HEADS, n_sub, NSA_HPG * tq), lambda i, j: (0, 0, i)),
            pl.BlockSpec((None, tq, LANES), lambda i, j: (j, i, 0)),
            pl.BlockSpec((1, LANES), lambda i, j: (0, 0)),
            pl.BlockSpec((LANES, LANES), lambda i, j: (0, 0)),
        ],
        out_specs=[
            pl.BlockSpec((None, tq, NSA_WIDTH), lambda i, j: (j, i, 0)),
            pl.BlockSpec((None, nslp, tq), lambda i, j: (j, 0, i)),
            pl.BlockSpec((None, nslp, tq), lambda i, j: (j, 0, i)),
        ],
        out_shape=[
            jax.ShapeDtypeStruct((b, s, NSA_WIDTH), F32),
            jax.ShapeDtypeStruct((b, nslp, s), F32),
            jax.ShapeDtypeStruct((b, nslp, s), F32),
        ],
        compiler_params=_cparams("parallel", "parallel"),
        name="cmp_attention",
    )(q, parts, bias_t, gn, gk, _seg_matrix(LANES))


def _flash_kernel(q_ref, ns0_ref, ns1_ref, kv_ref, vt_ref, tab_ref, gn_ref, o_ref, *, t, use_sel, band, gate_base):
    qi = pl.program_id(1)
    row_k = _row_iota((t, LANES))
    lane_k = _lane_iota((t, LANES))
    lo_tile = jnp.maximum(qi - band, 0) if band is not None else 0
    gnt = gn_ref[...].T

    qts = []
    for g in range(NSA_KV_HEADS):
        qt_g = _group_query_columns(q_ref, g, t)
        if use_sel:
            nst = (ns0_ref if g == 0 else ns1_ref)[...].astype(BF16)
            qt_g = jnp.concatenate([qt_g, jnp.concatenate([nst] * NSA_HPG, axis=1)], axis=0)
        qts.append(qt_g)

    def body(kj, carry):
        k0 = pl.multiple_of(kj * t, t)
        kk = kv_ref[pl.ds(k0, t), 0:LANES].astype(BF16)
        if use_sel:
            blk = kj * (t // SLC_BLOCK) + jnp.right_shift(row_k, SLC_BLOCK.bit_length() - 1)
            onehot = jnp.where(lane_k == blk, -(2.0 ** 30), 0.0).astype(BF16)
            kk = jnp.concatenate([kk, onehot], axis=1)
        delta = qi - kj
        if band is None:
            kind = jnp.minimum(delta, 2)
        else:
            kind = jnp.where(delta < 2, delta, jnp.where(delta < band, 2, 3))
        m_old, l_old, acc = carry
        s = _dot(kk, qt_all) + tab_ref[kind]
        m_new = jnp.maximum(m_old, jnp.max(s, axis=0, keepdims=True))
        alpha = jnp.exp(m_old - m_new)
        p = jnp.exp(s - m_new)
        l_new = alpha * l_old + jnp.sum(p, axis=0, keepdims=True)
        pb = p.astype(BF16)
        pv = jnp.concatenate(
            [_dot(vt_ref[g * HEAD_DIM:(g + 1) * HEAD_DIM, pl.ds(k0, t)].astype(BF16), pb[:, g * gw:(g + 1) * gw])
             for g in range(NSA_KV_HEADS)], axis=1)
        return m_new, l_new, acc * alpha + pv

    gw = NSA_HPG * t
    qt_all = jnp.concatenate(qts, axis=1)
    init = (jnp.full((1, NSA_HEADS * t), NEG, F32), jnp.zeros((1, NSA_HEADS * t), F32),
            jnp.zeros((HEAD_DIM, NSA_HEADS * t), F32))
    _, l_fin, acc = lax.fori_loop(lo_tile, qi + 1, body, init)
    gate = jnp.concatenate([gnt[gate_base + h:gate_base + h + 1] for h in range(NSA_HEADS)], axis=1)
    out = acc * (gate / l_fin)
    for g in range(NSA_KV_HEADS):
        _store_group_output(o_ref, out[:, g * gw:(g + 1) * gw], g, t)


def _flash_attention(q, ns0, ns1, kv, vt, tab, gn, *, use_sel, band, gate_base):
    b, s, _ = q.shape
    t = ATT_TILE
    assert ns0.shape[1] == LANES
    kern = functools.partial(_flash_kernel, t=t, use_sel=use_sel, band=band, gate_base=gate_base)
    tile = lambda w: pl.BlockSpec((None, t, w), lambda i, j: (i, j, 0))
    ns_tile = pl.BlockSpec((None, LANES, t), lambda i, j: (i, 0, j))
    return pl.pallas_call(
        kern,
        grid=(b, s // t),
        in_specs=[
            tile(NSA_WIDTH), ns_tile, ns_tile,
            pl.BlockSpec((None, s, KV_WIDTH), lambda i, j: (i, 0, 0)),
            pl.BlockSpec((LANES, s), lambda i, j: (0, i)),
            pl.BlockSpec(tab.shape, lambda i, j: (0, 0, 0)),
            tile(LANES),
        ],
        out_specs=tile(NSA_WIDTH),
        out_shape=jax.ShapeDtypeStruct((b, s, NSA_WIDTH), F32),
        compiler_params=_cparams("parallel", "parallel"),
        name="flash_sel" if use_sel else "flash_win",
    )(q, ns0, ns1, kv, vt, tab, gn)


def _memattn_kernel(q_ref, kv_ref, o_ref):
    lane = _lane_iota((kv_ref.shape[0], LANES))
    for pr in range(MEM_HEADS // 2):
        qpair = q_ref[:, pr * LANES:(pr + 1) * LANES]
        kblk = kv_ref[:, pr * LANES:(pr + 1) * LANES]
        vblk = kv_ref[:, MEM_WIDTH + pr * LANES:MEM_WIDTH + (pr + 1) * LANES]
        out = None
        for half in range(2):
            keep = (lane < HEAD_DIM) if half == 0 else (lane >= HEAD_DIM)
            kk = jnp.where(keep, kblk, 0.0).astype(BF16)
            vv = jnp.where(keep, vblk, 0.0).astype(BF16)
            s = _dot_nt(qpair, kk)
            m = jnp.max(s, axis=1, keepdims=True)
            e = jnp.exp(s - m)
            p = e / jnp.sum(e, axis=1, keepdims=True)
            o = _dot(p.astype(BF16), vv)
            out = o if out is None else out + o
        o_ref[:, pr * LANES:(pr + 1) * LANES] = out


def _mem_attention(qm, mem_kv, tq):
    b, s, _ = qm.shape
    m = mem_kv.shape[1]
    return pl.pallas_call(
        _memattn_kernel,
        grid=(b, s // tq),
        in_specs=[
            pl.BlockSpec((None, tq, MEM_WIDTH), lambda i, j: (i, j, 0)),
            pl.BlockSpec((None, m, 2 * MEM_WIDTH), lambda i, j: (i, 0, 0)),
        ],
        out_specs=pl.BlockSpec((None, tq, MEM_WIDTH), lambda i, j: (i, j, 0)),
        out_shape=jax.ShapeDtypeStruct((b, s, MEM_WIDTH), F32),
        compiler_params=_cparams("parallel", "parallel"),
        name="mem_attention",
    )(qm, mem_kv)


def _dec_kernel(*refs, n_pg, pps, has_new):
    bidx_ref = refs[2]
    page_refs = refs[3:3 + pps]
    new_ref, wq_ref, bias_ref, ns_ref, gate_ref, o_ref, acc_ref, m_ref, l_ref = refs[3 + pps:]
    c = pl.program_id(1)
    n_chunks = pl.num_programs(1)

    @pl.when(c == 0)
    def _():
        acc_ref[...] = jnp.zeros_like(acc_ref)
        m_ref[...] = jnp.full_like(m_ref, NEG)
        l_ref[...] = jnp.zeros_like(l_ref)

    rk = _row_iota((LANES, LANES))

    def step(tiles, first_page):
        feats = (tiles[0] if len(tiles) == 1 else jnp.concatenate(tiles, axis=1)).astype(BF16)
        s = lax.dot_general(feats, wq_ref[...], (((0,), (1,)), ((), ())),
                            preferred_element_type=F32)
        extra = []
        for k in range(len(tiles)):
            pg = first_page + k
            ns = jnp.where(rk < SLC_BLOCK, ns_ref[pl.ds(2 * pg, 1), :], ns_ref[pl.ds(2 * pg + 1, 1), :])
            extra.append(jnp.where(ns > 0.5, NEG, bias_ref[bidx_ref[pg]]))
        s = s + (extra[0] if len(extra) == 1 else jnp.concatenate(extra, axis=0))
        m_old = m_ref[...]
        m_new = jnp.maximum(m_old, jnp.max(s, axis=0, keepdims=True))
        alpha = jnp.exp(m_old - m_new)
        p = jnp.exp(s - m_new)
        l_ref[...] = alpha * l_ref[...] + jnp.sum(p, axis=0, keepdims=True)
        m_ref[...] = m_new
        acc_ref[...] = acc_ref[...] * alpha + _dot(feats, p.astype(BF16))

    if has_new:
        @pl.when(c < n_chunks - 1)
        def _():
            step([r[...] for r in page_refs], c * pps)

        @pl.when(c == n_chunks - 1)
        def _():
            step([r[...] for r in page_refs] + [new_ref[...]], n_pg - pps)
    else:
        step([r[...] for r in page_refs], c * pps)

    @pl.when(c == n_chunks - 1)
    def _():
        o_ref[...] = acc_ref[...] / l_ref[...] * gate_ref[...]


def _decode_attention(pages, phys, lblk, new_rows, wq, bias_tab, bias_idx, notsel, gate):
    bsz, n_pg = phys.shape
    w = pages.shape[1]
    has_new = new_rows is not None
    pps = math.gcd(n_pg, DEC_PAGES_PER_STEP)
    n_steps = n_pg // pps
    n_chunks = n_steps
    if not has_new:
        new_rows = jnp.zeros((1, w, LANES), F32)
    new_map = (lambda i, c, ph, lb, bi: (i, 0, 0)) if has_new else (lambda i, c, ph, lb, bi: (0, 0, 0))

    def page_spec(k):
        def index(i, c, ph, lb, bi):
            return (ph[i, c * pps + k], 0, lb[i, c * pps + k])
        return pl.BlockSpec((None, w, LANES), index)

    per_b = lambda i, c, ph, lb, bi: (i, 0, 0)
    grid_spec = pltpu.PrefetchScalarGridSpec(
        num_scalar_prefetch=3,
        grid=(bsz, n_chunks),
        in_specs=[page_spec(k) for k in range(pps)] + [
            pl.BlockSpec((None, w, LANES), new_map),
            pl.BlockSpec((None, LANES, w), per_b),
            pl.BlockSpec(bias_tab.shape, lambda i, c, ph, lb, bi: (0, 0, 0)),
            pl.BlockSpec((None, notsel.shape[1], LANES), per_b),
            pl.BlockSpec((None, 1, LANES), per_b),
        ],
        out_specs=pl.BlockSpec((None, w, LANES), per_b),
        scratch_shapes=[pltpu.VMEM((w, LANES), F32), pltpu.VMEM((1, LANES), F32), pltpu.VMEM((1, LANES), F32)],
    )
    return pl.pallas_call(
        functools.partial(_dec_kernel, n_pg=n_pg, pps=pps, has_new=has_new),
        grid_spec=grid_spec,
        out_shape=jax.ShapeDtypeStruct((bsz, w, LANES), F32),
        compiler_params=_cparams("parallel", "arbitrary"),
        name="decode_attention",
    )(phys, lblk, bias_idx, *([pages] * pps), new_rows, wq, bias_tab, notsel, gate)


def _feature_major(cache):
    n, rows = cache.shape[:2]
    return cache.transpose(0, 2, 3, 4, 1).reshape(n, -1, rows)


def _tail_kernel(x_ref, g1_ref, op_ref, oc_ref, os_ref, ow_ref, om_ref, cnt0_ref, wgb_ref, wup_p_ref, wup_n_ref,
                 wup_m_ref, wout_ref, g2_ref, rwh_ref, rwl_ref, rb_ref, x2_ref, h2_ref, ei_ref, gt_ref, cnt_ref):
    @pl.when(pl.program_id(0) == 0)
    def _():
        cnt_ref[...] = cnt0_ref[...]

    x = x_ref[...]
    h = _rms(x, g1_ref[...]).astype(BF16)
    onsa = (oc_ref[...] + os_ref[...] + ow_ref[...]).astype(BF16)
    ups = (_dot(op_ref[...].astype(BF16), wup_p_ref[...]), _dot(onsa, wup_n_ref[...]),
           _dot(om_ref[...].astype(BF16), wup_m_ref[...]))
    mixed = None
    for br in range(3):
        gb = _sigmoid(_dot(h, wgb_ref[:, br * D_MODEL:(br + 1) * D_MODEL]))
        mixed = gb * ups[br] if mixed is None else mixed + gb * ups[br]
    x2 = x + _dot(mixed.astype(BF16), wout_ref[...])
    x2_ref[...] = x2
    h2 = _rms(x2, g2_ref[...])
    h2_ref[...] = h2
    hi = h2.astype(BF16)
    lo = (h2 - hi.astype(F32)).astype(BF16)
    logits = _dot(hi, rwh_ref[...]) + _dot(lo, rwh_ref[...]) + _dot(hi, rwl_ref[...]) + rb_ref[...]
    lane = _lane_iota(logits.shape)
    lanef = lane.astype(F32)
    tops, idxs = [], []
    for _ in range(TOP_K):
        m = jnp.max(logits, axis=1, keepdims=True)
        idx = jnp.min(jnp.where(logits == m, lanef, 1e9), axis=1, keepdims=True)
        logits = jnp.where(lanef == idx, -jnp.inf, logits)
        tops.append(m)
        idxs.append(idx)
    es = [jnp.exp(tk - tops[0]) for tk in tops]
    den = es[0] + es[1] + es[2] + es[3]
    tm = logits.shape[0]
    onehot = jnp.zeros(logits.shape, F32)
    for k in range(TOP_K):
        onehot = jnp.where(lanef == idxs[k], 1.0, onehot)
    tri = jnp.where(_row_iota((tm, tm)) > _lane_iota((tm, tm)), 1.0, 0.0).astype(BF16)
    before = _dot(tri, onehot.astype(BF16)) + cnt_ref[...]
    cnt_ref[...] = cnt_ref[...] + jnp.sum(onehot, axis=0, keepdims=True)
    ei = jnp.zeros(logits.shape, F32)
    gt = jnp.zeros(logits.shape, F32)
    for k in range(TOP_K):
        rank = jnp.sum(jnp.where(lanef == idxs[k], before, 0.0), axis=1, keepdims=True)
        ei = jnp.where(lane == k, idxs[k], ei)
        ei = jnp.where(lane == TOP_K + k, rank, ei)
        gt = jnp.where(lane == k, es[k] / den, gt)
    ei_ref[...] = ei.astype(jnp.int32)
    gt_ref[...] = gt


def _layer_tail(x, o_pool, o_cmp, o_slc, o_win, o_mem, cnt0, w, tm):
    n = x.shape[0]
    row = lambda wd: pl.BlockSpec((tm, wd), lambda i: (i, 0))
    full = lambda a: pl.BlockSpec(a.shape, lambda i: (0,) * a.ndim)
    weights = (w["wgb"], w["wup_pool"], w["wup_nsa"], w["wup_mem"], w["wout"], w["g2"], w["rw_hi"], w["rw_lo"],
               w["rb"])
    return pl.pallas_call(
        _tail_kernel,
        grid=(n // tm,),
        in_specs=[row(D_MODEL), full(w["g1"]), row(POOL_WIDTH), row(NSA_WIDTH), row(NSA_WIDTH), row(NSA_WIDTH),
                  row(MEM_WIDTH), full(cnt0)] + [full(a) for a in weights],
        out_specs=[row(D_MODEL), row(D_MODEL), row(LANES), row(LANES), full(cnt0)],
        out_shape=[jax.ShapeDtypeStruct((n, D_MODEL), F32), jax.ShapeDtypeStruct((n, D_MODEL), F32),
                   jax.ShapeDtypeStruct((n, LANES), jnp.int32), jax.ShapeDtypeStruct((n, LANES), F32),
                   jax.ShapeDtypeStruct((1, LANES), F32)],
        compiler_params=_cparams("arbitrary"),
        name="layer_tail",
    )(x, w["g1"], o_pool, o_cmp, o_slc, o_win, o_mem, cnt0, *weights)


def _ffn_kernel(be_ref, nu_ref, x_ref, wgu_ref, bgu_ref, wd_ref, bd_ref, o_ref, wgu_bf, wd_bf):
    i = pl.program_id(0)

    @pl.when((i == 0) | (be_ref[i] != be_ref[jnp.maximum(i - 1, 0)]))
    def _():
        wgu_bf[...] = wgu_ref[...].astype(BF16)
        wd_bf[...] = wd_ref[...].astype(BF16)

    @pl.when(i < nu_ref[0])
    def _():
        gu = _dot(x_ref[...].astype(BF16), wgu_bf[...]) + bgu_ref[...]
        gate = jnp.minimum(gu[:, :D_FF], SWIGLU_LIMIT)
        up = jnp.clip(gu[:, D_FF:], -SWIGLU_LIMIT, SWIGLU_LIMIT)
        act = gate * _sigmoid(SWIGLU_ALPHA * gate) * (up + 1.0)
        o_ref[...] = _dot(act.astype(BF16), wd_bf[...]) + bd_ref[...]

    @pl.when(i >= nu_ref[0])
    def _():
        o_ref[...] = jnp.zeros_like(o_ref)


def _expert_ffn(rows, blk_e, n_used, wgu, bgu, wd, bd):
    n_rows = rows.shape[0]
    n_blocks = n_rows // MOE_TILE
    blk = lambda i, be, nu: (jnp.minimum(i, nu[0] - 1), 0)
    grid_spec = pltpu.PrefetchScalarGridSpec(
        num_scalar_prefetch=2,
        grid=(n_blocks,),
        in_specs=[
            pl.BlockSpec((MOE_TILE, D_MODEL), blk),
            pl.BlockSpec((None, D_MODEL, 2 * D_FF), lambda i, be, nu: (be[i], 0, 0)),
            pl.BlockSpec((None, 1, 2 * D_FF), lambda i, be, nu: (be[i], 0, 0)),
            pl.BlockSpec((None, D_FF, D_MODEL), lambda i, be, nu: (be[i], 0, 0)),
            pl.BlockSpec((None, 1, D_MODEL), lambda i, be, nu: (be[i], 0, 0)),
        ],
        out_specs=pl.BlockSpec((MOE_TILE, D_MODEL), lambda i, be, nu: (i, 0)),
        scratch_shapes=[pltpu.VMEM((D_MODEL, 2 * D_FF), BF16), pltpu.VMEM((D_FF, D_MODEL), BF16)],
    )
    return pl.pallas_call(
        _ffn_kernel,
        grid_spec=grid_spec,
        out_shape=jax.ShapeDtypeStruct((n_rows, D_MODEL), F32),
        compiler_params=pltpu.CompilerParams(dimension_semantics=("arbitrary",), vmem_limit_bytes=FFN_VMEM_LIMIT),
        name="expert_ffn",
    )(blk_e, n_used, rows, wgu, bgu, wd, bd)


def _token_tile(n):
    return next(t for t in (MOE_DMA_TOKENS, 384, 256, 128, n) if n % t == 0)


def _dispatch_scatter(dest_ref, h_ref, rows_ref, sem, td):
    def issue(i, carry):
        t0 = pl.multiple_of(i * 8, 8)
        for r in range(8):
            for k in range(TOP_K):
                d = dest_ref[i * (8 * TOP_K) + r * TOP_K + k]
                pltpu.make_async_copy(h_ref.at[pl.ds(t0 + r, 1)], rows_ref.at[pl.ds(d, 1)], sem).start()
        return carry

    lax.fori_loop(0, td // 8, issue, 0)
    for k in range(TOP_K):
        pltpu.make_async_copy(h_ref, rows_ref.at[pl.ds(0, td)], sem).wait()


def _dispatch_kernel(*refs, tiles, steps, n_blocks):
    ng = len(tiles)
    ends_ref, padded_ref, nu_ref = refs[:3]
    dest_refs = refs[3:3 + ng]
    h_refs = refs[3 + ng:3 + 2 * ng]
    rows_ref, zero_ref, sem, zsem = refs[3 + 2 * ng:]
    i = pl.program_id(0)

    @pl.when(i == 0)
    def _():
        zero_ref[...] = jnp.zeros_like(zero_ref)

        def tail_copy(e):
            start = pl.multiple_of(ends_ref[e] - MOE_TILE, MOE_TILE)
            return pltpu.make_async_copy(zero_ref, rows_ref.at[pl.ds(start, MOE_TILE)], zsem)

        def block_copy(blk):
            start = pl.multiple_of(blk * MOE_TILE, MOE_TILE)
            return pltpu.make_async_copy(zero_ref, rows_ref.at[pl.ds(start, MOE_TILE)], zsem)

        def each(start):
            def expert(e, carry):
                @pl.when(padded_ref[e] > 0)
                def _():
                    tail_copy(e).start() if start else tail_copy(e).wait()
                return carry

            def block(blk, carry):
                block_copy(blk).start() if start else block_copy(blk).wait()
                return carry

            lax.fori_loop(0, N_EXPERTS, expert, 0)
            lax.fori_loop(nu_ref[0], n_blocks, block, 0)

        each(True)
        each(False)

    first = 0
    for g in range(ng):
        @pl.when((i >= first) & (i < first + steps[g]))
        def _(g=g):
            _dispatch_scatter(dest_refs[g], h_refs[g], rows_ref, sem, tiles[g])
        first += steps[g]


def _moe_dispatch(h2s, dests, n_rows, pad_ends, padded, n_used):
    tiles = [_token_tile(h.shape[0]) for h in h2s]
    steps = [h.shape[0] // t for h, t in zip(h2s, tiles)]
    firsts = [sum(steps[:g]) for g in range(len(h2s))]

    def local(g):
        return lambda i, *_: jnp.clip(i - firsts[g], 0, steps[g] - 1)

    dest_specs = [pl.BlockSpec((tiles[g] * TOP_K,), lambda i, *_, f=local(g): (f(i),), memory_space=pltpu.SMEM)
                  for g in range(len(h2s))]
    tok_specs = [pl.BlockSpec((tiles[g], D_MODEL), lambda i, *_, f=local(g): (f(i), 0)) for g in range(len(h2s))]
    grid_spec = pltpu.PrefetchScalarGridSpec(
        num_scalar_prefetch=3,
        grid=(sum(steps),),
        in_specs=dest_specs + tok_specs,
        out_specs=pl.BlockSpec(memory_space=pl.ANY),
        scratch_shapes=[pltpu.VMEM((MOE_TILE, D_MODEL), F32), pltpu.SemaphoreType.DMA(()),
                        pltpu.SemaphoreType.DMA(())],
    )
    return pl.pallas_call(
        functools.partial(_dispatch_kernel, tiles=tiles, steps=steps, n_blocks=n_rows // MOE_TILE),
        grid_spec=grid_spec,
        out_shape=jax.ShapeDtypeStruct((n_rows, D_MODEL), F32),
        compiler_params=_cparams("arbitrary"),
        name="moe_dispatch",
    )(pad_ends, padded, n_used, *dests, *h2s)


def _combine_kernel(dest_ref, x2_ref, g_ref, rows_ref, o_ref, ybuf, sem, *, td):
    def issue(i, carry):
        t0 = pl.multiple_of(i * 8, 8)
        for r in range(8):
            for k in range(TOP_K):
                d = dest_ref[i * (8 * TOP_K) + r * TOP_K + k]
                pltpu.make_async_copy(rows_ref.at[pl.ds(d, 1)], ybuf.at[k, pl.ds(t0 + r, 1)], sem).start()
        return carry

    lax.fori_loop(0, td // 8, issue, 0)
    for k in range(TOP_K):
        pltpu.make_async_copy(rows_ref.at[pl.ds(0, td)], ybuf.at[k], sem).wait()
    out = x2_ref[...]
    for k in range(TOP_K):
        out = out + g_ref[:, k:k + 1] * ybuf[k]
    o_ref[...] = out


def _moe_combine(x2, out_rows, dest, gates):
    n = x2.shape[0]
    td = _token_tile(n)
    return pl.pallas_call(
        functools.partial(_combine_kernel, td=td),
        grid=(n // td,),
        in_specs=[pl.BlockSpec((td * TOP_K,), lambda i: (i,), memory_space=pltpu.SMEM),
                  pl.BlockSpec((td, D_MODEL), lambda i: (i, 0)),
                  pl.BlockSpec((td, LANES), lambda i: (i, 0)),
                  pl.BlockSpec(memory_space=pl.ANY)],
        out_specs=pl.BlockSpec((td, D_MODEL), lambda i: (i, 0)),
        out_shape=jax.ShapeDtypeStruct((n, D_MODEL), F32),
        scratch_shapes=[pltpu.VMEM((TOP_K, td, D_MODEL), F32), pltpu.SemaphoreType.DMA(())],
        compiler_params=_cparams("arbitrary"),
        name="moe_combine",
    )(dest, x2, gates, out_rows)


def _moe(groups, counts, w):
    n_total = sum(g[0].shape[0] for g in groups)
    cnt = counts[0, :N_EXPERTS].astype(jnp.int32)
    padded = (cnt + MOE_TILE - 1) // MOE_TILE * MOE_TILE
    pad_ends = jnp.cumsum(padded)
    pad_starts = pad_ends - padded
    n_blocks = -(-n_total * TOP_K // MOE_TILE) + N_EXPERTS
    blk_start = jnp.arange(n_blocks, dtype=jnp.int32) * MOE_TILE
    blk_e = jnp.minimum(jnp.sum(blk_start[:, None] >= pad_ends[None, :], axis=1), N_EXPERTS - 1).astype(jnp.int32)
    n_used = (pad_ends[-1] // MOE_TILE).astype(jnp.int32).reshape(1)
    dests = [(pad_starts[er[:, :TOP_K]] + er[:, TOP_K:2 * TOP_K]).astype(jnp.int32).reshape(-1)
             for (_, _, er, _) in groups]
    rows = _moe_dispatch([g[1] for g in groups], dests, n_blocks * MOE_TILE, pad_ends.astype(jnp.int32),
                         padded.astype(jnp.int32), n_used)
    out_rows = _expert_ffn(rows, blk_e, n_used, w["wgu"], w["bgu"], w["wd"], w["bd"])
    return [_moe_combine(x2, out_rows, dest, gates) for (x2, _, _, gates), dest in zip(groups, dests)]


def _rel_bucket(dist):
    n = jnp.maximum(dist, 0)
    max_exact = NUM_BUCKETS // 2
    nf = jnp.maximum(n, 1).astype(F32)
    large = max_exact + (jnp.log(nf / max_exact) / math.log(MAX_DISTANCE / max_exact)
                         * (NUM_BUCKETS - max_exact)).astype(jnp.int32)
    large = jnp.minimum(large, NUM_BUCKETS - 1)
    return jnp.where(n < max_exact, n, large)


def _bias_of(rel_bias, dist, valid):
    return jnp.where(valid[..., None], rel_bias[_rel_bucket(dist)], NEG)


_PROJ_SEGS = ((0, 512, "qscale"), (512, 256, "qscale"), (768, 256, "kv"), (1024, 256, "kv"), (1280, 256, "id"),
              (1536, 256, "id"), (1792, 128, "sigmoid"))
_PROJ_DTYPES = (BF16, BF16, F32, F32, F32, F32, F32)
_PROJ_NNORM = 1280


def _prep_layer(l, rel_bias, norm1_g, w_in, nsa_qk_norm, mem_qk_norm, cmp_w, cmp_pe, pool_w, pool_scale,
                mem_norm_g, w_mem_kv, w_up_pool, w_up_nsa, w_up_mem, w_out, norm2_g, router_w, router_b,
                w_gu, b_gu, w_down, b_down):
    wi = w_in[l]
    o_u, o_q, o_qm, o_kvc, o_kvs, o_kvw, o_gn, o_gb = 0, 256, 768, 1024, 1280, 1536, 1792, 1816
    w_proj = jnp.concatenate([
        wi[:, o_q:o_q + 512], wi[:, o_qm:o_qm + 256], wi[:, o_kvs:o_kvs + 256], wi[:, o_kvw:o_kvw + 256],
        wi[:, o_kvc:o_kvc + 256], wi[:, o_u:o_u + 256], wi[:, o_gn:o_gn + 24],
        jnp.zeros((D_MODEL, LANES - 24), F32)], axis=1).astype(BF16)
    nq, mq = nsa_qk_norm[l], mem_qk_norm[l]
    ones = jnp.ones((LANES,), F32)
    gain = jnp.concatenate([jnp.tile(nq[0], 8), jnp.tile(mq[0], 4), jnp.tile(nq[2], 2), ones,
                            jnp.tile(nq[3], 2), ones])[None, :]
    nmask = jnp.concatenate([jnp.ones((768,), F32), ones, 0 * ones, ones, 0 * ones])[None, :]
    eye4 = jnp.eye(4, dtype=F32)
    cw = cmp_w[l].reshape(2, 2, CMP_STRIDE, HEAD_DIM, HEAD_DIM)
    w_c = jnp.einsum("crjde,xy->cjxdrye", cw, jnp.eye(2, dtype=F32))
    w_c = w_c.reshape(2, CMP_STRIDE, LANES, KV_WIDTH).astype(BF16)
    pe = cmp_pe[l].reshape(2, 2, CMP_STRIDE, HEAD_DIM)
    pe_c = jnp.tile(pe.transpose(0, 2, 1, 3), (1, 1, 1, NSA_KV_HEADS))
    pe_c = jnp.pad(pe_c, ((0, 0), (0, 0), (0, 6), (0, 0)))
    w_pool = jnp.einsum("gde,gh->gdhe", pool_w[l], eye4).reshape(POOL_WIDTH, POOL_WIDTH).astype(BF16)
    rw = jnp.pad(router_w[l], ((0, 0), (0, LANES - N_EXPERTS)))
    rw_hi = rw.astype(BF16)
    rw_lo = (rw - rw_hi.astype(F32)).astype(BF16)
    rb = jnp.concatenate([router_b[l], jnp.full((LANES - N_EXPERTS,), NEG, F32)])[None, :]
    return {
        "g1": norm1_g[l][None, :], "w_proj": w_proj, "gain": gain, "nmask": nmask,
        "gk_cmp": jnp.tile(nq[1], 2)[None, :], "w_c": w_c, "pe_c": pe_c,
        "w_pool": w_pool, "pool_scale": pool_scale[l][None, :],
        "mem_g": mem_norm_g[l][None, :], "w_mem": w_mem_kv[l].astype(BF16),
        "mem_gain": jnp.concatenate([jnp.tile(mq[1], 4), jnp.ones((256,), F32)])[None, :],
        "mem_nmask": jnp.concatenate([jnp.ones((256,), F32), jnp.zeros((256,), F32)])[None, :],
        "wgb": wi[:, o_gb:o_gb + 3 * D_MODEL].astype(BF16),
        "wup_pool": w_up_pool[l].astype(BF16), "wup_nsa": w_up_nsa[l].astype(BF16),
        "wup_mem": w_up_mem[l].astype(BF16), "wout": w_out[l].astype(BF16), "g2": norm2_g[l][None, :],
        "rw_hi": rw_hi, "rw_lo": rw_lo, "rb": rb,
        "wgu": w_gu[l], "bgu": b_gu[l][:, None, :], "wd": w_down[l],
        "bd": b_down[l][:, None, :],
    }


def _project_in(x2d, w, tm):
    return _project(x2d, w["g1"], w["w_proj"], w["gain"], w["nmask"], _PROJ_SEGS, _PROJ_DTYPES, _PROJ_NNORM, tm)


def _toeplitz(v, t):
    lead = v.shape[:-1]
    flat = jnp.tile(v, (1,) * len(lead) + (t,))[..., t:t + t * (2 * t - 1)]
    return flat.reshape(lead + (t, 2 * t - 1))[..., :t]


def _flash_tables(rel_bias):
    t = ATT_TILE
    d0 = jnp.arange(-t, t)
    kinds = jnp.stack([
        _bias_of(rel_bias, d0, d0 >= 0),
        _bias_of(rel_bias, d0 + t, d0 + t >= 0),
        _bias_of(rel_bias, jnp.full((2 * t,), 2 * t), jnp.ones((2 * t,), bool)),
        _bias_of(rel_bias, d0 + WINDOW, d0 + WINDOW < WINDOW),
    ])
    tab = _toeplitz(kinds.transpose(2, 0, 1), t)
    return tab.transpose(1, 2, 0, 3).reshape(4, t, NSA_HEADS * t)


def _cmp_bias_table(rel_bias, s, n_sub, n_cmp, pos0):
    na = s // CMP_STRIDE
    m = max(na, n_sub)
    k = jnp.arange(-m, m)[None, :]
    r = jnp.arange(CMP_STRIDE)[:, None]
    d = CMP_STRIDE * k + r - (CMP_BLOCK - 1) + pos0
    v = _bias_of(rel_bias, d, d >= 0).transpose(2, 0, 1)
    tz = _toeplitz(v, m)[:, :, :n_sub, :na]
    tab = tz.transpose(0, 3, 1, 2).reshape(NSA_HEADS, s, n_sub)
    return jnp.where(jnp.arange(n_sub)[None, None, :] < n_cmp, tab, NEG)


def _prompt_pre(x, mem, w, rel_bias, cnt0):
    b, s, _ = x.shape
    n = b * s
    tm = 512 if n % 512 == 0 else ATT_TILE
    x2d = x.reshape(n, D_MODEL)
    q, qm, kvs, vts, kvw, vtw, kvc, u, gn = _project_in(x2d, w, tm)
    r3 = lambda a: a.reshape(b, s, a.shape[-1])
    q, qm, kvs, kvw, kvc, u, gn = map(r3, (q, qm, kvs, kvw, kvc, u, gn))

    o_pool = _pool_mix(u, jnp.zeros((b, 16, POOL_WIDTH), F32), w["w_pool"], w["pool_scale"], 0)

    n_cmp = (s - CMP_BLOCK) // CMP_STRIDE + 1
    n_slc = -(-s // SLC_BLOCK)
    parts = _cmp_partials_dense(kvc, w["pe_c"], w["w_c"])
    n_sub = parts.shape[1]
    bias_c = _cmp_bias_table(rel_bias, s, n_sub, n_cmp, 0)
    tq = 256 if s % 256 == 0 else ATT_TILE
    o_cmp, ns0, ns1 = _cmp_attention(q, parts, bias_c, gn, w["gk_cmp"], tq=tq, n_cmp=n_cmp, n_slc=n_slc, pos0=0)

    tab = _flash_tables(rel_bias)
    o_slc = _flash_attention(q, ns0, ns1, kvs, vts, tab, gn, use_sel=True, band=None, gate_base=NSA_HEADS)
    o_win = _flash_attention(q, ns0, ns1, kvw, vtw, tab, gn, use_sel=False, band=WINDOW // ATT_TILE,
                             gate_base=2 * NSA_HEADS)

    m = mem.shape[1]
    (mem_kv,) = _project(mem.reshape(b * m, D_MODEL), w["mem_g"], w["w_mem"], w["mem_gain"], w["mem_nmask"],
                         ((0, 2 * MEM_WIDTH, "id"),), (F32,), MXU_DIM, tm=min(512, b * m))
    mem_kv = mem_kv.reshape(b, m, 2 * MEM_WIDTH)
    o_mem = _mem_attention(qm, mem_kv, tq=min(512, s))

    f2 = lambda a: a.reshape(n, a.shape[-1])
    x2, h2, eidx, gates, cnt = _layer_tail(x2d, f2(o_pool), f2(o_cmp), f2(o_slc), f2(o_win), f2(o_mem), cnt0, w, tm)
    kvshape = (b, s, 2, NSA_KV_HEADS, HEAD_DIM)
    win_buf = WINDOW
    states = (kvc.reshape(kvshape), kvs.reshape(kvshape),
              _last_rows(kvw, win_buf).reshape(b, win_buf, 2, NSA_KV_HEADS, HEAD_DIM),
              mem_kv.reshape(b, m, 2, MEM_HEADS, HEAD_DIM), _last_rows(u, POOL_BUF))
    return (x2, h2, eidx, gates), states, cnt


def _last_rows(a, n):
    t = a.shape[1]
    if t < n:
        a = jnp.pad(a, [(0, 0), (n - t, 0)] + [(0, 0)] * (a.ndim - 2))
    return a[:, a.shape[1] - n:]


def _dec_columns_nsa(q):
    b, t, _ = q.shape
    qh = q.reshape(b, t, NSA_KV_HEADS, NSA_HPG, HEAD_DIM)
    w = jnp.einsum("btgpd,gx->bxdgtp", qh.astype(F32), jnp.eye(NSA_KV_HEADS, dtype=F32))
    w = w.reshape(b, NSA_KV_HEADS * HEAD_DIM, NSA_KV_HEADS * t * NSA_HPG)
    return jnp.pad(w, ((0, 0), (0, KV_WIDTH - w.shape[1]), (0, LANES - w.shape[2]))).astype(BF16)


def _dec_extract_nsa(o, t):
    b = o.shape[0]
    v = o[:, LANES:, :NSA_KV_HEADS * t * NSA_HPG]
    v = v.reshape(b, NSA_KV_HEADS, HEAD_DIM, NSA_KV_HEADS, t, NSA_HPG)
    v = jnp.einsum("bxdgtp,gx->btgpd", v, jnp.eye(NSA_KV_HEADS, dtype=F32))
    return v.reshape(b, t, NSA_WIDTH)


def _dec_bias_cols(bias_tph):
    k, t, _ = bias_tph.shape
    bt = bias_tph.reshape(k, t, NSA_KV_HEADS, NSA_HPG).transpose(0, 2, 1, 3).reshape(k, NSA_KV_HEADS * t * NSA_HPG)
    return jnp.pad(bt, ((0, 0), (0, LANES - bt.shape[1])))


def _dec_gate_cols(gn, base, t):
    b = gn.shape[0]
    gt = gn[:, :, base:base + NSA_HEADS].reshape(b, t, NSA_KV_HEADS, NSA_HPG).transpose(0, 2, 1, 3)
    gt = gt.reshape(b, 1, NSA_KV_HEADS * t * NSA_HPG)
    return jnp.pad(gt, ((0, 0), (0, 0), (0, LANES - gt.shape[2])), constant_values=1.0)


def _sample_pre(x, cache_cmp, cache_slc, cache_win, cache_mem, pool_buf, page_table, w, rel_bias, cnt0):
    b, t, _ = x.shape
    n = b * t
    page = cache_cmp.shape[1]
    n_pages = page_table.shape[1]
    past = n_pages * page
    x2d = x.reshape(n, D_MODEL)
    q, qm, kvs, _, kvw, _, kvc, u, gn = _project_in(x2d, w, n if n <= 512 else ATT_TILE)
    r3 = lambda a: a.reshape(b, t, a.shape[-1])
    q, qm, kvs, kvw, kvc, u, gn = map(r3, (q, qm, kvs, kvw, kvc, u, gn))
    qpos = past + jnp.arange(t)

    buf16 = jnp.pad(pool_buf, ((0, 0), (16 - POOL_BUF, 0), (0, 0)))
    o_pool = _pool_mix(u, buf16, w["w_pool"], w["pool_scale"], past)

    total = past + t
    n_cmp = (total - CMP_BLOCK) // CMP_STRIDE + 1
    n_sub_used = n_cmp + CMP_BLOCK // CMP_STRIDE - 1
    n_slc = -(-total // SLC_BLOCK)
    pps = math.gcd(n_pages, 16)
    parts = _cmp_partials_paged(_feature_major(cache_cmp), page_table, w["pe_c"], w["w_c"], pps)
    extra = n_sub_used * CMP_STRIDE - past
    if extra > 0:
        tail_rows = -(-extra // CMP_STRIDE) * CMP_STRIDE
        new_c = jnp.pad(kvc, ((0, 0), (0, max(0, tail_rows - t)), (0, 0)))[:, :tail_rows]
        parts = jnp.concatenate([parts, _cmp_partials_dense(new_c, w["pe_c"], w["w_c"])], axis=1)
    n_sub = parts.shape[1]
    end = jnp.arange(n_sub)[None, :] * CMP_STRIDE + CMP_BLOCK - 1
    bias_c = _bias_of(rel_bias, qpos[:, None] - end, (end <= qpos[:, None]) & (jnp.arange(n_sub)[None, :] < n_cmp))
    qpad = ((0, 0), (0, LANES - t), (0, 0))
    bias_c = jnp.pad(bias_c.transpose(2, 0, 1), qpad)
    o_cmp, ns0, ns1 = _cmp_attention(jnp.pad(q, qpad), parts, bias_c, jnp.pad(gn, qpad), w["gk_cmp"], tq=LANES,
                                     n_cmp=n_cmp, n_slc=n_slc, pos0=past)
    o_cmp = o_cmp[:, :t]

    wq = _dec_columns_nsa(q).transpose(0, 2, 1)
    ncol = NSA_KV_HEADS * t * NSA_HPG
    new_tile = lambda kv: jnp.pad(kv, ((0, 0), (0, LANES - t), (0, 0))).transpose(0, 2, 1)
    own = lambda npg: jnp.broadcast_to(jnp.arange(b, dtype=jnp.int32)[:, None], (b, npg))
    blocks = lambda npg: jnp.broadcast_to(jnp.arange(npg, dtype=jnp.int32)[None, :], (b, npg))

    n_chunks = n_pages + 1
    nblk = -(-2 * n_chunks // 8) * 8
    ns = jnp.stack([ns0, ns1], axis=1)[:, :, :, :t].transpose(0, 1, 3, 2)
    ns = jnp.pad(ns, ((0, 0), (0, 0), (0, 0), (0, max(0, nblk - ns.shape[3]))))[..., :nblk]
    ns = jnp.broadcast_to(ns[:, :, :, None, :], (b, NSA_KV_HEADS, t, NSA_HPG, nblk)).reshape(b, ncol, nblk)
    notsel = jnp.pad(ns.transpose(0, 2, 1), ((0, 0), (0, 0), (0, LANES - ncol)))
    rows = jnp.arange(LANES)
    far = _bias_of(rel_bias, jnp.full((LANES, t), 2 * MAX_DISTANCE), jnp.ones((LANES, t), bool))
    kpos_last = past - LANES + rows
    d_last = qpos[None, :] - kpos_last[:, None]
    near = _bias_of(rel_bias, d_last, d_last >= 0)
    kpos_new = past + rows
    d_new = qpos[None, :] - kpos_new[:, None]
    newb = _bias_of(rel_bias, d_new, (d_new >= 0) & (rows[:, None] < t))
    bias_tab = jnp.stack([_dec_bias_cols(far), _dec_bias_cols(near), _dec_bias_cols(newb)])
    bias_idx = jnp.concatenate([jnp.zeros((n_pages - 1,), jnp.int32), jnp.array([1, 2], jnp.int32)])
    o_slc = _decode_attention(_feature_major(cache_slc), page_table, jnp.zeros_like(page_table), new_tile(kvs), wq,
                              bias_tab, bias_idx, notsel, _dec_gate_cols(gn, NSA_HEADS, t))
    o_slc = _dec_extract_nsa(o_slc, t)

    wb = cache_win.shape[1]
    n_wpg = wb // LANES
    kpos_w = past - wb + jnp.arange(wb + LANES)
    d_w = qpos[None, :] - kpos_w[:, None]
    valid_w = (d_w >= 0) & (d_w < WINDOW) & (kpos_w[:, None] >= 0) & (jnp.arange(wb + LANES)[:, None] < wb + t)
    bias_w = _dec_bias_cols(_bias_of(rel_bias, d_w, valid_w)).reshape(n_wpg + 1, LANES, LANES)
    zeros_ns = jnp.zeros((b, -(-2 * (n_wpg + 1) // 8) * 8, LANES), F32)
    o_win = _decode_attention(_feature_major(cache_win), own(n_wpg), blocks(n_wpg), new_tile(kvw), wq, bias_w,
                              jnp.arange(n_wpg + 1, dtype=jnp.int32), zeros_ns, _dec_gate_cols(gn, 2 * NSA_HEADS, t))
    o_win = _dec_extract_nsa(o_win, t)

    m = cache_mem.shape[1]
    n_mpg = m // LANES
    qmh = qm.reshape(b, t, MEM_HEADS, HEAD_DIM).astype(F32)
    wqm = jnp.einsum("bthd,hx->bxdht", qmh, jnp.eye(MEM_HEADS, dtype=F32))
    wqm = wqm.reshape(b, MEM_WIDTH, MEM_HEADS * t)
    wqm = jnp.pad(wqm, ((0, 0), (0, MEM_WIDTH), (0, LANES - MEM_HEADS * t))).astype(BF16).transpose(0, 2, 1)
    o_mem = _decode_attention(_feature_major(cache_mem), own(n_mpg), blocks(n_mpg), None, wqm,
                              jnp.zeros((1, LANES, LANES), F32), jnp.zeros((n_mpg,), jnp.int32),
                              jnp.zeros((b, 8, LANES), F32), jnp.ones((b, 1, LANES), F32))
    om = o_mem[:, MEM_WIDTH:, :MEM_HEADS * t].reshape(b, MEM_HEADS, HEAD_DIM, MEM_HEADS, t)
    o_mem = jnp.einsum("bxdht,hx->bthd", om, jnp.eye(MEM_HEADS, dtype=F32)).reshape(b, t, MEM_WIDTH)

    f2 = lambda a: a.reshape(n, a.shape[-1])
    x2, h2, eidx, gates, cnt = _layer_tail(x2d, f2(o_pool), f2(o_cmp), f2(o_slc), f2(o_win), f2(o_mem), cnt0, w,
                                           n if n <= 512 else LANES)
    kvshape = (b, t, 2, NSA_KV_HEADS, HEAD_DIM)
    new_win = jnp.concatenate([cache_win.reshape(b, wb, KV_WIDTH), kvw], axis=1)[:, t:]
    new_pool = jnp.concatenate([pool_buf, u], axis=1)[:, t:]
    states = (kvc.reshape(kvshape), kvs.reshape(kvshape), new_win, new_pool)
    return (x2, h2, eidx, gates), states, cnt


def kernel(x_prompt, x_sample, cache_cmp_kv, cache_slc_kv, cache_win_kv, cache_mem_kv, state_pool, page_table,
           mem_prompt, rel_bias, norm1_g, w_in, nsa_qk_norm, mem_qk_norm, cmp_w, cmp_pe, pool_w, pool_scale,
           mem_norm_g, w_mem_kv, w_up_pool, w_up_nsa, w_up_mem, w_out, norm2_g, router_w, router_b, w_gu, b_gu,
           w_down, b_down):
    depth = w_in.shape[0]
    yp, ys = x_prompt, x_sample
    bp, sp, _ = x_prompt.shape
    bs, ts, _ = x_sample.shape
    outs_p = [[] for _ in range(5)]
    outs_s = [[] for _ in range(4)]
    for l in range(depth):
        w = _prep_layer(l, rel_bias, norm1_g, w_in, nsa_qk_norm, mem_qk_norm, cmp_w, cmp_pe, pool_w, pool_scale,
                        mem_norm_g, w_mem_kv, w_up_pool, w_up_nsa, w_up_mem, w_out, norm2_g, router_w, router_b,
                        w_gu, b_gu, w_down, b_down)
        pre_p, st_p, cnt = _prompt_pre(yp, mem_prompt, w, rel_bias, jnp.zeros((1, LANES), F32))
        pre_s, st_s, cnt = _sample_pre(ys, cache_cmp_kv[l], cache_slc_kv[l], cache_win_kv[l], cache_mem_kv[l],
                                       state_pool[l], page_table, w, rel_bias, cnt)
        yp, ys = _moe([pre_p, pre_s], cnt, w)
        yp = yp.reshape(bp, sp, D_MODEL)
        ys = ys.reshape(bs, ts, D_MODEL)
        for lst, a in zip(outs_p, st_p):
            lst.append(a)
        for lst, a in zip(outs_s, st_s):
            lst.append(a)
    new_cmp_p, new_slc_p, new_win_p, new_mem_p, new_pool_p = [jnp.stack(a) for a in outs_p]
    new_cmp_s, new_slc_s, new_win_s, new_pool_s = [jnp.stack(a) for a in outs_s]
    new_win_s = new_win_s.reshape(new_win_s.shape[:3] + (2, NSA_KV_HEADS, HEAD_DIM))
    return (yp, ys, new_cmp_p, new_slc_p, new_win_p, new_mem_p, new_pool_p,
            new_cmp_s, new_slc_s, new_win_s, new_pool_s)
```

```python
import functools
import math

import jax
import jax.numpy as jnp
from jax import lax
from jax.experimental import pallas as pl
from jax.experimental.pallas import tpu as pltpu

F32 = jnp.float32
BF16 = jnp.bfloat16

D_MODEL = 1024
HEAD_DIM = 64
POOL_WINDOWS = (2, 4, 8, 16)
POOL_GROUP = 64
POOL_WIDTH = 256
POOL_BUF = 15
NSA_HEADS = 8
NSA_KV_HEADS = 2
NSA_HPG = 4
NSA_WIDTH = 512
KV_WIDTH = 256
CMP_BLOCK = 32
CMP_STRIDE = 16
SLC_BLOCK = 64
SLC_TOPK = 16
WINDOW = 512
MEM_HEADS = 4
MEM_WIDTH = 256
NUM_BUCKETS = 32
MAX_DISTANCE = 128
N_EXPERTS = 32
TOP_K = 4
D_FF = 1024
SWIGLU_ALPHA = 1.702
SWIGLU_LIMIT = 7.0
EPS = 1e-6
SCALE = HEAD_DIM ** -0.5

LANES = 128
MXU_DIM = 256
NEG = -1e30
ATT_TILE = 256
DEC_PAGES_PER_STEP = 16
MOE_TILE = 512
MOE_DMA_TOKENS = 512
VMEM_LIMIT = 48 * 1024 * 1024
FFN_VMEM_LIMIT = 56 * 1024 * 1024


def _cparams(*sem):
    return pltpu.CompilerParams(dimension_semantics=sem, vmem_limit_bytes=VMEM_LIMIT)


def _dot(a, b):
    return jnp.dot(a, b, preferred_element_type=F32)


def _dot_nt(a, b):
    return lax.dot_general(a, b, (((1,), (1,)), ((), ())), preferred_element_type=F32)


def _split_dot(a, b):
    hi = a.astype(BF16)
    lo = (a - hi.astype(F32)).astype(BF16)
    return _dot(hi, b) + _dot(lo, b)


def _rms(x, g):
    r = lax.rsqrt(jnp.mean(x * x, axis=-1, keepdims=True) + EPS)
    return (x * r) * g


def _sigmoid(x):
    return 1.0 / (1.0 + jnp.exp(-x))


def _lane_iota(shape):
    return lax.broadcasted_iota(jnp.int32, shape, len(shape) - 1)


def _row_iota(shape):
    return lax.broadcasted_iota(jnp.int32, shape, len(shape) - 2)


def _proj_kernel(x_ref, g_ref, w_ref, gain_ref, nmask_ref, seg_ref, *rest, segs, n_norm, has_wt):
    wt_ref = rest[0] if has_wt else None
    out_refs = rest[1:] if has_wt else rest
    h = _rms(x_ref[...], g_ref[...]).astype(BF16)
    seg = seg_ref[...]
    outs = iter(out_refs)
    for (start, width, kind, forms, _) in segs:
        if forms == ("t",) and has_wt:
            next(outs)[...] = _dot_nt(wt_ref[...], h)
            continue
        z = _dot(h, w_ref[:, start:start + width])
        if start < n_norm:
            pieces = []
            for c in range(0, width, MXU_DIM):
                zc = z[:, c:c + MXU_DIM]
                ms = _split_dot(zc * zc, seg)
                zn = (zc * lax.rsqrt(ms + EPS)) * gain_ref[:, start + c:start + c + MXU_DIM]
                pieces.append(jnp.where(nmask_ref[:, start + c:start + c + MXU_DIM] > 0, zn, zc))
            z = pieces[0] if len(pieces) == 1 else jnp.concatenate(pieces, axis=1)
        if kind == "sigmoid":
            z = _sigmoid(z)
        elif kind == "qscale":
            z = z * SCALE
        for form in forms:
            o_ref = next(outs)
            if form == "rows":
                o_ref[...] = z.astype(o_ref.dtype)
            elif form == "key_rows":
                o_ref[...] = z[:, 0:LANES]
            else:
                o_ref[...] = z.T


def _project(x, g, w, gain, nmask, segs, n_norm, tm, seq, wt=None):
    n = x.shape[0]
    ncol = w.shape[1]
    seg = _seg_matrix(MXU_DIM)
    full = lambda i: (0, 0)
    tpb = seq // tm
    out_specs, out_shape = [], []
    for (_, wd, _, forms, dt) in segs:
        for form in forms:
            if form == "rows":
                out_specs.append(pl.BlockSpec((tm, wd), lambda i: (i, 0)))
                out_shape.append(jax.ShapeDtypeStruct((n, wd), dt))
            elif form == "key_rows":
                out_specs.append(pl.BlockSpec((tm, LANES), lambda i: (i, 0)))
                out_shape.append(jax.ShapeDtypeStruct((n, LANES), F32))
            else:
                out_specs.append(pl.BlockSpec((None, wd, tm), lambda i: (i // tpb, 0, i % tpb)))
                out_shape.append(jax.ShapeDtypeStruct((n // seq, wd, seq), F32))
    extra = () if wt is None else (wt,)
    return pl.pallas_call(
        functools.partial(_proj_kernel, segs=segs, n_norm=n_norm, has_wt=wt is not None),
        grid=(n // tm,),
        in_specs=[
            pl.BlockSpec((tm, D_MODEL), lambda i: (i, 0)),
            pl.BlockSpec((1, D_MODEL), full),
            pl.BlockSpec((D_MODEL, ncol), full),
            pl.BlockSpec((1, gain.shape[1]), full),
            pl.BlockSpec((1, nmask.shape[1]), full),
            pl.BlockSpec((MXU_DIM, MXU_DIM), full),
        ] + [pl.BlockSpec(a.shape, full) for a in extra],
        out_specs=out_specs,
        out_shape=out_shape,
        compiler_params=_cparams("parallel"),
        name="proj",
    )(x, g, w, gain, nmask, seg, *extra)


def _seg_matrix(n):
    i = jnp.arange(n) // HEAD_DIM
    return jnp.where(i[:, None] == i[None, :], 1.0 / HEAD_DIM, 0.0).astype(BF16)


def _pool_kernel(u_ref, buf_ref, w_ref, scale_ref, o_ref, zs_ref, *, t, pos0):
    zs_ref[0:16, :] = buf_ref[...]
    zs_ref[16:16 + t, :] = u_ref[...]
    u = u_ref[...]
    lane = _lane_iota((1, POOL_WIDTH))
    pos = (pos0 + _row_iota((t, 1))).astype(F32)
    acc = u
    mean = None
    for i in range(1, max(POOL_WINDOWS)):
        acc = acc + zs_ref[16 - i:16 - i + t, :]
        if i + 1 in POOL_WINDOWS:
            gi = POOL_WINDOWS.index(i + 1)
            m = acc / jnp.minimum(pos + 1.0, float(i + 1))
            mean = m if mean is None else jnp.where(lane >= gi * POOL_GROUP, m, mean)
    d = (mean - u).astype(BF16)
    o_ref[...] = _dot(d, w_ref[...]) * scale_ref[...]


def _pool_mix(u, buf16, w_bd, scale, pos0):
    b, t, _ = u.shape
    return pl.pallas_call(
        functools.partial(_pool_kernel, t=t, pos0=pos0),
        grid=(b,),
        in_specs=[
            pl.BlockSpec((None, t, POOL_WIDTH), lambda i: (i, 0, 0)),
            pl.BlockSpec((None, 16, POOL_WIDTH), lambda i: (i, 0, 0)),
            pl.BlockSpec((POOL_WIDTH, POOL_WIDTH), lambda i: (0, 0)),
            pl.BlockSpec((1, POOL_WIDTH), lambda i: (0, 0)),
        ],
        out_specs=pl.BlockSpec((None, t, POOL_WIDTH), lambda i: (i, 0, 0)),
        out_shape=jax.ShapeDtypeStruct((b, t, POOL_WIDTH), F32),
        scratch_shapes=[pltpu.VMEM((t + 16, POOL_WIDTH), F32)],
        compiler_params=_cparams("parallel"),
        name="pool",
    )(u, buf16, w_bd, scale)


def _cpart_compute(x_refs, pe_ref, w_ref, o_ref, m):
    for c in range(2):
        acc = jnp.zeros((m + 8, KV_WIDTH), F32)
        for j in range(CMP_STRIDE):
            lhs = jnp.concatenate([x_refs[c][pl.ds(j, m, stride=CMP_STRIDE), :], pe_ref[c, j]], axis=0)
            acc = acc + _dot(lhs.astype(BF16), w_ref[c, j])
        lane = _lane_iota((1, KV_WIDTH))
        pe_term = jnp.where(lane < LANES, acc[m:m + 1], acc[m + 1:m + 2])
        o_ref[:, c * KV_WIDTH:(c + 1) * KV_WIDTH] = acc[0:m] + pe_term


def _cpart_kernel(xk_ref, xv_ref, pe_ref, w_ref, o_ref, *, rows):
    _cpart_compute((xk_ref, xv_ref), pe_ref, w_ref, o_ref, rows // CMP_STRIDE)


def _cpart_paged_kernel(*refs, nop, page):
    page_refs = refs[2:2 + nop]
    pe_ref, w_ref, o_ref, xk_ref, xv_ref = refs[2 + nop:]
    for k, r in enumerate(page_refs):
        xk_ref[k * page:(k + 1) * page, :] = r[0:LANES, :].T
        xv_ref[k * page:(k + 1) * page, :] = r[LANES:2 * LANES, :].T
    _cpart_compute((xk_ref, xv_ref), pe_ref, w_ref, o_ref, nop * page // CMP_STRIDE)


def _cmp_partials_dense(kv, pe, w_c):
    b, t, _ = kv.shape
    rows = (t // CMP_STRIDE) * CMP_STRIDE
    n = rows // CMP_STRIDE
    return pl.pallas_call(
        functools.partial(_cpart_kernel, rows=rows),
        grid=(b,),
        in_specs=[
            pl.BlockSpec((None, rows, LANES), lambda i: (i, 0, 0)),
            pl.BlockSpec((None, rows, LANES), lambda i: (i, 0, 1)),
            pl.BlockSpec(pe.shape, lambda i: (0, 0, 0, 0)),
            pl.BlockSpec(w_c.shape, lambda i: (0, 0, 0, 0)),
        ],
        out_specs=pl.BlockSpec((None, n, 2 * KV_WIDTH), lambda i: (i, 0, 0)),
        out_shape=jax.ShapeDtypeStruct((b, n, 2 * KV_WIDTH), F32),
        compiler_params=_cparams("parallel"),
        name="cmp_partials",
    )(kv, kv, pe, w_c)


def _cmp_partials_paged(pool_t, phys, lblk, pe, w_c, pages_per_step):
    b, n_pages = phys.shape
    page = LANES
    nop = pages_per_step
    n = nop * page // CMP_STRIDE

    def page_spec(k):
        return pl.BlockSpec((None, KV_WIDTH, page),
                            lambda i, c, ph, lb: (ph[i, c * nop + k], 0, lb[i, c * nop + k]))

    grid_spec = pltpu.PrefetchScalarGridSpec(
        num_scalar_prefetch=2,
        grid=(b, n_pages // nop),
        in_specs=[page_spec(k) for k in range(nop)] + [
            pl.BlockSpec(pe.shape, lambda i, c, ph, lb: (0, 0, 0, 0)),
            pl.BlockSpec(w_c.shape, lambda i, c, ph, lb: (0, 0, 0, 0)),
        ],
        out_specs=pl.BlockSpec((None, n, 2 * KV_WIDTH), lambda i, c, ph, lb: (i, c, 0)),
        scratch_shapes=[pltpu.VMEM((nop * page, LANES), F32), pltpu.VMEM((nop * page, LANES), F32)],
    )
    return pl.pallas_call(
        functools.partial(_cpart_paged_kernel, nop=nop, page=page),
        grid_spec=grid_spec,
        out_shape=jax.ShapeDtypeStruct((b, n_pages * page // CMP_STRIDE, 2 * KV_WIDTH), F32),
        compiler_params=_cparams("parallel", "arbitrary"),
        name="cmp_partials_paged",
    )(phys, lblk, *([pool_t] * nop), pe, w_c)


def _group_query_columns(q_ref, g, t):
    zeros64 = jnp.zeros((HEAD_DIM, t), BF16)
    cols = []
    for pr in range(2):
        qt = q_ref[:, (2 * g + pr) * LANES:(2 * g + pr + 1) * LANES].astype(F32).T.astype(BF16)
        for half in range(2):
            qh = qt[half * HEAD_DIM:(half + 1) * HEAD_DIM]
            cols.append(jnp.concatenate([qh, zeros64] if g == 0 else [zeros64, qh], axis=0))
    return jnp.concatenate(cols, axis=1)


def _store_group_output(o_ref, out_t, g, t):
    for pr in range(2):
        pair = jnp.concatenate([out_t[:, (2 * pr) * t:(2 * pr + 1) * t], out_t[:, (2 * pr + 1) * t:(2 * pr + 2) * t]],
                               axis=0)
        o_ref[:, (2 * g + pr) * LANES:(2 * g + pr + 1) * LANES] = pair.T


def _cattn_kernel(q_ref, p_ref, bias_ref, gn_ref, gk_ref, seg_ref, o_ref, ns0_ref, ns1_ref, *,
                  tq, n_sub, n_cmp, n_slc, nslp, pos0):
    qi = pl.program_id(0)
    pall = p_ref[...]
    kraw = pall[:, 0:LANES] + pltpu.roll(pall[:, LANES:2 * LANES], n_sub - 1, 0)
    vc = pall[:, 2 * LANES:3 * LANES] + pltpu.roll(pall[:, 3 * LANES:4 * LANES], n_sub - 1, 0)
    ms = _split_dot(kraw * kraw, seg_ref[...])
    kc = ((kraw * lax.rsqrt(ms + EPS)) * gk_ref[...]).astype(BF16)
    vct = vc.T.astype(BF16)

    nsel = -(-n_slc // 8) * 8
    jj = _row_iota((nsel, n_sub))
    nn = _lane_iota((nsel, n_sub))
    covers_t = ((nn * CMP_STRIDE < (jj + 1) * SLC_BLOCK) & (nn * CMP_STRIDE + CMP_BLOCK - 1 >= jj * SLC_BLOCK)
                & (nn < n_cmp) & (jj < n_slc))
    covers_t = jnp.where(covers_t, 1.0, 0.0).astype(BF16)
    qpos = pos0 + qi * tq + _lane_iota((1, tq))
    qblk = jnp.right_shift(qpos, SLC_BLOCK.bit_length() - 1)
    jr = _row_iota((nsel, tq))
    jrf = jr.astype(F32)
    forced = (jr == 0) | (jr == qblk) | (jr == qblk - 1)
    causal = jr <= qblk
    gnt = gn_ref[...].T

    for g in range(NSA_KV_HEADS):
        bias = bias_ref[g]
        s = _dot(kc, _group_query_columns(q_ref, g, tq)) + bias
        m = jnp.max(s, axis=0, keepdims=True)
        m = jnp.where(m > 0.5 * NEG, m, 0.0)
        e = jnp.where(bias > 0.5 * NEG, jnp.exp(s - m), 0.0)
        p = e / jnp.maximum(jnp.sum(e, axis=0, keepdims=True), 1e-30)
        h0 = NSA_HPG * g
        gate = jnp.concatenate([gnt[h0 + c:h0 + c + 1] for c in range(NSA_HPG)], axis=1)
        out_t = _dot(vct[g * HEAD_DIM:(g + 1) * HEAD_DIM], p.astype(BF16)) * gate
        _store_group_output(o_ref, out_t, g, tq)

        prsum = p[:, 0:tq] + p[:, tq:2 * tq] + p[:, 2 * tq:3 * tq] + p[:, 3 * tq:4 * tq]
        hi = prsum.astype(BF16)
        lo = (prsum - hi.astype(F32)).astype(BF16)
        imp = _dot(covers_t, hi) + _dot(covers_t, lo)
        score = jnp.where(forced, jnp.inf, imp)
        score = jnp.where(causal, score, -jnp.inf)
        sel = jnp.zeros((nsel, tq), F32)
        for _ in range(min(SLC_TOPK, n_slc)):
            m = jnp.max(score, axis=0, keepdims=True)
            idx = jnp.min(jnp.where(score == m, jrf, 1e9), axis=0, keepdims=True)
            pick = jrf == idx
            sel = jnp.where(pick & (m > -jnp.inf), 1.0, sel)
            score = jnp.where(pick, -jnp.inf, score)
        ns = 1.0 - sel
        if nslp > nsel:
            ns = jnp.concatenate([ns, jnp.ones((nslp - nsel, tq), F32)], axis=0)
        (ns0_ref if g == 0 else ns1_ref)[...] = ns


def _cmp_attention(q, parts, bias, gn, gk, *, tq, n_cmp, n_slc, pos0):
    b, s, _ = q.shape
    n_sub = parts.shape[1]
    nslp = -(-n_slc // LANES) * LANES
    bias_t = bias.reshape(NSA_KV_HEADS, NSA_HPG, s // tq, tq, n_sub).transpose(0, 4, 2, 1, 3)
    bias_t = bias_t.reshape(NSA_KV_HEADS, n_sub, NSA_HPG * s)
    kern = functools.partial(_cattn_kernel, tq=tq, n_sub=n_sub, n_cmp=n_cmp, n_slc=n_slc, nslp=nslp, pos0=pos0)
    return pl.pallas_call(
        kern,
        grid=(s // tq, b),
        in_specs=[
            pl.BlockSpec((None, tq, NSA_WIDTH), lambda i, j: (j, i, 0)),
            pl.BlockSpec((None, n_sub, 2 * KV_WIDTH), lambda i, j: (j, 0, 0)),
            pl.BlockSpec((NSA_KV_HEADS, n_sub, NSA_HPG * tq), lambda i, j: (0, 0, i)),
            pl.BlockSpec((None, tq, LANES), lambda i, j: (j, i, 0)),
            pl.BlockSpec((1, LANES), lambda i, j: (0, 0)),
            pl.BlockSpec((LANES, LANES), lambda i, j: (0, 0)),
        ],
        out_specs=[
            pl.BlockSpec((None, tq, NSA_WIDTH), lambda i, j: (j, i, 0)),
            pl.BlockSpec((None, nslp, tq), lambda i, j: (j, 0, i)),
            pl.BlockSpec((None, nslp, tq), lambda i, j: (j, 0, i)),
        ],
        out_shape=[
            jax.ShapeDtypeStruct((b, s, NSA_WIDTH), F32),
            jax.ShapeDtypeStruct((b, nslp, s), F32),
            jax.ShapeDtypeStruct((b, nslp, s), F32),
        ],
        compiler_params=_cparams("parallel", "parallel"),
        name="cmp_attention",
    )(q, parts, bias_t, gn, gk, _seg_matrix(LANES))


def _flash_kernel(q_ref, ns0_ref, ns1_ref, kv_ref, vt_ref, tab_ref, gn_ref, o_ref, *, t, use_sel, band, gate_base):
    qi = pl.program_id(1)
    row_k = _row_iota((t, LANES))
    lane_k = _lane_iota((t, LANES))
    lo_tile = jnp.maximum(qi - band, 0) if band is not None else 0
    gnt = gn_ref[...].T

    qts = []
    for g in range(NSA_KV_HEADS):
        qt_g = _group_query_columns(q_ref, g, t)
        if use_sel:
            nst = (ns0_ref if g == 0 else ns1_ref)[...].astype(BF16)
            qt_g = jnp.concatenate([qt_g, jnp.concatenate([nst] * NSA_HPG, axis=1)], axis=0)
        qts.append(qt_g)

    def body(kj, carry):
        k0 = pl.multiple_of(kj * t, t)
        kk = kv_ref[pl.ds(k0, t), :].astype(BF16)
        if use_sel:
            blk = kj * (t // SLC_BLOCK) + jnp.right_shift(row_k, SLC_BLOCK.bit_length() - 1)
            onehot = jnp.where(lane_k == blk, -(2.0 ** 30), 0.0).astype(BF16)
            kk = jnp.concatenate([kk, onehot], axis=1)
        delta = qi - kj
        if band is None:
            kind = jnp.minimum(delta, 2)
        else:
            kind = jnp.where(delta < 2, delta, jnp.where(delta < band, 2, 3))
        m_old, l_old, acc = carry
        s = _dot(kk, qt_all) + tab_ref[kind]
        m_new = jnp.maximum(m_old, jnp.max(s, axis=0, keepdims=True))
        alpha = jnp.exp(m_old - m_new)
        p = jnp.exp(s - m_new)
        l_new = alpha * l_old + jnp.sum(p, axis=0, keepdims=True)
        pb = p.astype(BF16)
        pv = jnp.concatenate(
            [_dot(vt_ref[g * HEAD_DIM:(g + 1) * HEAD_DIM, pl.ds(k0, t)].astype(BF16), pb[:, g * gw:(g + 1) * gw])
             for g in range(NSA_KV_HEADS)], axis=1)
        return m_new, l_new, acc * alpha + pv

    gw = NSA_HPG * t
    qt_all = jnp.concatenate(qts, axis=1)
    init = (jnp.full((1, NSA_HEADS * t), NEG, F32), jnp.zeros((1, NSA_HEADS * t), F32),
            jnp.zeros((HEAD_DIM, NSA_HEADS * t), F32))
    _, l_fin, acc = lax.fori_loop(lo_tile, qi + 1, body, init)
    gate = jnp.concatenate([gnt[gate_base + h:gate_base + h + 1] for h in range(NSA_HEADS)], axis=1)
    out = acc * (gate / l_fin)
    for g in range(NSA_KV_HEADS):
        _store_group_output(o_ref, out[:, g * gw:(g + 1) * gw], g, t)


def _flash_attention(q, ns0, ns1, k_rows, kv_t, tab, gn, *, use_sel, band, gate_base):
    b, s, _ = q.shape
    t = ATT_TILE
    assert ns0.shape[1] == LANES
    kern = functools.partial(_flash_kernel, t=t, use_sel=use_sel, band=band, gate_base=gate_base)
    tile = lambda w: pl.BlockSpec((None, t, w), lambda i, j: (i, j, 0))
    ns_tile = pl.BlockSpec((None, LANES, t), lambda i, j: (i, 0, j))
    return pl.pallas_call(
        kern,
        grid=(b, s // t),
        in_specs=[
            tile(NSA_WIDTH), ns_tile, ns_tile,
            pl.BlockSpec((None, s, LANES), lambda i, j: (i, 0, 0)),
            pl.BlockSpec((None, LANES, s), lambda i, j: (i, 1, 0)),
            pl.BlockSpec(tab.shape, lambda i, j: (0, 0, 0)),
            tile(LANES),
        ],
        out_specs=tile(NSA_WIDTH),
        out_shape=jax.ShapeDtypeStruct((b, s, NSA_WIDTH), F32),
        compiler_params=_cparams("parallel", "parallel"),
        name="flash_sel" if use_sel else "flash_win",
    )(q, ns0, ns1, k_rows, kv_t, tab, gn)


def _memattn_kernel(q_ref, kv_ref, o_ref):
    lane = _lane_iota((kv_ref.shape[0], LANES))
    for pr in range(MEM_HEADS // 2):
        qpair = q_ref[:, pr * LANES:(pr + 1) * LANES]
        kblk = kv_ref[:, pr * LANES:(pr + 1) * LANES]
        vblk = kv_ref[:, MEM_WIDTH + pr * LANES:MEM_WIDTH + (pr + 1) * LANES]
        out = None
        for half in range(2):
            keep = (lane < HEAD_DIM) if half == 0 else (lane >= HEAD_DIM)
            kk = jnp.where(keep, kblk, 0.0).astype(BF16)
            vv = jnp.where(keep, vblk, 0.0).astype(BF16)
            s = _dot_nt(qpair, kk)
            m = jnp.max(s, axis=1, keepdims=True)
            e = jnp.exp(s - m)
            p = e / jnp.sum(e, axis=1, keepdims=True)
            o = _dot(p.astype(BF16), vv)
            out = o if out is None else out + o
        o_ref[:, pr * LANES:(pr + 1) * LANES] = out


def _mem_attention(qm, mem_kv, tq):
    b, s, _ = qm.shape
    m = mem_kv.shape[1]
    return pl.pallas_call(
        _memattn_kernel,
        grid=(b, s // tq),
        in_specs=[
            pl.BlockSpec((None, tq, MEM_WIDTH), lambda i, j: (i, j, 0)),
            pl.BlockSpec((None, m, 2 * MEM_WIDTH), lambda i, j: (i, 0, 0)),
        ],
        out_specs=pl.BlockSpec((None, tq, MEM_WIDTH), lambda i, j: (i, j, 0)),
        out_shape=jax.ShapeDtypeStruct((b, s, MEM_WIDTH), F32),
        compiler_params=_cparams("parallel", "parallel"),
        name="mem_attention",
    )(qm, mem_kv)


def _dec_kernel(*refs, n_pg, pps, has_new):
    bidx_ref = refs[2]
    page_refs = refs[3:3 + pps]
    new_ref, wq_ref, bias_ref, ns_ref, gate_ref, o_ref, acc_ref, m_ref, l_ref = refs[3 + pps:]
    c = pl.program_id(1)
    n_chunks = pl.num_programs(1)

    @pl.when(c == 0)
    def _():
        acc_ref[...] = jnp.zeros_like(acc_ref)
        m_ref[...] = jnp.full_like(m_ref, NEG)
        l_ref[...] = jnp.zeros_like(l_ref)

    rk = _row_iota((LANES, LANES))

    def step(tiles, first_page):
        feats = (tiles[0] if len(tiles) == 1 else jnp.concatenate(tiles, axis=1)).astype(BF16)
        s = lax.dot_general(feats, wq_ref[...], (((0,), (1,)), ((), ())),
                            preferred_element_type=F32)
        extra = []
        for k in range(len(tiles)):
            pg = first_page + k
            ns = jnp.where(rk < SLC_BLOCK, ns_ref[pl.ds(2 * pg, 1), :], ns_ref[pl.ds(2 * pg + 1, 1), :])
            extra.append(jnp.where(ns > 0.5, NEG, bias_ref[bidx_ref[pg]]))
        s = s + (extra[0] if len(extra) == 1 else jnp.concatenate(extra, axis=0))
        m_old = m_ref[...]
        m_new = jnp.maximum(m_old, jnp.max(s, axis=0, keepdims=True))
        alpha = jnp.exp(m_old - m_new)
        p = jnp.exp(s - m_new)
        l_ref[...] = alpha * l_ref[...] + jnp.sum(p, axis=0, keepdims=True)
        m_ref[...] = m_new
        acc_ref[...] = acc_ref[...] * alpha + _dot(feats, p.astype(BF16))

    if has_new:
        @pl.when(c < n_chunks - 1)
        def _():
            step([r[...] for r in page_refs], c * pps)

        @pl.when(c == n_chunks - 1)
        def _():
            step([r[...] for r in page_refs] + [new_ref[...]], n_pg - pps)
    else:
        step([r[...] for r in page_refs], c * pps)

    @pl.when(c == n_chunks - 1)
    def _():
        o_ref[...] = acc_ref[...] / l_ref[...] * gate_ref[...]


def _decode_attention(pages, phys, lblk, new_rows, wq, bias_tab, bias_idx, notsel, gate):
    bsz, n_pg = phys.shape
    w = pages.shape[1]
    has_new = new_rows is not None
    pps = math.gcd(n_pg, DEC_PAGES_PER_STEP)
    n_steps = n_pg // pps
    n_chunks = n_steps
    if not has_new:
        new_rows = jnp.zeros((1, w, LANES), F32)
    new_map = (lambda i, c, ph, lb, bi: (i, 0, 0)) if has_new else (lambda i, c, ph, lb, bi: (0, 0, 0))

    def page_spec(k):
        def index(i, c, ph, lb, bi):
            return (ph[i, c * pps + k], 0, lb[i, c * pps + k])
        return pl.BlockSpec((None, w, LANES), index)

    per_b = lambda i, c, ph, lb, bi: (i, 0, 0)
    grid_spec = pltpu.PrefetchScalarGridSpec(
        num_scalar_prefetch=3,
        grid=(bsz, n_chunks),
        in_specs=[page_spec(k) for k in range(pps)] + [
            pl.BlockSpec((None, w, LANES), new_map),
            pl.BlockSpec((None, LANES, w), per_b),
            pl.BlockSpec(bias_tab.shape, lambda i, c, ph, lb, bi: (0, 0, 0)),
            pl.BlockSpec((None, notsel.shape[1], LANES), per_b),
            pl.BlockSpec((None, 1, LANES), per_b),
        ],
        out_specs=pl.BlockSpec((None, w, LANES), per_b),
        scratch_shapes=[pltpu.VMEM((w, LANES), F32), pltpu.VMEM((1, LANES), F32), pltpu.VMEM((1, LANES), F32)],
    )
    return pl.pallas_call(
        functools.partial(_dec_kernel, n_pg=n_pg, pps=pps, has_new=has_new),
        grid_spec=grid_spec,
        out_shape=jax.ShapeDtypeStruct((bsz, w, LANES), F32),
        compiler_params=_cparams("parallel", "arbitrary"),
        name="decode_attention",
    )(phys, lblk, bias_idx, *([pages] * pps), new_rows, wq, bias_tab, notsel, gate)


def _feature_major(cache):
    n, rows = cache.shape[:2]
    return cache.transpose(0, 2, 3, 4, 1).reshape(n, -1, rows)


def _tail_kernel(x_ref, g1_ref, op_ref, oc_ref, os_ref, ow_ref, om_ref, cnt0_ref, wgb_ref, wup_p_ref, wup_n_ref,
                 wup_m_ref, wout_ref, g2_ref, rwh_ref, rwl_ref, rb_ref, x2_ref, h2_ref, ei_ref, gt_ref, cnt_ref):
    @pl.when(pl.program_id(0) == 0)
    def _():
        cnt_ref[...] = cnt0_ref[...]

    x = x_ref[...]
    h = _rms(x, g1_ref[...]).astype(BF16)
    onsa = (oc_ref[...] + os_ref[...] + ow_ref[...]).astype(BF16)
    ups = (_dot(op_ref[...].astype(BF16), wup_p_ref[...]), _dot(onsa, wup_n_ref[...]),
           _dot(om_ref[...].astype(BF16), wup_m_ref[...]))
    mixed = None
    for br in range(3):
        gb = _sigmoid(_dot(h, wgb_ref[:, br * D_MODEL:(br + 1) * D_MODEL]))
        mixed = gb * ups[br] if mixed is None else mixed + gb * ups[br]
    x2 = x + _dot(mixed.astype(BF16), wout_ref[...])
    x2_ref[...] = x2
    h2 = _rms(x2, g2_ref[...])
    h2_ref[...] = h2
    hi = h2.astype(BF16)
    lo = (h2 - hi.astype(F32)).astype(BF16)
    logits = _dot(hi, rwh_ref[...]) + _dot(lo, rwh_ref[...]) + _dot(hi, rwl_ref[...]) + rb_ref[...]
    lane = _lane_iota(logits.shape)
    lanef = lane.astype(F32)
    tops, idxs = [], []
    for _ in range(TOP_K):
        m = jnp.max(logits, axis=1, keepdims=True)
        idx = jnp.min(jnp.where(logits == m, lanef, 1e9), axis=1, keepdims=True)
        logits = jnp.where(lanef == idx, -jnp.inf, logits)
        tops.append(m)
        idxs.append(idx)
    es = [jnp.exp(tk - tops[0]) for tk in tops]
    den = es[0] + es[1] + es[2] + es[3]
    tm = logits.shape[0]
    onehot = jnp.zeros(logits.shape, F32)
    for k in range(TOP_K):
        onehot = jnp.where(lanef == idxs[k], 1.0, onehot)
    tri = jnp.where(_row_iota((tm, tm)) > _lane_iota((tm, tm)), 1.0, 0.0).astype(BF16)
    before = _dot(tri, onehot.astype(BF16)) + cnt_ref[...]
    cnt_ref[...] = cnt_ref[...] + jnp.sum(onehot, axis=0, keepdims=True)
    ei = jnp.zeros(logits.shape, F32)
    gt = jnp.zeros(logits.shape, F32)
    for k in range(TOP_K):
        rank = jnp.sum(jnp.where(lanef == idxs[k], before, 0.0), axis=1, keepdims=True)
        ei = jnp.where(lane == k, idxs[k], ei)
        ei = jnp.where(lane == TOP_K + k, rank, ei)
        gt = jnp.where(lane == k, es[k] / den, gt)
    ei_ref[...] = ei.astype(jnp.int32)
    gt_ref[...] = gt


def _layer_tail(x, o_pool, o_cmp, o_slc, o_win, o_mem, cnt0, w, tm):
    n = x.shape[0]
    row = lambda wd: pl.BlockSpec((tm, wd), lambda i: (i, 0))
    full = lambda a: pl.BlockSpec(a.shape, lambda i: (0,) * a.ndim)
    weights = (w["wgb"], w["wup_pool"], w["wup_nsa"], w["wup_mem"], w["wout"], w["g2"], w["rw_hi"], w["rw_lo"],
               w["rb"])
    return pl.pallas_call(
        _tail_kernel,
        grid=(n // tm,),
        in_specs=[row(D_MODEL), full(w["g1"]), row(POOL_WIDTH), row(NSA_WIDTH), row(NSA_WIDTH), row(NSA_WIDTH),
                  row(MEM_WIDTH), full(cnt0)] + [full(a) for a in weights],
        out_specs=[row(D_MODEL), row(D_MODEL), row(LANES), row(LANES), full(cnt0)],
        out_shape=[jax.ShapeDtypeStruct((n, D_MODEL), F32), jax.ShapeDtypeStruct((n, D_MODEL), F32),
                   jax.ShapeDtypeStruct((n, LANES), jnp.int32), jax.ShapeDtypeStruct((n, LANES), F32),
                   jax.ShapeDtypeStruct((1, LANES), F32)],
        compiler_params=_cparams("arbitrary"),
        name="layer_tail",
    )(x, w["g1"], o_pool, o_cmp, o_slc, o_win, o_mem, cnt0, *weights)


def _ffn_kernel(be_ref, nu_ref, x_ref, wgu_ref, bgu_ref, wd_ref, bd_ref, o_ref, wgu_bf, wd_bf):
    i = pl.program_id(0)

    @pl.when((i == 0) | (be_ref[i] != be_ref[jnp.maximum(i - 1, 0)]))
    def _():
        wgu_bf[...] = wgu_ref[...].astype(BF16)
        wd_bf[...] = wd_ref[...].astype(BF16)

    @pl.when(i < nu_ref[0])
    def _():
        gu = _dot(x_ref[...].astype(BF16), wgu_bf[...]) + bgu_ref[...]
        gate = jnp.minimum(gu[:, :D_FF], SWIGLU_LIMIT)
        up = jnp.clip(gu[:, D_FF:], -SWIGLU_LIMIT, SWIGLU_LIMIT)
        act = gate * _sigmoid(SWIGLU_ALPHA * gate) * (up + 1.0)
        o_ref[...] = _dot(act.astype(BF16), wd_bf[...]) + bd_ref[...]

    @pl.when(i >= nu_ref[0])
    def _():
        o_ref[...] = jnp.zeros_like(o_ref)


def _expert_ffn(rows, blk_e, n_used, wgu, bgu, wd, bd):
    n_rows = rows.shape[0]
    n_blocks = n_rows // MOE_TILE
    blk = lambda i, be, nu: (jnp.minimum(i, nu[0] - 1), 0)
    grid_spec = pltpu.PrefetchScalarGridSpec(
        num_scalar_prefetch=2,
        grid=(n_blocks,),
        in_specs=[
            pl.BlockSpec((MOE_TILE, D_MODEL), blk),
            pl.BlockSpec((None, D_MODEL, 2 * D_FF), lambda i, be, nu: (be[i], 0, 0)),
            pl.BlockSpec((None, 1, 2 * D_FF), lambda i, be, nu: (be[i], 0, 0)),
            pl.BlockSpec((None, D_FF, D_MODEL), lambda i, be, nu: (be[i], 0, 0)),
            pl.BlockSpec((None, 1, D_MODEL), lambda i, be, nu: (be[i], 0, 0)),
        ],
        out_specs=pl.BlockSpec((MOE_TILE, D_MODEL), lambda i, be, nu: (i, 0)),
        scratch_shapes=[pltpu.VMEM((D_MODEL, 2 * D_FF), BF16), pltpu.VMEM((D_FF, D_MODEL), BF16)],
    )
    return pl.pallas_call(
        _ffn_kernel,
        grid_spec=grid_spec,
        out_shape=jax.ShapeDtypeStruct((n_rows, D_MODEL), F32),
        compiler_params=pltpu.CompilerParams(dimension_semantics=("arbitrary",), vmem_limit_bytes=FFN_VMEM_LIMIT),
        name="expert_ffn",
    )(blk_e, n_used, rows, wgu, bgu, wd, bd)


def _token_tile(n):
    return next(t for t in (MOE_DMA_TOKENS, 384, 256, 128, n) if n % t == 0)


def _dispatch_scatter(dest_ref, h_ref, rows_ref, sem, td):
    def issue(i, carry):
        t0 = pl.multiple_of(i * 8, 8)
        for r in range(8):
            for k in range(TOP_K):
                d = dest_ref[i * (8 * TOP_K) + r * TOP_K + k]
                pltpu.make_async_copy(h_ref.at[pl.ds(t0 + r, 1)], rows_ref.at[pl.ds(d, 1)], sem).start()
        return carry

    lax.fori_loop(0, td // 8, issue, 0)
    for k in range(TOP_K):
        pltpu.make_async_copy(h_ref, rows_ref.at[pl.ds(0, td)], sem).wait()


def _dispatch_kernel(*refs, tiles, steps, n_blocks):
    ng = len(tiles)
    ends_ref, padded_ref, nu_ref = refs[:3]
    dest_refs = refs[3:3 + ng]
    h_refs = refs[3 + ng:3 + 2 * ng]
    rows_ref, zero_ref, sem, zsem = refs[3 + 2 * ng:]
    i = pl.program_id(0)

    @pl.when(i == 0)
    def _():
        zero_ref[...] = jnp.zeros_like(zero_ref)

        def tail_copy(e):
            start = pl.multiple_of(ends_ref[e] - MOE_TILE, MOE_TILE)
            return pltpu.make_async_copy(zero_ref, rows_ref.at[pl.ds(start, MOE_TILE)], zsem)

        def block_copy(blk):
            start = pl.multiple_of(blk * MOE_TILE, MOE_TILE)
            return pltpu.make_async_copy(zero_ref, rows_ref.at[pl.ds(start, MOE_TILE)], zsem)

        def each(start):
            def expert(e, carry):
                @pl.when(padded_ref[e] > 0)
                def _():
                    tail_copy(e).start() if start else tail_copy(e).wait()
                return carry

            def block(blk, carry):
                block_copy(blk).start() if start else block_copy(blk).wait()
                return carry

            lax.fori_loop(0, N_EXPERTS, expert, 0)
            lax.fori_loop(nu_ref[0], n_blocks, block, 0)

        each(True)
        each(False)

    first = 0
    for g in range(ng):
        @pl.when((i >= first) & (i < first + steps[g]))
        def _(g=g):
            _dispatch_scatter(dest_refs[g], h_refs[g], rows_ref, sem, tiles[g])
        first += steps[g]


def _moe_dispatch(h2s, dests, n_rows, pad_ends, padded, n_used):
    tiles = [_token_tile(h.shape[0]) for h in h2s]
    steps = [h.shape[0] // t for h, t in zip(h2s, tiles)]
    firsts = [sum(steps[:g]) for g in range(len(h2s))]

    def local(g):
        return lambda i, *_: jnp.clip(i - firsts[g], 0, steps[g] - 1)

    dest_specs = [pl.BlockSpec((tiles[g] * TOP_K,), lambda i, *_, f=local(g): (f(i),), memory_space=pltpu.SMEM)
                  for g in range(len(h2s))]
    tok_specs = [pl.BlockSpec((tiles[g], D_MODEL), lambda i, *_, f=local(g): (f(i), 0)) for g in range(len(h2s))]
    grid_spec = pltpu.PrefetchScalarGridSpec(
        num_scalar_prefetch=3,
        grid=(sum(steps),),
        in_specs=dest_specs + tok_specs,
        out_specs=pl.BlockSpec(memory_space=pl.ANY),
        scratch_shapes=[pltpu.VMEM((MOE_TILE, D_MODEL), F32), pltpu.SemaphoreType.DMA(()),
                        pltpu.SemaphoreType.DMA(())],
    )
    return pl.pallas_call(
        functools.partial(_dispatch_kernel, tiles=tiles, steps=steps, n_blocks=n_rows // MOE_TILE),
        grid_spec=grid_spec,
        out_shape=jax.ShapeDtypeStruct((n_rows, D_MODEL), F32),
        compiler_params=_cparams("arbitrary"),
        name="moe_dispatch",
    )(pad_ends, padded, n_used, *dests, *h2s)


def _combine_kernel(dest_ref, x2_ref, g_ref, rows_ref, o_ref, ybuf, sem, *, td):
    def issue(i, carry):
        t0 = pl.multiple_of(i * 8, 8)
        for r in range(8):
            for k in range(TOP_K):
                d = dest_ref[i * (8 * TOP_K) + r * TOP_K + k]
                pltpu.make_async_copy(rows_ref.at[pl.ds(d, 1)], ybuf.at[k, pl.ds(t0 + r, 1)], sem).start()
        return carry

    lax.fori_loop(0, td // 8, issue, 0)
    for k in range(TOP_K):
        pltpu.make_async_copy(rows_ref.at[pl.ds(0, td)], ybuf.at[k], sem).wait()
    out = x2_ref[...]
    for k in range(TOP_K):
        out = out + g_ref[:, k:k + 1] * ybuf[k]
    o_ref[...] = out


def _moe_combine(x2, out_rows, dest, gates):
    n = x2.shape[0]
    td = _token_tile(n)
    return pl.pallas_call(
        functools.partial(_combine_kernel, td=td),
        grid=(n // td,),
        in_specs=[pl.BlockSpec((td * TOP_K,), lambda i: (i,), memory_space=pltpu.SMEM),
                  pl.BlockSpec((td, D_MODEL), lambda i: (i, 0)),
                  pl.BlockSpec((td, LANES), lambda i: (i, 0)),
                  pl.BlockSpec(memory_space=pl.ANY)],
        out_specs=pl.BlockSpec((td, D_MODEL), lambda i: (i, 0)),
        out_shape=jax.ShapeDtypeStruct((n, D_MODEL), F32),
        scratch_shapes=[pltpu.VMEM((TOP_K, td, D_MODEL), F32), pltpu.SemaphoreType.DMA(())],
        compiler_params=_cparams("arbitrary"),
        name="moe_combine",
    )(dest, x2, gates, out_rows)


def _moe(groups, counts, w):
    n_total = sum(g[0].shape[0] for g in groups)
    cnt = counts[0, :N_EXPERTS].astype(jnp.int32)
    padded = (cnt + MOE_TILE - 1) // MOE_TILE * MOE_TILE
    pad_ends = jnp.cumsum(padded)
    pad_starts = pad_ends - padded
    n_blocks = -(-n_total * TOP_K // MOE_TILE) + N_EXPERTS
    blk_start = jnp.arange(n_blocks, dtype=jnp.int32) * MOE_TILE
    blk_e = jnp.minimum(jnp.sum(blk_start[:, None] >= pad_ends[None, :], axis=1), N_EXPERTS - 1).astype(jnp.int32)
    n_used = (pad_ends[-1] // MOE_TILE).astype(jnp.int32).reshape(1)
    dests = [(pad_starts[er[:, :TOP_K]] + er[:, TOP_K:2 * TOP_K]).astype(jnp.int32).reshape(-1)
             for (_, _, er, _) in groups]
    rows = _moe_dispatch([g[1] for g in groups], dests, n_blocks * MOE_TILE, pad_ends.astype(jnp.int32),
                         padded.astype(jnp.int32), n_used)
    out_rows = _expert_ffn(rows, blk_e, n_used, w["wgu"], w["bgu"], w["wd"], w["bd"])
    return [_moe_combine(x2, out_rows, dest, gates) for (x2, _, _, gates), dest in zip(groups, dests)]


def _rel_bucket(dist):
    n = jnp.maximum(dist, 0)
    max_exact = NUM_BUCKETS // 2
    nf = jnp.maximum(n, 1).astype(F32)
    large = max_exact + (jnp.log(nf / max_exact) / math.log(MAX_DISTANCE / max_exact)
                         * (NUM_BUCKETS - max_exact)).astype(jnp.int32)
    large = jnp.minimum(large, NUM_BUCKETS - 1)
    return jnp.where(n < max_exact, n, large)


def _bias_of(rel_bias, dist, valid):
    return jnp.where(valid[..., None], rel_bias[_rel_bucket(dist)], NEG)


def _proj_segs(kv_forms, kvc_forms):
    return ((0, 512, "qscale", ("rows",), BF16), (512, 256, "qscale", ("rows",), BF16),
            (768, 256, "id", kv_forms, F32), (1024, 256, "id", kv_forms, F32), (1280, 256, "id", kvc_forms, F32),
            (1536, 256, "id", ("rows",), F32), (1792, 128, "sigmoid", ("rows",), F32))


_PROJ_SEGS_PROMPT = _proj_segs(("key_rows", "t"), ("t",))
_PROJ_SEGS_SAMPLE = _proj_segs(("rows",), ("rows",))
_PROJ_NNORM = 1280


def _prep_layer(l, rel_bias, norm1_g, w_in, nsa_qk_norm, mem_qk_norm, cmp_w, cmp_pe, pool_w, pool_scale,
                mem_norm_g, w_mem_kv, w_up_pool, w_up_nsa, w_up_mem, w_out, norm2_g, router_w, router_b,
                w_gu, b_gu, w_down, b_down):
    wi = w_in[l]
    o_u, o_q, o_qm, o_kvc, o_kvs, o_kvw, o_gn, o_gb = 0, 256, 768, 1024, 1280, 1536, 1792, 1816
    w_proj = jnp.concatenate([
        wi[:, o_q:o_q + 512], wi[:, o_qm:o_qm + 256], wi[:, o_kvs:o_kvs + 256], wi[:, o_kvw:o_kvw + 256],
        wi[:, o_kvc:o_kvc + 256], wi[:, o_u:o_u + 256], wi[:, o_gn:o_gn + 24],
        jnp.zeros((D_MODEL, LANES - 24), F32)], axis=1).astype(BF16)
    nq, mq = nsa_qk_norm[l], mem_qk_norm[l]
    ones = jnp.ones((LANES,), F32)
    gain = jnp.concatenate([jnp.tile(nq[0], 8), jnp.tile(mq[0], 4), jnp.tile(nq[2], 2), ones,
                            jnp.tile(nq[3], 2), ones])[None, :]
    nmask = jnp.concatenate([jnp.ones((768,), F32), ones, 0 * ones, ones, 0 * ones])[None, :]
    eye4 = jnp.eye(4, dtype=F32)
    cw = cmp_w[l].reshape(2, 2, CMP_STRIDE, HEAD_DIM, HEAD_DIM)
    w_c = jnp.einsum("crjde,xy->cjxdrye", cw, jnp.eye(2, dtype=F32))
    w_c = w_c.reshape(2, CMP_STRIDE, LANES, KV_WIDTH).astype(BF16)
    pe = cmp_pe[l].reshape(2, 2, CMP_STRIDE, HEAD_DIM)
    pe_c = jnp.tile(pe.transpose(0, 2, 1, 3), (1, 1, 1, NSA_KV_HEADS))
    pe_c = jnp.pad(pe_c, ((0, 0), (0, 0), (0, 6), (0, 0)))
    w_pool = jnp.einsum("gde,gh->gdhe", pool_w[l], eye4).reshape(POOL_WIDTH, POOL_WIDTH).astype(BF16)
    rw = jnp.pad(router_w[l], ((0, 0), (0, LANES - N_EXPERTS)))
    rw_hi = rw.astype(BF16)
    rw_lo = (rw - rw_hi.astype(F32)).astype(BF16)
    rb = jnp.concatenate([router_b[l], jnp.full((LANES - N_EXPERTS,), NEG, F32)])[None, :]
    return {
        "g1": norm1_g[l][None, :], "w_proj": w_proj, "gain": gain, "nmask": nmask,
        "w_kvc_t": wi[:, o_kvc:o_kvc + 256].T.astype(BF16),
        "gk_cmp": jnp.tile(nq[1], 2)[None, :], "w_c": w_c, "pe_c": pe_c,
        "w_pool": w_pool, "pool_scale": pool_scale[l][None, :],
        "mem_g": mem_norm_g[l][None, :], "w_mem": w_mem_kv[l].astype(BF16),
        "mem_gain": jnp.concatenate([jnp.tile(mq[1], 4), jnp.ones((256,), F32)])[None, :],
        "mem_nmask": jnp.concatenate([jnp.ones((256,), F32), jnp.zeros((256,), F32)])[None, :],
        "wgb": wi[:, o_gb:o_gb + 3 * D_MODEL].astype(BF16),
        "wup_pool": w_up_pool[l].astype(BF16), "wup_nsa": w_up_nsa[l].astype(BF16),
        "wup_mem": w_up_mem[l].astype(BF16), "wout": w_out[l].astype(BF16), "g2": norm2_g[l][None, :],
        "rw_hi": rw_hi, "rw_lo": rw_lo, "rb": rb,
        "wgu": w_gu[l], "bgu": b_gu[l][:, None, :], "wd": w_down[l],
        "bd": b_down[l][:, None, :],
    }


def _project_in(x2d, w, segs, tm, seq):
    wt = w["w_kvc_t"] if segs is _PROJ_SEGS_PROMPT else None
    return _project(x2d, w["g1"], w["w_proj"], w["gain"], w["nmask"], segs, _PROJ_NNORM, tm, seq, wt)


def _rows_view(a_t, heads):
    b, _, rows = a_t.shape
    return a_t.reshape(b, 2, heads, HEAD_DIM, rows).transpose(0, 4, 1, 2, 3)


def _toeplitz(v, t):
    lead = v.shape[:-1]
    flat = jnp.tile(v, (1,) * len(lead) + (t,))[..., t:t + t * (2 * t - 1)]
    return flat.reshape(lead + (t, 2 * t - 1))[..., :t]


def _flash_tables(rel_bias):
    t = ATT_TILE
    d0 = jnp.arange(-t, t)
    kinds = jnp.stack([
        _bias_of(rel_bias, d0, d0 >= 0),
        _bias_of(rel_bias, d0 + t, d0 + t >= 0),
        _bias_of(rel_bias, jnp.full((2 * t,), 2 * t), jnp.ones((2 * t,), bool)),
        _bias_of(rel_bias, d0 + WINDOW, d0 + WINDOW < WINDOW),
    ])
    tab = _toeplitz(kinds.transpose(2, 0, 1), t)
    return tab.transpose(1, 2, 0, 3).reshape(4, t, NSA_HEADS * t)


def _cmp_bias_table(rel_bias, s, n_sub, n_cmp, pos0):
    na = s // CMP_STRIDE
    m = max(na, n_sub)
    k = jnp.arange(-m, m)[None, :]
    r = jnp.arange(CMP_STRIDE)[:, None]
    d = CMP_STRIDE * k + r - (CMP_BLOCK - 1) + pos0
    v = _bias_of(rel_bias, d, d >= 0).transpose(2, 0, 1)
    tz = _toeplitz(v, m)[:, :, :n_sub, :na]
    tab = tz.transpose(0, 3, 1, 2).reshape(NSA_HEADS, s, n_sub)
    return jnp.where(jnp.arange(n_sub)[None, None, :] < n_cmp, tab, NEG)


def _prompt_pre(x, mem, w, rel_bias, cnt0):
    b, s, _ = x.shape
    n = b * s
    tm = 512 if n % 512 == 0 else ATT_TILE
    x2d = x.reshape(n, D_MODEL)
    q, qm, ks, kvs_t, kw, kvw_t, kvc_t, u, gn = _project_in(x2d, w, _PROJ_SEGS_PROMPT, tm, s)
    r3 = lambda a: a.reshape(b, s, a.shape[-1])
    q, qm, ks, kw, u, gn = map(r3, (q, qm, ks, kw, u, gn))

    o_pool = _pool_mix(u, jnp.zeros((b, 16, POOL_WIDTH), F32), w["w_pool"], w["pool_scale"], 0)

    n_cmp = (s - CMP_BLOCK) // CMP_STRIDE + 1
    n_slc = -(-s // SLC_BLOCK)
    n_lb = s // LANES
    own = jnp.broadcast_to(jnp.arange(b, dtype=jnp.int32)[:, None], (b, n_lb))
    blocks = jnp.broadcast_to(jnp.arange(n_lb, dtype=jnp.int32)[None, :], (b, n_lb))
    parts = _cmp_partials_paged(kvc_t, own, blocks, w["pe_c"], w["w_c"], math.gcd(n_lb, 16))
    n_sub = parts.shape[1]
    bias_c = _cmp_bias_table(rel_bias, s, n_sub, n_cmp, 0)
    tq = 256 if s % 256 == 0 else ATT_TILE
    o_cmp, ns0, ns1 = _cmp_attention(q, parts, bias_c, gn, w["gk_cmp"], tq=tq, n_cmp=n_cmp, n_slc=n_slc, pos0=0)

    tab = _flash_tables(rel_bias)
    o_slc = _flash_attention(q, ns0, ns1, ks, kvs_t, tab, gn, use_sel=True, band=None, gate_base=NSA_HEADS)
    o_win = _flash_attention(q, ns0, ns1, kw, kvw_t, tab, gn, use_sel=False, band=WINDOW // ATT_TILE,
                             gate_base=2 * NSA_HEADS)

    m = mem.shape[1]
    mem_kv, mem_kv_t = _project(mem.reshape(b * m, D_MODEL), w["mem_g"], w["w_mem"], w["mem_gain"],
                                w["mem_nmask"], ((0, 2 * MEM_WIDTH, "id", ("rows", "t"), F32),), MXU_DIM,
                                tm=math.gcd(m, 512), seq=m)
    o_mem = _mem_attention(qm, mem_kv.reshape(b, m, 2 * MEM_WIDTH), tq=min(512, s))

    f2 = lambda a: a.reshape(n, a.shape[-1])
    x2, h2, eidx, gates, cnt = _layer_tail(x2d, f2(o_pool), f2(o_cmp), f2(o_slc), f2(o_win), f2(o_mem), cnt0, w, tm)
    win_t = kvw_t[:, :, max(0, s - WINDOW):]
    if s < WINDOW:
        win_t = jnp.pad(win_t, ((0, 0), (0, 0), (WINDOW - s, 0)))
    states = (_rows_view(kvc_t, NSA_KV_HEADS), _rows_view(kvs_t, NSA_KV_HEADS), _rows_view(win_t, NSA_KV_HEADS),
              _rows_view(mem_kv_t, MEM_HEADS), _last_rows(u, POOL_BUF))
    return (x2, h2, eidx, gates), states, cnt


def _last_rows(a, n):
    t = a.shape[1]
    if t < n:
        a = jnp.pad(a, [(0, 0), (n - t, 0)] + [(0, 0)] * (a.ndim - 2))
    return a[:, a.shape[1] - n:]


def _dec_columns_nsa(q):
    b, t, _ = q.shape
    qh = q.reshape(b, t, NSA_KV_HEADS, NSA_HPG, HEAD_DIM)
    w = jnp.einsum("btgpd,gx->bxdgtp", qh.astype(F32), jnp.eye(NSA_KV_HEADS, dtype=F32))
    w = w.reshape(b, NSA_KV_HEADS * HEAD_DIM, NSA_KV_HEADS * t * NSA_HPG)
    return jnp.pad(w, ((0, 0), (0, KV_WIDTH - w.shape[1]), (0, LANES - w.shape[2]))).astype(BF16)


def _dec_extract_nsa(o, t):
    b = o.shape[0]
    v = o[:, LANES:, :NSA_KV_HEADS * t * NSA_HPG]
    v = v.reshape(b, NSA_KV_HEADS, HEAD_DIM, NSA_KV_HEADS, t, NSA_HPG)
    v = jnp.einsum("bxdgtp,gx->btgpd", v, jnp.eye(NSA_KV_HEADS, dtype=F32))
    return v.reshape(b, t, NSA_WIDTH)


def _dec_bias_cols(bias_tph):
    k, t, _ = bias_tph.shape
    bt = bias_tph.reshape(k, t, NSA_KV_HEADS, NSA_HPG).transpose(0, 2, 1, 3).reshape(k, NSA_KV_HEADS * t * NSA_HPG)
    return jnp.pad(bt, ((0, 0), (0, LANES - bt.shape[1])))


def _dec_gate_cols(gn, base, t):
    b = gn.shape[0]
    gt = gn[:, :, base:base + NSA_HEADS].reshape(b, t, NSA_KV_HEADS, NSA_HPG).transpose(0, 2, 1, 3)
    gt = gt.reshape(b, 1, NSA_KV_HEADS * t * NSA_HPG)
    return jnp.pad(gt, ((0, 0), (0, 0), (0, LANES - gt.shape[2])), constant_values=1.0)


def _sample_pre(x, cache_cmp, cache_slc, cache_win, cache_mem, pool_buf, page_table, w, rel_bias, cnt0):
    b, t, _ = x.shape
    n = b * t
    page = cache_cmp.shape[1]
    n_pages = page_table.shape[1]
    past = n_pages * page
    x2d = x.reshape(n, D_MODEL)
    tm = n if n <= 512 else LANES
    q, qm, kvs, kvw, kvc, u, gn = _project_in(x2d, w, _PROJ_SEGS_SAMPLE, tm, tm)
    r3 = lambda a: a.reshape(b, t, a.shape[-1])
    q, qm, kvs, kvw, kvc, u, gn = map(r3, (q, qm, kvs, kvw, kvc, u, gn))
    qpos = past + jnp.arange(t)

    buf16 = jnp.pad(pool_buf, ((0, 0), (16 - POOL_BUF, 0), (0, 0)))
    o_pool = _pool_mix(u, buf16, w["w_pool"], w["pool_scale"], past)

    total = past + t
    n_cmp = (total - CMP_BLOCK) // CMP_STRIDE + 1
    n_sub_used = n_cmp + CMP_BLOCK // CMP_STRIDE - 1
    n_slc = -(-total // SLC_BLOCK)
    pps = math.gcd(n_pages, 16)
    parts = _cmp_partials_paged(_feature_major(cache_cmp), page_table, jnp.zeros_like(page_table), w["pe_c"],
                                w["w_c"], pps)
    extra = n_sub_used * CMP_STRIDE - past
    if extra > 0:
        tail_rows = -(-extra // CMP_STRIDE) * CMP_STRIDE
        new_c = jnp.pad(kvc, ((0, 0), (0, max(0, tail_rows - t)), (0, 0)))[:, :tail_rows]
        parts = jnp.concatenate([parts, _cmp_partials_dense(new_c, w["pe_c"], w["w_c"])], axis=1)
    n_sub = parts.shape[1]
    end = jnp.arange(n_sub)[None, :] * CMP_STRIDE + CMP_BLOCK - 1
    bias_c = _bias_of(rel_bias, qpos[:, None] - end, (end <= qpos[:, None]) & (jnp.arange(n_sub)[None, :] < n_cmp))
    qpad = ((0, 0), (0, LANES - t), (0, 0))
    bias_c = jnp.pad(bias_c.transpose(2, 0, 1), qpad)
    o_cmp, ns0, ns1 = _cmp_attention(jnp.pad(q, qpad), parts, bias_c, jnp.pad(gn, qpad), w["gk_cmp"], tq=LANES,
                                     n_cmp=n_cmp, n_slc=n_slc, pos0=past)
    o_cmp = o_cmp[:, :t]

    wq = _dec_columns_nsa(q).transpose(0, 2, 1)
    ncol = NSA_KV_HEADS * t * NSA_HPG
    new_tile = lambda kv: jnp.pad(kv, ((0, 0), (0, LANES - t), (0, 0))).transpose(0, 2, 1)
    own = lambda npg: jnp.broadcast_to(jnp.arange(b, dtype=jnp.int32)[:, None], (b, npg))
    blocks = lambda npg: jnp.broadcast_to(jnp.arange(npg, dtype=jnp.int32)[None, :], (b, npg))

    n_chunks = n_pages + 1
    nblk = -(-2 * n_chunks // 8) * 8
    ns = jnp.stack([ns0, ns1], axis=1)[:, :, :, :t].transpose(0, 1, 3, 2)
    ns = jnp.pad(ns, ((0, 0), (0, 0), (0, 0), (0, max(0, nblk - ns.shape[3]))))[..., :nblk]
    ns = jnp.broadcast_to(ns[:, :, :, None, :], (b, NSA_KV_HEADS, t, NSA_HPG, nblk)).reshape(b, ncol, nblk)
    notsel = jnp.pad(ns.transpose(0, 2, 1), ((0, 0), (0, 0), (0, LANES - ncol)))
    rows = jnp.arange(LANES)
    far = _bias_of(rel_bias, jnp.full((LANES, t), 2 * MAX_DISTANCE), jnp.ones((LANES, t), bool))
    kpos_last = past - LANES + rows
    d_last = qpos[None, :] - kpos_last[:, None]
    near = _bias_of(rel_bias, d_last, d_last >= 0)
    kpos_new = past + rows
    d_new = qpos[None, :] - kpos_new[:, None]
    newb = _bias_of(rel_bias, d_new, (d_new >= 0) & (rows[:, None] < t))
    bias_tab = jnp.stack([_dec_bias_cols(far), _dec_bias_cols(near), _dec_bias_cols(newb)])
    bias_idx = jnp.concatenate([jnp.zeros((n_pages - 1,), jnp.int32), jnp.array([1, 2], jnp.int32)])
    o_slc = _decode_attention(_feature_major(cache_slc), page_table, jnp.zeros_like(page_table), new_tile(kvs), wq,
                              bias_tab, bias_idx, notsel, _dec_gate_cols(gn, NSA_HEADS, t))
    o_slc = _dec_extract_nsa(o_slc, t)

    wb = cache_win.shape[1]
    n_wpg = wb // LANES
    kpos_w = past - wb + jnp.arange(wb + LANES)
    d_w = qpos[None, :] - kpos_w[:, None]
    valid_w = (d_w >= 0) & (d_w < WINDOW) & (kpos_w[:, None] >= 0) & (jnp.arange(wb + LANES)[:, None] < wb + t)
    bias_w = _dec_bias_cols(_bias_of(rel_bias, d_w, valid_w)).reshape(n_wpg + 1, LANES, LANES)
    zeros_ns = jnp.zeros((b, -(-2 * (n_wpg + 1) // 8) * 8, LANES), F32)
    o_win = _decode_attention(_feature_major(cache_win), own(n_wpg), blocks(n_wpg), new_tile(kvw), wq, bias_w,
                              jnp.arange(n_wpg + 1, dtype=jnp.int32), zeros_ns, _dec_gate_cols(gn, 2 * NSA_HEADS, t))
    o_win = _dec_extract_nsa(o_win, t)

    m = cache_mem.shape[1]
    n_mpg = m // LANES
    qmh = qm.reshape(b, t, MEM_HEADS, HEAD_DIM).astype(F32)
    wqm = jnp.einsum("bthd,hx->bxdht", qmh, jnp.eye(MEM_HEADS, dtype=F32))
    wqm = wqm.reshape(b, MEM_WIDTH, MEM_HEADS * t)
    wqm = jnp.pad(wqm, ((0, 0), (0, MEM_WIDTH), (0, LANES - MEM_HEADS * t))).astype(BF16).transpose(0, 2, 1)
    o_mem = _decode_attention(_feature_major(cache_mem), own(n_mpg), blocks(n_mpg), None, wqm,
                              jnp.zeros((1, LANES, LANES), F32), jnp.zeros((n_mpg,), jnp.int32),
                              jnp.zeros((b, 8, LANES), F32), jnp.ones((b, 1, LANES), F32))
    om = o_mem[:, MEM_WIDTH:, :MEM_HEADS * t].reshape(b, MEM_HEADS, HEAD_DIM, MEM_HEADS, t)
    o_mem = jnp.einsum("bxdht,hx->bthd", om, jnp.eye(MEM_HEADS, dtype=F32)).reshape(b, t, MEM_WIDTH)

    f2 = lambda a: a.reshape(n, a.shape[-1])
    x2, h2, eidx, gates, cnt = _layer_tail(x2d, f2(o_pool), f2(o_cmp), f2(o_slc), f2(o_win), f2(o_mem), cnt0, w,
                                           n if n <= 512 else LANES)
    kvshape = (b, t, 2, NSA_KV_HEADS, HEAD_DIM)
    new_win = _rows_view(jnp.concatenate([_feature_major(cache_win)[:, :, t:], kvw.transpose(0, 2, 1)], axis=2),
                         NSA_KV_HEADS)
    new_pool = jnp.concatenate([pool_buf, u], axis=1)[:, t:]
    states = (kvc.reshape(kvshape), kvs.reshape(kvshape), new_win, new_pool)
    return (x2, h2, eidx, gates), states, cnt


def kernel(x_prompt, x_sample, cache_cmp_kv, cache_slc_kv, cache_win_kv, cache_mem_kv, state_pool, page_table,
           mem_prompt, rel_bias, norm1_g, w_in, nsa_qk_norm, mem_qk_norm, cmp_w, cmp_pe, pool_w, pool_scale,
           mem_norm_g, w_mem_kv, w_up_pool, w_up_nsa, w_up_mem, w_out, norm2_g, router_w, router_b, w_gu, b_gu,
           w_down, b_down):
    depth = w_in.shape[0]
    yp, ys = x_prompt, x_sample
    bp, sp, _ = x_prompt.shape
    bs, ts, _ = x_sample.shape
    outs_p = [[] for _ in range(5)]
    outs_s = [[] for _ in range(4)]
    for l in range(depth):
        w = _prep_layer(l, rel_bias, norm1_g, w_in, nsa_qk_norm, mem_qk_norm, cmp_w, cmp_pe, pool_w, pool_scale,
                        mem_norm_g, w_mem_kv, w_up_pool, w_up_nsa, w_up_mem, w_out, norm2_g, router_w, router_b,
                        w_gu, b_gu, w_down, b_down)
        pre_p, st_p, cnt = _prompt_pre(yp, mem_prompt, w, rel_bias, jnp.zeros((1, LANES), F32))
        pre_s, st_s, cnt = _sample_pre(ys, cache_cmp_kv[l], cache_slc_kv[l], cache_win_kv[l], cache_mem_kv[l],
                                       state_pool[l], page_table, w, rel_bias, cnt)
        yp, ys = _moe([pre_p, pre_s], cnt, w)
        yp = yp.reshape(bp, sp, D_MODEL)
        ys = ys.reshape(bs, ts, D_MODEL)
        for lst, a in zip(outs_p, st_p):
            lst.append(a)
        for lst, a in zip(outs_s, st_s):
            lst.append(a)
    new_cmp_p, new_slc_p, new_win_p, new_mem_p, new_pool_p = [jnp.stack(a) for a in outs_p]
    new_cmp_s, new_slc_s, new_win_s, new_pool_s = [jnp.stack(a) for a in outs_s]
    new_win_s = new_win_s.reshape(new_win_s.shape[:3] + (2, NSA_KV_HEADS, HEAD_DIM))
    return (yp, ys, new_cmp_p, new_slc_p, new_win_p, new_mem_p, new_pool_p,
            new_cmp_s, new_slc_s, new_win_s, new_pool_s)
```

```python
import functools
import math

import jax
import jax.numpy as jnp
from jax import lax
from jax.experimental import pallas as pl
from jax.experimental.pallas import tpu as pltpu

F32 = jnp.float32
BF16 = jnp.bfloat16

D_MODEL = 1024
HEAD_DIM = 64
POOL_WINDOWS = (2, 4, 8, 16)
POOL_GROUP = 64
POOL_WIDTH = 256
POOL_BUF = 15
NSA_HEADS = 8
NSA_KV_HEADS = 2
NSA_HPG = 4
NSA_WIDTH = 512
KV_WIDTH = 256
CMP_BLOCK = 32
CMP_STRIDE = 16
SLC_BLOCK = 64
SLC_TOPK = 16
WINDOW = 512
MEM_HEADS = 4
MEM_WIDTH = 256
NUM_BUCKETS = 32
MAX_DISTANCE = 128
N_EXPERTS = 32
TOP_K = 4
D_FF = 1024
SWIGLU_ALPHA = 1.702
SWIGLU_LIMIT = 7.0
EPS = 1e-6
SCALE = HEAD_DIM ** -0.5

LANES = 128
MXU_DIM = 256
NEG = -1e30
ATT_TILE = 256
DEC_PAGES_PER_STEP = 16
MOE_TILE = 512
MOE_DMA_TOKENS = 512
VMEM_LIMIT = 48 * 1024 * 1024
FFN_VMEM_LIMIT = 56 * 1024 * 1024


def _cparams(*sem):
    return pltpu.CompilerParams(dimension_semantics=sem, vmem_limit_bytes=VMEM_LIMIT)


def _dot(a, b):
    return jnp.dot(a, b, preferred_element_type=F32)


def _dot_nt(a, b):
    return lax.dot_general(a, b, (((1,), (1,)), ((), ())), preferred_element_type=F32)


def _split_dot(a, b):
    hi = a.astype(BF16)
    lo = (a - hi.astype(F32)).astype(BF16)
    return _dot(hi, b) + _dot(lo, b)


def _rms(x, g):
    r = lax.rsqrt(jnp.mean(x * x, axis=-1, keepdims=True) + EPS)
    return (x * r) * g


def _sigmoid(x):
    return 1.0 / (1.0 + jnp.exp(-x))


def _pack_bf16_pairs(x):
    w = x.shape[1] // 2
    bits = lax.bitcast_convert_type(x.astype(BF16).astype(F32), jnp.uint32)
    return jnp.right_shift(bits[:, :w], jnp.uint32(16)) | (bits[:, w:] & jnp.uint32(0xFFFF0000))


def _unpack_bf16_pairs(p):
    lo = lax.bitcast_convert_type(jnp.left_shift(p, jnp.uint32(16)), F32)
    hi = lax.bitcast_convert_type(p & jnp.uint32(0xFFFF0000), F32)
    return jnp.concatenate([lo, hi], axis=1)


def _lane_iota(shape):
    return lax.broadcasted_iota(jnp.int32, shape, len(shape) - 1)


def _row_iota(shape):
    return lax.broadcasted_iota(jnp.int32, shape, len(shape) - 2)


def _proj_kernel(x_ref, g_ref, w_ref, gain_ref, nmask_ref, seg_ref, *rest, segs, n_norm, has_wt):
    wt_ref = rest[0] if has_wt else None
    out_refs = rest[1:] if has_wt else rest
    h = _rms(x_ref[...], g_ref[...]).astype(BF16)
    seg = seg_ref[...]
    outs = iter(out_refs)
    for (start, width, kind, forms, _) in segs:
        if forms == ("t",) and has_wt:
            next(outs)[...] = _dot_nt(wt_ref[...], h)
            continue
        z = _dot(h, w_ref[:, start:start + width])
        if start < n_norm:
            pieces = []
            for c in range(0, width, MXU_DIM):
                zc = z[:, c:c + MXU_DIM]
                ms = _split_dot(zc * zc, seg)
                zn = (zc * lax.rsqrt(ms + EPS)) * gain_ref[:, start + c:start + c + MXU_DIM]
                pieces.append(jnp.where(nmask_ref[:, start + c:start + c + MXU_DIM] > 0, zn, zc))
            z = pieces[0] if len(pieces) == 1 else jnp.concatenate(pieces, axis=1)
        if kind == "sigmoid":
            z = _sigmoid(z)
        elif kind == "qscale":
            z = z * SCALE
        for form in forms:
            o_ref = next(outs)
            if form == "rows":
                o_ref[...] = z.astype(o_ref.dtype)
            elif form == "key_rows":
                o_ref[...] = z[:, 0:LANES]
            else:
                o_ref[...] = z.T


def _project(x, g, w, gain, nmask, segs, n_norm, tm, seq, wt=None):
    n = x.shape[0]
    ncol = w.shape[1]
    seg = _seg_matrix(MXU_DIM)
    full = lambda i: (0, 0)
    tpb = seq // tm
    out_specs, out_shape = [], []
    for (_, wd, _, forms, dt) in segs:
        for form in forms:
            if form == "rows":
                out_specs.append(pl.BlockSpec((tm, wd), lambda i: (i, 0)))
                out_shape.append(jax.ShapeDtypeStruct((n, wd), dt))
            elif form == "key_rows":
                out_specs.append(pl.BlockSpec((tm, LANES), lambda i: (i, 0)))
                out_shape.append(jax.ShapeDtypeStruct((n, LANES), F32))
            else:
                out_specs.append(pl.BlockSpec((None, wd, tm), lambda i: (i // tpb, 0, i % tpb)))
                out_shape.append(jax.ShapeDtypeStruct((n // seq, wd, seq), F32))
    extra = () if wt is None else (wt,)
    return pl.pallas_call(
        functools.partial(_proj_kernel, segs=segs, n_norm=n_norm, has_wt=wt is not None),
        grid=(n // tm,),
        in_specs=[
            pl.BlockSpec((tm, D_MODEL), lambda i: (i, 0)),
            pl.BlockSpec((1, D_MODEL), full),
            pl.BlockSpec((D_MODEL, ncol), full),
            pl.BlockSpec((1, gain.shape[1]), full),
            pl.BlockSpec((1, nmask.shape[1]), full),
            pl.BlockSpec((MXU_DIM, MXU_DIM), full),
        ] + [pl.BlockSpec(a.shape, full) for a in extra],
        out_specs=out_specs,
        out_shape=out_shape,
        compiler_params=_cparams("parallel"),
        name="proj",
    )(x, g, w, gain, nmask, seg, *extra)


def _seg_matrix(n):
    i = jnp.arange(n) // HEAD_DIM
    return jnp.where(i[:, None] == i[None, :], 1.0 / HEAD_DIM, 0.0).astype(BF16)


def _pool_kernel(u_ref, buf_ref, w_ref, scale_ref, o_ref, zs_ref, *, t, pos0):
    zs_ref[0:16, :] = buf_ref[...]
    zs_ref[16:16 + t, :] = u_ref[...]
    u = u_ref[...]
    lane = _lane_iota((1, POOL_WIDTH))
    pos = (pos0 + _row_iota((t, 1))).astype(F32)
    acc = u
    mean = None
    for i in range(1, max(POOL_WINDOWS)):
        acc = acc + zs_ref[16 - i:16 - i + t, :]
        if i + 1 in POOL_WINDOWS:
            gi = POOL_WINDOWS.index(i + 1)
            m = acc / jnp.minimum(pos + 1.0, float(i + 1))
            mean = m if mean is None else jnp.where(lane >= gi * POOL_GROUP, m, mean)
    d = (mean - u).astype(BF16)
    o_ref[...] = _dot(d, w_ref[...]) * scale_ref[...]


def _pool_mix(u, buf16, w_bd, scale, pos0):
    b, t, _ = u.shape
    return pl.pallas_call(
        functools.partial(_pool_kernel, t=t, pos0=pos0),
        grid=(b,),
        in_specs=[
            pl.BlockSpec((None, t, POOL_WIDTH), lambda i: (i, 0, 0)),
            pl.BlockSpec((None, 16, POOL_WIDTH), lambda i: (i, 0, 0)),
            pl.BlockSpec((POOL_WIDTH, POOL_WIDTH), lambda i: (0, 0)),
            pl.BlockSpec((1, POOL_WIDTH), lambda i: (0, 0)),
        ],
        out_specs=pl.BlockSpec((None, t, POOL_WIDTH), lambda i: (i, 0, 0)),
        out_shape=jax.ShapeDtypeStruct((b, t, POOL_WIDTH), F32),
        scratch_shapes=[pltpu.VMEM((t + 16, POOL_WIDTH), F32)],
        compiler_params=_cparams("parallel"),
        name="pool",
    )(u, buf16, w_bd, scale)


def _cpart_compute(x_refs, pe_ref, w_ref, o_ref, m):
    for c in range(2):
        acc = jnp.zeros((m + 8, KV_WIDTH), F32)
        for j in range(CMP_STRIDE):
            lhs = jnp.concatenate([x_refs[c][pl.ds(j, m, stride=CMP_STRIDE), :], pe_ref[c, j]], axis=0)
            acc = acc + _dot(lhs.astype(BF16), w_ref[c, j])
        lane = _lane_iota((1, KV_WIDTH))
        pe_term = jnp.where(lane < LANES, acc[m:m + 1], acc[m + 1:m + 2])
        o_ref[:, c * KV_WIDTH:(c + 1) * KV_WIDTH] = acc[0:m] + pe_term


def _cpart_kernel(xk_ref, xv_ref, pe_ref, w_ref, o_ref, *, rows):
    _cpart_compute((xk_ref, xv_ref), pe_ref, w_ref, o_ref, rows // CMP_STRIDE)


def _cpart_paged_kernel(*refs, nop, page):
    page_refs = refs[2:2 + nop]
    pe_ref, w_ref, o_ref, xk_ref, xv_ref = refs[2 + nop:]
    for k, r in enumerate(page_refs):
        xk_ref[k * page:(k + 1) * page, :] = r[0:LANES, :].T
        xv_ref[k * page:(k + 1) * page, :] = r[LANES:2 * LANES, :].T
    _cpart_compute((xk_ref, xv_ref), pe_ref, w_ref, o_ref, nop * page // CMP_STRIDE)


def _cmp_partials_dense(kv, pe, w_c):
    b, t, _ = kv.shape
    rows = (t // CMP_STRIDE) * CMP_STRIDE
    n = rows // CMP_STRIDE
    return pl.pallas_call(
        functools.partial(_cpart_kernel, rows=rows),
        grid=(b,),
        in_specs=[
            pl.BlockSpec((None, rows, LANES), lambda i: (i, 0, 0)),
            pl.BlockSpec((None, rows, LANES), lambda i: (i, 0, 1)),
            pl.BlockSpec(pe.shape, lambda i: (0, 0, 0, 0)),
            pl.BlockSpec(w_c.shape, lambda i: (0, 0, 0, 0)),
        ],
        out_specs=pl.BlockSpec((None, n, 2 * KV_WIDTH), lambda i: (i, 0, 0)),
        out_shape=jax.ShapeDtypeStruct((b, n, 2 * KV_WIDTH), F32),
        compiler_params=_cparams("parallel"),
        name="cmp_partials",
    )(kv, kv, pe, w_c)


def _cmp_partials_paged(pool_t, phys, lblk, pe, w_c, pages_per_step):
    b, n_pages = phys.shape
    page = LANES
    nop = pages_per_step
    n = nop * page // CMP_STRIDE

    def page_spec(k):
        return pl.BlockSpec((None, KV_WIDTH, page),
                            lambda i, c, ph, lb: (ph[i, c * nop + k], 0, lb[i, c * nop + k]))

    grid_spec = pltpu.PrefetchScalarGridSpec(
        num_scalar_prefetch=2,
        grid=(b, n_pages // nop),
        in_specs=[page_spec(k) for k in range(nop)] + [
            pl.BlockSpec(pe.shape, lambda i, c, ph, lb: (0, 0, 0, 0)),
            pl.BlockSpec(w_c.shape, lambda i, c, ph, lb: (0, 0, 0, 0)),
        ],
        out_specs=pl.BlockSpec((None, n, 2 * KV_WIDTH), lambda i, c, ph, lb: (i, c, 0)),
        scratch_shapes=[pltpu.VMEM((nop * page, LANES), F32), pltpu.VMEM((nop * page, LANES), F32)],
    )
    return pl.pallas_call(
        functools.partial(_cpart_paged_kernel, nop=nop, page=page),
        grid_spec=grid_spec,
        out_shape=jax.ShapeDtypeStruct((b, n_pages * page // CMP_STRIDE, 2 * KV_WIDTH), F32),
        compiler_params=_cparams("parallel", "arbitrary"),
        name="cmp_partials_paged",
    )(phys, lblk, *([pool_t] * nop), pe, w_c)


def _group_query_columns(q_ref, g, t):
    zeros64 = jnp.zeros((HEAD_DIM, t), BF16)
    cols = []
    for pr in range(2):
        qt = q_ref[:, (2 * g + pr) * LANES:(2 * g + pr + 1) * LANES].astype(F32).T.astype(BF16)
        for half in range(2):
            qh = qt[half * HEAD_DIM:(half + 1) * HEAD_DIM]
            cols.append(jnp.concatenate([qh, zeros64] if g == 0 else [zeros64, qh], axis=0))
    return jnp.concatenate(cols, axis=1)


def _store_group_output(o_ref, out_t, g, t):
    for pr in range(2):
        pair = jnp.concatenate([out_t[:, (2 * pr) * t:(2 * pr + 1) * t], out_t[:, (2 * pr + 1) * t:(2 * pr + 2) * t]],
                               axis=0)
        o_ref[:, (2 * g + pr) * LANES:(2 * g + pr + 1) * LANES] = pair.T


def _cattn_kernel(q_ref, p_ref, bias_ref, gn_ref, gk_ref, seg_ref, o_ref, ns0_ref, ns1_ref, *,
                  tq, n_sub, n_cmp, n_slc, nslp, pos0):
    qi = pl.program_id(0)
    pall = p_ref[...]
    kraw = pall[:, 0:LANES] + pltpu.roll(pall[:, LANES:2 * LANES], n_sub - 1, 0)
    vc = pall[:, 2 * LANES:3 * LANES] + pltpu.roll(pall[:, 3 * LANES:4 * LANES], n_sub - 1, 0)
    ms = _split_dot(kraw * kraw, seg_ref[...])
    kc = ((kraw * lax.rsqrt(ms + EPS)) * gk_ref[...]).astype(BF16)
    vct = vc.T.astype(BF16)

    nsel = -(-n_slc // 8) * 8
    jj = _row_iota((nsel, n_sub))
    nn = _lane_iota((nsel, n_sub))
    covers_t = ((nn * CMP_STRIDE < (jj + 1) * SLC_BLOCK) & (nn * CMP_STRIDE + CMP_BLOCK - 1 >= jj * SLC_BLOCK)
                & (nn < n_cmp) & (jj < n_slc))
    covers_t = jnp.where(covers_t, 1.0, 0.0).astype(BF16)
    qpos = pos0 + qi * tq + _lane_iota((1, tq))
    qblk = jnp.right_shift(qpos, SLC_BLOCK.bit_length() - 1)
    jr = _row_iota((nsel, tq))
    jrf = jr.astype(F32)
    forced = (jr == 0) | (jr == qblk) | (jr == qblk - 1)
    causal = jr <= qblk
    gnt = gn_ref[...].T

    for g in range(NSA_KV_HEADS):
        bias = bias_ref[g]
        s = _dot(kc, _group_query_columns(q_ref, g, tq)) + bias
        m = jnp.max(s, axis=0, keepdims=True)
        m = jnp.where(m > 0.5 * NEG, m, 0.0)
        e = jnp.where(bias > 0.5 * NEG, jnp.exp(s - m), 0.0)
        p = e / jnp.maximum(jnp.sum(e, axis=0, keepdims=True), 1e-30)
        h0 = NSA_HPG * g
        gate = jnp.concatenate([gnt[h0 + c:h0 + c + 1] for c in range(NSA_HPG)], axis=1)
        out_t = _dot(vct[g * HEAD_DIM:(g + 1) * HEAD_DIM], p.astype(BF16)) * gate
        _store_group_output(o_ref, out_t, g, tq)

        prsum = p[:, 0:tq] + p[:, tq:2 * tq] + p[:, 2 * tq:3 * tq] + p[:, 3 * tq:4 * tq]
        hi = prsum.astype(BF16)
        lo = (prsum - hi.astype(F32)).astype(BF16)
        imp = _dot(covers_t, hi) + _dot(covers_t, lo)
        score = jnp.where(forced, jnp.inf, imp)
        score = jnp.where(causal, score, -jnp.inf)
        sel = jnp.zeros((nsel, tq), F32)
        for _ in range(min(SLC_TOPK, n_slc)):
            m = jnp.max(score, axis=0, keepdims=True)
            idx = jnp.min(jnp.where(score == m, jrf, 1e9), axis=0, keepdims=True)
            pick = jrf == idx
            sel = jnp.where(pick & (m > -jnp.inf), 1.0, sel)
            score = jnp.where(pick, -jnp.inf, score)
        ns = 1.0 - sel
        if nslp > nsel:
            ns = jnp.concatenate([ns, jnp.ones((nslp - nsel, tq), F32)], axis=0)
        (ns0_ref if g == 0 else ns1_ref)[...] = ns


def _cmp_attention(q, parts, bias, gn, gk, *, tq, n_cmp, n_slc, pos0):
    b, s, _ = q.shape
    n_sub = parts.shape[1]
    nslp = -(-n_slc // LANES) * LANES
    bias_t = bias.reshape(NSA_KV_HEADS, NSA_HPG, s // tq, tq, n_sub).transpose(0, 4, 2, 1, 3)
    bias_t = bias_t.reshape(NSA_KV_HEADS, n_sub, NSA_HPG * s)
    kern = functools.partial(_cattn_kernel, tq=tq, n_sub=n_sub, n_cmp=n_cmp, n_slc=n_slc, nslp=nslp, pos0=pos0)
    return pl.pallas_call(
        kern,
        grid=(s // tq, b),
        in_specs=[
            pl.BlockSpec((None, tq, NSA_WIDTH), lambda i, j: (j, i, 0)),
            pl.BlockSpec((None, n_sub, 2 * KV_WIDTH), lambda i, j: (j, 0, 0)),
            pl.BlockSpec((NSA_KV_HEADS, n_sub, NSA_HPG * tq), lambda i, j: (0, 0, i)),
            pl.BlockSpec((None, tq, LANES), lambda i, j: (j, i, 0)),
            pl.BlockSpec((1, LANES), lambda i, j: (0, 0)),
            pl.BlockSpec((LANES, LANES), lambda i, j: (0, 0)),
        ],
        out_specs=[
            pl.BlockSpec((None, tq, NSA_WIDTH), lambda i, j: (j, i, 0)),
            pl.BlockSpec((None, nslp, tq), lambda i, j: (j, 0, i)),
            pl.BlockSpec((None, nslp, tq), lambda i, j: (j, 0, i)),
        ],
        out_shape=[
            jax.ShapeDtypeStruct((b, s, NSA_WIDTH), F32),
            jax.ShapeDtypeStruct((b, nslp, s), F32),
            jax.ShapeDtypeStruct((b, nslp, s), F32),
        ],
        compiler_params=_cparams("parallel", "parallel"),
        name="cmp_attention",
    )(q, parts, bias_t, gn, gk, _seg_matrix(LANES))


def _flash_kernel(q_ref, ns0_ref, ns1_ref, kv_ref, vt_ref, tab_ref, gn_ref, o_ref, *, t, use_sel, band, gate_base):
    qi = pl.program_id(1)
    row_k = _row_iota((t, LANES))
    lane_k = _lane_iota((t, LANES))
    lo_tile = jnp.maximum(qi - band, 0) if band is not None else 0
    gnt = gn_ref[...].T

    qts = []
    for g in range(NSA_KV_HEADS):
        qt_g = _group_query_columns(q_ref, g, t)
        if use_sel:
            nst = (ns0_ref if g == 0 else ns1_ref)[...].astype(BF16)
            qt_g = jnp.concatenate([qt_g, jnp.concatenate([nst] * NSA_HPG, axis=1)], axis=0)
        qts.append(qt_g)

    def body(kj, carry):
        k0 = pl.multiple_of(kj * t, t)
        kk = kv_ref[pl.ds(k0, t), :].astype(BF16)
        if use_sel:
            blk = kj * (t // SLC_BLOCK) + jnp.right_shift(row_k, SLC_BLOCK.bit_length() - 1)
            onehot = jnp.where(lane_k == blk, -(2.0 ** 30), 0.0).astype(BF16)
            kk = jnp.concatenate([kk, onehot], axis=1)
        delta = qi - kj
        if band is None:
            kind = jnp.minimum(delta, 2)
        else:
            kind = jnp.where(delta < 2, delta, jnp.where(delta < band, 2, 3))
        m_old, l_old, acc = carry
        s = _dot(kk, qt_all) + tab_ref[kind]
        m_new = jnp.maximum(m_old, jnp.max(s, axis=0, keepdims=True))
        alpha = jnp.exp(m_old - m_new)
        p = jnp.exp(s - m_new)
        l_new = alpha * l_old + jnp.sum(p, axis=0, keepdims=True)
        pb = p.astype(BF16)
        pv = jnp.concatenate(
            [_dot(vt_ref[g * HEAD_DIM:(g + 1) * HEAD_DIM, pl.ds(k0, t)].astype(BF16), pb[:, g * gw:(g + 1) * gw])
             for g in range(NSA_KV_HEADS)], axis=1)
        return m_new, l_new, acc * alpha + pv

    gw = NSA_HPG * t
    qt_all = jnp.concatenate(qts, axis=1)
    init = (jnp.full((1, NSA_HEADS * t), NEG, F32), jnp.zeros((1, NSA_HEADS * t), F32),
            jnp.zeros((HEAD_DIM, NSA_HEADS * t), F32))
    _, l_fin, acc = lax.fori_loop(lo_tile, qi + 1, body, init)
    gate = jnp.concatenate([gnt[gate_base + h:gate_base + h + 1] for h in range(NSA_HEADS)], axis=1)
    out = acc * (gate / l_fin)
    for g in range(NSA_KV_HEADS):
        _store_group_output(o_ref, out[:, g * gw:(g + 1) * gw], g, t)


def _flash_attention(q, ns0, ns1, k_rows, kv_t, tab, gn, *, use_sel, band, gate_base):
    b, s, _ = q.shape
    t = ATT_TILE
    assert ns0.shape[1] == LANES
    kern = functools.partial(_flash_kernel, t=t, use_sel=use_sel, band=band, gate_base=gate_base)
    tile = lambda w: pl.BlockSpec((None, t, w), lambda i, j: (i, j, 0))
    ns_tile = pl.BlockSpec((None, LANES, t), lambda i, j: (i, 0, j))
    return pl.pallas_call(
        kern,
        grid=(b, s // t),
        in_specs=[
            tile(NSA_WIDTH), ns_tile, ns_tile,
            pl.BlockSpec((None, s, LANES), lambda i, j: (i, 0, 0)),
            pl.BlockSpec((None, LANES, s), lambda i, j: (i, 1, 0)),
            pl.BlockSpec(tab.shape, lambda i, j: (0, 0, 0)),
            tile(LANES),
        ],
        out_specs=tile(NSA_WIDTH),
        out_shape=jax.ShapeDtypeStruct((b, s, NSA_WIDTH), F32),
        compiler_params=_cparams("parallel", "parallel"),
        name="flash_sel" if use_sel else "flash_win",
    )(q, ns0, ns1, k_rows, kv_t, tab, gn)


def _memattn_kernel(q_ref, kv_ref, o_ref):
    lane = _lane_iota((kv_ref.shape[0], LANES))
    for pr in range(MEM_HEADS // 2):
        qpair = q_ref[:, pr * LANES:(pr + 1) * LANES]
        kblk = kv_ref[:, pr * LANES:(pr + 1) * LANES]
        vblk = kv_ref[:, MEM_WIDTH + pr * LANES:MEM_WIDTH + (pr + 1) * LANES]
        out = None
        for half in range(2):
            keep = (lane < HEAD_DIM) if half == 0 else (lane >= HEAD_DIM)
            kk = jnp.where(keep, kblk, 0.0).astype(BF16)
            vv = jnp.where(keep, vblk, 0.0).astype(BF16)
            s = _dot_nt(qpair, kk)
            m = jnp.max(s, axis=1, keepdims=True)
            e = jnp.exp(s - m)
            p = e / jnp.sum(e, axis=1, keepdims=True)
            o = _dot(p.astype(BF16), vv)
            out = o if out is None else out + o
        o_ref[:, pr * LANES:(pr + 1) * LANES] = out


def _mem_attention(qm, mem_kv, tq):
    b, s, _ = qm.shape
    m = mem_kv.shape[1]
    return pl.pallas_call(
        _memattn_kernel,
        grid=(b, s // tq),
        in_specs=[
            pl.BlockSpec((None, tq, MEM_WIDTH), lambda i, j: (i, j, 0)),
            pl.BlockSpec((None, m, 2 * MEM_WIDTH), lambda i, j: (i, 0, 0)),
        ],
        out_specs=pl.BlockSpec((None, tq, MEM_WIDTH), lambda i, j: (i, j, 0)),
        out_shape=jax.ShapeDtypeStruct((b, s, MEM_WIDTH), F32),
        compiler_params=_cparams("parallel", "parallel"),
        name="mem_attention",
    )(qm, mem_kv)


def _dec_kernel(*refs, n_pg, pps, has_new):
    bidx_ref = refs[2]
    page_refs = refs[3:3 + pps]
    new_ref, wq_ref, bias_ref, ns_ref, gate_ref, o_ref, acc_ref, m_ref, l_ref = refs[3 + pps:]
    c = pl.program_id(1)
    n_chunks = pl.num_programs(1)

    @pl.when(c == 0)
    def _():
        acc_ref[...] = jnp.zeros_like(acc_ref)
        m_ref[...] = jnp.full_like(m_ref, NEG)
        l_ref[...] = jnp.zeros_like(l_ref)

    rk = _row_iota((LANES, LANES))

    def step(tiles, first_page):
        feats = (tiles[0] if len(tiles) == 1 else jnp.concatenate(tiles, axis=1)).astype(BF16)
        s = lax.dot_general(feats, wq_ref[...], (((0,), (1,)), ((), ())),
                            preferred_element_type=F32)
        extra = []
        for k in range(len(tiles)):
            pg = first_page + k
            ns = jnp.where(rk < SLC_BLOCK, ns_ref[pl.ds(2 * pg, 1), :], ns_ref[pl.ds(2 * pg + 1, 1), :])
            extra.append(jnp.where(ns > 0.5, NEG, bias_ref[bidx_ref[pg]]))
        s = s + (extra[0] if len(extra) == 1 else jnp.concatenate(extra, axis=0))
        m_old = m_ref[...]
        m_new = jnp.maximum(m_old, jnp.max(s, axis=0, keepdims=True))
        alpha = jnp.exp(m_old - m_new)
        p = jnp.exp(s - m_new)
        l_ref[...] = alpha * l_ref[...] + jnp.sum(p, axis=0, keepdims=True)
        m_ref[...] = m_new
        acc_ref[...] = acc_ref[...] * alpha + _dot(feats, p.astype(BF16))

    if has_new:
        @pl.when(c < n_chunks - 1)
        def _():
            step([r[...] for r in page_refs], c * pps)

        @pl.when(c == n_chunks - 1)
        def _():
            step([r[...] for r in page_refs] + [new_ref[...]], n_pg - pps)
    else:
        step([r[...] for r in page_refs], c * pps)

    @pl.when(c == n_chunks - 1)
    def _():
        o_ref[...] = acc_ref[...] / l_ref[...] * gate_ref[...]


def _decode_attention(pages, phys, lblk, new_rows, wq, bias_tab, bias_idx, notsel, gate):
    bsz, n_pg = phys.shape
    w = pages.shape[1]
    has_new = new_rows is not None
    pps = math.gcd(n_pg, DEC_PAGES_PER_STEP)
    n_steps = n_pg // pps
    n_chunks = n_steps
    if not has_new:
        new_rows = jnp.zeros((1, w, LANES), F32)
    new_map = (lambda i, c, ph, lb, bi: (i, 0, 0)) if has_new else (lambda i, c, ph, lb, bi: (0, 0, 0))

    def page_spec(k):
        def index(i, c, ph, lb, bi):
            return (ph[i, c * pps + k], 0, lb[i, c * pps + k])
        return pl.BlockSpec((None, w, LANES), index)

    per_b = lambda i, c, ph, lb, bi: (i, 0, 0)
    grid_spec = pltpu.PrefetchScalarGridSpec(
        num_scalar_prefetch=3,
        grid=(bsz, n_chunks),
        in_specs=[page_spec(k) for k in range(pps)] + [
            pl.BlockSpec((None, w, LANES), new_map),
            pl.BlockSpec((None, LANES, w), per_b),
            pl.BlockSpec(bias_tab.shape, lambda i, c, ph, lb, bi: (0, 0, 0)),
            pl.BlockSpec((None, notsel.shape[1], LANES), per_b),
            pl.BlockSpec((None, 1, LANES), per_b),
        ],
        out_specs=pl.BlockSpec((None, w, LANES), per_b),
        scratch_shapes=[pltpu.VMEM((w, LANES), F32), pltpu.VMEM((1, LANES), F32), pltpu.VMEM((1, LANES), F32)],
    )
    return pl.pallas_call(
        functools.partial(_dec_kernel, n_pg=n_pg, pps=pps, has_new=has_new),
        grid_spec=grid_spec,
        out_shape=jax.ShapeDtypeStruct((bsz, w, LANES), F32),
        compiler_params=_cparams("parallel", "arbitrary"),
        name="decode_attention",
    )(phys, lblk, bias_idx, *([pages] * pps), new_rows, wq, bias_tab, notsel, gate)


def _feature_major(cache):
    n, rows = cache.shape[:2]
    return cache.transpose(0, 2, 3, 4, 1).reshape(n, -1, rows)


def _tail_kernel(x_ref, g1_ref, op_ref, oc_ref, os_ref, ow_ref, om_ref, cnt0_ref, wgb_ref, wup_p_ref, wup_n_ref,
                 wup_m_ref, wout_ref, g2_ref, rwh_ref, rwl_ref, rb_ref, x2_ref, h2_ref, ei_ref, gt_ref, cnt_ref):
    @pl.when(pl.program_id(0) == 0)
    def _():
        cnt_ref[...] = cnt0_ref[...]

    x = x_ref[...]
    h = _rms(x, g1_ref[...]).astype(BF16)
    onsa = (oc_ref[...] + os_ref[...] + ow_ref[...]).astype(BF16)
    ups = (_dot(op_ref[...].astype(BF16), wup_p_ref[...]), _dot(onsa, wup_n_ref[...]),
           _dot(om_ref[...].astype(BF16), wup_m_ref[...]))
    mixed = None
    for br in range(3):
        gb = _sigmoid(_dot(h, wgb_ref[:, br * D_MODEL:(br + 1) * D_MODEL]))
        mixed = gb * ups[br] if mixed is None else mixed + gb * ups[br]
    x2 = x + _dot(mixed.astype(BF16), wout_ref[...])
    x2_ref[...] = x2
    h2 = _rms(x2, g2_ref[...])
    h2_ref[...] = _pack_bf16_pairs(h2)
    hi = h2.astype(BF16)
    lo = (h2 - hi.astype(F32)).astype(BF16)
    logits = _dot(hi, rwh_ref[...]) + _dot(lo, rwh_ref[...]) + _dot(hi, rwl_ref[...]) + rb_ref[...]
    lane = _lane_iota(logits.shape)
    lanef = lane.astype(F32)
    tops, idxs = [], []
    for _ in range(TOP_K):
        m = jnp.max(logits, axis=1, keepdims=True)
        idx = jnp.min(jnp.where(logits == m, lanef, 1e9), axis=1, keepdims=True)
        logits = jnp.where(lanef == idx, -jnp.inf, logits)
        tops.append(m)
        idxs.append(idx)
    es = [jnp.exp(tk - tops[0]) for tk in tops]
    den = es[0] + es[1] + es[2] + es[3]
    tm = logits.shape[0]
    onehot = jnp.zeros(logits.shape, F32)
    for k in range(TOP_K):
        onehot = jnp.where(lanef == idxs[k], 1.0, onehot)
    tri = jnp.where(_row_iota((tm, tm)) > _lane_iota((tm, tm)), 1.0, 0.0).astype(BF16)
    before = _dot(tri, onehot.astype(BF16)) + cnt_ref[...]
    cnt_ref[...] = cnt_ref[...] + jnp.sum(onehot, axis=0, keepdims=True)
    ei = jnp.zeros(logits.shape, F32)
    gt = jnp.zeros(logits.shape, F32)
    for k in range(TOP_K):
        rank = jnp.sum(jnp.where(lanef == idxs[k], before, 0.0), axis=1, keepdims=True)
        ei = jnp.where(lane == k, idxs[k], ei)
        ei = jnp.where(lane == TOP_K + k, rank, ei)
        gt = jnp.where(lane == k, es[k] / den, gt)
    ei_ref[...] = ei.astype(jnp.int32)
    gt_ref[...] = gt


def _layer_tail(x, o_pool, o_cmp, o_slc, o_win, o_mem, cnt0, w, tm):
    n = x.shape[0]
    row = lambda wd: pl.BlockSpec((tm, wd), lambda i: (i, 0))
    full = lambda a: pl.BlockSpec(a.shape, lambda i: (0,) * a.ndim)
    weights = (w["wgb"], w["wup_pool"], w["wup_nsa"], w["wup_mem"], w["wout"], w["g2"], w["rw_hi"], w["rw_lo"],
               w["rb"])
    return pl.pallas_call(
        _tail_kernel,
        grid=(n // tm,),
        in_specs=[row(D_MODEL), full(w["g1"]), row(POOL_WIDTH), row(NSA_WIDTH), row(NSA_WIDTH), row(NSA_WIDTH),
                  row(MEM_WIDTH), full(cnt0)] + [full(a) for a in weights],
        out_specs=[row(D_MODEL), row(D_MODEL // 2), row(LANES), row(LANES), full(cnt0)],
        out_shape=[jax.ShapeDtypeStruct((n, D_MODEL), F32), jax.ShapeDtypeStruct((n, D_MODEL // 2), jnp.uint32),
                   jax.ShapeDtypeStruct((n, LANES), jnp.int32), jax.ShapeDtypeStruct((n, LANES), F32),
                   jax.ShapeDtypeStruct((1, LANES), F32)],
        compiler_params=_cparams("arbitrary"),
        name="layer_tail",
    )(x, w["g1"], o_pool, o_cmp, o_slc, o_win, o_mem, cnt0, *weights)


def _ffn_kernel(be_ref, nu_ref, x_ref, wgu_ref, bgu_ref, wd_ref, bd_ref, o_ref, wgu_bf, wd_bf):
    i = pl.program_id(0)

    @pl.when((i == 0) | (be_ref[i] != be_ref[jnp.maximum(i - 1, 0)]))
    def _():
        wgu_bf[...] = wgu_ref[...].astype(BF16)
        wd_bf[...] = wd_ref[...].astype(BF16)

    @pl.when(i < nu_ref[0])
    def _():
        gu = _dot(_unpack_bf16_pairs(x_ref[...]).astype(BF16), wgu_bf[...]) + bgu_ref[...]
        gate = jnp.minimum(gu[:, :D_FF], SWIGLU_LIMIT)
        up = jnp.clip(gu[:, D_FF:], -SWIGLU_LIMIT, SWIGLU_LIMIT)
        act = gate * _sigmoid(SWIGLU_ALPHA * gate) * (up + 1.0)
        o_ref[...] = _pack_bf16_pairs(_dot(act.astype(BF16), wd_bf[...]) + bd_ref[...])

    @pl.when(i >= nu_ref[0])
    def _():
        o_ref[...] = jnp.zeros_like(o_ref)


def _expert_ffn(rows, blk_e, n_used, wgu, bgu, wd, bd):
    n_rows = rows.shape[0]
    n_blocks = n_rows // MOE_TILE
    blk = lambda i, be, nu: (jnp.minimum(i, nu[0] - 1), 0)
    grid_spec = pltpu.PrefetchScalarGridSpec(
        num_scalar_prefetch=2,
        grid=(n_blocks,),
        in_specs=[
            pl.BlockSpec((MOE_TILE, D_MODEL // 2), blk),
            pl.BlockSpec((None, D_MODEL, 2 * D_FF), lambda i, be, nu: (be[i], 0, 0)),
            pl.BlockSpec((None, 1, 2 * D_FF), lambda i, be, nu: (be[i], 0, 0)),
            pl.BlockSpec((None, D_FF, D_MODEL), lambda i, be, nu: (be[i], 0, 0)),
            pl.BlockSpec((None, 1, D_MODEL), lambda i, be, nu: (be[i], 0, 0)),
        ],
        out_specs=pl.BlockSpec((MOE_TILE, D_MODEL // 2), lambda i, be, nu: (i, 0)),
        scratch_shapes=[pltpu.VMEM((D_MODEL, 2 * D_FF), BF16), pltpu.VMEM((D_FF, D_MODEL), BF16)],
    )
    return pl.pallas_call(
        _ffn_kernel,
        grid_spec=grid_spec,
        out_shape=jax.ShapeDtypeStruct((n_rows, D_MODEL // 2), jnp.uint32),
        compiler_params=pltpu.CompilerParams(dimension_semantics=("arbitrary",), vmem_limit_bytes=FFN_VMEM_LIMIT),
        name="expert_ffn",
    )(blk_e, n_used, rows, wgu, bgu, wd, bd)


def _token_tile(n):
    return next(t for t in (MOE_DMA_TOKENS, 384, 256, 128, n) if n % t == 0)


def _dispatch_scatter(dest_ref, h_ref, rows_ref, sem, td):
    def issue(i, carry):
        t0 = pl.multiple_of(i * 8, 8)
        for r in range(8):
            for k in range(TOP_K):
                d = dest_ref[i * (8 * TOP_K) + r * TOP_K + k]
                pltpu.make_async_copy(h_ref.at[pl.ds(t0 + r, 1)], rows_ref.at[pl.ds(d, 1)], sem).start()
        return carry

    lax.fori_loop(0, td // 8, issue, 0)
    for k in range(TOP_K):
        pltpu.make_async_copy(h_ref, rows_ref.at[pl.ds(0, td)], sem).wait()


def _dispatch_kernel(*refs, tiles, steps, n_blocks):
    ng = len(tiles)
    ends_ref, padded_ref, nu_ref = refs[:3]
    dest_refs = refs[3:3 + ng]
    h_refs = refs[3 + ng:3 + 2 * ng]
    rows_ref, zero_ref, sem, zsem = refs[3 + 2 * ng:]
    i = pl.program_id(0)

    @pl.when(i == 0)
    def _():
        zero_ref[...] = jnp.zeros_like(zero_ref)

        def tail_copy(e):
            start = pl.multiple_of(ends_ref[e] - MOE_TILE, MOE_TILE)
            return pltpu.make_async_copy(zero_ref, rows_ref.at[pl.ds(start, MOE_TILE)], zsem)

        def block_copy(blk):
            start = pl.multiple_of(blk * MOE_TILE, MOE_TILE)
            return pltpu.make_async_copy(zero_ref, rows_ref.at[pl.ds(start, MOE_TILE)], zsem)

        def each(start):
            def expert(e, carry):
                @pl.when(padded_ref[e] > 0)
                def _():
                    tail_copy(e).start() if start else tail_copy(e).wait()
                return carry

            def block(blk, carry):
                block_copy(blk).start() if start else block_copy(blk).wait()
                return carry

            lax.fori_loop(0, N_EXPERTS, expert, 0)
            lax.fori_loop(nu_ref[0], n_blocks, block, 0)

        each(True)
        each(False)

    first = 0
    for g in range(ng):
        @pl.when((i >= first) & (i < first + steps[g]))
        def _(g=g):
            _dispatch_scatter(dest_refs[g], h_refs[g], rows_ref, sem, tiles[g])
        first += steps[g]


def _moe_dispatch(h2s, dests, n_rows, pad_ends, padded, n_used):
    tiles = [_token_tile(h.shape[0]) for h in h2s]
    steps = [h.shape[0] // t for h, t in zip(h2s, tiles)]
    firsts = [sum(steps[:g]) for g in range(len(h2s))]

    def local(g):
        return lambda i, *_: jnp.clip(i - firsts[g], 0, steps[g] - 1)

    dest_specs = [pl.BlockSpec((tiles[g] * TOP_K,), lambda i, *_, f=local(g): (f(i),), memory_space=pltpu.SMEM)
                  for g in range(len(h2s))]
    width, dtype = h2s[0].shape[1], h2s[0].dtype
    tok_specs = [pl.BlockSpec((tiles[g], width), lambda i, *_, f=local(g): (f(i), 0)) for g in range(len(h2s))]
    grid_spec = pltpu.PrefetchScalarGridSpec(
        num_scalar_prefetch=3,
        grid=(sum(steps),),
        in_specs=dest_specs + tok_specs,
        out_specs=pl.BlockSpec(memory_space=pl.ANY),
        scratch_shapes=[pltpu.VMEM((MOE_TILE, width), dtype), pltpu.SemaphoreType.DMA(()),
                        pltpu.SemaphoreType.DMA(())],
    )
    return pl.pallas_call(
        functools.partial(_dispatch_kernel, tiles=tiles, steps=steps, n_blocks=n_rows // MOE_TILE),
        grid_spec=grid_spec,
        out_shape=jax.ShapeDtypeStruct((n_rows, width), dtype),
        compiler_params=_cparams("arbitrary"),
        name="moe_dispatch",
    )(pad_ends, padded, n_used, *dests, *h2s)


def _combine_kernel(dest_ref, x2_ref, g_ref, rows_ref, o_ref, ybuf, sem, *, td):
    def issue(i, carry):
        t0 = pl.multiple_of(i * 8, 8)
        for r in range(8):
            for k in range(TOP_K):
                d = dest_ref[i * (8 * TOP_K) + r * TOP_K + k]
                pltpu.make_async_copy(rows_ref.at[pl.ds(d, 1)], ybuf.at[k, pl.ds(t0 + r, 1)], sem).start()
        return carry

    lax.fori_loop(0, td // 8, issue, 0)
    for k in range(TOP_K):
        pltpu.make_async_copy(rows_ref.at[pl.ds(0, td)], ybuf.at[k], sem).wait()
    out = x2_ref[...]
    for k in range(TOP_K):
        out = out + g_ref[:, k:k + 1] * _unpack_bf16_pairs(ybuf[k])
    o_ref[...] = out


def _moe_combine(x2, out_rows, dest, gates):
    n = x2.shape[0]
    td = _token_tile(n)
    return pl.pallas_call(
        functools.partial(_combine_kernel, td=td),
        grid=(n // td,),
        in_specs=[pl.BlockSpec((td * TOP_K,), lambda i: (i,), memory_space=pltpu.SMEM),
                  pl.BlockSpec((td, D_MODEL), lambda i: (i, 0)),
                  pl.BlockSpec((td, LANES), lambda i: (i, 0)),
                  pl.BlockSpec(memory_space=pl.ANY)],
        out_specs=pl.BlockSpec((td, D_MODEL), lambda i: (i, 0)),
        out_shape=jax.ShapeDtypeStruct((n, D_MODEL), F32),
        scratch_shapes=[pltpu.VMEM((TOP_K, td) + out_rows.shape[1:], out_rows.dtype), pltpu.SemaphoreType.DMA(())],
        compiler_params=_cparams("arbitrary"),
        name="moe_combine",
    )(dest, x2, gates, out_rows)


def _moe(groups, counts, w):
    n_total = sum(g[0].shape[0] for g in groups)
    cnt = counts[0, :N_EXPERTS].astype(jnp.int32)
    padded = (cnt + MOE_TILE - 1) // MOE_TILE * MOE_TILE
    pad_ends = jnp.cumsum(padded)
    pad_starts = pad_ends - padded
    n_blocks = -(-n_total * TOP_K // MOE_TILE) + N_EXPERTS
    blk_start = jnp.arange(n_blocks, dtype=jnp.int32) * MOE_TILE
    blk_e = jnp.minimum(jnp.sum(blk_start[:, None] >= pad_ends[None, :], axis=1), N_EXPERTS - 1).astype(jnp.int32)
    n_used = (pad_ends[-1] // MOE_TILE).astype(jnp.int32).reshape(1)
    dests = [(pad_starts[er[:, :TOP_K]] + er[:, TOP_K:2 * TOP_K]).astype(jnp.int32).reshape(-1)
             for (_, _, er, _) in groups]
    rows = _moe_dispatch([g[1] for g in groups], dests, n_blocks * MOE_TILE, pad_ends.astype(jnp.int32),
                         padded.astype(jnp.int32), n_used)
    out_rows = _expert_ffn(rows, blk_e, n_used, w["wgu"], w["bgu"], w["wd"], w["bd"])
    return [_moe_combine(x2, out_rows, dest, gates) for (x2, _, _, gates), dest in zip(groups, dests)]


def _rel_bucket(dist):
    n = jnp.maximum(dist, 0)
    max_exact = NUM_BUCKETS // 2
    nf = jnp.maximum(n, 1).astype(F32)
    large = max_exact + (jnp.log(nf / max_exact) / math.log(MAX_DISTANCE / max_exact)
                         * (NUM_BUCKETS - max_exact)).astype(jnp.int32)
    large = jnp.minimum(large, NUM_BUCKETS - 1)
    return jnp.where(n < max_exact, n, large)


def _bias_of(rel_bias, dist, valid):
    return jnp.where(valid[..., None], rel_bias[_rel_bucket(dist)], NEG)


def _proj_segs(kv_forms, kvc_forms):
    return ((0, 512, "qscale", ("rows",), BF16), (512, 256, "qscale", ("rows",), BF16),
            (768, 256, "id", kv_forms, F32), (1024, 256, "id", kv_forms, F32), (1280, 256, "id", kvc_forms, F32),
            (1536, 256, "id", ("rows",), F32), (1792, 128, "sigmoid", ("rows",), F32))


_PROJ_SEGS_PROMPT = _proj_segs(("key_rows", "t"), ("t",))
_PROJ_SEGS_SAMPLE = _proj_segs(("rows",), ("rows",))
_PROJ_NNORM = 1280


def _prep_layer(l, rel_bias, norm1_g, w_in, nsa_qk_norm, mem_qk_norm, cmp_w, cmp_pe, pool_w, pool_scale,
                mem_norm_g, w_mem_kv, w_up_pool, w_up_nsa, w_up_mem, w_out, norm2_g, router_w, router_b,
                w_gu, b_gu, w_down, b_down):
    wi = w_in[l]
    o_u, o_q, o_qm, o_kvc, o_kvs, o_kvw, o_gn, o_gb = 0, 256, 768, 1024, 1280, 1536, 1792, 1816
    w_proj = jnp.concatenate([
        wi[:, o_q:o_q + 512], wi[:, o_qm:o_qm + 256], wi[:, o_kvs:o_kvs + 256], wi[:, o_kvw:o_kvw + 256],
        wi[:, o_kvc:o_kvc + 256], wi[:, o_u:o_u + 256], wi[:, o_gn:o_gn + 24],
        jnp.zeros((D_MODEL, LANES - 24), F32)], axis=1).astype(BF16)
    nq, mq = nsa_qk_norm[l], mem_qk_norm[l]
    ones = jnp.ones((LANES,), F32)
    gain = jnp.concatenate([jnp.tile(nq[0], 8), jnp.tile(mq[0], 4), jnp.tile(nq[2], 2), ones,
                            jnp.tile(nq[3], 2), ones])[None, :]
    nmask = jnp.concatenate([jnp.ones((768,), F32), ones, 0 * ones, ones, 0 * ones])[None, :]
    eye4 = jnp.eye(4, dtype=F32)
    cw = cmp_w[l].reshape(2, 2, CMP_STRIDE, HEAD_DIM, HEAD_DIM)
    w_c = jnp.einsum("crjde,xy->cjxdrye", cw, jnp.eye(2, dtype=F32))
    w_c = w_c.reshape(2, CMP_STRIDE, LANES, KV_WIDTH).astype(BF16)
    pe = cmp_pe[l].reshape(2, 2, CMP_STRIDE, HEAD_DIM)
    pe_c = jnp.tile(pe.transpose(0, 2, 1, 3), (1, 1, 1, NSA_KV_HEADS))
    pe_c = jnp.pad(pe_c, ((0, 0), (0, 0), (0, 6), (0, 0)))
    w_pool = jnp.einsum("gde,gh->gdhe", pool_w[l], eye4).reshape(POOL_WIDTH, POOL_WIDTH).astype(BF16)
    rw = jnp.pad(router_w[l], ((0, 0), (0, LANES - N_EXPERTS)))
    rw_hi = rw.astype(BF16)
    rw_lo = (rw - rw_hi.astype(F32)).astype(BF16)
    rb = jnp.concatenate([router_b[l], jnp.full((LANES - N_EXPERTS,), NEG, F32)])[None, :]
    return {
        "g1": norm1_g[l][None, :], "w_proj": w_proj, "gain": gain, "nmask": nmask,
        "w_kvc_t": wi[:, o_kvc:o_kvc + 256].T.astype(BF16),
        "gk_cmp": jnp.tile(nq[1], 2)[None, :], "w_c": w_c, "pe_c": pe_c,
        "w_pool": w_pool, "pool_scale": pool_scale[l][None, :],
        "mem_g": mem_norm_g[l][None, :], "w_mem": w_mem_kv[l].astype(BF16),
        "mem_gain": jnp.concatenate([jnp.tile(mq[1], 4), jnp.ones((256,), F32)])[None, :],
        "mem_nmask": jnp.concatenate([jnp.ones((256,), F32), jnp.zeros((256,), F32)])[None, :],
        "wgb": wi[:, o_gb:o_gb + 3 * D_MODEL].astype(BF16),
        "wup_pool": w_up_pool[l].astype(BF16), "wup_nsa": w_up_nsa[l].astype(BF16),
        "wup_mem": w_up_mem[l].astype(BF16), "wout": w_out[l].astype(BF16), "g2": norm2_g[l][None, :],
        "rw_hi": rw_hi, "rw_lo": rw_lo, "rb": rb,
        "wgu": w_gu[l], "bgu": b_gu[l][:, None, :], "wd": w_down[l],
        "bd": b_down[l][:, None, :],
    }


def _project_in(x2d, w, segs, tm, seq):
    wt = w["w_kvc_t"] if segs is _PROJ_SEGS_PROMPT else None
    return _project(x2d, w["g1"], w["w_proj"], w["gain"], w["nmask"], segs, _PROJ_NNORM, tm, seq, wt)


def _rows_view(a_t, heads):
    b, _, rows = a_t.shape
    return a_t.reshape(b, 2, heads, HEAD_DIM, rows).transpose(0, 4, 1, 2, 3)


def _toeplitz(v, t):
    lead = v.shape[:-1]
    flat = jnp.tile(v, (1,) * len(lead) + (t,))[..., t:t + t * (2 * t - 1)]
    return flat.reshape(lead + (t, 2 * t - 1))[..., :t]


def _flash_tables(rel_bias):
    t = ATT_TILE
    d0 = jnp.arange(-t, t)
    kinds = jnp.stack([
        _bias_of(rel_bias, d0, d0 >= 0),
        _bias_of(rel_bias, d0 + t, d0 + t >= 0),
        _bias_of(rel_bias, jnp.full((2 * t,), 2 * t), jnp.ones((2 * t,), bool)),
        _bias_of(rel_bias, d0 + WINDOW, d0 + WINDOW < WINDOW),
    ])
    tab = _toeplitz(kinds.transpose(2, 0, 1), t)
    return tab.transpose(1, 2, 0, 3).reshape(4, t, NSA_HEADS * t)


def _cmp_bias_table(rel_bias, s, n_sub, n_cmp, pos0):
    na = s // CMP_STRIDE
    m = max(na, n_sub)
    k = jnp.arange(-m, m)[None, :]
    r = jnp.arange(CMP_STRIDE)[:, None]
    d = CMP_STRIDE * k + r - (CMP_BLOCK - 1) + pos0
    v = _bias_of(rel_bias, d, d >= 0).transpose(2, 0, 1)
    tz = _toeplitz(v, m)[:, :, :n_sub, :na]
    tab = tz.transpose(0, 3, 1, 2).reshape(NSA_HEADS, s, n_sub)
    return jnp.where(jnp.arange(n_sub)[None, None, :] < n_cmp, tab, NEG)


def _prompt_pre(x, mem, w, rel_bias, cnt0):
    b, s, _ = x.shape
    n = b * s
    tm = 512 if n % 512 == 0 else ATT_TILE
    x2d = x.reshape(n, D_MODEL)
    q, qm, ks, kvs_t, kw, kvw_t, kvc_t, u, gn = _project_in(x2d, w, _PROJ_SEGS_PROMPT, tm, s)
    r3 = lambda a: a.reshape(b, s, a.shape[-1])
    q, qm, ks, kw, u, gn = map(r3, (q, qm, ks, kw, u, gn))

    o_pool = _pool_mix(u, jnp.zeros((b, 16, POOL_WIDTH), F32), w["w_pool"], w["pool_scale"], 0)

    n_cmp = (s - CMP_BLOCK) // CMP_STRIDE + 1
    n_slc = -(-s // SLC_BLOCK)
    n_lb = s // LANES
    own = jnp.broadcast_to(jnp.arange(b, dtype=jnp.int32)[:, None], (b, n_lb))
    blocks = jnp.broadcast_to(jnp.arange(n_lb, dtype=jnp.int32)[None, :], (b, n_lb))
    parts = _cmp_partials_paged(kvc_t, own, blocks, w["pe_c"], w["w_c"], math.gcd(n_lb, 16))
    n_sub = parts.shape[1]
    bias_c = _cmp_bias_table(rel_bias, s, n_sub, n_cmp, 0)
    tq = 256 if s % 256 == 0 else ATT_TILE
    o_cmp, ns0, ns1 = _cmp_attention(q, parts, bias_c, gn, w["gk_cmp"], tq=tq, n_cmp=n_cmp, n_slc=n_slc, pos0=0)

    tab = _flash_tables(rel_bias)
    o_slc = _flash_attention(q, ns0, ns1, ks, kvs_t, tab, gn, use_sel=True, band=None, gate_base=NSA_HEADS)
    o_win = _flash_attention(q, ns0, ns1, kw, kvw_t, tab, gn, use_sel=False, band=WINDOW // ATT_TILE,
                             gate_base=2 * NSA_HEADS)

    m = mem.shape[1]
    mem_kv, mem_kv_t = _project(mem.reshape(b * m, D_MODEL), w["mem_g"], w["w_mem"], w["mem_gain"],
                                w["mem_nmask"], ((0, 2 * MEM_WIDTH, "id", ("rows", "t"), F32),), MXU_DIM,
                                tm=math.gcd(m, 512), seq=m)
    o_mem = _mem_attention(qm, mem_kv.reshape(b, m, 2 * MEM_WIDTH), tq=min(512, s))

    f2 = lambda a: a.reshape(n, a.shape[-1])
    x2, h2, eidx, gates, cnt = _layer_tail(x2d, f2(o_pool), f2(o_cmp), f2(o_slc), f2(o_win), f2(o_mem), cnt0, w, tm)
    win_t = kvw_t[:, :, max(0, s - WINDOW):]
    if s < WINDOW:
        win_t = jnp.pad(win_t, ((0, 0), (0, 0), (WINDOW - s, 0)))
    states = (_rows_view(kvc_t, NSA_KV_HEADS), _rows_view(kvs_t, NSA_KV_HEADS), _rows_view(win_t, NSA_KV_HEADS),
              _rows_view(mem_kv_t, MEM_HEADS), _last_rows(u, POOL_BUF))
    return (x2, h2, eidx, gates), states, cnt


def _last_rows(a, n):
    t = a.shape[1]
    if t < n:
        a = jnp.pad(a, [(0, 0), (n - t, 0)] + [(0, 0)] * (a.ndim - 2))
    return a[:, a.shape[1] - n:]


def _dec_columns_nsa(q):
    b, t, _ = q.shape
    qh = q.reshape(b, t, NSA_KV_HEADS, NSA_HPG, HEAD_DIM)
    w = jnp.einsum("btgpd,gx->bxdgtp", qh.astype(F32), jnp.eye(NSA_KV_HEADS, dtype=F32))
    w = w.reshape(b, NSA_KV_HEADS * HEAD_DIM, NSA_KV_HEADS * t * NSA_HPG)
    return jnp.pad(w, ((0, 0), (0, KV_WIDTH - w.shape[1]), (0, LANES - w.shape[2]))).astype(BF16)


def _dec_extract_nsa(o, t):
    b = o.shape[0]
    v = o[:, LANES:, :NSA_KV_HEADS * t * NSA_HPG]
    v = v.reshape(b, NSA_KV_HEADS, HEAD_DIM, NSA_KV_HEADS, t, NSA_HPG)
    v = jnp.einsum("bxdgtp,gx->btgpd", v, jnp.eye(NSA_KV_HEADS, dtype=F32))
    return v.reshape(b, t, NSA_WIDTH)


def _dec_bias_cols(bias_tph):
    k, t, _ = bias_tph.shape
    bt = bias_tph.reshape(k, t, NSA_KV_HEADS, NSA_HPG).transpose(0, 2, 1, 3).reshape(k, NSA_KV_HEADS * t * NSA_HPG)
    return jnp.pad(bt, ((0, 0), (0, LANES - bt.shape[1])))


def _dec_gate_cols(gn, base, t):
    b = gn.shape[0]
    gt = gn[:, :, base:base + NSA_HEADS].reshape(b, t, NSA_KV_HEADS, NSA_HPG).transpose(0, 2, 1, 3)
    gt = gt.reshape(b, 1, NSA_KV_HEADS * t * NSA_HPG)
    return jnp.pad(gt, ((0, 0), (0, 0), (0, LANES - gt.shape[2])), constant_values=1.0)


def _sample_pre(x, cache_cmp, cache_slc, cache_win, cache_mem, pool_buf, page_table, w, rel_bias, cnt0):
    b, t, _ = x.shape
    n = b * t
    page = cache_cmp.shape[1]
    n_pages = page_table.shape[1]
    past = n_pages * page
    x2d = x.reshape(n, D_MODEL)
    tm = n if n <= 512 else LANES
    q, qm, kvs, kvw, kvc, u, gn = _project_in(x2d, w, _PROJ_SEGS_SAMPLE, tm, tm)
    r3 = lambda a: a.reshape(b, t, a.shape[-1])
    q, qm, kvs, kvw, kvc, u, gn = map(r3, (q, qm, kvs, kvw, kvc, u, gn))
    qpos = past + jnp.arange(t)

    buf16 = jnp.pad(pool_buf, ((0, 0), (16 - POOL_BUF, 0), (0, 0)))
    o_pool = _pool_mix(u, buf16, w["w_pool"], w["pool_scale"], past)

    total = past + t
    n_cmp = (total - CMP_BLOCK) // CMP_STRIDE + 1
    n_sub_used = n_cmp + CMP_BLOCK // CMP_STRIDE - 1
    n_slc = -(-total // SLC_BLOCK)
    pps = math.gcd(n_pages, 16)
    parts = _cmp_partials_paged(_feature_major(cache_cmp), page_table, jnp.zeros_like(page_table), w["pe_c"],
                                w["w_c"], pps)
    extra = n_sub_used * CMP_STRIDE - past
    if extra > 0:
        tail_rows = -(-extra // CMP_STRIDE) * CMP_STRIDE
        new_c = jnp.pad(kvc, ((0, 0), (0, max(0, tail_rows - t)), (0, 0)))[:, :tail_rows]
        parts = jnp.concatenate([parts, _cmp_partials_dense(new_c, w["pe_c"], w["w_c"])], axis=1)
    n_sub = parts.shape[1]
    end = jnp.arange(n_sub)[None, :] * CMP_STRIDE + CMP_BLOCK - 1
    bias_c = _bias_of(rel_bias, qpos[:, None] - end, (end <= qpos[:, None]) & (jnp.arange(n_sub)[None, :] < n_cmp))
    qpad = ((0, 0), (0, LANES - t), (0, 0))
    bias_c = jnp.pad(bias_c.transpose(2, 0, 1), qpad)
    o_cmp, ns0, ns1 = _cmp_attention(jnp.pad(q, qpad), parts, bias_c, jnp.pad(gn, qpad), w["gk_cmp"], tq=LANES,
                                     n_cmp=n_cmp, n_slc=n_slc, pos0=past)
    o_cmp = o_cmp[:, :t]

    wq = _dec_columns_nsa(q).transpose(0, 2, 1)
    ncol = NSA_KV_HEADS * t * NSA_HPG
    new_tile = lambda kv: jnp.pad(kv, ((0, 0), (0, LANES - t), (0, 0))).transpose(0, 2, 1)
    own = lambda npg: jnp.broadcast_to(jnp.arange(b, dtype=jnp.int32)[:, None], (b, npg))
    blocks = lambda npg: jnp.broadcast_to(jnp.arange(npg, dtype=jnp.int32)[None, :], (b, npg))

    n_chunks = n_pages + 1
    nblk = -(-2 * n_chunks // 8) * 8
    ns = jnp.stack([ns0, ns1], axis=1)[:, :, :, :t].transpose(0, 1, 3, 2)
    ns = jnp.pad(ns, ((0, 0), (0, 0), (0, 0), (0, max(0, nblk - ns.shape[3]))))[..., :nblk]
    ns = jnp.broadcast_to(ns[:, :, :, None, :], (b, NSA_KV_HEADS, t, NSA_HPG, nblk)).reshape(b, ncol, nblk)
    notsel = jnp.pad(ns.transpose(0, 2, 1), ((0, 0), (0, 0), (0, LANES - ncol)))
    rows = jnp.arange(LANES)
    far = _bias_of(rel_bias, jnp.full((LANES, t), 2 * MAX_DISTANCE), jnp.ones((LANES, t), bool))
    kpos_last = past - LANES + rows
    d_last = qpos[None, :] - kpos_last[:, None]
    near = _bias_of(rel_bias, d_last, d_last >= 0)
    kpos_new = past + rows
    d_new = qpos[None, :] - kpos_new[:, None]
    newb = _bias_of(rel_bias, d_new, (d_new >= 0) & (rows[:, None] < t))
    bias_tab = jnp.stack([_dec_bias_cols(far), _dec_bias_cols(near), _dec_bias_cols(newb)])
    bias_idx = jnp.concatenate([jnp.zeros((n_pages - 1,), jnp.int32), jnp.array([1, 2], jnp.int32)])
    o_slc = _decode_attention(_feature_major(cache_slc), page_table, jnp.zeros_like(page_table), new_tile(kvs), wq,
                              bias_tab, bias_idx, notsel, _dec_gate_cols(gn, NSA_HEADS, t))
    o_slc = _dec_extract_nsa(o_slc, t)

    wb = cache_win.shape[1]
    n_wpg = wb // LANES
    kpos_w = past - wb + jnp.arange(wb + LANES)
    d_w = qpos[None, :] - kpos_w[:, None]
    valid_w = (d_w >= 0) & (d_w < WINDOW) & (kpos_w[:, None] >= 0) & (jnp.arange(wb + LANES)[:, None] < wb + t)
    bias_w = _dec_bias_cols(_bias_of(rel_bias, d_w, valid_w)).reshape(n_wpg + 1, LANES, LANES)
    zeros_ns = jnp.zeros((b, -(-2 * (n_wpg + 1) // 8) * 8, LANES), F32)
    o_win = _decode_attention(_feature_major(cache_win), own(n_wpg), blocks(n_wpg), new_tile(kvw), wq, bias_w,
                              jnp.arange(n_wpg + 1, dtype=jnp.int32), zeros_ns, _dec_gate_cols(gn, 2 * NSA_HEADS, t))
    o_win = _dec_extract_nsa(o_win, t)

    m = cache_mem.shape[1]
    n_mpg = m // LANES
    qmh = qm.reshape(b, t, MEM_HEADS, HEAD_DIM).astype(F32)
    wqm = jnp.einsum("bthd,hx->bxdht", qmh, jnp.eye(MEM_HEADS, dtype=F32))
    wqm = wqm.reshape(b, MEM_WIDTH, MEM_HEADS * t)
    wqm = jnp.pad(wqm, ((0, 0), (0, MEM_WIDTH), (0, LANES - MEM_HEADS * t))).astype(BF16).transpose(0, 2, 1)
    o_mem = _decode_attention(_feature_major(cache_mem), own(n_mpg), blocks(n_mpg), None, wqm,
                              jnp.zeros((1, LANES, LANES), F32), jnp.zeros((n_mpg,), jnp.int32),
                              jnp.zeros((b, 8, LANES), F32), jnp.ones((b, 1, LANES), F32))
    om = o_mem[:, MEM_WIDTH:, :MEM_HEADS * t].reshape(b, MEM_HEADS, HEAD_DIM, MEM_HEADS, t)
    o_mem = jnp.einsum("bxdht,hx->bthd", om, jnp.eye(MEM_HEADS, dtype=F32)).reshape(b, t, MEM_WIDTH)

    f2 = lambda a: a.reshape(n, a.shape[-1])
    x2, h2, eidx, gates, cnt = _layer_tail(x2d, f2(o_pool), f2(o_cmp), f2(o_slc), f2(o_win), f2(o_mem), cnt0, w,
                                           n if n <= 512 else LANES)
    kvshape = (b, t, 2, NSA_KV_HEADS, HEAD_DIM)
    new_win = _rows_view(jnp.concatenate([_feature_major(cache_win)[:, :, t:], kvw.transpose(0, 2, 1)], axis=2),
                         NSA_KV_HEADS)
    new_pool = jnp.concatenate([pool_buf, u], axis=1)[:, t:]
    states = (kvc.reshape(kvshape), kvs.reshape(kvshape), new_win, new_pool)
    return (x2, h2, eidx, gates), states, cnt


def kernel(x_prompt, x_sample, cache_cmp_kv, cache_slc_kv, cache_win_kv, cache_mem_kv, state_pool, page_table,
           mem_prompt, rel_bias, norm1_g, w_in, nsa_qk_norm, mem_qk_norm, cmp_w, cmp_pe, pool_w, pool_scale,
           mem_norm_g, w_mem_kv, w_up_pool, w_up_nsa, w_up_mem, w_out, norm2_g, router_w, router_b, w_gu, b_gu,
           w_down, b_down):
    depth = w_in.shape[0]
    yp, ys = x_prompt, x_sample
    bp, sp, _ = x_prompt.shape
    bs, ts, _ = x_sample.shape
    outs_p = [[] for _ in range(5)]
    outs_s = [[] for _ in range(4)]
    for l in range(depth):
        w = _prep_layer(l, rel_bias, norm1_g, w_in, nsa_qk_norm, mem_qk_norm, cmp_w, cmp_pe, pool_w, pool_scale,
                        mem_norm_g, w_mem_kv, w_up_pool, w_up_nsa, w_up_mem, w_out, norm2_g, router_w, router_b,
                        w_gu, b_gu, w_down, b_down)
        pre_p, st_p, cnt = _prompt_pre(yp, mem_prompt, w, rel_bias, jnp.zeros((1, LANES), F32))
        pre_s, st_s, cnt = _sample_pre(ys, cache_cmp_kv[l], cache_slc_kv[l], cache_win_kv[l], cache_mem_kv[l],
                                       state_pool[l], page_table, w, rel_bias, cnt)
        yp, ys = _moe([pre_p, pre_s], cnt, w)
        yp = yp.reshape(bp, sp, D_MODEL)
        ys = ys.reshape(bs, ts, D_MODEL)
        for lst, a in zip(outs_p, st_p):
            lst.append(a)
        for lst, a in zip(outs_s, st_s):
            lst.append(a)
    new_cmp_p, new_slc_p, new_win_p, new_mem_p, new_pool_p = [jnp.stack(a) for a in outs_p]
    new_cmp_s, new_slc_s, new_win_s, new_pool_s = [jnp.stack(a) for a in outs_s]
    new_win_s = new_win_s.reshape(new_win_s.shape[:3] + (2, NSA_KV_HEADS, HEAD_DIM))
    return (yp, ys, new_cmp_p, new_slc_p, new_win_p, new_mem_p, new_pool_p,
            new_cmp_s, new_slc_s, new_win_s, new_pool_s)
```

```python
import functools
import math

import jax
import jax.numpy as jnp
from jax import lax
from jax.experimental import pallas as pl
from jax.experimental.pallas import tpu as pltpu

F32 = jnp.float32
BF16 = jnp.bfloat16

D_MODEL = 1024
HEAD_DIM = 64
POOL_WINDOWS = (2, 4, 8, 16)
POOL_GROUP = 64
POOL_WIDTH = 256
POOL_BUF = 15
NSA_HEADS = 8
NSA_KV_HEADS = 2
NSA_HPG = 4
NSA_WIDTH = 512
KV_WIDTH = 256
CMP_BLOCK = 32
CMP_STRIDE = 16
SLC_BLOCK = 64
SLC_TOPK = 16
WINDOW = 512
MEM_HEADS = 4
MEM_WIDTH = 256
NUM_BUCKETS = 32
MAX_DISTANCE = 128
N_EXPERTS = 32
TOP_K = 4
D_FF = 1024
SWIGLU_ALPHA = 1.702
SWIGLU_LIMIT = 7.0
EPS = 1e-6
SCALE = HEAD_DIM ** -0.5

LANES = 128
MXU_DIM = 256
NEG = -1e30
ATT_TILE = 256
DEC_PAGES_PER_STEP = 16
MOE_TILE = 512
MOE_DMA_TOKENS = 512
VMEM_LIMIT = 48 * 1024 * 1024
FFN_VMEM_LIMIT = 56 * 1024 * 1024


def _cparams(*sem):
    return pltpu.CompilerParams(dimension_semantics=sem, vmem_limit_bytes=VMEM_LIMIT)


def _dot(a, b):
    return jnp.dot(a, b, preferred_element_type=F32)


def _dot_nt(a, b):
    return lax.dot_general(a, b, (((1,), (1,)), ((), ())), preferred_element_type=F32)


def _split_dot(a, b):
    hi = a.astype(BF16)
    lo = (a - hi.astype(F32)).astype(BF16)
    return _dot(hi, b) + _dot(lo, b)


def _rms(x, g):
    r = lax.rsqrt(jnp.mean(x * x, axis=-1, keepdims=True) + EPS)
    return (x * r) * g


def _sigmoid(x):
    return 1.0 / (1.0 + jnp.exp(-x))


def _pack_bf16_pairs(x):
    w = x.shape[1] // 2
    bits = lax.bitcast_convert_type(x.astype(BF16).astype(F32), jnp.uint32)
    return jnp.right_shift(bits[:, :w], jnp.uint32(16)) | (bits[:, w:] & jnp.uint32(0xFFFF0000))


def _unpack_bf16_pairs(p):
    lo = lax.bitcast_convert_type(jnp.left_shift(p, jnp.uint32(16)), F32)
    hi = lax.bitcast_convert_type(p & jnp.uint32(0xFFFF0000), F32)
    return jnp.concatenate([lo, hi], axis=1)


def _lane_iota(shape):
    return lax.broadcasted_iota(jnp.int32, shape, len(shape) - 1)


def _row_iota(shape):
    return lax.broadcasted_iota(jnp.int32, shape, len(shape) - 2)


def _proj_kernel(x_ref, g_ref, w_ref, gain_ref, nmask_ref, seg_ref, *rest, segs, n_norm, has_wt):
    wt_ref = rest[0] if has_wt else None
    out_refs = rest[1:] if has_wt else rest
    h = _rms(x_ref[...], g_ref[...]).astype(BF16)
    seg = seg_ref[...]
    outs = iter(out_refs)
    for (start, width, kind, forms, _) in segs:
        if forms == ("t",) and has_wt:
            next(outs)[...] = _dot_nt(wt_ref[...], h)
            continue
        z = _dot(h, w_ref[:, start:start + width])
        if start < n_norm:
            pieces = []
            for c in range(0, width, MXU_DIM):
                zc = z[:, c:c + MXU_DIM]
                ms = _split_dot(zc * zc, seg)
                zn = (zc * lax.rsqrt(ms + EPS)) * gain_ref[:, start + c:start + c + MXU_DIM]
                pieces.append(jnp.where(nmask_ref[:, start + c:start + c + MXU_DIM] > 0, zn, zc))
            z = pieces[0] if len(pieces) == 1 else jnp.concatenate(pieces, axis=1)
        if kind == "sigmoid":
            z = _sigmoid(z)
        elif kind == "qscale":
            z = z * SCALE
        for form in forms:
            o_ref = next(outs)
            if form == "rows":
                o_ref[...] = z.astype(o_ref.dtype)
            elif form == "key_rows":
                o_ref[...] = z[:, 0:LANES]
            else:
                o_ref[...] = z.T


def _project(x, g, w, gain, nmask, segs, n_norm, tm, seq, wt=None):
    n = x.shape[0]
    ncol = w.shape[1]
    seg = _seg_matrix(MXU_DIM)
    full = lambda i: (0, 0)
    tpb = seq // tm
    out_specs, out_shape = [], []
    for (_, wd, _, forms, dt) in segs:
        for form in forms:
            if form == "rows":
                out_specs.append(pl.BlockSpec((tm, wd), lambda i: (i, 0)))
                out_shape.append(jax.ShapeDtypeStruct((n, wd), dt))
            elif form == "key_rows":
                out_specs.append(pl.BlockSpec((tm, LANES), lambda i: (i, 0)))
                out_shape.append(jax.ShapeDtypeStruct((n, LANES), F32))
            else:
                out_specs.append(pl.BlockSpec((None, wd, tm), lambda i: (i // tpb, 0, i % tpb)))
                out_shape.append(jax.ShapeDtypeStruct((n // seq, wd, seq), F32))
    extra = () if wt is None else (wt,)
    return pl.pallas_call(
        functools.partial(_proj_kernel, segs=segs, n_norm=n_norm, has_wt=wt is not None),
        grid=(n // tm,),
        in_specs=[
            pl.BlockSpec((tm, D_MODEL), lambda i: (i, 0)),
            pl.BlockSpec((1, D_MODEL), full),
            pl.BlockSpec((D_MODEL, ncol), full),
            pl.BlockSpec((1, gain.shape[1]), full),
            pl.BlockSpec((1, nmask.shape[1]), full),
            pl.BlockSpec((MXU_DIM, MXU_DIM), full),
        ] + [pl.BlockSpec(a.shape, full) for a in extra],
        out_specs=out_specs,
        out_shape=out_shape,
        compiler_params=_cparams("parallel"),
        name="proj",
    )(x, g, w, gain, nmask, seg, *extra)


def _seg_matrix(n):
    i = jnp.arange(n) // HEAD_DIM
    return jnp.where(i[:, None] == i[None, :], 1.0 / HEAD_DIM, 0.0).astype(BF16)


def _pool_kernel(u_ref, buf_ref, w_ref, scale_ref, o_ref, zs_ref, *, t, pos0):
    zs_ref[0:16, :] = buf_ref[...]
    zs_ref[16:16 + t, :] = u_ref[...]
    u = u_ref[...]
    lane = _lane_iota((1, POOL_WIDTH))
    pos = (pos0 + _row_iota((t, 1))).astype(F32)
    acc = u
    mean = None
    for i in range(1, max(POOL_WINDOWS)):
        acc = acc + zs_ref[16 - i:16 - i + t, :]
        if i + 1 in POOL_WINDOWS:
            gi = POOL_WINDOWS.index(i + 1)
            m = acc / jnp.minimum(pos + 1.0, float(i + 1))
            mean = m if mean is None else jnp.where(lane >= gi * POOL_GROUP, m, mean)
    d = (mean - u).astype(BF16)
    o_ref[...] = _dot(d, w_ref[...]) * scale_ref[...]


def _pool_mix(u, buf16, w_bd, scale, pos0):
    b, t, _ = u.shape
    return pl.pallas_call(
        functools.partial(_pool_kernel, t=t, pos0=pos0),
        grid=(b,),
        in_specs=[
            pl.BlockSpec((None, t, POOL_WIDTH), lambda i: (i, 0, 0)),
            pl.BlockSpec((None, 16, POOL_WIDTH), lambda i: (i, 0, 0)),
            pl.BlockSpec((POOL_WIDTH, POOL_WIDTH), lambda i: (0, 0)),
            pl.BlockSpec((1, POOL_WIDTH), lambda i: (0, 0)),
        ],
        out_specs=pl.BlockSpec((None, t, POOL_WIDTH), lambda i: (i, 0, 0)),
        out_shape=jax.ShapeDtypeStruct((b, t, POOL_WIDTH), F32),
        scratch_shapes=[pltpu.VMEM((t + 16, POOL_WIDTH), F32)],
        compiler_params=_cparams("parallel"),
        name="pool",
    )(u, buf16, w_bd, scale)


def _cpart_compute(rows_of, pe_ref, w_ref, o_ref, m):
    for c in range(2):
        acc = jnp.zeros((m + 8, KV_WIDTH), F32)
        for j in range(CMP_STRIDE):
            lhs = jnp.concatenate([rows_of(c, j), pe_ref[c, j]], axis=0)
            acc = acc + _dot(lhs.astype(BF16), w_ref[c, j])
        lane = _lane_iota((1, KV_WIDTH))
        pe_term = jnp.where(lane < LANES, acc[m:m + 1], acc[m + 1:m + 2])
        o_ref[:, c * KV_WIDTH:(c + 1) * KV_WIDTH] = acc[0:m] + pe_term


def _cpart_kernel(xk_ref, xv_ref, pe_ref, w_ref, o_ref, *, rows):
    m = rows // CMP_STRIDE
    x_refs = (xk_ref, xv_ref)
    _cpart_compute(lambda c, j: x_refs[c][pl.ds(j, m, stride=CMP_STRIDE), :], pe_ref, w_ref, o_ref, m)


def _cpart_paged_kernel(*refs, nop, page):
    page_refs = refs[2:2 + nop]
    perm_ref, pe_ref, w_ref, o_ref, xs_ref = refs[2 + nop:]
    n = page // CMP_STRIDE
    for k, r in enumerate(page_refs):
        y = _dot_nt(perm_ref[...], r[...].astype(BF16))
        for j in range(CMP_STRIDE):
            for c in range(2):
                xs_ref[c, j, k * n:(k + 1) * n, :] = y[j * n:(j + 1) * n, c * LANES:(c + 1) * LANES]
    _cpart_compute(lambda c, j: xs_ref[c, j], pe_ref, w_ref, o_ref, nop * n)


def _cmp_partials_dense(kv, pe, w_c):
    b, t, _ = kv.shape
    rows = (t // CMP_STRIDE) * CMP_STRIDE
    n = rows // CMP_STRIDE
    return pl.pallas_call(
        functools.partial(_cpart_kernel, rows=rows),
        grid=(b,),
        in_specs=[
            pl.BlockSpec((None, rows, LANES), lambda i: (i, 0, 0)),
            pl.BlockSpec((None, rows, LANES), lambda i: (i, 0, 1)),
            pl.BlockSpec(pe.shape, lambda i: (0, 0, 0, 0)),
            pl.BlockSpec(w_c.shape, lambda i: (0, 0, 0, 0)),
        ],
        out_specs=pl.BlockSpec((None, n, 2 * KV_WIDTH), lambda i: (i, 0, 0)),
        out_shape=jax.ShapeDtypeStruct((b, n, 2 * KV_WIDTH), F32),
        compiler_params=_cparams("parallel"),
        name="cmp_partials",
    )(kv, kv, pe, w_c)


def _cmp_partials_paged(pool_t, phys, lblk, pe, w_c, pages_per_step):
    b, n_pages = phys.shape
    page = LANES
    nop = pages_per_step
    n = nop * page // CMP_STRIDE

    def page_spec(k):
        return pl.BlockSpec((None, KV_WIDTH, page),
                            lambda i, c, ph, lb: (ph[i, c * nop + k], 0, lb[i, c * nop + k]))

    r_idx = jnp.arange(page)
    regroup = (r_idx[None, :] == ((r_idx % (page // CMP_STRIDE)) * CMP_STRIDE + r_idx // (page // CMP_STRIDE))[:, None])
    regroup = regroup.astype(BF16)
    grid_spec = pltpu.PrefetchScalarGridSpec(
        num_scalar_prefetch=2,
        grid=(b, n_pages // nop),
        in_specs=[page_spec(k) for k in range(nop)] + [
            pl.BlockSpec(regroup.shape, lambda i, c, ph, lb: (0, 0)),
            pl.BlockSpec(pe.shape, lambda i, c, ph, lb: (0, 0, 0, 0)),
            pl.BlockSpec(w_c.shape, lambda i, c, ph, lb: (0, 0, 0, 0)),
        ],
        out_specs=pl.BlockSpec((None, n, 2 * KV_WIDTH), lambda i, c, ph, lb: (i, c, 0)),
        scratch_shapes=[pltpu.VMEM((2, CMP_STRIDE, n, LANES), F32)],
    )
    return pl.pallas_call(
        functools.partial(_cpart_paged_kernel, nop=nop, page=page),
        grid_spec=grid_spec,
        out_shape=jax.ShapeDtypeStruct((b, n_pages * page // CMP_STRIDE, 2 * KV_WIDTH), F32),
        compiler_params=_cparams("parallel", "arbitrary"),
        name="cmp_partials_paged",
    )(phys, lblk, *([pool_t] * nop), regroup, pe, w_c)


def _group_query_columns(q_ref, g, t):
    zeros64 = jnp.zeros((HEAD_DIM, t), BF16)
    cols = []
    for pr in range(2):
        qt = q_ref[:, (2 * g + pr) * LANES:(2 * g + pr + 1) * LANES].astype(F32).T.astype(BF16)
        for half in range(2):
            qh = qt[half * HEAD_DIM:(half + 1) * HEAD_DIM]
            cols.append(jnp.concatenate([qh, zeros64] if g == 0 else [zeros64, qh], axis=0))
    return jnp.concatenate(cols, axis=1)


def _store_group_output(o_ref, out_t, g, t):
    for pr in range(2):
        pair = jnp.concatenate([out_t[:, (2 * pr) * t:(2 * pr + 1) * t], out_t[:, (2 * pr + 1) * t:(2 * pr + 2) * t]],
                               axis=0)
        o_ref[:, (2 * g + pr) * LANES:(2 * g + pr + 1) * LANES] = pair.T


def _cattn_kernel(q_ref, p_ref, bias_ref, gn_ref, gk_ref, seg_ref, o_ref, ns0_ref, ns1_ref, *,
                  tq, n_sub, n_cmp, n_slc, nslp, pos0):
    qi = pl.program_id(0)
    pall = p_ref[...]
    kraw = pall[:, 0:LANES] + pltpu.roll(pall[:, LANES:2 * LANES], n_sub - 1, 0)
    vc = pall[:, 2 * LANES:3 * LANES] + pltpu.roll(pall[:, 3 * LANES:4 * LANES], n_sub - 1, 0)
    ms = _split_dot(kraw * kraw, seg_ref[...])
    kc = ((kraw * lax.rsqrt(ms + EPS)) * gk_ref[...]).astype(BF16)
    vct = vc.T.astype(BF16)

    nsel = -(-n_slc // 8) * 8
    jj = _row_iota((nsel, n_sub))
    nn = _lane_iota((nsel, n_sub))
    covers_t = ((nn * CMP_STRIDE < (jj + 1) * SLC_BLOCK) & (nn * CMP_STRIDE + CMP_BLOCK - 1 >= jj * SLC_BLOCK)
                & (nn < n_cmp) & (jj < n_slc))
    covers_t = jnp.where(covers_t, 1.0, 0.0).astype(BF16)
    qpos = pos0 + qi * tq + _lane_iota((1, tq))
    qblk = jnp.right_shift(qpos, SLC_BLOCK.bit_length() - 1)
    jr = _row_iota((nsel, tq))
    jrf = jr.astype(F32)
    forced = (jr == 0) | (jr == qblk) | (jr == qblk - 1)
    causal = jr <= qblk
    gnt = gn_ref[...].T

    for g in range(NSA_KV_HEADS):
        bias = bias_ref[g]
        s = _dot(kc, _group_query_columns(q_ref, g, tq)) + bias
        m = jnp.max(s, axis=0, keepdims=True)
        m = jnp.where(m > 0.5 * NEG, m, 0.0)
        e = jnp.where(bias > 0.5 * NEG, jnp.exp(s - m), 0.0)
        p = e / jnp.maximum(jnp.sum(e, axis=0, keepdims=True), 1e-30)
        h0 = NSA_HPG * g
        gate = jnp.concatenate([gnt[h0 + c:h0 + c + 1] for c in range(NSA_HPG)], axis=1)
        out_t = _dot(vct[g * HEAD_DIM:(g + 1) * HEAD_DIM], p.astype(BF16)) * gate
        _store_group_output(o_ref, out_t, g, tq)

        prsum = p[:, 0:tq] + p[:, tq:2 * tq] + p[:, 2 * tq:3 * tq] + p[:, 3 * tq:4 * tq]
        hi = prsum.astype(BF16)
        lo = (prsum - hi.astype(F32)).astype(BF16)
        imp = _dot(covers_t, hi) + _dot(covers_t, lo)
        score = jnp.where(forced, jnp.inf, imp)
        score = jnp.where(causal, score, -jnp.inf)
        sel = jnp.zeros((nsel, tq), F32)
        for _ in range(min(SLC_TOPK, n_slc)):
            m = jnp.max(score, axis=0, keepdims=True)
            idx = jnp.min(jnp.where(score == m, jrf, 1e9), axis=0, keepdims=True)
            pick = jrf == idx
            sel = jnp.where(pick & (m > -jnp.inf), 1.0, sel)
            score = jnp.where(pick, -jnp.inf, score)
        ns = 1.0 - sel
        if nslp > nsel:
            ns = jnp.concatenate([ns, jnp.ones((nslp - nsel, tq), F32)], axis=0)
        (ns0_ref if g == 0 else ns1_ref)[...] = ns


def _cmp_attention(q, parts, bias, gn, gk, *, tq, n_cmp, n_slc, pos0):
    b, s, _ = q.shape
    n_sub = parts.shape[1]
    nslp = -(-n_slc // LANES) * LANES
    bias_t = bias.reshape(NSA_KV_HEADS, NSA_HPG, s // tq, tq, n_sub).transpose(0, 4, 2, 1, 3)
    bias_t = bias_t.reshape(NSA_KV_HEADS, n_sub, NSA_HPG * s)
    kern = functools.partial(_cattn_kernel, tq=tq, n_sub=n_sub, n_cmp=n_cmp, n_slc=n_slc, nslp=nslp, pos0=pos0)
    return pl.pallas_call(
        kern,
        grid=(s // tq, b),
        in_specs=[
            pl.BlockSpec((None, tq, NSA_WIDTH), lambda i, j: (j, i, 0)),
            pl.BlockSpec((None, n_sub, 2 * KV_WIDTH), lambda i, j: (j, 0, 0)),
            pl.BlockSpec((NSA_KV_HEADS, n_sub, NSA_HPG * tq), lambda i, j: (0, 0, i)),
            pl.BlockSpec((None, tq, LANES), lambda i, j: (j, i, 0)),
            pl.BlockSpec((1, LANES), lambda i, j: (0, 0)),
            pl.BlockSpec((LANES, LANES), lambda i, j: (0, 0)),
        ],
        out_specs=[
            pl.BlockSpec((None, tq, NSA_WIDTH), lambda i, j: (j, i, 0)),
            pl.BlockSpec((None, nslp, tq), lambda i, j: (j, 0, i)),
            pl.BlockSpec((None, nslp, tq), lambda i, j: (j, 0, i)),
        ],
        out_shape=[
            jax.ShapeDtypeStruct((b, s, NSA_WIDTH), F32),
            jax.ShapeDtypeStruct((b, nslp, s), F32),
            jax.ShapeDtypeStruct((b, nslp, s), F32),
        ],
        compiler_params=_cparams("parallel", "parallel"),
        name="cmp_attention",
    )(q, parts, bias_t, gn, gk, _seg_matrix(LANES))


def _flash_kernel(q_ref, ns0_ref, ns1_ref, kv_ref, vt_ref, tab_ref, gn_ref, o_ref, *, t, use_sel, band, gate_base):
    qi = pl.program_id(1)
    row_k = _row_iota((t, LANES))
    lane_k = _lane_iota((t, LANES))
    lo_tile = jnp.maximum(qi - band, 0) if band is not None else 0
    gnt = gn_ref[...].T

    qts = []
    for g in range(NSA_KV_HEADS):
        qt_g = _group_query_columns(q_ref, g, t)
        if use_sel:
            nst = (ns0_ref if g == 0 else ns1_ref)[...].astype(BF16)
            qt_g = jnp.concatenate([qt_g, jnp.concatenate([nst] * NSA_HPG, axis=1)], axis=0)
        qts.append(qt_g)

    def body(kj, carry):
        k0 = pl.multiple_of(kj * t, t)
        kk = kv_ref[pl.ds(k0, t), :].astype(BF16)
        if use_sel:
            blk = kj * (t // SLC_BLOCK) + jnp.right_shift(row_k, SLC_BLOCK.bit_length() - 1)
            onehot = jnp.where(lane_k == blk, -(2.0 ** 30), 0.0).astype(BF16)
            kk = jnp.concatenate([kk, onehot], axis=1)
        delta = qi - kj
        if band is None:
            kind = jnp.minimum(delta, 2)
        else:
            kind = jnp.where(delta < 2, delta, jnp.where(delta < band, 2, 3))
        m_old, l_old, acc = carry
        s = _dot(kk, qt_all) + tab_ref[kind]
        m_new = jnp.maximum(m_old, jnp.max(s, axis=0, keepdims=True))
        alpha = jnp.exp(m_old - m_new)
        p = jnp.exp(s - m_new)
        l_new = alpha * l_old + jnp.sum(p, axis=0, keepdims=True)
        pb = p.astype(BF16)
        pv = jnp.concatenate(
            [_dot(vt_ref[g * HEAD_DIM:(g + 1) * HEAD_DIM, pl.ds(k0, t)].astype(BF16), pb[:, g * gw:(g + 1) * gw])
             for g in range(NSA_KV_HEADS)], axis=1)
        return m_new, l_new, acc * alpha + pv

    gw = NSA_HPG * t
    qt_all = jnp.concatenate(qts, axis=1)
    init = (jnp.full((1, NSA_HEADS * t), NEG, F32), jnp.zeros((1, NSA_HEADS * t), F32),
            jnp.zeros((HEAD_DIM, NSA_HEADS * t), F32))
    _, l_fin, acc = lax.fori_loop(lo_tile, qi + 1, body, init)
    gate = jnp.concatenate([gnt[gate_base + h:gate_base + h + 1] for h in range(NSA_HEADS)], axis=1)
    out = acc * (gate / l_fin)
    for g in range(NSA_KV_HEADS):
        _store_group_output(o_ref, out[:, g * gw:(g + 1) * gw], g, t)


def _flash_attention(q, ns0, ns1, k_rows, kv_t, tab, gn, *, use_sel, band, gate_base):
    b, s, _ = q.shape
    t = ATT_TILE
    assert ns0.shape[1] == LANES
    kern = functools.partial(_flash_kernel, t=t, use_sel=use_sel, band=band, gate_base=gate_base)
    tile = lambda w: pl.BlockSpec((None, t, w), lambda i, j: (i, j, 0))
    ns_tile = pl.BlockSpec((None, LANES, t), lambda i, j: (i, 0, j))
    return pl.pallas_call(
        kern,
        grid=(b, s // t),
        in_specs=[
            tile(NSA_WIDTH), ns_tile, ns_tile,
            pl.BlockSpec((None, s, LANES), lambda i, j: (i, 0, 0)),
            pl.BlockSpec((None, LANES, s), lambda i, j: (i, 1, 0)),
            pl.BlockSpec(tab.shape, lambda i, j: (0, 0, 0)),
            tile(LANES),
        ],
        out_specs=tile(NSA_WIDTH),
        out_shape=jax.ShapeDtypeStruct((b, s, NSA_WIDTH), F32),
        compiler_params=_cparams("parallel", "parallel"),
        name="flash_sel" if use_sel else "flash_win",
    )(q, ns0, ns1, k_rows, kv_t, tab, gn)


def _memattn_kernel(q_ref, kv_ref, o_ref):
    lane = _lane_iota((kv_ref.shape[0], LANES))
    for pr in range(MEM_HEADS // 2):
        qpair = q_ref[:, pr * LANES:(pr + 1) * LANES]
        kblk = kv_ref[:, pr * LANES:(pr + 1) * LANES]
        vblk = kv_ref[:, MEM_WIDTH + pr * LANES:MEM_WIDTH + (pr + 1) * LANES]
        out = None
        for half in range(2):
            keep = (lane < HEAD_DIM) if half == 0 else (lane >= HEAD_DIM)
            kk = jnp.where(keep, kblk, 0.0).astype(BF16)
            vv = jnp.where(keep, vblk, 0.0).astype(BF16)
            s = _dot_nt(qpair, kk)
            m = jnp.max(s, axis=1, keepdims=True)
            e = jnp.exp(s - m)
            p = e / jnp.sum(e, axis=1, keepdims=True)
            o = _dot(p.astype(BF16), vv)
            out = o if out is None else out + o
        o_ref[:, pr * LANES:(pr + 1) * LANES] = out


def _mem_attention(qm, mem_kv, tq):
    b, s, _ = qm.shape
    m = mem_kv.shape[1]
    return pl.pallas_call(
        _memattn_kernel,
        grid=(b, s // tq),
        in_specs=[
            pl.BlockSpec((None, tq, MEM_WIDTH), lambda i, j: (i, j, 0)),
            pl.BlockSpec((None, m, 2 * MEM_WIDTH), lambda i, j: (i, 0, 0)),
        ],
        out_specs=pl.BlockSpec((None, tq, MEM_WIDTH), lambda i, j: (i, j, 0)),
        out_shape=jax.ShapeDtypeStruct((b, s, MEM_WIDTH), F32),
        compiler_params=_cparams("parallel", "parallel"),
        name="mem_attention",
    )(qm, mem_kv)


def _dec_kernel(*refs, n_pg, pps, has_new):
    bidx_ref = refs[2]
    page_refs = refs[3:3 + pps]
    new_ref, wq_ref, bias_ref, ns_ref, gate_ref, o_ref, acc_ref, m_ref, l_ref = refs[3 + pps:]
    c = pl.program_id(1)
    n_chunks = pl.num_programs(1)

    @pl.when(c == 0)
    def _():
        acc_ref[...] = jnp.zeros_like(acc_ref)
        m_ref[...] = jnp.full_like(m_ref, NEG)
        l_ref[...] = jnp.zeros_like(l_ref)

    rk = _row_iota((LANES, LANES))

    def step(tiles, first_page):
        feats = (tiles[0] if len(tiles) == 1 else jnp.concatenate(tiles, axis=1)).astype(BF16)
        s = lax.dot_general(feats, wq_ref[...], (((0,), (1,)), ((), ())),
                            preferred_element_type=F32)
        extra = []
        for k in range(len(tiles)):
            pg = first_page + k
            ns = jnp.where(rk < SLC_BLOCK, ns_ref[pl.ds(2 * pg, 1), :], ns_ref[pl.ds(2 * pg + 1, 1), :])
            extra.append(jnp.where(ns > 0.5, NEG, bias_ref[bidx_ref[pg]]))
        s = s + (extra[0] if len(extra) == 1 else jnp.concatenate(extra, axis=0))
        m_old = m_ref[...]
        m_new = jnp.maximum(m_old, jnp.max(s, axis=0, keepdims=True))
        alpha = jnp.exp(m_old - m_new)
        p = jnp.exp(s - m_new)
        l_ref[...] = alpha * l_ref[...] + jnp.sum(p, axis=0, keepdims=True)
        m_ref[...] = m_new
        acc_ref[...] = acc_ref[...] * alpha + _dot(feats, p.astype(BF16))

    if has_new:
        @pl.when(c < n_chunks - 1)
        def _():
            step([r[...] for r in page_refs], c * pps)

        @pl.when(c == n_chunks - 1)
        def _():
            step([r[...] for r in page_refs] + [new_ref[...]], n_pg - pps)
    else:
        step([r[...] for r in page_refs], c * pps)

    @pl.when(c == n_chunks - 1)
    def _():
        o_ref[...] = acc_ref[...] / l_ref[...] * gate_ref[...]


def _decode_attention(pages, phys, lblk, new_rows, wq, bias_tab, bias_idx, notsel, gate):
    bsz, n_pg = phys.shape
    w = pages.shape[1]
    has_new = new_rows is not None
    pps = math.gcd(n_pg, DEC_PAGES_PER_STEP)
    n_steps = n_pg // pps
    n_chunks = n_steps
    if not has_new:
        new_rows = jnp.zeros((1, w, LANES), F32)
    new_map = (lambda i, c, ph, lb, bi: (i, 0, 0)) if has_new else (lambda i, c, ph, lb, bi: (0, 0, 0))

    def page_spec(k):
        def index(i, c, ph, lb, bi):
            return (ph[i, c * pps + k], 0, lb[i, c * pps + k])
        return pl.BlockSpec((None, w, LANES), index)

    per_b = lambda i, c, ph, lb, bi: (i, 0, 0)
    grid_spec = pltpu.PrefetchScalarGridSpec(
        num_scalar_prefetch=3,
        grid=(bsz, n_chunks),
        in_specs=[page_spec(k) for k in range(pps)] + [
            pl.BlockSpec((None, w, LANES), new_map),
            pl.BlockSpec((None, LANES, w), per_b),
            pl.BlockSpec(bias_tab.shape, lambda i, c, ph, lb, bi: (0, 0, 0)),
            pl.BlockSpec((None, notsel.shape[1], LANES), per_b),
            pl.BlockSpec((None, 1, LANES), per_b),
        ],
        out_specs=pl.BlockSpec((None, w, LANES), per_b),
        scratch_shapes=[pltpu.VMEM((w, LANES), F32), pltpu.VMEM((1, LANES), F32), pltpu.VMEM((1, LANES), F32)],
    )
    return pl.pallas_call(
        functools.partial(_dec_kernel, n_pg=n_pg, pps=pps, has_new=has_new),
        grid_spec=grid_spec,
        out_shape=jax.ShapeDtypeStruct((bsz, w, LANES), F32),
        compiler_params=_cparams("parallel", "arbitrary"),
        name="decode_attention",
    )(phys, lblk, bias_idx, *([pages] * pps), new_rows, wq, bias_tab, notsel, gate)


def _feature_major(cache):
    n, rows = cache.shape[:2]
    return cache.transpose(0, 2, 3, 4, 1).reshape(n, -1, rows)


def _tail_kernel(x_ref, g1_ref, op_ref, oc_ref, os_ref, ow_ref, om_ref, cnt0_ref, wgb_ref, wup_p_ref, wup_n_ref,
                 wup_m_ref, wout_ref, g2_ref, rwh_ref, rwl_ref, rb_ref, x2_ref, h2_ref, ei_ref, gt_ref, cnt_ref):
    @pl.when(pl.program_id(0) == 0)
    def _():
        cnt_ref[...] = cnt0_ref[...]

    x = x_ref[...]
    h = _rms(x, g1_ref[...]).astype(BF16)
    onsa = (oc_ref[...] + os_ref[...] + ow_ref[...]).astype(BF16)
    ups = (_dot(op_ref[...].astype(BF16), wup_p_ref[...]), _dot(onsa, wup_n_ref[...]),
           _dot(om_ref[...].astype(BF16), wup_m_ref[...]))
    mixed = None
    for br in range(3):
        gb = _sigmoid(_dot(h, wgb_ref[:, br * D_MODEL:(br + 1) * D_MODEL]))
        mixed = gb * ups[br] if mixed is None else mixed + gb * ups[br]
    x2 = x + _dot(mixed.astype(BF16), wout_ref[...])
    x2_ref[...] = x2
    h2 = _rms(x2, g2_ref[...])
    h2_ref[...] = _pack_bf16_pairs(h2)
    hi = h2.astype(BF16)
    lo = (h2 - hi.astype(F32)).astype(BF16)
    logits = _dot(hi, rwh_ref[...]) + _dot(lo, rwh_ref[...]) + _dot(hi, rwl_ref[...]) + rb_ref[...]
    lane = _lane_iota(logits.shape)
    lanef = lane.astype(F32)
    tops, idxs = [], []
    for _ in range(TOP_K):
        m = jnp.max(logits, axis=1, keepdims=True)
        idx = jnp.min(jnp.where(logits == m, lanef, 1e9), axis=1, keepdims=True)
        logits = jnp.where(lanef == idx, -jnp.inf, logits)
        tops.append(m)
        idxs.append(idx)
    es = [jnp.exp(tk - tops[0]) for tk in tops]
    den = es[0] + es[1] + es[2] + es[3]
    tm = logits.shape[0]
    onehot = jnp.zeros(logits.shape, F32)
    for k in range(TOP_K):
        onehot = jnp.where(lanef == idxs[k], 1.0, onehot)
    tri = jnp.where(_row_iota((tm, tm)) > _lane_iota((tm, tm)), 1.0, 0.0).astype(BF16)
    before = _dot(tri, onehot.astype(BF16)) + cnt_ref[...]
    cnt_ref[...] = cnt_ref[...] + jnp.sum(onehot, axis=0, keepdims=True)
    ei = jnp.zeros(logits.shape, F32)
    gt = jnp.zeros(logits.shape, F32)
    for k in range(TOP_K):
        rank = jnp.sum(jnp.where(lanef == idxs[k], before, 0.0), axis=1, keepdims=True)
        ei = jnp.where(lane == k, idxs[k], ei)
        ei = jnp.where(lane == TOP_K + k, rank, ei)
        gt = jnp.where(lane == k, es[k] / den, gt)
    ei_ref[...] = ei.astype(jnp.int32)
    gt_ref[...] = gt


def _layer_tail(x, o_pool, o_cmp, o_slc, o_win, o_mem, cnt0, w, tm):
    n = x.shape[0]
    row = lambda wd: pl.BlockSpec((tm, wd), lambda i: (i, 0))
    full = lambda a: pl.BlockSpec(a.shape, lambda i: (0,) * a.ndim)
    weights = (w["wgb"], w["wup_pool"], w["wup_nsa"], w["wup_mem"], w["wout"], w["g2"], w["rw_hi"], w["rw_lo"],
               w["rb"])
    return pl.pallas_call(
        _tail_kernel,
        grid=(n // tm,),
        in_specs=[row(D_MODEL), full(w["g1"]), row(POOL_WIDTH), row(NSA_WIDTH), row(NSA_WIDTH), row(NSA_WIDTH),
                  row(MEM_WIDTH), full(cnt0)] + [full(a) for a in weights],
        out_specs=[row(D_MODEL), row(D_MODEL // 2), row(LANES), row(LANES), full(cnt0)],
        out_shape=[jax.ShapeDtypeStruct((n, D_MODEL), F32), jax.ShapeDtypeStruct((n, D_MODEL // 2), jnp.uint32),
                   jax.ShapeDtypeStruct((n, LANES), jnp.int32), jax.ShapeDtypeStruct((n, LANES), F32),
                   jax.ShapeDtypeStruct((1, LANES), F32)],
        compiler_params=_cparams("arbitrary"),
        name="layer_tail",
    )(x, w["g1"], o_pool, o_cmp, o_slc, o_win, o_mem, cnt0, *weights)


def _ffn_kernel(be_ref, nu_ref, x_ref, wgu_ref, bgu_ref, wd_ref, bd_ref, o_ref, wgu_bf, wd_bf):
    i = pl.program_id(0)

    @pl.when((i == 0) | (be_ref[i] != be_ref[jnp.maximum(i - 1, 0)]))
    def _():
        wgu_bf[...] = wgu_ref[...].astype(BF16)
        wd_bf[...] = wd_ref[...].astype(BF16)

    @pl.when(i < nu_ref[0])
    def _():
        gu = _dot(_unpack_bf16_pairs(x_ref[...]).astype(BF16), wgu_bf[...]) + bgu_ref[...]
        gate = jnp.minimum(gu[:, :D_FF], SWIGLU_LIMIT)
        up = jnp.clip(gu[:, D_FF:], -SWIGLU_LIMIT, SWIGLU_LIMIT)
        act = gate * _sigmoid(SWIGLU_ALPHA * gate) * (up + 1.0)
        o_ref[...] = _dot(act.astype(BF16), wd_bf[...]) + bd_ref[...]

    @pl.when(i >= nu_ref[0])
    def _():
        o_ref[...] = jnp.zeros_like(o_ref)


def _expert_ffn(rows, blk_e, n_used, wgu, bgu, wd, bd):
    n_rows = rows.shape[0]
    n_blocks = n_rows // MOE_TILE
    blk = lambda i, be, nu: (jnp.minimum(i, nu[0] - 1), 0)
    grid_spec = pltpu.PrefetchScalarGridSpec(
        num_scalar_prefetch=2,
        grid=(n_blocks,),
        in_specs=[
            pl.BlockSpec((MOE_TILE, D_MODEL // 2), blk),
            pl.BlockSpec((None, D_MODEL, 2 * D_FF), lambda i, be, nu: (be[i], 0, 0)),
            pl.BlockSpec((None, 1, 2 * D_FF), lambda i, be, nu: (be[i], 0, 0)),
            pl.BlockSpec((None, D_FF, D_MODEL), lambda i, be, nu: (be[i], 0, 0)),
            pl.BlockSpec((None, 1, D_MODEL), lambda i, be, nu: (be[i], 0, 0)),
        ],
        out_specs=pl.BlockSpec((MOE_TILE, D_MODEL), lambda i, be, nu: (i, 0)),
        scratch_shapes=[pltpu.VMEM((D_MODEL, 2 * D_FF), BF16), pltpu.VMEM((D_FF, D_MODEL), BF16)],
    )
    return pl.pallas_call(
        _ffn_kernel,
        grid_spec=grid_spec,
        out_shape=jax.ShapeDtypeStruct((n_rows, D_MODEL), F32),
        compiler_params=pltpu.CompilerParams(dimension_semantics=("arbitrary",), vmem_limit_bytes=FFN_VMEM_LIMIT),
        name="expert_ffn",
    )(blk_e, n_used, rows, wgu, bgu, wd, bd)


def _token_tile(n):
    return next(t for t in (MOE_DMA_TOKENS, 384, 256, 128, n) if n % t == 0)


def _dispatch_scatter(dest_ref, h_ref, rows_ref, sem, td):
    def issue(i, carry):
        t0 = pl.multiple_of(i * 8, 8)
        for r in range(8):
            for k in range(TOP_K):
                d = dest_ref[i * (8 * TOP_K) + r * TOP_K + k]
                pltpu.make_async_copy(h_ref.at[pl.ds(t0 + r, 1)], rows_ref.at[pl.ds(d, 1)], sem).start()
        return carry

    lax.fori_loop(0, td // 8, issue, 0)
    for k in range(TOP_K):
        pltpu.make_async_copy(h_ref, rows_ref.at[pl.ds(0, td)], sem).wait()


def _dispatch_kernel(*refs, tiles, steps, n_blocks):
    ng = len(tiles)
    ends_ref, padded_ref, nu_ref = refs[:3]
    dest_refs = refs[3:3 + ng]
    h_refs = refs[3 + ng:3 + 2 * ng]
    rows_ref, zero_ref, sem, zsem = refs[3 + 2 * ng:]
    i = pl.program_id(0)

    @pl.when(i == 0)
    def _():
        zero_ref[...] = jnp.zeros_like(zero_ref)

        def tail_copy(e):
            start = pl.multiple_of(ends_ref[e] - MOE_TILE, MOE_TILE)
            return pltpu.make_async_copy(zero_ref, rows_ref.at[pl.ds(start, MOE_TILE)], zsem)

        def block_copy(blk):
            start = pl.multiple_of(blk * MOE_TILE, MOE_TILE)
            return pltpu.make_async_copy(zero_ref, rows_ref.at[pl.ds(start, MOE_TILE)], zsem)

        def each(start):
            def expert(e, carry):
                @pl.when(padded_ref[e] > 0)
                def _():
                    tail_copy(e).start() if start else tail_copy(e).wait()
                return carry

            def block(blk, carry):
                block_copy(blk).start() if start else block_copy(blk).wait()
                return carry

            lax.fori_loop(0, N_EXPERTS, expert, 0)
            lax.fori_loop(nu_ref[0], n_blocks, block, 0)

        each(True)
        each(False)

    first = 0
    for g in range(ng):
        @pl.when((i >= first) & (i < first + steps[g]))
        def _(g=g):
            _dispatch_scatter(dest_refs[g], h_refs[g], rows_ref, sem, tiles[g])
        first += steps[g]


def _moe_dispatch(h2s, dests, n_rows, pad_ends, padded, n_used):
    tiles = [_token_tile(h.shape[0]) for h in h2s]
    steps = [h.shape[0] // t for h, t in zip(h2s, tiles)]
    firsts = [sum(steps[:g]) for g in range(len(h2s))]

    def local(g):
        return lambda i, *_: jnp.clip(i - firsts[g], 0, steps[g] - 1)

    dest_specs = [pl.BlockSpec((tiles[g] * TOP_K,), lambda i, *_, f=local(g): (f(i),), memory_space=pltpu.SMEM)
                  for g in range(len(h2s))]
    width, dtype = h2s[0].shape[1], h2s[0].dtype
    tok_specs = [pl.BlockSpec((tiles[g], width), lambda i, *_, f=local(g): (f(i), 0)) for g in range(len(h2s))]
    grid_spec = pltpu.PrefetchScalarGridSpec(
        num_scalar_prefetch=3,
        grid=(sum(steps),),
        in_specs=dest_specs + tok_specs,
        out_specs=pl.BlockSpec(memory_space=pl.ANY),
        scratch_shapes=[pltpu.VMEM((MOE_TILE, width), dtype), pltpu.SemaphoreType.DMA(()),
                        pltpu.SemaphoreType.DMA(())],
    )
    return pl.pallas_call(
        functools.partial(_dispatch_kernel, tiles=tiles, steps=steps, n_blocks=n_rows // MOE_TILE),
        grid_spec=grid_spec,
        out_shape=jax.ShapeDtypeStruct((n_rows, width), dtype),
        compiler_params=_cparams("arbitrary"),
        name="moe_dispatch",
    )(pad_ends, padded, n_used, *dests, *h2s)


def _combine_kernel(dest_ref, x2_ref, g_ref, rows_ref, o_ref, ybuf, sem, *, td):
    def issue(i, carry):
        t0 = pl.multiple_of(i * 8, 8)
        for r in range(8):
            for k in range(TOP_K):
                d = dest_ref[i * (8 * TOP_K) + r * TOP_K + k]
                pltpu.make_async_copy(rows_ref.at[pl.ds(d, 1)], ybuf.at[k, pl.ds(t0 + r, 1)], sem).start()
        return carry

    lax.fori_loop(0, td // 8, issue, 0)
    for k in range(TOP_K):
        pltpu.make_async_copy(rows_ref.at[pl.ds(0, td)], ybuf.at[k], sem).wait()
    out = x2_ref[...]
    for k in range(TOP_K):
        out = out + g_ref[:, k:k + 1] * ybuf[k]
    o_ref[...] = out


def _moe_combine(x2, out_rows, dest, gates):
    n = x2.shape[0]
    td = _token_tile(n)
    return pl.pallas_call(
        functools.partial(_combine_kernel, td=td),
        grid=(n // td,),
        in_specs=[pl.BlockSpec((td * TOP_K,), lambda i: (i,), memory_space=pltpu.SMEM),
                  pl.BlockSpec((td, D_MODEL), lambda i: (i, 0)),
                  pl.BlockSpec((td, LANES), lambda i: (i, 0)),
                  pl.BlockSpec(memory_space=pl.ANY)],
        out_specs=pl.BlockSpec((td, D_MODEL), lambda i: (i, 0)),
        out_shape=jax.ShapeDtypeStruct((n, D_MODEL), F32),
        scratch_shapes=[pltpu.VMEM((TOP_K, td) + out_rows.shape[1:], out_rows.dtype), pltpu.SemaphoreType.DMA(())],
        compiler_params=_cparams("arbitrary"),
        name="moe_combine",
    )(dest, x2, gates, out_rows)


def _moe(groups, counts, w):
    n_total = sum(g[0].shape[0] for g in groups)
    cnt = counts[0, :N_EXPERTS].astype(jnp.int32)
    padded = (cnt + MOE_TILE - 1) // MOE_TILE * MOE_TILE
    pad_ends = jnp.cumsum(padded)
    pad_starts = pad_ends - padded
    n_blocks = -(-n_total * TOP_K // MOE_TILE) + N_EXPERTS
    blk_start = jnp.arange(n_blocks, dtype=jnp.int32) * MOE_TILE
    blk_e = jnp.minimum(jnp.sum(blk_start[:, None] >= pad_ends[None, :], axis=1), N_EXPERTS - 1).astype(jnp.int32)
    n_used = (pad_ends[-1] // MOE_TILE).astype(jnp.int32).reshape(1)
    dests = [(pad_starts[er[:, :TOP_K]] + er[:, TOP_K:2 * TOP_K]).astype(jnp.int32).reshape(-1)
             for (_, _, er, _) in groups]
    rows = _moe_dispatch([g[1] for g in groups], dests, n_blocks * MOE_TILE, pad_ends.astype(jnp.int32),
                         padded.astype(jnp.int32), n_used)
    out_rows = _expert_ffn(rows, blk_e, n_used, w["wgu"], w["bgu"], w["wd"], w["bd"])
    return [_moe_combine(x2, out_rows, dest, gates) for (x2, _, _, gates), dest in zip(groups, dests)]


def _rel_bucket(dist):
    n = jnp.maximum(dist, 0)
    max_exact = NUM_BUCKETS // 2
    nf = jnp.maximum(n, 1).astype(F32)
    large = max_exact + (jnp.log(nf / max_exact) / math.log(MAX_DISTANCE / max_exact)
                         * (NUM_BUCKETS - max_exact)).astype(jnp.int32)
    large = jnp.minimum(large, NUM_BUCKETS - 1)
    return jnp.where(n < max_exact, n, large)


def _bias_of(rel_bias, dist, valid):
    return jnp.where(valid[..., None], rel_bias[_rel_bucket(dist)], NEG)


def _proj_segs(kv_forms, kvc_forms):
    return ((0, 512, "qscale", ("rows",), BF16), (512, 256, "qscale", ("rows",), BF16),
            (768, 256, "id", kv_forms, F32), (1024, 256, "id", kv_forms, F32), (1280, 256, "id", kvc_forms, F32),
            (1536, 256, "id", ("rows",), F32), (1792, 128, "sigmoid", ("rows",), F32))


_PROJ_SEGS_PROMPT = _proj_segs(("key_rows", "t"), ("t",))
_PROJ_SEGS_SAMPLE = _proj_segs(("rows",), ("rows",))
_PROJ_NNORM = 1280


def _prep_layer(l, rel_bias, norm1_g, w_in, nsa_qk_norm, mem_qk_norm, cmp_w, cmp_pe, pool_w, pool_scale,
                mem_norm_g, w_mem_kv, w_up_pool, w_up_nsa, w_up_mem, w_out, norm2_g, router_w, router_b,
                w_gu, b_gu, w_down, b_down):
    wi = w_in[l]
    o_u, o_q, o_qm, o_kvc, o_kvs, o_kvw, o_gn, o_gb = 0, 256, 768, 1024, 1280, 1536, 1792, 1816
    w_proj = jnp.concatenate([
        wi[:, o_q:o_q + 512], wi[:, o_qm:o_qm + 256], wi[:, o_kvs:o_kvs + 256], wi[:, o_kvw:o_kvw + 256],
        wi[:, o_kvc:o_kvc + 256], wi[:, o_u:o_u + 256], wi[:, o_gn:o_gn + 24],
        jnp.zeros((D_MODEL, LANES - 24), F32)], axis=1).astype(BF16)
    nq, mq = nsa_qk_norm[l], mem_qk_norm[l]
    ones = jnp.ones((LANES,), F32)
    gain = jnp.concatenate([jnp.tile(nq[0], 8), jnp.tile(mq[0], 4), jnp.tile(nq[2], 2), ones,
                            jnp.tile(nq[3], 2), ones])[None, :]
    nmask = jnp.concatenate([jnp.ones((768,), F32), ones, 0 * ones, ones, 0 * ones])[None, :]
    eye4 = jnp.eye(4, dtype=F32)
    cw = cmp_w[l].reshape(2, 2, CMP_STRIDE, HEAD_DIM, HEAD_DIM)
    w_c = jnp.einsum("crjde,xy->cjxdrye", cw, jnp.eye(2, dtype=F32))
    w_c = w_c.reshape(2, CMP_STRIDE, LANES, KV_WIDTH).astype(BF16)
    pe = cmp_pe[l].reshape(2, 2, CMP_STRIDE, HEAD_DIM)
    pe_c = jnp.tile(pe.transpose(0, 2, 1, 3), (1, 1, 1, NSA_KV_HEADS))
    pe_c = jnp.pad(pe_c, ((0, 0), (0, 0), (0, 6), (0, 0)))
    w_pool = jnp.einsum("gde,gh->gdhe", pool_w[l], eye4).reshape(POOL_WIDTH, POOL_WIDTH).astype(BF16)
    rw = jnp.pad(router_w[l], ((0, 0), (0, LANES - N_EXPERTS)))
    rw_hi = rw.astype(BF16)
    rw_lo = (rw - rw_hi.astype(F32)).astype(BF16)
    rb = jnp.concatenate([router_b[l], jnp.full((LANES - N_EXPERTS,), NEG, F32)])[None, :]
    return {
        "g1": norm1_g[l][None, :], "w_proj": w_proj, "gain": gain, "nmask": nmask,
        "w_kvc_t": wi[:, o_kvc:o_kvc + 256].T.astype(BF16),
        "gk_cmp": jnp.tile(nq[1], 2)[None, :], "w_c": w_c, "pe_c": pe_c,
        "w_pool": w_pool, "pool_scale": pool_scale[l][None, :],
        "mem_g": mem_norm_g[l][None, :], "w_mem": w_mem_kv[l].astype(BF16),
        "mem_gain": jnp.concatenate([jnp.tile(mq[1], 4), jnp.ones((256,), F32)])[None, :],
        "mem_nmask": jnp.concatenate([jnp.ones((256,), F32), jnp.zeros((256,), F32)])[None, :],
        "wgb": wi[:, o_gb:o_gb + 3 * D_MODEL].astype(BF16),
        "wup_pool": w_up_pool[l].astype(BF16), "wup_nsa": w_up_nsa[l].astype(BF16),
        "wup_mem": w_up_mem[l].astype(BF16), "wout": w_out[l].astype(BF16), "g2": norm2_g[l][None, :],
        "rw_hi": rw_hi, "rw_lo": rw_lo, "rb": rb,
        "wgu": w_gu[l], "bgu": b_gu[l][:, None, :], "wd": w_down[l],
        "bd": b_down[l][:, None, :],
    }


def _project_in(x2d, w, segs, tm, seq):
    wt = w["w_kvc_t"] if segs is _PROJ_SEGS_PROMPT else None
    return _project(x2d, w["g1"], w["w_proj"], w["gain"], w["nmask"], segs, _PROJ_NNORM, tm, seq, wt)


def _rows_view(a_t, heads):
    b, _, rows = a_t.shape
    return a_t.reshape(b, 2, heads, HEAD_DIM, rows).transpose(0, 4, 1, 2, 3)


def _toeplitz(v, t):
    lead = v.shape[:-1]
    flat = jnp.tile(v, (1,) * len(lead) + (t,))[..., t:t + t * (2 * t - 1)]
    return flat.reshape(lead + (t, 2 * t - 1))[..., :t]


def _flash_tables(rel_bias):
    t = ATT_TILE
    d0 = jnp.arange(-t, t)
    kinds = jnp.stack([
        _bias_of(rel_bias, d0, d0 >= 0),
        _bias_of(rel_bias, d0 + t, d0 + t >= 0),
        _bias_of(rel_bias, jnp.full((2 * t,), 2 * t), jnp.ones((2 * t,), bool)),
        _bias_of(rel_bias, d0 + WINDOW, d0 + WINDOW < WINDOW),
    ])
    tab = _toeplitz(kinds.transpose(2, 0, 1), t)
    return tab.transpose(1, 2, 0, 3).reshape(4, t, NSA_HEADS * t)


def _cmp_bias_table(rel_bias, s, n_sub, n_cmp, pos0):
    na = s // CMP_STRIDE
    m = max(na, n_sub)
    k = jnp.arange(-m, m)[None, :]
    r = jnp.arange(CMP_STRIDE)[:, None]
    d = CMP_STRIDE * k + r - (CMP_BLOCK - 1) + pos0
    v = _bias_of(rel_bias, d, d >= 0).transpose(2, 0, 1)
    tz = _toeplitz(v, m)[:, :, :n_sub, :na]
    tab = tz.transpose(0, 3, 1, 2).reshape(NSA_HEADS, s, n_sub)
    return jnp.where(jnp.arange(n_sub)[None, None, :] < n_cmp, tab, NEG)


def _prompt_pre(x, mem, w, rel_bias, cnt0):
    b, s, _ = x.shape
    n = b * s
    tm = 512 if n % 512 == 0 else ATT_TILE
    x2d = x.reshape(n, D_MODEL)
    q, qm, ks, kvs_t, kw, kvw_t, kvc_t, u, gn = _project_in(x2d, w, _PROJ_SEGS_PROMPT, tm, s)
    r3 = lambda a: a.reshape(b, s, a.shape[-1])
    q, qm, ks, kw, u, gn = map(r3, (q, qm, ks, kw, u, gn))

    o_pool = _pool_mix(u, jnp.zeros((b, 16, POOL_WIDTH), F32), w["w_pool"], w["pool_scale"], 0)

    n_cmp = (s - CMP_BLOCK) // CMP_STRIDE + 1
    n_slc = -(-s // SLC_BLOCK)
    n_lb = s // LANES
    own = jnp.broadcast_to(jnp.arange(b, dtype=jnp.int32)[:, None], (b, n_lb))
    blocks = jnp.broadcast_to(jnp.arange(n_lb, dtype=jnp.int32)[None, :], (b, n_lb))
    parts = _cmp_partials_paged(kvc_t, own, blocks, w["pe_c"], w["w_c"], math.gcd(n_lb, 16))
    n_sub = parts.shape[1]
    bias_c = _cmp_bias_table(rel_bias, s, n_sub, n_cmp, 0)
    tq = 256 if s % 256 == 0 else ATT_TILE
    o_cmp, ns0, ns1 = _cmp_attention(q, parts, bias_c, gn, w["gk_cmp"], tq=tq, n_cmp=n_cmp, n_slc=n_slc, pos0=0)

    tab = _flash_tables(rel_bias)
    o_slc = _flash_attention(q, ns0, ns1, ks, kvs_t, tab, gn, use_sel=True, band=None, gate_base=NSA_HEADS)
    o_win = _flash_attention(q, ns0, ns1, kw, kvw_t, tab, gn, use_sel=False, band=WINDOW // ATT_TILE,
                             gate_base=2 * NSA_HEADS)

    m = mem.shape[1]
    mem_kv, mem_kv_t = _project(mem.reshape(b * m, D_MODEL), w["mem_g"], w["w_mem"], w["mem_gain"],
                                w["mem_nmask"], ((0, 2 * MEM_WIDTH, "id", ("rows", "t"), F32),), MXU_DIM,
                                tm=math.gcd(m, 512), seq=m)
    o_mem = _mem_attention(qm, mem_kv.reshape(b, m, 2 * MEM_WIDTH), tq=min(512, s))

    f2 = lambda a: a.reshape(n, a.shape[-1])
    x2, h2, eidx, gates, cnt = _layer_tail(x2d, f2(o_pool), f2(o_cmp), f2(o_slc), f2(o_win), f2(o_mem), cnt0, w, tm)
    win_t = kvw_t[:, :, max(0, s - WINDOW):]
    if s < WINDOW:
        win_t = jnp.pad(win_t, ((0, 0), (0, 0), (WINDOW - s, 0)))
    states = (_rows_view(kvc_t, NSA_KV_HEADS), _rows_view(kvs_t, NSA_KV_HEADS), _rows_view(win_t, NSA_KV_HEADS),
              _rows_view(mem_kv_t, MEM_HEADS), _last_rows(u, POOL_BUF))
    return (x2, h2, eidx, gates), states, cnt


def _last_rows(a, n):
    t = a.shape[1]
    if t < n:
        a = jnp.pad(a, [(0, 0), (n - t, 0)] + [(0, 0)] * (a.ndim - 2))
    return a[:, a.shape[1] - n:]


def _dec_columns_nsa(q):
    b, t, _ = q.shape
    qh = q.reshape(b, t, NSA_KV_HEADS, NSA_HPG, HEAD_DIM)
    w = jnp.einsum("btgpd,gx->bxdgtp", qh.astype(F32), jnp.eye(NSA_KV_HEADS, dtype=F32))
    w = w.reshape(b, NSA_KV_HEADS * HEAD_DIM, NSA_KV_HEADS * t * NSA_HPG)
    return jnp.pad(w, ((0, 0), (0, KV_WIDTH - w.shape[1]), (0, LANES - w.shape[2]))).astype(BF16)


def _dec_extract_nsa(o, t):
    b = o.shape[0]
    v = o[:, LANES:, :NSA_KV_HEADS * t * NSA_HPG]
    v = v.reshape(b, NSA_KV_HEADS, HEAD_DIM, NSA_KV_HEADS, t, NSA_HPG)
    v = jnp.einsum("bxdgtp,gx->btgpd", v, jnp.eye(NSA_KV_HEADS, dtype=F32))
    return v.reshape(b, t, NSA_WIDTH)


def _dec_bias_cols(bias_tph):
    k, t, _ = bias_tph.shape
    bt = bias_tph.reshape(k, t, NSA_KV_HEADS, NSA_HPG).transpose(0, 2, 1, 3).reshape(k, NSA_KV_HEADS * t * NSA_HPG)
    return jnp.pad(bt, ((0, 0), (0, LANES - bt.shape[1])))


def _dec_gate_cols(gn, base, t):
    b = gn.shape[0]
    gt = gn[:, :, base:base + NSA_HEADS].reshape(b, t, NSA_KV_HEADS, NSA_HPG).transpose(0, 2, 1, 3)
    gt = gt.reshape(b, 1, NSA_KV_HEADS * t * NSA_HPG)
    return jnp.pad(gt, ((0, 0), (0, 0), (0, LANES - gt.shape[2])), constant_values=1.0)


def _sample_pre(x, cache_cmp, cache_slc, cache_win, cache_mem, pool_buf, page_table, w, rel_bias, cnt0):
    b, t, _ = x.shape
    n = b * t
    page = cache_cmp.shape[1]
    n_pages = page_table.shape[1]
    past = n_pages * page
    x2d = x.reshape(n, D_MODEL)
    tm = n if n <= 512 else LANES
    q, qm, kvs, kvw, kvc, u, gn = _project_in(x2d, w, _PROJ_SEGS_SAMPLE, tm, tm)
    r3 = lambda a: a.reshape(b, t, a.shape[-1])
    q, qm, kvs, kvw, kvc, u, gn = map(r3, (q, qm, kvs, kvw, kvc, u, gn))
    qpos = past + jnp.arange(t)

    buf16 = jnp.pad(pool_buf, ((0, 0), (16 - POOL_BUF, 0), (0, 0)))
    o_pool = _pool_mix(u, buf16, w["w_pool"], w["pool_scale"], past)

    total = past + t
    n_cmp = (total - CMP_BLOCK) // CMP_STRIDE + 1
    n_sub_used = n_cmp + CMP_BLOCK // CMP_STRIDE - 1
    n_slc = -(-total // SLC_BLOCK)
    pps = math.gcd(n_pages, 16)
    parts = _cmp_partials_paged(_feature_major(cache_cmp), page_table, jnp.zeros_like(page_table), w["pe_c"],
                                w["w_c"], pps)
    extra = n_sub_used * CMP_STRIDE - past
    if extra > 0:
        tail_rows = -(-extra // CMP_STRIDE) * CMP_STRIDE
        new_c = jnp.pad(kvc, ((0, 0), (0, max(0, tail_rows - t)), (0, 0)))[:, :tail_rows]
        parts = jnp.concatenate([parts, _cmp_partials_dense(new_c, w["pe_c"], w["w_c"])], axis=1)
    n_sub = parts.shape[1]
    end = jnp.arange(n_sub)[None, :] * CMP_STRIDE + CMP_BLOCK - 1
    bias_c = _bias_of(rel_bias, qpos[:, None] - end, (end <= qpos[:, None]) & (jnp.arange(n_sub)[None, :] < n_cmp))
    qpad = ((0, 0), (0, LANES - t), (0, 0))
    bias_c = jnp.pad(bias_c.transpose(2, 0, 1), qpad)
    o_cmp, ns0, ns1 = _cmp_attention(jnp.pad(q, qpad), parts, bias_c, jnp.pad(gn, qpad), w["gk_cmp"], tq=LANES,
                                     n_cmp=n_cmp, n_slc=n_slc, pos0=past)
    o_cmp = o_cmp[:, :t]

    wq = _dec_columns_nsa(q).transpose(0, 2, 1)
    ncol = NSA_KV_HEADS * t * NSA_HPG
    new_tile = lambda kv: jnp.pad(kv, ((0, 0), (0, LANES - t), (0, 0))).transpose(0, 2, 1)
    own = lambda npg: jnp.broadcast_to(jnp.arange(b, dtype=jnp.int32)[:, None], (b, npg))
    blocks = lambda npg: jnp.broadcast_to(jnp.arange(npg, dtype=jnp.int32)[None, :], (b, npg))

    n_chunks = n_pages + 1
    nblk = -(-2 * n_chunks // 8) * 8
    ns = jnp.stack([ns0, ns1], axis=1)[:, :, :, :t].transpose(0, 1, 3, 2)
    ns = jnp.pad(ns, ((0, 0), (0, 0), (0, 0), (0, max(0, nblk - ns.shape[3]))))[..., :nblk]
    ns = jnp.broadcast_to(ns[:, :, :, None, :], (b, NSA_KV_HEADS, t, NSA_HPG, nblk)).reshape(b, ncol, nblk)
    notsel = jnp.pad(ns.transpose(0, 2, 1), ((0, 0), (0, 0), (0, LANES - ncol)))
    rows = jnp.arange(LANES)
    far = _bias_of(rel_bias, jnp.full((LANES, t), 2 * MAX_DISTANCE), jnp.ones((LANES, t), bool))
    kpos_last = past - LANES + rows
    d_last = qpos[None, :] - kpos_last[:, None]
    near = _bias_of(rel_bias, d_last, d_last >= 0)
    kpos_new = past + rows
    d_new = qpos[None, :] - kpos_new[:, None]
    newb = _bias_of(rel_bias, d_new, (d_new >= 0) & (rows[:, None] < t))
    bias_tab = jnp.stack([_dec_bias_cols(far), _dec_bias_cols(near), _dec_bias_cols(newb)])
    bias_idx = jnp.concatenate([jnp.zeros((n_pages - 1,), jnp.int32), jnp.array([1, 2], jnp.int32)])
    o_slc = _decode_attention(_feature_major(cache_slc), page_table, jnp.zeros_like(page_table), new_tile(kvs), wq,
                              bias_tab, bias_idx, notsel, _dec_gate_cols(gn, NSA_HEADS, t))
    o_slc = _dec_extract_nsa(o_slc, t)

    wb = cache_win.shape[1]
    n_wpg = wb // LANES
    kpos_w = past - wb + jnp.arange(wb + LANES)
    d_w = qpos[None, :] - kpos_w[:, None]
    valid_w = (d_w >= 0) & (d_w < WINDOW) & (kpos_w[:, None] >= 0) & (jnp.arange(wb + LANES)[:, None] < wb + t)
    bias_w = _dec_bias_cols(_bias_of(rel_bias, d_w, valid_w)).reshape(n_wpg + 1, LANES, LANES)
    zeros_ns = jnp.zeros((b, -(-2 * (n_wpg + 1) // 8) * 8, LANES), F32)
    o_win = _decode_attention(_feature_major(cache_win), own(n_wpg), blocks(n_wpg), new_tile(kvw), wq, bias_w,
                              jnp.arange(n_wpg + 1, dtype=jnp.int32), zeros_ns, _dec_gate_cols(gn, 2 * NSA_HEADS, t))
    o_win = _dec_extract_nsa(o_win, t)

    m = cache_mem.shape[1]
    n_mpg = m // LANES
    qmh = qm.reshape(b, t, MEM_HEADS, HEAD_DIM).astype(F32)
    wqm = jnp.einsum("bthd,hx->bxdht", qmh, jnp.eye(MEM_HEADS, dtype=F32))
    wqm = wqm.reshape(b, MEM_WIDTH, MEM_HEADS * t)
    wqm = jnp.pad(wqm, ((0, 0), (0, MEM_WIDTH), (0, LANES - MEM_HEADS * t))).astype(BF16).transpose(0, 2, 1)
    o_mem = _decode_attention(_feature_major(cache_mem), own(n_mpg), blocks(n_mpg), None, wqm,
                              jnp.zeros((1, LANES, LANES), F32), jnp.zeros((n_mpg,), jnp.int32),
                              jnp.zeros((b, 8, LANES), F32), jnp.ones((b, 1, LANES), F32))
    om = o_mem[:, MEM_WIDTH:, :MEM_HEADS * t].reshape(b, MEM_HEADS, HEAD_DIM, MEM_HEADS, t)
    o_mem = jnp.einsum("bxdht,hx->bthd", om, jnp.eye(MEM_HEADS, dtype=F32)).reshape(b, t, MEM_WIDTH)

    f2 = lambda a: a.reshape(n, a.shape[-1])
    x2, h2, eidx, gates, cnt = _layer_tail(x2d, f2(o_pool), f2(o_cmp), f2(o_slc), f2(o_win), f2(o_mem), cnt0, w,
                                           n if n <= 512 else LANES)
    kvshape = (b, t, 2, NSA_KV_HEADS, HEAD_DIM)
    new_win = _rows_view(jnp.concatenate([_feature_major(cache_win)[:, :, t:], kvw.transpose(0, 2, 1)], axis=2),
                         NSA_KV_HEADS)
    new_pool = jnp.concatenate([pool_buf, u], axis=1)[:, t:]
    states = (kvc.reshape(kvshape), kvs.reshape(kvshape), new_win, new_pool)
    return (x2, h2, eidx, gates), states, cnt


def kernel(x_prompt, x_sample, cache_cmp_kv, cache_slc_kv, cache_win_kv, cache_mem_kv, state_pool, page_table,
           mem_prompt, rel_bias, norm1_g, w_in, nsa_qk_norm, mem_qk_norm, cmp_w, cmp_pe, pool_w, pool_scale,
           mem_norm_g, w_mem_kv, w_up_pool, w_up_nsa, w_up_mem, w_out, norm2_g, router_w, router_b, w_gu, b_gu,
           w_down, b_down):
    depth = w_in.shape[0]
    yp, ys = x_prompt, x_sample
    bp, sp, _ = x_prompt.shape
    bs, ts, _ = x_sample.shape
    outs_p = [[] for _ in range(5)]
    outs_s = [[] for _ in range(4)]
    for l in range(depth):
        w = _prep_layer(l, rel_bias, norm1_g, w_in, nsa_qk_norm, mem_qk_norm, cmp_w, cmp_pe, pool_w, pool_scale,
                        mem_norm_g, w_mem_kv, w_up_pool, w_up_nsa, w_up_mem, w_out, norm2_g, router_w, router_b,
                        w_gu, b_gu, w_down, b_down)
        pre_p, st_p, cnt = _prompt_pre(yp, mem_prompt, w, rel_bias, jnp.zeros((1, LANES), F32))
        pre_s, st_s, cnt = _sample_pre(ys, cache_cmp_kv[l], cache_slc_kv[l], cache_win_kv[l], cache_mem_kv[l],
                                       state_pool[l], page_table, w, rel_bias, cnt)
        yp, ys = _moe([pre_p, pre_s], cnt, w)
        yp = yp.reshape(bp, sp, D_MODEL)
        ys = ys.reshape(bs, ts, D_MODEL)
        for lst, a in zip(outs_p, st_p):
            lst.append(a)
        for lst, a in zip(outs_s, st_s):
            lst.append(a)
    new_cmp_p, new_slc_p, new_win_p, new_mem_p, new_pool_p = [jnp.stack(a) for a in outs_p]
    new_cmp_s, new_slc_s, new_win_s, new_pool_s = [jnp.stack(a) for a in outs_s]
    new_win_s = new_win_s.reshape(new_win_s.shape[:3] + (2, NSA_KV_HEADS, HEAD_DIM))
    return (yp, ys, new_cmp_p, new_slc_p, new_win_p, new_mem_p, new_pool_p,
            new_cmp_s, new_slc_s, new_win_s, new_pool_s)
```

```python
import functools
import math

import jax
import jax.numpy as jnp
from jax import lax
from jax.experimental import pallas as pl
from jax.experimental.pallas import tpu as pltpu

F32 = jnp.float32
BF16 = jnp.bfloat16

D_MODEL = 1024
HEAD_DIM = 64
POOL_WINDOWS = (2, 4, 8, 16)
POOL_GROUP = 64
POOL_WIDTH = 256
POOL_BUF = 15
NSA_HEADS = 8
NSA_KV_HEADS = 2
NSA_HPG = 4
NSA_WIDTH = 512
KV_WIDTH = 256
CMP_BLOCK = 32
CMP_STRIDE = 16
SLC_BLOCK = 64
SLC_TOPK = 16
WINDOW = 512
MEM_HEADS = 4
MEM_WIDTH = 256
NUM_BUCKETS = 32
MAX_DISTANCE = 128
N_EXPERTS = 32
TOP_K = 4
D_FF = 1024
SWIGLU_ALPHA = 1.702
SWIGLU_LIMIT = 7.0
EPS = 1e-6
SCALE = HEAD_DIM ** -0.5

LANES = 128
MXU_DIM = 256
NEG = -1e30
ATT_TILE = 256
DEC_PAGES_PER_STEP = 32
MOE_TILE = 512
MOE_DMA_TOKENS = 512
VMEM_LIMIT = 48 * 1024 * 1024
FFN_VMEM_LIMIT = 56 * 1024 * 1024


def _cparams(*sem):
    return pltpu.CompilerParams(dimension_semantics=sem, vmem_limit_bytes=VMEM_LIMIT)


def _dot(a, b):
    return jnp.dot(a, b, preferred_element_type=F32)


def _dot_nt(a, b):
    return lax.dot_general(a, b, (((1,), (1,)), ((), ())), preferred_element_type=F32)


def _split_dot(a, b):
    hi = a.astype(BF16)
    lo = (a - hi.astype(F32)).astype(BF16)
    return _dot(hi, b) + _dot(lo, b)


def _rms(x, g):
    r = lax.rsqrt(jnp.mean(x * x, axis=-1, keepdims=True) + EPS)
    return (x * r) * g


def _sigmoid(x):
    return 1.0 / (1.0 + jnp.exp(-x))


def _pack_bf16_pairs(x):
    w = x.shape[1] // 2
    bits = lax.bitcast_convert_type(x.astype(BF16).astype(F32), jnp.uint32)
    return jnp.right_shift(bits[:, :w], jnp.uint32(16)) | (bits[:, w:] & jnp.uint32(0xFFFF0000))


def _unpack_bf16_pairs(p):
    lo = lax.bitcast_convert_type(jnp.left_shift(p, jnp.uint32(16)), F32)
    hi = lax.bitcast_convert_type(p & jnp.uint32(0xFFFF0000), F32)
    return jnp.concatenate([lo, hi], axis=1)


def _lane_iota(shape):
    return lax.broadcasted_iota(jnp.int32, shape, len(shape) - 1)


def _row_iota(shape):
    return lax.broadcasted_iota(jnp.int32, shape, len(shape) - 2)


def _proj_kernel(x_ref, g_ref, w_ref, gain_ref, nmask_ref, seg_ref, *rest, segs, n_norm, has_wt):
    wt_ref = rest[0] if has_wt else None
    out_refs = rest[1:] if has_wt else rest
    h = _rms(x_ref[...], g_ref[...]).astype(BF16)
    seg = seg_ref[...]
    outs = iter(out_refs)
    for (start, width, kind, forms, _) in segs:
        if forms == ("t",) and has_wt:
            next(outs)[...] = _dot_nt(wt_ref[...], h)
            continue
        z = _dot(h, w_ref[:, start:start + width])
        if start < n_norm:
            pieces = []
            for c in range(0, width, MXU_DIM):
                zc = z[:, c:c + MXU_DIM]
                ms = _split_dot(zc * zc, seg)
                zn = (zc * lax.rsqrt(ms + EPS)) * gain_ref[:, start + c:start + c + MXU_DIM]
                pieces.append(jnp.where(nmask_ref[:, start + c:start + c + MXU_DIM] > 0, zn, zc))
            z = pieces[0] if len(pieces) == 1 else jnp.concatenate(pieces, axis=1)
        if kind == "sigmoid":
            z = _sigmoid(z)
        elif kind == "qscale":
            z = z * SCALE
        for form in forms:
            o_ref = next(outs)
            if form == "rows":
                o_ref[...] = z.astype(o_ref.dtype)
            elif form == "key_rows":
                o_ref[...] = z[:, 0:LANES]
            else:
                o_ref[...] = z.T


def _project(x, g, w, gain, nmask, segs, n_norm, tm, seq, wt=None):
    n = x.shape[0]
    ncol = w.shape[1]
    seg = _seg_matrix(MXU_DIM)
    full = lambda i: (0, 0)
    tpb = seq // tm
    out_specs, out_shape = [], []
    for (_, wd, _, forms, dt) in segs:
        for form in forms:
            if form == "rows":
                out_specs.append(pl.BlockSpec((tm, wd), lambda i: (i, 0)))
                out_shape.append(jax.ShapeDtypeStruct((n, wd), dt))
            elif form == "key_rows":
                out_specs.append(pl.BlockSpec((tm, LANES), lambda i: (i, 0)))
                out_shape.append(jax.ShapeDtypeStruct((n, LANES), F32))
            else:
                out_specs.append(pl.BlockSpec((None, wd, tm), lambda i: (i // tpb, 0, i % tpb)))
                out_shape.append(jax.ShapeDtypeStruct((n // seq, wd, seq), F32))
    extra = () if wt is None else (wt,)
    return pl.pallas_call(
        functools.partial(_proj_kernel, segs=segs, n_norm=n_norm, has_wt=wt is not None),
        grid=(n // tm,),
        in_specs=[
            pl.BlockSpec((tm, D_MODEL), lambda i: (i, 0)),
            pl.BlockSpec((1, D_MODEL), full),
            pl.BlockSpec((D_MODEL, ncol), full),
            pl.BlockSpec((1, gain.shape[1]), full),
            pl.BlockSpec((1, nmask.shape[1]), full),
            pl.BlockSpec((MXU_DIM, MXU_DIM), full),
        ] + [pl.BlockSpec(a.shape, full) for a in extra],
        out_specs=out_specs,
        out_shape=out_shape,
        compiler_params=_cparams("parallel"),
        name="proj",
    )(x, g, w, gain, nmask, seg, *extra)


def _seg_matrix(n):
    i = jnp.arange(n) // HEAD_DIM
    return jnp.where(i[:, None] == i[None, :], 1.0 / HEAD_DIM, 0.0).astype(BF16)


def _pool_kernel(u_ref, buf_ref, w_ref, scale_ref, o_ref, zs_ref, *, t, pos0):
    zs_ref[0:16, :] = buf_ref[...]
    zs_ref[16:16 + t, :] = u_ref[...]
    u = u_ref[...]
    lane = _lane_iota((1, POOL_WIDTH))
    pos = (pos0 + _row_iota((t, 1))).astype(F32)
    acc = u
    mean = None
    for i in range(1, max(POOL_WINDOWS)):
        acc = acc + zs_ref[16 - i:16 - i + t, :]
        if i + 1 in POOL_WINDOWS:
            gi = POOL_WINDOWS.index(i + 1)
            m = acc / jnp.minimum(pos + 1.0, float(i + 1))
            mean = m if mean is None else jnp.where(lane >= gi * POOL_GROUP, m, mean)
    d = (mean - u).astype(BF16)
    o_ref[...] = _dot(d, w_ref[...]) * scale_ref[...]


def _pool_mix(u, buf16, w_bd, scale, pos0):
    b, t, _ = u.shape
    return pl.pallas_call(
        functools.partial(_pool_kernel, t=t, pos0=pos0),
        grid=(b,),
        in_specs=[
            pl.BlockSpec((None, t, POOL_WIDTH), lambda i: (i, 0, 0)),
            pl.BlockSpec((None, 16, POOL_WIDTH), lambda i: (i, 0, 0)),
            pl.BlockSpec((POOL_WIDTH, POOL_WIDTH), lambda i: (0, 0)),
            pl.BlockSpec((1, POOL_WIDTH), lambda i: (0, 0)),
        ],
        out_specs=pl.BlockSpec((None, t, POOL_WIDTH), lambda i: (i, 0, 0)),
        out_shape=jax.ShapeDtypeStruct((b, t, POOL_WIDTH), F32),
        scratch_shapes=[pltpu.VMEM((t + 16, POOL_WIDTH), F32)],
        compiler_params=_cparams("parallel"),
        name="pool",
    )(u, buf16, w_bd, scale)


def _cpart_compute(rows_of, pe_ref, w_ref, o_ref, m):
    for c in range(2):
        acc = jnp.zeros((m + 8, KV_WIDTH), F32)
        for j in range(CMP_STRIDE):
            lhs = jnp.concatenate([rows_of(c, j), pe_ref[c, j]], axis=0)
            acc = acc + _dot(lhs.astype(BF16), w_ref[c, j])
        lane = _lane_iota((1, KV_WIDTH))
        pe_term = jnp.where(lane < LANES, acc[m:m + 1], acc[m + 1:m + 2])
        o_ref[:, c * KV_WIDTH:(c + 1) * KV_WIDTH] = acc[0:m] + pe_term


def _cpart_kernel(xk_ref, xv_ref, pe_ref, w_ref, o_ref, *, rows):
    m = rows // CMP_STRIDE
    x_refs = (xk_ref, xv_ref)
    _cpart_compute(lambda c, j: x_refs[c][pl.ds(j, m, stride=CMP_STRIDE), :], pe_ref, w_ref, o_ref, m)


def _cpart_paged_kernel(*refs, nop, page):
    page_refs = refs[2:2 + nop]
    perm_ref, pe_ref, w_ref, o_ref, xs_ref = refs[2 + nop:]
    n = page // CMP_STRIDE
    for k, r in enumerate(page_refs):
        y = _dot_nt(perm_ref[...], r[...].astype(BF16))
        for j in range(CMP_STRIDE):
            for c in range(2):
                xs_ref[c, j, k * n:(k + 1) * n, :] = y[j * n:(j + 1) * n, c * LANES:(c + 1) * LANES]
    _cpart_compute(lambda c, j: xs_ref[c, j], pe_ref, w_ref, o_ref, nop * n)


def _cmp_partials_dense(kv, pe, w_c):
    b, t, _ = kv.shape
    rows = (t // CMP_STRIDE) * CMP_STRIDE
    n = rows // CMP_STRIDE
    return pl.pallas_call(
        functools.partial(_cpart_kernel, rows=rows),
        grid=(b,),
        in_specs=[
            pl.BlockSpec((None, rows, LANES), lambda i: (i, 0, 0)),
            pl.BlockSpec((None, rows, LANES), lambda i: (i, 0, 1)),
            pl.BlockSpec(pe.shape, lambda i: (0, 0, 0, 0)),
            pl.BlockSpec(w_c.shape, lambda i: (0, 0, 0, 0)),
        ],
        out_specs=pl.BlockSpec((None, n, 2 * KV_WIDTH), lambda i: (i, 0, 0)),
        out_shape=jax.ShapeDtypeStruct((b, n, 2 * KV_WIDTH), F32),
        compiler_params=_cparams("parallel"),
        name="cmp_partials",
    )(kv, kv, pe, w_c)


def _cmp_partials_paged(pool_t, phys, lblk, pe, w_c, pages_per_step):
    b, n_pages = phys.shape
    page = LANES
    nop = pages_per_step
    n = nop * page // CMP_STRIDE

    def page_spec(k):
        return pl.BlockSpec((None, KV_WIDTH, page),
                            lambda i, c, ph, lb: (ph[i, c * nop + k], 0, lb[i, c * nop + k]))

    r_idx = jnp.arange(page)
    regroup = (r_idx[None, :] == ((r_idx % (page // CMP_STRIDE)) * CMP_STRIDE + r_idx // (page // CMP_STRIDE))[:, None])
    regroup = regroup.astype(BF16)
    grid_spec = pltpu.PrefetchScalarGridSpec(
        num_scalar_prefetch=2,
        grid=(b, n_pages // nop),
        in_specs=[page_spec(k) for k in range(nop)] + [
            pl.BlockSpec(regroup.shape, lambda i, c, ph, lb: (0, 0)),
            pl.BlockSpec(pe.shape, lambda i, c, ph, lb: (0, 0, 0, 0)),
            pl.BlockSpec(w_c.shape, lambda i, c, ph, lb: (0, 0, 0, 0)),
        ],
        out_specs=pl.BlockSpec((None, n, 2 * KV_WIDTH), lambda i, c, ph, lb: (i, c, 0)),
        scratch_shapes=[pltpu.VMEM((2, CMP_STRIDE, n, LANES), F32)],
    )
    return pl.pallas_call(
        functools.partial(_cpart_paged_kernel, nop=nop, page=page),
        grid_spec=grid_spec,
        out_shape=jax.ShapeDtypeStruct((b, n_pages * page // CMP_STRIDE, 2 * KV_WIDTH), F32),
        compiler_params=_cparams("parallel", "arbitrary"),
        name="cmp_partials_paged",
    )(phys, lblk, *([pool_t] * nop), regroup, pe, w_c)


def _group_query_columns(q_ref, g, t):
    zeros64 = jnp.zeros((HEAD_DIM, t), BF16)
    cols = []
    for pr in range(2):
        qt = q_ref[:, (2 * g + pr) * LANES:(2 * g + pr + 1) * LANES].astype(F32).T.astype(BF16)
        for half in range(2):
            qh = qt[half * HEAD_DIM:(half + 1) * HEAD_DIM]
            cols.append(jnp.concatenate([qh, zeros64] if g == 0 else [zeros64, qh], axis=0))
    return jnp.concatenate(cols, axis=1)


def _store_group_output(o_ref, out_t, g, t):
    for pr in range(2):
        pair = jnp.concatenate([out_t[:, (2 * pr) * t:(2 * pr + 1) * t], out_t[:, (2 * pr + 1) * t:(2 * pr + 2) * t]],
                               axis=0)
        o_ref[:, (2 * g + pr) * LANES:(2 * g + pr + 1) * LANES] = pair.T


def _cattn_kernel(q_ref, p_ref, bias_ref, gn_ref, gk_ref, seg_ref, o_ref, ns0_ref, ns1_ref, *,
                  tq, n_sub, n_cmp, n_slc, nslp, pos0):
    qi = pl.program_id(0)
    pall = p_ref[...]
    kraw = pall[:, 0:LANES] + pltpu.roll(pall[:, LANES:2 * LANES], n_sub - 1, 0)
    vc = pall[:, 2 * LANES:3 * LANES] + pltpu.roll(pall[:, 3 * LANES:4 * LANES], n_sub - 1, 0)
    ms = _split_dot(kraw * kraw, seg_ref[...])
    kc = ((kraw * lax.rsqrt(ms + EPS)) * gk_ref[...]).astype(BF16)
    vct = vc.T.astype(BF16)

    nsel = -(-n_slc // 8) * 8
    jj = _row_iota((nsel, n_sub))
    nn = _lane_iota((nsel, n_sub))
    covers_t = ((nn * CMP_STRIDE < (jj + 1) * SLC_BLOCK) & (nn * CMP_STRIDE + CMP_BLOCK - 1 >= jj * SLC_BLOCK)
                & (nn < n_cmp) & (jj < n_slc))
    covers_t = jnp.where(covers_t, 1.0, 0.0).astype(BF16)
    qpos = pos0 + qi * tq + _lane_iota((1, tq))
    qblk = jnp.right_shift(qpos, SLC_BLOCK.bit_length() - 1)
    jr = _row_iota((nsel, tq))
    jrf = jr.astype(F32)
    forced = (jr == 0) | (jr == qblk) | (jr == qblk - 1)
    causal = jr <= qblk
    gnt = gn_ref[...].T

    for g in range(NSA_KV_HEADS):
        bias = bias_ref[g]
        s = _dot(kc, _group_query_columns(q_ref, g, tq)) + bias
        m = jnp.max(s, axis=0, keepdims=True)
        m = jnp.where(m > 0.5 * NEG, m, 0.0)
        e = jnp.where(bias > 0.5 * NEG, jnp.exp(s - m), 0.0)
        p = e / jnp.maximum(jnp.sum(e, axis=0, keepdims=True), 1e-30)
        h0 = NSA_HPG * g
        gate = jnp.concatenate([gnt[h0 + c:h0 + c + 1] for c in range(NSA_HPG)], axis=1)
        out_t = _dot(vct[g * HEAD_DIM:(g + 1) * HEAD_DIM], p.astype(BF16)) * gate
        _store_group_output(o_ref, out_t, g, tq)

        prsum = p[:, 0:tq] + p[:, tq:2 * tq] + p[:, 2 * tq:3 * tq] + p[:, 3 * tq:4 * tq]
        hi = prsum.astype(BF16)
        lo = (prsum - hi.astype(F32)).astype(BF16)
        imp = _dot(covers_t, hi) + _dot(covers_t, lo)
        score = jnp.where(forced, jnp.inf, imp)
        score = jnp.where(causal, score, -jnp.inf)
        sel = jnp.zeros((nsel, tq), F32)
        for _ in range(min(SLC_TOPK, n_slc)):
            m = jnp.max(score, axis=0, keepdims=True)
            idx = jnp.min(jnp.where(score == m, jrf, 1e9), axis=0, keepdims=True)
            pick = jrf == idx
            sel = jnp.where(pick & (m > -jnp.inf), 1.0, sel)
            score = jnp.where(pick, -jnp.inf, score)
        ns = 1.0 - sel
        if nslp > nsel:
            ns = jnp.concatenate([ns, jnp.ones((nslp - nsel, tq), F32)], axis=0)
        (ns0_ref if g == 0 else ns1_ref)[...] = ns


def _cmp_bias_cols(bias, tq):
    _, s, n_sub = bias.shape
    bias_t = bias.reshape(NSA_KV_HEADS, NSA_HPG, s // tq, tq, n_sub).transpose(0, 4, 2, 1, 3)
    return bias_t.reshape(NSA_KV_HEADS, n_sub, NSA_HPG * s)


def _cmp_attention(q, parts, bias_t, gn, gk, *, tq, n_cmp, n_slc, pos0):
    b, s, _ = q.shape
    n_sub = parts.shape[1]
    nslp = -(-n_slc // LANES) * LANES
    kern = functools.partial(_cattn_kernel, tq=tq, n_sub=n_sub, n_cmp=n_cmp, n_slc=n_slc, nslp=nslp, pos0=pos0)
    return pl.pallas_call(
        kern,
        grid=(s // tq, b),
        in_specs=[
            pl.BlockSpec((None, tq, NSA_WIDTH), lambda i, j: (j, i, 0)),
            pl.BlockSpec((None, n_sub, 2 * KV_WIDTH), lambda i, j: (j, 0, 0)),
            pl.BlockSpec((NSA_KV_HEADS, n_sub, NSA_HPG * tq), lambda i, j: (0, 0, i)),
            pl.BlockSpec((None, tq, LANES), lambda i, j: (j, i, 0)),
            pl.BlockSpec((1, LANES), lambda i, j: (0, 0)),
            pl.BlockSpec((LANES, LANES), lambda i, j: (0, 0)),
        ],
        out_specs=[
            pl.BlockSpec((None, tq, NSA_WIDTH), lambda i, j: (j, i, 0)),
            pl.BlockSpec((None, nslp, tq), lambda i, j: (j, 0, i)),
            pl.BlockSpec((None, nslp, tq), lambda i, j: (j, 0, i)),
        ],
        out_shape=[
            jax.ShapeDtypeStruct((b, s, NSA_WIDTH), F32),
            jax.ShapeDtypeStruct((b, nslp, s), F32),
            jax.ShapeDtypeStruct((b, nslp, s), F32),
        ],
        compiler_params=_cparams("parallel", "parallel"),
        name="cmp_attention",
    )(q, parts, bias_t, gn, gk, _seg_matrix(LANES))


def _flash_kernel(q_ref, ns0_ref, ns1_ref, kv_ref, vt_ref, tab_ref, gn_ref, o_ref, *, t, use_sel, band, gate_base):
    qi = pl.program_id(1)
    row_k = _row_iota((t, LANES))
    lane_k = _lane_iota((t, LANES))
    lo_tile = jnp.maximum(qi - band, 0) if band is not None else 0
    gnt = gn_ref[...].T

    qts = []
    for g in range(NSA_KV_HEADS):
        qt_g = _group_query_columns(q_ref, g, t)
        if use_sel:
            nst = (ns0_ref if g == 0 else ns1_ref)[...].astype(BF16)
            qt_g = jnp.concatenate([qt_g, jnp.concatenate([nst] * NSA_HPG, axis=1)], axis=0)
        qts.append(qt_g)

    def body(kj, carry):
        k0 = pl.multiple_of(kj * t, t)
        kk = kv_ref[pl.ds(k0, t), :].astype(BF16)
        if use_sel:
            blk = kj * (t // SLC_BLOCK) + jnp.right_shift(row_k, SLC_BLOCK.bit_length() - 1)
            onehot = jnp.where(lane_k == blk, -(2.0 ** 30), 0.0).astype(BF16)
            kk = jnp.concatenate([kk, onehot], axis=1)
        delta = qi - kj
        if band is None:
            kind = jnp.minimum(delta, 2)
        else:
            kind = jnp.where(delta < 2, delta, jnp.where(delta < band, 2, 3))
        m_old, l_old, acc = carry
        s = _dot(kk, qt_all) + tab_ref[kind]
        m_new = jnp.maximum(m_old, jnp.max(s, axis=0, keepdims=True))
        alpha = jnp.exp(m_old - m_new)
        p = jnp.exp(s - m_new)
        l_new = alpha * l_old + jnp.sum(p, axis=0, keepdims=True)
        pb = p.astype(BF16)
        pv = jnp.concatenate(
            [_dot(vt_ref[g * HEAD_DIM:(g + 1) * HEAD_DIM, pl.ds(k0, t)].astype(BF16), pb[:, g * gw:(g + 1) * gw])
             for g in range(NSA_KV_HEADS)], axis=1)
        return m_new, l_new, acc * alpha + pv

    gw = NSA_HPG * t
    qt_all = jnp.concatenate(qts, axis=1)
    init = (jnp.full((1, NSA_HEADS * t), NEG, F32), jnp.zeros((1, NSA_HEADS * t), F32),
            jnp.zeros((HEAD_DIM, NSA_HEADS * t), F32))
    _, l_fin, acc = lax.fori_loop(lo_tile, qi + 1, body, init)
    gate = jnp.concatenate([gnt[gate_base + h:gate_base + h + 1] for h in range(NSA_HEADS)], axis=1)
    out = acc * (gate / l_fin)
    for g in range(NSA_KV_HEADS):
        _store_group_output(o_ref, out[:, g * gw:(g + 1) * gw], g, t)


def _flash_attention(q, ns0, ns1, k_rows, kv_t, tab, gn, *, use_sel, band, gate_base):
    b, s, _ = q.shape
    t = ATT_TILE
    assert ns0.shape[1] == LANES
    kern = functools.partial(_flash_kernel, t=t, use_sel=use_sel, band=band, gate_base=gate_base)
    tile = lambda w: pl.BlockSpec((None, t, w), lambda i, j: (i, j, 0))
    ns_tile = pl.BlockSpec((None, LANES, t), lambda i, j: (i, 0, j))
    return pl.pallas_call(
        kern,
        grid=(b, s // t),
        in_specs=[
            tile(NSA_WIDTH), ns_tile, ns_tile,
            pl.BlockSpec((None, s, LANES), lambda i, j: (i, 0, 0)),
            pl.BlockSpec((None, LANES, s), lambda i, j: (i, 1, 0)),
            pl.BlockSpec(tab.shape, lambda i, j: (0, 0, 0)),
            tile(LANES),
        ],
        out_specs=tile(NSA_WIDTH),
        out_shape=jax.ShapeDtypeStruct((b, s, NSA_WIDTH), F32),
        compiler_params=_cparams("parallel", "parallel"),
        name="flash_sel" if use_sel else "flash_win",
    )(q, ns0, ns1, k_rows, kv_t, tab, gn)


def _memattn_kernel(q_ref, kv_ref, o_ref):
    lane = _lane_iota((kv_ref.shape[0], LANES))
    for pr in range(MEM_HEADS // 2):
        qpair = q_ref[:, pr * LANES:(pr + 1) * LANES]
        kblk = kv_ref[:, pr * LANES:(pr + 1) * LANES]
        vblk = kv_ref[:, MEM_WIDTH + pr * LANES:MEM_WIDTH + (pr + 1) * LANES]
        out = None
        for half in range(2):
            keep = (lane < HEAD_DIM) if half == 0 else (lane >= HEAD_DIM)
            kk = jnp.where(keep, kblk, 0.0).astype(BF16)
            vv = jnp.where(keep, vblk, 0.0).astype(BF16)
            s = _dot_nt(qpair, kk)
            m = jnp.max(s, axis=1, keepdims=True)
            e = jnp.exp(s - m)
            p = e / jnp.sum(e, axis=1, keepdims=True)
            o = _dot(p.astype(BF16), vv)
            out = o if out is None else out + o
        o_ref[:, pr * LANES:(pr + 1) * LANES] = out


def _mem_attention(qm, mem_kv, tq):
    b, s, _ = qm.shape
    m = mem_kv.shape[1]
    return pl.pallas_call(
        _memattn_kernel,
        grid=(b, s // tq),
        in_specs=[
            pl.BlockSpec((None, tq, MEM_WIDTH), lambda i, j: (i, j, 0)),
            pl.BlockSpec((None, m, 2 * MEM_WIDTH), lambda i, j: (i, 0, 0)),
        ],
        out_specs=pl.BlockSpec((None, tq, MEM_WIDTH), lambda i, j: (i, j, 0)),
        out_shape=jax.ShapeDtypeStruct((b, s, MEM_WIDTH), F32),
        compiler_params=_cparams("parallel", "parallel"),
        name="mem_attention",
    )(qm, mem_kv)


def _dec_kernel(*refs, n_pg, pps, has_new):
    bidx_ref = refs[2]
    page_refs = refs[3:3 + pps]
    new_ref, wq_ref, bias_ref, ns_ref, gate_ref, o_ref, acc_ref, m_ref, l_ref = refs[3 + pps:]
    c = pl.program_id(1)
    n_chunks = pl.num_programs(1)

    @pl.when(c == 0)
    def _():
        acc_ref[...] = jnp.zeros_like(acc_ref)
        m_ref[...] = jnp.full_like(m_ref, NEG)
        l_ref[...] = jnp.zeros_like(l_ref)

    rk = _row_iota((LANES, LANES))

    def step(tiles, first_page):
        feats = (tiles[0] if len(tiles) == 1 else jnp.concatenate(tiles, axis=1)).astype(BF16)
        s = lax.dot_general(feats, wq_ref[...], (((0,), (1,)), ((), ())),
                            preferred_element_type=F32)
        extra = []
        for k in range(len(tiles)):
            pg = first_page + k
            ns = jnp.where(rk < SLC_BLOCK, ns_ref[pl.ds(2 * pg, 1), :], ns_ref[pl.ds(2 * pg + 1, 1), :])
            extra.append(jnp.where(ns > 0.5, NEG, bias_ref[bidx_ref[pg]]))
        s = s + (extra[0] if len(extra) == 1 else jnp.concatenate(extra, axis=0))
        m_old = m_ref[...]
        m_new = jnp.maximum(m_old, jnp.max(s, axis=0, keepdims=True))
        alpha = jnp.exp(m_old - m_new)
        p = jnp.exp(s - m_new)
        l_ref[...] = alpha * l_ref[...] + jnp.sum(p, axis=0, keepdims=True)
        m_ref[...] = m_new
        acc_ref[...] = acc_ref[...] * alpha + _dot(feats, p.astype(BF16))

    if has_new:
        @pl.when(c < n_chunks - 1)
        def _():
            step([r[...] for r in page_refs], c * pps)

        @pl.when(c == n_chunks - 1)
        def _():
            step([r[...] for r in page_refs] + [new_ref[...]], n_pg - pps)
    else:
        step([r[...] for r in page_refs], c * pps)

    @pl.when(c == n_chunks - 1)
    def _():
        o_ref[...] = acc_ref[...] / l_ref[...] * gate_ref[...]


def _decode_attention(pages, phys, lblk, new_rows, wq, bias_tab, bias_idx, notsel, gate):
    bsz, n_pg = phys.shape
    w = pages.shape[1]
    has_new = new_rows is not None
    pps = math.gcd(n_pg, DEC_PAGES_PER_STEP)
    n_steps = n_pg // pps
    n_chunks = n_steps
    if not has_new:
        new_rows = jnp.zeros((1, w, LANES), F32)
    new_map = (lambda i, c, ph, lb, bi: (i, 0, 0)) if has_new else (lambda i, c, ph, lb, bi: (0, 0, 0))

    def page_spec(k):
        def index(i, c, ph, lb, bi):
            return (ph[i, c * pps + k], 0, lb[i, c * pps + k])
        return pl.BlockSpec((None, w, LANES), index)

    per_b = lambda i, c, ph, lb, bi: (i, 0, 0)
    grid_spec = pltpu.PrefetchScalarGridSpec(
        num_scalar_prefetch=3,
        grid=(bsz, n_chunks),
        in_specs=[page_spec(k) for k in range(pps)] + [
            pl.BlockSpec((None, w, LANES), new_map),
            pl.BlockSpec((None, LANES, w), per_b),
            pl.BlockSpec(bias_tab.shape, lambda i, c, ph, lb, bi: (0, 0, 0)),
            pl.BlockSpec((None, notsel.shape[1], LANES), per_b),
            pl.BlockSpec((None, 1, LANES), per_b),
        ],
        out_specs=pl.BlockSpec((None, w, LANES), per_b),
        scratch_shapes=[pltpu.VMEM((w, LANES), F32), pltpu.VMEM((1, LANES), F32), pltpu.VMEM((1, LANES), F32)],
    )
    return pl.pallas_call(
        functools.partial(_dec_kernel, n_pg=n_pg, pps=pps, has_new=has_new),
        grid_spec=grid_spec,
        out_shape=jax.ShapeDtypeStruct((bsz, w, LANES), F32),
        compiler_params=_cparams("parallel", "arbitrary"),
        name="decode_attention",
    )(phys, lblk, bias_idx, *([pages] * pps), new_rows, wq, bias_tab, notsel, gate)


def _feature_major(cache):
    n, rows = cache.shape[:2]
    return cache.transpose(0, 2, 3, 4, 1).reshape(n, -1, rows)


def _tail_kernel(x_ref, g1_ref, op_ref, oc_ref, os_ref, ow_ref, om_ref, cnt0_ref, wgb_ref, wup_p_ref, wup_n_ref,
                 wup_m_ref, wout_ref, g2_ref, rwh_ref, rwl_ref, rb_ref, x2_ref, h2_ref, ei_ref, gt_ref, cnt_ref):
    @pl.when(pl.program_id(0) == 0)
    def _():
        cnt_ref[...] = cnt0_ref[...]

    x = x_ref[...]
    h = _rms(x, g1_ref[...]).astype(BF16)
    onsa = (oc_ref[...] + os_ref[...] + ow_ref[...]).astype(BF16)
    ups = (_dot(op_ref[...].astype(BF16), wup_p_ref[...]), _dot(onsa, wup_n_ref[...]),
           _dot(om_ref[...].astype(BF16), wup_m_ref[...]))
    mixed = None
    for br in range(3):
        gb = _sigmoid(_dot(h, wgb_ref[:, br * D_MODEL:(br + 1) * D_MODEL]))
        mixed = gb * ups[br] if mixed is None else mixed + gb * ups[br]
    x2 = x + _dot(mixed.astype(BF16), wout_ref[...])
    x2_ref[...] = x2
    h2 = _rms(x2, g2_ref[...])
    h2_ref[...] = _pack_bf16_pairs(h2)
    hi = h2.astype(BF16)
    lo = (h2 - hi.astype(F32)).astype(BF16)
    logits = _dot(hi, rwh_ref[...]) + _dot(lo, rwh_ref[...]) + _dot(hi, rwl_ref[...]) + rb_ref[...]
    lane = _lane_iota(logits.shape)
    lanef = lane.astype(F32)
    tops, idxs = [], []
    for _ in range(TOP_K):
        m = jnp.max(logits, axis=1, keepdims=True)
        idx = jnp.min(jnp.where(logits == m, lanef, 1e9), axis=1, keepdims=True)
        logits = jnp.where(lanef == idx, -jnp.inf, logits)
        tops.append(m)
        idxs.append(idx)
    es = [jnp.exp(tk - tops[0]) for tk in tops]
    den = es[0] + es[1] + es[2] + es[3]
    tm = logits.shape[0]
    onehot = jnp.zeros(logits.shape, F32)
    for k in range(TOP_K):
        onehot = jnp.where(lanef == idxs[k], 1.0, onehot)
    tri = jnp.where(_row_iota((tm, tm)) > _lane_iota((tm, tm)), 1.0, 0.0).astype(BF16)
    before = _dot(tri, onehot.astype(BF16)) + cnt_ref[...]
    cnt_ref[...] = cnt_ref[...] + jnp.sum(onehot, axis=0, keepdims=True)
    ei = jnp.zeros(logits.shape, F32)
    gt = jnp.zeros(logits.shape, F32)
    for k in range(TOP_K):
        rank = jnp.sum(jnp.where(lanef == idxs[k], before, 0.0), axis=1, keepdims=True)
        ei = jnp.where(lane == k, idxs[k], ei)
        ei = jnp.where(lane == TOP_K + k, rank, ei)
        gt = jnp.where(lane == k, es[k] / den, gt)
    ei_ref[...] = ei.astype(jnp.int32)
    gt_ref[...] = gt


def _layer_tail(x, o_pool, o_cmp, o_slc, o_win, o_mem, cnt0, w, tm):
    n = x.shape[0]
    row = lambda wd: pl.BlockSpec((tm, wd), lambda i: (i, 0))
    full = lambda a: pl.BlockSpec(a.shape, lambda i: (0,) * a.ndim)
    weights = (w["wgb"], w["wup_pool"], w["wup_nsa"], w["wup_mem"], w["wout"], w["g2"], w["rw_hi"], w["rw_lo"],
               w["rb"])
    return pl.pallas_call(
        _tail_kernel,
        grid=(n // tm,),
        in_specs=[row(D_MODEL), full(w["g1"]), row(POOL_WIDTH), row(NSA_WIDTH), row(NSA_WIDTH), row(NSA_WIDTH),
                  row(MEM_WIDTH), full(cnt0)] + [full(a) for a in weights],
        out_specs=[row(D_MODEL), row(D_MODEL // 2), row(LANES), row(LANES), full(cnt0)],
        out_shape=[jax.ShapeDtypeStruct((n, D_MODEL), F32), jax.ShapeDtypeStruct((n, D_MODEL // 2), jnp.uint32),
                   jax.ShapeDtypeStruct((n, LANES), jnp.int32), jax.ShapeDtypeStruct((n, LANES), F32),
                   jax.ShapeDtypeStruct((1, LANES), F32)],
        compiler_params=_cparams("arbitrary"),
        name="layer_tail",
    )(x, w["g1"], o_pool, o_cmp, o_slc, o_win, o_mem, cnt0, *weights)


def _ffn_kernel(be_ref, nu_ref, x_ref, wgu_ref, bgu_ref, wd_ref, bd_ref, o_ref, wgu_bf, wd_bf):
    i = pl.program_id(0)

    @pl.when((i == 0) | (be_ref[i] != be_ref[jnp.maximum(i - 1, 0)]))
    def _():
        wgu_bf[...] = wgu_ref[...].astype(BF16)
        wd_bf[...] = wd_ref[...].astype(BF16)

    @pl.when(i < nu_ref[0])
    def _():
        gu = _dot(_unpack_bf16_pairs(x_ref[...]).astype(BF16), wgu_bf[...]) + bgu_ref[...]
        gate = jnp.minimum(gu[:, :D_FF], SWIGLU_LIMIT)
        up = jnp.clip(gu[:, D_FF:], -SWIGLU_LIMIT, SWIGLU_LIMIT)
        act = gate * _sigmoid(SWIGLU_ALPHA * gate) * (up + 1.0)
        o_ref[...] = _dot(act.astype(BF16), wd_bf[...]) + bd_ref[...]

    @pl.when(i >= nu_ref[0])
    def _():
        o_ref[...] = jnp.zeros_like(o_ref)


def _expert_ffn(rows, blk_e, n_used, wgu, bgu, wd, bd):
    n_rows = rows.shape[0]
    n_blocks = n_rows // MOE_TILE
    blk = lambda i, be, nu: (jnp.minimum(i, nu[0] - 1), 0)
    grid_spec = pltpu.PrefetchScalarGridSpec(
        num_scalar_prefetch=2,
        grid=(n_blocks,),
        in_specs=[
            pl.BlockSpec((MOE_TILE, D_MODEL // 2), blk),
            pl.BlockSpec((None, D_MODEL, 2 * D_FF), lambda i, be, nu: (be[i], 0, 0)),
            pl.BlockSpec((None, 1, 2 * D_FF), lambda i, be, nu: (be[i], 0, 0)),
            pl.BlockSpec((None, D_FF, D_MODEL), lambda i, be, nu: (be[i], 0, 0)),
            pl.BlockSpec((None, 1, D_MODEL), lambda i, be, nu: (be[i], 0, 0)),
        ],
        out_specs=pl.BlockSpec((MOE_TILE, D_MODEL), lambda i, be, nu: (i, 0)),
        scratch_shapes=[pltpu.VMEM((D_MODEL, 2 * D_FF), BF16), pltpu.VMEM((D_FF, D_MODEL), BF16)],
    )
    return pl.pallas_call(
        _ffn_kernel,
        grid_spec=grid_spec,
        out_shape=jax.ShapeDtypeStruct((n_rows, D_MODEL), F32),
        compiler_params=pltpu.CompilerParams(dimension_semantics=("arbitrary",), vmem_limit_bytes=FFN_VMEM_LIMIT),
        name="expert_ffn",
    )(blk_e, n_used, rows, wgu, bgu, wd, bd)


def _token_tile(n):
    return next(t for t in (MOE_DMA_TOKENS, 384, 256, 128, n) if n % t == 0)


def _dispatch_scatter(dest_ref, h_ref, rows_ref, sem, td):
    def issue(i, carry):
        t0 = pl.multiple_of(i * 8, 8)
        for r in range(8):
            for k in range(TOP_K):
                d = dest_ref[i * (8 * TOP_K) + r * TOP_K + k]
                pltpu.make_async_copy(h_ref.at[pl.ds(t0 + r, 1)], rows_ref.at[pl.ds(d, 1)], sem).start()
        return carry

    lax.fori_loop(0, td // 8, issue, 0)
    for k in range(TOP_K):
        pltpu.make_async_copy(h_ref, rows_ref.at[pl.ds(0, td)], sem).wait()


def _dispatch_kernel(*refs, tiles, steps, n_blocks):
    ng = len(tiles)
    ends_ref, padded_ref, nu_ref = refs[:3]
    dest_refs = refs[3:3 + ng]
    h_refs = refs[3 + ng:3 + 2 * ng]
    rows_ref, zero_ref, sem, zsem = refs[3 + 2 * ng:]
    i = pl.program_id(0)

    @pl.when(i == 0)
    def _():
        zero_ref[...] = jnp.zeros_like(zero_ref)

        def tail_copy(e):
            start = pl.multiple_of(ends_ref[e] - MOE_TILE, MOE_TILE)
            return pltpu.make_async_copy(zero_ref, rows_ref.at[pl.ds(start, MOE_TILE)], zsem)

        def block_copy(blk):
            start = pl.multiple_of(blk * MOE_TILE, MOE_TILE)
            return pltpu.make_async_copy(zero_ref, rows_ref.at[pl.ds(start, MOE_TILE)], zsem)

        def each(start):
            def expert(e, carry):
                @pl.when(padded_ref[e] > 0)
                def _():
                    tail_copy(e).start() if start else tail_copy(e).wait()
                return carry

            def block(blk, carry):
                block_copy(blk).start() if start else block_copy(blk).wait()
                return carry

            lax.fori_loop(0, N_EXPERTS, expert, 0)
            lax.fori_loop(nu_ref[0], n_blocks, block, 0)

        each(True)
        each(False)

    first = 0
    for g in range(ng):
        @pl.when((i >= first) & (i < first + steps[g]))
        def _(g=g):
            _dispatch_scatter(dest_refs[g], h_refs[g], rows_ref, sem, tiles[g])
        first += steps[g]


def _moe_dispatch(h2s, dests, n_rows, pad_ends, padded, n_used):
    tiles = [_token_tile(h.shape[0]) for h in h2s]
    steps = [h.shape[0] // t for h, t in zip(h2s, tiles)]
    firsts = [sum(steps[:g]) for g in range(len(h2s))]

    def local(g):
        return lambda i, *_: jnp.clip(i - firsts[g], 0, steps[g] - 1)

    dest_specs = [pl.BlockSpec((tiles[g] * TOP_K,), lambda i, *_, f=local(g): (f(i),), memory_space=pltpu.SMEM)
                  for g in range(len(h2s))]
    width, dtype = h2s[0].shape[1], h2s[0].dtype
    tok_specs = [pl.BlockSpec((tiles[g], width), lambda i, *_, f=local(g): (f(i), 0)) for g in range(len(h2s))]
    grid_spec = pltpu.PrefetchScalarGridSpec(
        num_scalar_prefetch=3,
        grid=(sum(steps),),
        in_specs=dest_specs + tok_specs,
        out_specs=pl.BlockSpec(memory_space=pl.ANY),
        scratch_shapes=[pltpu.VMEM((MOE_TILE, width), dtype), pltpu.SemaphoreType.DMA(()),
                        pltpu.SemaphoreType.DMA(())],
    )
    return pl.pallas_call(
        functools.partial(_dispatch_kernel, tiles=tiles, steps=steps, n_blocks=n_rows // MOE_TILE),
        grid_spec=grid_spec,
        out_shape=jax.ShapeDtypeStruct((n_rows, width), dtype),
        compiler_params=_cparams("arbitrary"),
        name="moe_dispatch",
    )(pad_ends, padded, n_used, *dests, *h2s)


def _combine_kernel(dest_ref, x2_ref, g_ref, rows_ref, o_ref, ybuf, sem, *, td):
    def issue(i, carry):
        t0 = pl.multiple_of(i * 8, 8)
        for r in range(8):
            for k in range(TOP_K):
                d = dest_ref[i * (8 * TOP_K) + r * TOP_K + k]
                pltpu.make_async_copy(rows_ref.at[pl.ds(d, 1)], ybuf.at[k, pl.ds(t0 + r, 1)], sem).start()
        return carry

    lax.fori_loop(0, td // 8, issue, 0)
    for k in range(TOP_K):
        pltpu.make_async_copy(rows_ref.at[pl.ds(0, td)], ybuf.at[k], sem).wait()
    out = x2_ref[...]
    for k in range(TOP_K):
        out = out + g_ref[:, k:k + 1] * ybuf[k]
    o_ref[...] = out


def _moe_combine(x2, out_rows, dest, gates):
    n = x2.shape[0]
    td = _token_tile(n)
    return pl.pallas_call(
        functools.partial(_combine_kernel, td=td),
        grid=(n // td,),
        in_specs=[pl.BlockSpec((td * TOP_K,), lambda i: (i,), memory_space=pltpu.SMEM),
                  pl.BlockSpec((td, D_MODEL), lambda i: (i, 0)),
                  pl.BlockSpec((td, LANES), lambda i: (i, 0)),
                  pl.BlockSpec(memory_space=pl.ANY)],
        out_specs=pl.BlockSpec((td, D_MODEL), lambda i: (i, 0)),
        out_shape=jax.ShapeDtypeStruct((n, D_MODEL), F32),
        scratch_shapes=[pltpu.VMEM((TOP_K, td) + out_rows.shape[1:], out_rows.dtype), pltpu.SemaphoreType.DMA(())],
        compiler_params=_cparams("arbitrary"),
        name="moe_combine",
    )(dest, x2, gates, out_rows)


def _moe(groups, counts, w):
    n_total = sum(g[0].shape[0] for g in groups)
    cnt = counts[0, :N_EXPERTS].astype(jnp.int32)
    padded = (cnt + MOE_TILE - 1) // MOE_TILE * MOE_TILE
    pad_ends = jnp.cumsum(padded)
    pad_starts = pad_ends - padded
    n_blocks = -(-n_total * TOP_K // MOE_TILE) + N_EXPERTS
    blk_start = jnp.arange(n_blocks, dtype=jnp.int32) * MOE_TILE
    blk_e = jnp.minimum(jnp.sum(blk_start[:, None] >= pad_ends[None, :], axis=1), N_EXPERTS - 1).astype(jnp.int32)
    n_used = (pad_ends[-1] // MOE_TILE).astype(jnp.int32).reshape(1)
    dests = [(pad_starts[er[:, :TOP_K]] + er[:, TOP_K:2 * TOP_K]).astype(jnp.int32).reshape(-1)
             for (_, _, er, _) in groups]
    rows = _moe_dispatch([g[1] for g in groups], dests, n_blocks * MOE_TILE, pad_ends.astype(jnp.int32),
                         padded.astype(jnp.int32), n_used)
    out_rows = _expert_ffn(rows, blk_e, n_used, w["wgu"], w["bgu"], w["wd"], w["bd"])
    return [_moe_combine(x2, out_rows, dest, gates) for (x2, _, _, gates), dest in zip(groups, dests)]


def _rel_bucket(dist):
    n = jnp.maximum(dist, 0)
    max_exact = NUM_BUCKETS // 2
    nf = jnp.maximum(n, 1).astype(F32)
    large = max_exact + (jnp.log(nf / max_exact) / math.log(MAX_DISTANCE / max_exact)
                         * (NUM_BUCKETS - max_exact)).astype(jnp.int32)
    large = jnp.minimum(large, NUM_BUCKETS - 1)
    return jnp.where(n < max_exact, n, large)


def _bias_of(rel_bias, dist, valid):
    return jnp.where(valid[..., None], rel_bias[_rel_bucket(dist)], NEG)


def _proj_segs(kv_forms, kvc_forms):
    return ((0, 512, "qscale", ("rows",), BF16), (512, 256, "qscale", ("rows",), BF16),
            (768, 256, "id", kv_forms, F32), (1024, 256, "id", kv_forms, F32), (1280, 256, "id", kvc_forms, F32),
            (1536, 256, "id", ("rows",), F32), (1792, 128, "sigmoid", ("rows",), F32))


_PROJ_SEGS_PROMPT = _proj_segs(("key_rows", "t"), ("t",))
_PROJ_SEGS_SAMPLE = _proj_segs(("rows",), ("rows",))
_PROJ_NNORM = 1280


def _prep_layer(l, rel_bias, norm1_g, w_in, nsa_qk_norm, mem_qk_norm, cmp_w, cmp_pe, pool_w, pool_scale,
                mem_norm_g, w_mem_kv, w_up_pool, w_up_nsa, w_up_mem, w_out, norm2_g, router_w, router_b,
                w_gu, b_gu, w_down, b_down):
    wi = w_in[l]
    o_u, o_q, o_qm, o_kvc, o_kvs, o_kvw, o_gn, o_gb = 0, 256, 768, 1024, 1280, 1536, 1792, 1816
    w_proj = jnp.concatenate([
        wi[:, o_q:o_q + 512], wi[:, o_qm:o_qm + 256], wi[:, o_kvs:o_kvs + 256], wi[:, o_kvw:o_kvw + 256],
        wi[:, o_kvc:o_kvc + 256], wi[:, o_u:o_u + 256], wi[:, o_gn:o_gn + 24],
        jnp.zeros((D_MODEL, LANES - 24), F32)], axis=1).astype(BF16)
    nq, mq = nsa_qk_norm[l], mem_qk_norm[l]
    ones = jnp.ones((LANES,), F32)
    gain = jnp.concatenate([jnp.tile(nq[0], 8), jnp.tile(mq[0], 4), jnp.tile(nq[2], 2), ones,
                            jnp.tile(nq[3], 2), ones])[None, :]
    nmask = jnp.concatenate([jnp.ones((768,), F32), ones, 0 * ones, ones, 0 * ones])[None, :]
    eye4 = jnp.eye(4, dtype=F32)
    cw = cmp_w[l].reshape(2, 2, CMP_STRIDE, HEAD_DIM, HEAD_DIM)
    w_c = jnp.einsum("crjde,xy->cjxdrye", cw, jnp.eye(2, dtype=F32))
    w_c = w_c.reshape(2, CMP_STRIDE, LANES, KV_WIDTH).astype(BF16)
    pe = cmp_pe[l].reshape(2, 2, CMP_STRIDE, HEAD_DIM)
    pe_c = jnp.tile(pe.transpose(0, 2, 1, 3), (1, 1, 1, NSA_KV_HEADS))
    pe_c = jnp.pad(pe_c, ((0, 0), (0, 0), (0, 6), (0, 0)))
    w_pool = jnp.einsum("gde,gh->gdhe", pool_w[l], eye4).reshape(POOL_WIDTH, POOL_WIDTH).astype(BF16)
    rw = jnp.pad(router_w[l], ((0, 0), (0, LANES - N_EXPERTS)))
    rw_hi = rw.astype(BF16)
    rw_lo = (rw - rw_hi.astype(F32)).astype(BF16)
    rb = jnp.concatenate([router_b[l], jnp.full((LANES - N_EXPERTS,), NEG, F32)])[None, :]
    return {
        "g1": norm1_g[l][None, :], "w_proj": w_proj, "gain": gain, "nmask": nmask,
        "w_kvc_t": wi[:, o_kvc:o_kvc + 256].T.astype(BF16),
        "gk_cmp": jnp.tile(nq[1], 2)[None, :], "w_c": w_c, "pe_c": pe_c,
        "w_pool": w_pool, "pool_scale": pool_scale[l][None, :],
        "mem_g": mem_norm_g[l][None, :], "w_mem": w_mem_kv[l].astype(BF16),
        "mem_gain": jnp.concatenate([jnp.tile(mq[1], 4), jnp.ones((256,), F32)])[None, :],
        "mem_nmask": jnp.concatenate([jnp.ones((256,), F32), jnp.zeros((256,), F32)])[None, :],
        "wgb": wi[:, o_gb:o_gb + 3 * D_MODEL].astype(BF16),
        "wup_pool": w_up_pool[l].astype(BF16), "wup_nsa": w_up_nsa[l].astype(BF16),
        "wup_mem": w_up_mem[l].astype(BF16), "wout": w_out[l].astype(BF16), "g2": norm2_g[l][None, :],
        "rw_hi": rw_hi, "rw_lo": rw_lo, "rb": rb,
        "wgu": w_gu[l], "bgu": b_gu[l][:, None, :], "wd": w_down[l],
        "bd": b_down[l][:, None, :],
    }


def _project_in(x2d, w, segs, tm, seq):
    wt = w["w_kvc_t"] if segs is _PROJ_SEGS_PROMPT else None
    return _project(x2d, w["g1"], w["w_proj"], w["gain"], w["nmask"], segs, _PROJ_NNORM, tm, seq, wt)


def _rows_view(a_t, heads):
    b, _, rows = a_t.shape
    return a_t.reshape(b, 2, heads, HEAD_DIM, rows).transpose(0, 4, 1, 2, 3)


def _toeplitz(v, t):
    lead = v.shape[:-1]
    flat = jnp.tile(v, (1,) * len(lead) + (t,))[..., t:t + t * (2 * t - 1)]
    return flat.reshape(lead + (t, 2 * t - 1))[..., :t]


def _flash_tables(rel_bias):
    t = ATT_TILE
    d0 = jnp.arange(-t, t)
    kinds = jnp.stack([
        _bias_of(rel_bias, d0, d0 >= 0),
        _bias_of(rel_bias, d0 + t, d0 + t >= 0),
        _bias_of(rel_bias, jnp.full((2 * t,), 2 * t), jnp.ones((2 * t,), bool)),
        _bias_of(rel_bias, d0 + WINDOW, d0 + WINDOW < WINDOW),
    ])
    tab = _toeplitz(kinds.transpose(2, 0, 1), t)
    return tab.transpose(1, 2, 0, 3).reshape(4, t, NSA_HEADS * t)


def _cmp_bias_table(rel_bias, s, n_sub, n_cmp, pos0, tq):
    na = s // CMP_STRIDE
    m = max(na, n_sub)
    k = jnp.arange(-m, m)[None, :]
    r = jnp.arange(CMP_STRIDE)[:, None]
    d = CMP_STRIDE * k + r - (CMP_BLOCK - 1) + pos0
    v = _bias_of(rel_bias, d, d >= 0).transpose(2, 0, 1)
    tz = _toeplitz(v, m)[:, :, :n_sub, :na]
    tz = tz.reshape(NSA_KV_HEADS, NSA_HPG, CMP_STRIDE, n_sub, s // tq, tq // CMP_STRIDE)
    tab = tz.transpose(0, 3, 4, 1, 5, 2).reshape(NSA_KV_HEADS, n_sub, NSA_HPG * s)
    return jnp.where(jnp.arange(n_sub)[None, :, None] < n_cmp, tab, NEG)


def _prompt_pre(x, mem, w, rel_bias, cnt0):
    b, s, _ = x.shape
    n = b * s
    tm = 512 if n % 512 == 0 else ATT_TILE
    x2d = x.reshape(n, D_MODEL)
    q, qm, ks, kvs_t, kw, kvw_t, kvc_t, u, gn = _project_in(x2d, w, _PROJ_SEGS_PROMPT, tm, s)
    r3 = lambda a: a.reshape(b, s, a.shape[-1])
    q, qm, ks, kw, u, gn = map(r3, (q, qm, ks, kw, u, gn))

    o_pool = _pool_mix(u, jnp.zeros((b, 16, POOL_WIDTH), F32), w["w_pool"], w["pool_scale"], 0)

    n_cmp = (s - CMP_BLOCK) // CMP_STRIDE + 1
    n_slc = -(-s // SLC_BLOCK)
    n_lb = s // LANES
    own = jnp.broadcast_to(jnp.arange(b, dtype=jnp.int32)[:, None], (b, n_lb))
    blocks = jnp.broadcast_to(jnp.arange(n_lb, dtype=jnp.int32)[None, :], (b, n_lb))
    parts = _cmp_partials_paged(kvc_t, own, blocks, w["pe_c"], w["w_c"], math.gcd(n_lb, 16))
    n_sub = parts.shape[1]
    tq = 256 if s % 256 == 0 else ATT_TILE
    bias_c = _cmp_bias_table(rel_bias, s, n_sub, n_cmp, 0, tq)
    o_cmp, ns0, ns1 = _cmp_attention(q, parts, bias_c, gn, w["gk_cmp"], tq=tq, n_cmp=n_cmp, n_slc=n_slc, pos0=0)

    tab = _flash_tables(rel_bias)
    o_slc = _flash_attention(q, ns0, ns1, ks, kvs_t, tab, gn, use_sel=True, band=None, gate_base=NSA_HEADS)
    o_win = _flash_attention(q, ns0, ns1, kw, kvw_t, tab, gn, use_sel=False, band=WINDOW // ATT_TILE,
                             gate_base=2 * NSA_HEADS)

    m = mem.shape[1]
    mem_kv, mem_kv_t = _project(mem.reshape(b * m, D_MODEL), w["mem_g"], w["w_mem"], w["mem_gain"],
                                w["mem_nmask"], ((0, 2 * MEM_WIDTH, "id", ("rows", "t"), F32),), MXU_DIM,
                                tm=math.gcd(m, 512), seq=m)
    o_mem = _mem_attention(qm, mem_kv.reshape(b, m, 2 * MEM_WIDTH), tq=min(512, s))

    f2 = lambda a: a.reshape(n, a.shape[-1])
    x2, h2, eidx, gates, cnt = _layer_tail(x2d, f2(o_pool), f2(o_cmp), f2(o_slc), f2(o_win), f2(o_mem), cnt0, w, tm)
    win_t = kvw_t[:, :, max(0, s - WINDOW):]
    if s < WINDOW:
        win_t = jnp.pad(win_t, ((0, 0), (0, 0), (WINDOW - s, 0)))
    states = (_rows_view(kvc_t, NSA_KV_HEADS), _rows_view(kvs_t, NSA_KV_HEADS), _rows_view(win_t, NSA_KV_HEADS),
              _rows_view(mem_kv_t, MEM_HEADS), _last_rows(u, POOL_BUF))
    return (x2, h2, eidx, gates), states, cnt


def _last_rows(a, n):
    t = a.shape[1]
    if t < n:
        a = jnp.pad(a, [(0, 0), (n - t, 0)] + [(0, 0)] * (a.ndim - 2))
    return a[:, a.shape[1] - n:]


def _dec_columns_nsa(q):
    b, t, _ = q.shape
    qh = q.reshape(b, t, NSA_KV_HEADS, NSA_HPG, HEAD_DIM)
    w = jnp.einsum("btgpd,gx->bxdgtp", qh.astype(F32), jnp.eye(NSA_KV_HEADS, dtype=F32))
    w = w.reshape(b, NSA_KV_HEADS * HEAD_DIM, NSA_KV_HEADS * t * NSA_HPG)
    return jnp.pad(w, ((0, 0), (0, KV_WIDTH - w.shape[1]), (0, LANES - w.shape[2]))).astype(BF16)


def _dec_extract_nsa(o, t):
    b = o.shape[0]
    v = o[:, LANES:, :NSA_KV_HEADS * t * NSA_HPG]
    v = v.reshape(b, NSA_KV_HEADS, HEAD_DIM, NSA_KV_HEADS, t, NSA_HPG)
    v = jnp.einsum("bxdgtp,gx->btgpd", v, jnp.eye(NSA_KV_HEADS, dtype=F32))
    return v.reshape(b, t, NSA_WIDTH)


def _dec_bias_cols(bias_tph):
    k, t, _ = bias_tph.shape
    bt = bias_tph.reshape(k, t, NSA_KV_HEADS, NSA_HPG).transpose(0, 2, 1, 3).reshape(k, NSA_KV_HEADS * t * NSA_HPG)
    return jnp.pad(bt, ((0, 0), (0, LANES - bt.shape[1])))


def _dec_gate_cols(gn, base, t):
    b = gn.shape[0]
    gt = gn[:, :, base:base + NSA_HEADS].reshape(b, t, NSA_KV_HEADS, NSA_HPG).transpose(0, 2, 1, 3)
    gt = gt.reshape(b, 1, NSA_KV_HEADS * t * NSA_HPG)
    return jnp.pad(gt, ((0, 0), (0, 0), (0, LANES - gt.shape[2])), constant_values=1.0)


def _sample_pre(x, cache_cmp, cache_slc, cache_win, cache_mem, pool_buf, page_table, w, rel_bias, cnt0):
    b, t, _ = x.shape
    n = b * t
    page = cache_cmp.shape[1]
    n_pages = page_table.shape[1]
    past = n_pages * page
    x2d = x.reshape(n, D_MODEL)
    tm = n if n <= 512 else LANES
    q, qm, kvs, kvw, kvc, u, gn = _project_in(x2d, w, _PROJ_SEGS_SAMPLE, tm, tm)
    r3 = lambda a: a.reshape(b, t, a.shape[-1])
    q, qm, kvs, kvw, kvc, u, gn = map(r3, (q, qm, kvs, kvw, kvc, u, gn))
    qpos = past + jnp.arange(t)

    buf16 = jnp.pad(pool_buf, ((0, 0), (16 - POOL_BUF, 0), (0, 0)))
    o_pool = _pool_mix(u, buf16, w["w_pool"], w["pool_scale"], past)

    total = past + t
    n_cmp = (total - CMP_BLOCK) // CMP_STRIDE + 1
    n_sub_used = n_cmp + CMP_BLOCK // CMP_STRIDE - 1
    n_slc = -(-total // SLC_BLOCK)
    pps = math.gcd(n_pages, 16)
    parts = _cmp_partials_paged(_feature_major(cache_cmp), page_table, jnp.zeros_like(page_table), w["pe_c"],
                                w["w_c"], pps)
    extra = n_sub_used * CMP_STRIDE - past
    if extra > 0:
        tail_rows = -(-extra // CMP_STRIDE) * CMP_STRIDE
        new_c = jnp.pad(kvc, ((0, 0), (0, max(0, tail_rows - t)), (0, 0)))[:, :tail_rows]
        parts = jnp.concatenate([parts, _cmp_partials_dense(new_c, w["pe_c"], w["w_c"])], axis=1)
    n_sub = parts.shape[1]
    end = jnp.arange(n_sub)[None, :] * CMP_STRIDE + CMP_BLOCK - 1
    bias_c = _bias_of(rel_bias, qpos[:, None] - end, (end <= qpos[:, None]) & (jnp.arange(n_sub)[None, :] < n_cmp))
    qpad = ((0, 0), (0, LANES - t), (0, 0))
    bias_c = _cmp_bias_cols(jnp.pad(bias_c.transpose(2, 0, 1), qpad), LANES)
    o_cmp, ns0, ns1 = _cmp_attention(jnp.pad(q, qpad), parts, bias_c, jnp.pad(gn, qpad), w["gk_cmp"], tq=LANES,
                                     n_cmp=n_cmp, n_slc=n_slc, pos0=past)
    o_cmp = o_cmp[:, :t]

    wq = _dec_columns_nsa(q).transpose(0, 2, 1)
    ncol = NSA_KV_HEADS * t * NSA_HPG
    new_tile = lambda kv: jnp.pad(kv, ((0, 0), (0, LANES - t), (0, 0))).transpose(0, 2, 1)
    own = lambda npg: jnp.broadcast_to(jnp.arange(b, dtype=jnp.int32)[:, None], (b, npg))
    blocks = lambda npg: jnp.broadcast_to(jnp.arange(npg, dtype=jnp.int32)[None, :], (b, npg))

    n_chunks = n_pages + 1
    nblk = -(-2 * n_chunks // 8) * 8
    ns = jnp.stack([ns0, ns1], axis=1)[:, :, :, :t].transpose(0, 1, 3, 2)
    ns = jnp.pad(ns, ((0, 0), (0, 0), (0, 0), (0, max(0, nblk - ns.shape[3]))))[..., :nblk]
    ns = jnp.broadcast_to(ns[:, :, :, None, :], (b, NSA_KV_HEADS, t, NSA_HPG, nblk)).reshape(b, ncol, nblk)
    notsel = jnp.pad(ns.transpose(0, 2, 1), ((0, 0), (0, 0), (0, LANES - ncol)))
    rows = jnp.arange(LANES)
    far = _bias_of(rel_bias, jnp.full((LANES, t), 2 * MAX_DISTANCE), jnp.ones((LANES, t), bool))
    kpos_last = past - LANES + rows
    d_last = qpos[None, :] - kpos_last[:, None]
    near = _bias_of(rel_bias, d_last, d_last >= 0)
    kpos_new = past + rows
    d_new = qpos[None, :] - kpos_new[:, None]
    newb = _bias_of(rel_bias, d_new, (d_new >= 0) & (rows[:, None] < t))
    bias_tab = jnp.stack([_dec_bias_cols(far), _dec_bias_cols(near), _dec_bias_cols(newb)])
    bias_idx = jnp.concatenate([jnp.zeros((n_pages - 1,), jnp.int32), jnp.array([1, 2], jnp.int32)])
    o_slc = _decode_attention(_feature_major(cache_slc), page_table, jnp.zeros_like(page_table), new_tile(kvs), wq,
                              bias_tab, bias_idx, notsel, _dec_gate_cols(gn, NSA_HEADS, t))
    o_slc = _dec_extract_nsa(o_slc, t)

    wb = cache_win.shape[1]
    n_wpg = wb // LANES
    kpos_w = past - wb + jnp.arange(wb + LANES)
    d_w = qpos[None, :] - kpos_w[:, None]
    valid_w = (d_w >= 0) & (d_w < WINDOW) & (kpos_w[:, None] >= 0) & (jnp.arange(wb + LANES)[:, None] < wb + t)
    bias_w = _dec_bias_cols(_bias_of(rel_bias, d_w, valid_w)).reshape(n_wpg + 1, LANES, LANES)
    zeros_ns = jnp.zeros((b, -(-2 * (n_wpg + 1) // 8) * 8, LANES), F32)
    o_win = _decode_attention(_feature_major(cache_win), own(n_wpg), blocks(n_wpg), new_tile(kvw), wq, bias_w,
                              jnp.arange(n_wpg + 1, dtype=jnp.int32), zeros_ns, _dec_gate_cols(gn, 2 * NSA_HEADS, t))
    o_win = _dec_extract_nsa(o_win, t)

    m = cache_mem.shape[1]
    n_mpg = m // LANES
    qmh = qm.reshape(b, t, MEM_HEADS, HEAD_DIM).astype(F32)
    wqm = jnp.einsum("bthd,hx->bxdht", qmh, jnp.eye(MEM_HEADS, dtype=F32))
    wqm = wqm.reshape(b, MEM_WIDTH, MEM_HEADS * t)
    wqm = jnp.pad(wqm, ((0, 0), (0, MEM_WIDTH), (0, LANES - MEM_HEADS * t))).astype(BF16).transpose(0, 2, 1)
    o_mem = _decode_attention(_feature_major(cache_mem), own(n_mpg), blocks(n_mpg), None, wqm,
                              jnp.zeros((1, LANES, LANES), F32), jnp.zeros((n_mpg,), jnp.int32),
                              jnp.zeros((b, 8, LANES), F32), jnp.ones((b, 1, LANES), F32))
    om = o_mem[:, MEM_WIDTH:, :MEM_HEADS * t].reshape(b, MEM_HEADS, HEAD_DIM, MEM_HEADS, t)
    o_mem = jnp.einsum("bxdht,hx->bthd", om, jnp.eye(MEM_HEADS, dtype=F32)).reshape(b, t, MEM_WIDTH)

    f2 = lambda a: a.reshape(n, a.shape[-1])
    x2, h2, eidx, gates, cnt = _layer_tail(x2d, f2(o_pool), f2(o_cmp), f2(o_slc), f2(o_win), f2(o_mem), cnt0, w,
                                           n if n <= 512 else LANES)
    kvshape = (b, t, 2, NSA_KV_HEADS, HEAD_DIM)
    new_win = _rows_view(jnp.concatenate([_feature_major(cache_win)[:, :, t:], kvw.transpose(0, 2, 1)], axis=2),
                         NSA_KV_HEADS)
    new_pool = jnp.concatenate([pool_buf, u], axis=1)[:, t:]
    states = (kvc.reshape(kvshape), kvs.reshape(kvshape), new_win, new_pool)
    return (x2, h2, eidx, gates), states, cnt


def kernel(x_prompt, x_sample, cache_cmp_kv, cache_slc_kv, cache_win_kv, cache_mem_kv, state_pool, page_table,
           mem_prompt, rel_bias, norm1_g, w_in, nsa_qk_norm, mem_qk_norm, cmp_w, cmp_pe, pool_w, pool_scale,
           mem_norm_g, w_mem_kv, w_up_pool, w_up_nsa, w_up_mem, w_out, norm2_g, router_w, router_b, w_gu, b_gu,
           w_down, b_down):
    depth = w_in.shape[0]
    yp, ys = x_prompt, x_sample
    bp, sp, _ = x_prompt.shape
    bs, ts, _ = x_sample.shape
    outs_p = [[] for _ in range(5)]
    outs_s = [[] for _ in range(4)]
    for l in range(depth):
        w = _prep_layer(l, rel_bias, norm1_g, w_in, nsa_qk_norm, mem_qk_norm, cmp_w, cmp_pe, pool_w, pool_scale,
                        mem_norm_g, w_mem_kv, w_up_pool, w_up_nsa, w_up_mem, w_out, norm2_g, router_w, router_b,
                        w_gu, b_gu, w_down, b_down)
        pre_p, st_p, cnt = _prompt_pre(yp, mem_prompt, w, rel_bias, jnp.zeros((1, LANES), F32))
        pre_s, st_s, cnt = _sample_pre(ys, cache_cmp_kv[l], cache_slc_kv[l], cache_win_kv[l], cache_mem_kv[l],
                                       state_pool[l], page_table, w, rel_bias, cnt)
        yp, ys = _moe([pre_p, pre_s], cnt, w)
        yp = yp.reshape(bp, sp, D_MODEL)
        ys = ys.reshape(bs, ts, D_MODEL)
        for lst, a in zip(outs_p, st_p):
            lst.append(a)
        for lst, a in zip(outs_s, st_s):
            lst.append(a)
    new_cmp_p, new_slc_p, new_win_p, new_mem_p, new_pool_p = [jnp.stack(a) for a in outs_p]
    new_cmp_s, new_slc_s, new_win_s, new_pool_s = [jnp.stack(a) for a in outs_s]
    new_win_s = new_win_s.reshape(new_win_s.shape[:3] + (2, NSA_KV_HEADS, HEAD_DIM))
    return (yp, ys, new_cmp_p, new_slc_p, new_win_p, new_mem_p, new_pool_p,
            new_cmp_s, new_slc_s, new_win_s, new_pool_s)
```

```python
import functools
import math

import jax
import jax.numpy as jnp
from jax import lax
from jax.experimental import pallas as pl
from jax.experimental.pallas import tpu as pltpu

F32 = jnp.float32
BF16 = jnp.bfloat16

D_MODEL = 1024
HEAD_DIM = 64
POOL_WINDOWS = (2, 4, 8, 16)
POOL_GROUP = 64
POOL_WIDTH = 256
POOL_BUF = 15
NSA_HEADS = 8
NSA_KV_HEADS = 2
NSA_HPG = 4
NSA_WIDTH = 512
KV_WIDTH = 256
CMP_BLOCK = 32
CMP_STRIDE = 16
SLC_BLOCK = 64
SLC_TOPK = 16
WINDOW = 512
MEM_HEADS = 4
MEM_WIDTH = 256
NUM_BUCKETS = 32
MAX_DISTANCE = 128
N_EXPERTS = 32
TOP_K = 4
D_FF = 1024
SWIGLU_ALPHA = 1.702
SWIGLU_LIMIT = 7.0
EPS = 1e-6
SCALE = HEAD_DIM ** -0.5

LANES = 128
MXU_DIM = 256
NEG = -1e30
ATT_TILE = 256
DEC_PAGES_PER_STEP = 32
MOE_TILE = 512
MOE_DMA_TOKENS = 512
VMEM_LIMIT = 48 * 1024 * 1024
FFN_VMEM_LIMIT = 56 * 1024 * 1024


def _cparams(*sem):
    return pltpu.CompilerParams(dimension_semantics=sem, vmem_limit_bytes=VMEM_LIMIT)


def _dot(a, b):
    return jnp.dot(a, b, preferred_element_type=F32)


def _dot_nt(a, b):
    return lax.dot_general(a, b, (((1,), (1,)), ((), ())), preferred_element_type=F32)


def _split_dot(a, b):
    hi = a.astype(BF16)
    lo = (a - hi.astype(F32)).astype(BF16)
    return _dot(hi, b) + _dot(lo, b)


def _rms(x, g):
    r = lax.rsqrt(jnp.mean(x * x, axis=-1, keepdims=True) + EPS)
    return (x * r) * g


def _sigmoid(x):
    return 1.0 / (1.0 + jnp.exp(-x))


def _pack_bf16_pairs(x):
    w = x.shape[1] // 2
    bits = lax.bitcast_convert_type(x.astype(BF16).astype(F32), jnp.uint32)
    return jnp.right_shift(bits[:, :w], jnp.uint32(16)) | (bits[:, w:] & jnp.uint32(0xFFFF0000))


def _unpack_bf16_pairs(p):
    lo = lax.bitcast_convert_type(jnp.left_shift(p, jnp.uint32(16)), F32)
    hi = lax.bitcast_convert_type(p & jnp.uint32(0xFFFF0000), F32)
    return jnp.concatenate([lo, hi], axis=1)


def _lane_iota(shape):
    return lax.broadcasted_iota(jnp.int32, shape, len(shape) - 1)


def _row_iota(shape):
    return lax.broadcasted_iota(jnp.int32, shape, len(shape) - 2)


def _proj_kernel(x_ref, g_ref, w_ref, gain_ref, nmask_ref, seg_ref, *rest, segs, n_norm, has_wt):
    wt_ref = rest[0] if has_wt else None
    out_refs = rest[1:] if has_wt else rest
    h = _rms(x_ref[...], g_ref[...]).astype(BF16)
    seg = seg_ref[...]
    outs = iter(out_refs)
    for (start, width, kind, forms, _) in segs:
        if forms == ("t",) and has_wt:
            next(outs)[...] = _dot_nt(wt_ref[...], h)
            continue
        z = _dot(h, w_ref[:, start:start + width])
        if start < n_norm:
            pieces = []
            for c in range(0, width, MXU_DIM):
                zc = z[:, c:c + MXU_DIM]
                ms = _split_dot(zc * zc, seg)
                zn = (zc * lax.rsqrt(ms + EPS)) * gain_ref[:, start + c:start + c + MXU_DIM]
                pieces.append(jnp.where(nmask_ref[:, start + c:start + c + MXU_DIM] > 0, zn, zc))
            z = pieces[0] if len(pieces) == 1 else jnp.concatenate(pieces, axis=1)
        if kind == "sigmoid":
            z = _sigmoid(z)
        elif kind == "qscale":
            z = z * SCALE
        for form in forms:
            o_ref = next(outs)
            if form == "rows":
                o_ref[...] = z.astype(o_ref.dtype)
            elif form == "key_rows":
                o_ref[...] = z[:, 0:LANES]
            else:
                o_ref[...] = z.T


def _project(x, g, w, gain, nmask, segs, n_norm, tm, seq, wt=None):
    n = x.shape[0]
    ncol = w.shape[1]
    seg = _seg_matrix(MXU_DIM)
    full = lambda i: (0, 0)
    tpb = seq // tm
    out_specs, out_shape = [], []
    for (_, wd, _, forms, dt) in segs:
        for form in forms:
            if form == "rows":
                out_specs.append(pl.BlockSpec((tm, wd), lambda i: (i, 0)))
                out_shape.append(jax.ShapeDtypeStruct((n, wd), dt))
            elif form == "key_rows":
                out_specs.append(pl.BlockSpec((tm, LANES), lambda i: (i, 0)))
                out_shape.append(jax.ShapeDtypeStruct((n, LANES), F32))
            else:
                out_specs.append(pl.BlockSpec((None, wd, tm), lambda i: (i // tpb, 0, i % tpb)))
                out_shape.append(jax.ShapeDtypeStruct((n // seq, wd, seq), F32))
    extra = () if wt is None else (wt,)
    return pl.pallas_call(
        functools.partial(_proj_kernel, segs=segs, n_norm=n_norm, has_wt=wt is not None),
        grid=(n // tm,),
        in_specs=[
            pl.BlockSpec((tm, D_MODEL), lambda i: (i, 0)),
            pl.BlockSpec((1, D_MODEL), full),
            pl.BlockSpec((D_MODEL, ncol), full),
            pl.BlockSpec((1, gain.shape[1]), full),
            pl.BlockSpec((1, nmask.shape[1]), full),
            pl.BlockSpec((MXU_DIM, MXU_DIM), full),
        ] + [pl.BlockSpec(a.shape, full) for a in extra],
        out_specs=out_specs,
        out_shape=out_shape,
        compiler_params=_cparams("parallel"),
        name="proj",
    )(x, g, w, gain, nmask, seg, *extra)


def _seg_matrix(n):
    i = jnp.arange(n) // HEAD_DIM
    return jnp.where(i[:, None] == i[None, :], 1.0 / HEAD_DIM, 0.0).astype(BF16)


def _pool_kernel(u_ref, buf_ref, w_ref, scale_ref, o_ref, zs_ref, *, t, pos0):
    zs_ref[0:16, :] = buf_ref[...]
    zs_ref[16:16 + t, :] = u_ref[...]
    u = u_ref[...]
    lane = _lane_iota((1, POOL_WIDTH))
    pos = (pos0 + _row_iota((t, 1))).astype(F32)
    acc = u
    mean = None
    for i in range(1, max(POOL_WINDOWS)):
        acc = acc + zs_ref[16 - i:16 - i + t, :]
        if i + 1 in POOL_WINDOWS:
            gi = POOL_WINDOWS.index(i + 1)
            m = acc / jnp.minimum(pos + 1.0, float(i + 1))
            mean = m if mean is None else jnp.where(lane >= gi * POOL_GROUP, m, mean)
    d = (mean - u).astype(BF16)
    o_ref[...] = _dot(d, w_ref[...]) * scale_ref[...]


def _pool_mix(u, buf16, w_bd, scale, pos0):
    b, t, _ = u.shape
    return pl.pallas_call(
        functools.partial(_pool_kernel, t=t, pos0=pos0),
        grid=(b,),
        in_specs=[
            pl.BlockSpec((None, t, POOL_WIDTH), lambda i: (i, 0, 0)),
            pl.BlockSpec((None, 16, POOL_WIDTH), lambda i: (i, 0, 0)),
            pl.BlockSpec((POOL_WIDTH, POOL_WIDTH), lambda i: (0, 0)),
            pl.BlockSpec((1, POOL_WIDTH), lambda i: (0, 0)),
        ],
        out_specs=pl.BlockSpec((None, t, POOL_WIDTH), lambda i: (i, 0, 0)),
        out_shape=jax.ShapeDtypeStruct((b, t, POOL_WIDTH), F32),
        scratch_shapes=[pltpu.VMEM((t + 16, POOL_WIDTH), F32)],
        compiler_params=_cparams("parallel"),
        name="pool",
    )(u, buf16, w_bd, scale)


def _cpart_compute(rows_of, pe_ref, w_ref, o_ref, m):
    for c in range(2):
        acc = jnp.zeros((m + 8, KV_WIDTH), F32)
        for j in range(CMP_STRIDE):
            lhs = jnp.concatenate([rows_of(c, j), pe_ref[c, j]], axis=0)
            acc = acc + _dot(lhs.astype(BF16), w_ref[c, j])
        lane = _lane_iota((1, KV_WIDTH))
        pe_term = jnp.where(lane < LANES, acc[m:m + 1], acc[m + 1:m + 2])
        o_ref[:, c * KV_WIDTH:(c + 1) * KV_WIDTH] = acc[0:m] + pe_term


def _cpart_kernel(xk_ref, xv_ref, pe_ref, w_ref, o_ref, *, rows):
    m = rows // CMP_STRIDE
    x_refs = (xk_ref, xv_ref)
    _cpart_compute(lambda c, j: x_refs[c][pl.ds(j, m, stride=CMP_STRIDE), :], pe_ref, w_ref, o_ref, m)


def _cpart_paged_kernel(*refs, nop, page):
    page_refs = refs[2:2 + nop]
    perm_ref, pe_ref, w_ref, o_ref, xs_ref = refs[2 + nop:]
    n = page // CMP_STRIDE
    for k, r in enumerate(page_refs):
        y = _dot_nt(perm_ref[...], r[...].astype(BF16))
        for j in range(CMP_STRIDE):
            for c in range(2):
                xs_ref[c, j, k * n:(k + 1) * n, :] = y[j * n:(j + 1) * n, c * LANES:(c + 1) * LANES]
    _cpart_compute(lambda c, j: xs_ref[c, j], pe_ref, w_ref, o_ref, nop * n)


def _cmp_partials_dense(kv, pe, w_c):
    b, t, _ = kv.shape
    rows = (t // CMP_STRIDE) * CMP_STRIDE
    n = rows // CMP_STRIDE
    return pl.pallas_call(
        functools.partial(_cpart_kernel, rows=rows),
        grid=(b,),
        in_specs=[
            pl.BlockSpec((None, rows, LANES), lambda i: (i, 0, 0)),
            pl.BlockSpec((None, rows, LANES), lambda i: (i, 0, 1)),
            pl.BlockSpec(pe.shape, lambda i: (0, 0, 0, 0)),
            pl.BlockSpec(w_c.shape, lambda i: (0, 0, 0, 0)),
        ],
        out_specs=pl.BlockSpec((None, n, 2 * KV_WIDTH), lambda i: (i, 0, 0)),
        out_shape=jax.ShapeDtypeStruct((b, n, 2 * KV_WIDTH), F32),
        compiler_params=_cparams("parallel"),
        name="cmp_partials",
    )(kv, kv, pe, w_c)


def _cmp_partials_paged(pool_t, phys, lblk, pe, w_c, pages_per_step):
    b, n_pages = phys.shape
    page = LANES
    nop = pages_per_step
    n = nop * page // CMP_STRIDE

    def page_spec(k):
        return pl.BlockSpec((None, KV_WIDTH, page),
                            lambda i, c, ph, lb: (ph[i, c * nop + k], 0, lb[i, c * nop + k]))

    r_idx = jnp.arange(page)
    regroup = (r_idx[None, :] == ((r_idx % (page // CMP_STRIDE)) * CMP_STRIDE + r_idx // (page // CMP_STRIDE))[:, None])
    regroup = regroup.astype(BF16)
    grid_spec = pltpu.PrefetchScalarGridSpec(
        num_scalar_prefetch=2,
        grid=(b, n_pages // nop),
        in_specs=[page_spec(k) for k in range(nop)] + [
            pl.BlockSpec(regroup.shape, lambda i, c, ph, lb: (0, 0)),
            pl.BlockSpec(pe.shape, lambda i, c, ph, lb: (0, 0, 0, 0)),
            pl.BlockSpec(w_c.shape, lambda i, c, ph, lb: (0, 0, 0, 0)),
        ],
        out_specs=pl.BlockSpec((None, n, 2 * KV_WIDTH), lambda i, c, ph, lb: (i, c, 0)),
        scratch_shapes=[pltpu.VMEM((2, CMP_STRIDE, n, LANES), F32)],
    )
    return pl.pallas_call(
        functools.partial(_cpart_paged_kernel, nop=nop, page=page),
        grid_spec=grid_spec,
        out_shape=jax.ShapeDtypeStruct((b, n_pages * page // CMP_STRIDE, 2 * KV_WIDTH), F32),
        compiler_params=_cparams("parallel", "arbitrary"),
        name="cmp_partials_paged",
    )(phys, lblk, *([pool_t] * nop), regroup, pe, w_c)


def _group_query_columns(q_ref, g, t):
    zeros64 = jnp.zeros((HEAD_DIM, t), BF16)
    cols = []
    for pr in range(2):
        qt = q_ref[:, (2 * g + pr) * LANES:(2 * g + pr + 1) * LANES].astype(F32).T.astype(BF16)
        for half in range(2):
            qh = qt[half * HEAD_DIM:(half + 1) * HEAD_DIM]
            cols.append(jnp.concatenate([qh, zeros64] if g == 0 else [zeros64, qh], axis=0))
    return jnp.concatenate(cols, axis=1)


def _store_group_output(o_ref, out_t, g, t):
    for pr in range(2):
        pair = jnp.concatenate([out_t[:, (2 * pr) * t:(2 * pr + 1) * t], out_t[:, (2 * pr + 1) * t:(2 * pr + 2) * t]],
                               axis=0)
        o_ref[:, (2 * g + pr) * LANES:(2 * g + pr + 1) * LANES] = pair.T


def _cattn_kernel(q_ref, p_ref, bias_ref, gn_ref, gk_ref, seg_ref, o_ref, ns0_ref, ns1_ref, *,
                  tq, n_sub, n_cmp, n_slc, nslp, pos0):
    qi = pl.program_id(0)
    pall = p_ref[...]
    kraw = pall[:, 0:LANES] + pltpu.roll(pall[:, LANES:2 * LANES], n_sub - 1, 0)
    vc = pall[:, 2 * LANES:3 * LANES] + pltpu.roll(pall[:, 3 * LANES:4 * LANES], n_sub - 1, 0)
    ms = _split_dot(kraw * kraw, seg_ref[...])
    kc = ((kraw * lax.rsqrt(ms + EPS)) * gk_ref[...]).astype(BF16)
    vct = vc.T.astype(BF16)

    nsel = -(-n_slc // 8) * 8
    jj = _row_iota((nsel, n_sub))
    nn = _lane_iota((nsel, n_sub))
    covers_t = ((nn * CMP_STRIDE < (jj + 1) * SLC_BLOCK) & (nn * CMP_STRIDE + CMP_BLOCK - 1 >= jj * SLC_BLOCK)
                & (nn < n_cmp) & (jj < n_slc))
    covers_t = jnp.where(covers_t, 1.0, 0.0).astype(BF16)
    qpos = pos0 + qi * tq + _lane_iota((1, tq))
    qblk = jnp.right_shift(qpos, SLC_BLOCK.bit_length() - 1)
    jr = _row_iota((nsel, tq))
    jrf = jr.astype(F32)
    forced = (jr == 0) | (jr == qblk) | (jr == qblk - 1)
    causal = jr <= qblk
    gnt = gn_ref[...].T

    for g in range(NSA_KV_HEADS):
        bias = bias_ref[g]
        s = _dot(kc, _group_query_columns(q_ref, g, tq)) + bias
        m = jnp.max(s, axis=0, keepdims=True)
        m = jnp.where(m > 0.5 * NEG, m, 0.0)
        e = jnp.where(bias > 0.5 * NEG, jnp.exp(s - m), 0.0)
        p = e / jnp.maximum(jnp.sum(e, axis=0, keepdims=True), 1e-30)
        h0 = NSA_HPG * g
        gate = jnp.concatenate([gnt[h0 + c:h0 + c + 1] for c in range(NSA_HPG)], axis=1)
        out_t = _dot(vct[g * HEAD_DIM:(g + 1) * HEAD_DIM], p.astype(BF16)) * gate
        _store_group_output(o_ref, out_t, g, tq)

        prsum = p[:, 0:tq] + p[:, tq:2 * tq] + p[:, 2 * tq:3 * tq] + p[:, 3 * tq:4 * tq]
        hi = prsum.astype(BF16)
        lo = (prsum - hi.astype(F32)).astype(BF16)
        imp = _dot(covers_t, hi) + _dot(covers_t, lo)
        score = jnp.where(forced, jnp.inf, imp)
        score = jnp.where(causal, score, -jnp.inf)
        sel = jnp.zeros((nsel, tq), F32)
        for _ in range(min(SLC_TOPK, n_slc)):
            m = jnp.max(score, axis=0, keepdims=True)
            idx = jnp.min(jnp.where(score == m, jrf, 1e9), axis=0, keepdims=True)
            pick = jrf == idx
            sel = jnp.where(pick & (m > -jnp.inf), 1.0, sel)
            score = jnp.where(pick, -jnp.inf, score)
        ns = 1.0 - sel
        if nslp > nsel:
            ns = jnp.concatenate([ns, jnp.ones((nslp - nsel, tq), F32)], axis=0)
        (ns0_ref if g == 0 else ns1_ref)[...] = ns


def _cmp_bias_cols(bias, tq):
    _, s, n_sub = bias.shape
    bias_t = bias.reshape(NSA_KV_HEADS, NSA_HPG, s // tq, tq, n_sub).transpose(0, 4, 2, 1, 3)
    return bias_t.reshape(NSA_KV_HEADS, n_sub, NSA_HPG * s)


def _cmp_attention(q, parts, bias_t, gn, gk, *, tq, n_cmp, n_slc, pos0):
    b, s, _ = q.shape
    n_sub = parts.shape[1]
    nslp = -(-n_slc // LANES) * LANES
    kern = functools.partial(_cattn_kernel, tq=tq, n_sub=n_sub, n_cmp=n_cmp, n_slc=n_slc, nslp=nslp, pos0=pos0)
    return pl.pallas_call(
        kern,
        grid=(s // tq, b),
        in_specs=[
            pl.BlockSpec((None, tq, NSA_WIDTH), lambda i, j: (j, i, 0)),
            pl.BlockSpec((None, n_sub, 2 * KV_WIDTH), lambda i, j: (j, 0, 0)),
            pl.BlockSpec((NSA_KV_HEADS, n_sub, NSA_HPG * tq), lambda i, j: (0, 0, i)),
            pl.BlockSpec((None, tq, LANES), lambda i, j: (j, i, 0)),
            pl.BlockSpec((1, LANES), lambda i, j: (0, 0)),
            pl.BlockSpec((LANES, LANES), lambda i, j: (0, 0)),
        ],
        out_specs=[
            pl.BlockSpec((None, tq, NSA_WIDTH), lambda i, j: (j, i, 0)),
            pl.BlockSpec((None, nslp, tq), lambda i, j: (j, 0, i)),
            pl.BlockSpec((None, nslp, tq), lambda i, j: (j, 0, i)),
        ],
        out_shape=[
            jax.ShapeDtypeStruct((b, s, NSA_WIDTH), F32),
            jax.ShapeDtypeStruct((b, nslp, s), F32),
            jax.ShapeDtypeStruct((b, nslp, s), F32),
        ],
        compiler_params=_cparams("parallel", "parallel"),
        name="cmp_attention",
    )(q, parts, bias_t, gn, gk, _seg_matrix(LANES))


def _flash_kernel(q_ref, ns0_ref, ns1_ref, kv_ref, vt_ref, tab_ref, gn_ref, o_ref, *, t, use_sel, band, gate_base):
    qi = pl.program_id(1)
    row_k = _row_iota((t, LANES))
    lane_k = _lane_iota((t, LANES))
    lo_tile = jnp.maximum(qi - band, 0) if band is not None else 0
    gnt = gn_ref[...].T

    qts = []
    for g in range(NSA_KV_HEADS):
        qt_g = _group_query_columns(q_ref, g, t)
        if use_sel:
            nst = (ns0_ref if g == 0 else ns1_ref)[...].astype(BF16)
            qt_g = jnp.concatenate([qt_g, jnp.concatenate([nst] * NSA_HPG, axis=1)], axis=0)
        qts.append(qt_g)

    def body(kj, carry):
        k0 = pl.multiple_of(kj * t, t)
        kk = kv_ref[pl.ds(k0, t), :].astype(BF16)
        if use_sel:
            blk = kj * (t // SLC_BLOCK) + jnp.right_shift(row_k, SLC_BLOCK.bit_length() - 1)
            onehot = jnp.where(lane_k == blk, -(2.0 ** 30), 0.0).astype(BF16)
            kk = jnp.concatenate([kk, onehot], axis=1)
        delta = qi - kj
        if band is None:
            kind = jnp.minimum(delta, 2)
        else:
            kind = jnp.where(delta < 2, delta, jnp.where(delta < band, 2, 3))
        m_old, l_old, acc = carry
        s = _dot(kk, qt_all) + tab_ref[kind]
        m_new = jnp.maximum(m_old, jnp.max(s, axis=0, keepdims=True))
        alpha = jnp.exp(m_old - m_new)
        p = jnp.exp(s - m_new)
        l_new = alpha * l_old + jnp.sum(p, axis=0, keepdims=True)
        pb = p.astype(BF16)
        pv = jnp.concatenate(
            [_dot(vt_ref[g * HEAD_DIM:(g + 1) * HEAD_DIM, pl.ds(k0, t)].astype(BF16), pb[:, g * gw:(g + 1) * gw])
             for g in range(NSA_KV_HEADS)], axis=1)
        return m_new, l_new, acc * alpha + pv

    gw = NSA_HPG * t
    qt_all = jnp.concatenate(qts, axis=1)
    init = (jnp.full((1, NSA_HEADS * t), NEG, F32), jnp.zeros((1, NSA_HEADS * t), F32),
            jnp.zeros((HEAD_DIM, NSA_HEADS * t), F32))
    _, l_fin, acc = lax.fori_loop(lo_tile, qi + 1, body, init)
    gate = jnp.concatenate([gnt[gate_base + h:gate_base + h + 1] for h in range(NSA_HEADS)], axis=1)
    out = acc * (gate / l_fin)
    for g in range(NSA_KV_HEADS):
        _store_group_output(o_ref, out[:, g * gw:(g + 1) * gw], g, t)


def _flash_attention(q, ns0, ns1, k_rows, kv_t, tab, gn, *, use_sel, band, gate_base):
    b, s, _ = q.shape
    t = ATT_TILE
    assert ns0.shape[1] == LANES
    kern = functools.partial(_flash_kernel, t=t, use_sel=use_sel, band=band, gate_base=gate_base)
    tile = lambda w: pl.BlockSpec((None, t, w), lambda i, j: (i, j, 0))
    ns_tile = pl.BlockSpec((None, LANES, t), lambda i, j: (i, 0, j))
    return pl.pallas_call(
        kern,
        grid=(b, s // t),
        in_specs=[
            tile(NSA_WIDTH), ns_tile, ns_tile,
            pl.BlockSpec((None, s, LANES), lambda i, j: (i, 0, 0)),
            pl.BlockSpec((None, LANES, s), lambda i, j: (i, 1, 0)),
            pl.BlockSpec(tab.shape, lambda i, j: (0, 0, 0)),
            tile(LANES),
        ],
        out_specs=tile(NSA_WIDTH),
        out_shape=jax.ShapeDtypeStruct((b, s, NSA_WIDTH), F32),
        compiler_params=_cparams("parallel", "parallel"),
        name="flash_sel" if use_sel else "flash_win",
    )(q, ns0, ns1, k_rows, kv_t, tab, gn)


def _memattn_kernel(q_ref, kv_ref, o_ref):
    lane = _lane_iota((kv_ref.shape[0], LANES))
    for pr in range(MEM_HEADS // 2):
        qpair = q_ref[:, pr * LANES:(pr + 1) * LANES]
        kblk = kv_ref[:, pr * LANES:(pr + 1) * LANES]
        vblk = kv_ref[:, MEM_WIDTH + pr * LANES:MEM_WIDTH + (pr + 1) * LANES]
        out = None
        for half in range(2):
            keep = (lane < HEAD_DIM) if half == 0 else (lane >= HEAD_DIM)
            kk = jnp.where(keep, kblk, 0.0).astype(BF16)
            vv = jnp.where(keep, vblk, 0.0).astype(BF16)
            s = _dot_nt(qpair, kk)
            m = jnp.max(s, axis=1, keepdims=True)
            e = jnp.exp(s - m)
            p = e / jnp.sum(e, axis=1, keepdims=True)
            o = _dot(p.astype(BF16), vv)
            out = o if out is None else out + o
        o_ref[:, pr * LANES:(pr + 1) * LANES] = out


def _mem_attention(qm, mem_kv, tq):
    b, s, _ = qm.shape
    m = mem_kv.shape[1]
    return pl.pallas_call(
        _memattn_kernel,
        grid=(b, s // tq),
        in_specs=[
            pl.BlockSpec((None, tq, MEM_WIDTH), lambda i, j: (i, j, 0)),
            pl.BlockSpec((None, m, 2 * MEM_WIDTH), lambda i, j: (i, 0, 0)),
        ],
        out_specs=pl.BlockSpec((None, tq, MEM_WIDTH), lambda i, j: (i, j, 0)),
        out_shape=jax.ShapeDtypeStruct((b, s, MEM_WIDTH), F32),
        compiler_params=_cparams("parallel", "parallel"),
        name="mem_attention",
    )(qm, mem_kv)


def _dec_kernel(*refs, n_pg, pps, has_new):
    bidx_ref = refs[2]
    page_refs = refs[3:3 + pps]
    new_ref, wq_ref, bias_ref, ns_ref, gate_ref, o_ref, acc_ref, m_ref, l_ref = refs[3 + pps:]
    c = pl.program_id(1)
    n_chunks = pl.num_programs(1)

    @pl.when(c == 0)
    def _():
        acc_ref[...] = jnp.zeros_like(acc_ref)
        m_ref[...] = jnp.full_like(m_ref, NEG)
        l_ref[...] = jnp.zeros_like(l_ref)

    rk = _row_iota((LANES, LANES))

    def step(tiles, first_page):
        feats = (tiles[0] if len(tiles) == 1 else jnp.concatenate(tiles, axis=1)).astype(BF16)
        s = lax.dot_general(feats, wq_ref[...], (((0,), (1,)), ((), ())),
                            preferred_element_type=F32)
        extra = []
        for k in range(len(tiles)):
            pg = first_page + k
            ns = jnp.where(rk < SLC_BLOCK, ns_ref[pl.ds(2 * pg, 1), :], ns_ref[pl.ds(2 * pg + 1, 1), :])
            extra.append(jnp.where(ns > 0.5, NEG, bias_ref[bidx_ref[pg]]))
        s = s + (extra[0] if len(extra) == 1 else jnp.concatenate(extra, axis=0))
        m_old = m_ref[...]
        m_new = jnp.maximum(m_old, jnp.max(s, axis=0, keepdims=True))
        alpha = jnp.exp(m_old - m_new)
        p = jnp.exp(s - m_new)
        l_ref[...] = alpha * l_ref[...] + jnp.sum(p, axis=0, keepdims=True)
        m_ref[...] = m_new
        acc_ref[...] = acc_ref[...] * alpha + _dot(feats, p.astype(BF16))

    if has_new:
        @pl.when(c < n_chunks - 1)
        def _():
            step([r[...] for r in page_refs], c * pps)

        @pl.when(c == n_chunks - 1)
        def _():
            step([r[...] for r in page_refs] + [new_ref[...]], n_pg - pps)
    else:
        step([r[...] for r in page_refs], c * pps)

    @pl.when(c == n_chunks - 1)
    def _():
        o_ref[...] = acc_ref[...] / l_ref[...] * gate_ref[...]


def _decode_attention(pages, phys, lblk, new_rows, wq, bias_tab, bias_idx, notsel, gate):
    bsz, n_pg = phys.shape
    w = pages.shape[1]
    has_new = new_rows is not None
    pps = math.gcd(n_pg, DEC_PAGES_PER_STEP)
    n_steps = n_pg // pps
    n_chunks = n_steps
    if not has_new:
        new_rows = jnp.zeros((1, w, LANES), F32)
    new_map = (lambda i, c, ph, lb, bi: (i, 0, 0)) if has_new else (lambda i, c, ph, lb, bi: (0, 0, 0))

    def page_spec(k):
        def index(i, c, ph, lb, bi):
            return (ph[i, c * pps + k], 0, lb[i, c * pps + k])
        return pl.BlockSpec((None, w, LANES), index)

    per_b = lambda i, c, ph, lb, bi: (i, 0, 0)
    grid_spec = pltpu.PrefetchScalarGridSpec(
        num_scalar_prefetch=3,
        grid=(bsz, n_chunks),
        in_specs=[page_spec(k) for k in range(pps)] + [
            pl.BlockSpec((None, w, LANES), new_map),
            pl.BlockSpec((None, LANES, w), per_b),
            pl.BlockSpec(bias_tab.shape, lambda i, c, ph, lb, bi: (0, 0, 0)),
            pl.BlockSpec((None, notsel.shape[1], LANES), per_b),
            pl.BlockSpec((None, 1, LANES), per_b),
        ],
        out_specs=pl.BlockSpec((None, w, LANES), per_b),
        scratch_shapes=[pltpu.VMEM((w, LANES), F32), pltpu.VMEM((1, LANES), F32), pltpu.VMEM((1, LANES), F32)],
    )
    return pl.pallas_call(
        functools.partial(_dec_kernel, n_pg=n_pg, pps=pps, has_new=has_new),
        grid_spec=grid_spec,
        out_shape=jax.ShapeDtypeStruct((bsz, w, LANES), F32),
        compiler_params=_cparams("parallel", "arbitrary"),
        name="decode_attention",
    )(phys, lblk, bias_idx, *([pages] * pps), new_rows, wq, bias_tab, notsel, gate)


def _feature_major(cache):
    n, rows = cache.shape[:2]
    return cache.transpose(0, 2, 3, 4, 1).reshape(n, -1, rows)


def _tail_kernel(x_ref, g1_ref, op_ref, oc_ref, os_ref, ow_ref, om_ref, cnt0_ref, wgb_ref, wup_p_ref, wup_n_ref,
                 wup_m_ref, wout_ref, g2_ref, rw_ref, rb_ref, x2_ref, h2_ref, ei_ref, gt_ref, cnt_ref):
    @pl.when(pl.program_id(0) == 0)
    def _():
        cnt_ref[...] = cnt0_ref[...]

    x = x_ref[...]
    h = _rms(x, g1_ref[...]).astype(BF16)
    onsa = (oc_ref[...] + os_ref[...] + ow_ref[...]).astype(BF16)
    ups = (_dot(op_ref[...].astype(BF16), wup_p_ref[...]), _dot(onsa, wup_n_ref[...]),
           _dot(om_ref[...].astype(BF16), wup_m_ref[...]))
    mixed = None
    for br in range(3):
        gb = _sigmoid(_dot(h, wgb_ref[:, br * D_MODEL:(br + 1) * D_MODEL]))
        mixed = gb * ups[br] if mixed is None else mixed + gb * ups[br]
    x2 = x + _dot(mixed.astype(BF16), wout_ref[...])
    x2_ref[...] = x2
    h2 = _rms(x2, g2_ref[...])
    h2_ref[...] = _pack_bf16_pairs(h2)
    logits = _dot(h2.astype(BF16), rw_ref[...]) + rb_ref[...]
    lane = _lane_iota(logits.shape)
    lanef = lane.astype(F32)
    tops, idxs = [], []
    for _ in range(TOP_K):
        m = jnp.max(logits, axis=1, keepdims=True)
        idx = jnp.min(jnp.where(logits == m, lanef, 1e9), axis=1, keepdims=True)
        logits = jnp.where(lanef == idx, -jnp.inf, logits)
        tops.append(m)
        idxs.append(idx)
    es = [jnp.exp(tk - tops[0]) for tk in tops]
    den = es[0] + es[1] + es[2] + es[3]
    tm = logits.shape[0]
    onehot = jnp.zeros(logits.shape, F32)
    for k in range(TOP_K):
        onehot = jnp.where(lanef == idxs[k], 1.0, onehot)
    tri = jnp.where(_row_iota((tm, tm)) > _lane_iota((tm, tm)), 1.0, 0.0).astype(BF16)
    before = _dot(tri, onehot.astype(BF16)) + cnt_ref[...]
    cnt_ref[...] = cnt_ref[...] + jnp.sum(onehot, axis=0, keepdims=True)
    ei = jnp.zeros(logits.shape, F32)
    gt = jnp.zeros(logits.shape, F32)
    for k in range(TOP_K):
        rank = jnp.sum(jnp.where(lanef == idxs[k], before, 0.0), axis=1, keepdims=True)
        ei = jnp.where(lane == k, idxs[k], ei)
        ei = jnp.where(lane == TOP_K + k, rank, ei)
        gt = jnp.where(lane == k, es[k] / den, gt)
    ei_ref[...] = ei.astype(jnp.int32)
    gt_ref[...] = gt


def _layer_tail(x, o_pool, o_cmp, o_slc, o_win, o_mem, cnt0, w, tm):
    n = x.shape[0]
    row = lambda wd: pl.BlockSpec((tm, wd), lambda i: (i, 0))
    full = lambda a: pl.BlockSpec(a.shape, lambda i: (0,) * a.ndim)
    weights = (w["wgb"], w["wup_pool"], w["wup_nsa"], w["wup_mem"], w["wout"], w["g2"], w["rw"], w["rb"])
    return pl.pallas_call(
        _tail_kernel,
        grid=(n // tm,),
        in_specs=[row(D_MODEL), full(w["g1"]), row(POOL_WIDTH), row(NSA_WIDTH), row(NSA_WIDTH), row(NSA_WIDTH),
                  row(MEM_WIDTH), full(cnt0)] + [full(a) for a in weights],
        out_specs=[row(D_MODEL), row(D_MODEL // 2), row(LANES), row(LANES), full(cnt0)],
        out_shape=[jax.ShapeDtypeStruct((n, D_MODEL), F32), jax.ShapeDtypeStruct((n, D_MODEL // 2), jnp.uint32),
                   jax.ShapeDtypeStruct((n, LANES), jnp.int32), jax.ShapeDtypeStruct((n, LANES), F32),
                   jax.ShapeDtypeStruct((1, LANES), F32)],
        compiler_params=_cparams("arbitrary"),
        name="layer_tail",
    )(x, w["g1"], o_pool, o_cmp, o_slc, o_win, o_mem, cnt0, *weights)


def _ffn_kernel(be_ref, nu_ref, x_ref, wgu_ref, bgu_ref, wd_ref, bd_ref, o_ref, wgu_bf, wd_bf):
    i = pl.program_id(0)

    @pl.when((i == 0) | (be_ref[i] != be_ref[jnp.maximum(i - 1, 0)]))
    def _():
        wgu_bf[...] = wgu_ref[...].astype(BF16)
        wd_bf[...] = wd_ref[...].astype(BF16)

    @pl.when(i < nu_ref[0])
    def _():
        gu = _dot(_unpack_bf16_pairs(x_ref[...]).astype(BF16), wgu_bf[...]) + bgu_ref[...]
        gate = jnp.minimum(gu[:, :D_FF], SWIGLU_LIMIT)
        up = jnp.clip(gu[:, D_FF:], -SWIGLU_LIMIT, SWIGLU_LIMIT)
        act = gate * _sigmoid(SWIGLU_ALPHA * gate) * (up + 1.0)
        o_ref[...] = _dot(act.astype(BF16), wd_bf[...]) + bd_ref[...]

    @pl.when(i >= nu_ref[0])
    def _():
        o_ref[...] = jnp.zeros_like(o_ref)


def _expert_ffn(rows, blk_e, n_used, wgu, bgu, wd, bd):
    n_rows = rows.shape[0]
    n_blocks = n_rows // MOE_TILE
    blk = lambda i, be, nu: (jnp.minimum(i, nu[0] - 1), 0)
    grid_spec = pltpu.PrefetchScalarGridSpec(
        num_scalar_prefetch=2,
        grid=(n_blocks,),
        in_specs=[
            pl.BlockSpec((MOE_TILE, D_MODEL // 2), blk),
            pl.BlockSpec((None, D_MODEL, 2 * D_FF), lambda i, be, nu: (be[i], 0, 0)),
            pl.BlockSpec((None, 1, 2 * D_FF), lambda i, be, nu: (be[i], 0, 0)),
            pl.BlockSpec((None, D_FF, D_MODEL), lambda i, be, nu: (be[i], 0, 0)),
            pl.BlockSpec((None, 1, D_MODEL), lambda i, be, nu: (be[i], 0, 0)),
        ],
        out_specs=pl.BlockSpec((MOE_TILE, D_MODEL), lambda i, be, nu: (i, 0)),
        scratch_shapes=[pltpu.VMEM((D_MODEL, 2 * D_FF), BF16), pltpu.VMEM((D_FF, D_MODEL), BF16)],
    )
    return pl.pallas_call(
        _ffn_kernel,
        grid_spec=grid_spec,
        out_shape=jax.ShapeDtypeStruct((n_rows, D_MODEL), F32),
        compiler_params=pltpu.CompilerParams(dimension_semantics=("arbitrary",), vmem_limit_bytes=FFN_VMEM_LIMIT),
        name="expert_ffn",
    )(blk_e, n_used, rows, wgu, bgu, wd, bd)


def _token_tile(n):
    return next(t for t in (MOE_DMA_TOKENS, 384, 256, 128, n) if n % t == 0)


def _dispatch_scatter(dest_ref, h_ref, rows_ref, sem, td):
    def issue(i, carry):
        t0 = pl.multiple_of(i * 8, 8)
        for r in range(8):
            for k in range(TOP_K):
                d = dest_ref[i * (8 * TOP_K) + r * TOP_K + k]
                pltpu.make_async_copy(h_ref.at[pl.ds(t0 + r, 1)], rows_ref.at[pl.ds(d, 1)], sem).start()
        return carry

    lax.fori_loop(0, td // 8, issue, 0)
    for k in range(TOP_K):
        pltpu.make_async_copy(h_ref, rows_ref.at[pl.ds(0, td)], sem).wait()


def _dispatch_kernel(*refs, tiles, steps, n_blocks):
    ng = len(tiles)
    ends_ref, padded_ref, nu_ref = refs[:3]
    dest_refs = refs[3:3 + ng]
    h_refs = refs[3 + ng:3 + 2 * ng]
    rows_ref, zero_ref, sem, zsem = refs[3 + 2 * ng:]
    i = pl.program_id(0)

    @pl.when(i == 0)
    def _():
        zero_ref[...] = jnp.zeros_like(zero_ref)

        def tail_copy(e):
            start = pl.multiple_of(ends_ref[e] - MOE_TILE, MOE_TILE)
            return pltpu.make_async_copy(zero_ref, rows_ref.at[pl.ds(start, MOE_TILE)], zsem)

        def block_copy(blk):
            start = pl.multiple_of(blk * MOE_TILE, MOE_TILE)
            return pltpu.make_async_copy(zero_ref, rows_ref.at[pl.ds(start, MOE_TILE)], zsem)

        def each(start):
            def expert(e, carry):
                @pl.when(padded_ref[e] > 0)
                def _():
                    tail_copy(e).start() if start else tail_copy(e).wait()
                return carry

            def block(blk, carry):
                block_copy(blk).start() if start else block_copy(blk).wait()
                return carry

            lax.fori_loop(0, N_EXPERTS, expert, 0)
            lax.fori_loop(nu_ref[0], n_blocks, block, 0)

        each(True)
        each(False)

    first = 0
    for g in range(ng):
        @pl.when((i >= first) & (i < first + steps[g]))
        def _(g=g):
            _dispatch_scatter(dest_refs[g], h_refs[g], rows_ref, sem, tiles[g])
        first += steps[g]


def _moe_dispatch(h2s, dests, n_rows, pad_ends, padded, n_used):
    tiles = [_token_tile(h.shape[0]) for h in h2s]
    steps = [h.shape[0] // t for h, t in zip(h2s, tiles)]
    firsts = [sum(steps[:g]) for g in range(len(h2s))]

    def local(g):
        return lambda i, *_: jnp.clip(i - firsts[g], 0, steps[g] - 1)

    dest_specs = [pl.BlockSpec((tiles[g] * TOP_K,), lambda i, *_, f=local(g): (f(i),), memory_space=pltpu.SMEM)
                  for g in range(len(h2s))]
    width, dtype = h2s[0].shape[1], h2s[0].dtype
    tok_specs = [pl.BlockSpec((tiles[g], width), lambda i, *_, f=local(g): (f(i), 0)) for g in range(len(h2s))]
    grid_spec = pltpu.PrefetchScalarGridSpec(
        num_scalar_prefetch=3,
        grid=(sum(steps),),
        in_specs=dest_specs + tok_specs,
        out_specs=pl.BlockSpec(memory_space=pl.ANY),
        scratch_shapes=[pltpu.VMEM((MOE_TILE, width), dtype), pltpu.SemaphoreType.DMA(()),
                        pltpu.SemaphoreType.DMA(())],
    )
    return pl.pallas_call(
        functools.partial(_dispatch_kernel, tiles=tiles, steps=steps, n_blocks=n_rows // MOE_TILE),
        grid_spec=grid_spec,
        out_shape=jax.ShapeDtypeStruct((n_rows, width), dtype),
        compiler_params=_cparams("arbitrary"),
        name="moe_dispatch",
    )(pad_ends, padded, n_used, *dests, *h2s)


def _combine_kernel(dest_ref, dnext_ref, x2_ref, g_ref, rows_ref, o_ref, ybuf, sem, *, td):
    i = pl.program_id(0)
    last = pl.num_programs(0) - 1

    def start_tile(d_ref, slot):
        def issue(blk, carry):
            t0 = pl.multiple_of(blk * 8, 8)
            for r in range(8):
                for k in range(TOP_K):
                    d = d_ref[blk * (8 * TOP_K) + r * TOP_K + k]
                    pltpu.make_async_copy(rows_ref.at[pl.ds(d, 1)], ybuf.at[slot, k, pl.ds(t0 + r, 1)],
                                          sem.at[slot]).start()
            return carry

        lax.fori_loop(0, td // 8, issue, 0)

    @pl.when(i == 0)
    def _():
        start_tile(dest_ref, 0)

    @pl.when(i < last)
    def _():
        start_tile(dnext_ref, (i + 1) % 2)

    slot = i % 2
    for k in range(TOP_K):
        pltpu.make_async_copy(rows_ref.at[pl.ds(0, td)], ybuf.at[slot, k], sem.at[slot]).wait()
    out = x2_ref[...]
    for k in range(TOP_K):
        out = out + g_ref[:, k:k + 1] * ybuf[slot, k]
    o_ref[...] = out


def _moe_combine(x2, out_rows, dest, gates):
    n = x2.shape[0]
    td = _token_tile(n)
    steps = n // td
    return pl.pallas_call(
        functools.partial(_combine_kernel, td=td),
        grid=(steps,),
        in_specs=[pl.BlockSpec((td * TOP_K,), lambda i: (i,), memory_space=pltpu.SMEM),
                  pl.BlockSpec((td * TOP_K,), lambda i: (jnp.minimum(i + 1, steps - 1),), memory_space=pltpu.SMEM),
                  pl.BlockSpec((td, D_MODEL), lambda i: (i, 0)),
                  pl.BlockSpec((td, LANES), lambda i: (i, 0)),
                  pl.BlockSpec(memory_space=pl.ANY)],
        out_specs=pl.BlockSpec((td, D_MODEL), lambda i: (i, 0)),
        out_shape=jax.ShapeDtypeStruct((n, D_MODEL), F32),
        scratch_shapes=[pltpu.VMEM((2, TOP_K, td) + out_rows.shape[1:], out_rows.dtype),
                        pltpu.SemaphoreType.DMA((2,))],
        compiler_params=_cparams("arbitrary"),
        name="moe_combine",
    )(dest, dest, x2, gates, out_rows)


def _moe(groups, counts, w):
    n_total = sum(g[0].shape[0] for g in groups)
    cnt = counts[0, :N_EXPERTS].astype(jnp.int32)
    padded = (cnt + MOE_TILE - 1) // MOE_TILE * MOE_TILE
    pad_ends = jnp.cumsum(padded)
    pad_starts = pad_ends - padded
    n_blocks = -(-n_total * TOP_K // MOE_TILE) + N_EXPERTS
    blk_start = jnp.arange(n_blocks, dtype=jnp.int32) * MOE_TILE
    blk_e = jnp.minimum(jnp.sum(blk_start[:, None] >= pad_ends[None, :], axis=1), N_EXPERTS - 1).astype(jnp.int32)
    n_used = (pad_ends[-1] // MOE_TILE).astype(jnp.int32).reshape(1)
    dests = [(pad_starts[er[:, :TOP_K]] + er[:, TOP_K:2 * TOP_K]).astype(jnp.int32).reshape(-1)
             for (_, _, er, _) in groups]
    rows = _moe_dispatch([g[1] for g in groups], dests, n_blocks * MOE_TILE, pad_ends.astype(jnp.int32),
                         padded.astype(jnp.int32), n_used)
    out_rows = _expert_ffn(rows, blk_e, n_used, w["wgu"], w["bgu"], w["wd"], w["bd"])
    return [_moe_combine(x2, out_rows, dest, gates) for (x2, _, _, gates), dest in zip(groups, dests)]


def _rel_bucket(dist):
    n = jnp.maximum(dist, 0)
    max_exact = NUM_BUCKETS // 2
    nf = jnp.maximum(n, 1).astype(F32)
    large = max_exact + (jnp.log(nf / max_exact) / math.log(MAX_DISTANCE / max_exact)
                         * (NUM_BUCKETS - max_exact)).astype(jnp.int32)
    large = jnp.minimum(large, NUM_BUCKETS - 1)
    return jnp.where(n < max_exact, n, large)


def _bias_of(rel_bias, dist, valid):
    return jnp.where(valid[..., None], rel_bias[_rel_bucket(dist)], NEG)


def _proj_segs(kv_forms, kvc_forms):
    return ((0, 512, "qscale", ("rows",), BF16), (512, 256, "qscale", ("rows",), BF16),
            (768, 256, "id", kv_forms, F32), (1024, 256, "id", kv_forms, F32), (1280, 256, "id", kvc_forms, F32),
            (1536, 256, "id", ("rows",), F32), (1792, 128, "sigmoid", ("rows",), F32))


_PROJ_SEGS_PROMPT = _proj_segs(("key_rows", "t"), ("t",))
_PROJ_SEGS_SAMPLE = _proj_segs(("rows",), ("rows",))
_PROJ_NNORM = 1280


def _prep_layer(l, rel_bias, norm1_g, w_in, nsa_qk_norm, mem_qk_norm, cmp_w, cmp_pe, pool_w, pool_scale,
                mem_norm_g, w_mem_kv, w_up_pool, w_up_nsa, w_up_mem, w_out, norm2_g, router_w, router_b,
                w_gu, b_gu, w_down, b_down):
    wi = w_in[l]
    o_u, o_q, o_qm, o_kvc, o_kvs, o_kvw, o_gn, o_gb = 0, 256, 768, 1024, 1280, 1536, 1792, 1816
    w_proj = jnp.concatenate([
        wi[:, o_q:o_q + 512], wi[:, o_qm:o_qm + 256], wi[:, o_kvs:o_kvs + 256], wi[:, o_kvw:o_kvw + 256],
        wi[:, o_kvc:o_kvc + 256], wi[:, o_u:o_u + 256], wi[:, o_gn:o_gn + 24],
        jnp.zeros((D_MODEL, LANES - 24), F32)], axis=1).astype(BF16)
    nq, mq = nsa_qk_norm[l], mem_qk_norm[l]
    ones = jnp.ones((LANES,), F32)
    gain = jnp.concatenate([jnp.tile(nq[0], 8), jnp.tile(mq[0], 4), jnp.tile(nq[2], 2), ones,
                            jnp.tile(nq[3], 2), ones])[None, :]
    nmask = jnp.concatenate([jnp.ones((768,), F32), ones, 0 * ones, ones, 0 * ones])[None, :]
    eye4 = jnp.eye(4, dtype=F32)
    cw = cmp_w[l].reshape(2, 2, CMP_STRIDE, HEAD_DIM, HEAD_DIM)
    w_c = jnp.einsum("crjde,xy->cjxdrye", cw, jnp.eye(2, dtype=F32))
    w_c = w_c.reshape(2, CMP_STRIDE, LANES, KV_WIDTH).astype(BF16)
    pe = cmp_pe[l].reshape(2, 2, CMP_STRIDE, HEAD_DIM)
    pe_c = jnp.tile(pe.transpose(0, 2, 1, 3), (1, 1, 1, NSA_KV_HEADS))
    pe_c = jnp.pad(pe_c, ((0, 0), (0, 0), (0, 6), (0, 0)))
    w_pool = jnp.einsum("gde,gh->gdhe", pool_w[l], eye4).reshape(POOL_WIDTH, POOL_WIDTH).astype(BF16)
    rw = jnp.pad(router_w[l], ((0, 0), (0, LANES - N_EXPERTS))).astype(BF16)
    rb = jnp.concatenate([router_b[l], jnp.full((LANES - N_EXPERTS,), NEG, F32)])[None, :]
    return {
        "g1": norm1_g[l][None, :], "w_proj": w_proj, "gain": gain, "nmask": nmask,
        "w_kvc_t": wi[:, o_kvc:o_kvc + 256].T.astype(BF16),
        "gk_cmp": jnp.tile(nq[1], 2)[None, :], "w_c": w_c, "pe_c": pe_c,
        "w_pool": w_pool, "pool_scale": pool_scale[l][None, :],
        "mem_g": mem_norm_g[l][None, :], "w_mem": w_mem_kv[l].astype(BF16),
        "mem_gain": jnp.concatenate([jnp.tile(mq[1], 4), jnp.ones((256,), F32)])[None, :],
        "mem_nmask": jnp.concatenate([jnp.ones((256,), F32), jnp.zeros((256,), F32)])[None, :],
        "wgb": wi[:, o_gb:o_gb + 3 * D_MODEL].astype(BF16),
        "wup_pool": w_up_pool[l].astype(BF16), "wup_nsa": w_up_nsa[l].astype(BF16),
        "wup_mem": w_up_mem[l].astype(BF16), "wout": w_out[l].astype(BF16), "g2": norm2_g[l][None, :],
        "rw": rw, "rb": rb,
        "wgu": w_gu[l], "bgu": b_gu[l][:, None, :], "wd": w_down[l],
        "bd": b_down[l][:, None, :],
    }


def _project_in(x2d, w, segs, tm, seq):
    wt = w["w_kvc_t"] if segs is _PROJ_SEGS_PROMPT else None
    return _project(x2d, w["g1"], w["w_proj"], w["gain"], w["nmask"], segs, _PROJ_NNORM, tm, seq, wt)


def _rows_view(a_t, heads):
    b, _, rows = a_t.shape
    return a_t.reshape(b, 2, heads, HEAD_DIM, rows).transpose(0, 4, 1, 2, 3)


def _toeplitz(v, t):
    lead = v.shape[:-1]
    flat = jnp.tile(v, (1,) * len(lead) + (t,))[..., t:t + t * (2 * t - 1)]
    return flat.reshape(lead + (t, 2 * t - 1))[..., :t]


def _flash_tables(rel_bias):
    t = ATT_TILE
    d0 = jnp.arange(-t, t)
    kinds = jnp.stack([
        _bias_of(rel_bias, d0, d0 >= 0),
        _bias_of(rel_bias, d0 + t, d0 + t >= 0),
        _bias_of(rel_bias, jnp.full((2 * t,), 2 * t), jnp.ones((2 * t,), bool)),
        _bias_of(rel_bias, d0 + WINDOW, d0 + WINDOW < WINDOW),
    ])
    tab = _toeplitz(kinds.transpose(2, 0, 1), t)
    return tab.transpose(1, 2, 0, 3).reshape(4, t, NSA_HEADS * t)


def _cmp_bias_table(rel_bias, s, n_sub, n_cmp, pos0, tq):
    na = s // CMP_STRIDE
    m = max(na, n_sub)
    k = jnp.arange(-m, m)[None, :]
    r = jnp.arange(CMP_STRIDE)[:, None]
    d = CMP_STRIDE * k + r - (CMP_BLOCK - 1) + pos0
    v = _bias_of(rel_bias, d, d >= 0).transpose(2, 0, 1)
    tz = _toeplitz(v, m)[:, :, :n_sub, :na]
    tz = tz.reshape(NSA_KV_HEADS, NSA_HPG, CMP_STRIDE, n_sub, s // tq, tq // CMP_STRIDE)
    tab = tz.transpose(0, 3, 4, 1, 5, 2).reshape(NSA_KV_HEADS, n_sub, NSA_HPG * s)
    return jnp.where(jnp.arange(n_sub)[None, :, None] < n_cmp, tab, NEG)


def _prompt_pre(x, mem, w, rel_bias, cnt0):
    b, s, _ = x.shape
    n = b * s
    tm = 512 if n % 512 == 0 else ATT_TILE
    x2d = x.reshape(n, D_MODEL)
    q, qm, ks, kvs_t, kw, kvw_t, kvc_t, u, gn = _project_in(x2d, w, _PROJ_SEGS_PROMPT, tm, s)
    r3 = lambda a: a.reshape(b, s, a.shape[-1])
    q, qm, ks, kw, u, gn = map(r3, (q, qm, ks, kw, u, gn))

    o_pool = _pool_mix(u, jnp.zeros((b, 16, POOL_WIDTH), F32), w["w_pool"], w["pool_scale"], 0)

    n_cmp = (s - CMP_BLOCK) // CMP_STRIDE + 1
    n_slc = -(-s // SLC_BLOCK)
    n_lb = s // LANES
    own = jnp.broadcast_to(jnp.arange(b, dtype=jnp.int32)[:, None], (b, n_lb))
    blocks = jnp.broadcast_to(jnp.arange(n_lb, dtype=jnp.int32)[None, :], (b, n_lb))
    parts = _cmp_partials_paged(kvc_t, own, blocks, w["pe_c"], w["w_c"], math.gcd(n_lb, 16))
    n_sub = parts.shape[1]
    tq = 256 if s % 256 == 0 else ATT_TILE
    bias_c = _cmp_bias_table(rel_bias, s, n_sub, n_cmp, 0, tq)
    o_cmp, ns0, ns1 = _cmp_attention(q, parts, bias_c, gn, w["gk_cmp"], tq=tq, n_cmp=n_cmp, n_slc=n_slc, pos0=0)

    tab = _flash_tables(rel_bias)
    o_slc = _flash_attention(q, ns0, ns1, ks, kvs_t, tab, gn, use_sel=True, band=None, gate_base=NSA_HEADS)
    o_win = _flash_attention(q, ns0, ns1, kw, kvw_t, tab, gn, use_sel=False, band=WINDOW // ATT_TILE,
                             gate_base=2 * NSA_HEADS)

    m = mem.shape[1]
    mem_kv, mem_kv_t = _project(mem.reshape(b * m, D_MODEL), w["mem_g"], w["w_mem"], w["mem_gain"],
                                w["mem_nmask"], ((0, 2 * MEM_WIDTH, "id", ("rows", "t"), F32),), MXU_DIM,
                                tm=math.gcd(m, 512), seq=m)
    o_mem = _mem_attention(qm, mem_kv.reshape(b, m, 2 * MEM_WIDTH), tq=min(512, s))

    f2 = lambda a: a.reshape(n, a.shape[-1])
    x2, h2, eidx, gates, cnt = _layer_tail(x2d, f2(o_pool), f2(o_cmp), f2(o_slc), f2(o_win), f2(o_mem), cnt0, w, tm)
    win_t = kvw_t[:, :, max(0, s - WINDOW):]
    if s < WINDOW:
        win_t = jnp.pad(win_t, ((0, 0), (0, 0), (WINDOW - s, 0)))
    states = (_rows_view(kvc_t, NSA_KV_HEADS), _rows_view(kvs_t, NSA_KV_HEADS), _rows_view(win_t, NSA_KV_HEADS),
              _rows_view(mem_kv_t, MEM_HEADS), _last_rows(u, POOL_BUF))
    return (x2, h2, eidx, gates), states, cnt


def _last_rows(a, n):
    t = a.shape[1]
    if t < n:
        a = jnp.pad(a, [(0, 0), (n - t, 0)] + [(0, 0)] * (a.ndim - 2))
    return a[:, a.shape[1] - n:]


def _dec_columns_nsa(q):
    b, t, _ = q.shape
    qh = q.reshape(b, t, NSA_KV_HEADS, NSA_HPG, HEAD_DIM)
    w = jnp.einsum("btgpd,gx->bxdgtp", qh.astype(F32), jnp.eye(NSA_KV_HEADS, dtype=F32))
    w = w.reshape(b, NSA_KV_HEADS * HEAD_DIM, NSA_KV_HEADS * t * NSA_HPG)
    return jnp.pad(w, ((0, 0), (0, KV_WIDTH - w.shape[1]), (0, LANES - w.shape[2]))).astype(BF16)


def _dec_extract_nsa(o, t):
    b = o.shape[0]
    v = o[:, LANES:, :NSA_KV_HEADS * t * NSA_HPG]
    v = v.reshape(b, NSA_KV_HEADS, HEAD_DIM, NSA_KV_HEADS, t, NSA_HPG)
    v = jnp.einsum("bxdgtp,gx->btgpd", v, jnp.eye(NSA_KV_HEADS, dtype=F32))
    return v.reshape(b, t, NSA_WIDTH)


def _dec_bias_cols(bias_tph):
    k, t, _ = bias_tph.shape
    bt = bias_tph.reshape(k, t, NSA_KV_HEADS, NSA_HPG).transpose(0, 2, 1, 3).reshape(k, NSA_KV_HEADS * t * NSA_HPG)
    return jnp.pad(bt, ((0, 0), (0, LANES - bt.shape[1])))


def _dec_gate_cols(gn, base, t):
    b = gn.shape[0]
    gt = gn[:, :, base:base + NSA_HEADS].reshape(b, t, NSA_KV_HEADS, NSA_HPG).transpose(0, 2, 1, 3)
    gt = gt.reshape(b, 1, NSA_KV_HEADS * t * NSA_HPG)
    return jnp.pad(gt, ((0, 0), (0, 0), (0, LANES - gt.shape[2])), constant_values=1.0)


def _sample_pre(x, cache_cmp, cache_slc, cache_win, cache_mem, pool_buf, page_table, w, rel_bias, cnt0):
    b, t, _ = x.shape
    n = b * t
    page = cache_cmp.shape[1]
    n_pages = page_table.shape[1]
    past = n_pages * page
    x2d = x.reshape(n, D_MODEL)
    tm = n if n <= 512 else LANES
    q, qm, kvs, kvw, kvc, u, gn = _project_in(x2d, w, _PROJ_SEGS_SAMPLE, tm, tm)
    r3 = lambda a: a.reshape(b, t, a.shape[-1])
    q, qm, kvs, kvw, kvc, u, gn = map(r3, (q, qm, kvs, kvw, kvc, u, gn))
    qpos = past + jnp.arange(t)

    buf16 = jnp.pad(pool_buf, ((0, 0), (16 - POOL_BUF, 0), (0, 0)))
    o_pool = _pool_mix(u, buf16, w["w_pool"], w["pool_scale"], past)

    total = past + t
    n_cmp = (total - CMP_BLOCK) // CMP_STRIDE + 1
    n_sub_used = n_cmp + CMP_BLOCK // CMP_STRIDE - 1
    n_slc = -(-total // SLC_BLOCK)
    pps = math.gcd(n_pages, 16)
    parts = _cmp_partials_paged(_feature_major(cache_cmp), page_table, jnp.zeros_like(page_table), w["pe_c"],
                                w["w_c"], pps)
    extra = n_sub_used * CMP_STRIDE - past
    if extra > 0:
        tail_rows = -(-extra // CMP_STRIDE) * CMP_STRIDE
        new_c = jnp.pad(kvc, ((0, 0), (0, max(0, tail_rows - t)), (0, 0)))[:, :tail_rows]
        parts = jnp.concatenate([parts, _cmp_partials_dense(new_c, w["pe_c"], w["w_c"])], axis=1)
    n_sub = parts.shape[1]
    end = jnp.arange(n_sub)[None, :] * CMP_STRIDE + CMP_BLOCK - 1
    bias_c = _bias_of(rel_bias, qpos[:, None] - end, (end <= qpos[:, None]) & (jnp.arange(n_sub)[None, :] < n_cmp))
    qpad = ((0, 0), (0, LANES - t), (0, 0))
    bias_c = _cmp_bias_cols(jnp.pad(bias_c.transpose(2, 0, 1), qpad), LANES)
    o_cmp, ns0, ns1 = _cmp_attention(jnp.pad(q, qpad), parts, bias_c, jnp.pad(gn, qpad), w["gk_cmp"], tq=LANES,
                                     n_cmp=n_cmp, n_slc=n_slc, pos0=past)
    o_cmp = o_cmp[:, :t]

    wq = _dec_columns_nsa(q).transpose(0, 2, 1)
    ncol = NSA_KV_HEADS * t * NSA_HPG
    new_tile = lambda kv: jnp.pad(kv, ((0, 0), (0, LANES - t), (0, 0))).transpose(0, 2, 1)
    own = lambda npg: jnp.broadcast_to(jnp.arange(b, dtype=jnp.int32)[:, None], (b, npg))
    blocks = lambda npg: jnp.broadcast_to(jnp.arange(npg, dtype=jnp.int32)[None, :], (b, npg))

    n_chunks = n_pages + 1
    nblk = -(-2 * n_chunks // 8) * 8
    ns = jnp.stack([ns0, ns1], axis=1)[:, :, :, :t].transpose(0, 1, 3, 2)
    ns = jnp.pad(ns, ((0, 0), (0, 0), (0, 0), (0, max(0, nblk - ns.shape[3]))))[..., :nblk]
    ns = jnp.broadcast_to(ns[:, :, :, None, :], (b, NSA_KV_HEADS, t, NSA_HPG, nblk)).reshape(b, ncol, nblk)
    notsel = jnp.pad(ns.transpose(0, 2, 1), ((0, 0), (0, 0), (0, LANES - ncol)))
    rows = jnp.arange(LANES)
    far = _bias_of(rel_bias, jnp.full((LANES, t), 2 * MAX_DISTANCE), jnp.ones((LANES, t), bool))
    kpos_last = past - LANES + rows
    d_last = qpos[None, :] - kpos_last[:, None]
    near = _bias_of(rel_bias, d_last, d_last >= 0)
    kpos_new = past + rows
    d_new = qpos[None, :] - kpos_new[:, None]
    newb = _bias_of(rel_bias, d_new, (d_new >= 0) & (rows[:, None] < t))
    bias_tab = jnp.stack([_dec_bias_cols(far), _dec_bias_cols(near), _dec_bias_cols(newb)])
    bias_idx = jnp.concatenate([jnp.zeros((n_pages - 1,), jnp.int32), jnp.array([1, 2], jnp.int32)])
    o_slc = _decode_attention(_feature_major(cache_slc), page_table, jnp.zeros_like(page_table), new_tile(kvs), wq,
                              bias_tab, bias_idx, notsel, _dec_gate_cols(gn, NSA_HEADS, t))
    o_slc = _dec_extract_nsa(o_slc, t)

    wb = cache_win.shape[1]
    n_wpg = wb // LANES
    kpos_w = past - wb + jnp.arange(wb + LANES)
    d_w = qpos[None, :] - kpos_w[:, None]
    valid_w = (d_w >= 0) & (d_w < WINDOW) & (kpos_w[:, None] >= 0) & (jnp.arange(wb + LANES)[:, None] < wb + t)
    bias_w = _dec_bias_cols(_bias_of(rel_bias, d_w, valid_w)).reshape(n_wpg + 1, LANES, LANES)
    zeros_ns = jnp.zeros((b, -(-2 * (n_wpg + 1) // 8) * 8, LANES), F32)
    o_win = _decode_attention(_feature_major(cache_win), own(n_wpg), blocks(n_wpg), new_tile(kvw), wq, bias_w,
                              jnp.arange(n_wpg + 1, dtype=jnp.int32), zeros_ns, _dec_gate_cols(gn, 2 * NSA_HEADS, t))
    o_win = _dec_extract_nsa(o_win, t)

    m = cache_mem.shape[1]
    n_mpg = m // LANES
    qmh = qm.reshape(b, t, MEM_HEADS, HEAD_DIM).astype(F32)
    wqm = jnp.einsum("bthd,hx->bxdht", qmh, jnp.eye(MEM_HEADS, dtype=F32))
    wqm = wqm.reshape(b, MEM_WIDTH, MEM_HEADS * t)
    wqm = jnp.pad(wqm, ((0, 0), (0, MEM_WIDTH), (0, LANES - MEM_HEADS * t))).astype(BF16).transpose(0, 2, 1)
    o_mem = _decode_attention(_feature_major(cache_mem), own(n_mpg), blocks(n_mpg), None, wqm,
                              jnp.zeros((1, LANES, LANES), F32), jnp.zeros((n_mpg,), jnp.int32),
                              jnp.zeros((b, 8, LANES), F32), jnp.ones((b, 1, LANES), F32))
    om = o_mem[:, MEM_WIDTH:, :MEM_HEADS * t].reshape(b, MEM_HEADS, HEAD_DIM, MEM_HEADS, t)
    o_mem = jnp.einsum("bxdht,hx->bthd", om, jnp.eye(MEM_HEADS, dtype=F32)).reshape(b, t, MEM_WIDTH)

    f2 = lambda a: a.reshape(n, a.shape[-1])
    x2, h2, eidx, gates, cnt = _layer_tail(x2d, f2(o_pool), f2(o_cmp), f2(o_slc), f2(o_win), f2(o_mem), cnt0, w,
                                           n if n <= 512 else LANES)
    kvshape = (b, t, 2, NSA_KV_HEADS, HEAD_DIM)
    new_win = _rows_view(jnp.concatenate([_feature_major(cache_win)[:, :, t:], kvw.transpose(0, 2, 1)], axis=2),
                         NSA_KV_HEADS)
    new_pool = jnp.concatenate([pool_buf, u], axis=1)[:, t:]
    states = (kvc.reshape(kvshape), kvs.reshape(kvshape), new_win, new_pool)
    return (x2, h2, eidx, gates), states, cnt


def kernel(x_prompt, x_sample, cache_cmp_kv, cache_slc_kv, cache_win_kv, cache_mem_kv, state_pool, page_table,
           mem_prompt, rel_bias, norm1_g, w_in, nsa_qk_norm, mem_qk_norm, cmp_w, cmp_pe, pool_w, pool_scale,
           mem_norm_g, w_mem_kv, w_up_pool, w_up_nsa, w_up_mem, w_out, norm2_g, router_w, router_b, w_gu, b_gu,
           w_down, b_down):
    depth = w_in.shape[0]
    yp, ys = x_prompt, x_sample
    bp, sp, _ = x_prompt.shape
    bs, ts, _ = x_sample.shape
    outs_p = [[] for _ in range(5)]
    outs_s = [[] for _ in range(4)]
    for l in range(depth):
        w = _prep_layer(l, rel_bias, norm1_g, w_in, nsa_qk_norm, mem_qk_norm, cmp_w, cmp_pe, pool_w, pool_scale,
                        mem_norm_g, w_mem_kv, w_up_pool, w_up_nsa, w_up_mem, w_out, norm2_g, router_w, router_b,
                        w_gu, b_gu, w_down, b_down)
        pre_p, st_p, cnt = _prompt_pre(yp, mem_prompt, w, rel_bias, jnp.zeros((1, LANES), F32))
        pre_s, st_s, cnt = _sample_pre(ys, cache_cmp_kv[l], cache_slc_kv[l], cache_win_kv[l], cache_mem_kv[l],
                                       state_pool[l], page_table, w, rel_bias, cnt)
        yp, ys = _moe([pre_p, pre_s], cnt, w)
        yp = yp.reshape(bp, sp, D_MODEL)
        ys = ys.reshape(bs, ts, D_MODEL)
        for lst, a in zip(outs_p, st_p):
            lst.append(a)
        for lst, a in zip(outs_s, st_s):
            lst.append(a)
    new_cmp_p, new_slc_p, new_win_p, new_mem_p, new_pool_p = [jnp.stack(a) for a in outs_p]
    new_cmp_s, new_slc_s, new_win_s, new_pool_s = [jnp.stack(a) for a in outs_s]
    new_win_s = new_win_s.reshape(new_win_s.shape[:3] + (2, NSA_KV_HEADS, HEAD_DIM))
    return (yp, ys, new_cmp_p, new_slc_p, new_win_p, new_mem_p, new_pool_p,
            new_cmp_s, new_slc_s, new_win_s, new_pool_s)
```

```python
import functools
import math

import jax
import jax.numpy as jnp
from jax import lax
from jax.experimental import pallas as pl
from jax.experimental.pallas import tpu as pltpu

F32 = jnp.float32
BF16 = jnp.bfloat16

D_MODEL = 1024
HEAD_DIM = 64
POOL_WINDOWS = (2, 4, 8, 16)
POOL_GROUP = 64
POOL_WIDTH = 256
POOL_BUF = 15
NSA_HEADS = 8
NSA_KV_HEADS = 2
NSA_HPG = 4
NSA_WIDTH = 512
KV_WIDTH = 256
CMP_BLOCK = 32
CMP_STRIDE = 16
SLC_BLOCK = 64
SLC_TOPK = 16
WINDOW = 512
MEM_HEADS = 4
MEM_WIDTH = 256
NUM_BUCKETS = 32
MAX_DISTANCE = 128
N_EXPERTS = 32
TOP_K = 4
D_FF = 1024
SWIGLU_ALPHA = 1.702
SWIGLU_LIMIT = 7.0
EPS = 1e-6
SCALE = HEAD_DIM ** -0.5

LANES = 128
MXU_DIM = 256
NEG = -1e30
ATT_TILE = 256
DEC_PAGES_PER_STEP = 32
MOE_TILE = 512
MOE_DMA_TOKENS = 512
VMEM_LIMIT = 48 * 1024 * 1024
FFN_VMEM_LIMIT = 56 * 1024 * 1024


def _cparams(*sem):
    return pltpu.CompilerParams(dimension_semantics=sem, vmem_limit_bytes=VMEM_LIMIT)


def _dot(a, b):
    return jnp.dot(a, b, preferred_element_type=F32)


def _dot_nt(a, b):
    return lax.dot_general(a, b, (((1,), (1,)), ((), ())), preferred_element_type=F32)


def _split_dot(a, b):
    hi = a.astype(BF16)
    lo = (a - hi.astype(F32)).astype(BF16)
    return _dot(hi, b) + _dot(lo, b)


def _rms(x, g):
    r = lax.rsqrt(jnp.mean(x * x, axis=-1, keepdims=True) + EPS)
    return (x * r) * g


def _sigmoid(x):
    return 1.0 / (1.0 + jnp.exp(-x))


def _pack_bf16_pairs(x):
    w = x.shape[1] // 2
    bits = lax.bitcast_convert_type(x.astype(BF16).astype(F32), jnp.uint32)
    return jnp.right_shift(bits[:, :w], jnp.uint32(16)) | (bits[:, w:] & jnp.uint32(0xFFFF0000))


def _unpack_bf16_pairs(p):
    lo = lax.bitcast_convert_type(jnp.left_shift(p, jnp.uint32(16)), F32)
    hi = lax.bitcast_convert_type(p & jnp.uint32(0xFFFF0000), F32)
    return jnp.concatenate([lo, hi], axis=1)


def _lane_iota(shape):
    return lax.broadcasted_iota(jnp.int32, shape, len(shape) - 1)


def _row_iota(shape):
    return lax.broadcasted_iota(jnp.int32, shape, len(shape) - 2)


def _proj_kernel(x_ref, g_ref, w_ref, gain_ref, nmask_ref, seg_ref, *rest, segs, n_norm, has_wt):
    wt_ref = rest[0] if has_wt else None
    out_refs = rest[1:] if has_wt else rest
    h = _rms(x_ref[...], g_ref[...]).astype(BF16)
    seg = seg_ref[...]
    outs = iter(out_refs)
    for (start, width, kind, forms, _) in segs:
        if forms == ("t",) and has_wt:
            next(outs)[...] = _dot_nt(wt_ref[...], h)
            continue
        z = _dot(h, w_ref[:, start:start + width])
        if start < n_norm:
            pieces = []
            for c in range(0, width, MXU_DIM):
                zc = z[:, c:c + MXU_DIM]
                ms = _split_dot(zc * zc, seg)
                zn = (zc * lax.rsqrt(ms + EPS)) * gain_ref[:, start + c:start + c + MXU_DIM]
                pieces.append(jnp.where(nmask_ref[:, start + c:start + c + MXU_DIM] > 0, zn, zc))
            z = pieces[0] if len(pieces) == 1 else jnp.concatenate(pieces, axis=1)
        if kind == "sigmoid":
            z = _sigmoid(z)
        elif kind == "qscale":
            z = z * SCALE
        for form in forms:
            o_ref = next(outs)
            if form == "rows":
                o_ref[...] = z.astype(o_ref.dtype)
            elif form == "key_rows":
                o_ref[...] = z[:, 0:LANES]
            else:
                o_ref[...] = z.T


def _project(x, g, w, gain, nmask, segs, n_norm, tm, seq, wt=None):
    n = x.shape[0]
    ncol = w.shape[1]
    seg = _seg_matrix(MXU_DIM)
    full = lambda i: (0, 0)
    tpb = seq // tm
    out_specs, out_shape = [], []
    for (_, wd, _, forms, dt) in segs:
        for form in forms:
            if form == "rows":
                out_specs.append(pl.BlockSpec((tm, wd), lambda i: (i, 0)))
                out_shape.append(jax.ShapeDtypeStruct((n, wd), dt))
            elif form == "key_rows":
                out_specs.append(pl.BlockSpec((tm, LANES), lambda i: (i, 0)))
                out_shape.append(jax.ShapeDtypeStruct((n, LANES), F32))
            else:
                out_specs.append(pl.BlockSpec((None, wd, tm), lambda i: (i // tpb, 0, i % tpb)))
                out_shape.append(jax.ShapeDtypeStruct((n // seq, wd, seq), F32))
    extra = () if wt is None else (wt,)
    return pl.pallas_call(
        functools.partial(_proj_kernel, segs=segs, n_norm=n_norm, has_wt=wt is not None),
        grid=(n // tm,),
        in_specs=[
            pl.BlockSpec((tm, D_MODEL), lambda i: (i, 0)),
            pl.BlockSpec((1, D_MODEL), full),
            pl.BlockSpec((D_MODEL, ncol), full),
            pl.BlockSpec((1, gain.shape[1]), full),
            pl.BlockSpec((1, nmask.shape[1]), full),
            pl.BlockSpec((MXU_DIM, MXU_DIM), full),
        ] + [pl.BlockSpec(a.shape, full) for a in extra],
        out_specs=out_specs,
        out_shape=out_shape,
        compiler_params=_cparams("parallel"),
        name="proj",
    )(x, g, w, gain, nmask, seg, *extra)


def _seg_matrix(n):
    i = jnp.arange(n) // HEAD_DIM
    return jnp.where(i[:, None] == i[None, :], 1.0 / HEAD_DIM, 0.0).astype(BF16)


def _pool_kernel(u_ref, buf_ref, w_ref, scale_ref, o_ref, zs_ref, *, t, pos0):
    zs_ref[0:16, :] = buf_ref[...]
    zs_ref[16:16 + t, :] = u_ref[...]
    u = u_ref[...]
    lane = _lane_iota((1, POOL_WIDTH))
    pos = (pos0 + _row_iota((t, 1))).astype(F32)
    acc = u
    mean = None
    for i in range(1, max(POOL_WINDOWS)):
        acc = acc + zs_ref[16 - i:16 - i + t, :]
        if i + 1 in POOL_WINDOWS:
            gi = POOL_WINDOWS.index(i + 1)
            m = acc / jnp.minimum(pos + 1.0, float(i + 1))
            mean = m if mean is None else jnp.where(lane >= gi * POOL_GROUP, m, mean)
    d = (mean - u).astype(BF16)
    o_ref[...] = _dot(d, w_ref[...]) * scale_ref[...]


def _pool_mix(u, buf16, w_bd, scale, pos0):
    b, t, _ = u.shape
    return pl.pallas_call(
        functools.partial(_pool_kernel, t=t, pos0=pos0),
        grid=(b,),
        in_specs=[
            pl.BlockSpec((None, t, POOL_WIDTH), lambda i: (i, 0, 0)),
            pl.BlockSpec((None, 16, POOL_WIDTH), lambda i: (i, 0, 0)),
            pl.BlockSpec((POOL_WIDTH, POOL_WIDTH), lambda i: (0, 0)),
            pl.BlockSpec((1, POOL_WIDTH), lambda i: (0, 0)),
        ],
        out_specs=pl.BlockSpec((None, t, POOL_WIDTH), lambda i: (i, 0, 0)),
        out_shape=jax.ShapeDtypeStruct((b, t, POOL_WIDTH), F32),
        scratch_shapes=[pltpu.VMEM((t + 16, POOL_WIDTH), F32)],
        compiler_params=_cparams("parallel"),
        name="pool",
    )(u, buf16, w_bd, scale)


def _cpart_compute(rows_of, pe_ref, w_ref, o_ref, m):
    for c in range(2):
        acc = jnp.zeros((m + 8, KV_WIDTH), F32)
        for j in range(CMP_STRIDE):
            lhs = jnp.concatenate([rows_of(c, j), pe_ref[c, j]], axis=0)
            acc = acc + _dot(lhs.astype(BF16), w_ref[c, j])
        lane = _lane_iota((1, KV_WIDTH))
        pe_term = jnp.where(lane < LANES, acc[m:m + 1], acc[m + 1:m + 2])
        o_ref[:, c * KV_WIDTH:(c + 1) * KV_WIDTH] = acc[0:m] + pe_term


def _cpart_kernel(xk_ref, xv_ref, pe_ref, w_ref, o_ref, *, rows):
    m = rows // CMP_STRIDE
    x_refs = (xk_ref, xv_ref)
    _cpart_compute(lambda c, j: x_refs[c][pl.ds(j, m, stride=CMP_STRIDE), :], pe_ref, w_ref, o_ref, m)


def _cpart_paged_kernel(*refs, nop, page):
    page_refs = refs[2:2 + nop]
    perm_ref, pe_ref, w_ref, o_ref, xs_ref = refs[2 + nop:]
    n = page // CMP_STRIDE
    for k, r in enumerate(page_refs):
        y = _dot_nt(perm_ref[...], r[...].astype(BF16))
        for j in range(CMP_STRIDE):
            for c in range(2):
                xs_ref[c, j, k * n:(k + 1) * n, :] = y[j * n:(j + 1) * n, c * LANES:(c + 1) * LANES]
    _cpart_compute(lambda c, j: xs_ref[c, j], pe_ref, w_ref, o_ref, nop * n)


def _cmp_partials_dense(kv, pe, w_c):
    b, t, _ = kv.shape
    rows = (t // CMP_STRIDE) * CMP_STRIDE
    n = rows // CMP_STRIDE
    return pl.pallas_call(
        functools.partial(_cpart_kernel, rows=rows),
        grid=(b,),
        in_specs=[
            pl.BlockSpec((None, rows, LANES), lambda i: (i, 0, 0)),
            pl.BlockSpec((None, rows, LANES), lambda i: (i, 0, 1)),
            pl.BlockSpec(pe.shape, lambda i: (0, 0, 0, 0)),
            pl.BlockSpec(w_c.shape, lambda i: (0, 0, 0, 0)),
        ],
        out_specs=pl.BlockSpec((None, n, 2 * KV_WIDTH), lambda i: (i, 0, 0)),
        out_shape=jax.ShapeDtypeStruct((b, n, 2 * KV_WIDTH), F32),
        compiler_params=_cparams("parallel"),
        name="cmp_partials",
    )(kv, kv, pe, w_c)


def _cmp_partials_paged(pool_t, phys, lblk, pe, w_c, pages_per_step):
    b, n_pages = phys.shape
    page = LANES
    nop = pages_per_step
    n = nop * page // CMP_STRIDE

    def page_spec(k):
        return pl.BlockSpec((None, KV_WIDTH, page),
                            lambda i, c, ph, lb: (ph[i, c * nop + k], 0, lb[i, c * nop + k]))

    r_idx = jnp.arange(page)
    regroup = (r_idx[None, :] == ((r_idx % (page // CMP_STRIDE)) * CMP_STRIDE + r_idx // (page // CMP_STRIDE))[:, None])
    regroup = regroup.astype(BF16)
    grid_spec = pltpu.PrefetchScalarGridSpec(
        num_scalar_prefetch=2,
        grid=(b, n_pages // nop),
        in_specs=[page_spec(k) for k in range(nop)] + [
            pl.BlockSpec(regroup.shape, lambda i, c, ph, lb: (0, 0)),
            pl.BlockSpec(pe.shape, lambda i, c, ph, lb: (0, 0, 0, 0)),
            pl.BlockSpec(w_c.shape, lambda i, c, ph, lb: (0, 0, 0, 0)),
        ],
        out_specs=pl.BlockSpec((None, n, 2 * KV_WIDTH), lambda i, c, ph, lb: (i, c, 0)),
        scratch_shapes=[pltpu.VMEM((2, CMP_STRIDE, n, LANES), F32)],
    )
    return pl.pallas_call(
        functools.partial(_cpart_paged_kernel, nop=nop, page=page),
        grid_spec=grid_spec,
        out_shape=jax.ShapeDtypeStruct((b, n_pages * page // CMP_STRIDE, 2 * KV_WIDTH), F32),
        compiler_params=_cparams("parallel", "arbitrary"),
        name="cmp_partials_paged",
    )(phys, lblk, *([pool_t] * nop), regroup, pe, w_c)


def _group_query_columns(q_ref, g, t):
    zeros64 = jnp.zeros((HEAD_DIM, t), BF16)
    cols = []
    for pr in range(2):
        qt = q_ref[:, (2 * g + pr) * LANES:(2 * g + pr + 1) * LANES].astype(F32).T.astype(BF16)
        for half in range(2):
            qh = qt[half * HEAD_DIM:(half + 1) * HEAD_DIM]
            cols.append(jnp.concatenate([qh, zeros64] if g == 0 else [zeros64, qh], axis=0))
    return jnp.concatenate(cols, axis=1)


def _store_group_output(o_ref, out_t, g, t):
    for pr in range(2):
        pair = jnp.concatenate([out_t[:, (2 * pr) * t:(2 * pr + 1) * t], out_t[:, (2 * pr + 1) * t:(2 * pr + 2) * t]],
                               axis=0)
        o_ref[:, (2 * g + pr) * LANES:(2 * g + pr + 1) * LANES] = pair.T


def _cattn_kernel(q_ref, p_ref, bias_ref, gn_ref, gk_ref, seg_ref, o_ref, ns0_ref, ns1_ref, *,
                  tq, n_sub, n_cmp, n_slc, nslp, pos0):
    qi = pl.program_id(0)
    pall = p_ref[...]
    kraw = pall[:, 0:LANES] + pltpu.roll(pall[:, LANES:2 * LANES], n_sub - 1, 0)
    vc = pall[:, 2 * LANES:3 * LANES] + pltpu.roll(pall[:, 3 * LANES:4 * LANES], n_sub - 1, 0)
    ms = _split_dot(kraw * kraw, seg_ref[...])
    kc = ((kraw * lax.rsqrt(ms + EPS)) * gk_ref[...]).astype(BF16)
    vct = vc.T.astype(BF16)

    nsel = -(-n_slc // 8) * 8
    jj = _row_iota((nsel, n_sub))
    nn = _lane_iota((nsel, n_sub))
    covers_t = ((nn * CMP_STRIDE < (jj + 1) * SLC_BLOCK) & (nn * CMP_STRIDE + CMP_BLOCK - 1 >= jj * SLC_BLOCK)
                & (nn < n_cmp) & (jj < n_slc))
    covers_t = jnp.where(covers_t, 1.0, 0.0).astype(BF16)
    qpos = pos0 + qi * tq + _lane_iota((1, tq))
    qblk = jnp.right_shift(qpos, SLC_BLOCK.bit_length() - 1)
    jr = _row_iota((nsel, tq))
    jrf = jr.astype(F32)
    forced = (jr == 0) | (jr == qblk) | (jr == qblk - 1)
    causal = jr <= qblk
    gnt = gn_ref[...].T

    for g in range(NSA_KV_HEADS):
        bias = bias_ref[g]
        s = _dot(kc, _group_query_columns(q_ref, g, tq)) + bias
        m = jnp.max(s, axis=0, keepdims=True)
        m = jnp.where(m > 0.5 * NEG, m, 0.0)
        e = jnp.where(bias > 0.5 * NEG, jnp.exp(s - m), 0.0)
        p = e / jnp.maximum(jnp.sum(e, axis=0, keepdims=True), 1e-30)
        h0 = NSA_HPG * g
        gate = jnp.concatenate([gnt[h0 + c:h0 + c + 1] for c in range(NSA_HPG)], axis=1)
        out_t = _dot(vct[g * HEAD_DIM:(g + 1) * HEAD_DIM], p.astype(BF16)) * gate
        _store_group_output(o_ref, out_t, g, tq)

        prsum = p[:, 0:tq] + p[:, tq:2 * tq] + p[:, 2 * tq:3 * tq] + p[:, 3 * tq:4 * tq]
        hi = prsum.astype(BF16)
        lo = (prsum - hi.astype(F32)).astype(BF16)
        imp = _dot(covers_t, hi) + _dot(covers_t, lo)
        score = jnp.where(forced, jnp.inf, imp)
        score = jnp.where(causal, score, -jnp.inf)
        sel = jnp.zeros((nsel, tq), F32)
        for _ in range(min(SLC_TOPK, n_slc)):
            m = jnp.max(score, axis=0, keepdims=True)
            idx = jnp.min(jnp.where(score == m, jrf, 1e9), axis=0, keepdims=True)
            pick = jrf == idx
            sel = jnp.where(pick & (m > -jnp.inf), 1.0, sel)
            score = jnp.where(pick, -jnp.inf, score)
        ns = 1.0 - sel
        if nslp > nsel:
            ns = jnp.concatenate([ns, jnp.ones((nslp - nsel, tq), F32)], axis=0)
        (ns0_ref if g == 0 else ns1_ref)[...] = ns


def _cmp_bias_cols(bias, tq):
    _, s, n_sub = bias.shape
    bias_t = bias.reshape(NSA_KV_HEADS, NSA_HPG, s // tq, tq, n_sub).transpose(0, 4, 2, 1, 3)
    return bias_t.reshape(NSA_KV_HEADS, n_sub, NSA_HPG * s)


def _cmp_attention(q, parts, bias_t, gn, gk, *, tq, n_cmp, n_slc, pos0):
    b, s, _ = q.shape
    n_sub = parts.shape[1]
    nslp = -(-n_slc // LANES) * LANES
    kern = functools.partial(_cattn_kernel, tq=tq, n_sub=n_sub, n_cmp=n_cmp, n_slc=n_slc, nslp=nslp, pos0=pos0)
    return pl.pallas_call(
        kern,
        grid=(s // tq, b),
        in_specs=[
            pl.BlockSpec((None, tq, NSA_WIDTH), lambda i, j: (j, i, 0)),
            pl.BlockSpec((None, n_sub, 2 * KV_WIDTH), lambda i, j: (j, 0, 0)),
            pl.BlockSpec((NSA_KV_HEADS, n_sub, NSA_HPG * tq), lambda i, j: (0, 0, i)),
            pl.BlockSpec((None, tq, LANES), lambda i, j: (j, i, 0)),
            pl.BlockSpec((1, LANES), lambda i, j: (0, 0)),
            pl.BlockSpec((LANES, LANES), lambda i, j: (0, 0)),
        ],
        out_specs=[
            pl.BlockSpec((None, tq, NSA_WIDTH), lambda i, j: (j, i, 0)),
            pl.BlockSpec((None, nslp, tq), lambda i, j: (j, 0, i)),
            pl.BlockSpec((None, nslp, tq), lambda i, j: (j, 0, i)),
        ],
        out_shape=[
            jax.ShapeDtypeStruct((b, s, NSA_WIDTH), F32),
            jax.ShapeDtypeStruct((b, nslp, s), F32),
            jax.ShapeDtypeStruct((b, nslp, s), F32),
        ],
        compiler_params=_cparams("parallel", "parallel"),
        name="cmp_attention",
    )(q, parts, bias_t, gn, gk, _seg_matrix(LANES))


def _flash_kernel(q_ref, ns0_ref, ns1_ref, kv_ref, vt_ref, tab_ref, gn_ref, o_ref, *, t, use_sel, band, gate_base):
    qi = pl.program_id(1)
    row_k = _row_iota((t, LANES))
    lane_k = _lane_iota((t, LANES))
    lo_tile = jnp.maximum(qi - band, 0) if band is not None else 0
    gnt = gn_ref[...].T

    qts = []
    for g in range(NSA_KV_HEADS):
        qt_g = _group_query_columns(q_ref, g, t)
        if use_sel:
            nst = (ns0_ref if g == 0 else ns1_ref)[...].astype(BF16)
            qt_g = jnp.concatenate([qt_g, jnp.concatenate([nst] * NSA_HPG, axis=1)], axis=0)
        qts.append(qt_g)

    def body(kj, carry):
        k0 = pl.multiple_of(kj * t, t)
        kk = kv_ref[pl.ds(k0, t), :].astype(BF16)
        if use_sel:
            blk = kj * (t // SLC_BLOCK) + jnp.right_shift(row_k, SLC_BLOCK.bit_length() - 1)
            onehot = jnp.where(lane_k == blk, -(2.0 ** 30), 0.0).astype(BF16)
            kk = jnp.concatenate([kk, onehot], axis=1)
        delta = qi - kj
        if band is None:
            kind = jnp.minimum(delta, 2)
        else:
            kind = jnp.where(delta < 2, delta, jnp.where(delta < band, 2, 3))
        m_old, l_old, acc = carry
        s = _dot(kk, qt_all) + tab_ref[kind]
        m_new = jnp.maximum(m_old, jnp.max(s, axis=0, keepdims=True))
        alpha = jnp.exp(m_old - m_new)
        p = jnp.exp(s - m_new)
        l_new = alpha * l_old + jnp.sum(p, axis=0, keepdims=True)
        pb = p.astype(BF16)
        pv = jnp.concatenate(
            [_dot(vt_ref[g * HEAD_DIM:(g + 1) * HEAD_DIM, pl.ds(k0, t)].astype(BF16), pb[:, g * gw:(g + 1) * gw])
             for g in range(NSA_KV_HEADS)], axis=1)
        return m_new, l_new, acc * alpha + pv

    gw = NSA_HPG * t
    qt_all = jnp.concatenate(qts, axis=1)
    init = (jnp.full((1, NSA_HEADS * t), NEG, F32), jnp.zeros((1, NSA_HEADS * t), F32),
            jnp.zeros((HEAD_DIM, NSA_HEADS * t), F32))
    _, l_fin, acc = lax.fori_loop(lo_tile, qi + 1, body, init)
    gate = jnp.concatenate([gnt[gate_base + h:gate_base + h + 1] for h in range(NSA_HEADS)], axis=1)
    out = acc * (gate / l_fin)
    for g in range(NSA_KV_HEADS):
        _store_group_output(o_ref, out[:, g * gw:(g + 1) * gw], g, t)


def _flash_attention(q, ns0, ns1, k_rows, kv_t, tab, gn, *, use_sel, band, gate_base):
    b, s, _ = q.shape
    t = ATT_TILE
    assert ns0.shape[1] == LANES
    kern = functools.partial(_flash_kernel, t=t, use_sel=use_sel, band=band, gate_base=gate_base)
    tile = lambda w: pl.BlockSpec((None, t, w), lambda i, j: (i, j, 0))
    ns_tile = pl.BlockSpec((None, LANES, t), lambda i, j: (i, 0, j))
    return pl.pallas_call(
        kern,
        grid=(b, s // t),
        in_specs=[
            tile(NSA_WIDTH), ns_tile, ns_tile,
            pl.BlockSpec((None, s, LANES), lambda i, j: (i, 0, 0)),
            pl.BlockSpec((None, LANES, s), lambda i, j: (i, 1, 0)),
            pl.BlockSpec(tab.shape, lambda i, j: (0, 0, 0)),
            tile(LANES),
        ],
        out_specs=tile(NSA_WIDTH),
        out_shape=jax.ShapeDtypeStruct((b, s, NSA_WIDTH), F32),
        compiler_params=_cparams("parallel", "parallel"),
        name="flash_sel" if use_sel else "flash_win",
    )(q, ns0, ns1, k_rows, kv_t, tab, gn)


def _memattn_kernel(q_ref, kv_ref, o_ref):
    lane = _lane_iota((kv_ref.shape[0], LANES))
    for pr in range(MEM_HEADS // 2):
        qpair = q_ref[:, pr * LANES:(pr + 1) * LANES]
        kblk = kv_ref[:, pr * LANES:(pr + 1) * LANES]
        vblk = kv_ref[:, MEM_WIDTH + pr * LANES:MEM_WIDTH + (pr + 1) * LANES]
        out = None
        for half in range(2):
            keep = (lane < HEAD_DIM) if half == 0 else (lane >= HEAD_DIM)
            kk = jnp.where(keep, kblk, 0.0).astype(BF16)
            vv = jnp.where(keep, vblk, 0.0).astype(BF16)
            s = _dot_nt(qpair, kk)
            m = jnp.max(s, axis=1, keepdims=True)
            e = jnp.exp(s - m)
            p = e / jnp.sum(e, axis=1, keepdims=True)
            o = _dot(p.astype(BF16), vv)
            out = o if out is None else out + o
        o_ref[:, pr * LANES:(pr + 1) * LANES] = out


def _mem_attention(qm, mem_kv, tq):
    b, s, _ = qm.shape
    m = mem_kv.shape[1]
    return pl.pallas_call(
        _memattn_kernel,
        grid=(b, s // tq),
        in_specs=[
            pl.BlockSpec((None, tq, MEM_WIDTH), lambda i, j: (i, j, 0)),
            pl.BlockSpec((None, m, 2 * MEM_WIDTH), lambda i, j: (i, 0, 0)),
        ],
        out_specs=pl.BlockSpec((None, tq, MEM_WIDTH), lambda i, j: (i, j, 0)),
        out_shape=jax.ShapeDtypeStruct((b, s, MEM_WIDTH), F32),
        compiler_params=_cparams("parallel", "parallel"),
        name="mem_attention",
    )(qm, mem_kv)


def _dec_kernel(*refs, n_pg, pps, has_new):
    bidx_ref = refs[2]
    page_refs = refs[3:3 + pps]
    new_ref, wq_ref, bias_ref, ns_ref, gate_ref, o_ref, acc_ref, m_ref, l_ref = refs[3 + pps:]
    c = pl.program_id(1)
    n_chunks = pl.num_programs(1)

    @pl.when(c == 0)
    def _():
        acc_ref[...] = jnp.zeros_like(acc_ref)
        m_ref[...] = jnp.full_like(m_ref, NEG)
        l_ref[...] = jnp.zeros_like(l_ref)

    rk = _row_iota((LANES, LANES))

    def step(tiles, first_page):
        feats = (tiles[0] if len(tiles) == 1 else jnp.concatenate(tiles, axis=1)).astype(BF16)
        s = lax.dot_general(feats, wq_ref[...], (((0,), (1,)), ((), ())),
                            preferred_element_type=F32)
        extra = []
        for k in range(len(tiles)):
            pg = first_page + k
            ns = jnp.where(rk < SLC_BLOCK, ns_ref[pl.ds(2 * pg, 1), :], ns_ref[pl.ds(2 * pg + 1, 1), :])
            extra.append(jnp.where(ns > 0.5, NEG, bias_ref[bidx_ref[pg]]))
        s = s + (extra[0] if len(extra) == 1 else jnp.concatenate(extra, axis=0))
        m_old = m_ref[...]
        m_new = jnp.maximum(m_old, jnp.max(s, axis=0, keepdims=True))
        alpha = jnp.exp(m_old - m_new)
        p = jnp.exp(s - m_new)
        l_ref[...] = alpha * l_ref[...] + jnp.sum(p, axis=0, keepdims=True)
        m_ref[...] = m_new
        acc_ref[...] = acc_ref[...] * alpha + _dot(feats, p.astype(BF16))

    if has_new:
        @pl.when(c < n_chunks - 1)
        def _():
            step([r[...] for r in page_refs], c * pps)

        @pl.when(c == n_chunks - 1)
        def _():
            step([r[...] for r in page_refs] + [new_ref[...]], n_pg - pps)
    else:
        step([r[...] for r in page_refs], c * pps)

    @pl.when(c == n_chunks - 1)
    def _():
        o_ref[...] = acc_ref[...] / l_ref[...] * gate_ref[...]


def _decode_attention(pages, phys, lblk, new_rows, wq, bias_tab, bias_idx, notsel, gate):
    bsz, n_pg = phys.shape
    w = pages.shape[1]
    has_new = new_rows is not None
    pps = math.gcd(n_pg, DEC_PAGES_PER_STEP)
    n_steps = n_pg // pps
    n_chunks = n_steps
    if not has_new:
        new_rows = jnp.zeros((1, w, LANES), F32)
    new_map = (lambda i, c, ph, lb, bi: (i, 0, 0)) if has_new else (lambda i, c, ph, lb, bi: (0, 0, 0))

    def page_spec(k):
        def index(i, c, ph, lb, bi):
            return (ph[i, c * pps + k], 0, lb[i, c * pps + k])
        return pl.BlockSpec((None, w, LANES), index)

    per_b = lambda i, c, ph, lb, bi: (i, 0, 0)
    grid_spec = pltpu.PrefetchScalarGridSpec(
        num_scalar_prefetch=3,
        grid=(bsz, n_chunks),
        in_specs=[page_spec(k) for k in range(pps)] + [
            pl.BlockSpec((None, w, LANES), new_map),
            pl.BlockSpec((None, LANES, w), per_b),
            pl.BlockSpec(bias_tab.shape, lambda i, c, ph, lb, bi: (0, 0, 0)),
            pl.BlockSpec((None, notsel.shape[1], LANES), per_b),
            pl.BlockSpec((None, 1, LANES), per_b),
        ],
        out_specs=pl.BlockSpec((None, w, LANES), per_b),
        scratch_shapes=[pltpu.VMEM((w, LANES), F32), pltpu.VMEM((1, LANES), F32), pltpu.VMEM((1, LANES), F32)],
    )
    return pl.pallas_call(
        functools.partial(_dec_kernel, n_pg=n_pg, pps=pps, has_new=has_new),
        grid_spec=grid_spec,
        out_shape=jax.ShapeDtypeStruct((bsz, w, LANES), F32),
        compiler_params=_cparams("parallel", "arbitrary"),
        name="decode_attention",
    )(phys, lblk, bias_idx, *([pages] * pps), new_rows, wq, bias_tab, notsel, gate)


def _feature_major(cache):
    n, rows = cache.shape[:2]
    return cache.transpose(0, 2, 3, 4, 1).reshape(n, -1, rows)


def _tail_kernel(x_ref, g1_ref, op_ref, oc_ref, os_ref, ow_ref, om_ref, cnt0_ref, wgb_ref, wup_p_ref, wup_n_ref,
                 wup_m_ref, wout_ref, g2_ref, rw_ref, rb_ref, x2_ref, h2_ref, ei_ref, gt_ref, cnt_ref):
    @pl.when(pl.program_id(0) == 0)
    def _():
        cnt_ref[...] = cnt0_ref[...]

    x = x_ref[...]
    h = _rms(x, g1_ref[...]).astype(BF16)
    onsa = (oc_ref[...] + os_ref[...] + ow_ref[...]).astype(BF16)
    ups = (_dot(op_ref[...].astype(BF16), wup_p_ref[...]), _dot(onsa, wup_n_ref[...]),
           _dot(om_ref[...].astype(BF16), wup_m_ref[...]))
    mixed = None
    for br in range(3):
        gb = _sigmoid(_dot(h, wgb_ref[:, br * D_MODEL:(br + 1) * D_MODEL]))
        mixed = gb * ups[br] if mixed is None else mixed + gb * ups[br]
    x2 = x + _dot(mixed.astype(BF16), wout_ref[...])
    x2_ref[...] = x2
    h2 = _rms(x2, g2_ref[...])
    h2_ref[...] = _pack_bf16_pairs(h2)
    logits = _dot(h2.astype(BF16), rw_ref[...]) + rb_ref[...]
    lane = _lane_iota(logits.shape)
    lanef = lane.astype(F32)
    tops, idxs = [], []
    for _ in range(TOP_K):
        m = jnp.max(logits, axis=1, keepdims=True)
        idx = jnp.min(jnp.where(logits == m, lanef, 1e9), axis=1, keepdims=True)
        logits = jnp.where(lanef == idx, -jnp.inf, logits)
        tops.append(m)
        idxs.append(idx)
    es = [jnp.exp(tk - tops[0]) for tk in tops]
    den = es[0] + es[1] + es[2] + es[3]
    tm = logits.shape[0]
    onehot = jnp.zeros(logits.shape, F32)
    for k in range(TOP_K):
        onehot = jnp.where(lanef == idxs[k], 1.0, onehot)
    tri = jnp.where(_row_iota((tm, tm)) > _lane_iota((tm, tm)), 1.0, 0.0).astype(BF16)
    before = _dot(tri, onehot.astype(BF16)) + cnt_ref[...]
    cnt_ref[...] = cnt_ref[...] + jnp.sum(onehot, axis=0, keepdims=True)
    ei = jnp.zeros(logits.shape, F32)
    gt = jnp.zeros(logits.shape, F32)
    for k in range(TOP_K):
        rank = jnp.sum(jnp.where(lanef == idxs[k], before, 0.0), axis=1, keepdims=True)
        ei = jnp.where(lane == k, idxs[k], ei)
        ei = jnp.where(lane == TOP_K + k, rank, ei)
        gt = jnp.where(lane == k, es[k] / den, gt)
    ei_ref[...] = ei.astype(jnp.int32)
    gt_ref[...] = gt


def _layer_tail(x, o_pool, o_cmp, o_slc, o_win, o_mem, cnt0, w, tm):
    n = x.shape[0]
    row = lambda wd: pl.BlockSpec((tm, wd), lambda i: (i, 0))
    full = lambda a: pl.BlockSpec(a.shape, lambda i: (0,) * a.ndim)
    weights = (w["wgb"], w["wup_pool"], w["wup_nsa"], w["wup_mem"], w["wout"], w["g2"], w["rw"], w["rb"])
    return pl.pallas_call(
        _tail_kernel,
        grid=(n // tm,),
        in_specs=[row(D_MODEL), full(w["g1"]), row(POOL_WIDTH), row(NSA_WIDTH), row(NSA_WIDTH), row(NSA_WIDTH),
                  row(MEM_WIDTH), full(cnt0)] + [full(a) for a in weights],
        out_specs=[row(D_MODEL), row(D_MODEL // 2), row(LANES), row(LANES), full(cnt0)],
        out_shape=[jax.ShapeDtypeStruct((n, D_MODEL), F32), jax.ShapeDtypeStruct((n, D_MODEL // 2), jnp.uint32),
                   jax.ShapeDtypeStruct((n, LANES), jnp.int32), jax.ShapeDtypeStruct((n, LANES), F32),
                   jax.ShapeDtypeStruct((1, LANES), F32)],
        compiler_params=_cparams("arbitrary"),
        name="layer_tail",
    )(x, w["g1"], o_pool, o_cmp, o_slc, o_win, o_mem, cnt0, *weights)


def _ffn_kernel(be_ref, nu_ref, x_ref, wgu_ref, bgu_ref, wd_ref, bd_ref, o_ref, wgu_bf, wd_bf):
    i = pl.program_id(0)

    @pl.when((i == 0) | (be_ref[i] != be_ref[jnp.maximum(i - 1, 0)]))
    def _():
        wgu_bf[...] = wgu_ref[...].astype(BF16)
        wd_bf[...] = wd_ref[...].astype(BF16)

    @pl.when(i < nu_ref[0])
    def _():
        gu = _dot(_unpack_bf16_pairs(x_ref[...]).astype(BF16), wgu_bf[...]) + bgu_ref[...]
        gate = jnp.minimum(gu[:, :D_FF], SWIGLU_LIMIT)
        up = jnp.clip(gu[:, D_FF:], -SWIGLU_LIMIT, SWIGLU_LIMIT)
        act = gate * _sigmoid(SWIGLU_ALPHA * gate) * (up + 1.0)
        o_ref[...] = _dot(act.astype(BF16), wd_bf[...]) + bd_ref[...]

    @pl.when(i >= nu_ref[0])
    def _():
        o_ref[...] = jnp.zeros_like(o_ref)


def _expert_ffn(rows, blk_e, n_used, wgu, bgu, wd, bd):
    n_rows = rows.shape[0]
    n_blocks = n_rows // MOE_TILE
    blk = lambda i, be, nu: (jnp.minimum(i, nu[0] - 1), 0)
    grid_spec = pltpu.PrefetchScalarGridSpec(
        num_scalar_prefetch=2,
        grid=(n_blocks,),
        in_specs=[
            pl.BlockSpec((MOE_TILE, D_MODEL // 2), blk),
            pl.BlockSpec((None, D_MODEL, 2 * D_FF), lambda i, be, nu: (be[i], 0, 0)),
            pl.BlockSpec((None, 1, 2 * D_FF), lambda i, be, nu: (be[i], 0, 0)),
            pl.BlockSpec((None, D_FF, D_MODEL), lambda i, be, nu: (be[i], 0, 0)),
            pl.BlockSpec((None, 1, D_MODEL), lambda i, be, nu: (be[i], 0, 0)),
        ],
        out_specs=pl.BlockSpec((MOE_TILE, D_MODEL), lambda i, be, nu: (i, 0)),
        scratch_shapes=[pltpu.VMEM((D_MODEL, 2 * D_FF), BF16), pltpu.VMEM((D_FF, D_MODEL), BF16)],
    )
    return pl.pallas_call(
        _ffn_kernel,
        grid_spec=grid_spec,
        out_shape=jax.ShapeDtypeStruct((n_rows, D_MODEL), F32),
        compiler_params=pltpu.CompilerParams(dimension_semantics=("arbitrary",), vmem_limit_bytes=FFN_VMEM_LIMIT),
        name="expert_ffn",
    )(blk_e, n_used, rows, wgu, bgu, wd, bd)


def _token_tile(n):
    return next(t for t in (MOE_DMA_TOKENS, 384, 256, 128, n) if n % t == 0)


def _dispatch_scatter(dest_ref, h_ref, rows_ref, sem, td):
    def issue(i, carry):
        t0 = pl.multiple_of(i * 8, 8)
        for r in range(8):
            for k in range(TOP_K):
                d = dest_ref[i * (8 * TOP_K) + r * TOP_K + k]
                pltpu.make_async_copy(h_ref.at[pl.ds(t0 + r, 1)], rows_ref.at[pl.ds(d, 1)], sem).start()
        return carry

    lax.fori_loop(0, td // 8, issue, 0)
    for k in range(TOP_K):
        pltpu.make_async_copy(h_ref, rows_ref.at[pl.ds(0, td)], sem).wait()


def _dispatch_kernel(*refs, tiles, steps, n_blocks):
    ng = len(tiles)
    ends_ref, padded_ref, nu_ref = refs[:3]
    dest_refs = refs[3:3 + ng]
    h_refs = refs[3 + ng:3 + 2 * ng]
    rows_ref, zero_ref, sem, zsem = refs[3 + 2 * ng:]
    i = pl.program_id(0)

    @pl.when(i == 0)
    def _():
        zero_ref[...] = jnp.zeros_like(zero_ref)

        def tail_copy(e):
            start = pl.multiple_of(ends_ref[e] - MOE_TILE, MOE_TILE)
            return pltpu.make_async_copy(zero_ref, rows_ref.at[pl.ds(start, MOE_TILE)], zsem)

        def block_copy(blk):
            start = pl.multiple_of(blk * MOE_TILE, MOE_TILE)
            return pltpu.make_async_copy(zero_ref, rows_ref.at[pl.ds(start, MOE_TILE)], zsem)

        def each(start):
            def expert(e, carry):
                @pl.when(padded_ref[e] > 0)
                def _():
                    tail_copy(e).start() if start else tail_copy(e).wait()
                return carry

            def block(blk, carry):
                block_copy(blk).start() if start else block_copy(blk).wait()
                return carry

            lax.fori_loop(0, N_EXPERTS, expert, 0)
            lax.fori_loop(nu_ref[0], n_blocks, block, 0)

        each(True)
        each(False)

    first = 0
    for g in range(ng):
        @pl.when((i >= first) & (i < first + steps[g]))
        def _(g=g):
            _dispatch_scatter(dest_refs[g], h_refs[g], rows_ref, sem, tiles[g])
        first += steps[g]


def _moe_dispatch(h2s, dests, n_rows, pad_ends, padded, n_used):
    tiles = [_token_tile(h.shape[0]) for h in h2s]
    steps = [h.shape[0] // t for h, t in zip(h2s, tiles)]
    firsts = [sum(steps[:g]) for g in range(len(h2s))]

    def local(g):
        return lambda i, *_: jnp.clip(i - firsts[g], 0, steps[g] - 1)

    dest_specs = [pl.BlockSpec((tiles[g] * TOP_K,), lambda i, *_, f=local(g): (f(i),), memory_space=pltpu.SMEM)
                  for g in range(len(h2s))]
    width, dtype = h2s[0].shape[1], h2s[0].dtype
    tok_specs = [pl.BlockSpec((tiles[g], width), lambda i, *_, f=local(g): (f(i), 0)) for g in range(len(h2s))]
    grid_spec = pltpu.PrefetchScalarGridSpec(
        num_scalar_prefetch=3,
        grid=(sum(steps),),
        in_specs=dest_specs + tok_specs,
        out_specs=pl.BlockSpec(memory_space=pl.ANY),
        scratch_shapes=[pltpu.VMEM((MOE_TILE, width), dtype), pltpu.SemaphoreType.DMA(()),
                        pltpu.SemaphoreType.DMA(())],
    )
    return pl.pallas_call(
        functools.partial(_dispatch_kernel, tiles=tiles, steps=steps, n_blocks=n_rows // MOE_TILE),
        grid_spec=grid_spec,
        out_shape=jax.ShapeDtypeStruct((n_rows, width), dtype),
        compiler_params=_cparams("arbitrary"),
        name="moe_dispatch",
    )(pad_ends, padded, n_used, *dests, *h2s)


def _combine_kernel(dest_ref, dnext_ref, x2_ref, g_ref, rows_ref, o_ref, ybuf, sem, *, td):
    i = pl.program_id(0)
    last = pl.num_programs(0) - 1

    def start_tile(d_ref, slot):
        def issue(blk, carry):
            t0 = pl.multiple_of(blk * 8, 8)
            for r in range(8):
                for k in range(TOP_K):
                    d = d_ref[blk * (8 * TOP_K) + r * TOP_K + k]
                    pltpu.make_async_copy(rows_ref.at[pl.ds(d, 1)], ybuf.at[slot, k, pl.ds(t0 + r, 1)],
                                          sem.at[slot]).start()
            return carry

        lax.fori_loop(0, td // 8, issue, 0)

    @pl.when(i == 0)
    def _():
        start_tile(dest_ref, 0)

    @pl.when(i < last)
    def _():
        start_tile(dnext_ref, (i + 1) % 2)

    slot = i % 2
    for k in range(TOP_K):
        pltpu.make_async_copy(rows_ref.at[pl.ds(0, td)], ybuf.at[slot, k], sem.at[slot]).wait()
    out = x2_ref[...]
    for k in range(TOP_K):
        out = out + g_ref[:, k:k + 1] * ybuf[slot, k]
    o_ref[...] = out


def _moe_combine(x2, out_rows, dest, gates):
    n = x2.shape[0]
    td = _token_tile(n)
    steps = n // td
    return pl.pallas_call(
        functools.partial(_combine_kernel, td=td),
        grid=(steps,),
        in_specs=[pl.BlockSpec((td * TOP_K,), lambda i: (i,), memory_space=pltpu.SMEM),
                  pl.BlockSpec((td * TOP_K,), lambda i: (jnp.minimum(i + 1, steps - 1),), memory_space=pltpu.SMEM),
                  pl.BlockSpec((td, D_MODEL), lambda i: (i, 0)),
                  pl.BlockSpec((td, LANES), lambda i: (i, 0)),
                  pl.BlockSpec(memory_space=pl.ANY)],
        out_specs=pl.BlockSpec((td, D_MODEL), lambda i: (i, 0)),
        out_shape=jax.ShapeDtypeStruct((n, D_MODEL), F32),
        scratch_shapes=[pltpu.VMEM((2, TOP_K, td) + out_rows.shape[1:], out_rows.dtype),
                        pltpu.SemaphoreType.DMA((2,))],
        compiler_params=_cparams("arbitrary"),
        name="moe_combine",
    )(dest, dest, x2, gates, out_rows)


def _moe(groups, counts, w):
    n_total = sum(g[0].shape[0] for g in groups)
    cnt = counts[0, :N_EXPERTS].astype(jnp.int32)
    padded = (cnt + MOE_TILE - 1) // MOE_TILE * MOE_TILE
    pad_ends = jnp.cumsum(padded)
    pad_starts = pad_ends - padded
    n_blocks = -(-n_total * TOP_K // MOE_TILE) + N_EXPERTS
    blk_start = jnp.arange(n_blocks, dtype=jnp.int32) * MOE_TILE
    blk_e = jnp.minimum(jnp.sum(blk_start[:, None] >= pad_ends[None, :], axis=1), N_EXPERTS - 1).astype(jnp.int32)
    n_used = (pad_ends[-1] // MOE_TILE).astype(jnp.int32).reshape(1)
    dests = [(pad_starts[er[:, :TOP_K]] + er[:, TOP_K:2 * TOP_K]).astype(jnp.int32).reshape(-1)
             for (_, _, er, _) in groups]
    rows = _moe_dispatch([g[1] for g in groups], dests, n_blocks * MOE_TILE, pad_ends.astype(jnp.int32),
                         padded.astype(jnp.int32), n_used)
    out_rows = _expert_ffn(rows, blk_e, n_used, w["wgu"], w["bgu"], w["wd"], w["bd"])
    return [_moe_combine(x2, out_rows, dest, gates) for (x2, _, _, gates), dest in zip(groups, dests)]


def _rel_bucket(dist):
    n = jnp.maximum(dist, 0)
    max_exact = NUM_BUCKETS // 2
    nf = jnp.maximum(n, 1).astype(F32)
    large = max_exact + (jnp.log(nf / max_exact) / math.log(MAX_DISTANCE / max_exact)
                         * (NUM_BUCKETS - max_exact)).astype(jnp.int32)
    large = jnp.minimum(large, NUM_BUCKETS - 1)
    return jnp.where(n < max_exact, n, large)


def _bias_of(rel_bias, dist, valid):
    return jnp.where(valid[..., None], rel_bias[_rel_bucket(dist)], NEG)


def _proj_segs(kv_forms, kvc_forms):
    return ((0, 512, "qscale", ("rows",), BF16), (512, 256, "qscale", ("rows",), BF16),
            (768, 256, "id", kv_forms, F32), (1024, 256, "id", kv_forms, F32), (1280, 256, "id", kvc_forms, F32),
            (1536, 256, "id", ("rows",), F32), (1792, 128, "sigmoid", ("rows",), F32))


_PROJ_SEGS_PROMPT = _proj_segs(("key_rows", "t"), ("t",))
_PROJ_SEGS_SAMPLE = _proj_segs(("rows",), ("rows",))
_PROJ_NNORM = 1280


def _prep_layer(l, rel_bias, norm1_g, w_in, nsa_qk_norm, mem_qk_norm, cmp_w, cmp_pe, pool_w, pool_scale,
                mem_norm_g, w_mem_kv, w_up_pool, w_up_nsa, w_up_mem, w_out, norm2_g, router_w, router_b,
                w_gu, b_gu, w_down, b_down):
    wi = w_in[l]
    o_u, o_q, o_qm, o_kvc, o_kvs, o_kvw, o_gn, o_gb = 0, 256, 768, 1024, 1280, 1536, 1792, 1816
    w_proj = jnp.concatenate([
        wi[:, o_q:o_q + 512], wi[:, o_qm:o_qm + 256], wi[:, o_kvs:o_kvs + 256], wi[:, o_kvw:o_kvw + 256],
        wi[:, o_kvc:o_kvc + 256], wi[:, o_u:o_u + 256], wi[:, o_gn:o_gn + 24],
        jnp.zeros((D_MODEL, LANES - 24), F32)], axis=1).astype(BF16)
    nq, mq = nsa_qk_norm[l], mem_qk_norm[l]
    ones = jnp.ones((LANES,), F32)
    gain = jnp.concatenate([jnp.tile(nq[0], 8), jnp.tile(mq[0], 4), jnp.tile(nq[2], 2), ones,
                            jnp.tile(nq[3], 2), ones])[None, :]
    nmask = jnp.concatenate([jnp.ones((768,), F32), ones, 0 * ones, ones, 0 * ones])[None, :]
    eye4 = jnp.eye(4, dtype=F32)
    cw = cmp_w[l].reshape(2, 2, CMP_STRIDE, HEAD_DIM, HEAD_DIM)
    w_c = jnp.einsum("crjde,xy->cjxdrye", cw, jnp.eye(2, dtype=F32))
    w_c = w_c.reshape(2, CMP_STRIDE, LANES, KV_WIDTH).astype(BF16)
    pe = cmp_pe[l].reshape(2, 2, CMP_STRIDE, HEAD_DIM)
    pe_c = jnp.tile(pe.transpose(0, 2, 1, 3), (1, 1, 1, NSA_KV_HEADS))
    pe_c = jnp.pad(pe_c, ((0, 0), (0, 0), (0, 6), (0, 0)))
    w_pool = jnp.einsum("gde,gh->gdhe", pool_w[l], eye4).reshape(POOL_WIDTH, POOL_WIDTH).astype(BF16)
    rw = jnp.pad(router_w[l], ((0, 0), (0, LANES - N_EXPERTS))).astype(BF16)
    rb = jnp.concatenate([router_b[l], jnp.full((LANES - N_EXPERTS,), NEG, F32)])[None, :]
    return {
        "g1": norm1_g[l][None, :], "w_proj": w_proj, "gain": gain, "nmask": nmask,
        "w_kvc_t": wi[:, o_kvc:o_kvc + 256].T.astype(BF16),
        "gk_cmp": jnp.tile(nq[1], 2)[None, :], "w_c": w_c, "pe_c": pe_c,
        "w_pool": w_pool, "pool_scale": pool_scale[l][None, :],
        "mem_g": mem_norm_g[l][None, :], "w_mem": w_mem_kv[l].astype(BF16),
        "mem_gain": jnp.concatenate([jnp.tile(mq[1], 4), jnp.ones((256,), F32)])[None, :],
        "mem_nmask": jnp.concatenate([jnp.ones((256,), F32), jnp.zeros((256,), F32)])[None, :],
        "wgb": wi[:, o_gb:o_gb + 3 * D_MODEL].astype(BF16),
        "wup_pool": w_up_pool[l].astype(BF16), "wup_nsa": w_up_nsa[l].astype(BF16),
        "wup_mem": w_up_mem[l].astype(BF16), "wout": w_out[l].astype(BF16), "g2": norm2_g[l][None, :],
        "rw": rw, "rb": rb,
        "wgu": w_gu[l], "bgu": b_gu[l][:, None, :], "wd": w_down[l],
        "bd": b_down[l][:, None, :],
    }


def _project_in(x2d, w, segs, tm, seq):
    wt = w["w_kvc_t"] if segs is _PROJ_SEGS_PROMPT else None
    return _project(x2d, w["g1"], w["w_proj"], w["gain"], w["nmask"], segs, _PROJ_NNORM, tm, seq, wt)


def _rows_view(a_t, heads):
    b, _, rows = a_t.shape
    return a_t.reshape(b, 2, heads, HEAD_DIM, rows).transpose(0, 4, 1, 2, 3)


def _toeplitz(v, t):
    lead = v.shape[:-1]
    flat = jnp.tile(v, (1,) * len(lead) + (t,))[..., t:t + t * (2 * t - 1)]
    return flat.reshape(lead + (t, 2 * t - 1))[..., :t]


def _flash_tables(rel_bias):
    t = ATT_TILE
    d0 = jnp.arange(-t, t)
    kinds = jnp.stack([
        _bias_of(rel_bias, d0, d0 >= 0),
        _bias_of(rel_bias, d0 + t, d0 + t >= 0),
        _bias_of(rel_bias, jnp.full((2 * t,), 2 * t), jnp.ones((2 * t,), bool)),
        _bias_of(rel_bias, d0 + WINDOW, d0 + WINDOW < WINDOW),
    ])
    tab = _toeplitz(kinds.transpose(2, 0, 1), t)
    return tab.transpose(1, 2, 0, 3).reshape(4, t, NSA_HEADS * t)


def _cmp_bias_table(rel_bias, s, n_sub, n_cmp, pos0, tq):
    na = s // CMP_STRIDE
    m = max(na, n_sub)
    k = jnp.arange(-m, m)[None, :]
    r = jnp.arange(CMP_STRIDE)[:, None]
    d = CMP_STRIDE * k + r - (CMP_BLOCK - 1) + pos0
    v = _bias_of(rel_bias, d, d >= 0).transpose(2, 0, 1)
    tz = _toeplitz(v, m)[:, :, :n_sub, :na]
    tz = tz.reshape(NSA_KV_HEADS, NSA_HPG, CMP_STRIDE, n_sub, s // tq, tq // CMP_STRIDE)
    tab = tz.transpose(0, 3, 4, 1, 5, 2).reshape(NSA_KV_HEADS, n_sub, NSA_HPG * s)
    return jnp.where(jnp.arange(n_sub)[None, :, None] < n_cmp, tab, NEG)


def _prompt_pre(x, mem, w, rel_bias, cnt0):
    b, s, _ = x.shape
    n = b * s
    tm = 512 if n % 512 == 0 else ATT_TILE
    x2d = x.reshape(n, D_MODEL)
    q, qm, ks, kvs_t, kw, kvw_t, kvc_t, u, gn = _project_in(x2d, w, _PROJ_SEGS_PROMPT, tm, s)
    r3 = lambda a: a.reshape(b, s, a.shape[-1])
    q, qm, ks, kw, u, gn = map(r3, (q, qm, ks, kw, u, gn))

    o_pool = _pool_mix(u, jnp.zeros((b, 16, POOL_WIDTH), F32), w["w_pool"], w["pool_scale"], 0)

    n_cmp = (s - CMP_BLOCK) // CMP_STRIDE + 1
    n_slc = -(-s // SLC_BLOCK)
    n_lb = s // LANES
    own = jnp.broadcast_to(jnp.arange(b, dtype=jnp.int32)[:, None], (b, n_lb))
    blocks = jnp.broadcast_to(jnp.arange(n_lb, dtype=jnp.int32)[None, :], (b, n_lb))
    parts = _cmp_partials_paged(kvc_t, own, blocks, w["pe_c"], w["w_c"], math.gcd(n_lb, 16))
    n_sub = parts.shape[1]
    tq = math.gcd(s, 512)
    bias_c = _cmp_bias_table(rel_bias, s, n_sub, n_cmp, 0, tq)
    o_cmp, ns0, ns1 = _cmp_attention(q, parts, bias_c, gn, w["gk_cmp"], tq=tq, n_cmp=n_cmp, n_slc=n_slc, pos0=0)

    tab = _flash_tables(rel_bias)
    o_slc = _flash_attention(q, ns0, ns1, ks, kvs_t, tab, gn, use_sel=True, band=None, gate_base=NSA_HEADS)
    o_win = _flash_attention(q, ns0, ns1, kw, kvw_t, tab, gn, use_sel=False, band=WINDOW // ATT_TILE,
                             gate_base=2 * NSA_HEADS)

    m = mem.shape[1]
    mem_kv, mem_kv_t = _project(mem.reshape(b * m, D_MODEL), w["mem_g"], w["w_mem"], w["mem_gain"],
                                w["mem_nmask"], ((0, 2 * MEM_WIDTH, "id", ("rows", "t"), F32),), MXU_DIM,
                                tm=math.gcd(m, 512), seq=m)
    o_mem = _mem_attention(qm, mem_kv.reshape(b, m, 2 * MEM_WIDTH), tq=min(512, s))

    f2 = lambda a: a.reshape(n, a.shape[-1])
    x2, h2, eidx, gates, cnt = _layer_tail(x2d, f2(o_pool), f2(o_cmp), f2(o_slc), f2(o_win), f2(o_mem), cnt0, w, tm)
    win_t = kvw_t[:, :, max(0, s - WINDOW):]
    if s < WINDOW:
        win_t = jnp.pad(win_t, ((0, 0), (0, 0), (WINDOW - s, 0)))
    states = (_rows_view(kvc_t, NSA_KV_HEADS), _rows_view(kvs_t, NSA_KV_HEADS), _rows_view(win_t, NSA_KV_HEADS),
              _rows_view(mem_kv_t, MEM_HEADS), _last_rows(u, POOL_BUF))
    return (x2, h2, eidx, gates), states, cnt


def _last_rows(a, n):
    t = a.shape[1]
    if t < n:
        a = jnp.pad(a, [(0, 0), (n - t, 0)] + [(0, 0)] * (a.ndim - 2))
    return a[:, a.shape[1] - n:]


def _dec_columns_nsa(q):
    b, t, _ = q.shape
    qh = q.reshape(b, t, NSA_KV_HEADS, NSA_HPG, HEAD_DIM)
    w = jnp.einsum("btgpd,gx->bxdgtp", qh.astype(F32), jnp.eye(NSA_KV_HEADS, dtype=F32))
    w = w.reshape(b, NSA_KV_HEADS * HEAD_DIM, NSA_KV_HEADS * t * NSA_HPG)
    return jnp.pad(w, ((0, 0), (0, KV_WIDTH - w.shape[1]), (0, LANES - w.shape[2]))).astype(BF16)


def _dec_extract_nsa(o, t):
    b = o.shape[0]
    v = o[:, LANES:, :NSA_KV_HEADS * t * NSA_HPG]
    v = v.reshape(b, NSA_KV_HEADS, HEAD_DIM, NSA_KV_HEADS, t, NSA_HPG)
    v = jnp.einsum("bxdgtp,gx->btgpd", v, jnp.eye(NSA_KV_HEADS, dtype=F32))
    return v.reshape(b, t, NSA_WIDTH)


def _dec_bias_cols(bias_tph):
    k, t, _ = bias_tph.shape
    bt = bias_tph.reshape(k, t, NSA_KV_HEADS, NSA_HPG).transpose(0, 2, 1, 3).reshape(k, NSA_KV_HEADS * t * NSA_HPG)
    return jnp.pad(bt, ((0, 0), (0, LANES - bt.shape[1])))


def _dec_gate_cols(gn, base, t):
    b = gn.shape[0]
    gt = gn[:, :, base:base + NSA_HEADS].reshape(b, t, NSA_KV_HEADS, NSA_HPG).transpose(0, 2, 1, 3)
    gt = gt.reshape(b, 1, NSA_KV_HEADS * t * NSA_HPG)
    return jnp.pad(gt, ((0, 0), (0, 0), (0, LANES - gt.shape[2])), constant_values=1.0)


def _sample_pre(x, cache_cmp, cache_slc, cache_win, cache_mem, pool_buf, page_table, w, rel_bias, cnt0):
    b, t, _ = x.shape
    n = b * t
    page = cache_cmp.shape[1]
    n_pages = page_table.shape[1]
    past = n_pages * page
    x2d = x.reshape(n, D_MODEL)
    tm = n if n <= 512 else LANES
    q, qm, kvs, kvw, kvc, u, gn = _project_in(x2d, w, _PROJ_SEGS_SAMPLE, tm, tm)
    r3 = lambda a: a.reshape(b, t, a.shape[-1])
    q, qm, kvs, kvw, kvc, u, gn = map(r3, (q, qm, kvs, kvw, kvc, u, gn))
    qpos = past + jnp.arange(t)

    buf16 = jnp.pad(pool_buf, ((0, 0), (16 - POOL_BUF, 0), (0, 0)))
    o_pool = _pool_mix(u, buf16, w["w_pool"], w["pool_scale"], past)

    total = past + t
    n_cmp = (total - CMP_BLOCK) // CMP_STRIDE + 1
    n_sub_used = n_cmp + CMP_BLOCK // CMP_STRIDE - 1
    n_slc = -(-total // SLC_BLOCK)
    pps = math.gcd(n_pages, 16)
    parts = _cmp_partials_paged(_feature_major(cache_cmp), page_table, jnp.zeros_like(page_table), w["pe_c"],
                                w["w_c"], pps)
    extra = n_sub_used * CMP_STRIDE - past
    if extra > 0:
        tail_rows = -(-extra // CMP_STRIDE) * CMP_STRIDE
        new_c = jnp.pad(kvc, ((0, 0), (0, max(0, tail_rows - t)), (0, 0)))[:, :tail_rows]
        parts = jnp.concatenate([parts, _cmp_partials_dense(new_c, w["pe_c"], w["w_c"])], axis=1)
    n_sub = parts.shape[1]
    end = jnp.arange(n_sub)[None, :] * CMP_STRIDE + CMP_BLOCK - 1
    bias_c = _bias_of(rel_bias, qpos[:, None] - end, (end <= qpos[:, None]) & (jnp.arange(n_sub)[None, :] < n_cmp))
    qpad = ((0, 0), (0, LANES - t), (0, 0))
    bias_c = _cmp_bias_cols(jnp.pad(bias_c.transpose(2, 0, 1), qpad), LANES)
    o_cmp, ns0, ns1 = _cmp_attention(jnp.pad(q, qpad), parts, bias_c, jnp.pad(gn, qpad), w["gk_cmp"], tq=LANES,
                                     n_cmp=n_cmp, n_slc=n_slc, pos0=past)
    o_cmp = o_cmp[:, :t]

    wq = _dec_columns_nsa(q).transpose(0, 2, 1)
    ncol = NSA_KV_HEADS * t * NSA_HPG
    new_tile = lambda kv: jnp.pad(kv, ((0, 0), (0, LANES - t), (0, 0))).transpose(0, 2, 1)
    own = lambda npg: jnp.broadcast_to(jnp.arange(b, dtype=jnp.int32)[:, None], (b, npg))
    blocks = lambda npg: jnp.broadcast_to(jnp.arange(npg, dtype=jnp.int32)[None, :], (b, npg))

    n_chunks = n_pages + 1
    nblk = -(-2 * n_chunks // 8) * 8
    ns = jnp.stack([ns0, ns1], axis=1)[:, :, :, :t].transpose(0, 1, 3, 2)
    ns = jnp.pad(ns, ((0, 0), (0, 0), (0, 0), (0, max(0, nblk - ns.shape[3]))))[..., :nblk]
    ns = jnp.broadcast_to(ns[:, :, :, None, :], (b, NSA_KV_HEADS, t, NSA_HPG, nblk)).reshape(b, ncol, nblk)
    notsel = jnp.pad(ns.transpose(0, 2, 1), ((0, 0), (0, 0), (0, LANES - ncol)))
    rows = jnp.arange(LANES)
    far = _bias_of(rel_bias, jnp.full((LANES, t), 2 * MAX_DISTANCE), jnp.ones((LANES, t), bool))
    kpos_last = past - LANES + rows
    d_last = qpos[None, :] - kpos_last[:, None]
    near = _bias_of(rel_bias, d_last, d_last >= 0)
    kpos_new = past + rows
    d_new = qpos[None, :] - kpos_new[:, None]
    newb = _bias_of(rel_bias, d_new, (d_new >= 0) & (rows[:, None] < t))
    bias_tab = jnp.stack([_dec_bias_cols(far), _dec_bias_cols(near), _dec_bias_cols(newb)])
    bias_idx = jnp.concatenate([jnp.zeros((n_pages - 1,), jnp.int32), jnp.array([1, 2], jnp.int32)])
    o_slc = _decode_attention(_feature_major(cache_slc), page_table, jnp.zeros_like(page_table), new_tile(kvs), wq,
                              bias_tab, bias_idx, notsel, _dec_gate_cols(gn, NSA_HEADS, t))
    o_slc = _dec_extract_nsa(o_slc, t)

    wb = cache_win.shape[1]
    n_wpg = wb // LANES
    kpos_w = past - wb + jnp.arange(wb + LANES)
    d_w = qpos[None, :] - kpos_w[:, None]
    valid_w = (d_w >= 0) & (d_w < WINDOW) & (kpos_w[:, None] >= 0) & (jnp.arange(wb + LANES)[:, None] < wb + t)
    bias_w = _dec_bias_cols(_bias_of(rel_bias, d_w, valid_w)).reshape(n_wpg + 1, LANES, LANES)
    zeros_ns = jnp.zeros((b, -(-2 * (n_wpg + 1) // 8) * 8, LANES), F32)
    o_win = _decode_attention(_feature_major(cache_win), own(n_wpg), blocks(n_wpg), new_tile(kvw), wq, bias_w,
                              jnp.arange(n_wpg + 1, dtype=jnp.int32), zeros_ns, _dec_gate_cols(gn, 2 * NSA_HEADS, t))
    o_win = _dec_extract_nsa(o_win, t)

    m = cache_mem.shape[1]
    n_mpg = m // LANES
    qmh = qm.reshape(b, t, MEM_HEADS, HEAD_DIM).astype(F32)
    wqm = jnp.einsum("bthd,hx->bxdht", qmh, jnp.eye(MEM_HEADS, dtype=F32))
    wqm = wqm.reshape(b, MEM_WIDTH, MEM_HEADS * t)
    wqm = jnp.pad(wqm, ((0, 0), (0, MEM_WIDTH), (0, LANES - MEM_HEADS * t))).astype(BF16).transpose(0, 2, 1)
    o_mem = _decode_attention(_feature_major(cache_mem), own(n_mpg), blocks(n_mpg), None, wqm,
                              jnp.zeros((1, LANES, LANES), F32), jnp.zeros((n_mpg,), jnp.int32),
                              jnp.zeros((b, 8, LANES), F32), jnp.ones((b, 1, LANES), F32))
    om = o_mem[:, MEM_WIDTH:, :MEM_HEADS * t].reshape(b, MEM_HEADS, HEAD_DIM, MEM_HEADS, t)
    o_mem = jnp.einsum("bxdht,hx->bthd", om, jnp.eye(MEM_HEADS, dtype=F32)).reshape(b, t, MEM_WIDTH)

    f2 = lambda a: a.reshape(n, a.shape[-1])
    x2, h2, eidx, gates, cnt = _layer_tail(x2d, f2(o_pool), f2(o_cmp), f2(o_slc), f2(o_win), f2(o_mem), cnt0, w,
                                           n if n <= 512 else LANES)
    kvshape = (b, t, 2, NSA_KV_HEADS, HEAD_DIM)
    new_win = _rows_view(jnp.concatenate([_feature_major(cache_win)[:, :, t:], kvw.transpose(0, 2, 1)], axis=2),
                         NSA_KV_HEADS)
    new_pool = jnp.concatenate([pool_buf, u], axis=1)[:, t:]
    states = (kvc.reshape(kvshape), kvs.reshape(kvshape), new_win, new_pool)
    return (x2, h2, eidx, gates), states, cnt


def kernel(x_prompt, x_sample, cache_cmp_kv, cache_slc_kv, cache_win_kv, cache_mem_kv, state_pool, page_table,
           mem_prompt, rel_bias, norm1_g, w_in, nsa_qk_norm, mem_qk_norm, cmp_w, cmp_pe, pool_w, pool_scale,
           mem_norm_g, w_mem_kv, w_up_pool, w_up_nsa, w_up_mem, w_out, norm2_g, router_w, router_b, w_gu, b_gu,
           w_down, b_down):
    depth = w_in.shape[0]
    yp, ys = x_prompt, x_sample
    bp, sp, _ = x_prompt.shape
    bs, ts, _ = x_sample.shape
    outs_p = [[] for _ in range(5)]
    outs_s = [[] for _ in range(4)]
    for l in range(depth):
        w = _prep_layer(l, rel_bias, norm1_g, w_in, nsa_qk_norm, mem_qk_norm, cmp_w, cmp_pe, pool_w, pool_scale,
                        mem_norm_g, w_mem_kv, w_up_pool, w_up_nsa, w_up_mem, w_out, norm2_g, router_w, router_b,
                        w_gu, b_gu, w_down, b_down)
        pre_p, st_p, cnt = _prompt_pre(yp, mem_prompt, w, rel_bias, jnp.zeros((1, LANES), F32))
        pre_s, st_s, cnt = _sample_pre(ys, cache_cmp_kv[l], cache_slc_kv[l], cache_win_kv[l], cache_mem_kv[l],
                                       state_pool[l], page_table, w, rel_bias, cnt)
        yp, ys = _moe([pre_p, pre_s], cnt, w)
        yp = yp.reshape(bp, sp, D_MODEL)
        ys = ys.reshape(bs, ts, D_MODEL)
        for lst, a in zip(outs_p, st_p):
            lst.append(a)
        for lst, a in zip(outs_s, st_s):
            lst.append(a)
    new_cmp_p, new_slc_p, new_win_p, new_mem_p, new_pool_p = [jnp.stack(a) for a in outs_p]
    new_cmp_s, new_slc_s, new_win_s, new_pool_s = [jnp.stack(a) for a in outs_s]
    new_win_s = new_win_s.reshape(new_win_s.shape[:3] + (2, NSA_KV_HEADS, HEAD_DIM))
    return (yp, ys, new_cmp_p, new_slc_p, new_win_p, new_mem_p, new_pool_p,
            new_cmp_s, new_slc_s, new_win_s, new_pool_s)
```

```python
import functools
import math

import jax
import jax.numpy as jnp
from jax import lax
from jax.experimental import pallas as pl
from jax.experimental.pallas import tpu as pltpu

F32 = jnp.float32
BF16 = jnp.bfloat16

D_MODEL = 1024
HEAD_DIM = 64
POOL_WINDOWS = (2, 4, 8, 16)
POOL_GROUP = 64
POOL_WIDTH = 256
POOL_BUF = 15
NSA_HEADS = 8
NSA_KV_HEADS = 2
NSA_HPG = 4
NSA_WIDTH = 512
KV_WIDTH = 256
CMP_BLOCK = 32
CMP_STRIDE = 16
SLC_BLOCK = 64
SLC_TOPK = 16
WINDOW = 512
MEM_HEADS = 4
MEM_WIDTH = 256
NUM_BUCKETS = 32
MAX_DISTANCE = 128
N_EXPERTS = 32
TOP_K = 4
D_FF = 1024
SWIGLU_ALPHA = 1.702
SWIGLU_LIMIT = 7.0
EPS = 1e-6
SCALE = HEAD_DIM ** -0.5

LANES = 128
MXU_DIM = 256
NEG = -1e30
ATT_TILE = 256
DEC_PAGES_PER_STEP = 32
MOE_TILE = 512
MOE_DMA_TOKENS = 512
VMEM_LIMIT = 48 * 1024 * 1024
FFN_VMEM_LIMIT = 56 * 1024 * 1024


def _cparams(*sem):
    return pltpu.CompilerParams(dimension_semantics=sem, vmem_limit_bytes=VMEM_LIMIT)


def _dot(a, b):
    return jnp.dot(a, b, preferred_element_type=F32)


def _dot_nt(a, b):
    return lax.dot_general(a, b, (((1,), (1,)), ((), ())), preferred_element_type=F32)


def _split_dot(a, b):
    hi = a.astype(BF16)
    lo = (a - hi.astype(F32)).astype(BF16)
    return _dot(hi, b) + _dot(lo, b)


def _rms(x, g):
    r = lax.rsqrt(jnp.mean(x * x, axis=-1, keepdims=True) + EPS)
    return (x * r) * g


def _sigmoid(x):
    return 1.0 / (1.0 + jnp.exp(-x))


def _pack_bf16_pairs(x):
    w = x.shape[1] // 2
    bits = lax.bitcast_convert_type(x.astype(BF16).astype(F32), jnp.uint32)
    return jnp.right_shift(bits[:, :w], jnp.uint32(16)) | (bits[:, w:] & jnp.uint32(0xFFFF0000))


def _unpack_bf16_pairs(p):
    lo = lax.bitcast_convert_type(jnp.left_shift(p, jnp.uint32(16)), F32)
    hi = lax.bitcast_convert_type(p & jnp.uint32(0xFFFF0000), F32)
    return jnp.concatenate([lo, hi], axis=1)


def _lane_iota(shape):
    return lax.broadcasted_iota(jnp.int32, shape, len(shape) - 1)


def _row_iota(shape):
    return lax.broadcasted_iota(jnp.int32, shape, len(shape) - 2)


def _proj_kernel(x_ref, g_ref, w_ref, gain_ref, nmask_ref, seg_ref, *rest, segs, n_norm, has_wt):
    wt_ref = rest[0] if has_wt else None
    out_refs = rest[1:] if has_wt else rest
    h = _rms(x_ref[...], g_ref[...]).astype(BF16)
    seg = seg_ref[...]
    outs = iter(out_refs)
    for (start, width, kind, forms, _) in segs:
        if forms == ("t",) and has_wt:
            next(outs)[...] = _dot_nt(wt_ref[...], h)
            continue
        z = _dot(h, w_ref[:, start:start + width])
        if start < n_norm:
            pieces = []
            for c in range(0, width, MXU_DIM):
                zc = z[:, c:c + MXU_DIM]
                ms = _split_dot(zc * zc, seg)
                zn = (zc * lax.rsqrt(ms + EPS)) * gain_ref[:, start + c:start + c + MXU_DIM]
                pieces.append(jnp.where(nmask_ref[:, start + c:start + c + MXU_DIM] > 0, zn, zc))
            z = pieces[0] if len(pieces) == 1 else jnp.concatenate(pieces, axis=1)
        if kind == "sigmoid":
            z = _sigmoid(z)
        elif kind == "qscale":
            z = z * SCALE
        for form in forms:
            o_ref = next(outs)
            if form == "rows":
                o_ref[...] = z.astype(o_ref.dtype)
            elif form == "key_rows":
                o_ref[...] = z[:, 0:LANES]
            else:
                o_ref[...] = z.T


def _project(x, g, w, gain, nmask, segs, n_norm, tm, seq, wt=None):
    n = x.shape[0]
    ncol = w.shape[1]
    seg = _seg_matrix(MXU_DIM)
    full = lambda i: (0, 0)
    tpb = seq // tm
    out_specs, out_shape = [], []
    for (_, wd, _, forms, dt) in segs:
        for form in forms:
            if form == "rows":
                out_specs.append(pl.BlockSpec((tm, wd), lambda i: (i, 0)))
                out_shape.append(jax.ShapeDtypeStruct((n, wd), dt))
            elif form == "key_rows":
                out_specs.append(pl.BlockSpec((tm, LANES), lambda i: (i, 0)))
                out_shape.append(jax.ShapeDtypeStruct((n, LANES), F32))
            else:
                out_specs.append(pl.BlockSpec((None, wd, tm), lambda i: (i // tpb, 0, i % tpb)))
                out_shape.append(jax.ShapeDtypeStruct((n // seq, wd, seq), F32))
    extra = () if wt is None else (wt,)
    return pl.pallas_call(
        functools.partial(_proj_kernel, segs=segs, n_norm=n_norm, has_wt=wt is not None),
        grid=(n // tm,),
        in_specs=[
            pl.BlockSpec((tm, D_MODEL), lambda i: (i, 0)),
            pl.BlockSpec((1, D_MODEL), full),
            pl.BlockSpec((D_MODEL, ncol), full),
            pl.BlockSpec((1, gain.shape[1]), full),
            pl.BlockSpec((1, nmask.shape[1]), full),
            pl.BlockSpec((MXU_DIM, MXU_DIM), full),
        ] + [pl.BlockSpec(a.shape, full) for a in extra],
        out_specs=out_specs,
        out_shape=out_shape,
        compiler_params=_cparams("parallel"),
        name="proj",
    )(x, g, w, gain, nmask, seg, *extra)


def _seg_matrix(n):
    i = jnp.arange(n) // HEAD_DIM
    return jnp.where(i[:, None] == i[None, :], 1.0 / HEAD_DIM, 0.0).astype(BF16)


def _pool_kernel(u_ref, buf_ref, w_ref, scale_ref, o_ref, zs_ref, *, t, pos0):
    zs_ref[0:16, :] = buf_ref[...]
    zs_ref[16:16 + t, :] = u_ref[...]
    u = u_ref[...]
    lane = _lane_iota((1, POOL_WIDTH))
    pos = (pos0 + _row_iota((t, 1))).astype(F32)
    acc = u
    mean = None
    for i in range(1, max(POOL_WINDOWS)):
        acc = acc + zs_ref[16 - i:16 - i + t, :]
        if i + 1 in POOL_WINDOWS:
            gi = POOL_WINDOWS.index(i + 1)
            m = acc / jnp.minimum(pos + 1.0, float(i + 1))
            mean = m if mean is None else jnp.where(lane >= gi * POOL_GROUP, m, mean)
    d = (mean - u).astype(BF16)
    o_ref[...] = _dot(d, w_ref[...]) * scale_ref[...]


def _pool_mix(u, buf16, w_bd, scale, pos0):
    b, t, _ = u.shape
    return pl.pallas_call(
        functools.partial(_pool_kernel, t=t, pos0=pos0),
        grid=(b,),
        in_specs=[
            pl.BlockSpec((None, t, POOL_WIDTH), lambda i: (i, 0, 0)),
            pl.BlockSpec((None, 16, POOL_WIDTH), lambda i: (i, 0, 0)),
            pl.BlockSpec((POOL_WIDTH, POOL_WIDTH), lambda i: (0, 0)),
            pl.BlockSpec((1, POOL_WIDTH), lambda i: (0, 0)),
        ],
        out_specs=pl.BlockSpec((None, t, POOL_WIDTH), lambda i: (i, 0, 0)),
        out_shape=jax.ShapeDtypeStruct((b, t, POOL_WIDTH), F32),
        scratch_shapes=[pltpu.VMEM((t + 16, POOL_WIDTH), F32)],
        compiler_params=_cparams("parallel"),
        name="pool",
    )(u, buf16, w_bd, scale)


def _cpart_compute(rows_of, pe_ref, w_ref, o_ref, m):
    for c in range(2):
        acc = jnp.zeros((m + 8, KV_WIDTH), F32)
        for j in range(CMP_STRIDE):
            lhs = jnp.concatenate([rows_of(c, j), pe_ref[c, j]], axis=0)
            acc = acc + _dot(lhs.astype(BF16), w_ref[c, j])
        lane = _lane_iota((1, KV_WIDTH))
        pe_term = jnp.where(lane < LANES, acc[m:m + 1], acc[m + 1:m + 2])
        o_ref[:, c * KV_WIDTH:(c + 1) * KV_WIDTH] = acc[0:m] + pe_term


def _cpart_kernel(xk_ref, xv_ref, pe_ref, w_ref, o_ref, *, rows):
    m = rows // CMP_STRIDE
    x_refs = (xk_ref, xv_ref)
    _cpart_compute(lambda c, j: x_refs[c][pl.ds(j, m, stride=CMP_STRIDE), :], pe_ref, w_ref, o_ref, m)


def _cpart_paged_kernel(*refs, nop, page):
    page_refs = refs[2:2 + nop]
    perm_ref, pe_ref, w_ref, o_ref, xs_ref = refs[2 + nop:]
    n = page // CMP_STRIDE
    for k, r in enumerate(page_refs):
        y = _dot_nt(perm_ref[...], r[...].astype(BF16))
        for j in range(CMP_STRIDE):
            for c in range(2):
                xs_ref[c, j, k * n:(k + 1) * n, :] = y[j * n:(j + 1) * n, c * LANES:(c + 1) * LANES]
    _cpart_compute(lambda c, j: xs_ref[c, j], pe_ref, w_ref, o_ref, nop * n)


def _cmp_partials_dense(kv, pe, w_c):
    b, t, _ = kv.shape
    rows = (t // CMP_STRIDE) * CMP_STRIDE
    n = rows // CMP_STRIDE
    return pl.pallas_call(
        functools.partial(_cpart_kernel, rows=rows),
        grid=(b,),
        in_specs=[
            pl.BlockSpec((None, rows, LANES), lambda i: (i, 0, 0)),
            pl.BlockSpec((None, rows, LANES), lambda i: (i, 0, 1)),
            pl.BlockSpec(pe.shape, lambda i: (0, 0, 0, 0)),
            pl.BlockSpec(w_c.shape, lambda i: (0, 0, 0, 0)),
        ],
        out_specs=pl.BlockSpec((None, n, 2 * KV_WIDTH), lambda i: (i, 0, 0)),
        out_shape=jax.ShapeDtypeStruct((b, n, 2 * KV_WIDTH), F32),
        compiler_params=_cparams("parallel"),
        name="cmp_partials",
    )(kv, kv, pe, w_c)


def _cmp_partials_paged(pool_t, phys, lblk, pe, w_c, pages_per_step):
    b, n_pages = phys.shape
    page = LANES
    nop = pages_per_step
    n = nop * page // CMP_STRIDE

    def page_spec(k):
        return pl.BlockSpec((None, KV_WIDTH, page),
                            lambda i, c, ph, lb: (ph[i, c * nop + k], 0, lb[i, c * nop + k]))

    r_idx = jnp.arange(page)
    regroup = (r_idx[None, :] == ((r_idx % (page // CMP_STRIDE)) * CMP_STRIDE + r_idx // (page // CMP_STRIDE))[:, None])
    regroup = regroup.astype(BF16)
    grid_spec = pltpu.PrefetchScalarGridSpec(
        num_scalar_prefetch=2,
        grid=(b, n_pages // nop),
        in_specs=[page_spec(k) for k in range(nop)] + [
            pl.BlockSpec(regroup.shape, lambda i, c, ph, lb: (0, 0)),
            pl.BlockSpec(pe.shape, lambda i, c, ph, lb: (0, 0, 0, 0)),
            pl.BlockSpec(w_c.shape, lambda i, c, ph, lb: (0, 0, 0, 0)),
        ],
        out_specs=pl.BlockSpec((None, n, 2 * KV_WIDTH), lambda i, c, ph, lb: (i, c, 0)),
        scratch_shapes=[pltpu.VMEM((2, CMP_STRIDE, n, LANES), F32)],
    )
    return pl.pallas_call(
        functools.partial(_cpart_paged_kernel, nop=nop, page=page),
        grid_spec=grid_spec,
        out_shape=jax.ShapeDtypeStruct((b, n_pages * page // CMP_STRIDE, 2 * KV_WIDTH), F32),
        compiler_params=_cparams("parallel", "arbitrary"),
        name="cmp_partials_paged",
    )(phys, lblk, *([pool_t] * nop), regroup, pe, w_c)


def _group_query_columns(q_ref, g, t):
    zeros64 = jnp.zeros((HEAD_DIM, t), BF16)
    cols = []
    for pr in range(2):
        qt = q_ref[:, (2 * g + pr) * LANES:(2 * g + pr + 1) * LANES].astype(F32).T.astype(BF16)
        for half in range(2):
            qh = qt[half * HEAD_DIM:(half + 1) * HEAD_DIM]
            cols.append(jnp.concatenate([qh, zeros64] if g == 0 else [zeros64, qh], axis=0))
    return jnp.concatenate(cols, axis=1)


def _store_group_output(o_ref, out_t, g, t):
    for pr in range(2):
        pair = jnp.concatenate([out_t[:, (2 * pr) * t:(2 * pr + 1) * t], out_t[:, (2 * pr + 1) * t:(2 * pr + 2) * t]],
                               axis=0)
        o_ref[:, (2 * g + pr) * LANES:(2 * g + pr + 1) * LANES] = pair.T


def _cattn_kernel(q_ref, p_ref, bias_ref, gn_ref, gk_ref, seg_ref, o_ref, ns0_ref, ns1_ref, *,
                  tq, n_sub, n_cmp, n_slc, nslp, pos0):
    qi = pl.program_id(0)
    pall = p_ref[...]
    kraw = pall[:, 0:LANES] + pltpu.roll(pall[:, LANES:2 * LANES], n_sub - 1, 0)
    vc = pall[:, 2 * LANES:3 * LANES] + pltpu.roll(pall[:, 3 * LANES:4 * LANES], n_sub - 1, 0)
    ms = _split_dot(kraw * kraw, seg_ref[...])
    kc = ((kraw * lax.rsqrt(ms + EPS)) * gk_ref[...]).astype(BF16)
    vct = vc.T.astype(BF16)

    nsel = -(-n_slc // 8) * 8
    jj = _row_iota((nsel, n_sub))
    nn = _lane_iota((nsel, n_sub))
    covers_t = ((nn * CMP_STRIDE < (jj + 1) * SLC_BLOCK) & (nn * CMP_STRIDE + CMP_BLOCK - 1 >= jj * SLC_BLOCK)
                & (nn < n_cmp) & (jj < n_slc))
    covers_t = jnp.where(covers_t, 1.0, 0.0).astype(BF16)
    qpos = pos0 + qi * tq + _lane_iota((1, tq))
    qblk = jnp.right_shift(qpos, SLC_BLOCK.bit_length() - 1)
    jr = _row_iota((nsel, tq))
    jrf = jr.astype(F32)
    forced = (jr == 0) | (jr == qblk) | (jr == qblk - 1)
    causal = jr <= qblk
    gnt = gn_ref[...].T

    for g in range(NSA_KV_HEADS):
        bias = bias_ref[g]
        s = _dot(kc, _group_query_columns(q_ref, g, tq)) + bias
        m = jnp.max(s, axis=0, keepdims=True)
        m = jnp.where(m > 0.5 * NEG, m, 0.0)
        e = jnp.where(bias > 0.5 * NEG, jnp.exp(s - m), 0.0)
        p = e / jnp.maximum(jnp.sum(e, axis=0, keepdims=True), 1e-30)
        h0 = NSA_HPG * g
        gate = jnp.concatenate([gnt[h0 + c:h0 + c + 1] for c in range(NSA_HPG)], axis=1)
        out_t = _dot(vct[g * HEAD_DIM:(g + 1) * HEAD_DIM], p.astype(BF16)) * gate
        _store_group_output(o_ref, out_t, g, tq)

        prsum = p[:, 0:tq] + p[:, tq:2 * tq] + p[:, 2 * tq:3 * tq] + p[:, 3 * tq:4 * tq]
        hi = prsum.astype(BF16)
        lo = (prsum - hi.astype(F32)).astype(BF16)
        imp = _dot(covers_t, hi) + _dot(covers_t, lo)
        score = jnp.where(forced, jnp.inf, imp)
        score = jnp.where(causal, score, -jnp.inf)
        sel = jnp.zeros((nsel, tq), F32)
        for _ in range(min(SLC_TOPK, n_slc)):
            m = jnp.max(score, axis=0, keepdims=True)
            idx = jnp.min(jnp.where(score == m, jrf, 1e9), axis=0, keepdims=True)
            pick = jrf == idx
            sel = jnp.where(pick & (m > -jnp.inf), 1.0, sel)
            score = jnp.where(pick, -jnp.inf, score)
        ns = 1.0 - sel
        if nslp > nsel:
            ns = jnp.concatenate([ns, jnp.ones((nslp - nsel, tq), F32)], axis=0)
        (ns0_ref if g == 0 else ns1_ref)[...] = ns


def _cmp_bias_cols(bias, tq):
    _, s, n_sub = bias.shape
    bias_t = bias.reshape(NSA_KV_HEADS, NSA_HPG, s // tq, tq, n_sub).transpose(0, 4, 2, 1, 3)
    return bias_t.reshape(NSA_KV_HEADS, n_sub, NSA_HPG * s)


def _cmp_attention(q, parts, bias_t, gn, gk, *, tq, n_cmp, n_slc, pos0):
    b, s, _ = q.shape
    n_sub = parts.shape[1]
    nslp = -(-n_slc // LANES) * LANES
    kern = functools.partial(_cattn_kernel, tq=tq, n_sub=n_sub, n_cmp=n_cmp, n_slc=n_slc, nslp=nslp, pos0=pos0)
    return pl.pallas_call(
        kern,
        grid=(s // tq, b),
        in_specs=[
            pl.BlockSpec((None, tq, NSA_WIDTH), lambda i, j: (j, i, 0)),
            pl.BlockSpec((None, n_sub, 2 * KV_WIDTH), lambda i, j: (j, 0, 0)),
            pl.BlockSpec((NSA_KV_HEADS, n_sub, NSA_HPG * tq), lambda i, j: (0, 0, i)),
            pl.BlockSpec((None, tq, LANES), lambda i, j: (j, i, 0)),
            pl.BlockSpec((1, LANES), lambda i, j: (0, 0)),
            pl.BlockSpec((LANES, LANES), lambda i, j: (0, 0)),
        ],
        out_specs=[
            pl.BlockSpec((None, tq, NSA_WIDTH), lambda i, j: (j, i, 0)),
            pl.BlockSpec((None, nslp, tq), lambda i, j: (j, 0, i)),
            pl.BlockSpec((None, nslp, tq), lambda i, j: (j, 0, i)),
        ],
        out_shape=[
            jax.ShapeDtypeStruct((b, s, NSA_WIDTH), F32),
            jax.ShapeDtypeStruct((b, nslp, s), F32),
            jax.ShapeDtypeStruct((b, nslp, s), F32),
        ],
        compiler_params=_cparams("parallel", "parallel"),
        name="cmp_attention",
    )(q, parts, bias_t, gn, gk, _seg_matrix(LANES))


def _flash_kernel(q_ref, ns0_ref, ns1_ref, kv_ref, vt_ref, tab_ref, gn_ref, o_ref, *, t, use_sel, band, gate_base):
    qi = pl.program_id(1)
    row_k = _row_iota((t, LANES))
    lane_k = _lane_iota((t, LANES))
    lo_tile = jnp.maximum(qi - band, 0) if band is not None else 0
    gnt = gn_ref[...].T

    qts = []
    for g in range(NSA_KV_HEADS):
        qt_g = _group_query_columns(q_ref, g, t)
        if use_sel:
            nst = (ns0_ref if g == 0 else ns1_ref)[...].astype(BF16)
            qt_g = jnp.concatenate([qt_g, jnp.concatenate([nst] * NSA_HPG, axis=1)], axis=0)
        qts.append(qt_g)

    def body(kj, carry):
        k0 = pl.multiple_of(kj * t, t)
        kk = kv_ref[pl.ds(k0, t), :].astype(BF16)
        if use_sel:
            blk = kj * (t // SLC_BLOCK) + jnp.right_shift(row_k, SLC_BLOCK.bit_length() - 1)
            onehot = jnp.where(lane_k == blk, -(2.0 ** 30), 0.0).astype(BF16)
            kk = jnp.concatenate([kk, onehot], axis=1)
        delta = qi - kj
        if band is None:
            kind = jnp.minimum(delta, 2)
        else:
            kind = jnp.where(delta < 2, delta, jnp.where(delta < band, 2, 3))
        m_old, l_old, acc = carry
        s = _dot(kk, qt_all) + tab_ref[kind]
        m_new = jnp.maximum(m_old, jnp.max(s, axis=0, keepdims=True))
        alpha = jnp.exp(m_old - m_new)
        p = jnp.exp(s - m_new)
        l_new = alpha * l_old + jnp.sum(p, axis=0, keepdims=True)
        pb = p.astype(BF16)
        pv = jnp.concatenate(
            [_dot(vt_ref[g * HEAD_DIM:(g + 1) * HEAD_DIM, pl.ds(k0, t)].astype(BF16), pb[:, g * gw:(g + 1) * gw])
             for g in range(NSA_KV_HEADS)], axis=1)
        return m_new, l_new, acc * alpha + pv

    gw = NSA_HPG * t
    qt_all = jnp.concatenate(qts, axis=1)
    init = (jnp.full((1, NSA_HEADS * t), NEG, F32), jnp.zeros((1, NSA_HEADS * t), F32),
            jnp.zeros((HEAD_DIM, NSA_HEADS * t), F32))
    _, l_fin, acc = lax.fori_loop(lo_tile, qi + 1, body, init)
    gate = jnp.concatenate([gnt[gate_base + h:gate_base + h + 1] for h in range(NSA_HEADS)], axis=1)
    out = acc * (gate / l_fin)
    for g in range(NSA_KV_HEADS):
        _store_group_output(o_ref, out[:, g * gw:(g + 1) * gw], g, t)


def _flash_attention(q, ns0, ns1, k_rows, kv_t, tab, gn, *, use_sel, band, gate_base):
    b, s, _ = q.shape
    t = ATT_TILE
    assert ns0.shape[1] == LANES
    kern = functools.partial(_flash_kernel, t=t, use_sel=use_sel, band=band, gate_base=gate_base)
    tile = lambda w: pl.BlockSpec((None, t, w), lambda i, j: (i, j, 0))
    ns_tile = pl.BlockSpec((None, LANES, t), lambda i, j: (i, 0, j))
    return pl.pallas_call(
        kern,
        grid=(b, s // t),
        in_specs=[
            tile(NSA_WIDTH), ns_tile, ns_tile,
            pl.BlockSpec((None, s, LANES), lambda i, j: (i, 0, 0)),
            pl.BlockSpec((None, LANES, s), lambda i, j: (i, 1, 0)),
            pl.BlockSpec(tab.shape, lambda i, j: (0, 0, 0)),
            tile(LANES),
        ],
        out_specs=tile(NSA_WIDTH),
        out_shape=jax.ShapeDtypeStruct((b, s, NSA_WIDTH), F32),
        compiler_params=_cparams("parallel", "parallel"),
        name="flash_sel" if use_sel else "flash_win",
    )(q, ns0, ns1, k_rows, kv_t, tab, gn)


def _memattn_kernel(q_ref, kv_ref, o_ref):
    lane = _lane_iota((kv_ref.shape[0], LANES))
    for pr in range(MEM_HEADS // 2):
        qpair = q_ref[:, pr * LANES:(pr + 1) * LANES]
        kblk = kv_ref[:, pr * LANES:(pr + 1) * LANES]
        vblk = kv_ref[:, MEM_WIDTH + pr * LANES:MEM_WIDTH + (pr + 1) * LANES]
        out = None
        for half in range(2):
            keep = (lane < HEAD_DIM) if half == 0 else (lane >= HEAD_DIM)
            kk = jnp.where(keep, kblk, 0.0).astype(BF16)
            vv = jnp.where(keep, vblk, 0.0).astype(BF16)
            s = _dot_nt(qpair, kk)
            m = jnp.max(s, axis=1, keepdims=True)
            e = jnp.exp(s - m)
            p = e / jnp.sum(e, axis=1, keepdims=True)
            o = _dot(p.astype(BF16), vv)
            out = o if out is None else out + o
        o_ref[:, pr * LANES:(pr + 1) * LANES] = out


def _mem_attention(qm, mem_kv, tq):
    b, s, _ = qm.shape
    m = mem_kv.shape[1]
    return pl.pallas_call(
        _memattn_kernel,
        grid=(b, s // tq),
        in_specs=[
            pl.BlockSpec((None, tq, MEM_WIDTH), lambda i, j: (i, j, 0)),
            pl.BlockSpec((None, m, 2 * MEM_WIDTH), lambda i, j: (i, 0, 0)),
        ],
        out_specs=pl.BlockSpec((None, tq, MEM_WIDTH), lambda i, j: (i, j, 0)),
        out_shape=jax.ShapeDtypeStruct((b, s, MEM_WIDTH), F32),
        compiler_params=_cparams("parallel", "parallel"),
        name="mem_attention",
    )(qm, mem_kv)


def _dec_kernel(*refs, n_pg, pps, has_new):
    bidx_ref = refs[2]
    page_refs = refs[3:3 + pps]
    new_ref, wq_ref, bias_ref, ns_ref, gate_ref, o_ref, acc_ref, m_ref, l_ref = refs[3 + pps:]
    c = pl.program_id(1)
    n_chunks = pl.num_programs(1)

    @pl.when(c == 0)
    def _():
        acc_ref[...] = jnp.zeros_like(acc_ref)
        m_ref[...] = jnp.full_like(m_ref, NEG)
        l_ref[...] = jnp.zeros_like(l_ref)

    rk = _row_iota((LANES, LANES))

    def step(tiles, first_page):
        feats = (tiles[0] if len(tiles) == 1 else jnp.concatenate(tiles, axis=1)).astype(BF16)
        s = lax.dot_general(feats, wq_ref[...], (((0,), (1,)), ((), ())),
                            preferred_element_type=F32)
        extra = []
        for k in range(len(tiles)):
            pg = first_page + k
            ns = jnp.where(rk < SLC_BLOCK, ns_ref[pl.ds(2 * pg, 1), :], ns_ref[pl.ds(2 * pg + 1, 1), :])
            extra.append(jnp.where(ns > 0.5, NEG, bias_ref[bidx_ref[pg]]))
        s = s + (extra[0] if len(extra) == 1 else jnp.concatenate(extra, axis=0))
        m_old = m_ref[...]
        m_new = jnp.maximum(m_old, jnp.max(s, axis=0, keepdims=True))
        alpha = jnp.exp(m_old - m_new)
        p = jnp.exp(s - m_new)
        l_ref[...] = alpha * l_ref[...] + jnp.sum(p, axis=0, keepdims=True)
        m_ref[...] = m_new
        acc_ref[...] = acc_ref[...] * alpha + _dot(feats, p.astype(BF16))

    if has_new:
        @pl.when(c < n_chunks - 1)
        def _():
            step([r[...] for r in page_refs], c * pps)

        @pl.when(c == n_chunks - 1)
        def _():
            step([r[...] for r in page_refs] + [new_ref[...]], n_pg - pps)
    else:
        step([r[...] for r in page_refs], c * pps)

    @pl.when(c == n_chunks - 1)
    def _():
        o_ref[...] = acc_ref[...] / l_ref[...] * gate_ref[...]


def _decode_attention(pages, phys, lblk, new_rows, wq, bias_tab, bias_idx, notsel, gate):
    bsz, n_pg = phys.shape
    w = pages.shape[1]
    has_new = new_rows is not None
    pps = math.gcd(n_pg, DEC_PAGES_PER_STEP)
    n_steps = n_pg // pps
    n_chunks = n_steps
    if not has_new:
        new_rows = jnp.zeros((1, w, LANES), F32)
    new_map = (lambda i, c, ph, lb, bi: (i, 0, 0)) if has_new else (lambda i, c, ph, lb, bi: (0, 0, 0))

    def page_spec(k):
        def index(i, c, ph, lb, bi):
            return (ph[i, c * pps + k], 0, lb[i, c * pps + k])
        return pl.BlockSpec((None, w, LANES), index)

    per_b = lambda i, c, ph, lb, bi: (i, 0, 0)
    grid_spec = pltpu.PrefetchScalarGridSpec(
        num_scalar_prefetch=3,
        grid=(bsz, n_chunks),
        in_specs=[page_spec(k) for k in range(pps)] + [
            pl.BlockSpec((None, w, LANES), new_map),
            pl.BlockSpec((None, LANES, w), per_b),
            pl.BlockSpec(bias_tab.shape, lambda i, c, ph, lb, bi: (0, 0, 0)),
            pl.BlockSpec((None, notsel.shape[1], LANES), per_b),
            pl.BlockSpec((None, 1, LANES), per_b),
        ],
        out_specs=pl.BlockSpec((None, w, LANES), per_b),
        scratch_shapes=[pltpu.VMEM((w, LANES), F32), pltpu.VMEM((1, LANES), F32), pltpu.VMEM((1, LANES), F32)],
    )
    return pl.pallas_call(
        functools.partial(_dec_kernel, n_pg=n_pg, pps=pps, has_new=has_new),
        grid_spec=grid_spec,
        out_shape=jax.ShapeDtypeStruct((bsz, w, LANES), F32),
        compiler_params=_cparams("parallel", "arbitrary"),
        name="decode_attention",
    )(phys, lblk, bias_idx, *([pages] * pps), new_rows, wq, bias_tab, notsel, gate)


def _feature_major(cache):
    n, rows = cache.shape[:2]
    return cache.transpose(0, 2, 3, 4, 1).reshape(n, -1, rows)


def _tail_kernel(x_ref, g1_ref, op_ref, oc_ref, os_ref, ow_ref, om_ref, cnt0_ref, wgb_ref, wup_p_ref, wup_n_ref,
                 wup_m_ref, wout_ref, g2_ref, rw_ref, rb_ref, x2_ref, h2_ref, ei_ref, gt_ref, cnt_ref):
    @pl.when(pl.program_id(0) == 0)
    def _():
        cnt_ref[...] = cnt0_ref[...]

    x = x_ref[...]
    h = _rms(x, g1_ref[...]).astype(BF16)
    onsa = (oc_ref[...] + os_ref[...] + ow_ref[...]).astype(BF16)
    ups = (_dot(op_ref[...].astype(BF16), wup_p_ref[...]), _dot(onsa, wup_n_ref[...]),
           _dot(om_ref[...].astype(BF16), wup_m_ref[...]))
    mixed = None
    for br in range(3):
        gb = _sigmoid(_dot(h, wgb_ref[:, br * D_MODEL:(br + 1) * D_MODEL]))
        mixed = gb * ups[br] if mixed is None else mixed + gb * ups[br]
    x2 = x + _dot(mixed.astype(BF16), wout_ref[...])
    x2_ref[...] = x2
    h2 = _rms(x2, g2_ref[...])
    h2_ref[...] = _pack_bf16_pairs(h2)
    logits = _dot(h2.astype(BF16), rw_ref[...]) + rb_ref[...]
    lane = _lane_iota(logits.shape)
    lanef = lane.astype(F32)
    tops, idxs = [], []
    for _ in range(TOP_K):
        m = jnp.max(logits, axis=1, keepdims=True)
        idx = jnp.min(jnp.where(logits == m, lanef, 1e9), axis=1, keepdims=True)
        logits = jnp.where(lanef == idx, -jnp.inf, logits)
        tops.append(m)
        idxs.append(idx)
    es = [jnp.exp(tk - tops[0]) for tk in tops]
    den = es[0] + es[1] + es[2] + es[3]
    tm = logits.shape[0]
    onehot = jnp.zeros(logits.shape, F32)
    for k in range(TOP_K):
        onehot = jnp.where(lanef == idxs[k], 1.0, onehot)
    tri = jnp.where(_row_iota((tm, tm)) > _lane_iota((tm, tm)), 1.0, 0.0).astype(BF16)
    before = _dot(tri, onehot.astype(BF16)) + cnt_ref[...]
    cnt_ref[...] = cnt_ref[...] + jnp.sum(onehot, axis=0, keepdims=True)
    ei = jnp.zeros(logits.shape, F32)
    gt = jnp.zeros(logits.shape, F32)
    for k in range(TOP_K):
        rank = jnp.sum(jnp.where(lanef == idxs[k], before, 0.0), axis=1, keepdims=True)
        ei = jnp.where(lane == k, idxs[k], ei)
        ei = jnp.where(lane == TOP_K + k, rank, ei)
        gt = jnp.where(lane == k, es[k] / den, gt)
    ei_ref[...] = ei.astype(jnp.int32)
    gt_ref[...] = gt


def _layer_tail(x, o_pool, o_cmp, o_slc, o_win, o_mem, cnt0, w, tm):
    n = x.shape[0]
    row = lambda wd: pl.BlockSpec((tm, wd), lambda i: (i, 0))
    full = lambda a: pl.BlockSpec(a.shape, lambda i: (0,) * a.ndim)
    weights = (w["wgb"], w["wup_pool"], w["wup_nsa"], w["wup_mem"], w["wout"], w["g2"], w["rw"], w["rb"])
    return pl.pallas_call(
        _tail_kernel,
        grid=(n // tm,),
        in_specs=[row(D_MODEL), full(w["g1"]), row(POOL_WIDTH), row(NSA_WIDTH), row(NSA_WIDTH), row(NSA_WIDTH),
                  row(MEM_WIDTH), full(cnt0)] + [full(a) for a in weights],
        out_specs=[row(D_MODEL), row(D_MODEL // 2), row(LANES), row(LANES), full(cnt0)],
        out_shape=[jax.ShapeDtypeStruct((n, D_MODEL), F32), jax.ShapeDtypeStruct((n, D_MODEL // 2), jnp.uint32),
                   jax.ShapeDtypeStruct((n, LANES), jnp.int32), jax.ShapeDtypeStruct((n, LANES), F32),
                   jax.ShapeDtypeStruct((1, LANES), F32)],
        compiler_params=_cparams("arbitrary"),
        name="layer_tail",
    )(x, w["g1"], o_pool, o_cmp, o_slc, o_win, o_mem, cnt0, *weights)


def _ffn_kernel(be_ref, nu_ref, x_ref, wgu_ref, bgu_ref, wd_ref, bd_ref, o_ref, wgu_bf, wd_bf):
    i = pl.program_id(0)

    @pl.when((i == 0) | (be_ref[i] != be_ref[jnp.maximum(i - 1, 0)]))
    def _():
        wgu_bf[...] = wgu_ref[...].astype(BF16)
        wd_bf[...] = wd_ref[...].astype(BF16)

    @pl.when(i < nu_ref[0])
    def _():
        gu = _dot(_unpack_bf16_pairs(x_ref[...]).astype(BF16), wgu_bf[...]) + bgu_ref[...]
        gate = jnp.minimum(gu[:, :D_FF], SWIGLU_LIMIT)
        up = jnp.clip(gu[:, D_FF:], -SWIGLU_LIMIT, SWIGLU_LIMIT)
        act = gate * _sigmoid(SWIGLU_ALPHA * gate) * (up + 1.0)
        o_ref[...] = _dot(act.astype(BF16), wd_bf[...]) + bd_ref[...]

    @pl.when(i >= nu_ref[0])
    def _():
        o_ref[...] = jnp.zeros_like(o_ref)


def _expert_ffn(rows, blk_e, n_used, wgu, bgu, wd, bd):
    n_rows = rows.shape[0]
    n_blocks = n_rows // MOE_TILE
    blk = lambda i, be, nu: (jnp.minimum(i, nu[0] - 1), 0)
    grid_spec = pltpu.PrefetchScalarGridSpec(
        num_scalar_prefetch=2,
        grid=(n_blocks,),
        in_specs=[
            pl.BlockSpec((MOE_TILE, D_MODEL // 2), blk),
            pl.BlockSpec((None, D_MODEL, 2 * D_FF), lambda i, be, nu: (be[i], 0, 0)),
            pl.BlockSpec((None, 1, 2 * D_FF), lambda i, be, nu: (be[i], 0, 0)),
            pl.BlockSpec((None, D_FF, D_MODEL), lambda i, be, nu: (be[i], 0, 0)),
            pl.BlockSpec((None, 1, D_MODEL), lambda i, be, nu: (be[i], 0, 0)),
        ],
        out_specs=pl.BlockSpec((MOE_TILE, D_MODEL), lambda i, be, nu: (i, 0)),
        scratch_shapes=[pltpu.VMEM((D_MODEL, 2 * D_FF), BF16), pltpu.VMEM((D_FF, D_MODEL), BF16)],
    )
    return pl.pallas_call(
        _ffn_kernel,
        grid_spec=grid_spec,
        out_shape=jax.ShapeDtypeStruct((n_rows, D_MODEL), F32),
        compiler_params=pltpu.CompilerParams(dimension_semantics=("arbitrary",), vmem_limit_bytes=FFN_VMEM_LIMIT),
        name="expert_ffn",
    )(blk_e, n_used, rows, wgu, bgu, wd, bd)


def _token_tile(n, largest=MOE_DMA_TOKENS):
    return next(t for t in (largest, MOE_DMA_TOKENS, 384, 256, 128, n) if n % t == 0)


def _dispatch_scatter(dest_ref, h_ref, rows_ref, sem, td):
    def issue(i, carry):
        t0 = pl.multiple_of(i * 8, 8)
        for r in range(8):
            for k in range(TOP_K):
                d = dest_ref[i * (8 * TOP_K) + r * TOP_K + k]
                pltpu.make_async_copy(h_ref.at[pl.ds(t0 + r, 1)], rows_ref.at[pl.ds(d, 1)], sem).start()
        return carry

    lax.fori_loop(0, td // 8, issue, 0)
    for k in range(TOP_K):
        pltpu.make_async_copy(h_ref, rows_ref.at[pl.ds(0, td)], sem).wait()


def _dispatch_kernel(*refs, tiles, steps, n_blocks):
    ng = len(tiles)
    ends_ref, padded_ref, nu_ref = refs[:3]
    dest_refs = refs[3:3 + ng]
    h_refs = refs[3 + ng:3 + 2 * ng]
    rows_ref, zero_ref, sem, zsem = refs[3 + 2 * ng:]
    i = pl.program_id(0)

    @pl.when(i == 0)
    def _():
        zero_ref[...] = jnp.zeros_like(zero_ref)

        def tail_copy(e):
            start = pl.multiple_of(ends_ref[e] - MOE_TILE, MOE_TILE)
            return pltpu.make_async_copy(zero_ref, rows_ref.at[pl.ds(start, MOE_TILE)], zsem)

        def block_copy(blk):
            start = pl.multiple_of(blk * MOE_TILE, MOE_TILE)
            return pltpu.make_async_copy(zero_ref, rows_ref.at[pl.ds(start, MOE_TILE)], zsem)

        def each(start):
            def expert(e, carry):
                @pl.when(padded_ref[e] > 0)
                def _():
                    tail_copy(e).start() if start else tail_copy(e).wait()
                return carry

            def block(blk, carry):
                block_copy(blk).start() if start else block_copy(blk).wait()
                return carry

            lax.fori_loop(0, N_EXPERTS, expert, 0)
            lax.fori_loop(nu_ref[0], n_blocks, block, 0)

        each(True)
        each(False)

    first = 0
    for g in range(ng):
        @pl.when((i >= first) & (i < first + steps[g]))
        def _(g=g):
            _dispatch_scatter(dest_refs[g], h_refs[g], rows_ref, sem, tiles[g])
        first += steps[g]


def _moe_dispatch(h2s, dests, n_rows, pad_ends, padded, n_used):
    tiles = [_token_tile(h.shape[0], 2 * MOE_DMA_TOKENS) for h in h2s]
    steps = [h.shape[0] // t for h, t in zip(h2s, tiles)]
    firsts = [sum(steps[:g]) for g in range(len(h2s))]

    def local(g):
        return lambda i, *_: jnp.clip(i - firsts[g], 0, steps[g] - 1)

    dest_specs = [pl.BlockSpec((tiles[g] * TOP_K,), lambda i, *_, f=local(g): (f(i),), memory_space=pltpu.SMEM)
                  for g in range(len(h2s))]
    width, dtype = h2s[0].shape[1], h2s[0].dtype
    tok_specs = [pl.BlockSpec((tiles[g], width), lambda i, *_, f=local(g): (f(i), 0)) for g in range(len(h2s))]
    grid_spec = pltpu.PrefetchScalarGridSpec(
        num_scalar_prefetch=3,
        grid=(sum(steps),),
        in_specs=dest_specs + tok_specs,
        out_specs=pl.BlockSpec(memory_space=pl.ANY),
        scratch_shapes=[pltpu.VMEM((MOE_TILE, width), dtype), pltpu.SemaphoreType.DMA(()),
                        pltpu.SemaphoreType.DMA(())],
    )
    return pl.pallas_call(
        functools.partial(_dispatch_kernel, tiles=tiles, steps=steps, n_blocks=n_rows // MOE_TILE),
        grid_spec=grid_spec,
        out_shape=jax.ShapeDtypeStruct((n_rows, width), dtype),
        compiler_params=_cparams("arbitrary"),
        name="moe_dispatch",
    )(pad_ends, padded, n_used, *dests, *h2s)


def _combine_kernel(dest_ref, dnext_ref, x2_ref, g_ref, rows_ref, o_ref, ybuf, sem, *, td):
    i = pl.program_id(0)
    last = pl.num_programs(0) - 1

    def start_tile(d_ref, slot):
        def issue(blk, carry):
            t0 = pl.multiple_of(blk * 8, 8)
            for r in range(8):
                for k in range(TOP_K):
                    d = d_ref[blk * (8 * TOP_K) + r * TOP_K + k]
                    pltpu.make_async_copy(rows_ref.at[pl.ds(d, 1)], ybuf.at[slot, k, pl.ds(t0 + r, 1)],
                                          sem.at[slot]).start()
            return carry

        lax.fori_loop(0, td // 8, issue, 0)

    @pl.when(i == 0)
    def _():
        start_tile(dest_ref, 0)

    @pl.when(i < last)
    def _():
        start_tile(dnext_ref, (i + 1) % 2)

    slot = i % 2
    for k in range(TOP_K):
        pltpu.make_async_copy(rows_ref.at[pl.ds(0, td)], ybuf.at[slot, k], sem.at[slot]).wait()
    out = x2_ref[...]
    for k in range(TOP_K):
        out = out + g_ref[:, k:k + 1] * ybuf[slot, k]
    o_ref[...] = out


def _moe_combine(x2, out_rows, dest, gates):
    n = x2.shape[0]
    td = _token_tile(n)
    steps = n // td
    return pl.pallas_call(
        functools.partial(_combine_kernel, td=td),
        grid=(steps,),
        in_specs=[pl.BlockSpec((td * TOP_K,), lambda i: (i,), memory_space=pltpu.SMEM),
                  pl.BlockSpec((td * TOP_K,), lambda i: (jnp.minimum(i + 1, steps - 1),), memory_space=pltpu.SMEM),
                  pl.BlockSpec((td, D_MODEL), lambda i: (i, 0)),
                  pl.BlockSpec((td, LANES), lambda i: (i, 0)),
                  pl.BlockSpec(memory_space=pl.ANY)],
        out_specs=pl.BlockSpec((td, D_MODEL), lambda i: (i, 0)),
        out_shape=jax.ShapeDtypeStruct((n, D_MODEL), F32),
        scratch_shapes=[pltpu.VMEM((2, TOP_K, td) + out_rows.shape[1:], out_rows.dtype),
                        pltpu.SemaphoreType.DMA((2,))],
        compiler_params=_cparams("arbitrary"),
        name="moe_combine",
    )(dest, dest, x2, gates, out_rows)


def _moe(groups, counts, w):
    n_total = sum(g[0].shape[0] for g in groups)
    cnt = counts[0, :N_EXPERTS].astype(jnp.int32)
    padded = (cnt + MOE_TILE - 1) // MOE_TILE * MOE_TILE
    pad_ends = jnp.cumsum(padded)
    pad_starts = pad_ends - padded
    n_blocks = -(-n_total * TOP_K // MOE_TILE) + N_EXPERTS
    blk_start = jnp.arange(n_blocks, dtype=jnp.int32) * MOE_TILE
    blk_e = jnp.minimum(jnp.sum(blk_start[:, None] >= pad_ends[None, :], axis=1), N_EXPERTS - 1).astype(jnp.int32)
    n_used = (pad_ends[-1] // MOE_TILE).astype(jnp.int32).reshape(1)
    dests = [(pad_starts[er[:, :TOP_K]] + er[:, TOP_K:2 * TOP_K]).astype(jnp.int32).reshape(-1)
             for (_, _, er, _) in groups]
    rows = _moe_dispatch([g[1] for g in groups], dests, n_blocks * MOE_TILE, pad_ends.astype(jnp.int32),
                         padded.astype(jnp.int32), n_used)
    out_rows = _expert_ffn(rows, blk_e, n_used, w["wgu"], w["bgu"], w["wd"], w["bd"])
    return [_moe_combine(x2, out_rows, dest, gates) for (x2, _, _, gates), dest in zip(groups, dests)]


def _rel_bucket(dist):
    n = jnp.maximum(dist, 0)
    max_exact = NUM_BUCKETS // 2
    nf = jnp.maximum(n, 1).astype(F32)
    large = max_exact + (jnp.log(nf / max_exact) / math.log(MAX_DISTANCE / max_exact)
                         * (NUM_BUCKETS - max_exact)).astype(jnp.int32)
    large = jnp.minimum(large, NUM_BUCKETS - 1)
    return jnp.where(n < max_exact, n, large)


def _bias_of(rel_bias, dist, valid):
    return jnp.where(valid[..., None], rel_bias[_rel_bucket(dist)], NEG)


def _proj_segs(kv_forms, kvc_forms):
    return ((0, 512, "qscale", ("rows",), BF16), (512, 256, "qscale", ("rows",), BF16),
            (768, 256, "id", kv_forms, F32), (1024, 256, "id", kv_forms, F32), (1280, 256, "id", kvc_forms, F32),
            (1536, 256, "id", ("rows",), F32), (1792, 128, "sigmoid", ("rows",), F32))


_PROJ_SEGS_PROMPT = _proj_segs(("key_rows", "t"), ("t",))
_PROJ_SEGS_SAMPLE = _proj_segs(("rows",), ("rows",))
_PROJ_NNORM = 1280


def _prep_layer(l, rel_bias, norm1_g, w_in, nsa_qk_norm, mem_qk_norm, cmp_w, cmp_pe, pool_w, pool_scale,
                mem_norm_g, w_mem_kv, w_up_pool, w_up_nsa, w_up_mem, w_out, norm2_g, router_w, router_b,
                w_gu, b_gu, w_down, b_down):
    wi = w_in[l]
    o_u, o_q, o_qm, o_kvc, o_kvs, o_kvw, o_gn, o_gb = 0, 256, 768, 1024, 1280, 1536, 1792, 1816
    w_proj = jnp.concatenate([
        wi[:, o_q:o_q + 512], wi[:, o_qm:o_qm + 256], wi[:, o_kvs:o_kvs + 256], wi[:, o_kvw:o_kvw + 256],
        wi[:, o_kvc:o_kvc + 256], wi[:, o_u:o_u + 256], wi[:, o_gn:o_gn + 24],
        jnp.zeros((D_MODEL, LANES - 24), F32)], axis=1).astype(BF16)
    nq, mq = nsa_qk_norm[l], mem_qk_norm[l]
    ones = jnp.ones((LANES,), F32)
    gain = jnp.concatenate([jnp.tile(nq[0], 8), jnp.tile(mq[0], 4), jnp.tile(nq[2], 2), ones,
                            jnp.tile(nq[3], 2), ones])[None, :]
    nmask = jnp.concatenate([jnp.ones((768,), F32), ones, 0 * ones, ones, 0 * ones])[None, :]
    eye4 = jnp.eye(4, dtype=F32)
    cw = cmp_w[l].reshape(2, 2, CMP_STRIDE, HEAD_DIM, HEAD_DIM)
    w_c = jnp.einsum("crjde,xy->cjxdrye", cw, jnp.eye(2, dtype=F32))
    w_c = w_c.reshape(2, CMP_STRIDE, LANES, KV_WIDTH).astype(BF16)
    pe = cmp_pe[l].reshape(2, 2, CMP_STRIDE, HEAD_DIM)
    pe_c = jnp.tile(pe.transpose(0, 2, 1, 3), (1, 1, 1, NSA_KV_HEADS))
    pe_c = jnp.pad(pe_c, ((0, 0), (0, 0), (0, 6), (0, 0)))
    w_pool = jnp.einsum("gde,gh->gdhe", pool_w[l], eye4).reshape(POOL_WIDTH, POOL_WIDTH).astype(BF16)
    rw = jnp.pad(router_w[l], ((0, 0), (0, LANES - N_EXPERTS))).astype(BF16)
    rb = jnp.concatenate([router_b[l], jnp.full((LANES - N_EXPERTS,), NEG, F32)])[None, :]
    return {
        "g1": norm1_g[l][None, :], "w_proj": w_proj, "gain": gain, "nmask": nmask,
        "w_kvc_t": wi[:, o_kvc:o_kvc + 256].T.astype(BF16),
        "gk_cmp": jnp.tile(nq[1], 2)[None, :], "w_c": w_c, "pe_c": pe_c,
        "w_pool": w_pool, "pool_scale": pool_scale[l][None, :],
        "mem_g": mem_norm_g[l][None, :], "w_mem": w_mem_kv[l].astype(BF16),
        "mem_gain": jnp.concatenate([jnp.tile(mq[1], 4), jnp.ones((256,), F32)])[None, :],
        "mem_nmask": jnp.concatenate([jnp.ones((256,), F32), jnp.zeros((256,), F32)])[None, :],
        "wgb": wi[:, o_gb:o_gb + 3 * D_MODEL].astype(BF16),
        "wup_pool": w_up_pool[l].astype(BF16), "wup_nsa": w_up_nsa[l].astype(BF16),
        "wup_mem": w_up_mem[l].astype(BF16), "wout": w_out[l].astype(BF16), "g2": norm2_g[l][None, :],
        "rw": rw, "rb": rb,
        "wgu": w_gu[l], "bgu": b_gu[l][:, None, :], "wd": w_down[l],
        "bd": b_down[l][:, None, :],
    }


def _project_in(x2d, w, segs, tm, seq):
    wt = w["w_kvc_t"] if segs is _PROJ_SEGS_PROMPT else None
    return _project(x2d, w["g1"], w["w_proj"], w["gain"], w["nmask"], segs, _PROJ_NNORM, tm, seq, wt)


def _rows_view(a_t, heads):
    b, _, rows = a_t.shape
    return a_t.reshape(b, 2, heads, HEAD_DIM, rows).transpose(0, 4, 1, 2, 3)


def _toeplitz(v, t):
    lead = v.shape[:-1]
    flat = jnp.tile(v, (1,) * len(lead) + (t,))[..., t:t + t * (2 * t - 1)]
    return flat.reshape(lead + (t, 2 * t - 1))[..., :t]


def _flash_tables(rel_bias):
    t = ATT_TILE
    d0 = jnp.arange(-t, t)
    kinds = jnp.stack([
        _bias_of(rel_bias, d0, d0 >= 0),
        _bias_of(rel_bias, d0 + t, d0 + t >= 0),
        _bias_of(rel_bias, jnp.full((2 * t,), 2 * t), jnp.ones((2 * t,), bool)),
        _bias_of(rel_bias, d0 + WINDOW, d0 + WINDOW < WINDOW),
    ])
    tab = _toeplitz(kinds.transpose(2, 0, 1), t)
    return tab.transpose(1, 2, 0, 3).reshape(4, t, NSA_HEADS * t)


def _cmp_bias_table(rel_bias, s, n_sub, n_cmp, pos0, tq):
    na = s // CMP_STRIDE
    m = max(na, n_sub)
    k = jnp.arange(-m, m)[None, :]
    r = jnp.arange(CMP_STRIDE)[:, None]
    d = CMP_STRIDE * k + r - (CMP_BLOCK - 1) + pos0
    v = _bias_of(rel_bias, d, d >= 0).transpose(2, 0, 1)
    tz = _toeplitz(v, m)[:, :, :n_sub, :na]
    tz = tz.reshape(NSA_KV_HEADS, NSA_HPG, CMP_STRIDE, n_sub, s // tq, tq // CMP_STRIDE)
    tab = tz.transpose(0, 3, 4, 1, 5, 2).reshape(NSA_KV_HEADS, n_sub, NSA_HPG * s)
    return jnp.where(jnp.arange(n_sub)[None, :, None] < n_cmp, tab, NEG)


def _prompt_pre(x, mem, w, rel_bias, cnt0):
    b, s, _ = x.shape
    n = b * s
    tm = 512 if n % 512 == 0 else ATT_TILE
    x2d = x.reshape(n, D_MODEL)
    q, qm, ks, kvs_t, kw, kvw_t, kvc_t, u, gn = _project_in(x2d, w, _PROJ_SEGS_PROMPT, tm, s)
    r3 = lambda a: a.reshape(b, s, a.shape[-1])
    q, qm, ks, kw, u, gn = map(r3, (q, qm, ks, kw, u, gn))

    o_pool = _pool_mix(u, jnp.zeros((b, 16, POOL_WIDTH), F32), w["w_pool"], w["pool_scale"], 0)

    n_cmp = (s - CMP_BLOCK) // CMP_STRIDE + 1
    n_slc = -(-s // SLC_BLOCK)
    n_lb = s // LANES
    own = jnp.broadcast_to(jnp.arange(b, dtype=jnp.int32)[:, None], (b, n_lb))
    blocks = jnp.broadcast_to(jnp.arange(n_lb, dtype=jnp.int32)[None, :], (b, n_lb))
    parts = _cmp_partials_paged(kvc_t, own, blocks, w["pe_c"], w["w_c"], math.gcd(n_lb, 16))
    n_sub = parts.shape[1]
    tq = math.gcd(s, 512)
    bias_c = _cmp_bias_table(rel_bias, s, n_sub, n_cmp, 0, tq)
    o_cmp, ns0, ns1 = _cmp_attention(q, parts, bias_c, gn, w["gk_cmp"], tq=tq, n_cmp=n_cmp, n_slc=n_slc, pos0=0)

    tab = _flash_tables(rel_bias)
    o_slc = _flash_attention(q, ns0, ns1, ks, kvs_t, tab, gn, use_sel=True, band=None, gate_base=NSA_HEADS)
    o_win = _flash_attention(q, ns0, ns1, kw, kvw_t, tab, gn, use_sel=False, band=WINDOW // ATT_TILE,
                             gate_base=2 * NSA_HEADS)

    m = mem.shape[1]
    mem_kv, mem_kv_t = _project(mem.reshape(b * m, D_MODEL), w["mem_g"], w["w_mem"], w["mem_gain"],
                                w["mem_nmask"], ((0, 2 * MEM_WIDTH, "id", ("rows", "t"), F32),), MXU_DIM,
                                tm=math.gcd(m, 512), seq=m)
    o_mem = _mem_attention(qm, mem_kv.reshape(b, m, 2 * MEM_WIDTH), tq=min(512, s))

    f2 = lambda a: a.reshape(n, a.shape[-1])
    x2, h2, eidx, gates, cnt = _layer_tail(x2d, f2(o_pool), f2(o_cmp), f2(o_slc), f2(o_win), f2(o_mem), cnt0, w, tm)
    win_t = kvw_t[:, :, max(0, s - WINDOW):]
    if s < WINDOW:
        win_t = jnp.pad(win_t, ((0, 0), (0, 0), (WINDOW - s, 0)))
    states = (_rows_view(kvc_t, NSA_KV_HEADS), _rows_view(kvs_t, NSA_KV_HEADS), _rows_view(win_t, NSA_KV_HEADS),
              _rows_view(mem_kv_t, MEM_HEADS), _last_rows(u, POOL_BUF))
    return (x2, h2, eidx, gates), states, cnt


def _last_rows(a, n):
    t = a.shape[1]
    if t < n:
        a = jnp.pad(a, [(0, 0), (n - t, 0)] + [(0, 0)] * (a.ndim - 2))
    return a[:, a.shape[1] - n:]


def _dec_columns_nsa(q):
    b, t, _ = q.shape
    qh = q.reshape(b, t, NSA_KV_HEADS, NSA_HPG, HEAD_DIM)
    w = jnp.einsum("btgpd,gx->bxdgtp", qh.astype(F32), jnp.eye(NSA_KV_HEADS, dtype=F32))
    w = w.reshape(b, NSA_KV_HEADS * HEAD_DIM, NSA_KV_HEADS * t * NSA_HPG)
    return jnp.pad(w, ((0, 0), (0, KV_WIDTH - w.shape[1]), (0, LANES - w.shape[2]))).astype(BF16)


def _dec_extract_nsa(o, t):
    b = o.shape[0]
    v = o[:, LANES:, :NSA_KV_HEADS * t * NSA_HPG]
    v = v.reshape(b, NSA_KV_HEADS, HEAD_DIM, NSA_KV_HEADS, t, NSA_HPG)
    v = jnp.einsum("bxdgtp,gx->btgpd", v, jnp.eye(NSA_KV_HEADS, dtype=F32))
    return v.reshape(b, t, NSA_WIDTH)


def _dec_bias_cols(bias_tph):
    k, t, _ = bias_tph.shape
    bt = bias_tph.reshape(k, t, NSA_KV_HEADS, NSA_HPG).transpose(0, 2, 1, 3).reshape(k, NSA_KV_HEADS * t * NSA_HPG)
    return jnp.pad(bt, ((0, 0), (0, LANES - bt.shape[1])))


def _dec_gate_cols(gn, base, t):
    b = gn.shape[0]
    gt = gn[:, :, base:base + NSA_HEADS].reshape(b, t, NSA_KV_HEADS, NSA_HPG).transpose(0, 2, 1, 3)
    gt = gt.reshape(b, 1, NSA_KV_HEADS * t * NSA_HPG)
    return jnp.pad(gt, ((0, 0), (0, 0), (0, LANES - gt.shape[2])), constant_values=1.0)


def _sample_pre(x, cache_cmp, cache_slc, cache_win, cache_mem, pool_buf, page_table, w, rel_bias, cnt0):
    b, t, _ = x.shape
    n = b * t
    page = cache_cmp.shape[1]
    n_pages = page_table.shape[1]
    past = n_pages * page
    x2d = x.reshape(n, D_MODEL)
    tm = n if n <= 512 else LANES
    q, qm, kvs, kvw, kvc, u, gn = _project_in(x2d, w, _PROJ_SEGS_SAMPLE, tm, tm)
    r3 = lambda a: a.reshape(b, t, a.shape[-1])
    q, qm, kvs, kvw, kvc, u, gn = map(r3, (q, qm, kvs, kvw, kvc, u, gn))
    qpos = past + jnp.arange(t)

    buf16 = jnp.pad(pool_buf, ((0, 0), (16 - POOL_BUF, 0), (0, 0)))
    o_pool = _pool_mix(u, buf16, w["w_pool"], w["pool_scale"], past)

    total = past + t
    n_cmp = (total - CMP_BLOCK) // CMP_STRIDE + 1
    n_sub_used = n_cmp + CMP_BLOCK // CMP_STRIDE - 1
    n_slc = -(-total // SLC_BLOCK)
    pps = math.gcd(n_pages, 16)
    parts = _cmp_partials_paged(_feature_major(cache_cmp), page_table, jnp.zeros_like(page_table), w["pe_c"],
                                w["w_c"], pps)
    extra = n_sub_used * CMP_STRIDE - past
    if extra > 0:
        tail_rows = -(-extra // CMP_STRIDE) * CMP_STRIDE
        new_c = jnp.pad(kvc, ((0, 0), (0, max(0, tail_rows - t)), (0, 0)))[:, :tail_rows]
        parts = jnp.concatenate([parts, _cmp_partials_dense(new_c, w["pe_c"], w["w_c"])], axis=1)
    n_sub = parts.shape[1]
    end = jnp.arange(n_sub)[None, :] * CMP_STRIDE + CMP_BLOCK - 1
    bias_c = _bias_of(rel_bias, qpos[:, None] - end, (end <= qpos[:, None]) & (jnp.arange(n_sub)[None, :] < n_cmp))
    qpad = ((0, 0), (0, LANES - t), (0, 0))
    bias_c = _cmp_bias_cols(jnp.pad(bias_c.transpose(2, 0, 1), qpad), LANES)
    o_cmp, ns0, ns1 = _cmp_attention(jnp.pad(q, qpad), parts, bias_c, jnp.pad(gn, qpad), w["gk_cmp"], tq=LANES,
                                     n_cmp=n_cmp, n_slc=n_slc, pos0=past)
    o_cmp = o_cmp[:, :t]

    wq = _dec_columns_nsa(q).transpose(0, 2, 1)
    ncol = NSA_KV_HEADS * t * NSA_HPG
    new_tile = lambda kv: jnp.pad(kv, ((0, 0), (0, LANES - t), (0, 0))).transpose(0, 2, 1)
    own = lambda npg: jnp.broadcast_to(jnp.arange(b, dtype=jnp.int32)[:, None], (b, npg))
    blocks = lambda npg: jnp.broadcast_to(jnp.arange(npg, dtype=jnp.int32)[None, :], (b, npg))

    n_chunks = n_pages + 1
    nblk = -(-2 * n_chunks // 8) * 8
    ns = jnp.stack([ns0, ns1], axis=1)[:, :, :, :t].transpose(0, 1, 3, 2)
    ns = jnp.pad(ns, ((0, 0), (0, 0), (0, 0), (0, max(0, nblk - ns.shape[3]))))[..., :nblk]
    ns = jnp.broadcast_to(ns[:, :, :, None, :], (b, NSA_KV_HEADS, t, NSA_HPG, nblk)).reshape(b, ncol, nblk)
    notsel = jnp.pad(ns.transpose(0, 2, 1), ((0, 0), (0, 0), (0, LANES - ncol)))
    rows = jnp.arange(LANES)
    far = _bias_of(rel_bias, jnp.full((LANES, t), 2 * MAX_DISTANCE), jnp.ones((LANES, t), bool))
    kpos_last = past - LANES + rows
    d_last = qpos[None, :] - kpos_last[:, None]
    near = _bias_of(rel_bias, d_last, d_last >= 0)
    kpos_new = past + rows
    d_new = qpos[None, :] - kpos_new[:, None]
    newb = _bias_of(rel_bias, d_new, (d_new >= 0) & (rows[:, None] < t))
    bias_tab = jnp.stack([_dec_bias_cols(far), _dec_bias_cols(near), _dec_bias_cols(newb)])
    bias_idx = jnp.concatenate([jnp.zeros((n_pages - 1,), jnp.int32), jnp.array([1, 2], jnp.int32)])
    o_slc = _decode_attention(_feature_major(cache_slc), page_table, jnp.zeros_like(page_table), new_tile(kvs), wq,
                              bias_tab, bias_idx, notsel, _dec_gate_cols(gn, NSA_HEADS, t))
    o_slc = _dec_extract_nsa(o_slc, t)

    wb = cache_win.shape[1]
    n_wpg = wb // LANES
    kpos_w = past - wb + jnp.arange(wb + LANES)
    d_w = qpos[None, :] - kpos_w[:, None]
    valid_w = (d_w >= 0) & (d_w < WINDOW) & (kpos_w[:, None] >= 0) & (jnp.arange(wb + LANES)[:, None] < wb + t)
    bias_w = _dec_bias_cols(_bias_of(rel_bias, d_w, valid_w)).reshape(n_wpg + 1, LANES, LANES)
    zeros_ns = jnp.zeros((b, -(-2 * (n_wpg + 1) // 8) * 8, LANES), F32)
    o_win = _decode_attention(_feature_major(cache_win), own(n_wpg), blocks(n_wpg), new_tile(kvw), wq, bias_w,
                              jnp.arange(n_wpg + 1, dtype=jnp.int32), zeros_ns, _dec_gate_cols(gn, 2 * NSA_HEADS, t))
    o_win = _dec_extract_nsa(o_win, t)

    m = cache_mem.shape[1]
    n_mpg = m // LANES
    qmh = qm.reshape(b, t, MEM_HEADS, HEAD_DIM).astype(F32)
    wqm = jnp.einsum("bthd,hx->bxdht", qmh, jnp.eye(MEM_HEADS, dtype=F32))
    wqm = wqm.reshape(b, MEM_WIDTH, MEM_HEADS * t)
    wqm = jnp.pad(wqm, ((0, 0), (0, MEM_WIDTH), (0, LANES - MEM_HEADS * t))).astype(BF16).transpose(0, 2, 1)
    o_mem = _decode_attention(_feature_major(cache_mem), own(n_mpg), blocks(n_mpg), None, wqm,
                              jnp.zeros((1, LANES, LANES), F32), jnp.zeros((n_mpg,), jnp.int32),
                              jnp.zeros((b, 8, LANES), F32), jnp.ones((b, 1, LANES), F32))
    om = o_mem[:, MEM_WIDTH:, :MEM_HEADS * t].reshape(b, MEM_HEADS, HEAD_DIM, MEM_HEADS, t)
    o_mem = jnp.einsum("bxdht,hx->bthd", om, jnp.eye(MEM_HEADS, dtype=F32)).reshape(b, t, MEM_WIDTH)

    f2 = lambda a: a.reshape(n, a.shape[-1])
    x2, h2, eidx, gates, cnt = _layer_tail(x2d, f2(o_pool), f2(o_cmp), f2(o_slc), f2(o_win), f2(o_mem), cnt0, w,
                                           n if n <= 512 else LANES)
    kvshape = (b, t, 2, NSA_KV_HEADS, HEAD_DIM)
    new_win = _rows_view(jnp.concatenate([_feature_major(cache_win)[:, :, t:], kvw.transpose(0, 2, 1)], axis=2),
                         NSA_KV_HEADS)
    new_pool = jnp.concatenate([pool_buf, u], axis=1)[:, t:]
    states = (kvc.reshape(kvshape), kvs.reshape(kvshape), new_win, new_pool)
    return (x2, h2, eidx, gates), states, cnt


def kernel(x_prompt, x_sample, cache_cmp_kv, cache_slc_kv, cache_win_kv, cache_mem_kv, state_pool, page_table,
           mem_prompt, rel_bias, norm1_g, w_in, nsa_qk_norm, mem_qk_norm, cmp_w, cmp_pe, pool_w, pool_scale,
           mem_norm_g, w_mem_kv, w_up_pool, w_up_nsa, w_up_mem, w_out, norm2_g, router_w, router_b, w_gu, b_gu,
           w_down, b_down):
    depth = w_in.shape[0]
    yp, ys = x_prompt, x_sample
    bp, sp, _ = x_prompt.shape
    bs, ts, _ = x_sample.shape
    outs_p = [[] for _ in range(5)]
    outs_s = [[] for _ in range(4)]
    for l in range(depth):
        w = _prep_layer(l, rel_bias, norm1_g, w_in, nsa_qk_norm, mem_qk_norm, cmp_w, cmp_pe, pool_w, pool_scale,
                        mem_norm_g, w_mem_kv, w_up_pool, w_up_nsa, w_up_mem, w_out, norm2_g, router_w, router_b,
                        w_gu, b_gu, w_down, b_down)
        pre_p, st_p, cnt = _prompt_pre(yp, mem_prompt, w, rel_bias, jnp.zeros((1, LANES), F32))
        pre_s, st_s, cnt = _sample_pre(ys, cache_cmp_kv[l], cache_slc_kv[l], cache_win_kv[l], cache_mem_kv[l],
                                       state_pool[l], page_table, w, rel_bias, cnt)
        yp, ys = _moe([pre_p, pre_s], cnt, w)
        yp = yp.reshape(bp, sp, D_MODEL)
        ys = ys.reshape(bs, ts, D_MODEL)
        for lst, a in zip(outs_p, st_p):
            lst.append(a)
        for lst, a in zip(outs_s, st_s):
            lst.append(a)
    new_cmp_p, new_slc_p, new_win_p, new_mem_p, new_pool_p = [jnp.stack(a) for a in outs_p]
    new_cmp_s, new_slc_s, new_win_s, new_pool_s = [jnp.stack(a) for a in outs_s]
    new_win_s = new_win_s.reshape(new_win_s.shape[:3] + (2, NSA_KV_HEADS, HEAD_DIM))
    return (yp, ys, new_cmp_p, new_slc_p, new_win_p, new_mem_p, new_pool_p,
            new_cmp_s, new_slc_s, new_win_s, new_pool_s)
```

```python
import functools
import math

import jax
import jax.numpy as jnp
from jax import lax
from jax.experimental import pallas as pl
from jax.experimental.pallas import tpu as pltpu

F32 = jnp.float32
BF16 = jnp.bfloat16

D_MODEL = 1024
HEAD_DIM = 64
POOL_WINDOWS = (2, 4, 8, 16)
POOL_GROUP = 64
POOL_WIDTH = 256
POOL_BUF = 15
NSA_HEADS = 8
NSA_KV_HEADS = 2
NSA_HPG = 4
NSA_WIDTH = 512
KV_WIDTH = 256
CMP_BLOCK = 32
CMP_STRIDE = 16
SLC_BLOCK = 64
SLC_TOPK = 16
WINDOW = 512
MEM_HEADS = 4
MEM_WIDTH = 256
NUM_BUCKETS = 32
MAX_DISTANCE = 128
N_EXPERTS = 32
TOP_K = 4
D_FF = 1024
SWIGLU_ALPHA = 1.702
SWIGLU_LIMIT = 7.0
EPS = 1e-6
SCALE = HEAD_DIM ** -0.5

LANES = 128
MXU_DIM = 256
NEG = -1e30
ATT_TILE = 256
DEC_PAGES_PER_STEP = 32
MOE_TILE = 512
MOE_DMA_TOKENS = 512
VMEM_LIMIT = 48 * 1024 * 1024
FFN_VMEM_LIMIT = 56 * 1024 * 1024


def _cparams(*sem):
    return pltpu.CompilerParams(dimension_semantics=sem, vmem_limit_bytes=VMEM_LIMIT)


def _dot(a, b):
    return jnp.dot(a, b, preferred_element_type=F32)


def _dot_nt(a, b):
    return lax.dot_general(a, b, (((1,), (1,)), ((), ())), preferred_element_type=F32)


def _split_dot(a, b):
    hi = a.astype(BF16)
    lo = (a - hi.astype(F32)).astype(BF16)
    return _dot(hi, b) + _dot(lo, b)


def _rms(x, g):
    r = lax.rsqrt(jnp.mean(x * x, axis=-1, keepdims=True) + EPS)
    return (x * r) * g


def _sigmoid(x):
    return 1.0 / (1.0 + jnp.exp(-x))


def _pack_bf16_pairs(x):
    w = x.shape[1] // 2
    bits = lax.bitcast_convert_type(x.astype(BF16).astype(F32), jnp.uint32)
    return jnp.right_shift(bits[:, :w], jnp.uint32(16)) | (bits[:, w:] & jnp.uint32(0xFFFF0000))


def _unpack_bf16_pairs(p):
    lo = lax.bitcast_convert_type(jnp.left_shift(p, jnp.uint32(16)), F32)
    hi = lax.bitcast_convert_type(p & jnp.uint32(0xFFFF0000), F32)
    return jnp.concatenate([lo, hi], axis=1)


def _lane_iota(shape):
    return lax.broadcasted_iota(jnp.int32, shape, len(shape) - 1)


def _row_iota(shape):
    return lax.broadcasted_iota(jnp.int32, shape, len(shape) - 2)


def _proj_kernel(x_ref, g_ref, w_ref, gain_ref, nmask_ref, seg_ref, *rest, segs, n_norm, has_wt):
    wt_ref = rest[0] if has_wt else None
    out_refs = rest[1:] if has_wt else rest
    h = _rms(x_ref[...], g_ref[...]).astype(BF16)
    seg = seg_ref[...]
    outs = iter(out_refs)
    for (start, width, kind, forms, _) in segs:
        if forms == ("t",) and has_wt:
            next(outs)[...] = _dot_nt(wt_ref[...], h)
            continue
        z = _dot(h, w_ref[:, start:start + width])
        if start < n_norm:
            pieces = []
            for c in range(0, width, MXU_DIM):
                zc = z[:, c:c + MXU_DIM]
                ms = _split_dot(zc * zc, seg)
                zn = (zc * lax.rsqrt(ms + EPS)) * gain_ref[:, start + c:start + c + MXU_DIM]
                pieces.append(jnp.where(nmask_ref[:, start + c:start + c + MXU_DIM] > 0, zn, zc))
            z = pieces[0] if len(pieces) == 1 else jnp.concatenate(pieces, axis=1)
        if kind == "sigmoid":
            z = _sigmoid(z)
        elif kind == "qscale":
            z = z * SCALE
        for form in forms:
            o_ref = next(outs)
            if form == "rows":
                o_ref[...] = z.astype(o_ref.dtype)
            elif form == "key_rows":
                o_ref[...] = z[:, 0:LANES]
            else:
                o_ref[...] = z.T


def _project(x, g, w, gain, nmask, segs, n_norm, tm, seq, wt=None):
    n = x.shape[0]
    ncol = w.shape[1]
    seg = _seg_matrix(MXU_DIM)
    full = lambda i: (0, 0)
    tpb = seq // tm
    out_specs, out_shape = [], []
    for (_, wd, _, forms, dt) in segs:
        for form in forms:
            if form == "rows":
                out_specs.append(pl.BlockSpec((tm, wd), lambda i: (i, 0)))
                out_shape.append(jax.ShapeDtypeStruct((n, wd), dt))
            elif form == "key_rows":
                out_specs.append(pl.BlockSpec((tm, LANES), lambda i: (i, 0)))
                out_shape.append(jax.ShapeDtypeStruct((n, LANES), F32))
            else:
                out_specs.append(pl.BlockSpec((None, wd, tm), lambda i: (i // tpb, 0, i % tpb)))
                out_shape.append(jax.ShapeDtypeStruct((n // seq, wd, seq), F32))
    extra = () if wt is None else (wt,)
    return pl.pallas_call(
        functools.partial(_proj_kernel, segs=segs, n_norm=n_norm, has_wt=wt is not None),
        grid=(n // tm,),
        in_specs=[
            pl.BlockSpec((tm, D_MODEL), lambda i: (i, 0)),
            pl.BlockSpec((1, D_MODEL), full),
            pl.BlockSpec((D_MODEL, ncol), full),
            pl.BlockSpec((1, gain.shape[1]), full),
            pl.BlockSpec((1, nmask.shape[1]), full),
            pl.BlockSpec((MXU_DIM, MXU_DIM), full),
        ] + [pl.BlockSpec(a.shape, full) for a in extra],
        out_specs=out_specs,
        out_shape=out_shape,
        compiler_params=_cparams("parallel"),
        name="proj",
    )(x, g, w, gain, nmask, seg, *extra)


def _seg_matrix(n):
    i = jnp.arange(n) // HEAD_DIM
    return jnp.where(i[:, None] == i[None, :], 1.0 / HEAD_DIM, 0.0).astype(BF16)


def _pool_kernel(u_ref, buf_ref, w_ref, scale_ref, o_ref, zs_ref, *, t, pos0):
    zs_ref[0:16, :] = buf_ref[...]
    zs_ref[16:16 + t, :] = u_ref[...]
    u = u_ref[...]
    lane = _lane_iota((1, POOL_WIDTH))
    pos = (pos0 + _row_iota((t, 1))).astype(F32)
    acc = u
    mean = None
    for i in range(1, max(POOL_WINDOWS)):
        acc = acc + zs_ref[16 - i:16 - i + t, :]
        if i + 1 in POOL_WINDOWS:
            gi = POOL_WINDOWS.index(i + 1)
            m = acc / jnp.minimum(pos + 1.0, float(i + 1))
            mean = m if mean is None else jnp.where(lane >= gi * POOL_GROUP, m, mean)
    d = (mean - u).astype(BF16)
    o_ref[...] = _dot(d, w_ref[...]) * scale_ref[...]


def _pool_mix(u, buf16, w_bd, scale, pos0):
    b, t, _ = u.shape
    return pl.pallas_call(
        functools.partial(_pool_kernel, t=t, pos0=pos0),
        grid=(b,),
        in_specs=[
            pl.BlockSpec((None, t, POOL_WIDTH), lambda i: (i, 0, 0)),
            pl.BlockSpec((None, 16, POOL_WIDTH), lambda i: (i, 0, 0)),
            pl.BlockSpec((POOL_WIDTH, POOL_WIDTH), lambda i: (0, 0)),
            pl.BlockSpec((1, POOL_WIDTH), lambda i: (0, 0)),
        ],
        out_specs=pl.BlockSpec((None, t, POOL_WIDTH), lambda i: (i, 0, 0)),
        out_shape=jax.ShapeDtypeStruct((b, t, POOL_WIDTH), F32),
        scratch_shapes=[pltpu.VMEM((t + 16, POOL_WIDTH), F32)],
        compiler_params=_cparams("parallel"),
        name="pool",
    )(u, buf16, w_bd, scale)


def _cpart_compute(rows_of, pe_ref, w_ref, o_ref, m):
    for c in range(2):
        acc = jnp.zeros((m + 8, KV_WIDTH), F32)
        for j in range(CMP_STRIDE):
            lhs = jnp.concatenate([rows_of(c, j), pe_ref[c, j]], axis=0)
            acc = acc + _dot(lhs.astype(BF16), w_ref[c, j])
        lane = _lane_iota((1, KV_WIDTH))
        pe_term = jnp.where(lane < LANES, acc[m:m + 1], acc[m + 1:m + 2])
        o_ref[:, c * KV_WIDTH:(c + 1) * KV_WIDTH] = acc[0:m] + pe_term


def _cpart_kernel(xk_ref, xv_ref, pe_ref, w_ref, o_ref, *, rows):
    m = rows // CMP_STRIDE
    x_refs = (xk_ref, xv_ref)
    _cpart_compute(lambda c, j: x_refs[c][pl.ds(j, m, stride=CMP_STRIDE), :], pe_ref, w_ref, o_ref, m)


def _cpart_paged_kernel(*refs, nop, page):
    page_refs = refs[2:2 + nop]
    perm_ref, pe_ref, w_ref, o_ref, xs_ref = refs[2 + nop:]
    n = page // CMP_STRIDE
    for k, r in enumerate(page_refs):
        y = _dot_nt(perm_ref[...], r[...].astype(BF16))
        for j in range(CMP_STRIDE):
            for c in range(2):
                xs_ref[c, j, k * n:(k + 1) * n, :] = y[j * n:(j + 1) * n, c * LANES:(c + 1) * LANES]
    _cpart_compute(lambda c, j: xs_ref[c, j], pe_ref, w_ref, o_ref, nop * n)


def _cmp_partials_dense(kv, pe, w_c):
    b, t, _ = kv.shape
    rows = (t // CMP_STRIDE) * CMP_STRIDE
    n = rows // CMP_STRIDE
    return pl.pallas_call(
        functools.partial(_cpart_kernel, rows=rows),
        grid=(b,),
        in_specs=[
            pl.BlockSpec((None, rows, LANES), lambda i: (i, 0, 0)),
            pl.BlockSpec((None, rows, LANES), lambda i: (i, 0, 1)),
            pl.BlockSpec(pe.shape, lambda i: (0, 0, 0, 0)),
            pl.BlockSpec(w_c.shape, lambda i: (0, 0, 0, 0)),
        ],
        out_specs=pl.BlockSpec((None, n, 2 * KV_WIDTH), lambda i: (i, 0, 0)),
        out_shape=jax.ShapeDtypeStruct((b, n, 2 * KV_WIDTH), F32),
        compiler_params=_cparams("parallel"),
        name="cmp_partials",
    )(kv, kv, pe, w_c)


def _cmp_partials_paged(pool_t, phys, lblk, pe, w_c, pages_per_step):
    b, n_pages = phys.shape
    page = LANES
    nop = pages_per_step
    n = nop * page // CMP_STRIDE

    def page_spec(k):
        return pl.BlockSpec((None, KV_WIDTH, page),
                            lambda i, c, ph, lb: (ph[i, c * nop + k], 0, lb[i, c * nop + k]))

    r_idx = jnp.arange(page)
    regroup = (r_idx[None, :] == ((r_idx % (page // CMP_STRIDE)) * CMP_STRIDE + r_idx // (page // CMP_STRIDE))[:, None])
    regroup = regroup.astype(BF16)
    grid_spec = pltpu.PrefetchScalarGridSpec(
        num_scalar_prefetch=2,
        grid=(b, n_pages // nop),
        in_specs=[page_spec(k) for k in range(nop)] + [
            pl.BlockSpec(regroup.shape, lambda i, c, ph, lb: (0, 0)),
            pl.BlockSpec(pe.shape, lambda i, c, ph, lb: (0, 0, 0, 0)),
            pl.BlockSpec(w_c.shape, lambda i, c, ph, lb: (0, 0, 0, 0)),
        ],
        out_specs=pl.BlockSpec((None, n, 2 * KV_WIDTH), lambda i, c, ph, lb: (i, c, 0)),
        scratch_shapes=[pltpu.VMEM((2, CMP_STRIDE, n, LANES), F32)],
    )
    return pl.pallas_call(
        functools.partial(_cpart_paged_kernel, nop=nop, page=page),
        grid_spec=grid_spec,
        out_shape=jax.ShapeDtypeStruct((b, n_pages * page // CMP_STRIDE, 2 * KV_WIDTH), F32),
        compiler_params=_cparams("parallel", "arbitrary"),
        name="cmp_partials_paged",
    )(phys, lblk, *([pool_t] * nop), regroup, pe, w_c)


def _group_query_columns(q_ref, g, t):
    zeros64 = jnp.zeros((HEAD_DIM, t), BF16)
    cols = []
    for pr in range(2):
        qt = q_ref[:, (2 * g + pr) * LANES:(2 * g + pr + 1) * LANES].astype(F32).T.astype(BF16)
        for half in range(2):
            qh = qt[half * HEAD_DIM:(half + 1) * HEAD_DIM]
            cols.append(jnp.concatenate([qh, zeros64] if g == 0 else [zeros64, qh], axis=0))
    return jnp.concatenate(cols, axis=1)


def _store_group_output(o_ref, out_t, g, t):
    for pr in range(2):
        pair = jnp.concatenate([out_t[:, (2 * pr) * t:(2 * pr + 1) * t], out_t[:, (2 * pr + 1) * t:(2 * pr + 2) * t]],
                               axis=0)
        o_ref[:, (2 * g + pr) * LANES:(2 * g + pr + 1) * LANES] = pair.T


def _cattn_kernel(q_ref, p_ref, bias_ref, gn_ref, gk_ref, seg_ref, o_ref, ns0_ref, ns1_ref, *,
                  tq, n_sub, n_cmp, n_slc, nslp, pos0):
    qi = pl.program_id(0)
    pall = p_ref[...]
    kraw = pall[:, 0:LANES] + pltpu.roll(pall[:, LANES:2 * LANES], n_sub - 1, 0)
    vc = pall[:, 2 * LANES:3 * LANES] + pltpu.roll(pall[:, 3 * LANES:4 * LANES], n_sub - 1, 0)
    ms = _split_dot(kraw * kraw, seg_ref[...])
    kc = ((kraw * lax.rsqrt(ms + EPS)) * gk_ref[...]).astype(BF16)
    vct = vc.T.astype(BF16)

    nsel = -(-n_slc // 8) * 8
    jj = _row_iota((nsel, n_sub))
    nn = _lane_iota((nsel, n_sub))
    covers_t = ((nn * CMP_STRIDE < (jj + 1) * SLC_BLOCK) & (nn * CMP_STRIDE + CMP_BLOCK - 1 >= jj * SLC_BLOCK)
                & (nn < n_cmp) & (jj < n_slc))
    covers_t = jnp.where(covers_t, 1.0, 0.0).astype(BF16)
    qpos = pos0 + qi * tq + _lane_iota((1, tq))
    qblk = jnp.right_shift(qpos, SLC_BLOCK.bit_length() - 1)
    jr = _row_iota((nsel, tq))
    jrf = jr.astype(F32)
    forced = (jr == 0) | (jr == qblk) | (jr == qblk - 1)
    causal = jr <= qblk
    gnt = gn_ref[...].T

    for g in range(NSA_KV_HEADS):
        bias = bias_ref[g]
        s = _dot(kc, _group_query_columns(q_ref, g, tq)) + bias
        m = jnp.max(s, axis=0, keepdims=True)
        m = jnp.where(m > 0.5 * NEG, m, 0.0)
        e = jnp.where(bias > 0.5 * NEG, jnp.exp(s - m), 0.0)
        p = e / jnp.maximum(jnp.sum(e, axis=0, keepdims=True), 1e-30)
        h0 = NSA_HPG * g
        gate = jnp.concatenate([gnt[h0 + c:h0 + c + 1] for c in range(NSA_HPG)], axis=1)
        out_t = _dot(vct[g * HEAD_DIM:(g + 1) * HEAD_DIM], p.astype(BF16)) * gate
        _store_group_output(o_ref, out_t, g, tq)

        prsum = p[:, 0:tq] + p[:, tq:2 * tq] + p[:, 2 * tq:3 * tq] + p[:, 3 * tq:4 * tq]
        hi = prsum.astype(BF16)
        lo = (prsum - hi.astype(F32)).astype(BF16)
        imp = _dot(covers_t, hi) + _dot(covers_t, lo)
        score = jnp.where(forced, jnp.inf, imp)
        score = jnp.where(causal, score, -jnp.inf)
        sel = jnp.zeros((nsel, tq), F32)
        for _ in range(min(SLC_TOPK, n_slc)):
            m = jnp.max(score, axis=0, keepdims=True)
            idx = jnp.min(jnp.where(score == m, jrf, 1e9), axis=0, keepdims=True)
            pick = jrf == idx
            sel = jnp.where(pick & (m > -jnp.inf), 1.0, sel)
            score = jnp.where(pick, -jnp.inf, score)
        ns = 1.0 - sel
        if nslp > nsel:
            ns = jnp.concatenate([ns, jnp.ones((nslp - nsel, tq), F32)], axis=0)
        (ns0_ref if g == 0 else ns1_ref)[...] = ns


def _cmp_bias_cols(bias, tq):
    _, s, n_sub = bias.shape
    bias_t = bias.reshape(NSA_KV_HEADS, NSA_HPG, s // tq, tq, n_sub).transpose(0, 4, 2, 1, 3)
    return bias_t.reshape(NSA_KV_HEADS, n_sub, NSA_HPG * s)


def _cmp_attention(q, parts, bias_t, gn, gk, *, tq, n_cmp, n_slc, pos0):
    b, s, _ = q.shape
    n_sub = parts.shape[1]
    nslp = -(-n_slc // LANES) * LANES
    kern = functools.partial(_cattn_kernel, tq=tq, n_sub=n_sub, n_cmp=n_cmp, n_slc=n_slc, nslp=nslp, pos0=pos0)
    return pl.pallas_call(
        kern,
        grid=(s // tq, b),
        in_specs=[
            pl.BlockSpec((None, tq, NSA_WIDTH), lambda i, j: (j, i, 0)),
            pl.BlockSpec((None, n_sub, 2 * KV_WIDTH), lambda i, j: (j, 0, 0)),
            pl.BlockSpec((NSA_KV_HEADS, n_sub, NSA_HPG * tq), lambda i, j: (0, 0, i)),
            pl.BlockSpec((None, tq, LANES), lambda i, j: (j, i, 0)),
            pl.BlockSpec((1, LANES), lambda i, j: (0, 0)),
            pl.BlockSpec((LANES, LANES), lambda i, j: (0, 0)),
        ],
        out_specs=[
            pl.BlockSpec((None, tq, NSA_WIDTH), lambda i, j: (j, i, 0)),
            pl.BlockSpec((None, nslp, tq), lambda i, j: (j, 0, i)),
            pl.BlockSpec((None, nslp, tq), lambda i, j: (j, 0, i)),
        ],
        out_shape=[
            jax.ShapeDtypeStruct((b, s, NSA_WIDTH), F32),
            jax.ShapeDtypeStruct((b, nslp, s), F32),
            jax.ShapeDtypeStruct((b, nslp, s), F32),
        ],
        compiler_params=_cparams("parallel", "parallel"),
        name="cmp_attention",
    )(q, parts, bias_t, gn, gk, _seg_matrix(LANES))


def _flash_kernel(q_ref, ns0_ref, ns1_ref, kv_ref, vt_ref, tab_ref, gn_ref, o_ref, *, t, use_sel, band, gate_base):
    qi = pl.program_id(1)
    row_k = _row_iota((t, LANES))
    lane_k = _lane_iota((t, LANES))
    lo_tile = jnp.maximum(qi - band, 0) if band is not None else 0
    gnt = gn_ref[...].T

    qts = []
    for g in range(NSA_KV_HEADS):
        qt_g = _group_query_columns(q_ref, g, t)
        if use_sel:
            nst = (ns0_ref if g == 0 else ns1_ref)[...].astype(BF16)
            qt_g = jnp.concatenate([qt_g, jnp.concatenate([nst] * NSA_HPG, axis=1)], axis=0)
        qts.append(qt_g)

    def body(kj, carry):
        k0 = pl.multiple_of(kj * t, t)
        kk = kv_ref[pl.ds(k0, t), :].astype(BF16)
        if use_sel:
            blk = kj * (t // SLC_BLOCK) + jnp.right_shift(row_k, SLC_BLOCK.bit_length() - 1)
            onehot = jnp.where(lane_k == blk, -(2.0 ** 30), 0.0).astype(BF16)
            kk = jnp.concatenate([kk, onehot], axis=1)
        delta = qi - kj
        if band is None:
            kind = jnp.minimum(delta, 2)
        else:
            kind = jnp.where(delta < 2, delta, jnp.where(delta < band, 2, 3))
        m_old, l_old, acc = carry
        s = _dot(kk, qt_all) + tab_ref[kind]
        m_new = jnp.maximum(m_old, jnp.max(s, axis=0, keepdims=True))
        alpha = jnp.exp(m_old - m_new)
        p = jnp.exp(s - m_new)
        l_new = alpha * l_old + jnp.sum(p, axis=0, keepdims=True)
        pb = p.astype(BF16)
        pv = jnp.concatenate(
            [_dot(vt_ref[g * HEAD_DIM:(g + 1) * HEAD_DIM, pl.ds(k0, t)].astype(BF16), pb[:, g * gw:(g + 1) * gw])
             for g in range(NSA_KV_HEADS)], axis=1)
        return m_new, l_new, acc * alpha + pv

    gw = NSA_HPG * t
    qt_all = jnp.concatenate(qts, axis=1)
    init = (jnp.full((1, NSA_HEADS * t), NEG, F32), jnp.zeros((1, NSA_HEADS * t), F32),
            jnp.zeros((HEAD_DIM, NSA_HEADS * t), F32))
    _, l_fin, acc = lax.fori_loop(lo_tile, qi + 1, body, init)
    gate = jnp.concatenate([gnt[gate_base + h:gate_base + h + 1] for h in range(NSA_HEADS)], axis=1)
    out = acc * (gate / l_fin)
    for g in range(NSA_KV_HEADS):
        _store_group_output(o_ref, out[:, g * gw:(g + 1) * gw], g, t)


def _flash_attention(q, ns0, ns1, k_rows, kv_t, tab, gn, *, use_sel, band, gate_base):
    b, s, _ = q.shape
    t = ATT_TILE
    assert ns0.shape[1] == LANES
    kern = functools.partial(_flash_kernel, t=t, use_sel=use_sel, band=band, gate_base=gate_base)
    tile = lambda w: pl.BlockSpec((None, t, w), lambda i, j: (i, j, 0))
    ns_tile = pl.BlockSpec((None, LANES, t), lambda i, j: (i, 0, j))
    return pl.pallas_call(
        kern,
        grid=(b, s // t),
        in_specs=[
            tile(NSA_WIDTH), ns_tile, ns_tile,
            pl.BlockSpec((None, s, LANES), lambda i, j: (i, 0, 0)),
            pl.BlockSpec((None, LANES, s), lambda i, j: (i, 1, 0)),
            pl.BlockSpec(tab.shape, lambda i, j: (0, 0, 0)),
            tile(LANES),
        ],
        out_specs=tile(NSA_WIDTH),
        out_shape=jax.ShapeDtypeStruct((b, s, NSA_WIDTH), F32),
        compiler_params=_cparams("parallel", "parallel"),
        name="flash_sel" if use_sel else "flash_win",
    )(q, ns0, ns1, k_rows, kv_t, tab, gn)


def _memattn_kernel(q_ref, kv_ref, o_ref):
    lane = _lane_iota((kv_ref.shape[0], LANES))
    for pr in range(MEM_HEADS // 2):
        qpair = q_ref[:, pr * LANES:(pr + 1) * LANES]
        kblk = kv_ref[:, pr * LANES:(pr + 1) * LANES]
        vblk = kv_ref[:, MEM_WIDTH + pr * LANES:MEM_WIDTH + (pr + 1) * LANES]
        out = None
        for half in range(2):
            keep = (lane < HEAD_DIM) if half == 0 else (lane >= HEAD_DIM)
            kk = jnp.where(keep, kblk, 0.0).astype(BF16)
            vv = jnp.where(keep, vblk, 0.0).astype(BF16)
            s = _dot_nt(qpair, kk)
            m = jnp.max(s, axis=1, keepdims=True)
            e = jnp.exp(s - m)
            p = e / jnp.sum(e, axis=1, keepdims=True)
            o = _dot(p.astype(BF16), vv)
            out = o if out is None else out + o
        o_ref[:, pr * LANES:(pr + 1) * LANES] = out


def _mem_attention(qm, mem_kv, tq):
    b, s, _ = qm.shape
    m = mem_kv.shape[1]
    return pl.pallas_call(
        _memattn_kernel,
        grid=(b, s // tq),
        in_specs=[
            pl.BlockSpec((None, tq, MEM_WIDTH), lambda i, j: (i, j, 0)),
            pl.BlockSpec((None, m, 2 * MEM_WIDTH), lambda i, j: (i, 0, 0)),
        ],
        out_specs=pl.BlockSpec((None, tq, MEM_WIDTH), lambda i, j: (i, j, 0)),
        out_shape=jax.ShapeDtypeStruct((b, s, MEM_WIDTH), F32),
        compiler_params=_cparams("parallel", "parallel"),
        name="mem_attention",
    )(qm, mem_kv)


def _dec_kernel(*refs, n_pg, pps, has_new):
    bidx_ref = refs[2]
    page_refs = refs[3:3 + pps]
    new_ref, wq_ref, bias_ref, ns_ref, gate_ref, o_ref, acc_ref, m_ref, l_ref = refs[3 + pps:]
    c = pl.program_id(1)
    n_chunks = pl.num_programs(1)

    @pl.when(c == 0)
    def _():
        acc_ref[...] = jnp.zeros_like(acc_ref)
        m_ref[...] = jnp.full_like(m_ref, NEG)
        l_ref[...] = jnp.zeros_like(l_ref)

    rk = _row_iota((LANES, LANES))

    def step(tiles, first_page):
        feats = (tiles[0] if len(tiles) == 1 else jnp.concatenate(tiles, axis=1)).astype(BF16)
        s = lax.dot_general(feats, wq_ref[...], (((0,), (1,)), ((), ())),
                            preferred_element_type=F32)
        extra = []
        for k in range(len(tiles)):
            pg = first_page + k
            ns = jnp.where(rk < SLC_BLOCK, ns_ref[pl.ds(2 * pg, 1), :], ns_ref[pl.ds(2 * pg + 1, 1), :])
            extra.append(jnp.where(ns > 0.5, NEG, bias_ref[bidx_ref[pg]]))
        s = s + (extra[0] if len(extra) == 1 else jnp.concatenate(extra, axis=0))
        m_old = m_ref[...]
        m_new = jnp.maximum(m_old, jnp.max(s, axis=0, keepdims=True))
        alpha = jnp.exp(m_old - m_new)
        p = jnp.exp(s - m_new)
        l_ref[...] = alpha * l_ref[...] + jnp.sum(p, axis=0, keepdims=True)
        m_ref[...] = m_new
        acc_ref[...] = acc_ref[...] * alpha + _dot(feats, p.astype(BF16))

    if has_new:
        @pl.when(c < n_chunks - 1)
        def _():
            step([r[...] for r in page_refs], c * pps)

        @pl.when(c == n_chunks - 1)
        def _():
            step([r[...] for r in page_refs] + [new_ref[...]], n_pg - pps)
    else:
        step([r[...] for r in page_refs], c * pps)

    @pl.when(c == n_chunks - 1)
    def _():
        o_ref[...] = acc_ref[...] / l_ref[...] * gate_ref[...]


def _decode_attention(pages, phys, lblk, new_rows, wq, bias_tab, bias_idx, notsel, gate):
    bsz, n_pg = phys.shape
    w = pages.shape[1]
    has_new = new_rows is not None
    pps = math.gcd(n_pg, DEC_PAGES_PER_STEP)
    n_steps = n_pg // pps
    n_chunks = n_steps
    if not has_new:
        new_rows = jnp.zeros((1, w, LANES), F32)
    new_map = (lambda i, c, ph, lb, bi: (i, 0, 0)) if has_new else (lambda i, c, ph, lb, bi: (0, 0, 0))

    def page_spec(k):
        def index(i, c, ph, lb, bi):
            return (ph[i, c * pps + k], 0, lb[i, c * pps + k])
        return pl.BlockSpec((None, w, LANES), index)

    per_b = lambda i, c, ph, lb, bi: (i, 0, 0)
    grid_spec = pltpu.PrefetchScalarGridSpec(
        num_scalar_prefetch=3,
        grid=(bsz, n_chunks),
        in_specs=[page_spec(k) for k in range(pps)] + [
            pl.BlockSpec((None, w, LANES), new_map),
            pl.BlockSpec((None, LANES, w), per_b),
            pl.BlockSpec(bias_tab.shape, lambda i, c, ph, lb, bi: (0, 0, 0)),
            pl.BlockSpec((None, notsel.shape[1], LANES), per_b),
            pl.BlockSpec((None, 1, LANES), per_b),
        ],
        out_specs=pl.BlockSpec((None, w, LANES), per_b),
        scratch_shapes=[pltpu.VMEM((w, LANES), F32), pltpu.VMEM((1, LANES), F32), pltpu.VMEM((1, LANES), F32)],
    )
    return pl.pallas_call(
        functools.partial(_dec_kernel, n_pg=n_pg, pps=pps, has_new=has_new),
        grid_spec=grid_spec,
        out_shape=jax.ShapeDtypeStruct((bsz, w, LANES), F32),
        compiler_params=_cparams("parallel", "arbitrary"),
        name="decode_attention",
    )(phys, lblk, bias_idx, *([pages] * pps), new_rows, wq, bias_tab, notsel, gate)


def _feature_major(cache):
    n, rows = cache.shape[:2]
    return cache.transpose(0, 2, 3, 4, 1).reshape(n, -1, rows)


def _tail_kernel(x_ref, g1_ref, op_ref, oc_ref, os_ref, ow_ref, om_ref, cnt0_ref, wgb_ref, wup_p_ref, wup_n_ref,
                 wup_m_ref, wout_ref, g2_ref, rw_ref, rb_ref, x2_ref, h2_ref, ei_ref, gt_ref, cnt_ref):
    @pl.when(pl.program_id(0) == 0)
    def _():
        cnt_ref[...] = cnt0_ref[...]

    x = x_ref[...]
    h = _rms(x, g1_ref[...]).astype(BF16)
    onsa = (oc_ref[...] + os_ref[...] + ow_ref[...]).astype(BF16)
    ups = (_dot(op_ref[...].astype(BF16), wup_p_ref[...]), _dot(onsa, wup_n_ref[...]),
           _dot(om_ref[...].astype(BF16), wup_m_ref[...]))
    mixed = None
    for br in range(3):
        gb = _sigmoid(_dot(h, wgb_ref[:, br * D_MODEL:(br + 1) * D_MODEL]))
        mixed = gb * ups[br] if mixed is None else mixed + gb * ups[br]
    x2 = x + _dot(mixed.astype(BF16), wout_ref[...])
    x2_ref[...] = x2
    h2 = _rms(x2, g2_ref[...])
    h2_ref[...] = _pack_bf16_pairs(h2)
    logits = _dot(h2.astype(BF16), rw_ref[...]) + rb_ref[...]
    lane = _lane_iota(logits.shape)
    lanef = lane.astype(F32)
    tops, idxs = [], []
    for _ in range(TOP_K):
        m = jnp.max(logits, axis=1, keepdims=True)
        idx = jnp.min(jnp.where(logits == m, lanef, 1e9), axis=1, keepdims=True)
        logits = jnp.where(lanef == idx, -jnp.inf, logits)
        tops.append(m)
        idxs.append(idx)
    es = [jnp.exp(tk - tops[0]) for tk in tops]
    den = es[0] + es[1] + es[2] + es[3]
    tm = logits.shape[0]
    onehot = jnp.zeros(logits.shape, F32)
    for k in range(TOP_K):
        onehot = jnp.where(lanef == idxs[k], 1.0, onehot)
    tri = jnp.where(_row_iota((tm, tm)) > _lane_iota((tm, tm)), 1.0, 0.0).astype(BF16)
    before = _dot(tri, onehot.astype(BF16)) + cnt_ref[...]
    cnt_ref[...] = cnt_ref[...] + jnp.sum(onehot, axis=0, keepdims=True)
    ei = jnp.zeros(logits.shape, F32)
    gt = jnp.zeros(logits.shape, F32)
    for k in range(TOP_K):
        rank = jnp.sum(jnp.where(lanef == idxs[k], before, 0.0), axis=1, keepdims=True)
        ei = jnp.where(lane == k, idxs[k], ei)
        ei = jnp.where(lane == TOP_K + k, rank, ei)
        gt = jnp.where(lane == k, es[k] / den, gt)
    ei_ref[...] = ei.astype(jnp.int32)
    gt_ref[...] = gt


def _layer_tail(x, o_pool, o_cmp, o_slc, o_win, o_mem, cnt0, w, tm):
    n = x.shape[0]
    row = lambda wd: pl.BlockSpec((tm, wd), lambda i: (i, 0))
    full = lambda a: pl.BlockSpec(a.shape, lambda i: (0,) * a.ndim)
    weights = (w["wgb"], w["wup_pool"], w["wup_nsa"], w["wup_mem"], w["wout"], w["g2"], w["rw"], w["rb"])
    return pl.pallas_call(
        _tail_kernel,
        grid=(n // tm,),
        in_specs=[row(D_MODEL), full(w["g1"]), row(POOL_WIDTH), row(NSA_WIDTH), row(NSA_WIDTH), row(NSA_WIDTH),
                  row(MEM_WIDTH), full(cnt0)] + [full(a) for a in weights],
        out_specs=[row(D_MODEL), row(D_MODEL // 2), row(LANES), row(LANES), full(cnt0)],
        out_shape=[jax.ShapeDtypeStruct((n, D_MODEL), F32), jax.ShapeDtypeStruct((n, D_MODEL // 2), jnp.uint32),
                   jax.ShapeDtypeStruct((n, LANES), jnp.int32), jax.ShapeDtypeStruct((n, LANES), F32),
                   jax.ShapeDtypeStruct((1, LANES), F32)],
        compiler_params=_cparams("arbitrary"),
        name="layer_tail",
    )(x, w["g1"], o_pool, o_cmp, o_slc, o_win, o_mem, cnt0, *weights)


def _ffn_kernel(be_ref, nu_ref, x_ref, wgu_ref, bgu_ref, wd_ref, bd_ref, o_ref, wgu_bf, wd_bf):
    i = pl.program_id(0)

    @pl.when((i == 0) | (be_ref[i] != be_ref[jnp.maximum(i - 1, 0)]))
    def _():
        wgu_bf[...] = wgu_ref[...].astype(BF16)
        wd_bf[...] = wd_ref[...].astype(BF16)

    @pl.when(i < nu_ref[0])
    def _():
        gu = _dot(_unpack_bf16_pairs(x_ref[...]).astype(BF16), wgu_bf[...]) + bgu_ref[...]
        gate = jnp.minimum(gu[:, :D_FF], SWIGLU_LIMIT)
        up = jnp.clip(gu[:, D_FF:], -SWIGLU_LIMIT, SWIGLU_LIMIT)
        act = gate * _sigmoid(SWIGLU_ALPHA * gate) * (up + 1.0)
        o_ref[...] = _dot(act.astype(BF16), wd_bf[...]) + bd_ref[...]

    @pl.when(i >= nu_ref[0])
    def _():
        o_ref[...] = jnp.zeros_like(o_ref)


def _expert_ffn(rows, blk_e, n_used, wgu, bgu, wd, bd):
    n_rows = rows.shape[0]
    n_blocks = n_rows // MOE_TILE
    blk = lambda i, be, nu: (jnp.minimum(i, nu[0] - 1), 0)
    grid_spec = pltpu.PrefetchScalarGridSpec(
        num_scalar_prefetch=2,
        grid=(n_blocks,),
        in_specs=[
            pl.BlockSpec((MOE_TILE, D_MODEL // 2), blk),
            pl.BlockSpec((None, D_MODEL, 2 * D_FF), lambda i, be, nu: (be[i], 0, 0)),
            pl.BlockSpec((None, 1, 2 * D_FF), lambda i, be, nu: (be[i], 0, 0)),
            pl.BlockSpec((None, D_FF, D_MODEL), lambda i, be, nu: (be[i], 0, 0)),
            pl.BlockSpec((None, 1, D_MODEL), lambda i, be, nu: (be[i], 0, 0)),
        ],
        out_specs=pl.BlockSpec((MOE_TILE, D_MODEL), lambda i, be, nu: (i, 0)),
        scratch_shapes=[pltpu.VMEM((D_MODEL, 2 * D_FF), BF16), pltpu.VMEM((D_FF, D_MODEL), BF16)],
    )
    return pl.pallas_call(
        _ffn_kernel,
        grid_spec=grid_spec,
        out_shape=jax.ShapeDtypeStruct((n_rows, D_MODEL), F32),
        compiler_params=pltpu.CompilerParams(dimension_semantics=("arbitrary",), vmem_limit_bytes=FFN_VMEM_LIMIT),
        name="expert_ffn",
    )(blk_e, n_used, rows, wgu, bgu, wd, bd)


def _token_tile(n):
    return next(t for t in (MOE_DMA_TOKENS, 384, 256, 128, n) if n % t == 0)


def _dispatch_scatter(dest_ref, h_ref, rows_ref, sem, td):
    def issue(i, carry):
        t0 = pl.multiple_of(i * 8, 8)
        for r in range(8):
            for k in range(TOP_K):
                d = dest_ref[i * (8 * TOP_K) + r * TOP_K + k]
                pltpu.make_async_copy(h_ref.at[pl.ds(t0 + r, 1)], rows_ref.at[pl.ds(d, 1)], sem).start()
        return carry

    lax.fori_loop(0, td // 8, issue, 0)
    for k in range(TOP_K):
        pltpu.make_async_copy(h_ref, rows_ref.at[pl.ds(0, td)], sem).wait()


def _dispatch_kernel(*refs, tiles, steps, n_blocks):
    ng = len(tiles)
    ends_ref, padded_ref, nu_ref = refs[:3]
    dest_refs = refs[3:3 + ng]
    h_refs = refs[3 + ng:3 + 2 * ng]
    rows_ref, zero_ref, sem, zsem = refs[3 + 2 * ng:]
    i = pl.program_id(0)

    @pl.when(i == 0)
    def _():
        zero_ref[...] = jnp.zeros_like(zero_ref)

        def tail_copy(e):
            start = pl.multiple_of(ends_ref[e] - MOE_TILE, MOE_TILE)
            return pltpu.make_async_copy(zero_ref, rows_ref.at[pl.ds(start, MOE_TILE)], zsem)

        def block_copy(blk):
            start = pl.multiple_of(blk * MOE_TILE, MOE_TILE)
            return pltpu.make_async_copy(zero_ref, rows_ref.at[pl.ds(start, MOE_TILE)], zsem)

        def each(start):
            def expert(e, carry):
                @pl.when(padded_ref[e] > 0)
                def _():
                    tail_copy(e).start() if start else tail_copy(e).wait()
                return carry

            def block(blk, carry):
                block_copy(blk).start() if start else block_copy(blk).wait()
                return carry

            lax.fori_loop(0, N_EXPERTS, expert, 0)
            lax.fori_loop(nu_ref[0], n_blocks, block, 0)

        each(True)
        each(False)

    first = 0
    for g in range(ng):
        @pl.when((i >= first) & (i < first + steps[g]))
        def _(g=g):
            _dispatch_scatter(dest_refs[g], h_refs[g], rows_ref, sem, tiles[g])
        first += steps[g]


def _moe_dispatch(h2s, dests, n_rows, pad_ends, padded, n_used):
    tiles = [_token_tile(h.shape[0]) for h in h2s]
    steps = [h.shape[0] // t for h, t in zip(h2s, tiles)]
    firsts = [sum(steps[:g]) for g in range(len(h2s))]

    def local(g):
        return lambda i, *_: jnp.clip(i - firsts[g], 0, steps[g] - 1)

    dest_specs = [pl.BlockSpec((tiles[g] * TOP_K,), lambda i, *_, f=local(g): (f(i),), memory_space=pltpu.SMEM)
                  for g in range(len(h2s))]
    width, dtype = h2s[0].shape[1], h2s[0].dtype
    tok_specs = [pl.BlockSpec((tiles[g], width), lambda i, *_, f=local(g): (f(i), 0)) for g in range(len(h2s))]
    grid_spec = pltpu.PrefetchScalarGridSpec(
        num_scalar_prefetch=3,
        grid=(sum(steps),),
        in_specs=dest_specs + tok_specs,
        out_specs=pl.BlockSpec(memory_space=pl.ANY),
        scratch_shapes=[pltpu.VMEM((MOE_TILE, width), dtype), pltpu.SemaphoreType.DMA(()),
                        pltpu.SemaphoreType.DMA(())],
    )
    return pl.pallas_call(
        functools.partial(_dispatch_kernel, tiles=tiles, steps=steps, n_blocks=n_rows // MOE_TILE),
        grid_spec=grid_spec,
        out_shape=jax.ShapeDtypeStruct((n_rows, width), dtype),
        compiler_params=_cparams("arbitrary"),
        name="moe_dispatch",
    )(pad_ends, padded, n_used, *dests, *h2s)


def _combine_kernel(dest_ref, dnext_ref, x2_ref, g_ref, rows_ref, o_ref, ybuf, sem, *, td):
    i = pl.program_id(0)
    last = pl.num_programs(0) - 1

    def start_tile(d_ref, slot):
        def issue(blk, carry):
            t0 = pl.multiple_of(blk * 8, 8)
            for r in range(8):
                for k in range(TOP_K):
                    d = d_ref[blk * (8 * TOP_K) + r * TOP_K + k]
                    pltpu.make_async_copy(rows_ref.at[pl.ds(d, 1)], ybuf.at[slot, k, pl.ds(t0 + r, 1)],
                                          sem.at[slot]).start()
            return carry

        lax.fori_loop(0, td // 8, issue, 0)

    @pl.when(i == 0)
    def _():
        start_tile(dest_ref, 0)

    @pl.when(i < last)
    def _():
        start_tile(dnext_ref, (i + 1) % 2)

    slot = i % 2
    for k in range(TOP_K):
        pltpu.make_async_copy(rows_ref.at[pl.ds(0, td)], ybuf.at[slot, k], sem.at[slot]).wait()
    out = x2_ref[...]
    for k in range(TOP_K):
        out = out + g_ref[:, k:k + 1] * ybuf[slot, k]
    o_ref[...] = out


def _moe_combine(x2, out_rows, dest, gates):
    n = x2.shape[0]
    td = _token_tile(n)
    steps = n // td
    return pl.pallas_call(
        functools.partial(_combine_kernel, td=td),
        grid=(steps,),
        in_specs=[pl.BlockSpec((td * TOP_K,), lambda i: (i,), memory_space=pltpu.SMEM),
                  pl.BlockSpec((td * TOP_K,), lambda i: (jnp.minimum(i + 1, steps - 1),), memory_space=pltpu.SMEM),
                  pl.BlockSpec((td, D_MODEL), lambda i: (i, 0)),
                  pl.BlockSpec((td, LANES), lambda i: (i, 0)),
                  pl.BlockSpec(memory_space=pl.ANY)],
        out_specs=pl.BlockSpec((td, D_MODEL), lambda i: (i, 0)),
        out_shape=jax.ShapeDtypeStruct((n, D_MODEL), F32),
        scratch_shapes=[pltpu.VMEM((2, TOP_K, td) + out_rows.shape[1:], out_rows.dtype),
                        pltpu.SemaphoreType.DMA((2,))],
        compiler_params=_cparams("arbitrary"),
        name="moe_combine",
    )(dest, dest, x2, gates, out_rows)


def _moe(groups, counts, w):
    n_total = sum(g[0].shape[0] for g in groups)
    cnt = counts[0, :N_EXPERTS].astype(jnp.int32)
    padded = (cnt + MOE_TILE - 1) // MOE_TILE * MOE_TILE
    pad_ends = jnp.cumsum(padded)
    pad_starts = pad_ends - padded
    n_blocks = -(-n_total * TOP_K // MOE_TILE) + N_EXPERTS
    blk_start = jnp.arange(n_blocks, dtype=jnp.int32) * MOE_TILE
    blk_e = jnp.minimum(jnp.sum(blk_start[:, None] >= pad_ends[None, :], axis=1), N_EXPERTS - 1).astype(jnp.int32)
    n_used = (pad_ends[-1] // MOE_TILE).astype(jnp.int32).reshape(1)
    dests = [(pad_starts[er[:, :TOP_K]] + er[:, TOP_K:2 * TOP_K]).astype(jnp.int32).reshape(-1)
             for (_, _, er, _) in groups]
    rows = _moe_dispatch([g[1] for g in groups], dests, n_blocks * MOE_TILE, pad_ends.astype(jnp.int32),
                         padded.astype(jnp.int32), n_used)
    out_rows = _expert_ffn(rows, blk_e, n_used, w["wgu"], w["bgu"], w["wd"], w["bd"])
    return [_moe_combine(x2, out_rows, dest, gates) for (x2, _, _, gates), dest in zip(groups, dests)]


def _rel_bucket(dist):
    n = jnp.maximum(dist, 0)
    max_exact = NUM_BUCKETS // 2
    nf = jnp.maximum(n, 1).astype(F32)
    large = max_exact + (jnp.log(nf / max_exact) / math.log(MAX_DISTANCE / max_exact)
                         * (NUM_BUCKETS - max_exact)).astype(jnp.int32)
    large = jnp.minimum(large, NUM_BUCKETS - 1)
    return jnp.where(n < max_exact, n, large)


def _bias_of(rel_bias, dist, valid):
    return jnp.where(valid[..., None], rel_bias[_rel_bucket(dist)], NEG)


def _proj_segs(kv_forms, kvc_forms):
    return ((0, 512, "qscale", ("rows",), BF16), (512, 256, "qscale", ("rows",), BF16),
            (768, 256, "id", kv_forms, F32), (1024, 256, "id", kv_forms, F32), (1280, 256, "id", kvc_forms, F32),
            (1536, 256, "id", ("rows",), F32), (1792, 128, "sigmoid", ("rows",), F32))


_PROJ_SEGS_PROMPT = _proj_segs(("key_rows", "t"), ("t",))
_PROJ_SEGS_SAMPLE = _proj_segs(("rows",), ("rows",))
_PROJ_NNORM = 1280


def _prep_layer(l, rel_bias, norm1_g, w_in, nsa_qk_norm, mem_qk_norm, cmp_w, cmp_pe, pool_w, pool_scale,
                mem_norm_g, w_mem_kv, w_up_pool, w_up_nsa, w_up_mem, w_out, norm2_g, router_w, router_b,
                w_gu, b_gu, w_down, b_down):
    wi = w_in[l]
    o_u, o_q, o_qm, o_kvc, o_kvs, o_kvw, o_gn, o_gb = 0, 256, 768, 1024, 1280, 1536, 1792, 1816
    w_proj = jnp.concatenate([
        wi[:, o_q:o_q + 512], wi[:, o_qm:o_qm + 256], wi[:, o_kvs:o_kvs + 256], wi[:, o_kvw:o_kvw + 256],
        wi[:, o_kvc:o_kvc + 256], wi[:, o_u:o_u + 256], wi[:, o_gn:o_gn + 24],
        jnp.zeros((D_MODEL, LANES - 24), F32)], axis=1).astype(BF16)
    nq, mq = nsa_qk_norm[l], mem_qk_norm[l]
    ones = jnp.ones((LANES,), F32)
    gain = jnp.concatenate([jnp.tile(nq[0], 8), jnp.tile(mq[0], 4), jnp.tile(nq[2], 2), ones,
                            jnp.tile(nq[3], 2), ones])[None, :]
    nmask = jnp.concatenate([jnp.ones((768,), F32), ones, 0 * ones, ones, 0 * ones])[None, :]
    eye4 = jnp.eye(4, dtype=F32)
    cw = cmp_w[l].reshape(2, 2, CMP_STRIDE, HEAD_DIM, HEAD_DIM)
    w_c = jnp.einsum("crjde,xy->cjxdrye", cw, jnp.eye(2, dtype=F32))
    w_c = w_c.reshape(2, CMP_STRIDE, LANES, KV_WIDTH).astype(BF16)
    pe = cmp_pe[l].reshape(2, 2, CMP_STRIDE, HEAD_DIM)
    pe_c = jnp.tile(pe.transpose(0, 2, 1, 3), (1, 1, 1, NSA_KV_HEADS))
    pe_c = jnp.pad(pe_c, ((0, 0), (0, 0), (0, 6), (0, 0)))
    w_pool = jnp.einsum("gde,gh->gdhe", pool_w[l], eye4).reshape(POOL_WIDTH, POOL_WIDTH).astype(BF16)
    rw = jnp.pad(router_w[l], ((0, 0), (0, LANES - N_EXPERTS))).astype(BF16)
    rb = jnp.concatenate([router_b[l], jnp.full((LANES - N_EXPERTS,), NEG, F32)])[None, :]
    return {
        "g1": norm1_g[l][None, :], "w_proj": w_proj, "gain": gain, "nmask": nmask,
        "w_kvc_t": wi[:, o_kvc:o_kvc + 256].T.astype(BF16),
        "gk_cmp": jnp.tile(nq[1], 2)[None, :], "w_c": w_c, "pe_c": pe_c,
        "w_pool": w_pool, "pool_scale": pool_scale[l][None, :],
        "mem_g": mem_norm_g[l][None, :], "w_mem": w_mem_kv[l].astype(BF16),
        "mem_gain": jnp.concatenate([jnp.tile(mq[1], 4), jnp.ones((256,), F32)])[None, :],
        "mem_nmask": jnp.concatenate([jnp.ones((256,), F32), jnp.zeros((256,), F32)])[None, :],
        "wgb": wi[:, o_gb:o_gb + 3 * D_MODEL].astype(BF16),
        "wup_pool": w_up_pool[l].astype(BF16), "wup_nsa": w_up_nsa[l].astype(BF16),
        "wup_mem": w_up_mem[l].astype(BF16), "wout": w_out[l].astype(BF16), "g2": norm2_g[l][None, :],
        "rw": rw, "rb": rb,
        "wgu": w_gu[l], "bgu": b_gu[l][:, None, :], "wd": w_down[l],
        "bd": b_down[l][:, None, :],
    }


def _project_in(x2d, w, segs, tm, seq):
    wt = w["w_kvc_t"] if segs is _PROJ_SEGS_PROMPT else None
    return _project(x2d, w["g1"], w["w_proj"], w["gain"], w["nmask"], segs, _PROJ_NNORM, tm, seq, wt)


def _rows_view(a_t, heads):
    b, _, rows = a_t.shape
    return a_t.reshape(b, 2, heads, HEAD_DIM, rows).transpose(0, 4, 1, 2, 3)


def _toeplitz(v, t):
    lead = v.shape[:-1]
    flat = jnp.tile(v, (1,) * len(lead) + (t,))[..., t:t + t * (2 * t - 1)]
    return flat.reshape(lead + (t, 2 * t - 1))[..., :t]


def _flash_tables(rel_bias):
    t = ATT_TILE
    d0 = jnp.arange(-t, t)
    kinds = jnp.stack([
        _bias_of(rel_bias, d0, d0 >= 0),
        _bias_of(rel_bias, d0 + t, d0 + t >= 0),
        _bias_of(rel_bias, jnp.full((2 * t,), 2 * t), jnp.ones((2 * t,), bool)),
        _bias_of(rel_bias, d0 + WINDOW, d0 + WINDOW < WINDOW),
    ])
    tab = _toeplitz(kinds.transpose(2, 0, 1), t)
    return tab.transpose(1, 2, 0, 3).reshape(4, t, NSA_HEADS * t)


def _cmp_bias_table(rel_bias, s, n_sub, n_cmp, pos0, tq):
    na = s // CMP_STRIDE
    m = max(na, n_sub)
    k = jnp.arange(-m, m)[None, :]
    r = jnp.arange(CMP_STRIDE)[:, None]
    d = CMP_STRIDE * k + r - (CMP_BLOCK - 1) + pos0
    v = _bias_of(rel_bias, d, d >= 0).transpose(2, 0, 1)
    tz = _toeplitz(v, m)[:, :, :n_sub, :na]
    tz = tz.reshape(NSA_KV_HEADS, NSA_HPG, CMP_STRIDE, n_sub, s // tq, tq // CMP_STRIDE)
    tab = tz.transpose(0, 3, 4, 1, 5, 2).reshape(NSA_KV_HEADS, n_sub, NSA_HPG * s)
    return jnp.where(jnp.arange(n_sub)[None, :, None] < n_cmp, tab, NEG)


def _prompt_pre(x, mem, w, rel_bias, cnt0):
    b, s, _ = x.shape
    n = b * s
    tm = 512 if n % 512 == 0 else ATT_TILE
    x2d = x.reshape(n, D_MODEL)
    q, qm, ks, kvs_t, kw, kvw_t, kvc_t, u, gn = _project_in(x2d, w, _PROJ_SEGS_PROMPT, math.gcd(s, 1024), s)
    r3 = lambda a: a.reshape(b, s, a.shape[-1])
    q, qm, ks, kw, u, gn = map(r3, (q, qm, ks, kw, u, gn))

    o_pool = _pool_mix(u, jnp.zeros((b, 16, POOL_WIDTH), F32), w["w_pool"], w["pool_scale"], 0)

    n_cmp = (s - CMP_BLOCK) // CMP_STRIDE + 1
    n_slc = -(-s // SLC_BLOCK)
    n_lb = s // LANES
    own = jnp.broadcast_to(jnp.arange(b, dtype=jnp.int32)[:, None], (b, n_lb))
    blocks = jnp.broadcast_to(jnp.arange(n_lb, dtype=jnp.int32)[None, :], (b, n_lb))
    parts = _cmp_partials_paged(kvc_t, own, blocks, w["pe_c"], w["w_c"], math.gcd(n_lb, 16))
    n_sub = parts.shape[1]
    tq = math.gcd(s, 512)
    bias_c = _cmp_bias_table(rel_bias, s, n_sub, n_cmp, 0, tq)
    o_cmp, ns0, ns1 = _cmp_attention(q, parts, bias_c, gn, w["gk_cmp"], tq=tq, n_cmp=n_cmp, n_slc=n_slc, pos0=0)

    tab = _flash_tables(rel_bias)
    o_slc = _flash_attention(q, ns0, ns1, ks, kvs_t, tab, gn, use_sel=True, band=None, gate_base=NSA_HEADS)
    o_win = _flash_attention(q, ns0, ns1, kw, kvw_t, tab, gn, use_sel=False, band=WINDOW // ATT_TILE,
                             gate_base=2 * NSA_HEADS)

    m = mem.shape[1]
    mem_kv, mem_kv_t = _project(mem.reshape(b * m, D_MODEL), w["mem_g"], w["w_mem"], w["mem_gain"],
                                w["mem_nmask"], ((0, 2 * MEM_WIDTH, "id", ("rows", "t"), F32),), MXU_DIM,
                                tm=math.gcd(m, 512), seq=m)
    o_mem = _mem_attention(qm, mem_kv.reshape(b, m, 2 * MEM_WIDTH), tq=math.gcd(s, 2048))

    f2 = lambda a: a.reshape(n, a.shape[-1])
    x2, h2, eidx, gates, cnt = _layer_tail(x2d, f2(o_pool), f2(o_cmp), f2(o_slc), f2(o_win), f2(o_mem), cnt0, w, tm)
    win_t = kvw_t[:, :, max(0, s - WINDOW):]
    if s < WINDOW:
        win_t = jnp.pad(win_t, ((0, 0), (0, 0), (WINDOW - s, 0)))
    states = (_rows_view(kvc_t, NSA_KV_HEADS), _rows_view(kvs_t, NSA_KV_HEADS), _rows_view(win_t, NSA_KV_HEADS),
              _rows_view(mem_kv_t, MEM_HEADS), _last_rows(u, POOL_BUF))
    return (x2, h2, eidx, gates), states, cnt


def _last_rows(a, n):
    t = a.shape[1]
    if t < n:
        a = jnp.pad(a, [(0, 0), (n - t, 0)] + [(0, 0)] * (a.ndim - 2))
    return a[:, a.shape[1] - n:]


def _dec_columns_nsa(q):
    b, t, _ = q.shape
    qh = q.reshape(b, t, NSA_KV_HEADS, NSA_HPG, HEAD_DIM)
    w = jnp.einsum("btgpd,gx->bxdgtp", qh.astype(F32), jnp.eye(NSA_KV_HEADS, dtype=F32))
    w = w.reshape(b, NSA_KV_HEADS * HEAD_DIM, NSA_KV_HEADS * t * NSA_HPG)
    return jnp.pad(w, ((0, 0), (0, KV_WIDTH - w.shape[1]), (0, LANES - w.shape[2]))).astype(BF16)


def _dec_extract_nsa(o, t):
    b = o.shape[0]
    v = o[:, LANES:, :NSA_KV_HEADS * t * NSA_HPG]
    v = v.reshape(b, NSA_KV_HEADS, HEAD_DIM, NSA_KV_HEADS, t, NSA_HPG)
    v = jnp.einsum("bxdgtp,gx->btgpd", v, jnp.eye(NSA_KV_HEADS, dtype=F32))
    return v.reshape(b, t, NSA_WIDTH)


def _dec_bias_cols(bias_tph):
    k, t, _ = bias_tph.shape
    bt = bias_tph.reshape(k, t, NSA_KV_HEADS, NSA_HPG).transpose(0, 2, 1, 3).reshape(k, NSA_KV_HEADS * t * NSA_HPG)
    return jnp.pad(bt, ((0, 0), (0, LANES - bt.shape[1])))


def _dec_gate_cols(gn, base, t):
    b = gn.shape[0]
    gt = gn[:, :, base:base + NSA_HEADS].reshape(b, t, NSA_KV_HEADS, NSA_HPG).transpose(0, 2, 1, 3)
    gt = gt.reshape(b, 1, NSA_KV_HEADS * t * NSA_HPG)
    return jnp.pad(gt, ((0, 0), (0, 0), (0, LANES - gt.shape[2])), constant_values=1.0)


def _sample_pre(x, cache_cmp, cache_slc, cache_win, cache_mem, pool_buf, page_table, w, rel_bias, cnt0):
    b, t, _ = x.shape
    n = b * t
    page = cache_cmp.shape[1]
    n_pages = page_table.shape[1]
    past = n_pages * page
    x2d = x.reshape(n, D_MODEL)
    tm = n if n <= 512 else LANES
    q, qm, kvs, kvw, kvc, u, gn = _project_in(x2d, w, _PROJ_SEGS_SAMPLE, tm, tm)
    r3 = lambda a: a.reshape(b, t, a.shape[-1])
    q, qm, kvs, kvw, kvc, u, gn = map(r3, (q, qm, kvs, kvw, kvc, u, gn))
    qpos = past + jnp.arange(t)

    buf16 = jnp.pad(pool_buf, ((0, 0), (16 - POOL_BUF, 0), (0, 0)))
    o_pool = _pool_mix(u, buf16, w["w_pool"], w["pool_scale"], past)

    total = past + t
    n_cmp = (total - CMP_BLOCK) // CMP_STRIDE + 1
    n_sub_used = n_cmp + CMP_BLOCK // CMP_STRIDE - 1
    n_slc = -(-total // SLC_BLOCK)
    pps = math.gcd(n_pages, 16)
    parts = _cmp_partials_paged(_feature_major(cache_cmp), page_table, jnp.zeros_like(page_table), w["pe_c"],
                                w["w_c"], pps)
    extra = n_sub_used * CMP_STRIDE - past
    if extra > 0:
        tail_rows = -(-extra // CMP_STRIDE) * CMP_STRIDE
        new_c = jnp.pad(kvc, ((0, 0), (0, max(0, tail_rows - t)), (0, 0)))[:, :tail_rows]
        parts = jnp.concatenate([parts, _cmp_partials_dense(new_c, w["pe_c"], w["w_c"])], axis=1)
    n_sub = parts.shape[1]
    end = jnp.arange(n_sub)[None, :] * CMP_STRIDE + CMP_BLOCK - 1
    bias_c = _bias_of(rel_bias, qpos[:, None] - end, (end <= qpos[:, None]) & (jnp.arange(n_sub)[None, :] < n_cmp))
    qpad = ((0, 0), (0, LANES - t), (0, 0))
    bias_c = _cmp_bias_cols(jnp.pad(bias_c.transpose(2, 0, 1), qpad), LANES)
    o_cmp, ns0, ns1 = _cmp_attention(jnp.pad(q, qpad), parts, bias_c, jnp.pad(gn, qpad), w["gk_cmp"], tq=LANES,
                                     n_cmp=n_cmp, n_slc=n_slc, pos0=past)
    o_cmp = o_cmp[:, :t]

    wq = _dec_columns_nsa(q).transpose(0, 2, 1)
    ncol = NSA_KV_HEADS * t * NSA_HPG
    new_tile = lambda kv: jnp.pad(kv, ((0, 0), (0, LANES - t), (0, 0))).transpose(0, 2, 1)
    own = lambda npg: jnp.broadcast_to(jnp.arange(b, dtype=jnp.int32)[:, None], (b, npg))
    blocks = lambda npg: jnp.broadcast_to(jnp.arange(npg, dtype=jnp.int32)[None, :], (b, npg))

    n_chunks = n_pages + 1
    nblk = -(-2 * n_chunks // 8) * 8
    ns = jnp.stack([ns0, ns1], axis=1)[:, :, :, :t].transpose(0, 1, 3, 2)
    ns = jnp.pad(ns, ((0, 0), (0, 0), (0, 0), (0, max(0, nblk - ns.shape[3]))))[..., :nblk]
    ns = jnp.broadcast_to(ns[:, :, :, None, :], (b, NSA_KV_HEADS, t, NSA_HPG, nblk)).reshape(b, ncol, nblk)
    notsel = jnp.pad(ns.transpose(0, 2, 1), ((0, 0), (0, 0), (0, LANES - ncol)))
    rows = jnp.arange(LANES)
    far = _bias_of(rel_bias, jnp.full((LANES, t), 2 * MAX_DISTANCE), jnp.ones((LANES, t), bool))
    kpos_last = past - LANES + rows
    d_last = qpos[None, :] - kpos_last[:, None]
    near = _bias_of(rel_bias, d_last, d_last >= 0)
    kpos_new = past + rows
    d_new = qpos[None, :] - kpos_new[:, None]
    newb = _bias_of(rel_bias, d_new, (d_new >= 0) & (rows[:, None] < t))
    bias_tab = jnp.stack([_dec_bias_cols(far), _dec_bias_cols(near), _dec_bias_cols(newb)])
    bias_idx = jnp.concatenate([jnp.zeros((n_pages - 1,), jnp.int32), jnp.array([1, 2], jnp.int32)])
    o_slc = _decode_attention(_feature_major(cache_slc), page_table, jnp.zeros_like(page_table), new_tile(kvs), wq,
                              bias_tab, bias_idx, notsel, _dec_gate_cols(gn, NSA_HEADS, t))
    o_slc = _dec_extract_nsa(o_slc, t)

    wb = cache_win.shape[1]
    n_wpg = wb // LANES
    kpos_w = past - wb + jnp.arange(wb + LANES)
    d_w = qpos[None, :] - kpos_w[:, None]
    valid_w = (d_w >= 0) & (d_w < WINDOW) & (kpos_w[:, None] >= 0) & (jnp.arange(wb + LANES)[:, None] < wb + t)
    bias_w = _dec_bias_cols(_bias_of(rel_bias, d_w, valid_w)).reshape(n_wpg + 1, LANES, LANES)
    zeros_ns = jnp.zeros((b, -(-2 * (n_wpg + 1) // 8) * 8, LANES), F32)
    o_win = _decode_attention(_feature_major(cache_win), own(n_wpg), blocks(n_wpg), new_tile(kvw), wq, bias_w,
                              jnp.arange(n_wpg + 1, dtype=jnp.int32), zeros_ns, _dec_gate_cols(gn, 2 * NSA_HEADS, t))
    o_win = _dec_extract_nsa(o_win, t)

    m = cache_mem.shape[1]
    n_mpg = m // LANES
    qmh = qm.reshape(b, t, MEM_HEADS, HEAD_DIM).astype(F32)
    wqm = jnp.einsum("bthd,hx->bxdht", qmh, jnp.eye(MEM_HEADS, dtype=F32))
    wqm = wqm.reshape(b, MEM_WIDTH, MEM_HEADS * t)
    wqm = jnp.pad(wqm, ((0, 0), (0, MEM_WIDTH), (0, LANES - MEM_HEADS * t))).astype(BF16).transpose(0, 2, 1)
    o_mem = _decode_attention(_feature_major(cache_mem), own(n_mpg), blocks(n_mpg), None, wqm,
                              jnp.zeros((1, LANES, LANES), F32), jnp.zeros((n_mpg,), jnp.int32),
                              jnp.zeros((b, 8, LANES), F32), jnp.ones((b, 1, LANES), F32))
    om = o_mem[:, MEM_WIDTH:, :MEM_HEADS * t].reshape(b, MEM_HEADS, HEAD_DIM, MEM_HEADS, t)
    o_mem = jnp.einsum("bxdht,hx->bthd", om, jnp.eye(MEM_HEADS, dtype=F32)).reshape(b, t, MEM_WIDTH)

    f2 = lambda a: a.reshape(n, a.shape[-1])
    x2, h2, eidx, gates, cnt = _layer_tail(x2d, f2(o_pool), f2(o_cmp), f2(o_slc), f2(o_win), f2(o_mem), cnt0, w,
                                           n if n <= 512 else LANES)
    kvshape = (b, t, 2, NSA_KV_HEADS, HEAD_DIM)
    new_win = _rows_view(jnp.concatenate([_feature_major(cache_win)[:, :, t:], kvw.transpose(0, 2, 1)], axis=2),
                         NSA_KV_HEADS)
    new_pool = jnp.concatenate([pool_buf, u], axis=1)[:, t:]
    states = (kvc.reshape(kvshape), kvs.reshape(kvshape), new_win, new_pool)
    return (x2, h2, eidx, gates), states, cnt


def kernel(x_prompt, x_sample, cache_cmp_kv, cache_slc_kv, cache_win_kv, cache_mem_kv, state_pool, page_table,
           mem_prompt, rel_bias, norm1_g, w_in, nsa_qk_norm, mem_qk_norm, cmp_w, cmp_pe, pool_w, pool_scale,
           mem_norm_g, w_mem_kv, w_up_pool, w_up_nsa, w_up_mem, w_out, norm2_g, router_w, router_b, w_gu, b_gu,
           w_down, b_down):
    depth = w_in.shape[0]
    yp, ys = x_prompt, x_sample
    bp, sp, _ = x_prompt.shape
    bs, ts, _ = x_sample.shape
    outs_p = [[] for _ in range(5)]
    outs_s = [[] for _ in range(4)]
    for l in range(depth):
        w = _prep_layer(l, rel_bias, norm1_g, w_in, nsa_qk_norm, mem_qk_norm, cmp_w, cmp_pe, pool_w, pool_scale,
                        mem_norm_g, w_mem_kv, w_up_pool, w_up_nsa, w_up_mem, w_out, norm2_g, router_w, router_b,
                        w_gu, b_gu, w_down, b_down)
        pre_p, st_p, cnt = _prompt_pre(yp, mem_prompt, w, rel_bias, jnp.zeros((1, LANES), F32))
        pre_s, st_s, cnt = _sample_pre(ys, cache_cmp_kv[l], cache_slc_kv[l], cache_win_kv[l], cache_mem_kv[l],
                                       state_pool[l], page_table, w, rel_bias, cnt)
        yp, ys = _moe([pre_p, pre_s], cnt, w)
        yp = yp.reshape(bp, sp, D_MODEL)
        ys = ys.reshape(bs, ts, D_MODEL)
        for lst, a in zip(outs_p, st_p):
            lst.append(a)
        for lst, a in zip(outs_s, st_s):
            lst.append(a)
    new_cmp_p, new_slc_p, new_win_p, new_mem_p, new_pool_p = [jnp.stack(a) for a in outs_p]
    new_cmp_s, new_slc_s, new_win_s, new_pool_s = [jnp.stack(a) for a in outs_s]
    new_win_s = new_win_s.reshape(new_win_s.shape[:3] + (2, NSA_KV_HEADS, HEAD_DIM))
    return (yp, ys, new_cmp_p, new_slc_p, new_win_p, new_mem_p, new_pool_p,
            new_cmp_s, new_slc_s, new_win_s, new_pool_s)
```

```python
import functools
import math

import jax
import jax.numpy as jnp
from jax import lax
from jax.experimental import pallas as pl
from jax.experimental.pallas import tpu as pltpu

F32 = jnp.float32
BF16 = jnp.bfloat16

D_MODEL = 1024
HEAD_DIM = 64
POOL_WINDOWS = (2, 4, 8, 16)
POOL_GROUP = 64
POOL_WIDTH = 256
POOL_BUF = 15
NSA_HEADS = 8
NSA_KV_HEADS = 2
NSA_HPG = 4
NSA_WIDTH = 512
KV_WIDTH = 256
CMP_BLOCK = 32
CMP_STRIDE = 16
SLC_BLOCK = 64
SLC_TOPK = 16
WINDOW = 512
MEM_HEADS = 4
MEM_WIDTH = 256
NUM_BUCKETS = 32
MAX_DISTANCE = 128
N_EXPERTS = 32
TOP_K = 4
D_FF = 1024
SWIGLU_ALPHA = 1.702
SWIGLU_LIMIT = 7.0
EPS = 1e-6
SCALE = HEAD_DIM ** -0.5

LANES = 128
MXU_DIM = 256
NEG = -1e30
ATT_TILE = 256
DEC_PAGES_PER_STEP = 32
MOE_TILE = 512
MOE_DMA_TOKENS = 512
VMEM_LIMIT = 48 * 1024 * 1024
FFN_VMEM_LIMIT = 56 * 1024 * 1024


def _cparams(*sem):
    return pltpu.CompilerParams(dimension_semantics=sem, vmem_limit_bytes=VMEM_LIMIT)


def _dot(a, b):
    return jnp.dot(a, b, preferred_element_type=F32)


def _dot_nt(a, b):
    return lax.dot_general(a, b, (((1,), (1,)), ((), ())), preferred_element_type=F32)


def _split_dot(a, b):
    hi = a.astype(BF16)
    lo = (a - hi.astype(F32)).astype(BF16)
    return _dot(hi, b) + _dot(lo, b)


def _rms(x, g):
    r = lax.rsqrt(jnp.mean(x * x, axis=-1, keepdims=True) + EPS)
    return (x * r) * g


def _sigmoid(x):
    return 1.0 / (1.0 + jnp.exp(-x))


def _pack_bf16_pairs(x):
    w = x.shape[1] // 2
    bits = lax.bitcast_convert_type(x.astype(BF16).astype(F32), jnp.uint32)
    return jnp.right_shift(bits[:, :w], jnp.uint32(16)) | (bits[:, w:] & jnp.uint32(0xFFFF0000))


def _unpack_bf16_pairs(p):
    lo = lax.bitcast_convert_type(jnp.left_shift(p, jnp.uint32(16)), F32)
    hi = lax.bitcast_convert_type(p & jnp.uint32(0xFFFF0000), F32)
    return jnp.concatenate([lo, hi], axis=1)


def _lane_iota(shape):
    return lax.broadcasted_iota(jnp.int32, shape, len(shape) - 1)


def _row_iota(shape):
    return lax.broadcasted_iota(jnp.int32, shape, len(shape) - 2)


def _proj_kernel(x_ref, g_ref, w_ref, gain_ref, nmask_ref, seg_ref, *rest, segs, n_norm, has_wt):
    wt_ref = rest[0] if has_wt else None
    out_refs = rest[1:] if has_wt else rest
    h = _rms(x_ref[...], g_ref[...]).astype(BF16)
    seg = seg_ref[...]
    outs = iter(out_refs)
    for (start, width, kind, forms, _) in segs:
        if forms == ("t",) and has_wt:
            next(outs)[...] = _dot_nt(wt_ref[...], h)
            continue
        z = _dot(h, w_ref[:, start:start + width])
        if start < n_norm:
            pieces = []
            for c in range(0, width, MXU_DIM):
                zc = z[:, c:c + MXU_DIM]
                ms = _split_dot(zc * zc, seg)
                zn = (zc * lax.rsqrt(ms + EPS)) * gain_ref[:, start + c:start + c + MXU_DIM]
                pieces.append(jnp.where(nmask_ref[:, start + c:start + c + MXU_DIM] > 0, zn, zc))
            z = pieces[0] if len(pieces) == 1 else jnp.concatenate(pieces, axis=1)
        if kind == "sigmoid":
            z = _sigmoid(z)
        elif kind == "qscale":
            z = z * SCALE
        for form in forms:
            o_ref = next(outs)
            if form == "rows":
                o_ref[...] = z.astype(o_ref.dtype)
            elif form == "key_rows":
                o_ref[...] = z[:, 0:LANES]
            else:
                o_ref[...] = z.T


def _project(x, g, w, gain, nmask, segs, n_norm, tm, seq, wt=None):
    n = x.shape[0]
    ncol = w.shape[1]
    seg = _seg_matrix(MXU_DIM)
    full = lambda i: (0, 0)
    tpb = seq // tm
    out_specs, out_shape = [], []
    for (_, wd, _, forms, dt) in segs:
        for form in forms:
            if form == "rows":
                out_specs.append(pl.BlockSpec((tm, wd), lambda i: (i, 0)))
                out_shape.append(jax.ShapeDtypeStruct((n, wd), dt))
            elif form == "key_rows":
                out_specs.append(pl.BlockSpec((tm, LANES), lambda i: (i, 0)))
                out_shape.append(jax.ShapeDtypeStruct((n, LANES), F32))
            else:
                out_specs.append(pl.BlockSpec((None, wd, tm), lambda i: (i // tpb, 0, i % tpb)))
                out_shape.append(jax.ShapeDtypeStruct((n // seq, wd, seq), F32))
    extra = () if wt is None else (wt,)
    return pl.pallas_call(
        functools.partial(_proj_kernel, segs=segs, n_norm=n_norm, has_wt=wt is not None),
        grid=(n // tm,),
        in_specs=[
            pl.BlockSpec((tm, D_MODEL), lambda i: (i, 0)),
            pl.BlockSpec((1, D_MODEL), full),
            pl.BlockSpec((D_MODEL, ncol), full),
            pl.BlockSpec((1, gain.shape[1]), full),
            pl.BlockSpec((1, nmask.shape[1]), full),
            pl.BlockSpec((MXU_DIM, MXU_DIM), full),
        ] + [pl.BlockSpec(a.shape, full) for a in extra],
        out_specs=out_specs,
        out_shape=out_shape,
        compiler_params=_cparams("parallel"),
        name="proj",
    )(x, g, w, gain, nmask, seg, *extra)


def _seg_matrix(n):
    i = jnp.arange(n) // HEAD_DIM
    return jnp.where(i[:, None] == i[None, :], 1.0 / HEAD_DIM, 0.0).astype(BF16)


def _pool_kernel(u_ref, buf_ref, w_ref, scale_ref, o_ref, zs_ref, *, t, pos0):
    zs_ref[0:16, :] = buf_ref[...]
    zs_ref[16:16 + t, :] = u_ref[...]
    u = u_ref[...]
    lane = _lane_iota((1, POOL_WIDTH))
    pos = (pos0 + _row_iota((t, 1))).astype(F32)
    acc = u
    mean = None
    for i in range(1, max(POOL_WINDOWS)):
        acc = acc + zs_ref[16 - i:16 - i + t, :]
        if i + 1 in POOL_WINDOWS:
            gi = POOL_WINDOWS.index(i + 1)
            m = acc / jnp.minimum(pos + 1.0, float(i + 1))
            mean = m if mean is None else jnp.where(lane >= gi * POOL_GROUP, m, mean)
    d = (mean - u).astype(BF16)
    o_ref[...] = _dot(d, w_ref[...]) * scale_ref[...]


def _pool_mix(u, buf16, w_bd, scale, pos0):
    b, t, _ = u.shape
    return pl.pallas_call(
        functools.partial(_pool_kernel, t=t, pos0=pos0),
        grid=(b,),
        in_specs=[
            pl.BlockSpec((None, t, POOL_WIDTH), lambda i: (i, 0, 0)),
            pl.BlockSpec((None, 16, POOL_WIDTH), lambda i: (i, 0, 0)),
            pl.BlockSpec((POOL_WIDTH, POOL_WIDTH), lambda i: (0, 0)),
            pl.BlockSpec((1, POOL_WIDTH), lambda i: (0, 0)),
        ],
        out_specs=pl.BlockSpec((None, t, POOL_WIDTH), lambda i: (i, 0, 0)),
        out_shape=jax.ShapeDtypeStruct((b, t, POOL_WIDTH), F32),
        scratch_shapes=[pltpu.VMEM((t + 16, POOL_WIDTH), F32)],
        compiler_params=_cparams("parallel"),
        name="pool",
    )(u, buf16, w_bd, scale)


def _cpart_compute(rows_of, pe_ref, w_ref, o_ref, m):
    for c in range(2):
        acc = jnp.zeros((m + 8, KV_WIDTH), F32)
        for j in range(CMP_STRIDE):
            lhs = jnp.concatenate([rows_of(c, j), pe_ref[c, j]], axis=0)
            acc = acc + _dot(lhs.astype(BF16), w_ref[c, j])
        lane = _lane_iota((1, KV_WIDTH))
        pe_term = jnp.where(lane < LANES, acc[m:m + 1], acc[m + 1:m + 2])
        o_ref[:, c * KV_WIDTH:(c + 1) * KV_WIDTH] = acc[0:m] + pe_term


def _cpart_kernel(xk_ref, xv_ref, pe_ref, w_ref, o_ref, *, rows):
    m = rows // CMP_STRIDE
    x_refs = (xk_ref, xv_ref)
    _cpart_compute(lambda c, j: x_refs[c][pl.ds(j, m, stride=CMP_STRIDE), :], pe_ref, w_ref, o_ref, m)


def _cpart_paged_kernel(*refs, nop, page):
    page_refs = refs[2:2 + nop]
    perm_ref, pe_ref, w_ref, o_ref, xs_ref = refs[2 + nop:]
    n = page // CMP_STRIDE
    for k, r in enumerate(page_refs):
        y = _dot_nt(perm_ref[...], r[...].astype(BF16))
        for j in range(CMP_STRIDE):
            for c in range(2):
                xs_ref[c, j, k * n:(k + 1) * n, :] = y[j * n:(j + 1) * n, c * LANES:(c + 1) * LANES]
    _cpart_compute(lambda c, j: xs_ref[c, j], pe_ref, w_ref, o_ref, nop * n)


def _cmp_partials_dense(kv, pe, w_c):
    b, t, _ = kv.shape
    rows = (t // CMP_STRIDE) * CMP_STRIDE
    n = rows // CMP_STRIDE
    return pl.pallas_call(
        functools.partial(_cpart_kernel, rows=rows),
        grid=(b,),
        in_specs=[
            pl.BlockSpec((None, rows, LANES), lambda i: (i, 0, 0)),
            pl.BlockSpec((None, rows, LANES), lambda i: (i, 0, 1)),
            pl.BlockSpec(pe.shape, lambda i: (0, 0, 0, 0)),
            pl.BlockSpec(w_c.shape, lambda i: (0, 0, 0, 0)),
        ],
        out_specs=pl.BlockSpec((None, n, 2 * KV_WIDTH), lambda i: (i, 0, 0)),
        out_shape=jax.ShapeDtypeStruct((b, n, 2 * KV_WIDTH), F32),
        compiler_params=_cparams("parallel"),
        name="cmp_partials",
    )(kv, kv, pe, w_c)


def _cmp_partials_paged(pool_t, phys, lblk, pe, w_c, pages_per_step):
    b, n_pages = phys.shape
    page = LANES
    nop = pages_per_step
    n = nop * page // CMP_STRIDE

    def page_spec(k):
        return pl.BlockSpec((None, KV_WIDTH, page),
                            lambda i, c, ph, lb: (ph[i, c * nop + k], 0, lb[i, c * nop + k]))

    r_idx = jnp.arange(page)
    regroup = (r_idx[None, :] == ((r_idx % (page // CMP_STRIDE)) * CMP_STRIDE + r_idx // (page // CMP_STRIDE))[:, None])
    regroup = regroup.astype(BF16)
    grid_spec = pltpu.PrefetchScalarGridSpec(
        num_scalar_prefetch=2,
        grid=(b, n_pages // nop),
        in_specs=[page_spec(k) for k in range(nop)] + [
            pl.BlockSpec(regroup.shape, lambda i, c, ph, lb: (0, 0)),
            pl.BlockSpec(pe.shape, lambda i, c, ph, lb: (0, 0, 0, 0)),
            pl.BlockSpec(w_c.shape, lambda i, c, ph, lb: (0, 0, 0, 0)),
        ],
        out_specs=pl.BlockSpec((None, n, 2 * KV_WIDTH), lambda i, c, ph, lb: (i, c, 0)),
        scratch_shapes=[pltpu.VMEM((2, CMP_STRIDE, n, LANES), F32)],
    )
    return pl.pallas_call(
        functools.partial(_cpart_paged_kernel, nop=nop, page=page),
        grid_spec=grid_spec,
        out_shape=jax.ShapeDtypeStruct((b, n_pages * page // CMP_STRIDE, 2 * KV_WIDTH), F32),
        compiler_params=_cparams("parallel", "arbitrary"),
        name="cmp_partials_paged",
    )(phys, lblk, *([pool_t] * nop), regroup, pe, w_c)


def _group_query_columns(q_ref, g, t):
    zeros64 = jnp.zeros((HEAD_DIM, t), BF16)
    cols = []
    for pr in range(2):
        qt = q_ref[:, (2 * g + pr) * LANES:(2 * g + pr + 1) * LANES].astype(F32).T.astype(BF16)
        for half in range(2):
            qh = qt[half * HEAD_DIM:(half + 1) * HEAD_DIM]
            cols.append(jnp.concatenate([qh, zeros64] if g == 0 else [zeros64, qh], axis=0))
    return jnp.concatenate(cols, axis=1)


def _store_group_output(o_ref, out_t, g, t):
    for pr in range(2):
        pair = jnp.concatenate([out_t[:, (2 * pr) * t:(2 * pr + 1) * t], out_t[:, (2 * pr + 1) * t:(2 * pr + 2) * t]],
                               axis=0)
        o_ref[:, (2 * g + pr) * LANES:(2 * g + pr + 1) * LANES] = pair.T


def _cattn_kernel(q_ref, p_ref, bias_ref, gn_ref, gk_ref, seg_ref, o_ref, ns0_ref, ns1_ref, *,
                  tq, n_sub, n_cmp, n_slc, nslp, pos0):
    qi = pl.program_id(0)
    pall = p_ref[...]
    kraw = pall[:, 0:LANES] + pltpu.roll(pall[:, LANES:2 * LANES], n_sub - 1, 0)
    vc = pall[:, 2 * LANES:3 * LANES] + pltpu.roll(pall[:, 3 * LANES:4 * LANES], n_sub - 1, 0)
    ms = _split_dot(kraw * kraw, seg_ref[...])
    kc = ((kraw * lax.rsqrt(ms + EPS)) * gk_ref[...]).astype(BF16)
    vct = vc.T.astype(BF16)

    nsel = -(-n_slc // 8) * 8
    jj = _row_iota((nsel, n_sub))
    nn = _lane_iota((nsel, n_sub))
    covers_t = ((nn * CMP_STRIDE < (jj + 1) * SLC_BLOCK) & (nn * CMP_STRIDE + CMP_BLOCK - 1 >= jj * SLC_BLOCK)
                & (nn < n_cmp) & (jj < n_slc))
    covers_t = jnp.where(covers_t, 1.0, 0.0).astype(BF16)
    qpos = pos0 + qi * tq + _lane_iota((1, tq))
    qblk = jnp.right_shift(qpos, SLC_BLOCK.bit_length() - 1)
    jr = _row_iota((nsel, tq))
    jrf = jr.astype(F32)
    forced = (jr == 0) | (jr == qblk) | (jr == qblk - 1)
    causal = jr <= qblk
    gnt = gn_ref[...].T

    for g in range(NSA_KV_HEADS):
        bias = bias_ref[g]
        s = _dot(kc, _group_query_columns(q_ref, g, tq)) + bias
        m = jnp.max(s, axis=0, keepdims=True)
        m = jnp.where(m > 0.5 * NEG, m, 0.0)
        e = jnp.where(bias > 0.5 * NEG, jnp.exp(s - m), 0.0)
        p = e / jnp.maximum(jnp.sum(e, axis=0, keepdims=True), 1e-30)
        h0 = NSA_HPG * g
        gate = jnp.concatenate([gnt[h0 + c:h0 + c + 1] for c in range(NSA_HPG)], axis=1)
        out_t = _dot(vct[g * HEAD_DIM:(g + 1) * HEAD_DIM], p.astype(BF16)) * gate
        _store_group_output(o_ref, out_t, g, tq)

        prsum = p[:, 0:tq] + p[:, tq:2 * tq] + p[:, 2 * tq:3 * tq] + p[:, 3 * tq:4 * tq]
        hi = prsum.astype(BF16)
        lo = (prsum - hi.astype(F32)).astype(BF16)
        imp = _dot(covers_t, hi) + _dot(covers_t, lo)
        score = jnp.where(forced, jnp.inf, imp)
        score = jnp.where(causal, score, -jnp.inf)
        sel = jnp.zeros((nsel, tq), F32)
        for _ in range(min(SLC_TOPK, n_slc)):
            m = jnp.max(score, axis=0, keepdims=True)
            idx = jnp.min(jnp.where(score == m, jrf, 1e9), axis=0, keepdims=True)
            pick = jrf == idx
            sel = jnp.where(pick & (m > -jnp.inf), 1.0, sel)
            score = jnp.where(pick, -jnp.inf, score)
        ns = 1.0 - sel
        if nslp > nsel:
            ns = jnp.concatenate([ns, jnp.ones((nslp - nsel, tq), F32)], axis=0)
        (ns0_ref if g == 0 else ns1_ref)[...] = ns


def _cmp_bias_cols(bias, tq):
    _, s, n_sub = bias.shape
    bias_t = bias.reshape(NSA_KV_HEADS, NSA_HPG, s // tq, tq, n_sub).transpose(0, 4, 2, 1, 3)
    return bias_t.reshape(NSA_KV_HEADS, n_sub, NSA_HPG * s)


def _cmp_attention(q, parts, bias_t, gn, gk, *, tq, n_cmp, n_slc, pos0):
    b, s, _ = q.shape
    n_sub = parts.shape[1]
    nslp = -(-n_slc // LANES) * LANES
    kern = functools.partial(_cattn_kernel, tq=tq, n_sub=n_sub, n_cmp=n_cmp, n_slc=n_slc, nslp=nslp, pos0=pos0)
    return pl.pallas_call(
        kern,
        grid=(s // tq, b),
        in_specs=[
            pl.BlockSpec((None, tq, NSA_WIDTH), lambda i, j: (j, i, 0)),
            pl.BlockSpec((None, n_sub, 2 * KV_WIDTH), lambda i, j: (j, 0, 0)),
            pl.BlockSpec((NSA_KV_HEADS, n_sub, NSA_HPG * tq), lambda i, j: (0, 0, i)),
            pl.BlockSpec((None, tq, LANES), lambda i, j: (j, i, 0)),
            pl.BlockSpec((1, LANES), lambda i, j: (0, 0)),
            pl.BlockSpec((LANES, LANES), lambda i, j: (0, 0)),
        ],
        out_specs=[
            pl.BlockSpec((None, tq, NSA_WIDTH), lambda i, j: (j, i, 0)),
            pl.BlockSpec((None, nslp, tq), lambda i, j: (j, 0, i)),
            pl.BlockSpec((None, nslp, tq), lambda i, j: (j, 0, i)),
        ],
        out_shape=[
            jax.ShapeDtypeStruct((b, s, NSA_WIDTH), F32),
            jax.ShapeDtypeStruct((b, nslp, s), F32),
            jax.ShapeDtypeStruct((b, nslp, s), F32),
        ],
        compiler_params=_cparams("parallel", "parallel"),
        name="cmp_attention",
    )(q, parts, bias_t, gn, gk, _seg_matrix(LANES))


def _flash_kernel(q_ref, ns0_ref, ns1_ref, kv_ref, vt_ref, tab_ref, gn_ref, o_ref, *, t, use_sel, band, gate_base):
    qi = pl.program_id(1)
    row_k = _row_iota((t, LANES))
    lane_k = _lane_iota((t, LANES))
    lo_tile = jnp.maximum(qi - band, 0) if band is not None else 0
    gnt = gn_ref[...].T

    qts = []
    for g in range(NSA_KV_HEADS):
        qt_g = _group_query_columns(q_ref, g, t)
        if use_sel:
            nst = (ns0_ref if g == 0 else ns1_ref)[...].astype(BF16)
            qt_g = jnp.concatenate([qt_g, jnp.concatenate([nst] * NSA_HPG, axis=1)], axis=0)
        qts.append(qt_g)

    def body(kj, carry):
        k0 = pl.multiple_of(kj * t, t)
        kk = kv_ref[pl.ds(k0, t), :].astype(BF16)
        if use_sel:
            blk = kj * (t // SLC_BLOCK) + jnp.right_shift(row_k, SLC_BLOCK.bit_length() - 1)
            onehot = jnp.where(lane_k == blk, -(2.0 ** 30), 0.0).astype(BF16)
            kk = jnp.concatenate([kk, onehot], axis=1)
        delta = qi - kj
        if band is None:
            kind = jnp.minimum(delta, 2)
        else:
            kind = jnp.where(delta < 2, delta, jnp.where(delta < band, 2, 3))
        m_old, l_old, acc = carry
        s = _dot(kk, qt_all) + tab_ref[kind]
        m_new = jnp.maximum(m_old, jnp.max(s, axis=0, keepdims=True))
        alpha = jnp.exp(m_old - m_new)
        p = jnp.exp(s - m_new)
        l_new = alpha * l_old + jnp.sum(p, axis=0, keepdims=True)
        pb = p.astype(BF16)
        pv = jnp.concatenate(
            [_dot(vt_ref[g * HEAD_DIM:(g + 1) * HEAD_DIM, pl.ds(k0, t)].astype(BF16), pb[:, g * gw:(g + 1) * gw])
             for g in range(NSA_KV_HEADS)], axis=1)
        return m_new, l_new, acc * alpha + pv

    gw = NSA_HPG * t
    qt_all = jnp.concatenate(qts, axis=1)
    init = (jnp.full((1, NSA_HEADS * t), NEG, F32), jnp.zeros((1, NSA_HEADS * t), F32),
            jnp.zeros((HEAD_DIM, NSA_HEADS * t), F32))
    _, l_fin, acc = lax.fori_loop(lo_tile, qi + 1, body, init)
    gate = jnp.concatenate([gnt[gate_base + h:gate_base + h + 1] for h in range(NSA_HEADS)], axis=1)
    out = acc * (gate / l_fin)
    for g in range(NSA_KV_HEADS):
        _store_group_output(o_ref, out[:, g * gw:(g + 1) * gw], g, t)


def _flash_attention(q, ns0, ns1, k_rows, kv_t, tab, gn, *, use_sel, band, gate_base):
    b, s, _ = q.shape
    t = ATT_TILE
    assert ns0.shape[1] == LANES
    kern = functools.partial(_flash_kernel, t=t, use_sel=use_sel, band=band, gate_base=gate_base)
    tile = lambda w: pl.BlockSpec((None, t, w), lambda i, j: (i, j, 0))
    ns_tile = pl.BlockSpec((None, LANES, t), lambda i, j: (i, 0, j))
    return pl.pallas_call(
        kern,
        grid=(b, s // t),
        in_specs=[
            tile(NSA_WIDTH), ns_tile, ns_tile,
            pl.BlockSpec((None, s, LANES), lambda i, j: (i, 0, 0)),
            pl.BlockSpec((None, LANES, s), lambda i, j: (i, 1, 0)),
            pl.BlockSpec(tab.shape, lambda i, j: (0, 0, 0)),
            tile(LANES),
        ],
        out_specs=tile(NSA_WIDTH),
        out_shape=jax.ShapeDtypeStruct((b, s, NSA_WIDTH), F32),
        compiler_params=_cparams("parallel", "parallel"),
        name="flash_sel" if use_sel else "flash_win",
    )(q, ns0, ns1, k_rows, kv_t, tab, gn)


def _memattn_kernel(q_ref, kv_ref, o_ref):
    lane = _lane_iota((kv_ref.shape[0], LANES))
    for pr in range(MEM_HEADS // 2):
        qpair = q_ref[:, pr * LANES:(pr + 1) * LANES]
        kblk = kv_ref[:, pr * LANES:(pr + 1) * LANES]
        vblk = kv_ref[:, MEM_WIDTH + pr * LANES:MEM_WIDTH + (pr + 1) * LANES]
        out = None
        for half in range(2):
            keep = (lane < HEAD_DIM) if half == 0 else (lane >= HEAD_DIM)
            kk = jnp.where(keep, kblk, 0.0).astype(BF16)
            vv = jnp.where(keep, vblk, 0.0).astype(BF16)
            s = _dot_nt(qpair, kk)
            m = jnp.max(s, axis=1, keepdims=True)
            e = jnp.exp(s - m)
            p = e / jnp.sum(e, axis=1, keepdims=True)
            o = _dot(p.astype(BF16), vv)
            out = o if out is None else out + o
        o_ref[:, pr * LANES:(pr + 1) * LANES] = out


def _mem_attention(qm, mem_kv, tq):
    b, s, _ = qm.shape
    m = mem_kv.shape[1]
    return pl.pallas_call(
        _memattn_kernel,
        grid=(b, s // tq),
        in_specs=[
            pl.BlockSpec((None, tq, MEM_WIDTH), lambda i, j: (i, j, 0)),
            pl.BlockSpec((None, m, 2 * MEM_WIDTH), lambda i, j: (i, 0, 0)),
        ],
        out_specs=pl.BlockSpec((None, tq, MEM_WIDTH), lambda i, j: (i, j, 0)),
        out_shape=jax.ShapeDtypeStruct((b, s, MEM_WIDTH), F32),
        compiler_params=_cparams("parallel", "parallel"),
        name="mem_attention",
    )(qm, mem_kv)


def _dec_kernel(*refs, n_pg, pps, has_new):
    bidx_ref = refs[2]
    page_refs = refs[3:3 + pps]
    new_ref, wq_ref, bias_ref, ns_ref, gate_ref, o_ref, acc_ref, m_ref, l_ref = refs[3 + pps:]
    c = pl.program_id(1)
    n_chunks = pl.num_programs(1)

    @pl.when(c == 0)
    def _():
        acc_ref[...] = jnp.zeros_like(acc_ref)
        m_ref[...] = jnp.full_like(m_ref, NEG)
        l_ref[...] = jnp.zeros_like(l_ref)

    rk = _row_iota((LANES, LANES))

    def step(tiles, first_page):
        feats = (tiles[0] if len(tiles) == 1 else jnp.concatenate(tiles, axis=1)).astype(BF16)
        s = lax.dot_general(feats, wq_ref[...], (((0,), (1,)), ((), ())),
                            preferred_element_type=F32)
        extra = []
        for k in range(len(tiles)):
            pg = first_page + k
            ns = jnp.where(rk < SLC_BLOCK, ns_ref[pl.ds(2 * pg, 1), :], ns_ref[pl.ds(2 * pg + 1, 1), :])
            extra.append(jnp.where(ns > 0.5, NEG, bias_ref[bidx_ref[pg]]))
        s = s + (extra[0] if len(extra) == 1 else jnp.concatenate(extra, axis=0))
        m_old = m_ref[...]
        m_new = jnp.maximum(m_old, jnp.max(s, axis=0, keepdims=True))
        alpha = jnp.exp(m_old - m_new)
        p = jnp.exp(s - m_new)
        l_ref[...] = alpha * l_ref[...] + jnp.sum(p, axis=0, keepdims=True)
        m_ref[...] = m_new
        acc_ref[...] = acc_ref[...] * alpha + _dot(feats, p.astype(BF16))

    if has_new:
        @pl.when(c < n_chunks - 1)
        def _():
            step([r[...] for r in page_refs], c * pps)

        @pl.when(c == n_chunks - 1)
        def _():
            step([r[...] for r in page_refs] + [new_ref[...]], n_pg - pps)
    else:
        step([r[...] for r in page_refs], c * pps)

    @pl.when(c == n_chunks - 1)
    def _():
        o_ref[...] = acc_ref[...] / l_ref[...] * gate_ref[...]


def _decode_attention(pages, phys, lblk, new_rows, wq, bias_tab, bias_idx, notsel, gate):
    bsz, n_pg = phys.shape
    w = pages.shape[1]
    has_new = new_rows is not None
    pps = math.gcd(n_pg, DEC_PAGES_PER_STEP)
    n_steps = n_pg // pps
    n_chunks = n_steps
    if not has_new:
        new_rows = jnp.zeros((1, w, LANES), F32)
    new_map = (lambda i, c, ph, lb, bi: (i, 0, 0)) if has_new else (lambda i, c, ph, lb, bi: (0, 0, 0))

    def page_spec(k):
        def index(i, c, ph, lb, bi):
            return (ph[i, c * pps + k], 0, lb[i, c * pps + k])
        return pl.BlockSpec((None, w, LANES), index)

    per_b = lambda i, c, ph, lb, bi: (i, 0, 0)
    grid_spec = pltpu.PrefetchScalarGridSpec(
        num_scalar_prefetch=3,
        grid=(bsz, n_chunks),
        in_specs=[page_spec(k) for k in range(pps)] + [
            pl.BlockSpec((None, w, LANES), new_map),
            pl.BlockSpec((None, LANES, w), per_b),
            pl.BlockSpec(bias_tab.shape, lambda i, c, ph, lb, bi: (0, 0, 0)),
            pl.BlockSpec((None, notsel.shape[1], LANES), per_b),
            pl.BlockSpec((None, 1, LANES), per_b),
        ],
        out_specs=pl.BlockSpec((None, w, LANES), per_b),
        scratch_shapes=[pltpu.VMEM((w, LANES), F32), pltpu.VMEM((1, LANES), F32), pltpu.VMEM((1, LANES), F32)],
    )
    return pl.pallas_call(
        functools.partial(_dec_kernel, n_pg=n_pg, pps=pps, has_new=has_new),
        grid_spec=grid_spec,
        out_shape=jax.ShapeDtypeStruct((bsz, w, LANES), F32),
        compiler_params=_cparams("parallel", "arbitrary"),
        name="decode_attention",
    )(phys, lblk, bias_idx, *([pages] * pps), new_rows, wq, bias_tab, notsel, gate)


def _feature_major(cache):
    n, rows = cache.shape[:2]
    return cache.transpose(0, 2, 3, 4, 1).reshape(n, -1, rows)


def _tail_kernel(x_ref, g1_ref, op_ref, oc_ref, os_ref, ow_ref, om_ref, cnt0_ref, wgb_ref, wup_p_ref, wup_n_ref,
                 wup_m_ref, wout_ref, g2_ref, rw_ref, rb_ref, x2_ref, h2_ref, ei_ref, gt_ref, cnt_ref):
    @pl.when(pl.program_id(0) == 0)
    def _():
        cnt_ref[...] = cnt0_ref[...]

    x = x_ref[...]
    h = _rms(x, g1_ref[...]).astype(BF16)
    onsa = (oc_ref[...] + os_ref[...] + ow_ref[...]).astype(BF16)
    ups = (_dot(op_ref[...].astype(BF16), wup_p_ref[...]), _dot(onsa, wup_n_ref[...]),
           _dot(om_ref[...].astype(BF16), wup_m_ref[...]))
    mixed = None
    for br in range(3):
        gb = _sigmoid(_dot(h, wgb_ref[:, br * D_MODEL:(br + 1) * D_MODEL]))
        mixed = gb * ups[br] if mixed is None else mixed + gb * ups[br]
    x2 = x + _dot(mixed.astype(BF16), wout_ref[...])
    x2_ref[...] = x2
    h2 = _rms(x2, g2_ref[...])
    h2_ref[...] = _pack_bf16_pairs(h2)
    logits = _dot(h2.astype(BF16), rw_ref[...]) + rb_ref[...]
    lane = _lane_iota(logits.shape)
    lanef = lane.astype(F32)
    tops, idxs = [], []
    for _ in range(TOP_K):
        m = jnp.max(logits, axis=1, keepdims=True)
        idx = jnp.min(jnp.where(logits == m, lanef, 1e9), axis=1, keepdims=True)
        logits = jnp.where(lanef == idx, -jnp.inf, logits)
        tops.append(m)
        idxs.append(idx)
    es = [jnp.exp(tk - tops[0]) for tk in tops]
    den = es[0] + es[1] + es[2] + es[3]
    tm = logits.shape[0]
    onehot = jnp.zeros(logits.shape, F32)
    for k in range(TOP_K):
        onehot = jnp.where(lanef == idxs[k], 1.0, onehot)
    tri = jnp.where(_row_iota((tm, tm)) > _lane_iota((tm, tm)), 1.0, 0.0).astype(BF16)
    before = _dot(tri, onehot.astype(BF16)) + cnt_ref[...]
    cnt_ref[...] = cnt_ref[...] + jnp.sum(onehot, axis=0, keepdims=True)
    ei = jnp.zeros(logits.shape, F32)
    gt = jnp.zeros(logits.shape, F32)
    for k in range(TOP_K):
        rank = jnp.sum(jnp.where(lanef == idxs[k], before, 0.0), axis=1, keepdims=True)
        ei = jnp.where(lane == k, idxs[k], ei)
        ei = jnp.where(lane == TOP_K + k, rank, ei)
        gt = jnp.where(lane == k, es[k] / den, gt)
    ei_ref[...] = ei.astype(jnp.int32)
    gt_ref[...] = gt


def _layer_tail(x, o_pool, o_cmp, o_slc, o_win, o_mem, cnt0, w, tm):
    n = x.shape[0]
    row = lambda wd: pl.BlockSpec((tm, wd), lambda i: (i, 0))
    full = lambda a: pl.BlockSpec(a.shape, lambda i: (0,) * a.ndim)
    weights = (w["wgb"], w["wup_pool"], w["wup_nsa"], w["wup_mem"], w["wout"], w["g2"], w["rw"], w["rb"])
    return pl.pallas_call(
        _tail_kernel,
        grid=(n // tm,),
        in_specs=[row(D_MODEL), full(w["g1"]), row(POOL_WIDTH), row(NSA_WIDTH), row(NSA_WIDTH), row(NSA_WIDTH),
                  row(MEM_WIDTH), full(cnt0)] + [full(a) for a in weights],
        out_specs=[row(D_MODEL), row(D_MODEL // 2), row(LANES), row(LANES), full(cnt0)],
        out_shape=[jax.ShapeDtypeStruct((n, D_MODEL), F32), jax.ShapeDtypeStruct((n, D_MODEL // 2), jnp.uint32),
                   jax.ShapeDtypeStruct((n, LANES), jnp.int32), jax.ShapeDtypeStruct((n, LANES), F32),
                   jax.ShapeDtypeStruct((1, LANES), F32)],
        compiler_params=_cparams("arbitrary"),
        name="layer_tail",
    )(x, w["g1"], o_pool, o_cmp, o_slc, o_win, o_mem, cnt0, *weights)


def _ffn_kernel(be_ref, nu_ref, x_ref, wgu_ref, bgu_ref, wd_ref, bd_ref, o_ref, wgu_bf, wd_bf):
    i = pl.program_id(0)

    @pl.when((i == 0) | (be_ref[i] != be_ref[jnp.maximum(i - 1, 0)]))
    def _():
        wgu_bf[...] = wgu_ref[...].astype(BF16)
        wd_bf[...] = wd_ref[...].astype(BF16)

    @pl.when(i < nu_ref[0])
    def _():
        gu = _dot(_unpack_bf16_pairs(x_ref[...]).astype(BF16), wgu_bf[...]) + bgu_ref[...]
        gate = jnp.minimum(gu[:, :D_FF], SWIGLU_LIMIT)
        up = jnp.clip(gu[:, D_FF:], -SWIGLU_LIMIT, SWIGLU_LIMIT)
        act = gate * _sigmoid(SWIGLU_ALPHA * gate) * (up + 1.0)
        o_ref[...] = _dot(act.astype(BF16), wd_bf[...]) + bd_ref[...]

    @pl.when(i >= nu_ref[0])
    def _():
        o_ref[...] = jnp.zeros_like(o_ref)


def _expert_ffn(rows, blk_e, n_used, wgu, bgu, wd, bd):
    n_rows = rows.shape[0]
    n_blocks = n_rows // MOE_TILE
    blk = lambda i, be, nu: (jnp.minimum(i, nu[0] - 1), 0)
    grid_spec = pltpu.PrefetchScalarGridSpec(
        num_scalar_prefetch=2,
        grid=(n_blocks,),
        in_specs=[
            pl.BlockSpec((MOE_TILE, D_MODEL // 2), blk),
            pl.BlockSpec((None, D_MODEL, 2 * D_FF), lambda i, be, nu: (be[i], 0, 0)),
            pl.BlockSpec((None, 1, 2 * D_FF), lambda i, be, nu: (be[i], 0, 0)),
            pl.BlockSpec((None, D_FF, D_MODEL), lambda i, be, nu: (be[i], 0, 0)),
            pl.BlockSpec((None, 1, D_MODEL), lambda i, be, nu: (be[i], 0, 0)),
        ],
        out_specs=pl.BlockSpec((MOE_TILE, D_MODEL), lambda i, be, nu: (i, 0)),
        scratch_shapes=[pltpu.VMEM((D_MODEL, 2 * D_FF), BF16), pltpu.VMEM((D_FF, D_MODEL), BF16)],
    )
    return pl.pallas_call(
        _ffn_kernel,
        grid_spec=grid_spec,
        out_shape=jax.ShapeDtypeStruct((n_rows, D_MODEL), F32),
        compiler_params=pltpu.CompilerParams(dimension_semantics=("arbitrary",), vmem_limit_bytes=FFN_VMEM_LIMIT),
        name="expert_ffn",
    )(blk_e, n_used, rows, wgu, bgu, wd, bd)


def _token_tile(n):
    return next(t for t in (MOE_DMA_TOKENS, 384, 256, 128, n) if n % t == 0)


def _dispatch_scatter(dest_ref, h_ref, rows_ref, sem, td):
    def issue(i, carry):
        t0 = pl.multiple_of(i * 8, 8)
        for r in range(8):
            for k in range(TOP_K):
                d = dest_ref[i * (8 * TOP_K) + r * TOP_K + k]
                pltpu.make_async_copy(h_ref.at[pl.ds(t0 + r, 1)], rows_ref.at[pl.ds(d, 1)],
                                      sem).start(priority=k % 2)
        return carry

    lax.fori_loop(0, td // 8, issue, 0)
    for k in range(TOP_K):
        pltpu.make_async_copy(h_ref, rows_ref.at[pl.ds(0, td)], sem).wait()


def _dispatch_kernel(*refs, tiles, steps, n_blocks):
    ng = len(tiles)
    ends_ref, padded_ref, nu_ref = refs[:3]
    dest_refs = refs[3:3 + ng]
    h_refs = refs[3 + ng:3 + 2 * ng]
    rows_ref, zero_ref, sem, zsem = refs[3 + 2 * ng:]
    i = pl.program_id(0)

    @pl.when(i == 0)
    def _():
        zero_ref[...] = jnp.zeros_like(zero_ref)

        def tail_copy(e):
            start = pl.multiple_of(ends_ref[e] - MOE_TILE, MOE_TILE)
            return pltpu.make_async_copy(zero_ref, rows_ref.at[pl.ds(start, MOE_TILE)], zsem)

        def block_copy(blk):
            start = pl.multiple_of(blk * MOE_TILE, MOE_TILE)
            return pltpu.make_async_copy(zero_ref, rows_ref.at[pl.ds(start, MOE_TILE)], zsem)

        def each(start):
            def expert(e, carry):
                @pl.when(padded_ref[e] > 0)
                def _():
                    tail_copy(e).start() if start else tail_copy(e).wait()
                return carry

            def block(blk, carry):
                block_copy(blk).start() if start else block_copy(blk).wait()
                return carry

            lax.fori_loop(0, N_EXPERTS, expert, 0)
            lax.fori_loop(nu_ref[0], n_blocks, block, 0)

        each(True)
        each(False)

    first = 0
    for g in range(ng):
        @pl.when((i >= first) & (i < first + steps[g]))
        def _(g=g):
            _dispatch_scatter(dest_refs[g], h_refs[g], rows_ref, sem, tiles[g])
        first += steps[g]


def _moe_dispatch(h2s, dests, n_rows, pad_ends, padded, n_used):
    tiles = [_token_tile(h.shape[0]) for h in h2s]
    steps = [h.shape[0] // t for h, t in zip(h2s, tiles)]
    firsts = [sum(steps[:g]) for g in range(len(h2s))]

    def local(g):
        return lambda i, *_: jnp.clip(i - firsts[g], 0, steps[g] - 1)

    dest_specs = [pl.BlockSpec((tiles[g] * TOP_K,), lambda i, *_, f=local(g): (f(i),), memory_space=pltpu.SMEM)
                  for g in range(len(h2s))]
    width, dtype = h2s[0].shape[1], h2s[0].dtype
    tok_specs = [pl.BlockSpec((tiles[g], width), lambda i, *_, f=local(g): (f(i), 0)) for g in range(len(h2s))]
    grid_spec = pltpu.PrefetchScalarGridSpec(
        num_scalar_prefetch=3,
        grid=(sum(steps),),
        in_specs=dest_specs + tok_specs,
        out_specs=pl.BlockSpec(memory_space=pl.ANY),
        scratch_shapes=[pltpu.VMEM((MOE_TILE, width), dtype), pltpu.SemaphoreType.DMA(()),
                        pltpu.SemaphoreType.DMA(())],
    )
    return pl.pallas_call(
        functools.partial(_dispatch_kernel, tiles=tiles, steps=steps, n_blocks=n_rows // MOE_TILE),
        grid_spec=grid_spec,
        out_shape=jax.ShapeDtypeStruct((n_rows, width), dtype),
        compiler_params=_cparams("arbitrary"),
        name="moe_dispatch",
    )(pad_ends, padded, n_used, *dests, *h2s)


def _combine_kernel(dest_ref, dnext_ref, x2_ref, g_ref, rows_ref, o_ref, ybuf, sem, *, td):
    i = pl.program_id(0)
    last = pl.num_programs(0) - 1

    def start_tile(d_ref, slot):
        def issue(blk, carry):
            t0 = pl.multiple_of(blk * 8, 8)
            for r in range(8):
                for k in range(TOP_K):
                    d = d_ref[blk * (8 * TOP_K) + r * TOP_K + k]
                    pltpu.make_async_copy(rows_ref.at[pl.ds(d, 1)], ybuf.at[slot, k, pl.ds(t0 + r, 1)],
                                          sem.at[slot]).start(priority=k % 2)
            return carry

        lax.fori_loop(0, td // 8, issue, 0)

    @pl.when(i == 0)
    def _():
        start_tile(dest_ref, 0)

    @pl.when(i < last)
    def _():
        start_tile(dnext_ref, (i + 1) % 2)

    slot = i % 2
    for k in range(TOP_K):
        pltpu.make_async_copy(rows_ref.at[pl.ds(0, td)], ybuf.at[slot, k], sem.at[slot]).wait()
    out = x2_ref[...]
    for k in range(TOP_K):
        out = out + g_ref[:, k:k + 1] * ybuf[slot, k]
    o_ref[...] = out


def _moe_combine(x2, out_rows, dest, gates):
    n = x2.shape[0]
    td = _token_tile(n)
    steps = n // td
    return pl.pallas_call(
        functools.partial(_combine_kernel, td=td),
        grid=(steps,),
        in_specs=[pl.BlockSpec((td * TOP_K,), lambda i: (i,), memory_space=pltpu.SMEM),
                  pl.BlockSpec((td * TOP_K,), lambda i: (jnp.minimum(i + 1, steps - 1),), memory_space=pltpu.SMEM),
                  pl.BlockSpec((td, D_MODEL), lambda i: (i, 0)),
                  pl.BlockSpec((td, LANES), lambda i: (i, 0)),
                  pl.BlockSpec(memory_space=pl.ANY)],
        out_specs=pl.BlockSpec((td, D_MODEL), lambda i: (i, 0)),
        out_shape=jax.ShapeDtypeStruct((n, D_MODEL), F32),
        scratch_shapes=[pltpu.VMEM((2, TOP_K, td) + out_rows.shape[1:], out_rows.dtype),
                        pltpu.SemaphoreType.DMA((2,))],
        compiler_params=_cparams("arbitrary"),
        name="moe_combine",
    )(dest, dest, x2, gates, out_rows)


def _moe(groups, counts, w):
    n_total = sum(g[0].shape[0] for g in groups)
    cnt = counts[0, :N_EXPERTS].astype(jnp.int32)
    padded = (cnt + MOE_TILE - 1) // MOE_TILE * MOE_TILE
    pad_ends = jnp.cumsum(padded)
    pad_starts = pad_ends - padded
    n_blocks = -(-n_total * TOP_K // MOE_TILE) + N_EXPERTS
    blk_start = jnp.arange(n_blocks, dtype=jnp.int32) * MOE_TILE
    blk_e = jnp.minimum(jnp.sum(blk_start[:, None] >= pad_ends[None, :], axis=1), N_EXPERTS - 1).astype(jnp.int32)
    n_used = (pad_ends[-1] // MOE_TILE).astype(jnp.int32).reshape(1)
    dests = [(pad_starts[er[:, :TOP_K]] + er[:, TOP_K:2 * TOP_K]).astype(jnp.int32).reshape(-1)
             for (_, _, er, _) in groups]
    rows = _moe_dispatch([g[1] for g in groups], dests, n_blocks * MOE_TILE, pad_ends.astype(jnp.int32),
                         padded.astype(jnp.int32), n_used)
    out_rows = _expert_ffn(rows, blk_e, n_used, w["wgu"], w["bgu"], w["wd"], w["bd"])
    return [_moe_combine(x2, out_rows, dest, gates) for (x2, _, _, gates), dest in zip(groups, dests)]


def _rel_bucket(dist):
    n = jnp.maximum(dist, 0)
    max_exact = NUM_BUCKETS // 2
    nf = jnp.maximum(n, 1).astype(F32)
    large = max_exact + (jnp.log(nf / max_exact) / math.log(MAX_DISTANCE / max_exact)
                         * (NUM_BUCKETS - max_exact)).astype(jnp.int32)
    large = jnp.minimum(large, NUM_BUCKETS - 1)
    return jnp.where(n < max_exact, n, large)


def _bias_of(rel_bias, dist, valid):
    return jnp.where(valid[..., None], rel_bias[_rel_bucket(dist)], NEG)


def _proj_segs(kv_forms, kvc_forms):
    return ((0, 512, "qscale", ("rows",), BF16), (512, 256, "qscale", ("rows",), BF16),
            (768, 256, "id", kv_forms, F32), (1024, 256, "id", kv_forms, F32), (1280, 256, "id", kvc_forms, F32),
            (1536, 256, "id", ("rows",), F32), (1792, 128, "sigmoid", ("rows",), F32))


_PROJ_SEGS_PROMPT = _proj_segs(("key_rows", "t"), ("t",))
_PROJ_SEGS_SAMPLE = _proj_segs(("rows",), ("rows",))
_PROJ_NNORM = 1280


def _prep_layer(l, rel_bias, norm1_g, w_in, nsa_qk_norm, mem_qk_norm, cmp_w, cmp_pe, pool_w, pool_scale,
                mem_norm_g, w_mem_kv, w_up_pool, w_up_nsa, w_up_mem, w_out, norm2_g, router_w, router_b,
                w_gu, b_gu, w_down, b_down):
    wi = w_in[l]
    o_u, o_q, o_qm, o_kvc, o_kvs, o_kvw, o_gn, o_gb = 0, 256, 768, 1024, 1280, 1536, 1792, 1816
    w_proj = jnp.concatenate([
        wi[:, o_q:o_q + 512], wi[:, o_qm:o_qm + 256], wi[:, o_kvs:o_kvs + 256], wi[:, o_kvw:o_kvw + 256],
        wi[:, o_kvc:o_kvc + 256], wi[:, o_u:o_u + 256], wi[:, o_gn:o_gn + 24],
        jnp.zeros((D_MODEL, LANES - 24), F32)], axis=1).astype(BF16)
    nq, mq = nsa_qk_norm[l], mem_qk_norm[l]
    ones = jnp.ones((LANES,), F32)
    gain = jnp.concatenate([jnp.tile(nq[0], 8), jnp.tile(mq[0], 4), jnp.tile(nq[2], 2), ones,
                            jnp.tile(nq[3], 2), ones])[None, :]
    nmask = jnp.concatenate([jnp.ones((768,), F32), ones, 0 * ones, ones, 0 * ones])[None, :]
    eye4 = jnp.eye(4, dtype=F32)
    cw = cmp_w[l].reshape(2, 2, CMP_STRIDE, HEAD_DIM, HEAD_DIM)
    w_c = jnp.einsum("crjde,xy->cjxdrye", cw, jnp.eye(2, dtype=F32))
    w_c = w_c.reshape(2, CMP_STRIDE, LANES, KV_WIDTH).astype(BF16)
    pe = cmp_pe[l].reshape(2, 2, CMP_STRIDE, HEAD_DIM)
    pe_c = jnp.tile(pe.transpose(0, 2, 1, 3), (1, 1, 1, NSA_KV_HEADS))
    pe_c = jnp.pad(pe_c, ((0, 0), (0, 0), (0, 6), (0, 0)))
    w_pool = jnp.einsum("gde,gh->gdhe", pool_w[l], eye4).reshape(POOL_WIDTH, POOL_WIDTH).astype(BF16)
    rw = jnp.pad(router_w[l], ((0, 0), (0, LANES - N_EXPERTS))).astype(BF16)
    rb = jnp.concatenate([router_b[l], jnp.full((LANES - N_EXPERTS,), NEG, F32)])[None, :]
    return {
        "g1": norm1_g[l][None, :], "w_proj": w_proj, "gain": gain, "nmask": nmask,
        "w_kvc_t": wi[:, o_kvc:o_kvc + 256].T.astype(BF16),
        "gk_cmp": jnp.tile(nq[1], 2)[None, :], "w_c": w_c, "pe_c": pe_c,
        "w_pool": w_pool, "pool_scale": pool_scale[l][None, :],
        "mem_g": mem_norm_g[l][None, :], "w_mem": w_mem_kv[l].astype(BF16),
        "mem_gain": jnp.concatenate([jnp.tile(mq[1], 4), jnp.ones((256,), F32)])[None, :],
        "mem_nmask": jnp.concatenate([jnp.ones((256,), F32), jnp.zeros((256,), F32)])[None, :],
        "wgb": wi[:, o_gb:o_gb + 3 * D_MODEL].astype(BF16),
        "wup_pool": w_up_pool[l].astype(BF16), "wup_nsa": w_up_nsa[l].astype(BF16),
        "wup_mem": w_up_mem[l].astype(BF16), "wout": w_out[l].astype(BF16), "g2": norm2_g[l][None, :],
        "rw": rw, "rb": rb,
        "wgu": w_gu[l], "bgu": b_gu[l][:, None, :], "wd": w_down[l],
        "bd": b_down[l][:, None, :],
    }


def _project_in(x2d, w, segs, tm, seq):
    wt = w["w_kvc_t"] if segs is _PROJ_SEGS_PROMPT else None
    return _project(x2d, w["g1"], w["w_proj"], w["gain"], w["nmask"], segs, _PROJ_NNORM, tm, seq, wt)


def _rows_view(a_t, heads):
    b, _, rows = a_t.shape
    return a_t.reshape(b, 2, heads, HEAD_DIM, rows).transpose(0, 4, 1, 2, 3)


def _toeplitz(v, t):
    lead = v.shape[:-1]
    flat = jnp.tile(v, (1,) * len(lead) + (t,))[..., t:t + t * (2 * t - 1)]
    return flat.reshape(lead + (t, 2 * t - 1))[..., :t]


def _flash_tables(rel_bias):
    t = ATT_TILE
    d0 = jnp.arange(-t, t)
    kinds = jnp.stack([
        _bias_of(rel_bias, d0, d0 >= 0),
        _bias_of(rel_bias, d0 + t, d0 + t >= 0),
        _bias_of(rel_bias, jnp.full((2 * t,), 2 * t), jnp.ones((2 * t,), bool)),
        _bias_of(rel_bias, d0 + WINDOW, d0 + WINDOW < WINDOW),
    ])
    tab = _toeplitz(kinds.transpose(2, 0, 1), t)
    return tab.transpose(1, 2, 0, 3).reshape(4, t, NSA_HEADS * t)


def _cmp_bias_table(rel_bias, s, n_sub, n_cmp, pos0, tq):
    na = s // CMP_STRIDE
    m = max(na, n_sub)
    k = jnp.arange(-m, m)[None, :]
    r = jnp.arange(CMP_STRIDE)[:, None]
    d = CMP_STRIDE * k + r - (CMP_BLOCK - 1) + pos0
    v = _bias_of(rel_bias, d, d >= 0).transpose(2, 0, 1)
    tz = _toeplitz(v, m)[:, :, :n_sub, :na]
    tz = tz.reshape(NSA_KV_HEADS, NSA_HPG, CMP_STRIDE, n_sub, s // tq, tq // CMP_STRIDE)
    tab = tz.transpose(0, 3, 4, 1, 5, 2).reshape(NSA_KV_HEADS, n_sub, NSA_HPG * s)
    return jnp.where(jnp.arange(n_sub)[None, :, None] < n_cmp, tab, NEG)


def _prompt_pre(x, mem, w, rel_bias, cnt0):
    b, s, _ = x.shape
    n = b * s
    tm = 512 if n % 512 == 0 else ATT_TILE
    x2d = x.reshape(n, D_MODEL)
    q, qm, ks, kvs_t, kw, kvw_t, kvc_t, u, gn = _project_in(x2d, w, _PROJ_SEGS_PROMPT, tm, s)
    r3 = lambda a: a.reshape(b, s, a.shape[-1])
    q, qm, ks, kw, u, gn = map(r3, (q, qm, ks, kw, u, gn))

    o_pool = _pool_mix(u, jnp.zeros((b, 16, POOL_WIDTH), F32), w["w_pool"], w["pool_scale"], 0)

    n_cmp = (s - CMP_BLOCK) // CMP_STRIDE + 1
    n_slc = -(-s // SLC_BLOCK)
    n_lb = s // LANES
    own = jnp.broadcast_to(jnp.arange(b, dtype=jnp.int32)[:, None], (b, n_lb))
    blocks = jnp.broadcast_to(jnp.arange(n_lb, dtype=jnp.int32)[None, :], (b, n_lb))
    parts = _cmp_partials_paged(kvc_t, own, blocks, w["pe_c"], w["w_c"], math.gcd(n_lb, 16))
    n_sub = parts.shape[1]
    tq = math.gcd(s, 512)
    bias_c = _cmp_bias_table(rel_bias, s, n_sub, n_cmp, 0, tq)
    o_cmp, ns0, ns1 = _cmp_attention(q, parts, bias_c, gn, w["gk_cmp"], tq=tq, n_cmp=n_cmp, n_slc=n_slc, pos0=0)

    tab = _flash_tables(rel_bias)
    o_slc = _flash_attention(q, ns0, ns1, ks, kvs_t, tab, gn, use_sel=True, band=None, gate_base=NSA_HEADS)
    o_win = _flash_attention(q, ns0, ns1, kw, kvw_t, tab, gn, use_sel=False, band=WINDOW // ATT_TILE,
                             gate_base=2 * NSA_HEADS)

    m = mem.shape[1]
    mem_kv, mem_kv_t = _project(mem.reshape(b * m, D_MODEL), w["mem_g"], w["w_mem"], w["mem_gain"],
                                w["mem_nmask"], ((0, 2 * MEM_WIDTH, "id", ("rows", "t"), F32),), MXU_DIM,
                                tm=math.gcd(m, 512), seq=m)
    o_mem = _mem_attention(qm, mem_kv.reshape(b, m, 2 * MEM_WIDTH), tq=min(512, s))

    f2 = lambda a: a.reshape(n, a.shape[-1])
    x2, h2, eidx, gates, cnt = _layer_tail(x2d, f2(o_pool), f2(o_cmp), f2(o_slc), f2(o_win), f2(o_mem), cnt0, w, tm)
    win_t = kvw_t[:, :, max(0, s - WINDOW):]
    if s < WINDOW:
        win_t = jnp.pad(win_t, ((0, 0), (0, 0), (WINDOW - s, 0)))
    states = (_rows_view(kvc_t, NSA_KV_HEADS), _rows_view(kvs_t, NSA_KV_HEADS), _rows_view(win_t, NSA_KV_HEADS),
              _rows_view(mem_kv_t, MEM_HEADS), _last_rows(u, POOL_BUF))
    return (x2, h2, eidx, gates), states, cnt


def _last_rows(a, n):
    t = a.shape[1]
    if t < n:
        a = jnp.pad(a, [(0, 0), (n - t, 0)] + [(0, 0)] * (a.ndim - 2))
    return a[:, a.shape[1] - n:]


def _dec_columns_nsa(q):
    b, t, _ = q.shape
    qh = q.reshape(b, t, NSA_KV_HEADS, NSA_HPG, HEAD_DIM)
    w = jnp.einsum("btgpd,gx->bxdgtp", qh.astype(F32), jnp.eye(NSA_KV_HEADS, dtype=F32))
    w = w.reshape(b, NSA_KV_HEADS * HEAD_DIM, NSA_KV_HEADS * t * NSA_HPG)
    return jnp.pad(w, ((0, 0), (0, KV_WIDTH - w.shape[1]), (0, LANES - w.shape[2]))).astype(BF16)


def _dec_extract_nsa(o, t):
    b = o.shape[0]
    v = o[:, LANES:, :NSA_KV_HEADS * t * NSA_HPG]
    v = v.reshape(b, NSA_KV_HEADS, HEAD_DIM, NSA_KV_HEADS, t, NSA_HPG)
    v = jnp.einsum("bxdgtp,gx->btgpd", v, jnp.eye(NSA_KV_HEADS, dtype=F32))
    return v.reshape(b, t, NSA_WIDTH)


def _dec_bias_cols(bias_tph):
    k, t, _ = bias_tph.shape
    bt = bias_tph.reshape(k, t, NSA_KV_HEADS, NSA_HPG).transpose(0, 2, 1, 3).reshape(k, NSA_KV_HEADS * t * NSA_HPG)
    return jnp.pad(bt, ((0, 0), (0, LANES - bt.shape[1])))


def _dec_gate_cols(gn, base, t):
    b = gn.shape[0]
    gt = gn[:, :, base:base + NSA_HEADS].reshape(b, t, NSA_KV_HEADS, NSA_HPG).transpose(0, 2, 1, 3)
    gt = gt.reshape(b, 1, NSA_KV_HEADS * t * NSA_HPG)
    return jnp.pad(gt, ((0, 0), (0, 0), (0, LANES - gt.shape[2])), constant_values=1.0)


def _sample_pre(x, cache_cmp, cache_slc, cache_win, cache_mem, pool_buf, page_table, w, rel_bias, cnt0):
    b, t, _ = x.shape
    n = b * t
    page = cache_cmp.shape[1]
    n_pages = page_table.shape[1]
    past = n_pages * page
    x2d = x.reshape(n, D_MODEL)
    tm = n if n <= 512 else LANES
    q, qm, kvs, kvw, kvc, u, gn = _project_in(x2d, w, _PROJ_SEGS_SAMPLE, tm, tm)
    r3 = lambda a: a.reshape(b, t, a.shape[-1])
    q, qm, kvs, kvw, kvc, u, gn = map(r3, (q, qm, kvs, kvw, kvc, u, gn))
    qpos = past + jnp.arange(t)

    buf16 = jnp.pad(pool_buf, ((0, 0), (16 - POOL_BUF, 0), (0, 0)))
    o_pool = _pool_mix(u, buf16, w["w_pool"], w["pool_scale"], past)

    total = past + t
    n_cmp = (total - CMP_BLOCK) // CMP_STRIDE + 1
    n_sub_used = n_cmp + CMP_BLOCK // CMP_STRIDE - 1
    n_slc = -(-total // SLC_BLOCK)
    pps = math.gcd(n_pages, 16)
    parts = _cmp_partials_paged(_feature_major(cache_cmp), page_table, jnp.zeros_like(page_table), w["pe_c"],
                                w["w_c"], pps)
    extra = n_sub_used * CMP_STRIDE - past
    if extra > 0:
        tail_rows = -(-extra // CMP_STRIDE) * CMP_STRIDE
        new_c = jnp.pad(kvc, ((0, 0), (0, max(0, tail_rows - t)), (0, 0)))[:, :tail_rows]
        parts = jnp.concatenate([parts, _cmp_partials_dense(new_c, w["pe_c"], w["w_c"])], axis=1)
    n_sub = parts.shape[1]
    end = jnp.arange(n_sub)[None, :] * CMP_STRIDE + CMP_BLOCK - 1
    bias_c = _bias_of(rel_bias, qpos[:, None] - end, (end <= qpos[:, None]) & (jnp.arange(n_sub)[None, :] < n_cmp))
    qpad = ((0, 0), (0, LANES - t), (0, 0))
    bias_c = _cmp_bias_cols(jnp.pad(bias_c.transpose(2, 0, 1), qpad), LANES)
    o_cmp, ns0, ns1 = _cmp_attention(jnp.pad(q, qpad), parts, bias_c, jnp.pad(gn, qpad), w["gk_cmp"], tq=LANES,
                                     n_cmp=n_cmp, n_slc=n_slc, pos0=past)
    o_cmp = o_cmp[:, :t]

    wq = _dec_columns_nsa(q).transpose(0, 2, 1)
    ncol = NSA_KV_HEADS * t * NSA_HPG
    new_tile = lambda kv: jnp.pad(kv, ((0, 0), (0, LANES - t), (0, 0))).transpose(0, 2, 1)
    own = lambda npg: jnp.broadcast_to(jnp.arange(b, dtype=jnp.int32)[:, None], (b, npg))
    blocks = lambda npg: jnp.broadcast_to(jnp.arange(npg, dtype=jnp.int32)[None, :], (b, npg))

    n_chunks = n_pages + 1
    nblk = -(-2 * n_chunks // 8) * 8
    ns = jnp.stack([ns0, ns1], axis=1)[:, :, :, :t].transpose(0, 1, 3, 2)
    ns = jnp.pad(ns, ((0, 0), (0, 0), (0, 0), (0, max(0, nblk - ns.shape[3]))))[..., :nblk]
    ns = jnp.broadcast_to(ns[:, :, :, None, :], (b, NSA_KV_HEADS, t, NSA_HPG, nblk)).reshape(b, ncol, nblk)
    notsel = jnp.pad(ns.transpose(0, 2, 1), ((0, 0), (0, 0), (0, LANES - ncol)))
    rows = jnp.arange(LANES)
    far = _bias_of(rel_bias, jnp.full((LANES, t), 2 * MAX_DISTANCE), jnp.ones((LANES, t), bool))
    kpos_last = past - LANES + rows
    d_last = qpos[None, :] - kpos_last[:, None]
    near = _bias_of(rel_bias, d_last, d_last >= 0)
    kpos_new = past + rows
    d_new = qpos[None, :] - kpos_new[:, None]
    newb = _bias_of(rel_bias, d_new, (d_new >= 0) & (rows[:, None] < t))
    bias_tab = jnp.stack([_dec_bias_cols(far), _dec_bias_cols(near), _dec_bias_cols(newb)])
    bias_idx = jnp.concatenate([jnp.zeros((n_pages - 1,), jnp.int32), jnp.array([1, 2], jnp.int32)])
    o_slc = _decode_attention(_feature_major(cache_slc), page_table, jnp.zeros_like(page_table), new_tile(kvs), wq,
                              bias_tab, bias_idx, notsel, _dec_gate_cols(gn, NSA_HEADS, t))
    o_slc = _dec_extract_nsa(o_slc, t)

    wb = cache_win.shape[1]
    n_wpg = wb // LANES
    kpos_w = past - wb + jnp.arange(wb + LANES)
    d_w = qpos[None, :] - kpos_w[:, None]
    valid_w = (d_w >= 0) & (d_w < WINDOW) & (kpos_w[:, None] >= 0) & (jnp.arange(wb + LANES)[:, None] < wb + t)
    bias_w = _dec_bias_cols(_bias_of(rel_bias, d_w, valid_w)).reshape(n_wpg + 1, LANES, LANES)
    zeros_ns = jnp.zeros((b, -(-2 * (n_wpg + 1) // 8) * 8, LANES), F32)
    o_win = _decode_attention(_feature_major(cache_win), own(n_wpg), blocks(n_wpg), new_tile(kvw), wq, bias_w,
                              jnp.arange(n_wpg + 1, dtype=jnp.int32), zeros_ns, _dec_gate_cols(gn, 2 * NSA_HEADS, t))
    o_win = _dec_extract_nsa(o_win, t)

    m = cache_mem.shape[1]
    n_mpg = m // LANES
    qmh = qm.reshape(b, t, MEM_HEADS, HEAD_DIM).astype(F32)
    wqm = jnp.einsum("bthd,hx->bxdht", qmh, jnp.eye(MEM_HEADS, dtype=F32))
    wqm = wqm.reshape(b, MEM_WIDTH, MEM_HEADS * t)
    wqm = jnp.pad(wqm, ((0, 0), (0, MEM_WIDTH), (0, LANES - MEM_HEADS * t))).astype(BF16).transpose(0, 2, 1)
    o_mem = _decode_attention(_feature_major(cache_mem), own(n_mpg), blocks(n_mpg), None, wqm,
                              jnp.zeros((1, LANES, LANES), F32), jnp.zeros((n_mpg,), jnp.int32),
                              jnp.zeros((b, 8, LANES), F32), jnp.ones((b, 1, LANES), F32))
    om = o_mem[:, MEM_WIDTH:, :MEM_HEADS * t].reshape(b, MEM_HEADS, HEAD_DIM, MEM_HEADS, t)
    o_mem = jnp.einsum("bxdht,hx->bthd", om, jnp.eye(MEM_HEADS, dtype=F32)).reshape(b, t, MEM_WIDTH)

    f2 = lambda a: a.reshape(n, a.shape[-1])
    x2, h2, eidx, gates, cnt = _layer_tail(x2d, f2(o_pool), f2(o_cmp), f2(o_slc), f2(o_win), f2(o_mem), cnt0, w,
                                           n if n <= 512 else LANES)
    kvshape = (b, t, 2, NSA_KV_HEADS, HEAD_DIM)
    new_win = _rows_view(jnp.concatenate([_feature_major(cache_win)[:, :, t:], kvw.transpose(0, 2, 1)], axis=2),
                         NSA_KV_HEADS)
    new_pool = jnp.concatenate([pool_buf, u], axis=1)[:, t:]
    states = (kvc.reshape(kvshape), kvs.reshape(kvshape), new_win, new_pool)
    return (x2, h2, eidx, gates), states, cnt


def kernel(x_prompt, x_sample, cache_cmp_kv, cache_slc_kv, cache_win_kv, cache_mem_kv, state_pool, page_table,
           mem_prompt, rel_bias, norm1_g, w_in, nsa_qk_norm, mem_qk_norm, cmp_w, cmp_pe, pool_w, pool_scale,
           mem_norm_g, w_mem_kv, w_up_pool, w_up_nsa, w_up_mem, w_out, norm2_g, router_w, router_b, w_gu, b_gu,
           w_down, b_down):
    depth = w_in.shape[0]
    yp, ys = x_prompt, x_sample
    bp, sp, _ = x_prompt.shape
    bs, ts, _ = x_sample.shape
    outs_p = [[] for _ in range(5)]
    outs_s = [[] for _ in range(4)]
    for l in range(depth):
        w = _prep_layer(l, rel_bias, norm1_g, w_in, nsa_qk_norm, mem_qk_norm, cmp_w, cmp_pe, pool_w, pool_scale,
                        mem_norm_g, w_mem_kv, w_up_pool, w_up_nsa, w_up_mem, w_out, norm2_g, router_w, router_b,
                        w_gu, b_gu, w_down, b_down)
        pre_p, st_p, cnt = _prompt_pre(yp, mem_prompt, w, rel_bias, jnp.zeros((1, LANES), F32))
        pre_s, st_s, cnt = _sample_pre(ys, cache_cmp_kv[l], cache_slc_kv[l], cache_win_kv[l], cache_mem_kv[l],
                                       state_pool[l], page_table, w, rel_bias, cnt)
        yp, ys = _moe([pre_p, pre_s], cnt, w)
        yp = yp.reshape(bp, sp, D_MODEL)
        ys = ys.reshape(bs, ts, D_MODEL)
        for lst, a in zip(outs_p, st_p):
            lst.append(a)
        for lst, a in zip(outs_s, st_s):
            lst.append(a)
    new_cmp_p, new_slc_p, new_win_p, new_mem_p, new_pool_p = [jnp.stack(a) for a in outs_p]
    new_cmp_s, new_slc_s, new_win_s, new_pool_s = [jnp.stack(a) for a in outs_s]
    new_win_s = new_win_s.reshape(new_win_s.shape[:3] + (2, NSA_KV_HEADS, HEAD_DIM))
    return (yp, ys, new_cmp_p, new_slc_p, new_win_p, new_mem_p, new_pool_p,
            new_cmp_s, new_slc_s, new_win_s, new_pool_s)
```
